```python
import math
import jax, jax.numpy as jnp
from jax import lax
import numpy as np

D_MODEL = 1024
BATCH = 8
SEQ = 8192
DEPTH = 2

GRID_W = 64
CTX_LEN = 256
N_MIXERS = 2
N_SSM_LAYERS = (DEPTH + 1) // 2
N_ATTN_LAYERS = DEPTH // 2
NORM_EPS = 1e-6

SSM_WIDTH = D_MODEL
SSM_GROUP = 16
SSM_GROUPS = SSM_WIDTH // SSM_GROUP
SSM_STATE = 64
DT_MIN = 1e-3
DT_MAX = 1e-1

HEAD_DIM = 64
N_Q_HEADS = D_MODEL // HEAD_DIM
N_KV_HEADS = 4
KV_REP = N_Q_HEADS // N_KV_HEADS
ATTN_WIDTH = N_Q_HEADS * HEAD_DIM
KV_WIDTH = N_KV_HEADS * HEAD_DIM
ATTN_IN = ATTN_WIDTH + 2 * KV_WIDTH + ATTN_WIDTH
Q_BLOCK = 128
ROPE_THETA = 10000.0
ROPE_AXIS_DIM = HEAD_DIM // 2

kernel_name = "hybrid_s5_gqa_prefix_dit"


def _rmsnorm(x, g):
    xf = x.astype(jnp.float32)
    y = xf * lax.rsqrt(jnp.mean(xf * xf, axis=-1, keepdims=True) + NORM_EPS)
    return (y * g.astype(jnp.float32)).astype(x.dtype)


def _rope_tables(L):
    rows = L // GRID_W
    row = jnp.repeat(jnp.arange(rows), GRID_W).astype(jnp.float32)
    col = jnp.tile(jnp.arange(GRID_W), rows).astype(jnp.float32)
    n_freq = ROPE_AXIS_DIM // 2
    freqs = ROPE_THETA ** (-jnp.arange(n_freq, dtype=jnp.float32) / n_freq)
    ang_r = row[:, None] * freqs[None]
    ang_c = col[:, None] * freqs[None]
    return (jnp.cos(ang_r), jnp.sin(ang_r), jnp.cos(ang_c), jnp.sin(ang_c))


def _rope_half(x, cos, sin):
    h = x.shape[-1] // 2
    x1, x2 = x[..., :h], x[..., h:]
    cs, sn = cos[:, None, :], sin[:, None, :]
    return jnp.concatenate([x1 * cs - x2 * sn, x2 * cs + x1 * sn], axis=-1)


def _rope_2d(x, rope):
    cos_r, sin_r, cos_c, sin_c = rope
    xf = x.astype(jnp.float32)
    out = jnp.concatenate([_rope_half(xf[..., :ROPE_AXIS_DIM], cos_r, sin_r),
                           _rope_half(xf[..., ROPE_AXIS_DIM:], cos_c, sin_c)], axis=-1)
    return out.astype(x.dtype)


def _linear_scan(bu, abar, reverse):
    L = bu.shape[1]
    a = jnp.broadcast_to(abar, (1, L) + abar.shape)

    def combine(e_i, e_j):
        a_i, b_i = e_i
        a_j, b_j = e_j
        return a_j * a_i, a_j * b_i + b_j

    _, h = lax.associative_scan(combine, (a, bu), axis=1, reverse=reverse)
    return h


def _s5_core(u_lat, u_ctx, a_re, a_im, log_dt, b_re, b_im, c_re, c_im, d_skip, with_ctx):
    B, L, E = u_lat.shape
    C = u_ctx.shape[1]
    f32 = jnp.float32
    ul = u_lat.astype(f32).reshape(B, L, SSM_GROUPS, SSM_GROUP).astype(jnp.complex64)
    uc = u_ctx.astype(f32).reshape(B, C, SSM_GROUPS, SSM_GROUP).astype(jnp.complex64)
    dsk = d_skip.astype(f32)
    y_lat = u_lat.astype(f32) * dsk
    y_ctx = u_ctx.astype(f32) * dsk if with_ctx else None
    for d in range(2):
        rev = d == 1
        lam = lax.complex(a_re[d].astype(f32), a_im[d].astype(f32))
        lam_dt = lam * jnp.exp(log_dt[d].astype(f32))[:, None]
        abar = jnp.exp(lam_dt)
        bmat = lax.complex(b_re[d].astype(f32), b_im[d].astype(f32))
        bbar = ((abar - 1.0) / lam)[..., None] * bmat
        cmat = lax.complex(c_re[d].astype(f32), c_im[d].astype(f32))
        h_ctx = _linear_scan(jnp.einsum('bcgh,gph->bcgp', uc, bbar), abar, rev)
        h0 = h_ctx[:, 0] if rev else h_ctx[:, -1]
        steps = jnp.arange(L, 0, -1) if rev else jnp.arange(1, L + 1)
        carry = jnp.exp(lam_dt[None] * steps.astype(f32)[:, None, None])
        h_lat = _linear_scan(jnp.einsum('blgh,gph->blgp', ul, bbar), abar, rev) \
            + carry[None] * h0[:, None]
        y_lat = y_lat + jnp.real(jnp.einsum('blgp,ghp->blgh', h_lat, cmat)).reshape(B, L, E)
        if with_ctx:
            y_ctx = y_ctx + jnp.real(jnp.einsum('bcgp,ghp->bcgh', h_ctx, cmat)).reshape(B, C, E)
    return y_lat.astype(u_lat.dtype), (y_ctx.astype(u_ctx.dtype) if with_ctx else None)


def _s5_post(y, z, w_glu, b_glu, w_out):
    y = jax.nn.gelu(y, approximate=False)
    y = y * jax.nn.sigmoid(y @ w_glu + b_glu)
    return (y * jax.nn.silu(z)) @ w_out


def _ssm_layer(h, hc, w_in, a_re, a_im, log_dt, b_re, b_im, c_re, c_im, d_skip,
               w_glu, b_glu, w_out, with_ctx):
    proj = h @ w_in
    u, z = proj[..., :SSM_WIDTH], proj[..., SSM_WIDTH:]
    if with_ctx:
        proj_c = hc @ w_in
        u_c, z_c = proj_c[..., :SSM_WIDTH], proj_c[..., SSM_WIDTH:]
    else:
        u_c = hc @ w_in[:, :SSM_WIDTH]
    y, y_c = _s5_core(u, u_c, a_re, a_im, log_dt, b_re, b_im, c_re, c_im, d_skip, with_ctx)
    out = _s5_post(y, z, w_glu, b_glu, w_out)
    out_c = _s5_post(y_c, z_c, w_glu, b_glu, w_out) if with_ctx else None
    return out, out_c


def _sdpa(qb, k, v):
    s = jnp.einsum('bqgrd,bkgd->bgrqk', qb, k, preferred_element_type=jnp.float32)
    p = jax.nn.softmax(s * (1.0 / math.sqrt(HEAD_DIM)), axis=-1).astype(v.dtype)
    return jnp.einsum('bgrqk,bkgd->bqgrd', p, v)


def _attn_layer(h, hc, w_in, q_norm, k_norm, w_out, rope, with_ctx):
    B, L, _ = h.shape
    C = hc.shape[1]
    proj = h @ w_in
    q = proj[..., :ATTN_WIDTH].reshape(B, L, N_Q_HEADS, HEAD_DIM)
    k = proj[..., ATTN_WIDTH:ATTN_WIDTH + KV_WIDTH].reshape(B, L, N_KV_HEADS, HEAD_DIM)
    v = proj[..., ATTN_WIDTH + KV_WIDTH:ATTN_WIDTH + 2 * KV_WIDTH].reshape(B, L, N_KV_HEADS, HEAD_DIM)
    z = proj[..., ATTN_WIDTH + 2 * KV_WIDTH:]
    q = _rope_2d(_rmsnorm(q, q_norm), rope)
    k = _rope_2d(_rmsnorm(k, k_norm), rope)
    proj_c = hc @ w_in if with_ctx else hc @ w_in[:, ATTN_WIDTH:ATTN_WIDTH + 2 * KV_WIDTH]
    off = ATTN_WIDTH if with_ctx else 0
    k_c = _rmsnorm(proj_c[..., off:off + KV_WIDTH].reshape(B, C, N_KV_HEADS, HEAD_DIM), k_norm)
    v_c = proj_c[..., off + KV_WIDTH:off + 2 * KV_WIDTH].reshape(B, C, N_KV_HEADS, HEAD_DIM)
    k_all = jnp.concatenate([k, k_c], axis=1)
    v_all = jnp.concatenate([v, v_c], axis=1)
    nb = L // Q_BLOCK
    qb = q.reshape(B, nb, Q_BLOCK, N_KV_HEADS, KV_REP, HEAD_DIM).transpose(1, 0, 2, 3, 4, 5)
    o = lax.map(lambda blk: _sdpa(blk, k_all, v_all), qb)
    o = o.transpose(1, 0, 2, 3, 4, 5).reshape(B, L, ATTN_WIDTH)
    out = (o * jax.nn.silu(z)) @ w_out
    out_c = None
    if with_ctx:
        q_c = _rmsnorm(proj_c[..., :ATTN_WIDTH].reshape(B, C, N_Q_HEADS, HEAD_DIM), q_norm)
        o_c = _sdpa(q_c.reshape(B, C, N_KV_HEADS, KV_REP, HEAD_DIM), k_c, v_c).reshape(B, C, ATTN_WIDTH)
        out_c = (o_c * jax.nn.silu(proj_c[..., ATTN_WIDTH + 2 * KV_WIDTH:])) @ w_out
    return out, out_c


def _fwd_setup_inputs(seed: int = 0) -> dict:
    key = jax.random.key(seed)
    ks = jax.random.split(key, 24)
    f32 = jnp.float32
    nrm = lambda k, shape, s: jax.random.normal(k, shape, f32) * s
    NA, NB, G, P, H = N_SSM_LAYERS, N_ATTN_LAYERS, SSM_GROUPS, SSM_STATE, SSM_GROUP
    a_im0 = jnp.pi * jnp.arange(P, dtype=f32)
    return {
        "x": nrm(ks[0], (BATCH, SEQ, D_MODEL), 1.0),
        "c": nrm(ks[1], (BATCH, D_MODEL), 1.0),
        "ctx": nrm(ks[2], (BATCH, CTX_LEN, D_MODEL), 1.0),
        "c_ctx": nrm(ks[3], (D_MODEL,), 1.0),
        "w_mod": nrm(ks[4], (DEPTH, D_MODEL, 3 * D_MODEL), D_MODEL ** -0.5),
        "b_mod": nrm(ks[5], (DEPTH, 3 * D_MODEL), 0.02),
        "norm_g": 1.0 + nrm(ks[6], (DEPTH, D_MODEL), 0.05),
        "ssm_w_in": nrm(ks[7], (NA, D_MODEL, 2 * SSM_WIDTH), D_MODEL ** -0.5),
        "ssm_a_re": -0.5 + nrm(ks[8], (NA, 2, G, P), 0.01),
        "ssm_a_im": a_im0 + nrm(ks[9], (NA, 2, G, P), 0.01),
        "ssm_log_dt": jax.random.uniform(ks[10], (NA, 2, G), f32,
                                         minval=math.log(DT_MIN), maxval=math.log(DT_MAX)),
        "ssm_b_re": nrm(ks[11], (NA, 2, G, P, H), (2.0 * H) ** -0.5),
        "ssm_b_im": nrm(ks[12], (NA, 2, G, P, H), (2.0 * H) ** -0.5),
        "ssm_c_re": nrm(ks[13], (NA, 2, G, H, P), (2.0 * P) ** -0.5),
        "ssm_c_im": nrm(ks[14], (NA, 2, G, H, P), (2.0 * P) ** -0.5),
        "ssm_d": nrm(ks[15], (NA, SSM_WIDTH), 0.5),
        "ssm_w_glu": nrm(ks[16], (NA, SSM_WIDTH, SSM_WIDTH), SSM_WIDTH ** -0.5),
        "ssm_b_glu": nrm(ks[17], (NA, SSM_WIDTH), 0.02),
        "ssm_w_out": nrm(ks[18], (NA, SSM_WIDTH, D_MODEL), SSM_WIDTH ** -0.5),
        "attn_w_in": nrm(ks[19], (NB, D_MODEL, ATTN_IN), D_MODEL ** -0.5),
        "attn_q_norm": 1.0 + nrm(ks[20], (NB, HEAD_DIM), 0.05),
        "attn_k_norm": 1.0 + nrm(ks[21], (NB, HEAD_DIM), 0.05),
        "attn_w_out": nrm(ks[22], (NB, ATTN_WIDTH, D_MODEL), ATTN_WIDTH ** -0.5),
        "final_norm_g": 1.0 + nrm(ks[23], (D_MODEL,), 0.05),
    }


def _fwd_reference(x, c, ctx, c_ctx, w_mod, b_mod, norm_g, ssm_w_in, ssm_a_re, ssm_a_im,
              ssm_log_dt, ssm_b_re, ssm_b_im, ssm_c_re, ssm_c_im, ssm_d, ssm_w_glu,
              ssm_b_glu, ssm_w_out, attn_w_in, attn_q_norm, attn_k_norm, attn_w_out,
              final_norm_g):
    L = x.shape[1]
    rope = _rope_tables(L)
    s_c = jax.nn.silu(c)
    s_cc = jax.nn.silu(c_ctx)
    for i in range(DEPTH):
        kind, j = i % N_MIXERS, i // N_MIXERS
        with_ctx = i < DEPTH - 1
        mod = s_c @ w_mod[i] + b_mod[i]
        shift, scale, gate = jnp.split(mod, 3, axis=-1)
        mod_c = s_cc @ w_mod[i] + b_mod[i]
        shift_c, scale_c, gate_c = jnp.split(mod_c, 3, axis=-1)
        h = _rmsnorm(x, norm_g[i]) * (1.0 + scale[:, None]) + shift[:, None]
        hc = _rmsnorm(ctx, norm_g[i]) * (1.0 + scale_c) + shift_c
        if kind == 0:
            out, out_c = _ssm_layer(h, hc, ssm_w_in[j], ssm_a_re[j], ssm_a_im[j], ssm_log_dt[j],
                                    ssm_b_re[j], ssm_b_im[j], ssm_c_re[j], ssm_c_im[j], ssm_d[j],
                                    ssm_w_glu[j], ssm_b_glu[j], ssm_w_out[j], with_ctx)
        else:
            out, out_c = _attn_layer(h, hc, attn_w_in[j], attn_q_norm[j], attn_k_norm[j],
                                     attn_w_out[j], rope, with_ctx)
        x = x + gate[:, None] * out
        if with_ctx:
            ctx = ctx + gate_c * out_c
    return _rmsnorm(x, final_norm_g)


import jax as _jax
import jax.numpy as _jnp

TWIN_FORMAT = 'train_step'
FWD_PARAMS = ['x', 'c', 'ctx', 'c_ctx', 'w_mod', 'b_mod', 'norm_g', 'ssm_w_in', 'ssm_a_re', 'ssm_a_im', 'ssm_log_dt', 'ssm_b_re', 'ssm_b_im', 'ssm_c_re', 'ssm_c_im', 'ssm_d', 'ssm_w_glu', 'ssm_b_glu', 'ssm_w_out', 'attn_w_in', 'attn_q_norm', 'attn_k_norm', 'attn_w_out', 'final_norm_g']
TWIN_WEIGHTS = ['c_ctx', 'w_mod', 'b_mod', 'norm_g', 'ssm_w_in', 'ssm_a_re', 'ssm_a_im', 'ssm_log_dt', 'ssm_b_re', 'ssm_b_im', 'ssm_c_re', 'ssm_c_im', 'ssm_d', 'ssm_w_glu', 'ssm_b_glu', 'ssm_w_out', 'attn_w_in', 'attn_q_norm', 'attn_k_norm', 'attn_w_out', 'final_norm_g']
TWIN_DIFF_INPUT = 'x'
TWIN_INPUTS = ['x', 'c', 'ctx', 'c_ctx', 'w_mod', 'b_mod', 'norm_g', 'ssm_w_in', 'ssm_a_re', 'ssm_a_im', 'ssm_log_dt', 'ssm_b_re', 'ssm_b_im', 'ssm_c_re', 'ssm_c_im', 'ssm_d', 'ssm_w_glu', 'ssm_b_glu', 'ssm_w_out', 'attn_w_in', 'attn_q_norm', 'attn_k_norm', 'attn_w_out', 'final_norm_g', 'loss_target', 'm_c_ctx', 'm_w_mod', 'm_b_mod', 'm_norm_g', 'm_ssm_w_in', 'm_ssm_a_re', 'm_ssm_a_im', 'm_ssm_log_dt', 'm_ssm_b_re', 'm_ssm_b_im', 'm_ssm_c_re', 'm_ssm_c_im', 'm_ssm_d', 'm_ssm_w_glu', 'm_ssm_b_glu', 'm_ssm_w_out', 'm_attn_w_in', 'm_attn_q_norm', 'm_attn_k_norm', 'm_attn_w_out', 'm_final_norm_g', 'v_c_ctx', 'v_w_mod', 'v_b_mod', 'v_norm_g', 'v_ssm_w_in', 'v_ssm_a_re', 'v_ssm_a_im', 'v_ssm_log_dt', 'v_ssm_b_re', 'v_ssm_b_im', 'v_ssm_c_re', 'v_ssm_c_im', 'v_ssm_d', 'v_ssm_w_glu', 'v_ssm_b_glu', 'v_ssm_w_out', 'v_attn_w_in', 'v_attn_q_norm', 'v_attn_k_norm', 'v_attn_w_out', 'v_final_norm_g']
TWIN_OUTPUTS = ['loss', 'grad_x', 'grad_c_ctx', 'grad_w_mod', 'grad_b_mod', 'grad_norm_g', 'grad_ssm_w_in', 'grad_ssm_a_re', 'grad_ssm_a_im', 'grad_ssm_log_dt', 'grad_ssm_b_re', 'grad_ssm_b_im', 'grad_ssm_c_re', 'grad_ssm_c_im', 'grad_ssm_d', 'grad_ssm_w_glu', 'grad_ssm_b_glu', 'grad_ssm_w_out', 'grad_attn_w_in', 'grad_attn_q_norm', 'grad_attn_k_norm', 'grad_attn_w_out', 'grad_final_norm_g', 'delta_c_ctx', 'delta_w_mod', 'delta_b_mod', 'delta_norm_g', 'delta_ssm_w_in', 'delta_ssm_a_re', 'delta_ssm_a_im', 'delta_ssm_log_dt', 'delta_ssm_b_re', 'delta_ssm_b_im', 'delta_ssm_c_re', 'delta_ssm_c_im', 'delta_ssm_d', 'delta_ssm_w_glu', 'delta_ssm_b_glu', 'delta_ssm_w_out', 'delta_attn_w_in', 'delta_attn_q_norm', 'delta_attn_k_norm', 'delta_attn_w_out', 'delta_final_norm_g', 'new_m_c_ctx', 'new_m_w_mod', 'new_m_b_mod', 'new_m_norm_g', 'new_m_ssm_w_in', 'new_m_ssm_a_re', 'new_m_ssm_a_im', 'new_m_ssm_log_dt', 'new_m_ssm_b_re', 'new_m_ssm_b_im', 'new_m_ssm_c_re', 'new_m_ssm_c_im', 'new_m_ssm_d', 'new_m_ssm_w_glu', 'new_m_ssm_b_glu', 'new_m_ssm_w_out', 'new_m_attn_w_in', 'new_m_attn_q_norm', 'new_m_attn_k_norm', 'new_m_attn_w_out', 'new_m_final_norm_g', 'new_v_c_ctx', 'new_v_w_mod', 'new_v_b_mod', 'new_v_norm_g', 'new_v_ssm_w_in', 'new_v_ssm_a_re', 'new_v_ssm_a_im', 'new_v_ssm_log_dt', 'new_v_ssm_b_re', 'new_v_ssm_b_im', 'new_v_ssm_c_re', 'new_v_ssm_c_im', 'new_v_ssm_d', 'new_v_ssm_w_glu', 'new_v_ssm_b_glu', 'new_v_ssm_w_out', 'new_v_attn_w_in', 'new_v_attn_q_norm', 'new_v_attn_k_norm', 'new_v_attn_w_out', 'new_v_final_norm_g']
TWIN_LEAF_KINDS = {'loss': 'loss', 'grad_x': 'grad_x', 'grad_c_ctx': 'grad_w', 'grad_w_mod': 'grad_w', 'grad_b_mod': 'grad_w', 'grad_norm_g': 'grad_w', 'grad_ssm_w_in': 'grad_w', 'grad_ssm_a_re': 'grad_w', 'grad_ssm_a_im': 'grad_w', 'grad_ssm_log_dt': 'grad_w', 'grad_ssm_b_re': 'grad_w', 'grad_ssm_b_im': 'grad_w', 'grad_ssm_c_re': 'grad_w', 'grad_ssm_c_im': 'grad_w', 'grad_ssm_d': 'grad_w', 'grad_ssm_w_glu': 'grad_w', 'grad_ssm_b_glu': 'grad_w', 'grad_ssm_w_out': 'grad_w', 'grad_attn_w_in': 'grad_w', 'grad_attn_q_norm': 'grad_w', 'grad_attn_k_norm': 'grad_w', 'grad_attn_w_out': 'grad_w', 'grad_final_norm_g': 'grad_w', 'delta_c_ctx': 'delta_w', 'delta_w_mod': 'delta_w', 'delta_b_mod': 'delta_w', 'delta_norm_g': 'delta_w', 'delta_ssm_w_in': 'delta_w', 'delta_ssm_a_re': 'delta_w', 'delta_ssm_a_im': 'delta_w', 'delta_ssm_log_dt': 'delta_w', 'delta_ssm_b_re': 'delta_w', 'delta_ssm_b_im': 'delta_w', 'delta_ssm_c_re': 'delta_w', 'delta_ssm_c_im': 'delta_w', 'delta_ssm_d': 'delta_w', 'delta_ssm_w_glu': 'delta_w', 'delta_ssm_b_glu': 'delta_w', 'delta_ssm_w_out': 'delta_w', 'delta_attn_w_in': 'delta_w', 'delta_attn_q_norm': 'delta_w', 'delta_attn_k_norm': 'delta_w', 'delta_attn_w_out': 'delta_w', 'delta_final_norm_g': 'delta_w', 'new_m_c_ctx': 'new_m', 'new_m_w_mod': 'new_m', 'new_m_b_mod': 'new_m', 'new_m_norm_g': 'new_m', 'new_m_ssm_w_in': 'new_m', 'new_m_ssm_a_re': 'new_m', 'new_m_ssm_a_im': 'new_m', 'new_m_ssm_log_dt': 'new_m', 'new_m_ssm_b_re': 'new_m', 'new_m_ssm_b_im': 'new_m', 'new_m_ssm_c_re': 'new_m', 'new_m_ssm_c_im': 'new_m', 'new_m_ssm_d': 'new_m', 'new_m_ssm_w_glu': 'new_m', 'new_m_ssm_b_glu': 'new_m', 'new_m_ssm_w_out': 'new_m', 'new_m_attn_w_in': 'new_m', 'new_m_attn_q_norm': 'new_m', 'new_m_attn_k_norm': 'new_m', 'new_m_attn_w_out': 'new_m', 'new_m_final_norm_g': 'new_m', 'new_v_c_ctx': 'new_v', 'new_v_w_mod': 'new_v', 'new_v_b_mod': 'new_v', 'new_v_norm_g': 'new_v', 'new_v_ssm_w_in': 'new_v', 'new_v_ssm_a_re': 'new_v', 'new_v_ssm_a_im': 'new_v', 'new_v_ssm_log_dt': 'new_v', 'new_v_ssm_b_re': 'new_v', 'new_v_ssm_b_im': 'new_v', 'new_v_ssm_c_re': 'new_v', 'new_v_ssm_c_im': 'new_v', 'new_v_ssm_d': 'new_v', 'new_v_ssm_w_glu': 'new_v', 'new_v_ssm_b_glu': 'new_v', 'new_v_ssm_w_out': 'new_v', 'new_v_attn_w_in': 'new_v', 'new_v_attn_q_norm': 'new_v', 'new_v_attn_k_norm': 'new_v', 'new_v_attn_w_out': 'new_v', 'new_v_final_norm_g': 'new_v'}


def _forward(args):
    return _fwd_reference(*[args[k] for k in FWD_PARAMS])


def _output_shape():
    out = _jax.eval_shape(lambda: _forward(_fwd_setup_inputs(0)))
    return out.shape, out.dtype

N_MICROBATCH = 1
ADAM_LR = 0.001
ADAM_B1 = 0.9
ADAM_B2 = 0.999
ADAM_EPS = 1e-08
ADAM_WD = 0.01
ADAM_STEP = 10
PER_EXAMPLE_BATCH_AXIS = {'x': 0, 'c': 0, 'ctx': 0, 'loss_target': 0}
SHARED_INPUTS = []
_WEIGHT_DTYPES = {'c_ctx': _jnp.float32, 'w_mod': _jnp.float32, 'b_mod': _jnp.float32, 'norm_g': _jnp.float32, 'ssm_w_in': _jnp.float32, 'ssm_a_re': _jnp.float32, 'ssm_a_im': _jnp.float32, 'ssm_log_dt': _jnp.float32, 'ssm_b_re': _jnp.float32, 'ssm_b_im': _jnp.float32, 'ssm_c_re': _jnp.float32, 'ssm_c_im': _jnp.float32, 'ssm_d': _jnp.float32, 'ssm_w_glu': _jnp.float32, 'ssm_b_glu': _jnp.float32, 'ssm_w_out': _jnp.float32, 'attn_w_in': _jnp.float32, 'attn_q_norm': _jnp.float32, 'attn_k_norm': _jnp.float32, 'attn_w_out': _jnp.float32, 'final_norm_g': _jnp.float32}
MOMENT_SCALE = {'c_ctx': 3.927267e-02, 'w_mod': 7.845804e-02, 'b_mod': 1.250377e-01, 'norm_g': 5.104293e-02, 'ssm_w_in': 3.204502e-02, 'ssm_a_re': 4.187254e-03, 'ssm_a_im': 4.312499e-03, 'ssm_log_dt': 1.393976e+00, 'ssm_b_re': 2.702275e-03, 'ssm_b_im': 3.199012e-03, 'ssm_c_re': 5.544348e-03, 'ssm_c_im': 5.990552e-03, 'ssm_d': 6.216913e-02, 'ssm_w_glu': 6.503873e-03, 'ssm_b_glu': 1.479606e-02, 'ssm_w_out': 3.180017e-02, 'attn_w_in': 7.919679e-02, 'attn_q_norm': 6.384697e-02, 'attn_k_norm': 6.514118e-02, 'attn_w_out': 9.696996e-02, 'final_norm_g': 6.404913e+01}


def _to_microbatches(a, axis):
    t = _jnp.moveaxis(a, axis, 0)
    t = t.reshape((N_MICROBATCH, t.shape[0] // N_MICROBATCH) + t.shape[1:])
    return _jnp.moveaxis(t, 1, axis + 1)


def setup_inputs(seed: int = 0) -> dict:
    inp = _fwd_setup_inputs(seed)
    key = _jax.random.fold_in(_jax.random.key(seed), 7919)
    shape, _ = _output_shape()
    out = dict(inp)
    out["loss_target"] = _jax.random.normal(_jax.random.fold_in(key, 0), shape, _jnp.float32)
    for i, name in enumerate(TWIN_WEIGHTS):
        w = inp[name].astype(_jnp.float32)
        if MOMENT_SCALE is None:
            s = _jnp.sqrt(_jnp.mean(_jnp.square(w)) + 1e-30)
        else:
            s = MOMENT_SCALE[name]
        km, kv = _jax.random.split(_jax.random.fold_in(key, i + 1))
        out[name] = w
        out["m_" + name] = s * _jax.random.normal(km, w.shape, _jnp.float32)
        out["v_" + name] = (s * s) * _jax.random.uniform(kv, w.shape, _jnp.float32, 0.5, 1.5)
    if N_MICROBATCH > 1:
        for name, axis in PER_EXAMPLE_BATCH_AXIS.items():
            out[name] = _to_microbatches(out[name], axis)
    return {'x': out['x'], 'c': out['c'], 'ctx': out['ctx'], 'c_ctx': out['c_ctx'], 'w_mod': out['w_mod'], 'b_mod': out['b_mod'], 'norm_g': out['norm_g'], 'ssm_w_in': out['ssm_w_in'], 'ssm_a_re': out['ssm_a_re'], 'ssm_a_im': out['ssm_a_im'], 'ssm_log_dt': out['ssm_log_dt'], 'ssm_b_re': out['ssm_b_re'], 'ssm_b_im': out['ssm_b_im'], 'ssm_c_re': out['ssm_c_re'], 'ssm_c_im': out['ssm_c_im'], 'ssm_d': out['ssm_d'], 'ssm_w_glu': out['ssm_w_glu'], 'ssm_b_glu': out['ssm_b_glu'], 'ssm_w_out': out['ssm_w_out'], 'attn_w_in': out['attn_w_in'], 'attn_q_norm': out['attn_q_norm'], 'attn_k_norm': out['attn_k_norm'], 'attn_w_out': out['attn_w_out'], 'final_norm_g': out['final_norm_g'], 'loss_target': out['loss_target'], 'm_c_ctx': out['m_c_ctx'], 'm_w_mod': out['m_w_mod'], 'm_b_mod': out['m_b_mod'], 'm_norm_g': out['m_norm_g'], 'm_ssm_w_in': out['m_ssm_w_in'], 'm_ssm_a_re': out['m_ssm_a_re'], 'm_ssm_a_im': out['m_ssm_a_im'], 'm_ssm_log_dt': out['m_ssm_log_dt'], 'm_ssm_b_re': out['m_ssm_b_re'], 'm_ssm_b_im': out['m_ssm_b_im'], 'm_ssm_c_re': out['m_ssm_c_re'], 'm_ssm_c_im': out['m_ssm_c_im'], 'm_ssm_d': out['m_ssm_d'], 'm_ssm_w_glu': out['m_ssm_w_glu'], 'm_ssm_b_glu': out['m_ssm_b_glu'], 'm_ssm_w_out': out['m_ssm_w_out'], 'm_attn_w_in': out['m_attn_w_in'], 'm_attn_q_norm': out['m_attn_q_norm'], 'm_attn_k_norm': out['m_attn_k_norm'], 'm_attn_w_out': out['m_attn_w_out'], 'm_final_norm_g': out['m_final_norm_g'], 'v_c_ctx': out['v_c_ctx'], 'v_w_mod': out['v_w_mod'], 'v_b_mod': out['v_b_mod'], 'v_norm_g': out['v_norm_g'], 'v_ssm_w_in': out['v_ssm_w_in'], 'v_ssm_a_re': out['v_ssm_a_re'], 'v_ssm_a_im': out['v_ssm_a_im'], 'v_ssm_log_dt': out['v_ssm_log_dt'], 'v_ssm_b_re': out['v_ssm_b_re'], 'v_ssm_b_im': out['v_ssm_b_im'], 'v_ssm_c_re': out['v_ssm_c_re'], 'v_ssm_c_im': out['v_ssm_c_im'], 'v_ssm_d': out['v_ssm_d'], 'v_ssm_w_glu': out['v_ssm_w_glu'], 'v_ssm_b_glu': out['v_ssm_b_glu'], 'v_ssm_w_out': out['v_ssm_w_out'], 'v_attn_w_in': out['v_attn_w_in'], 'v_attn_q_norm': out['v_attn_q_norm'], 'v_attn_k_norm': out['v_attn_k_norm'], 'v_attn_w_out': out['v_attn_w_out'], 'v_final_norm_g': out['v_final_norm_g']}


def _loss(weights, diff, rest, loss_target):
    with _jax.named_scope("forward"):
        args = {**rest, TWIN_DIFF_INPUT: diff, **{k: w.astype(_WEIGHT_DTYPES[k]) for k, w in weights.items()}}
        y = _forward(args)
    with _jax.named_scope("loss_head"):
        err = _jnp.square(y.astype(_jnp.float32) - loss_target)
        return 0.5 * _jnp.sum(_jnp.mean(err, axis=-1)) if err.ndim else 0.5 * err


def _adamw(w, g, m, v):
    m = ADAM_B1 * m + (1.0 - ADAM_B1) * g
    v = ADAM_B2 * v + (1.0 - ADAM_B2) * _jnp.square(g)
    m_hat = m / (1.0 - ADAM_B1 ** ADAM_STEP)
    v_hat = v / (1.0 - ADAM_B2 ** ADAM_STEP)
    delta = -ADAM_LR * (m_hat / (_jnp.sqrt(v_hat) + ADAM_EPS) + ADAM_WD * w)
    return delta, m, v


def reference(x, c, ctx, c_ctx, w_mod, b_mod, norm_g, ssm_w_in, ssm_a_re, ssm_a_im, ssm_log_dt, ssm_b_re, ssm_b_im, ssm_c_re, ssm_c_im, ssm_d, ssm_w_glu, ssm_b_glu, ssm_w_out, attn_w_in, attn_q_norm, attn_k_norm, attn_w_out, final_norm_g, loss_target, m_c_ctx, m_w_mod, m_b_mod, m_norm_g, m_ssm_w_in, m_ssm_a_re, m_ssm_a_im, m_ssm_log_dt, m_ssm_b_re, m_ssm_b_im, m_ssm_c_re, m_ssm_c_im, m_ssm_d, m_ssm_w_glu, m_ssm_b_glu, m_ssm_w_out, m_attn_w_in, m_attn_q_norm, m_attn_k_norm, m_attn_w_out, m_final_norm_g, v_c_ctx, v_w_mod, v_b_mod, v_norm_g, v_ssm_w_in, v_ssm_a_re, v_ssm_a_im, v_ssm_log_dt, v_ssm_b_re, v_ssm_b_im, v_ssm_c_re, v_ssm_c_im, v_ssm_d, v_ssm_w_glu, v_ssm_b_glu, v_ssm_w_out, v_attn_w_in, v_attn_q_norm, v_attn_k_norm, v_attn_w_out, v_final_norm_g):
    given = dict(x=x, c=c, ctx=ctx, c_ctx=c_ctx, w_mod=w_mod, b_mod=b_mod, norm_g=norm_g, ssm_w_in=ssm_w_in, ssm_a_re=ssm_a_re, ssm_a_im=ssm_a_im, ssm_log_dt=ssm_log_dt, ssm_b_re=ssm_b_re, ssm_b_im=ssm_b_im, ssm_c_re=ssm_c_re, ssm_c_im=ssm_c_im, ssm_d=ssm_d, ssm_w_glu=ssm_w_glu, ssm_b_glu=ssm_b_glu, ssm_w_out=ssm_w_out, attn_w_in=attn_w_in, attn_q_norm=attn_q_norm, attn_k_norm=attn_k_norm, attn_w_out=attn_w_out, final_norm_g=final_norm_g, loss_target=loss_target, m_c_ctx=m_c_ctx, m_w_mod=m_w_mod, m_b_mod=m_b_mod, m_norm_g=m_norm_g, m_ssm_w_in=m_ssm_w_in, m_ssm_a_re=m_ssm_a_re, m_ssm_a_im=m_ssm_a_im, m_ssm_log_dt=m_ssm_log_dt, m_ssm_b_re=m_ssm_b_re, m_ssm_b_im=m_ssm_b_im, m_ssm_c_re=m_ssm_c_re, m_ssm_c_im=m_ssm_c_im, m_ssm_d=m_ssm_d, m_ssm_w_glu=m_ssm_w_glu, m_ssm_b_glu=m_ssm_b_glu, m_ssm_w_out=m_ssm_w_out, m_attn_w_in=m_attn_w_in, m_attn_q_norm=m_attn_q_norm, m_attn_k_norm=m_attn_k_norm, m_attn_w_out=m_attn_w_out, m_final_norm_g=m_final_norm_g, v_c_ctx=v_c_ctx, v_w_mod=v_w_mod, v_b_mod=v_b_mod, v_norm_g=v_norm_g, v_ssm_w_in=v_ssm_w_in, v_ssm_a_re=v_ssm_a_re, v_ssm_a_im=v_ssm_a_im, v_ssm_log_dt=v_ssm_log_dt, v_ssm_b_re=v_ssm_b_re, v_ssm_b_im=v_ssm_b_im, v_ssm_c_re=v_ssm_c_re, v_ssm_c_im=v_ssm_c_im, v_ssm_d=v_ssm_d, v_ssm_w_glu=v_ssm_w_glu, v_ssm_b_glu=v_ssm_b_glu, v_ssm_w_out=v_ssm_w_out, v_attn_w_in=v_attn_w_in, v_attn_q_norm=v_attn_q_norm, v_attn_k_norm=v_attn_k_norm, v_attn_w_out=v_attn_w_out, v_final_norm_g=v_final_norm_g)
    weights = {n: given[n] for n in TWIN_WEIGHTS}
    shared = {n: given[n] for n in SHARED_INPUTS}
    per_example = {n: given[n] for n in ['x', 'c', 'ctx']}
    grad_fn = _jax.value_and_grad(_loss, argnums=(0, 1))

    def one_microbatch(ex, loss_target):
        ex = dict(ex)
        diff = ex.pop(TWIN_DIFF_INPUT)
        return grad_fn(weights, diff, {**shared, **ex}, loss_target)

    if N_MICROBATCH == 1:
        loss, (grad_w, grad_x) = one_microbatch(per_example, given["loss_target"])
    else:
        def body(carry, xs):
            loss_sum, grad_sum = carry
            l_k, (gw_k, gx_k) = one_microbatch(xs[0], xs[1])
            with _jax.named_scope("update"):
                return (loss_sum + l_k, _jax.tree.map(_jnp.add, grad_sum, gw_k)), gx_k

        init = (_jnp.zeros((), _jnp.float32), _jax.tree.map(_jnp.zeros_like, weights))
        (loss, grad_w), grad_x = _jax.lax.scan(body, init, (per_example, given["loss_target"]))
    with _jax.named_scope("update"):
        delta_w, new_m, new_v = {}, {}, {}
        for n in TWIN_WEIGHTS:
            delta_w[n], new_m[n], new_v[n] = _adamw(weights[n], grad_w[n], given["m_" + n], given["v_" + n])
    return (loss, grad_x, *[grad_w[n] for n in TWIN_WEIGHTS], *[delta_w[n] for n in TWIN_WEIGHTS],
            *[new_m[n] for n in TWIN_WEIGHTS], *[new_v[n] for n in TWIN_WEIGHTS])
```

```python
import functools
import math

import jax
import jax.numpy as jnp
from jax import lax
from jax.experimental import pallas as pl
from jax.experimental.pallas import tpu as pltpu

F32 = jnp.float32
BF16 = jnp.bfloat16

N_DEV = 8
D_MODEL = 1024
NORM_EPS = 1e-6
SSM_GROUP = 16
SSM_GROUPS = 64
SSM_STATE = 64
GROUPS_PER_BLOCK = 8
N_BLOCKS = SSM_GROUPS // GROUPS_PER_BLOCK
HALF = GROUPS_PER_BLOCK * SSM_STATE
HEAD_DIM = 64
N_Q_HEADS = 16
N_KV_HEADS = 4
KV_REP = N_Q_HEADS // N_KV_HEADS
KV_WIDTH = N_KV_HEADS * HEAD_DIM
GRID_W = 64
ROPE_THETA = 10000.0
ADAM_LR, ADAM_B1, ADAM_B2, ADAM_EPS, ADAM_WD, ADAM_STEP = 0.001, 0.9, 0.999, 1e-08, 0.01, 10

ROW_TILE = 256
SCAN_CHUNK = 128
VMEM_LIMIT = 56 * 1024 * 1024
MESH_IDS = pl.DeviceIdType.MESH


def _cparams(n_axes):
    return pltpu.CompilerParams(dimension_semantics=("arbitrary",) * n_axes, vmem_limit_bytes=VMEM_LIMIT)


def _dot(a, b):
    return jnp.dot(a.astype(BF16), b.astype(BF16), preferred_element_type=F32)


def _dot_t0(a, b):
    return lax.dot_general(a.astype(BF16), b.astype(BF16), (((0,), (0,)), ((), ())), preferred_element_type=F32)


def _dot_t1(a, b):
    return lax.dot_general(a.astype(BF16), b.astype(BF16), (((1,), (1,)), ((), ())), preferred_element_type=F32)


def _s5_prep(a_re, a_im, log_dt, b_re, b_im):
    dt = jnp.exp(log_dt)[:, None]
    ldr, ldi = a_re * dt, a_im * dt
    mag = jnp.exp(ldr)
    abar_re, abar_im = mag * jnp.cos(ldi), mag * jnp.sin(ldi)
    den = a_re * a_re + a_im * a_im
    num_re, num_im = abar_re - 1.0, abar_im
    coef_re = (num_re * a_re + num_im * a_im) / den
    coef_im = (num_im * a_re - num_re * a_im) / den
    bbar_re = coef_re[..., None] * b_re - coef_im[..., None] * b_im
    bbar_im = coef_re[..., None] * b_im + coef_im[..., None] * b_re
    return abar_re, abar_im, bbar_re, bbar_im


def _s5_blocks(abar_re, abar_im, bbar_re, bbar_im, c_re, c_im):
    eye = jnp.eye(GROUPS_PER_BLOCK, dtype=F32)
    bb = jnp.stack([bbar_re, bbar_im]).reshape(2, N_BLOCKS, GROUPS_PER_BLOCK, SSM_STATE, SSM_GROUP)
    b_blk = jnp.einsum('rqgph,gk->qghrkp', bb, eye).reshape(N_BLOCKS, 128, 2 * HALF)
    cc = jnp.stack([c_re, -c_im]).reshape(2, N_BLOCKS, GROUPS_PER_BLOCK, SSM_GROUP, SSM_STATE)
    c_blk = jnp.einsum('rqghp,gk->qrgpkh', cc, eye).reshape(N_BLOCKS, 2 * HALF, 128)
    abar = jnp.stack([abar_re.reshape(N_BLOCKS, HALF), abar_im.reshape(N_BLOCKS, HALF)])
    return abar, b_blk, c_blk


def _s5_unblock(d_b_blk, d_ct_blk):
    db = d_b_blk.reshape(N_BLOCKS, GROUPS_PER_BLOCK, SSM_GROUP, 2, GROUPS_PER_BLOCK, SSM_STATE)
    db = jnp.einsum('qghrgp->rqgph', db).reshape(2, SSM_GROUPS, SSM_STATE, SSM_GROUP)
    dc = d_ct_blk.reshape(N_BLOCKS, GROUPS_PER_BLOCK, SSM_GROUP, 2, GROUPS_PER_BLOCK, SSM_STATE)
    dc = jnp.einsum('qghrgp->rqghp', dc).reshape(2, SSM_GROUPS, SSM_GROUP, SSM_STATE)
    return db[0], db[1], dc[0], -dc[1]


def _scan_chunk_of_step(j, n_chunks, n_ctx_chunks, reverse):
    if not reverse:
        return j
    return jnp.where(j < n_ctx_chunks, n_ctx_chunks - 1 - j, n_chunks - 1 - j + n_ctx_chunks)


def _s5_scan_fwd(u, abar, b_blk, c_blk, d_skip, n_ctx, reverse):
    n_rows, width = u.shape
    tc = SCAN_CHUNK
    n_chunks, n_ctx_chunks = n_rows // tc, n_ctx // tc
    with_skip = d_skip is not None

    def body(*refs):
        if with_skip:
            u_ref, a_ref, b_ref, c_ref, d_ref, y_ref, hb_ref, s_ref, h_ref = refs
        else:
            u_ref, a_ref, b_ref, c_ref, y_ref, hb_ref, s_ref, h_ref = refs
        j = pl.program_id(0)

        @pl.when(j == 0)
        def _():
            h_ref[...] = jnp.zeros_like(h_ref)

        hb_ref[0] = h_ref[...]
        for q in range(N_BLOCKS):
            s_ref[:, q, :] = _dot(u_ref[:, q * 128:(q + 1) * 128], b_ref[q])
        ar, ai = a_ref[0], a_ref[1]

        def step(s, carry):
            hr, hi = carry
            t = tc - 1 - s if reverse else s
            x = s_ref[t]
            nr = ar * hr - ai * hi + x[:, :HALF]
            ni = ar * hi + ai * hr + x[:, HALF:]
            s_ref[t] = jnp.concatenate([nr, ni], axis=1)
            return nr, ni

        hr, hi = lax.fori_loop(0, tc, step, (h_ref[:, :HALF], h_ref[:, HALF:]), unroll=8)
        h_ref[...] = jnp.concatenate([hr, hi], axis=1)
        for q in range(N_BLOCKS):
            yq = _dot(s_ref[:, q, :], c_ref[q])
            if with_skip:
                yq = yq + d_ref[:, q * 128:(q + 1) * 128] * u_ref[:, q * 128:(q + 1) * 128]
            y_ref[:, q * 128:(q + 1) * 128] = yq

    chunk = functools.partial(_scan_chunk_of_step, n_chunks=n_chunks, n_ctx_chunks=n_ctx_chunks, reverse=reverse)
    full3 = lambda j: (0, 0, 0)
    in_specs = [pl.BlockSpec((tc, width), lambda j: (chunk(j), 0)),
                pl.BlockSpec((2, N_BLOCKS, HALF), full3),
                pl.BlockSpec((N_BLOCKS, 128, 2 * HALF), full3),
                pl.BlockSpec((N_BLOCKS, 2 * HALF, 128), full3)]
    args = [u, abar, b_blk.astype(BF16), c_blk.astype(BF16)]
    if with_skip:
        in_specs.append(pl.BlockSpec((1, width), lambda j: (0, 0)))
        args.append(d_skip.reshape(1, width))
    return pl.pallas_call(
        body, name="s5_scan_fwd_rev" if reverse else "s5_scan_fwd",
        grid=(n_chunks,),
        in_specs=in_specs,
        out_specs=[pl.BlockSpec((tc, width), lambda j: (chunk(j), 0)),
                   pl.BlockSpec((1, N_BLOCKS, 2 * HALF), lambda j: (chunk(j), 0, 0))],
        out_shape=[jax.ShapeDtypeStruct((n_rows, width), F32),
                   jax.ShapeDtypeStruct((n_chunks, N_BLOCKS, 2 * HALF), F32)],
        scratch_shapes=[pltpu.VMEM((tc, N_BLOCKS, 2 * HALF), F32), pltpu.VMEM((N_BLOCKS, 2 * HALF), F32)],
        compiler_params=_cparams(1),
    )(*args)


def _s5_scan_bwd(u, dy, hb, abar, b_blk, c_blk, d_skip, n_ctx, reverse):
    n_rows, width = u.shape
    tc = SCAN_CHUNK
    n_chunks, n_ctx_chunks = n_rows // tc, n_ctx // tc
    with_skip = d_skip is not None

    def body(*refs):
        if with_skip:
            (u_ref, dy_ref, hb_ref, a_ref, b_ref, bt_ref, ct_ref, d_ref,
             du_ref, da_ref, db_ref, dct_ref, dd_ref, sh_ref, sg_ref, g_ref) = refs
        else:
            (u_ref, dy_ref, hb_ref, a_ref, b_ref, bt_ref, ct_ref,
             du_ref, da_ref, db_ref, dct_ref, sh_ref, sg_ref, g_ref) = refs
        j = pl.program_id(0)

        @pl.when(j == 0)
        def _():
            g_ref[...] = jnp.zeros_like(g_ref)
            da_ref[...] = jnp.zeros_like(da_ref)
            db_ref[...] = jnp.zeros_like(db_ref)
            dct_ref[...] = jnp.zeros_like(dct_ref)
            if with_skip:
                dd_ref[...] = jnp.zeros_like(dd_ref)

        for q in range(N_BLOCKS):
            sh_ref[:, q, :] = _dot(u_ref[:, q * 128:(q + 1) * 128], b_ref[q])
            sg_ref[:, q, :] = _dot(dy_ref[:, q * 128:(q + 1) * 128], ct_ref[q])
        ar, ai = a_ref[0], a_ref[1]
        time_of = (lambda s: tc - 1 - s) if reverse else (lambda s: s)

        def fwd_step(s, carry):
            hr, hi = carry
            t = time_of(s)
            x = sh_ref[t]
            nr = ar * hr - ai * hi + x[:, :HALF]
            ni = ar * hi + ai * hr + x[:, HALF:]
            sh_ref[t] = jnp.concatenate([nr, ni], axis=1)
            return nr, ni

        h0 = hb_ref[0]
        lax.fori_loop(0, tc, fwd_step, (h0[:, :HALF], h0[:, HALF:]), unroll=8)

        def adj(t, h_prev, carry):
            gr, gi, dar, dai = carry
            x = sg_ref[t]
            nr = ar * gr + ai * gi + x[:, :HALF]
            ni = ar * gi - ai * gr + x[:, HALF:]
            sg_ref[t] = jnp.concatenate([nr, ni], axis=1)
            pr, pi = h_prev[:, :HALF], h_prev[:, HALF:]
            return nr, ni, dar + nr * pr + ni * pi, dai + ni * pr - nr * pi

        def bwd_step(i, carry):
            s = tc - 1 - i
            return adj(time_of(s), sh_ref[time_of(s - 1)], carry)

        carry = (g_ref[:, :HALF], g_ref[:, HALF:], da_ref[0], da_ref[1])
        carry = lax.fori_loop(0, tc - 1, bwd_step, carry, unroll=8)
        gr, gi, dar, dai = adj(time_of(0), h0, carry)
        g_ref[...] = jnp.concatenate([gr, gi], axis=1)
        da_ref[0] = dar
        da_ref[1] = dai

        for q in range(N_BLOCKS):
            cols = slice(q * 128, (q + 1) * 128)
            uq, dyq = u_ref[:, cols], dy_ref[:, cols]
            gq = sg_ref[:, q, :]
            duq = _dot(gq, bt_ref[q])
            if with_skip:
                duq = duq + d_ref[:, cols] * dyq
                dd_ref[:, cols] += jnp.sum(dyq * uq, axis=0, keepdims=True)
            du_ref[:, cols] = duq
            db_ref[q] += _dot_t0(uq, gq)
            dct_ref[q] += _dot_t0(dyq, sh_ref[:, q, :])

    def chunk(j):
        return _scan_chunk_of_step(n_chunks - 1 - j, n_chunks, n_ctx_chunks, reverse)

    full2 = lambda j: (0, 0)
    full3 = lambda j: (0, 0, 0)
    row = pl.BlockSpec((tc, width), lambda j: (chunk(j), 0))
    in_specs = [row, row,
                pl.BlockSpec((1, N_BLOCKS, 2 * HALF), lambda j: (chunk(j), 0, 0)),
                pl.BlockSpec((2, N_BLOCKS, HALF), full3),
                pl.BlockSpec((N_BLOCKS, 128, 2 * HALF), full3),
                pl.BlockSpec((N_BLOCKS, 2 * HALF, 128), full3),
                pl.BlockSpec((N_BLOCKS, 128, 2 * HALF), full3)]
    args = [u, dy, hb, abar, b_blk.astype(BF16), jnp.swapaxes(b_blk, 1, 2).astype(BF16),
            jnp.swapaxes(c_blk, 1, 2).astype(BF16)]
    out_specs = [row,
                 pl.BlockSpec((2, N_BLOCKS, HALF), full3),
                 pl.BlockSpec((N_BLOCKS, 128, 2 * HALF), full3),
                 pl.BlockSpec((N_BLOCKS, 128, 2 * HALF), full3)]
    out_shape = [jax.ShapeDtypeStruct((n_rows, width), F32),
                 jax.ShapeDtypeStruct((2, N_BLOCKS, HALF), F32),
                 jax.ShapeDtypeStruct((N_BLOCKS, 128, 2 * HALF), F32),
                 jax.ShapeDtypeStruct((N_BLOCKS, 128, 2 * HALF), F32)]
    if with_skip:
        in_specs.append(pl.BlockSpec((1, width), full2))
        args.append(d_skip.reshape(1, width))
        out_specs.append(pl.BlockSpec((1, width), full2))
        out_shape.append(jax.ShapeDtypeStruct((1, width), F32))
    return pl.pallas_call(
        body, name="s5_scan_bwd_rev" if reverse else "s5_scan_bwd",
        grid=(n_chunks,),
        in_specs=in_specs, out_specs=out_specs, out_shape=out_shape,
        scratch_shapes=[pltpu.VMEM((tc, N_BLOCKS, 2 * HALF), F32), pltpu.VMEM((tc, N_BLOCKS, 2 * HALF), F32),
                        pltpu.VMEM((N_BLOCKS, 2 * HALF), F32)],
        compiler_params=_cparams(1),
    )(*args)


def _s5_dir_params(d, a_re, a_im, log_dt, b_re, b_im):
    return a_re[d], a_im[d], log_dt[d], b_re[d], b_im[d]


def _s5_forward(u, ssm, n_ctx):
    a_re, a_im, log_dt, b_re, b_im, c_re, c_im, d_skip = ssm
    outs, saved = [], []
    for d in range(2):
        prep = _s5_prep(*_s5_dir_params(d, a_re, a_im, log_dt, b_re, b_im))
        abar, b_blk, c_blk = _s5_blocks(*prep, c_re[d], c_im[d])
        y, hb = _s5_scan_fwd(u, abar, b_blk, c_blk, d_skip if d == 0 else None, n_ctx, reverse=(d == 1))
        outs.append(y)
        saved.append((hb, abar, b_blk, c_blk))
    return outs, saved


def _s5_backward(u, dy, ssm, saved, n_ctx):
    a_re, a_im, log_dt, b_re, b_im, c_re, c_im, d_skip = ssm
    dus, grads = [], [[] for _ in range(7)]
    d_d = None
    for d in range(2):
        hb, abar, b_blk, c_blk = saved[d]
        res = _s5_scan_bwd(u, dy, hb, abar, b_blk, c_blk, d_skip if d == 0 else None, n_ctx, reverse=(d == 1))
        if d == 0:
            du, d_abar, d_b_blk, d_ct_blk, d_d = res
        else:
            du, d_abar, d_b_blk, d_ct_blk = res
        dus.append(du)
        dbb_re, dbb_im, dc_re, dc_im = _s5_unblock(d_b_blk, d_ct_blk)
        _, vjp = jax.vjp(_s5_prep, *_s5_dir_params(d, a_re, a_im, log_dt, b_re, b_im))
        shape = (SSM_GROUPS, SSM_STATE)
        g5 = vjp((d_abar[0].reshape(shape), d_abar[1].reshape(shape), dbb_re, dbb_im))
        for k, g in enumerate(tuple(g5) + (dc_re, dc_im)):
            grads[k].append(g)
    grads = [jnp.stack(g) for g in grads]
    return dus, grads + [d_d.reshape(-1)]


INV_SQRT2 = 0.7071067811865476
INV_SQRT_2PI = 0.3989422804014327


def _rows(cols):
    return pl.BlockSpec((ROW_TILE, cols), lambda i: (i, 0))


def _rows_skip_ctx(cols):
    return pl.BlockSpec((ROW_TILE, cols), lambda i: (i + 1, 0))


def _rows_lat(cols):
    return pl.BlockSpec((ROW_TILE, cols), lambda i: (jnp.maximum(i - 1, 0), 0))


def _full(shape):
    nd = len(shape)
    return pl.BlockSpec(shape, lambda i: (0,) * nd)


def _seg(cols):
    return pl.BlockSpec((1, 1, cols), lambda i: (jnp.minimum(i, 1), 0, 0))


def _lat_seg(cols):
    return pl.BlockSpec((1, 1, cols), lambda i: (1, 0, 0))


def _sds(shape, dtype=F32):
    return jax.ShapeDtypeStruct(shape, dtype)


def _sum0(x):
    return jnp.sum(x, axis=0, keepdims=True)


def _sigmoid(x):
    return jax.nn.sigmoid(x)


def _rms_mod(x, g, scale, shift):
    r = lax.rsqrt(jnp.mean(x * x, axis=-1, keepdims=True) + NORM_EPS)
    return (x * r * g) * (1.0 + scale) + shift


def _rms_mod_bwd(x, g, scale, dh):
    r = lax.rsqrt(jnp.mean(x * x, axis=-1, keepdims=True) + NORM_EPS)
    n = x * r
    dyg = dh * (1.0 + scale)
    dn = dyg * g
    dx = r * (dn - n * jnp.mean(dn * n, axis=-1, keepdims=True))
    return dx, _sum0(dyg * n), _sum0(dh * (n * g)), _sum0(dh)


def _seg_sum(t, seg):
    n = t.shape[1]
    lane = lax.broadcasted_iota(jnp.int32, t.shape, 1)
    s = seg // 2
    while s >= 1:
        t = t + jnp.where((lane & s) == 0, pltpu.roll(t, n - s, 1), pltpu.roll(t, s, 1))
        s //= 2
    return t


def _rope_partner(x):
    n = x.shape[1]
    lane = lax.broadcasted_iota(jnp.int32, x.shape, 1)
    return jnp.where((lane & 16) == 0, pltpu.roll(x, n - 16, 1), pltpu.roll(x, 16, 1))


def _lanes(tab, width):
    return jnp.tile(tab, (1, width // tab.shape[1]))


def _head_norm_rope(x, gain, cos, sin):
    r = lax.rsqrt(_seg_sum(x * x, HEAD_DIM) * (1.0 / HEAD_DIM) + NORM_EPS)
    y = x * r * gain
    return y * cos + _rope_partner(y) * sin


def _head_norm_rope_bwd(x, gain, cos, sin, dout):
    dy = dout * cos + _rope_partner(dout * sin)
    r = lax.rsqrt(_seg_sum(x * x, HEAD_DIM) * (1.0 / HEAD_DIM) + NORM_EPS)
    n = x * r
    dn = dy * gain
    dx = r * (dn - n * (_seg_sum(dn * n, HEAD_DIM) * (1.0 / HEAD_DIM)))
    return dx, _sum0(dy * n)


def _rope_tables(n_ctx, n_lat):
    t = jnp.arange(n_lat)
    pos = jnp.stack([(t // GRID_W).astype(F32), (t % GRID_W).astype(F32)], axis=1)
    n_freq = HEAD_DIM // 4
    freqs = ROPE_THETA ** (-jnp.arange(n_freq, dtype=F32) / n_freq)
    ang = pos[:, :, None] * freqs[None, None, :]
    cos = jnp.repeat(jnp.cos(ang)[:, :, None, :], 2, axis=2).reshape(n_lat, HEAD_DIM)
    sin = jnp.sin(ang)
    sin = jnp.stack([-sin, sin], axis=2).reshape(n_lat, HEAD_DIM)
    cos = jnp.concatenate([jnp.ones((n_ctx, HEAD_DIM), F32), cos], axis=0)
    sin = jnp.concatenate([jnp.zeros((n_ctx, HEAD_DIM), F32), sin], axis=0)
    return jnp.tile(cos, (1, 2)), jnp.tile(sin, (1, 2))


def _ssm_in(xa, g, scale, shift, w_in):
    n_rows, d = xa.shape
    e = w_in.shape[1] // 2

    def body(x_ref, g_ref, sc_ref, sh_ref, w_ref, u_ref, z_ref):
        h = _rms_mod(x_ref[...], g_ref[...], sc_ref[0], sh_ref[0])
        proj = _dot(h, w_ref[...])
        u_ref[...] = proj[:, :e]
        z_ref[...] = proj[:, e:]

    return pl.pallas_call(
        body, name="ssm_in", grid=(n_rows // ROW_TILE,),
        in_specs=[_rows(d), _full((1, d)), _seg(d), _seg(d), _full(w_in.shape)],
        out_specs=[_rows(e), _rows(e)], out_shape=[_sds((n_rows, e)), _sds((n_rows, e))],
        compiler_params=_cparams(1),
    )(xa, g, scale, shift, w_in)


def _s5_post_math(y, z, w_glu, b_glu, w_out):
    er = lax.erf(y * INV_SQRT2)
    g = 0.5 * y * (1.0 + er)
    sg = _sigmoid(_dot(g, w_glu) + b_glu)
    g2 = g * sg
    sz = _sigmoid(z)
    silu_z = z * sz
    m = g2 * silu_z
    return er, g, sg, g2, sz, silu_z, m, _dot(m, w_out)


def _ssm_post(xa, y0, y1, z, gate, w_glu, b_glu, w_out):
    n_rows, d = xa.shape
    e = z.shape[1]

    def body(x_ref, y0_ref, y1_ref, z_ref, gt_ref, wg_ref, bg_ref, wo_ref, o_ref):
        out = _s5_post_math(y0_ref[...] + y1_ref[...], z_ref[...], wg_ref[...], bg_ref[...], wo_ref[...])[-1]
        o_ref[...] = x_ref[...] + gt_ref[0] * out

    return pl.pallas_call(
        body, name="ssm_post", grid=(n_rows // ROW_TILE,),
        in_specs=[_rows(d), _rows(e), _rows(e), _rows(e), _seg(d), _full(w_glu.shape), _full((1, e)), _full(w_out.shape)],
        out_specs=_rows(d), out_shape=_sds((n_rows, d)),
        compiler_params=_cparams(1),
    )(xa, y0, y1, z, gate, w_glu, b_glu, w_out)


def _init_acc(first, *refs):
    @pl.when(first)
    def _():
        for r in refs:
            r[...] = jnp.zeros_like(r)


def _ssm_post_bwd(dxa, y0, y1, z, gate, w_glu, b_glu, w_out, w_glu_t, w_out_t):
    n_rows, d = dxa.shape
    e = z.shape[1]

    def body(dx_ref, y0_ref, y1_ref, z_ref, gt_ref, wg_ref, bg_ref, wo_ref, wgt_ref, wot_ref,
             dy_ref, dz_ref, dgt_ref, dwo_ref, dwg_ref, dbg_ref):
        i = pl.program_id(0)
        _init_acc(i == 0, dwo_ref, dwg_ref, dbg_ref)
        _init_acc(i <= 1, dgt_ref)
        y, zz = y0_ref[...] + y1_ref[...], z_ref[...]
        er, g, sg, g2, sz, silu_z, m, out = _s5_post_math(y, zz, wg_ref[...], bg_ref[...], wo_ref[...])
        dxa_t = dx_ref[...]
        dgt_ref[0] += _sum0(dxa_t * out)
        dout = gt_ref[0] * dxa_t
        dm = _dot(dout, wot_ref[...])
        dwo_ref[...] += _dot_t0(m, dout)
        dg2 = dm * silu_z
        dz_ref[...] = dm * g2 * (sz * (1.0 + zz * (1.0 - sz)))
        dt = dg2 * g * sg * (1.0 - sg)
        dwg_ref[...] += _dot_t0(g, dt)
        dbg_ref[...] += _sum0(dt)
        dg = dg2 * sg + _dot(dt, wgt_ref[...])
        dy_ref[...] = dg * (0.5 * (1.0 + er) + y * jnp.exp(-0.5 * y * y) * INV_SQRT_2PI)

    return pl.pallas_call(
        body, name="ssm_post_bwd", grid=(n_rows // ROW_TILE,),
        in_specs=[_rows(d), _rows(e), _rows(e), _rows(e), _seg(d), _full(w_glu.shape), _full((1, e)), _full(w_out.shape),
                  _full(w_glu_t.shape), _full(w_out_t.shape)],
        out_specs=[_rows(e), _rows(e), _seg(d), _full(w_out.shape), _full(w_glu.shape), _full((1, e))],
        out_shape=[_sds((n_rows, e)), _sds((n_rows, e)), _sds((2, 1, d)), _sds(w_out.shape), _sds(w_glu.shape), _sds((1, e))],
        compiler_params=_cparams(1),
    )(dxa, y0, y1, z, gate, w_glu, b_glu, w_out, w_glu_t, w_out_t)


def _ssm_in_bwd(du0, du1, dz, xa, dxa_next, g, scale, shift, w_in_t):
    n_rows, d = xa.shape
    e = dz.shape[1]
    n_lat = n_rows - ROW_TILE

    def body(du0_ref, du1_ref, dz_ref, x_ref, dn_ref, g_ref, sc_ref, sh_ref, wt_ref,
             gx_ref, dw_ref, dg_ref, dsc_ref, dsh_ref):
        i = pl.program_id(0)
        _init_acc(i == 0, dw_ref, dg_ref)
        _init_acc(i <= 1, dsc_ref, dsh_ref)
        x = x_ref[...]
        h = _rms_mod(x, g_ref[...], sc_ref[0], sh_ref[0])
        dproj = jnp.concatenate([du0_ref[...] + du1_ref[...], dz_ref[...]], axis=1)
        dh = _dot(dproj, wt_ref[...])
        dw_ref[...] += _dot_t0(h, dproj)
        dx, dg, dsc, dsh = _rms_mod_bwd(x, g_ref[...], sc_ref[0], dh)
        dg_ref[...] += dg
        dsc_ref[0] += dsc
        dsh_ref[0] += dsh
        gx_ref[...] = dn_ref[...] + dx

    return pl.pallas_call(
        body, name="ssm_in_bwd", grid=(n_rows // ROW_TILE,),
        in_specs=[_rows(e), _rows(e), _rows(e), _rows(d), _rows(d), _full((1, d)), _seg(d), _seg(d), _full(w_in_t.shape)],
        out_specs=[_rows_lat(d), _full((d, 2 * e)), _full((1, d)), _seg(d), _seg(d)],
        out_shape=[_sds((n_lat, d)), _sds((d, 2 * e)), _sds((1, d)), _sds((2, 1, d)), _sds((2, 1, d))],
        compiler_params=_cparams(1),
    )(du0, du1, dz, xa, dxa_next, g, scale, shift, w_in_t)


Q_WIDTH = N_Q_HEADS * HEAD_DIM
SM_SCALE = 1.0 / math.sqrt(HEAD_DIM)


def _attn_in(xa, g, scale, shift, w_in, q_gain, k_gain, cos, sin):
    n_rows, d = xa.shape
    qk = Q_WIDTH + KV_WIDTH

    def body(x_ref, g_ref, sc_ref, sh_ref, w_ref, qg_ref, kg_ref, cos_ref, sin_ref, q_ref, k_ref, v_ref, z_ref, raw_ref):
        h = _rms_mod(x_ref[...], g_ref[...], sc_ref[0], sh_ref[0])
        proj = _dot(h, w_ref[...])
        q_raw, k_raw = proj[:, :Q_WIDTH], proj[:, Q_WIDTH:qk]
        cos, sin = cos_ref[...], sin_ref[...]
        q = _head_norm_rope(q_raw, qg_ref[...], _lanes(cos, Q_WIDTH), _lanes(sin, Q_WIDTH))
        k = _head_norm_rope(k_raw, kg_ref[...], _lanes(cos, KV_WIDTH), _lanes(sin, KV_WIDTH))
        q_ref[...] = (q * SM_SCALE).astype(BF16)
        k_ref[...] = k.astype(BF16)
        v_ref[...] = proj[:, qk:qk + KV_WIDTH].astype(BF16)
        z_ref[...] = proj[:, qk + KV_WIDTH:]
        raw_ref[...] = proj[:, :qk]

    return pl.pallas_call(
        body, name="attn_in", grid=(n_rows // ROW_TILE,),
        in_specs=[_rows(d), _full((1, d)), _seg(d), _seg(d), _full(w_in.shape), _full((1, Q_WIDTH)), _full((1, KV_WIDTH)),
                  _rows(128), _rows(128)],
        out_specs=[_rows(Q_WIDTH), _rows(KV_WIDTH), _rows(KV_WIDTH), _rows(Q_WIDTH), _rows(qk)],
        out_shape=[_sds((n_rows, Q_WIDTH), BF16), _sds((n_rows, KV_WIDTH), BF16), _sds((n_rows, KV_WIDTH), BF16),
                   _sds((n_rows, Q_WIDTH)), _sds((n_rows, qk))],
        compiler_params=_cparams(1),
    )(xa, g, scale, shift, w_in, q_gain, k_gain, cos, sin)


def _kv_tile(n_keys):
    return 768 if n_keys % 768 == 0 else 256


Q_TILE = 256


def _flash_fwd(q, k, v, n_ctx):
    n_keys = k.shape[1]
    n_lat = n_keys - n_ctx
    tq, tk = Q_TILE, _kv_tile(n_keys)
    rows = KV_REP * tq
    off = n_ctx // tq
    n_kv = n_keys // tk

    def body(q_ref, k_ref, v_ref, o_ref, lse_ref, m_ref, l_ref, acc_ref):
        kj = pl.program_id(2)

        @pl.when(kj == 0)
        def _():
            m_ref[...] = jnp.full_like(m_ref, -jnp.inf)
            l_ref[...] = jnp.zeros_like(l_ref)
            acc_ref[...] = jnp.zeros_like(acc_ref)

        s = _dot_t1(q_ref[...].reshape(rows, HEAD_DIM), k_ref[0])
        m_prev = m_ref[...]
        m_new = jnp.maximum(m_prev, jnp.max(s, axis=1, keepdims=True))
        alpha = jnp.exp(m_prev - m_new)
        p = jnp.exp(s - m_new)
        l_ref[...] = alpha * l_ref[...] + jnp.sum(p, axis=1, keepdims=True)
        acc_ref[...] = alpha * acc_ref[...] + _dot(p, v_ref[0])
        m_ref[...] = m_new

        @pl.when(kj == n_kv - 1)
        def _():
            l = l_ref[...]
            o_ref[...] = (acc_ref[...] / l).reshape(KV_REP, tq, HEAD_DIM)
            lse_ref[...] = (m_ref[...] + jnp.log(l)).reshape(KV_REP, tq, 1)

    return pl.pallas_call(
        body, name="flash_fwd", grid=(N_KV_HEADS, n_lat // tq, n_kv),
        in_specs=[pl.BlockSpec((KV_REP, tq, HEAD_DIM), lambda g, i, j: (g, i + off, 0)),
                  pl.BlockSpec((1, tk, HEAD_DIM), lambda g, i, j: (g, j, 0)),
                  pl.BlockSpec((1, tk, HEAD_DIM), lambda g, i, j: (g, j, 0))],
        out_specs=[pl.BlockSpec((KV_REP, tq, HEAD_DIM), lambda g, i, j: (g, i, 0)),
                   pl.BlockSpec((KV_REP, tq, 1), lambda g, i, j: (g, i, 0))],
        out_shape=[_sds((N_Q_HEADS, n_lat, HEAD_DIM)), _sds((N_Q_HEADS, n_lat, 1))],
        scratch_shapes=[pltpu.VMEM((rows, 1), F32), pltpu.VMEM((rows, 1), F32), pltpu.VMEM((rows, HEAD_DIM), F32)],
        compiler_params=_cparams(3),
    )(q, k, v)


def _flash_bwd_dq(q, k, v, o, do, lse, n_ctx):
    n_keys = k.shape[1]
    n_lat = n_keys - n_ctx
    tq, tk = Q_TILE, _kv_tile(n_keys)
    rows = KV_REP * tq
    off = n_ctx // tq
    n_kv = n_keys // tk

    def body(q_ref, k_ref, v_ref, o_ref, do_ref, lse_ref, dq_ref, dl_ref, acc_ref, dl_s):
        kj = pl.program_id(2)
        do = do_ref[...].reshape(rows, HEAD_DIM)

        @pl.when(kj == 0)
        def _():
            acc_ref[...] = jnp.zeros_like(acc_ref)
            dl_s[...] = jnp.sum(do * o_ref[...].reshape(rows, HEAD_DIM), axis=1, keepdims=True)

        k = k_ref[0]
        s = _dot_t1(q_ref[...].reshape(rows, HEAD_DIM), k)
        p = jnp.exp(s - lse_ref[...].reshape(rows, 1))
        ds = p * (_dot_t1(do, v_ref[0]) - dl_s[...])
        acc_ref[...] += _dot(ds, k)

        @pl.when(kj == n_kv - 1)
        def _():
            dq_ref[...] = acc_ref[...].reshape(KV_REP, tq, HEAD_DIM)
            dl_ref[...] = dl_s[...].reshape(KV_REP, tq, 1)

    qspec = lambda o_: pl.BlockSpec((KV_REP, tq, HEAD_DIM), lambda g, i, j: (g, i + o_, 0))
    kspec = pl.BlockSpec((1, tk, HEAD_DIM), lambda g, i, j: (g, j, 0))
    col = pl.BlockSpec((KV_REP, tq, 1), lambda g, i, j: (g, i, 0))
    return pl.pallas_call(
        body, name="flash_bwd_dq", grid=(N_KV_HEADS, n_lat // tq, n_kv),
        in_specs=[qspec(off), kspec, kspec, qspec(0), qspec(0), col],
        out_specs=[qspec(0), col],
        out_shape=[_sds((N_Q_HEADS, n_lat, HEAD_DIM)), _sds((N_Q_HEADS, n_lat, 1))],
        scratch_shapes=[pltpu.VMEM((rows, HEAD_DIM), F32), pltpu.VMEM((rows, 1), F32)],
        compiler_params=_cparams(3),
    )(q, k, v, o, do, lse)


def _flash_bwd_dkv(q, k, v, do, lse_row, delta_row, n_ctx):
    n_keys = k.shape[1]
    n_lat = n_keys - n_ctx
    tq, tk = Q_TILE, _kv_tile(n_keys)
    rows = KV_REP * tq
    off = n_ctx // tq
    n_q = n_lat // tq

    def body(q_ref, k_ref, v_ref, do_ref, lse_ref, dl_ref, dk_ref, dv_ref, dk_acc, dv_acc):
        qi = pl.program_id(2)
        _init_acc(qi == 0, dk_acc, dv_acc)
        q = q_ref[...].reshape(rows, HEAD_DIM)
        do = do_ref[...].reshape(rows, HEAD_DIM)
        lse = jnp.concatenate([lse_ref[h] for h in range(KV_REP)], axis=1)
        delta = jnp.concatenate([dl_ref[h] for h in range(KV_REP)], axis=1)
        p_t = jnp.exp(_dot_t1(k_ref[0], q) - lse)
        dv_acc[...] += _dot(p_t, do)
        ds_t = p_t * (_dot_t1(v_ref[0], do) - delta)
        dk_acc[...] += _dot(ds_t, q)

        @pl.when(qi == n_q - 1)
        def _():
            dk_ref[0] = dk_acc[...]
            dv_ref[0] = dv_acc[...]

    qspec = lambda o_: pl.BlockSpec((KV_REP, tq, HEAD_DIM), lambda g, j, i: (g, i + o_, 0))
    kspec = pl.BlockSpec((1, tk, HEAD_DIM), lambda g, j, i: (g, j, 0))
    rowspec = pl.BlockSpec((KV_REP, 1, tq), lambda g, j, i: (g, 0, i))
    return pl.pallas_call(
        body, name="flash_bwd_dkv", grid=(N_KV_HEADS, n_keys // tk, n_q),
        in_specs=[qspec(off), kspec, kspec, qspec(0), rowspec, rowspec],
        out_specs=[kspec, kspec],
        out_shape=[_sds((N_KV_HEADS, n_keys, HEAD_DIM)), _sds((N_KV_HEADS, n_keys, HEAD_DIM))],
        scratch_shapes=[pltpu.VMEM((tk, HEAD_DIM), F32), pltpu.VMEM((tk, HEAD_DIM), F32)],
        compiler_params=_cparams(3),
    )(q, k, v, do, lse_row, delta_row)


def _attn_post_loss(o, z, xa, gate, w_out, w_out_t, final_g, target):
    n_lat, d = target.shape
    e = o.shape[1]

    def body(o_ref, z_ref, x_ref, gt_ref, w_ref, wt_ref, fg_ref, tg_ref,
             do_ref, dz_ref, dx_ref, loss_ref, dfg_ref, dgt_ref, dw_ref):
        _init_acc(pl.program_id(0) == 0, loss_ref, dfg_ref, dgt_ref, dw_ref)
        oo, zz, gate_t, fg = o_ref[...], z_ref[...], gt_ref[0], fg_ref[...]
        sz = _sigmoid(zz)
        silu_z = zz * sz
        m = oo * silu_z
        out = _dot(m, w_ref[...])
        x2 = x_ref[...] + gate_t * out
        r = lax.rsqrt(jnp.mean(x2 * x2, axis=-1, keepdims=True) + NORM_EPS)
        n = x2 * r
        err = n * fg - tg_ref[...]
        loss_ref[...] += 0.5 * jnp.sum(jnp.mean(err * err, axis=-1, keepdims=True), axis=0, keepdims=True)
        dy = err * (1.0 / d)
        dfg_ref[...] += _sum0(dy * n)
        dn = dy * fg
        dx2 = r * (dn - n * jnp.mean(dn * n, axis=-1, keepdims=True))
        dx_ref[...] = dx2
        dgt_ref[...] += _sum0(dx2 * out)
        dout = gate_t * dx2
        dw_ref[...] += _dot_t0(m, dout)
        dm = _dot(dout, wt_ref[...])
        do_ref[...] = dm * silu_z
        dz_ref[...] = dm * oo * (sz * (1.0 + zz * (1.0 - sz)))

    return pl.pallas_call(
        body, name="attn_post_loss", grid=(n_lat // ROW_TILE,),
        in_specs=[_rows(e), _rows_skip_ctx(e), _rows_skip_ctx(d), _lat_seg(d), _full(w_out.shape), _full(w_out_t.shape),
                  _full((1, d)), _rows(d)],
        out_specs=[_rows(e), _rows(e), _rows(d), _full((1, 1)), _full((1, d)), _full((1, d)), _full(w_out.shape)],
        out_shape=[_sds((n_lat, e)), _sds((n_lat, e)), _sds((n_lat, d)), _sds((1, 1)), _sds((1, d)), _sds((1, d)),
                   _sds(w_out.shape)],
        compiler_params=_cparams(1),
    )(o, z, xa, gate, w_out, w_out_t, final_g, target)


def _attn_in_bwd(dq, dk, dv, dz, raw, xa, dx2, g, scale, shift, q_gain, k_gain, cos, sin, w_in_t):
    n_rows, d = xa.shape
    qk = Q_WIDTH + KV_WIDTH
    n_in = w_in_t.shape[0]

    def body(dq_ref, dk_ref, dv_ref, dz_ref, raw_ref, x_ref, dx2_ref, g_ref, sc_ref, sh_ref, qg_ref, kg_ref, cos_ref, sin_ref,
             wt_ref, dxa_ref, dw_ref, dqg_ref, dkg_ref, dg_ref, dsc_ref, dsh_ref):
        i = pl.program_id(0)
        _init_acc(i == 0, dw_ref, dqg_ref, dkg_ref, dg_ref)
        _init_acc(i <= 1, dsc_ref, dsh_ref)
        is_lat = (i > 0).astype(F32)
        x = x_ref[...]
        h = _rms_mod(x, g_ref[...], sc_ref[0], sh_ref[0])
        cos, sin = cos_ref[...], sin_ref[...]
        raw_t = raw_ref[...]
        dq_raw, dqg = _head_norm_rope_bwd(raw_t[:, :Q_WIDTH], qg_ref[...], _lanes(cos, Q_WIDTH), _lanes(sin, Q_WIDTH),
                                          dq_ref[...] * (SM_SCALE * is_lat))
        dk_raw, dkg = _head_norm_rope_bwd(raw_t[:, Q_WIDTH:], kg_ref[...], _lanes(cos, KV_WIDTH), _lanes(sin, KV_WIDTH),
                                          dk_ref[...])
        dqg_ref[...] += dqg
        dkg_ref[...] += dkg
        dproj = jnp.concatenate([dq_raw, dk_raw, dv_ref[...], dz_ref[...] * is_lat], axis=1)
        dh = _dot(dproj, wt_ref[...])
        dw_ref[...] += _dot_t0(h, dproj)
        dx, dg, dsc, dsh = _rms_mod_bwd(x, g_ref[...], sc_ref[0], dh)
        dg_ref[...] += dg
        dsc_ref[0] += dsc
        dsh_ref[0] += dsh
        dxa_ref[...] = dx + dx2_ref[...] * is_lat

    return pl.pallas_call(
        body, name="attn_in_bwd", grid=(n_rows // ROW_TILE,),
        in_specs=[_rows_lat(Q_WIDTH), _rows(KV_WIDTH), _rows(KV_WIDTH), _rows_lat(Q_WIDTH), _rows(qk), _rows(d), _rows_lat(d),
                  _full((1, d)), _seg(d), _seg(d), _full((1, Q_WIDTH)), _full((1, KV_WIDTH)), _rows(128), _rows(128),
                  _full(w_in_t.shape)],
        out_specs=[_rows(d), _full((d, n_in)), _full((1, Q_WIDTH)), _full((1, KV_WIDTH)), _full((1, d)), _seg(d), _seg(d)],
        out_shape=[_sds((n_rows, d)), _sds((d, n_in)), _sds((1, Q_WIDTH)), _sds((1, KV_WIDTH)), _sds((1, d)),
                   _sds((2, 1, d)), _sds((2, 1, d))],
        compiler_params=_cparams(1),
    )(dq, dk, dv, dz, raw, xa, dx2, g, scale, shift, q_gain, k_gain, cos, sin, w_in_t)


def _heads_major(a, n_heads):
    return a.reshape(a.shape[0], n_heads, HEAD_DIM).transpose(1, 0, 2)


def _tokens_major(a):
    return a.transpose(1, 0, 2).reshape(a.shape[1], a.shape[0] * HEAD_DIM)


def _local_step(x, ctx, target, mods, norm_g, ssm, w_ssm_in, w_glu, b_glu, w_ssm_out, w_attn_in, q_norm, k_norm, w_attn_out,
                final_g):
    n_ctx, d = ctx.shape
    assert n_ctx == ROW_TILE
    n_lat = x.shape[0]
    (shift0, scale0, gate0), (shift1, scale1, gate1) = mods
    g0, g1, fg = norm_g[0:1], norm_g[1:2], final_g.reshape(1, d)
    b_glu = b_glu.reshape(1, -1)
    q_gain = jnp.tile(q_norm.reshape(1, HEAD_DIM), (1, N_Q_HEADS))
    k_gain = jnp.tile(k_norm.reshape(1, HEAD_DIM), (1, N_KV_HEADS))
    cos, sin = _rope_tables(n_ctx, n_lat)

    xa0 = jnp.concatenate([ctx, x], axis=0)
    u, z0 = _ssm_in(xa0, g0, scale0, shift0, w_ssm_in)
    (y0, y1), saved = _s5_forward(u, ssm, n_ctx)
    xa1 = _ssm_post(xa0, y0, y1, z0, gate0, w_glu, b_glu, w_ssm_out)

    q, k, v, z1, raw = _attn_in(xa1, g1, scale1, shift1, w_attn_in, q_gain, k_gain, cos, sin)
    q_h, k_h, v_h = _heads_major(q, N_Q_HEADS), _heads_major(k, N_KV_HEADS), _heads_major(v, N_KV_HEADS)
    o_h, lse = _flash_fwd(q_h, k_h, v_h, n_ctx)
    do, dz1, dx2, loss, d_fg, d_gate1, d_w_attn_out = _attn_post_loss(
        _tokens_major(o_h), z1, xa1, gate1, w_attn_out, w_attn_out.T, fg, target)

    do_h = _heads_major(do, N_Q_HEADS)
    dq_h, delta = _flash_bwd_dq(q_h, k_h, v_h, o_h, do_h, lse, n_ctx)
    as_row = lambda a: a.reshape(N_Q_HEADS, 1, n_lat)
    dk_h, dv_h = _flash_bwd_dkv(q_h, k_h, v_h, do_h, as_row(lse), as_row(delta), n_ctx)
    dxa1, d_w_attn_in, d_qg, d_kg, d_g1, d_scale1, d_shift1 = _attn_in_bwd(
        _tokens_major(dq_h), _tokens_major(dk_h), _tokens_major(dv_h), dz1, raw, xa1, dx2, g1, scale1, shift1,
        q_gain, k_gain, cos, sin, w_attn_in.T)
    dy, dz0, d_gate0, d_w_ssm_out, d_w_glu, d_b_glu = _ssm_post_bwd(
        dxa1, y0, y1, z0, gate0, w_glu, b_glu, w_ssm_out, w_glu.T, w_ssm_out.T)
    (du0, du1), d_ssm = _s5_backward(u, dy, ssm, saved, n_ctx)
    grad_x, d_w_ssm_in, d_g0, d_scale0, d_shift0 = _ssm_in_bwd(du0, du1, dz0, xa0, dxa1, g0, scale0, shift0, w_ssm_in.T)

    d_gate1_seg = jnp.concatenate([jnp.zeros((1, 1, d), F32), d_gate1.reshape(1, 1, d)], axis=0)
    grads = dict(
        norm_g=jnp.concatenate([d_g0, d_g1], axis=0),
        ssm_w_in=d_w_ssm_in, ssm=d_ssm, ssm_w_glu=d_w_glu, ssm_b_glu=d_b_glu.reshape(-1), ssm_w_out=d_w_ssm_out,
        attn_w_in=d_w_attn_in,
        attn_q_norm=d_qg.reshape(N_Q_HEADS, HEAD_DIM).sum(0), attn_k_norm=d_kg.reshape(N_KV_HEADS, HEAD_DIM).sum(0),
        attn_w_out=d_w_attn_out, final_norm_g=d_fg.reshape(-1))
    d_mods = ((d_shift0, d_scale0, d_gate0), (d_shift1, d_scale1, d_gate1_seg))
    return loss[0, 0], grad_x, grads, d_mods


def _my_index():
    return 4 * lax.axis_index("x") + 2 * lax.axis_index("y") + lax.axis_index("c")


def _peer(k):
    mx, my, mc = lax.axis_index("x"), lax.axis_index("y"), lax.axis_index("c")
    px = 1 - mx if k & 4 else mx
    py = 1 - my if k & 2 else my
    pc = 1 - mc if k & 1 else mc
    return (px, py, pc), 4 * px + 2 * py + pc


HBM_SPEC = pl.BlockSpec(memory_space=pl.ANY)


def _exchange(x, name, all_to_all):
    block = x.shape[1:] if all_to_all else x.shape

    def body(x_ref, out_ref, send_sems, recv_sems, local_sem):
        me = _my_index()
        mine = pltpu.make_async_copy(x_ref.at[me] if all_to_all else x_ref, out_ref.at[me], local_sem)
        mine.start()
        sends = []
        for k in range(1, N_DEV):
            peer, peer_idx = _peer(k)
            cp = pltpu.make_async_remote_copy(
                src_ref=x_ref.at[peer_idx] if all_to_all else x_ref, dst_ref=out_ref.at[me],
                send_sem=send_sems.at[k - 1], recv_sem=recv_sems.at[k - 1], device_id=peer, device_id_type=MESH_IDS)
            cp.start()
            sends.append(cp)
        for k in range(1, N_DEV):
            peer, peer_idx = _peer(k)
            pltpu.make_async_remote_copy(
                src_ref=x_ref.at[me] if all_to_all else x_ref, dst_ref=out_ref.at[peer_idx],
                send_sem=send_sems.at[k - 1], recv_sem=recv_sems.at[k - 1], device_id=peer, device_id_type=MESH_IDS).wait_recv()
        for cp in sends:
            cp.wait_send()
        mine.wait()

    return pl.pallas_call(
        body, name=name, in_specs=[HBM_SPEC], out_specs=HBM_SPEC,
        out_shape=_sds((N_DEV,) + tuple(block), x.dtype),
        scratch_shapes=[pltpu.SemaphoreType.DMA((N_DEV - 1,)), pltpu.SemaphoreType.DMA((N_DEV - 1,)), pltpu.SemaphoreType.DMA],
    )(x)


MOD_ROWS = 16
CTX_ROW = N_DEV


def _mod_fwd(cond, w_shard, b_cols):
    n_layers, d, cols = w_shard.shape

    def body(c_ref, w_ref, b_ref, o_ref):
        c = c_ref[...]
        s = c * _sigmoid(c)
        for i in range(n_layers):
            o_ref[i] = _dot(s, w_ref[i]) + b_ref[i]

    return pl.pallas_call(
        body, name="mod_fwd", out_shape=_sds((n_layers, MOD_ROWS, cols)),
        compiler_params=pltpu.CompilerParams(vmem_limit_bytes=VMEM_LIMIT),
    )(cond, w_shard, b_cols.reshape(n_layers, 1, cols))


def _mod_bwd(cond, d_lat_cols, d_ctx_cols, w_shard):
    n_layers, d, cols = w_shard.shape

    def body(c_ref, dl_ref, dc_ref, w_ref, dw_ref, dcc_ref):
        c = c_ref[...]
        sg = _sigmoid(c)
        s = c * sg
        d_s = jnp.zeros((MOD_ROWS, d), F32)
        for i in range(n_layers):
            d_ctx = dc_ref[0, i]
            for j in range(1, N_DEV):
                d_ctx = d_ctx + dc_ref[j, i]
            dm = jnp.concatenate([dl_ref[i], d_ctx, jnp.zeros((MOD_ROWS - N_DEV - 1, cols), F32)], axis=0)
            dw_ref[i] = _dot_t0(s, dm)
            d_s = d_s + _dot_t1(dm, w_ref[i])
        d_c = d_s * (sg * (1.0 + c * (1.0 - sg)))
        dcc_ref[...] = d_c[CTX_ROW:CTX_ROW + 1]

    return pl.pallas_call(
        body, name="mod_bwd", out_shape=[_sds((n_layers, d, cols)), _sds((1, d))],
        compiler_params=pltpu.CompilerParams(vmem_limit_bytes=VMEM_LIMIT),
    )(cond, d_lat_cols, d_ctx_cols, w_shard)


ADAM_TILE = 512


def _adamw(w, g_parts, m, v, name):
    n_parts, n_rows, lanes = g_parts.shape
    c1 = 1.0 - ADAM_B1 ** ADAM_STEP
    c2 = 1.0 - ADAM_B2 ** ADAM_STEP

    def body(w_ref, g_ref, m_ref, v_ref, go_ref, d_ref, mo_ref, vo_ref):
        g = g_ref[0].astype(F32)
        for p in range(1, n_parts):
            g = g + g_ref[p].astype(F32)
        m_new = ADAM_B1 * m_ref[...] + (1.0 - ADAM_B1) * g
        v_new = ADAM_B2 * v_ref[...] + (1.0 - ADAM_B2) * (g * g)
        go_ref[...] = g
        mo_ref[...] = m_new
        vo_ref[...] = v_new
        d_ref[...] = -ADAM_LR * ((m_new / c1) / (jnp.sqrt(v_new / c2) + ADAM_EPS) + ADAM_WD * w_ref[...])

    row = pl.BlockSpec((ADAM_TILE, lanes), lambda i: (i, 0))
    return pl.pallas_call(
        body, name=name, grid=(n_rows // ADAM_TILE,),
        in_specs=[row, pl.BlockSpec((n_parts, ADAM_TILE, lanes), lambda i: (0, i, 0)), row, row],
        out_specs=[row] * 4, out_shape=[_sds((n_rows, lanes))] * 4,
        compiler_params=_cparams(1),
    )(w, g_parts, m, v)


def _pack(arrays, row_multiple):
    parts = []
    for a in arrays:
        flat = a.reshape(-1)
        parts.append(jnp.pad(flat, (0, (-flat.shape[0]) % 1024)))
    flat = jnp.concatenate(parts)
    flat = jnp.pad(flat, (0, (-flat.shape[0]) % (row_multiple * 128)))
    return flat.reshape(-1, 128)


def _unpack(packed, shapes):
    flat = packed.reshape(-1)
    out, pos = [], 0
    for s in shapes:
        n = math.prod(s)
        out.append(flat[pos:pos + n].reshape(s))
        pos += n + (-n) % 1024
    return out


WEIGHT_NAMES = ['c_ctx', 'w_mod', 'b_mod', 'norm_g', 'ssm_w_in', 'ssm_a_re', 'ssm_a_im', 'ssm_log_dt', 'ssm_b_re', 'ssm_b_im',
                'ssm_c_re', 'ssm_c_im', 'ssm_d', 'ssm_w_glu', 'ssm_b_glu', 'ssm_w_out', 'attn_w_in', 'attn_q_norm',
                'attn_k_norm', 'attn_w_out', 'final_norm_g']
SHARDED = ['ssm_w_in', 'ssm_w_glu', 'ssm_w_out', 'attn_w_in', 'attn_w_out']
COLUMN_SHARDED = ('ssm_w_in', 'attn_w_in')
REPLICATED = ['c_ctx', 'b_mod', 'norm_g', 'ssm_a_re', 'ssm_a_im', 'ssm_log_dt', 'ssm_b_re', 'ssm_b_im', 'ssm_c_re', 'ssm_c_im',
              'ssm_d', 'ssm_b_glu', 'attn_q_norm', 'attn_k_norm', 'final_norm_g']
SSM_NAMES = ['ssm_a_re', 'ssm_a_im', 'ssm_log_dt', 'ssm_b_re', 'ssm_b_im', 'ssm_c_re', 'ssm_c_im', 'ssm_d']


def _full_from_shards(gathered, name, shard_shape):
    rows, cols = shard_shape
    w = gathered.reshape(N_DEV, rows, cols)
    if name in COLUMN_SHARDED:
        return w.transpose(1, 0, 2).reshape(rows, N_DEV * cols)
    return w.reshape(N_DEV * rows, cols)


def _shards_from_full(g, name):
    if name in COLUMN_SHARDED:
        rows, cols = g.shape
        g = g.reshape(rows, N_DEV, cols // N_DEV).transpose(1, 0, 2)
    return g.reshape(N_DEV, -1, 128)


def kernel(x, c, ctx, c_ctx, w_mod, b_mod, norm_g, ssm_w_in, ssm_a_re, ssm_a_im, ssm_log_dt, ssm_b_re, ssm_b_im, ssm_c_re, ssm_c_im, ssm_d, ssm_w_glu, ssm_b_glu, ssm_w_out, attn_w_in, attn_q_norm, attn_k_norm, attn_w_out, final_norm_g, loss_target, m_c_ctx, m_w_mod, m_b_mod, m_norm_g, m_ssm_w_in, m_ssm_a_re, m_ssm_a_im, m_ssm_log_dt, m_ssm_b_re, m_ssm_b_im, m_ssm_c_re, m_ssm_c_im, m_ssm_d, m_ssm_w_glu, m_ssm_b_glu, m_ssm_w_out, m_attn_w_in, m_attn_q_norm, m_attn_k_norm, m_attn_w_out, m_final_norm_g, v_c_ctx, v_w_mod, v_b_mod, v_norm_g, v_ssm_w_in, v_ssm_a_re, v_ssm_a_im, v_ssm_log_dt, v_ssm_b_re, v_ssm_b_im, v_ssm_c_re, v_ssm_c_im, v_ssm_d, v_ssm_w_glu, v_ssm_b_glu, v_ssm_w_out, v_attn_w_in, v_attn_q_norm, v_attn_k_norm, v_attn_w_out, v_final_norm_g):
    env = dict(locals())
    weights = {n: env[n] for n in WEIGHT_NAMES}
    mom_m = {n: env["m_" + n] for n in WEIGHT_NAMES}
    mom_v = {n: env["v_" + n] for n in WEIGHT_NAMES}
    d = D_MODEL
    me = _my_index()
    mod_cols = w_mod.shape[-1]

    c_all = _exchange(c.reshape(8, d // 8), "gather_c", False).reshape(N_DEV, d)
    cond = jnp.concatenate([c_all, c_ctx.reshape(1, d), jnp.zeros((MOD_ROWS - N_DEV - 1, d), F32)], axis=0)
    shard_shapes = {n: weights[n].shape[1:] for n in SHARDED}
    w_gathered = _exchange(_pack([weights[n] for n in SHARDED], 1).astype(BF16), "gather_weights", False)
    full, pos = {}, 0
    for n in SHARDED:
        rows = math.prod(shard_shapes[n]) // 128
        full[n] = _full_from_shards(w_gathered[:, pos:pos + rows], n, shard_shapes[n])
        pos += rows

    b_cols = lax.dynamic_slice(b_mod, (0, me * mod_cols), (2, mod_cols))
    mod_shard = _mod_fwd(cond, w_mod, b_cols)
    mod_all = _exchange(mod_shard.reshape(2 * MOD_ROWS, mod_cols), "gather_mod", False)
    mod_full = mod_all.reshape(N_DEV, 2, MOD_ROWS, mod_cols).transpose(1, 2, 0, 3).reshape(2, MOD_ROWS, 3 * d)
    lat_rows = lax.dynamic_slice(mod_full, (0, me, 0), (2, 1, 3 * d))
    mods = []
    for i in range(2):
        seg = jnp.stack([mod_full[i, CTX_ROW:CTX_ROW + 1], lat_rows[i]])
        mods.append((seg[:, :, :d], seg[:, :, d:2 * d], seg[:, :, 2 * d:]))

    ssm = tuple(weights[n][0] for n in SSM_NAMES)
    loss, grad_x, g, d_mods = _local_step(
        x[0], ctx[0], loss_target[0], mods, norm_g, ssm, full['ssm_w_in'], full['ssm_w_glu'], ssm_b_glu[0], full['ssm_w_out'],
        full['attn_w_in'], attn_q_norm[0], attn_k_norm[0], full['attn_w_out'], final_norm_g)
    loss = lax.psum(loss, ("x", "y", "c"))

    d_rows = jnp.stack([jnp.concatenate(dm, axis=-1) for dm in d_mods])
    d_rows = jnp.concatenate([d_rows.reshape(4, 3 * d), jnp.zeros((4, 3 * d), F32)], axis=0)
    d_all = _exchange(d_rows, "gather_dmod", False)[:, :4].reshape(N_DEV, 2, 2, 3 * d)
    d_all = lax.dynamic_slice(d_all, (0, 0, 0, me * mod_cols), (N_DEV, 2, 2, mod_cols))
    d_w_mod, d_c_ctx = _mod_bwd(cond, d_all[:, :, 1].transpose(1, 0, 2), d_all[:, :, 0:1], w_mod)
    d_b_mod = jnp.stack([jnp.concatenate([t[0] + t[1] for t in dm], axis=-1).reshape(3 * d) for dm in d_mods])

    g_big = jnp.concatenate([_shards_from_full(g[n], n) for n in SHARDED], axis=1)
    g_big_parts = _exchange(g_big, "scatter_grads", True)
    pack_big = lambda t: _pack([t[n] for n in SHARDED], ADAM_TILE)
    big = _adamw(pack_big(weights), g_big_parts, pack_big(mom_m), pack_big(mom_v), "adamw_sharded")
    big = [_unpack(t, [weights[n].shape for n in SHARDED]) for t in big]

    mod_res = _adamw(_pack([w_mod], ADAM_TILE), _pack([d_w_mod], ADAM_TILE)[None], _pack([m_w_mod], ADAM_TILE),
                     _pack([v_w_mod], ADAM_TILE), "adamw_w_mod")
    mod_res = [t.reshape(w_mod.shape) for t in mod_res]

    small = dict(zip(SSM_NAMES, g['ssm']))
    small.update(c_ctx=d_c_ctx, b_mod=d_b_mod, norm_g=g['norm_g'], ssm_b_glu=g['ssm_b_glu'], attn_q_norm=g['attn_q_norm'],
                 attn_k_norm=g['attn_k_norm'], final_norm_g=g['final_norm_g'])
    pack_small = lambda t: _pack([t[n] for n in REPLICATED], ADAM_TILE)
    g_small_parts = _exchange(pack_small(small), "gather_small_grads", False)
    rep = _adamw(pack_small(weights), g_small_parts, pack_small(mom_m), pack_small(mom_v), "adamw_replicated")
    rep = [_unpack(t, [weights[n].shape for n in REPLICATED]) for t in rep]

    results = []
    for kind in range(4):
        by_name = dict(zip(SHARDED, big[kind]))
        by_name.update(zip(REPLICATED, rep[kind]))
        by_name['w_mod'] = mod_res[kind]
        results.extend(by_name[n] for n in WEIGHT_NAMES)
    return (loss, grad_x[None], *results)
```

```python
import functools
import math

import jax
import jax.numpy as jnp
from jax import lax
from jax.experimental import pallas as pl
from jax.experimental.pallas import tpu as pltpu

F32 = jnp.float32
BF16 = jnp.bfloat16

N_DEV = 8
D_MODEL = 1024
NORM_EPS = 1e-6
SSM_GROUP = 16
SSM_GROUPS = 64
SSM_STATE = 64
GROUPS_PER_BLOCK = 8
N_BLOCKS = SSM_GROUPS // GROUPS_PER_BLOCK
HALF = GROUPS_PER_BLOCK * SSM_STATE
HEAD_DIM = 64
N_Q_HEADS = 16
N_KV_HEADS = 4
KV_REP = N_Q_HEADS // N_KV_HEADS
KV_WIDTH = N_KV_HEADS * HEAD_DIM
GRID_W = 64
ROPE_THETA = 10000.0
ADAM_LR, ADAM_B1, ADAM_B2, ADAM_EPS, ADAM_WD, ADAM_STEP = 0.001, 0.9, 0.999, 1e-08, 0.01, 10

ROW_TILE = 256
SCAN_CHUNK = 128
VMEM_LIMIT = 56 * 1024 * 1024
MESH_IDS = pl.DeviceIdType.MESH


def _cparams(n_axes):
    return pltpu.CompilerParams(dimension_semantics=("arbitrary",) * n_axes, vmem_limit_bytes=VMEM_LIMIT)


def _dot(a, b):
    return jnp.dot(a.astype(BF16), b.astype(BF16), preferred_element_type=F32)


def _dot_t0(a, b):
    return lax.dot_general(a.astype(BF16), b.astype(BF16), (((0,), (0,)), ((), ())), preferred_element_type=F32)


def _dot_t1(a, b):
    return lax.dot_general(a.astype(BF16), b.astype(BF16), (((1,), (1,)), ((), ())), preferred_element_type=F32)


def _s5_prep(a_re, a_im, log_dt, b_re, b_im):
    dt = jnp.exp(log_dt)[:, None]
    ldr, ldi = a_re * dt, a_im * dt
    mag = jnp.exp(ldr)
    abar_re, abar_im = mag * jnp.cos(ldi), mag * jnp.sin(ldi)
    den = a_re * a_re + a_im * a_im
    num_re, num_im = abar_re - 1.0, abar_im
    coef_re = (num_re * a_re + num_im * a_im) / den
    coef_im = (num_im * a_re - num_re * a_im) / den
    bbar_re = coef_re[..., None] * b_re - coef_im[..., None] * b_im
    bbar_im = coef_re[..., None] * b_im + coef_im[..., None] * b_re
    return abar_re, abar_im, bbar_re, bbar_im


def _s5_blocks(abar_re, abar_im, bbar_re, bbar_im, c_re, c_im):
    eye = jnp.eye(GROUPS_PER_BLOCK, dtype=F32)
    bb = jnp.stack([bbar_re, bbar_im]).reshape(2, N_BLOCKS, GROUPS_PER_BLOCK, SSM_STATE, SSM_GROUP)
    b_blk = jnp.einsum('rqgph,gk->qghrkp', bb, eye).reshape(N_BLOCKS, 128, 2 * HALF)
    cc = jnp.stack([c_re, -c_im]).reshape(2, N_BLOCKS, GROUPS_PER_BLOCK, SSM_GROUP, SSM_STATE)
    c_blk = jnp.einsum('rqghp,gk->qrgpkh', cc, eye).reshape(N_BLOCKS, 2 * HALF, 128)
    abar = jnp.stack([abar_re.reshape(N_BLOCKS, HALF), abar_im.reshape(N_BLOCKS, HALF)])
    return abar, b_blk, c_blk


def _s5_unblock(d_b_blk, d_ct_blk):
    db = d_b_blk.reshape(N_BLOCKS, GROUPS_PER_BLOCK, SSM_GROUP, 2, GROUPS_PER_BLOCK, SSM_STATE)
    db = jnp.einsum('qghrgp->rqgph', db).reshape(2, SSM_GROUPS, SSM_STATE, SSM_GROUP)
    dc = d_ct_blk.reshape(N_BLOCKS, GROUPS_PER_BLOCK, SSM_GROUP, 2, GROUPS_PER_BLOCK, SSM_STATE)
    dc = jnp.einsum('qghrgp->rqghp', dc).reshape(2, SSM_GROUPS, SSM_GROUP, SSM_STATE)
    return db[0], db[1], dc[0], -dc[1]


def _scan_chunk_of_step(j, n_chunks, n_ctx_chunks, reverse):
    if not reverse:
        return j
    return jnp.where(j < n_ctx_chunks, n_ctx_chunks - 1 - j, n_chunks - 1 - j + n_ctx_chunks)


LANE_TILES = 2 * HALF // 128
RE_TILES = HALF // 128


def _tiles(v):
    return [v[:, l * 128:(l + 1) * 128] for l in range(v.shape[1] // 128)]


def _scatter_steps(s_ref, q, x):
    for l in range(LANE_TILES):
        s_ref[l, pl.ds(q, x.shape[0], stride=N_BLOCKS), :] = x[:, l * 128:(l + 1) * 128]


def _gather_steps(s_ref, q, n_steps):
    return jnp.concatenate([s_ref[l, pl.ds(q, n_steps, stride=N_BLOCKS), :] for l in range(LANE_TILES)], axis=1)


def _load_step(s_ref, t):
    row = pl.multiple_of(t * N_BLOCKS, N_BLOCKS)
    return [s_ref[l, pl.ds(row, N_BLOCKS), :] for l in range(LANE_TILES)]


def _store_step(s_ref, t, tiles):
    row = pl.multiple_of(t * N_BLOCKS, N_BLOCKS)
    for l in range(LANE_TILES):
        s_ref[l, pl.ds(row, N_BLOCKS), :] = tiles[l]


def _cmul_add(a, h, x, conj):
    re, im = [], []
    for l in range(RE_TILES):
        ar, ai, hr, hi = a[l], a[RE_TILES + l], h[l], h[RE_TILES + l]
        if conj:
            re.append(ar * hr + ai * hi + x[l])
            im.append(ar * hi - ai * hr + x[RE_TILES + l])
        else:
            re.append(ar * hr - ai * hi + x[l])
            im.append(ar * hi + ai * hr + x[RE_TILES + l])
    return re + im


def _s5_scan_fwd(u, abar, b_blk, c_blk, d_skip, n_ctx, reverse):
    n_rows, width = u.shape
    tc = SCAN_CHUNK
    n_chunks, n_ctx_chunks = n_rows // tc, n_ctx // tc
    with_skip = d_skip is not None

    def body(*refs):
        if with_skip:
            u_ref, a_ref, b_ref, c_ref, d_ref, y_ref, hb_ref, s_ref, h_ref = refs
        else:
            u_ref, a_ref, b_ref, c_ref, y_ref, hb_ref, s_ref, h_ref = refs
        j = pl.program_id(0)

        @pl.when(j == 0)
        def _():
            h_ref[...] = jnp.zeros_like(h_ref)

        hb_ref[0] = h_ref[...]
        for q in range(N_BLOCKS):
            _scatter_steps(s_ref, q, _dot(u_ref[:, q * 128:(q + 1) * 128], b_ref[q]))
        a = _tiles(a_ref[0]) + _tiles(a_ref[1])

        def step(s, h):
            t = tc - 1 - s if reverse else s
            h = _cmul_add(a, h, _load_step(s_ref, t), conj=False)
            _store_step(s_ref, t, h)
            return h

        h = lax.fori_loop(0, tc, step, _tiles(h_ref[...]))
        h_ref[...] = jnp.concatenate(h, axis=1)
        for q in range(N_BLOCKS):
            yq = _dot(_gather_steps(s_ref, q, tc), c_ref[q])
            if with_skip:
                yq = yq + d_ref[:, q * 128:(q + 1) * 128] * u_ref[:, q * 128:(q + 1) * 128]
            y_ref[:, q * 128:(q + 1) * 128] = yq

    chunk = functools.partial(_scan_chunk_of_step, n_chunks=n_chunks, n_ctx_chunks=n_ctx_chunks, reverse=reverse)
    full3 = lambda j: (0, 0, 0)
    in_specs = [pl.BlockSpec((tc, width), lambda j: (chunk(j), 0)),
                pl.BlockSpec((2, N_BLOCKS, HALF), full3),
                pl.BlockSpec((N_BLOCKS, 128, 2 * HALF), full3),
                pl.BlockSpec((N_BLOCKS, 2 * HALF, 128), full3)]
    args = [u, abar, b_blk.astype(BF16), c_blk.astype(BF16)]
    if with_skip:
        in_specs.append(pl.BlockSpec((1, width), lambda j: (0, 0)))
        args.append(d_skip.reshape(1, width))
    return pl.pallas_call(
        body, name="s5_scan_fwd_rev" if reverse else "s5_scan_fwd",
        grid=(n_chunks,),
        in_specs=in_specs,
        out_specs=[pl.BlockSpec((tc, width), lambda j: (chunk(j), 0)),
                   pl.BlockSpec((1, N_BLOCKS, 2 * HALF), lambda j: (chunk(j), 0, 0))],
        out_shape=[jax.ShapeDtypeStruct((n_rows, width), F32),
                   jax.ShapeDtypeStruct((n_chunks, N_BLOCKS, 2 * HALF), F32)],
        scratch_shapes=[pltpu.VMEM((LANE_TILES, tc * N_BLOCKS, 128), F32), pltpu.VMEM((N_BLOCKS, 2 * HALF), F32)],
        compiler_params=_cparams(1),
    )(*args)


def _s5_scan_bwd(u, dy, hb, abar, b_blk, c_blk, d_skip, n_ctx, reverse):
    n_rows, width = u.shape
    tc = SCAN_CHUNK
    n_chunks, n_ctx_chunks = n_rows // tc, n_ctx // tc
    with_skip = d_skip is not None

    def body(*refs):
        if with_skip:
            (u_ref, dy_ref, hb_ref, a_ref, b_ref, bt_ref, ct_ref, d_ref,
             du_ref, da_ref, db_ref, dct_ref, dd_ref, sh_ref, sg_ref, g_ref) = refs
        else:
            (u_ref, dy_ref, hb_ref, a_ref, b_ref, bt_ref, ct_ref,
             du_ref, da_ref, db_ref, dct_ref, sh_ref, sg_ref, g_ref) = refs
        j = pl.program_id(0)

        @pl.when(j == 0)
        def _():
            g_ref[...] = jnp.zeros_like(g_ref)
            da_ref[...] = jnp.zeros_like(da_ref)
            db_ref[...] = jnp.zeros_like(db_ref)
            dct_ref[...] = jnp.zeros_like(dct_ref)
            if with_skip:
                dd_ref[...] = jnp.zeros_like(dd_ref)

        for q in range(N_BLOCKS):
            _scatter_steps(sh_ref, q, _dot(u_ref[:, q * 128:(q + 1) * 128], b_ref[q]))
            _scatter_steps(sg_ref, q, _dot(dy_ref[:, q * 128:(q + 1) * 128], ct_ref[q]))
        a = _tiles(a_ref[0]) + _tiles(a_ref[1])
        time_of = (lambda s: tc - 1 - s) if reverse else (lambda s: s)

        def fwd_step(s, h):
            h = _cmul_add(a, h, _load_step(sh_ref, time_of(s)), conj=False)
            _store_step(sh_ref, time_of(s), h)
            return h

        h0 = _tiles(hb_ref[0])
        lax.fori_loop(0, tc, fwd_step, h0)

        def adj(t, h_prev, carry):
            g, da = carry
            g = _cmul_add(a, g, _load_step(sg_ref, t), conj=True)
            _store_step(sg_ref, t, g)
            da_re = [da[l] + g[l] * h_prev[l] + g[RE_TILES + l] * h_prev[RE_TILES + l] for l in range(RE_TILES)]
            da_im = [da[RE_TILES + l] + g[RE_TILES + l] * h_prev[l] - g[l] * h_prev[RE_TILES + l] for l in range(RE_TILES)]
            return g, da_re + da_im

        def bwd_step(i, carry):
            s = tc - 1 - i
            return adj(time_of(s), _load_step(sh_ref, time_of(s - 1)), carry)

        carry = (_tiles(g_ref[...]), _tiles(da_ref[0]) + _tiles(da_ref[1]))
        carry = lax.fori_loop(0, tc - 1, bwd_step, carry)
        g, da = adj(time_of(0), h0, carry)
        g_ref[...] = jnp.concatenate(g, axis=1)
        da_ref[0] = jnp.concatenate(da[:RE_TILES], axis=1)
        da_ref[1] = jnp.concatenate(da[RE_TILES:], axis=1)

        for q in range(N_BLOCKS):
            cols = slice(q * 128, (q + 1) * 128)
            uq, dyq = u_ref[:, cols], dy_ref[:, cols]
            gq = _gather_steps(sg_ref, q, tc)
            duq = _dot(gq, bt_ref[q])
            if with_skip:
                duq = duq + d_ref[:, cols] * dyq
                dd_ref[:, cols] += jnp.sum(dyq * uq, axis=0, keepdims=True)
            du_ref[:, cols] = duq
            db_ref[q] += _dot_t0(uq, gq)
            dct_ref[q] += _dot_t0(dyq, _gather_steps(sh_ref, q, tc))

    def chunk(j):
        return _scan_chunk_of_step(n_chunks - 1 - j, n_chunks, n_ctx_chunks, reverse)

    full2 = lambda j: (0, 0)
    full3 = lambda j: (0, 0, 0)
    row = pl.BlockSpec((tc, width), lambda j: (chunk(j), 0))
    in_specs = [row, row,
                pl.BlockSpec((1, N_BLOCKS, 2 * HALF), lambda j: (chunk(j), 0, 0)),
                pl.BlockSpec((2, N_BLOCKS, HALF), full3),
                pl.BlockSpec((N_BLOCKS, 128, 2 * HALF), full3),
                pl.BlockSpec((N_BLOCKS, 2 * HALF, 128), full3),
                pl.BlockSpec((N_BLOCKS, 128, 2 * HALF), full3)]
    args = [u, dy, hb, abar, b_blk.astype(BF16), jnp.swapaxes(b_blk, 1, 2).astype(BF16),
            jnp.swapaxes(c_blk, 1, 2).astype(BF16)]
    out_specs = [row,
                 pl.BlockSpec((2, N_BLOCKS, HALF), full3),
                 pl.BlockSpec((N_BLOCKS, 128, 2 * HALF), full3),
                 pl.BlockSpec((N_BLOCKS, 128, 2 * HALF), full3)]
    out_shape = [jax.ShapeDtypeStruct((n_rows, width), F32),
                 jax.ShapeDtypeStruct((2, N_BLOCKS, HALF), F32),
                 jax.ShapeDtypeStruct((N_BLOCKS, 128, 2 * HALF), F32),
                 jax.ShapeDtypeStruct((N_BLOCKS, 128, 2 * HALF), F32)]
    if with_skip:
        in_specs.append(pl.BlockSpec((1, width), full2))
        args.append(d_skip.reshape(1, width))
        out_specs.append(pl.BlockSpec((1, width), full2))
        out_shape.append(jax.ShapeDtypeStruct((1, width), F32))
    return pl.pallas_call(
        body, name="s5_scan_bwd_rev" if reverse else "s5_scan_bwd",
        grid=(n_chunks,),
        in_specs=in_specs, out_specs=out_specs, out_shape=out_shape,
        scratch_shapes=[pltpu.VMEM((LANE_TILES, tc * N_BLOCKS, 128), F32), pltpu.VMEM((LANE_TILES, tc * N_BLOCKS, 128), F32),
                        pltpu.VMEM((N_BLOCKS, 2 * HALF), F32)],
        compiler_params=_cparams(1),
    )(*args)


def _s5_dir_params(d, a_re, a_im, log_dt, b_re, b_im):
    return a_re[d], a_im[d], log_dt[d], b_re[d], b_im[d]


def _s5_forward(u, ssm, n_ctx):
    a_re, a_im, log_dt, b_re, b_im, c_re, c_im, d_skip = ssm
    outs, saved = [], []
    for d in range(2):
        prep = _s5_prep(*_s5_dir_params(d, a_re, a_im, log_dt, b_re, b_im))
        abar, b_blk, c_blk = _s5_blocks(*prep, c_re[d], c_im[d])
        y, hb = _s5_scan_fwd(u, abar, b_blk, c_blk, d_skip if d == 0 else None, n_ctx, reverse=(d == 1))
        outs.append(y)
        saved.append((hb, abar, b_blk, c_blk))
    return outs, saved


def _s5_backward(u, dy, ssm, saved, n_ctx):
    a_re, a_im, log_dt, b_re, b_im, c_re, c_im, d_skip = ssm
    dus, grads = [], [[] for _ in range(7)]
    d_d = None
    for d in range(2):
        hb, abar, b_blk, c_blk = saved[d]
        res = _s5_scan_bwd(u, dy, hb, abar, b_blk, c_blk, d_skip if d == 0 else None, n_ctx, reverse=(d == 1))
        if d == 0:
            du, d_abar, d_b_blk, d_ct_blk, d_d = res
        else:
            du, d_abar, d_b_blk, d_ct_blk = res
        dus.append(du)
        dbb_re, dbb_im, dc_re, dc_im = _s5_unblock(d_b_blk, d_ct_blk)
        _, vjp = jax.vjp(_s5_prep, *_s5_dir_params(d, a_re, a_im, log_dt, b_re, b_im))
        shape = (SSM_GROUPS, SSM_STATE)
        g5 = vjp((d_abar[0].reshape(shape), d_abar[1].reshape(shape), dbb_re, dbb_im))
        for k, g in enumerate(tuple(g5) + (dc_re, dc_im)):
            grads[k].append(g)
    grads = [jnp.stack(g) for g in grads]
    return dus, grads + [d_d.reshape(-1)]


INV_SQRT2 = 0.7071067811865476
INV_SQRT_2PI = 0.3989422804014327


def _rows(cols):
    return pl.BlockSpec((ROW_TILE, cols), lambda i: (i, 0))


def _rows_skip_ctx(cols):
    return pl.BlockSpec((ROW_TILE, cols), lambda i: (i + 1, 0))


def _rows_lat(cols):
    return pl.BlockSpec((ROW_TILE, cols), lambda i: (jnp.maximum(i - 1, 0), 0))


def _full(shape):
    nd = len(shape)
    return pl.BlockSpec(shape, lambda i: (0,) * nd)


def _seg(cols):
    return pl.BlockSpec((1, 1, cols), lambda i: (jnp.minimum(i, 1), 0, 0))


def _lat_seg(cols):
    return pl.BlockSpec((1, 1, cols), lambda i: (1, 0, 0))


def _sds(shape, dtype=F32):
    return jax.ShapeDtypeStruct(shape, dtype)


def _sum0(x):
    return jnp.sum(x, axis=0, keepdims=True)


def _sigmoid(x):
    return jax.nn.sigmoid(x)


def _rms_mod(x, g, scale, shift):
    r = lax.rsqrt(jnp.mean(x * x, axis=-1, keepdims=True) + NORM_EPS)
    return (x * r * g) * (1.0 + scale) + shift


def _rms_mod_bwd(x, g, scale, dh):
    r = lax.rsqrt(jnp.mean(x * x, axis=-1, keepdims=True) + NORM_EPS)
    n = x * r
    dyg = dh * (1.0 + scale)
    dn = dyg * g
    dx = r * (dn - n * jnp.mean(dn * n, axis=-1, keepdims=True))
    return dx, _sum0(dyg * n), _sum0(dh * (n * g)), _sum0(dh)


def _seg_sum(t, seg):
    n = t.shape[1]
    lane = lax.broadcasted_iota(jnp.int32, t.shape, 1)
    s = seg // 2
    while s >= 1:
        t = t + jnp.where((lane & s) == 0, pltpu.roll(t, n - s, 1), pltpu.roll(t, s, 1))
        s //= 2
    return t


def _rope_partner(x):
    n = x.shape[1]
    lane = lax.broadcasted_iota(jnp.int32, x.shape, 1)
    return jnp.where((lane & 16) == 0, pltpu.roll(x, n - 16, 1), pltpu.roll(x, 16, 1))


def _lanes(tab, width):
    return jnp.tile(tab, (1, width // tab.shape[1]))


def _head_norm_rope(x, gain, cos, sin):
    r = lax.rsqrt(_seg_sum(x * x, HEAD_DIM) * (1.0 / HEAD_DIM) + NORM_EPS)
    y = x * r * gain
    return y * cos + _rope_partner(y) * sin


def _head_norm_rope_bwd(x, gain, cos, sin, dout):
    dy = dout * cos + _rope_partner(dout * sin)
    r = lax.rsqrt(_seg_sum(x * x, HEAD_DIM) * (1.0 / HEAD_DIM) + NORM_EPS)
    n = x * r
    dn = dy * gain
    dx = r * (dn - n * (_seg_sum(dn * n, HEAD_DIM) * (1.0 / HEAD_DIM)))
    return dx, _sum0(dy * n)


def _rope_tables(n_ctx, n_lat):
    t = jnp.arange(n_lat)
    pos = jnp.stack([(t // GRID_W).astype(F32), (t % GRID_W).astype(F32)], axis=1)
    n_freq = HEAD_DIM // 4
    freqs = ROPE_THETA ** (-jnp.arange(n_freq, dtype=F32) / n_freq)
    ang = pos[:, :, None] * freqs[None, None, :]
    cos = jnp.repeat(jnp.cos(ang)[:, :, None, :], 2, axis=2).reshape(n_lat, HEAD_DIM)
    sin = jnp.sin(ang)
    sin = jnp.stack([-sin, sin], axis=2).reshape(n_lat, HEAD_DIM)
    cos = jnp.concatenate([jnp.ones((n_ctx, HEAD_DIM), F32), cos], axis=0)
    sin = jnp.concatenate([jnp.zeros((n_ctx, HEAD_DIM), F32), sin], axis=0)
    return jnp.tile(cos, (1, 2)), jnp.tile(sin, (1, 2))


def _ssm_in(xa, g, scale, shift, w_in):
    n_rows, d = xa.shape
    e = w_in.shape[1] // 2

    def body(x_ref, g_ref, sc_ref, sh_ref, w_ref, u_ref, z_ref):
        h = _rms_mod(x_ref[...], g_ref[...], sc_ref[0], sh_ref[0])
        proj = _dot(h, w_ref[...])
        u_ref[...] = proj[:, :e]
        z_ref[...] = proj[:, e:]

    return pl.pallas_call(
        body, name="ssm_in", grid=(n_rows // ROW_TILE,),
        in_specs=[_rows(d), _full((1, d)), _seg(d), _seg(d), _full(w_in.shape)],
        out_specs=[_rows(e), _rows(e)], out_shape=[_sds((n_rows, e)), _sds((n_rows, e))],
        compiler_params=_cparams(1),
    )(xa, g, scale, shift, w_in)


def _s5_post_math(y, z, w_glu, b_glu, w_out):
    er = lax.erf(y * INV_SQRT2)
    g = 0.5 * y * (1.0 + er)
    sg = _sigmoid(_dot(g, w_glu) + b_glu)
    g2 = g * sg
    sz = _sigmoid(z)
    silu_z = z * sz
    m = g2 * silu_z
    return er, g, sg, g2, sz, silu_z, m, _dot(m, w_out)


def _ssm_post(xa, y0, y1, z, gate, w_glu, b_glu, w_out):
    n_rows, d = xa.shape
    e = z.shape[1]

    def body(x_ref, y0_ref, y1_ref, z_ref, gt_ref, wg_ref, bg_ref, wo_ref, o_ref):
        out = _s5_post_math(y0_ref[...] + y1_ref[...], z_ref[...], wg_ref[...], bg_ref[...], wo_ref[...])[-1]
        o_ref[...] = x_ref[...] + gt_ref[0] * out

    return pl.pallas_call(
        body, name="ssm_post", grid=(n_rows // ROW_TILE,),
        in_specs=[_rows(d), _rows(e), _rows(e), _rows(e), _seg(d), _full(w_glu.shape), _full((1, e)), _full(w_out.shape)],
        out_specs=_rows(d), out_shape=_sds((n_rows, d)),
        compiler_params=_cparams(1),
    )(xa, y0, y1, z, gate, w_glu, b_glu, w_out)


def _init_acc(first, *refs):
    @pl.when(first)
    def _():
        for r in refs:
            r[...] = jnp.zeros_like(r)


def _ssm_post_bwd(dxa, y0, y1, z, gate, w_glu, b_glu, w_out, w_glu_t, w_out_t):
    n_rows, d = dxa.shape
    e = z.shape[1]

    def body(dx_ref, y0_ref, y1_ref, z_ref, gt_ref, wg_ref, bg_ref, wo_ref, wgt_ref, wot_ref,
             dy_ref, dz_ref, dgt_ref, dwo_ref, dwg_ref, dbg_ref):
        i = pl.program_id(0)
        _init_acc(i == 0, dwo_ref, dwg_ref, dbg_ref)
        _init_acc(i <= 1, dgt_ref)
        y, zz = y0_ref[...] + y1_ref[...], z_ref[...]
        er, g, sg, g2, sz, silu_z, m, out = _s5_post_math(y, zz, wg_ref[...], bg_ref[...], wo_ref[...])
        dxa_t = dx_ref[...]
        dgt_ref[0] += _sum0(dxa_t * out)
        dout = gt_ref[0] * dxa_t
        dm = _dot(dout, wot_ref[...])
        dwo_ref[...] += _dot_t0(m, dout)
        dg2 = dm * silu_z
        dz_ref[...] = dm * g2 * (sz * (1.0 + zz * (1.0 - sz)))
        dt = dg2 * g * sg * (1.0 - sg)
        dwg_ref[...] += _dot_t0(g, dt)
        dbg_ref[...] += _sum0(dt)
        dg = dg2 * sg + _dot(dt, wgt_ref[...])
        dy_ref[...] = dg * (0.5 * (1.0 + er) + y * jnp.exp(-0.5 * y * y) * INV_SQRT_2PI)

    return pl.pallas_call(
        body, name="ssm_post_bwd", grid=(n_rows // ROW_TILE,),
        in_specs=[_rows(d), _rows(e), _rows(e), _rows(e), _seg(d), _full(w_glu.shape), _full((1, e)), _full(w_out.shape),
                  _full(w_glu_t.shape), _full(w_out_t.shape)],
        out_specs=[_rows(e), _rows(e), _seg(d), _full(w_out.shape), _full(w_glu.shape), _full((1, e))],
        out_shape=[_sds((n_rows, e)), _sds((n_rows, e)), _sds((2, 1, d)), _sds(w_out.shape), _sds(w_glu.shape), _sds((1, e))],
        compiler_params=_cparams(1),
    )(dxa, y0, y1, z, gate, w_glu, b_glu, w_out, w_glu_t, w_out_t)


def _ssm_in_bwd(du0, du1, dz, xa, dxa_next, g, scale, shift, w_in_t):
    n_rows, d = xa.shape
    e = dz.shape[1]
    n_lat = n_rows - ROW_TILE

    def body(du0_ref, du1_ref, dz_ref, x_ref, dn_ref, g_ref, sc_ref, sh_ref, wt_ref,
             gx_ref, dw_ref, dg_ref, dsc_ref, dsh_ref):
        i = pl.program_id(0)
        _init_acc(i == 0, dw_ref, dg_ref)
        _init_acc(i <= 1, dsc_ref, dsh_ref)
        x = x_ref[...]
        h = _rms_mod(x, g_ref[...], sc_ref[0], sh_ref[0])
        dproj = jnp.concatenate([du0_ref[...] + du1_ref[...], dz_ref[...]], axis=1)
        dh = _dot(dproj, wt_ref[...])
        dw_ref[...] += _dot_t0(h, dproj)
        dx, dg, dsc, dsh = _rms_mod_bwd(x, g_ref[...], sc_ref[0], dh)
        dg_ref[...] += dg
        dsc_ref[0] += dsc
        dsh_ref[0] += dsh
        gx_ref[...] = dn_ref[...] + dx

    return pl.pallas_call(
        body, name="ssm_in_bwd", grid=(n_rows // ROW_TILE,),
        in_specs=[_rows(e), _rows(e), _rows(e), _rows(d), _rows(d), _full((1, d)), _seg(d), _seg(d), _full(w_in_t.shape)],
        out_specs=[_rows_lat(d), _full((d, 2 * e)), _full((1, d)), _seg(d), _seg(d)],
        out_shape=[_sds((n_lat, d)), _sds((d, 2 * e)), _sds((1, d)), _sds((2, 1, d)), _sds((2, 1, d))],
        compiler_params=_cparams(1),
    )(du0, du1, dz, xa, dxa_next, g, scale, shift, w_in_t)


Q_WIDTH = N_Q_HEADS * HEAD_DIM
SM_SCALE = 1.0 / math.sqrt(HEAD_DIM)


def _attn_in(xa, g, scale, shift, w_in, q_gain, k_gain, cos, sin):
    n_rows, d = xa.shape
    qk = Q_WIDTH + KV_WIDTH

    def body(x_ref, g_ref, sc_ref, sh_ref, w_ref, qg_ref, kg_ref, cos_ref, sin_ref, q_ref, k_ref, v_ref, z_ref, raw_ref):
        h = _rms_mod(x_ref[...], g_ref[...], sc_ref[0], sh_ref[0])
        proj = _dot(h, w_ref[...])
        q_raw, k_raw = proj[:, :Q_WIDTH], proj[:, Q_WIDTH:qk]
        cos, sin = cos_ref[...], sin_ref[...]
        q = _head_norm_rope(q_raw, qg_ref[...], _lanes(cos, Q_WIDTH), _lanes(sin, Q_WIDTH))
        k = _head_norm_rope(k_raw, kg_ref[...], _lanes(cos, KV_WIDTH), _lanes(sin, KV_WIDTH))
        q_ref[...] = (q * SM_SCALE).astype(BF16)
        k_ref[...] = k.astype(BF16)
        v_ref[...] = proj[:, qk:qk + KV_WIDTH].astype(BF16)
        z_ref[...] = proj[:, qk + KV_WIDTH:]
        raw_ref[...] = proj[:, :qk]

    return pl.pallas_call(
        body, name="attn_in", grid=(n_rows // ROW_TILE,),
        in_specs=[_rows(d), _full((1, d)), _seg(d), _seg(d), _full(w_in.shape), _full((1, Q_WIDTH)), _full((1, KV_WIDTH)),
                  _rows(128), _rows(128)],
        out_specs=[_rows(Q_WIDTH), _rows(KV_WIDTH), _rows(KV_WIDTH), _rows(Q_WIDTH), _rows(qk)],
        out_shape=[_sds((n_rows, Q_WIDTH), BF16), _sds((n_rows, KV_WIDTH), BF16), _sds((n_rows, KV_WIDTH), BF16),
                   _sds((n_rows, Q_WIDTH)), _sds((n_rows, qk))],
        compiler_params=_cparams(1),
    )(xa, g, scale, shift, w_in, q_gain, k_gain, cos, sin)


def _kv_tile(n_keys):
    return 768 if n_keys % 768 == 0 else 256


def _q_tile(n_lat):
    return 512 if n_lat % 512 == 0 else 256


def _flash_fwd(q, k, v_t):
    n_keys, n_lat = k.shape[1], q.shape[1]
    tq, tk = _q_tile(n_lat), _kv_tile(n_keys)
    rows = KV_REP * tq
    n_kv, n_q = n_keys // tk, n_lat // tq

    def body(q_ref, k_ref, vt_ref, o_ref, lse_ref, m_ref, l_ref, acc_ref):
        kj = pl.program_id(2)

        @pl.when(kj == 0)
        def _():
            m_ref[...] = jnp.full_like(m_ref, -jnp.inf)
            l_ref[...] = jnp.zeros_like(l_ref)
            acc_ref[...] = jnp.zeros_like(acc_ref)

        s_t = _dot_t1(k_ref[0], q_ref[...].reshape(rows, HEAD_DIM))
        m_prev = m_ref[...]
        m_new = jnp.maximum(m_prev, jnp.max(s_t, axis=0, keepdims=True))
        alpha = jnp.exp(m_prev - m_new)
        p_t = jnp.exp(s_t - m_new)
        l_ref[...] = alpha * l_ref[...] + jnp.sum(p_t, axis=0, keepdims=True)
        acc_ref[...] = alpha * acc_ref[...] + _dot(vt_ref[0], p_t)
        m_ref[...] = m_new

        @pl.when(kj == n_kv - 1)
        def _():
            l = l_ref[...]
            o_ref[0, 0] = acc_ref[...] / l
            lse_ref[0, 0] = m_ref[...] + jnp.log(l)

    return pl.pallas_call(
        body, name="flash_fwd", grid=(N_KV_HEADS, n_q, n_kv),
        in_specs=[pl.BlockSpec((KV_REP, tq, HEAD_DIM), lambda g, i, j: (g, i, 0)),
                  pl.BlockSpec((1, tk, HEAD_DIM), lambda g, i, j: (g, j, 0)),
                  pl.BlockSpec((1, HEAD_DIM, tk), lambda g, i, j: (g, 0, j))],
        out_specs=[pl.BlockSpec((1, 1, HEAD_DIM, rows), lambda g, i, j: (g, i, 0, 0)),
                   pl.BlockSpec((1, 1, 1, rows), lambda g, i, j: (g, i, 0, 0))],
        out_shape=[_sds((N_KV_HEADS, n_q, HEAD_DIM, rows)), _sds((N_KV_HEADS, n_q, 1, rows))],
        scratch_shapes=[pltpu.VMEM((1, rows), F32), pltpu.VMEM((1, rows), F32), pltpu.VMEM((HEAD_DIM, rows), F32)],
        compiler_params=_cparams(3),
    )(q, k, v_t)


def _flash_bwd(q, k, k_t, v, do, lse_t, delta_t):
    n_keys, n_lat = k.shape[1], q.shape[1]
    tq, tk = _q_tile(n_lat), _kv_tile(n_keys)
    rows = KV_REP * tq
    n_kv, n_q = n_keys // tk, n_lat // tq

    def body(q_ref, k_ref, kt_ref, v_ref, do_ref, lse_ref, dl_ref, dq_ref, dk_ref, dv_ref, dq_acc):
        qi, kj = pl.program_id(1), pl.program_id(2)
        _init_acc(kj == 0, dq_acc)
        q = q_ref[...].reshape(rows, HEAD_DIM)
        do = do_ref[...].reshape(rows, HEAD_DIM)
        p_t = jnp.exp(_dot_t1(k_ref[0], q) - lse_ref[0, 0])
        dv = _dot(p_t, do)
        ds_t = p_t * (_dot_t1(v_ref[0], do) - dl_ref[0, 0])
        dk = _dot(ds_t, q)
        dq_acc[...] += _dot(kt_ref[0], ds_t)
        keys = pl.ds(pl.multiple_of(kj * tk, tk), tk)

        @pl.when(qi == 0)
        def _():
            dk_ref[0, keys, :] = dk
            dv_ref[0, keys, :] = dv

        @pl.when(qi > 0)
        def _():
            dk_ref[0, keys, :] += dk
            dv_ref[0, keys, :] += dv

        @pl.when(kj == n_kv - 1)
        def _():
            dq_ref[0, 0] = dq_acc[...]

    qspec = pl.BlockSpec((KV_REP, tq, HEAD_DIM), lambda g, i, j: (g, i, 0))
    kspec = pl.BlockSpec((1, tk, HEAD_DIM), lambda g, i, j: (g, j, 0))
    rowspec = pl.BlockSpec((1, 1, 1, rows), lambda g, i, j: (g, i, 0, 0))
    kv_all = pl.BlockSpec((1, n_keys, HEAD_DIM), lambda g, i, j: (g, 0, 0))
    return pl.pallas_call(
        body, name="flash_bwd", grid=(N_KV_HEADS, n_q, n_kv),
        in_specs=[qspec, kspec, pl.BlockSpec((1, HEAD_DIM, tk), lambda g, i, j: (g, 0, j)), kspec, qspec, rowspec, rowspec],
        out_specs=[pl.BlockSpec((1, 1, HEAD_DIM, rows), lambda g, i, j: (g, i, 0, 0)), kv_all, kv_all],
        out_shape=[_sds((N_KV_HEADS, n_q, HEAD_DIM, rows)), _sds((N_KV_HEADS, n_keys, HEAD_DIM)),
                   _sds((N_KV_HEADS, n_keys, HEAD_DIM))],
        scratch_shapes=[pltpu.VMEM((HEAD_DIM, rows), F32)],
        compiler_params=_cparams(3),
    )(q, k, k_t, v, do, lse_t, delta_t)


def _from_lane_stacked(a_t, tq):
    n_g, n_q, width, _ = a_t.shape
    a = a_t.reshape(n_g, n_q, width, KV_REP, tq).transpose(1, 4, 0, 3, 2)
    return a.reshape(n_q * tq, n_g * KV_REP * width)


def _to_lane_stacked(a, tq):
    n_lat = a.shape[0]
    a = a.reshape(n_lat // tq, tq, N_KV_HEADS, KV_REP).transpose(2, 0, 3, 1)
    return a.reshape(N_KV_HEADS, n_lat // tq, 1, KV_REP * tq)


def _attn_post_loss(o, z, xa, gate, w_out, w_out_t, final_g, target):
    n_lat, d = target.shape
    e = o.shape[1]
    head_of_lane = (jnp.arange(e)[:, None] // HEAD_DIM == jnp.arange(128)[None, :]).astype(BF16)

    def body(o_ref, z_ref, x_ref, gt_ref, w_ref, wt_ref, fg_ref, tg_ref, hl_ref,
             do_ref, dl_ref, dz_ref, dx_ref, loss_ref, dfg_ref, dgt_ref, dw_ref):
        _init_acc(pl.program_id(0) == 0, loss_ref, dfg_ref, dgt_ref, dw_ref)
        oo, zz, gate_t, fg = o_ref[...], z_ref[...], gt_ref[0], fg_ref[...]
        sz = _sigmoid(zz)
        silu_z = zz * sz
        m = oo * silu_z
        out = _dot(m, w_ref[...])
        x2 = x_ref[...] + gate_t * out
        r = lax.rsqrt(jnp.mean(x2 * x2, axis=-1, keepdims=True) + NORM_EPS)
        n = x2 * r
        err = n * fg - tg_ref[...]
        loss_ref[...] += 0.5 * jnp.sum(jnp.mean(err * err, axis=-1, keepdims=True), axis=0, keepdims=True)
        dy = err * (1.0 / d)
        dfg_ref[...] += _sum0(dy * n)
        dn = dy * fg
        dx2 = r * (dn - n * jnp.mean(dn * n, axis=-1, keepdims=True))
        dx_ref[...] = dx2
        dgt_ref[...] += _sum0(dx2 * out)
        dout = gate_t * dx2
        dw_ref[...] += _dot_t0(m, dout)
        dm = _dot(dout, wt_ref[...])
        do = dm * silu_z
        do_ref[...] = do.astype(BF16)
        prod = do * oo
        hi = prod.astype(BF16)
        lo = (prod - hi.astype(F32)).astype(BF16)
        dl_ref[...] = _dot(hi, hl_ref[...]) + _dot(lo, hl_ref[...])
        dz_ref[...] = dm * oo * (sz * (1.0 + zz * (1.0 - sz)))

    return pl.pallas_call(
        body, name="attn_post_loss", grid=(n_lat // ROW_TILE,),
        in_specs=[_rows(e), _rows_skip_ctx(e), _rows_skip_ctx(d), _lat_seg(d), _full(w_out.shape), _full(w_out_t.shape),
                  _full((1, d)), _rows(d), _full((e, 128))],
        out_specs=[_rows(e), _rows(128), _rows(e), _rows(d), _full((1, 1)), _full((1, d)), _full((1, d)), _full(w_out.shape)],
        out_shape=[_sds((n_lat, e), BF16), _sds((n_lat, 128)), _sds((n_lat, e)), _sds((n_lat, d)), _sds((1, 1)), _sds((1, d)),
                   _sds((1, d)), _sds(w_out.shape)],
        compiler_params=_cparams(1),
    )(o, z, xa, gate, w_out, w_out_t, final_g, target, head_of_lane)


def _attn_in_bwd(dq, dk, dv, dz, raw, xa, dx2, g, scale, shift, q_gain, k_gain, cos, sin, w_in_t):
    n_rows, d = xa.shape
    qk = Q_WIDTH + KV_WIDTH
    n_in = w_in_t.shape[0]

    def body(dq_ref, dk_ref, dv_ref, dz_ref, raw_ref, x_ref, dx2_ref, g_ref, sc_ref, sh_ref, qg_ref, kg_ref, cos_ref, sin_ref,
             wt_ref, dxa_ref, dw_ref, dqg_ref, dkg_ref, dg_ref, dsc_ref, dsh_ref):
        i = pl.program_id(0)
        _init_acc(i == 0, dw_ref, dqg_ref, dkg_ref, dg_ref)
        _init_acc(i <= 1, dsc_ref, dsh_ref)
        is_lat = (i > 0).astype(F32)
        x = x_ref[...]
        h = _rms_mod(x, g_ref[...], sc_ref[0], sh_ref[0])
        cos, sin = cos_ref[...], sin_ref[...]
        raw_t = raw_ref[...]
        dq_raw, dqg = _head_norm_rope_bwd(raw_t[:, :Q_WIDTH], qg_ref[...], _lanes(cos, Q_WIDTH), _lanes(sin, Q_WIDTH),
                                          dq_ref[...] * (SM_SCALE * is_lat))
        dk_raw, dkg = _head_norm_rope_bwd(raw_t[:, Q_WIDTH:], kg_ref[...], _lanes(cos, KV_WIDTH), _lanes(sin, KV_WIDTH),
                                          dk_ref[...])
        dqg_ref[...] += dqg
        dkg_ref[...] += dkg
        dproj = jnp.concatenate([dq_raw, dk_raw, dv_ref[...], dz_ref[...] * is_lat], axis=1)
        dh = _dot(dproj, wt_ref[...])
        dw_ref[...] += _dot_t0(h, dproj)
        dx, dg, dsc, dsh = _rms_mod_bwd(x, g_ref[...], sc_ref[0], dh)
        dg_ref[...] += dg
        dsc_ref[0] += dsc
        dsh_ref[0] += dsh
        dxa_ref[...] = dx + dx2_ref[...] * is_lat

    return pl.pallas_call(
        body, name="attn_in_bwd", grid=(n_rows // ROW_TILE,),
        in_specs=[_rows_lat(Q_WIDTH), _rows(KV_WIDTH), _rows(KV_WIDTH), _rows_lat(Q_WIDTH), _rows(qk), _rows(d), _rows_lat(d),
                  _full((1, d)), _seg(d), _seg(d), _full((1, Q_WIDTH)), _full((1, KV_WIDTH)), _rows(128), _rows(128),
                  _full(w_in_t.shape)],
        out_specs=[_rows(d), _full((d, n_in)), _full((1, Q_WIDTH)), _full((1, KV_WIDTH)), _full((1, d)), _seg(d), _seg(d)],
        out_shape=[_sds((n_rows, d)), _sds((d, n_in)), _sds((1, Q_WIDTH)), _sds((1, KV_WIDTH)), _sds((1, d)),
                   _sds((2, 1, d)), _sds((2, 1, d))],
        compiler_params=_cparams(1),
    )(dq, dk, dv, dz, raw, xa, dx2, g, scale, shift, q_gain, k_gain, cos, sin, w_in_t)


def _heads_major(a, n_heads):
    return a.reshape(a.shape[0], n_heads, HEAD_DIM).transpose(1, 0, 2)


def _tokens_major(a):
    return a.transpose(1, 0, 2).reshape(a.shape[1], a.shape[0] * HEAD_DIM)


def _local_step(x, ctx, target, mods, norm_g, ssm, w_ssm_in, w_glu, b_glu, w_ssm_out, w_attn_in, q_norm, k_norm, w_attn_out,
                final_g):
    n_ctx, d = ctx.shape
    assert n_ctx == ROW_TILE
    n_lat = x.shape[0]
    (shift0, scale0, gate0), (shift1, scale1, gate1) = mods
    g0, g1, fg = norm_g[0:1], norm_g[1:2], final_g.reshape(1, d)
    b_glu = b_glu.reshape(1, -1)
    q_gain = jnp.tile(q_norm.reshape(1, HEAD_DIM), (1, N_Q_HEADS))
    k_gain = jnp.tile(k_norm.reshape(1, HEAD_DIM), (1, N_KV_HEADS))
    cos, sin = _rope_tables(n_ctx, n_lat)

    xa0 = jnp.concatenate([ctx, x], axis=0)
    u, z0 = _ssm_in(xa0, g0, scale0, shift0, w_ssm_in)
    (y0, y1), saved = _s5_forward(u, ssm, n_ctx)
    xa1 = _ssm_post(xa0, y0, y1, z0, gate0, w_glu, b_glu, w_ssm_out)

    q, k, v, z1, raw = _attn_in(xa1, g1, scale1, shift1, w_attn_in, q_gain, k_gain, cos, sin)
    q_h, k_h, v_h = _heads_major(q[n_ctx:], N_Q_HEADS), _heads_major(k, N_KV_HEADS), _heads_major(v, N_KV_HEADS)
    tq = _q_tile(n_lat)
    o_t, lse_t = _flash_fwd(q_h, k_h, v_h.transpose(0, 2, 1))
    do, delta, dz1, dx2, loss, d_fg, d_gate1, d_w_attn_out = _attn_post_loss(
        _from_lane_stacked(o_t, tq), z1, xa1, gate1, w_attn_out, w_attn_out.T, fg, target)

    dq_t, dk_h, dv_h = _flash_bwd(q_h, k_h, k_h.transpose(0, 2, 1), v_h, _heads_major(do, N_Q_HEADS), lse_t,
                                  _to_lane_stacked(delta[:, :N_Q_HEADS], tq))
    dxa1, d_w_attn_in, d_qg, d_kg, d_g1, d_scale1, d_shift1 = _attn_in_bwd(
        _from_lane_stacked(dq_t, tq), _tokens_major(dk_h), _tokens_major(dv_h), dz1, raw, xa1, dx2, g1, scale1, shift1,
        q_gain, k_gain, cos, sin, w_attn_in.T)
    dy, dz0, d_gate0, d_w_ssm_out, d_w_glu, d_b_glu = _ssm_post_bwd(
        dxa1, y0, y1, z0, gate0, w_glu, b_glu, w_ssm_out, w_glu.T, w_ssm_out.T)
    (du0, du1), d_ssm = _s5_backward(u, dy, ssm, saved, n_ctx)
    grad_x, d_w_ssm_in, d_g0, d_scale0, d_shift0 = _ssm_in_bwd(du0, du1, dz0, xa0, dxa1, g0, scale0, shift0, w_ssm_in.T)

    d_gate1_seg = jnp.concatenate([jnp.zeros((1, 1, d), F32), d_gate1.reshape(1, 1, d)], axis=0)
    grads = dict(
        norm_g=jnp.concatenate([d_g0, d_g1], axis=0),
        ssm_w_in=d_w_ssm_in, ssm=d_ssm, ssm_w_glu=d_w_glu, ssm_b_glu=d_b_glu.reshape(-1), ssm_w_out=d_w_ssm_out,
        attn_w_in=d_w_attn_in,
        attn_q_norm=d_qg.reshape(N_Q_HEADS, HEAD_DIM).sum(0), attn_k_norm=d_kg.reshape(N_KV_HEADS, HEAD_DIM).sum(0),
        attn_w_out=d_w_attn_out, final_norm_g=d_fg.reshape(-1))
    d_mods = ((d_shift0, d_scale0, d_gate0), (d_shift1, d_scale1, d_gate1_seg))
    return loss[0, 0], grad_x, grads, d_mods


def _my_index():
    return 4 * lax.axis_index("x") + 2 * lax.axis_index("y") + lax.axis_index("c")


def _peer(k):
    mx, my, mc = lax.axis_index("x"), lax.axis_index("y"), lax.axis_index("c")
    px = 1 - mx if k & 4 else mx
    py = 1 - my if k & 2 else my
    pc = 1 - mc if k & 1 else mc
    return (px, py, pc), 4 * px + 2 * py + pc


HBM_SPEC = pl.BlockSpec(memory_space=pl.ANY)


def _exchange(x, name, all_to_all):
    block = x.shape[1:] if all_to_all else x.shape

    def body(x_ref, out_ref, send_sems, recv_sems, local_sem):
        me = _my_index()
        mine = pltpu.make_async_copy(x_ref.at[me] if all_to_all else x_ref, out_ref.at[me], local_sem)
        mine.start()
        sends = []
        for k in range(1, N_DEV):
            peer, peer_idx = _peer(k)
            cp = pltpu.make_async_remote_copy(
                src_ref=x_ref.at[peer_idx] if all_to_all else x_ref, dst_ref=out_ref.at[me],
                send_sem=send_sems.at[k - 1], recv_sem=recv_sems.at[k - 1], device_id=peer, device_id_type=MESH_IDS)
            cp.start()
            sends.append(cp)
        for k in range(1, N_DEV):
            peer, peer_idx = _peer(k)
            pltpu.make_async_remote_copy(
                src_ref=x_ref.at[me] if all_to_all else x_ref, dst_ref=out_ref.at[peer_idx],
                send_sem=send_sems.at[k - 1], recv_sem=recv_sems.at[k - 1], device_id=peer, device_id_type=MESH_IDS).wait_recv()
        for cp in sends:
            cp.wait_send()
        mine.wait()

    return pl.pallas_call(
        body, name=name, in_specs=[HBM_SPEC], out_specs=HBM_SPEC,
        out_shape=_sds((N_DEV,) + tuple(block), x.dtype),
        scratch_shapes=[pltpu.SemaphoreType.DMA((N_DEV - 1,)), pltpu.SemaphoreType.DMA((N_DEV - 1,)), pltpu.SemaphoreType.DMA],
    )(x)


MOD_ROWS = 16
CTX_ROW = N_DEV


def _mod_fwd(cond, w_shard, b_cols):
    n_layers, d, cols = w_shard.shape

    def body(c_ref, w_ref, b_ref, o_ref):
        c = c_ref[...]
        s = c * _sigmoid(c)
        for i in range(n_layers):
            o_ref[i] = _dot(s, w_ref[i]) + b_ref[i]

    return pl.pallas_call(
        body, name="mod_fwd", out_shape=_sds((n_layers, MOD_ROWS, cols)),
        compiler_params=pltpu.CompilerParams(vmem_limit_bytes=VMEM_LIMIT),
    )(cond, w_shard, b_cols.reshape(n_layers, 1, cols))


def _mod_bwd(cond, d_lat_cols, d_ctx_cols, w_shard):
    n_layers, d, cols = w_shard.shape

    def body(c_ref, dl_ref, dc_ref, w_ref, dw_ref, dcc_ref):
        c = c_ref[...]
        sg = _sigmoid(c)
        s = c * sg
        d_s = jnp.zeros((MOD_ROWS, d), F32)
        for i in range(n_layers):
            d_ctx = dc_ref[0, i]
            for j in range(1, N_DEV):
                d_ctx = d_ctx + dc_ref[j, i]
            dm = jnp.concatenate([dl_ref[i], d_ctx, jnp.zeros((MOD_ROWS - N_DEV - 1, cols), F32)], axis=0)
            dw_ref[i] = _dot_t0(s, dm)
            d_s = d_s + _dot_t1(dm, w_ref[i])
        d_c = d_s * (sg * (1.0 + c * (1.0 - sg)))
        dcc_ref[...] = d_c[CTX_ROW:CTX_ROW + 1]

    return pl.pallas_call(
        body, name="mod_bwd", out_shape=[_sds((n_layers, d, cols)), _sds((1, d))],
        compiler_params=pltpu.CompilerParams(vmem_limit_bytes=VMEM_LIMIT),
    )(cond, d_lat_cols, d_ctx_cols, w_shard)


ADAM_TILE = 512


def _adamw(w, g_parts, m, v, name):
    n_parts, n_rows, lanes = g_parts.shape
    c1 = 1.0 - ADAM_B1 ** ADAM_STEP
    c2 = 1.0 - ADAM_B2 ** ADAM_STEP

    def body(w_ref, g_ref, m_ref, v_ref, go_ref, d_ref, mo_ref, vo_ref):
        g = g_ref[0].astype(F32)
        for p in range(1, n_parts):
            g = g + g_ref[p].astype(F32)
        m_new = ADAM_B1 * m_ref[...] + (1.0 - ADAM_B1) * g
        v_new = ADAM_B2 * v_ref[...] + (1.0 - ADAM_B2) * (g * g)
        go_ref[...] = g
        mo_ref[...] = m_new
        vo_ref[...] = v_new
        d_ref[...] = -ADAM_LR * ((m_new / c1) / (jnp.sqrt(v_new / c2) + ADAM_EPS) + ADAM_WD * w_ref[...])

    row = pl.BlockSpec((ADAM_TILE, lanes), lambda i: (i, 0))
    return pl.pallas_call(
        body, name=name, grid=(n_rows // ADAM_TILE,),
        in_specs=[row, pl.BlockSpec((n_parts, ADAM_TILE, lanes), lambda i: (0, i, 0)), row, row],
        out_specs=[row] * 4, out_shape=[_sds((n_rows, lanes))] * 4,
        compiler_params=_cparams(1),
    )(w, g_parts, m, v)


def _sum_parts(parts):
    n_parts, n_rows, lanes = parts.shape

    def body(p_ref, o_ref):
        acc = p_ref[0]
        for p in range(1, n_parts):
            acc = acc + p_ref[p]
        o_ref[...] = acc

    return pl.pallas_call(body, name="sum_parts", out_shape=_sds((n_rows, lanes)))(parts)


def _pack(arrays, row_multiple):
    parts = []
    for a in arrays:
        flat = a.reshape(-1)
        parts.append(jnp.pad(flat, (0, (-flat.shape[0]) % 1024)))
    flat = jnp.concatenate(parts)
    flat = jnp.pad(flat, (0, (-flat.shape[0]) % (row_multiple * 128)))
    return flat.reshape(-1, 128)


def _unpack(packed, shapes):
    flat = packed.reshape(-1)
    out, pos = [], 0
    for s in shapes:
        n = math.prod(s)
        out.append(flat[pos:pos + n].reshape(s))
        pos += n + (-n) % 1024
    return out


WEIGHT_NAMES = ['c_ctx', 'w_mod', 'b_mod', 'norm_g', 'ssm_w_in', 'ssm_a_re', 'ssm_a_im', 'ssm_log_dt', 'ssm_b_re', 'ssm_b_im',
                'ssm_c_re', 'ssm_c_im', 'ssm_d', 'ssm_w_glu', 'ssm_b_glu', 'ssm_w_out', 'attn_w_in', 'attn_q_norm',
                'attn_k_norm', 'attn_w_out', 'final_norm_g']
SHARDED = ['ssm_w_in', 'ssm_w_glu', 'ssm_w_out', 'attn_w_in', 'attn_w_out']
COLUMN_SHARDED = ('ssm_w_in', 'attn_w_in')
REPLICATED = ['c_ctx', 'b_mod', 'norm_g', 'ssm_a_re', 'ssm_a_im', 'ssm_log_dt', 'ssm_b_re', 'ssm_b_im', 'ssm_c_re', 'ssm_c_im',
              'ssm_d', 'ssm_b_glu', 'attn_q_norm', 'attn_k_norm', 'final_norm_g']
SSM_NAMES = ['ssm_a_re', 'ssm_a_im', 'ssm_log_dt', 'ssm_b_re', 'ssm_b_im', 'ssm_c_re', 'ssm_c_im', 'ssm_d']


def _full_from_shards(gathered, name, shard_shape):
    rows, cols = shard_shape
    w = gathered.reshape(N_DEV, rows, cols)
    if name in COLUMN_SHARDED:
        return w.transpose(1, 0, 2).reshape(rows, N_DEV * cols)
    return w.reshape(N_DEV * rows, cols)


def _shards_from_full(g, name):
    if name in COLUMN_SHARDED:
        rows, cols = g.shape
        g = g.reshape(rows, N_DEV, cols // N_DEV).transpose(1, 0, 2)
    return g.reshape(N_DEV, -1, 128)


def kernel(x, c, ctx, c_ctx, w_mod, b_mod, norm_g, ssm_w_in, ssm_a_re, ssm_a_im, ssm_log_dt, ssm_b_re, ssm_b_im, ssm_c_re, ssm_c_im, ssm_d, ssm_w_glu, ssm_b_glu, ssm_w_out, attn_w_in, attn_q_norm, attn_k_norm, attn_w_out, final_norm_g, loss_target, m_c_ctx, m_w_mod, m_b_mod, m_norm_g, m_ssm_w_in, m_ssm_a_re, m_ssm_a_im, m_ssm_log_dt, m_ssm_b_re, m_ssm_b_im, m_ssm_c_re, m_ssm_c_im, m_ssm_d, m_ssm_w_glu, m_ssm_b_glu, m_ssm_w_out, m_attn_w_in, m_attn_q_norm, m_attn_k_norm, m_attn_w_out, m_final_norm_g, v_c_ctx, v_w_mod, v_b_mod, v_norm_g, v_ssm_w_in, v_ssm_a_re, v_ssm_a_im, v_ssm_log_dt, v_ssm_b_re, v_ssm_b_im, v_ssm_c_re, v_ssm_c_im, v_ssm_d, v_ssm_w_glu, v_ssm_b_glu, v_ssm_w_out, v_attn_w_in, v_attn_q_norm, v_attn_k_norm, v_attn_w_out, v_final_norm_g):
    env = dict(locals())
    weights = {n: env[n] for n in WEIGHT_NAMES}
    mom_m = {n: env["m_" + n] for n in WEIGHT_NAMES}
    mom_v = {n: env["v_" + n] for n in WEIGHT_NAMES}
    d = D_MODEL
    me = _my_index()
    mod_cols = w_mod.shape[-1]

    c_all = _exchange(c.reshape(8, d // 8), "gather_c", False).reshape(N_DEV, d)
    cond = jnp.concatenate([c_all, c_ctx.reshape(1, d), jnp.zeros((MOD_ROWS - N_DEV - 1, d), F32)], axis=0)
    shard_shapes = {n: weights[n].shape[1:] for n in SHARDED}
    w_gathered = _exchange(_pack([weights[n] for n in SHARDED], 1).astype(BF16), "gather_weights", False)
    full, pos = {}, 0
    for n in SHARDED:
        rows = math.prod(shard_shapes[n]) // 128
        full[n] = _full_from_shards(w_gathered[:, pos:pos + rows], n, shard_shapes[n])
        pos += rows

    b_cols = lax.dynamic_slice(b_mod, (0, me * mod_cols), (2, mod_cols))
    mod_shard = _mod_fwd(cond, w_mod, b_cols)
    mod_all = _exchange(mod_shard.reshape(2 * MOD_ROWS, mod_cols), "gather_mod", False)
    mod_full = mod_all.reshape(N_DEV, 2, MOD_ROWS, mod_cols).transpose(1, 2, 0, 3).reshape(2, MOD_ROWS, 3 * d)
    lat_rows = lax.dynamic_slice(mod_full, (0, me, 0), (2, 1, 3 * d))
    mods = []
    for i in range(2):
        seg = jnp.stack([mod_full[i, CTX_ROW:CTX_ROW + 1], lat_rows[i]])
        mods.append((seg[:, :, :d], seg[:, :, d:2 * d], seg[:, :, 2 * d:]))

    ssm = tuple(weights[n][0] for n in SSM_NAMES)
    loss, grad_x, g, d_mods = _local_step(
        x[0], ctx[0], loss_target[0], mods, norm_g, ssm, full['ssm_w_in'], full['ssm_w_glu'], ssm_b_glu[0], full['ssm_w_out'],
        full['attn_w_in'], attn_q_norm[0], attn_k_norm[0], full['attn_w_out'], final_norm_g)
    loss = lax.psum(loss, ("x", "y", "c"))

    d_rows = jnp.stack([jnp.concatenate(dm, axis=-1) for dm in d_mods])
    d_rows = jnp.concatenate([d_rows.reshape(4, 3 * d), jnp.zeros((4, 3 * d), F32)], axis=0)
    d_all = _exchange(d_rows, "gather_dmod", False)[:, :4].reshape(N_DEV, 2, 2, 3 * d)
    d_all = lax.dynamic_slice(d_all, (0, 0, 0, me * mod_cols), (N_DEV, 2, 2, mod_cols))
    d_w_mod, d_c_ctx = _mod_bwd(cond, d_all[:, :, 1].transpose(1, 0, 2), d_all[:, :, 0:1], w_mod)
    d_b_mod = jnp.stack([jnp.concatenate([t[0] + t[1] for t in dm], axis=-1).reshape(3 * d) for dm in d_mods])

    g_big = jnp.concatenate([_shards_from_full(g[n], n) for n in SHARDED], axis=1)
    g_big_parts = _exchange(g_big.astype(BF16), "scatter_grads", True)
    pack_big = lambda t: _pack([t[n] for n in SHARDED], ADAM_TILE)
    big = _adamw(pack_big(weights), g_big_parts, pack_big(mom_m), pack_big(mom_v), "adamw_sharded")
    big = [_unpack(t, [weights[n].shape for n in SHARDED]) for t in big]

    mod_res = _adamw(_pack([w_mod], ADAM_TILE), _pack([d_w_mod], ADAM_TILE)[None], _pack([m_w_mod], ADAM_TILE),
                     _pack([v_w_mod], ADAM_TILE), "adamw_w_mod")
    mod_res = [t.reshape(w_mod.shape) for t in mod_res]

    small = dict(zip(SSM_NAMES, g['ssm']))
    small.update(c_ctx=d_c_ctx, b_mod=d_b_mod, norm_g=g['norm_g'], ssm_b_glu=g['ssm_b_glu'], attn_q_norm=g['attn_q_norm'],
                 attn_k_norm=g['attn_k_norm'], final_norm_g=g['final_norm_g'])
    pack_small = lambda t: _pack([t[n] for n in REPLICATED], ADAM_TILE)
    g_small = pack_small(small)
    slices = _exchange(g_small.reshape(N_DEV, -1, 128), "scatter_small_grads", True)
    g_small = _exchange(_sum_parts(slices), "gather_small_grads", False).reshape(1, -1, 128)
    rep = _adamw(pack_small(weights), g_small, pack_small(mom_m), pack_small(mom_v), "adamw_replicated")
    rep = [_unpack(t, [weights[n].shape for n in REPLICATED]) for t in rep]

    results = []
    for kind in range(4):
        by_name = dict(zip(SHARDED, big[kind]))
        by_name.update(zip(REPLICATED, rep[kind]))
        by_name['w_mod'] = mod_res[kind]
        results.extend(by_name[n] for n in WEIGHT_NAMES)
    return (loss, grad_x[None], *results)
```

```python
import functools
import math

import jax
import jax.numpy as jnp
from jax import lax
from jax.experimental import pallas as pl
from jax.experimental.pallas import tpu as pltpu

F32 = jnp.float32
BF16 = jnp.bfloat16

N_DEV = 8
D_MODEL = 1024
NORM_EPS = 1e-6
SSM_GROUP = 16
SSM_GROUPS = 64
SSM_STATE = 64
GROUPS_PER_BLOCK = 8
N_BLOCKS = SSM_GROUPS // GROUPS_PER_BLOCK
HALF = GROUPS_PER_BLOCK * SSM_STATE
HEAD_DIM = 64
N_Q_HEADS = 16
N_KV_HEADS = 4
KV_REP = N_Q_HEADS // N_KV_HEADS
KV_WIDTH = N_KV_HEADS * HEAD_DIM
GRID_W = 64
ROPE_THETA = 10000.0
ADAM_LR, ADAM_B1, ADAM_B2, ADAM_EPS, ADAM_WD, ADAM_STEP = 0.001, 0.9, 0.999, 1e-08, 0.01, 10

ROW_TILE = 256
SCAN_CHUNK = 256
VMEM_LIMIT = 56 * 1024 * 1024
MESH_IDS = pl.DeviceIdType.MESH


def _cparams(n_axes):
    return pltpu.CompilerParams(dimension_semantics=("arbitrary",) * n_axes, vmem_limit_bytes=VMEM_LIMIT)


def _dot(a, b):
    return jnp.dot(a.astype(BF16), b.astype(BF16), preferred_element_type=F32)


def _dot_t0(a, b):
    return lax.dot_general(a.astype(BF16), b.astype(BF16), (((0,), (0,)), ((), ())), preferred_element_type=F32)


def _dot_t1(a, b):
    return lax.dot_general(a.astype(BF16), b.astype(BF16), (((1,), (1,)), ((), ())), preferred_element_type=F32)


def _s5_prep(a_re, a_im, log_dt, b_re, b_im):
    dt = jnp.exp(log_dt)[:, None]
    ldr, ldi = a_re * dt, a_im * dt
    mag = jnp.exp(ldr)
    abar_re, abar_im = mag * jnp.cos(ldi), mag * jnp.sin(ldi)
    den = a_re * a_re + a_im * a_im
    num_re, num_im = abar_re - 1.0, abar_im
    coef_re = (num_re * a_re + num_im * a_im) / den
    coef_im = (num_im * a_re - num_re * a_im) / den
    bbar_re = coef_re[..., None] * b_re - coef_im[..., None] * b_im
    bbar_im = coef_re[..., None] * b_im + coef_im[..., None] * b_re
    return abar_re, abar_im, bbar_re, bbar_im


def _s5_blocks(abar_re, abar_im, bbar_re, bbar_im, c_re, c_im):
    eye = jnp.eye(GROUPS_PER_BLOCK, dtype=F32)
    bb = jnp.stack([bbar_re, bbar_im]).reshape(2, N_BLOCKS, GROUPS_PER_BLOCK, SSM_STATE, SSM_GROUP)
    b_blk = jnp.einsum('rqgph,gk->qghrkp', bb, eye).reshape(N_BLOCKS, 128, 2 * HALF)
    cc = jnp.stack([c_re, -c_im]).reshape(2, N_BLOCKS, GROUPS_PER_BLOCK, SSM_GROUP, SSM_STATE)
    c_blk = jnp.einsum('rqghp,gk->qrgpkh', cc, eye).reshape(N_BLOCKS, 2 * HALF, 128)
    abar = jnp.stack([abar_re.reshape(N_BLOCKS, HALF), abar_im.reshape(N_BLOCKS, HALF)])
    return abar, b_blk, c_blk


def _s5_unblock(d_b_blk, d_ct_blk):
    db = d_b_blk.reshape(N_BLOCKS, GROUPS_PER_BLOCK, SSM_GROUP, 2, GROUPS_PER_BLOCK, SSM_STATE)
    db = jnp.einsum('qghrgp->rqgph', db).reshape(2, SSM_GROUPS, SSM_STATE, SSM_GROUP)
    dc = d_ct_blk.reshape(N_BLOCKS, GROUPS_PER_BLOCK, SSM_GROUP, 2, GROUPS_PER_BLOCK, SSM_STATE)
    dc = jnp.einsum('qghrgp->rqghp', dc).reshape(2, SSM_GROUPS, SSM_GROUP, SSM_STATE)
    return db[0], db[1], dc[0], -dc[1]


def _scan_chunk_of_step(j, n_chunks, n_ctx_chunks, reverse):
    if not reverse:
        return j
    return jnp.where(j < n_ctx_chunks, n_ctx_chunks - 1 - j, n_chunks - 1 - j + n_ctx_chunks)


LANE_TILES = 2 * HALF // 128
RE_TILES = HALF // 128


def _tiles(v):
    return [v[:, l * 128:(l + 1) * 128] for l in range(v.shape[1] // 128)]


def _scatter_steps(s_ref, q, x):
    for l in range(LANE_TILES):
        s_ref[l, pl.ds(q, x.shape[0], stride=N_BLOCKS), :] = x[:, l * 128:(l + 1) * 128]


def _gather_steps(s_ref, q, n_steps):
    return jnp.concatenate([s_ref[l, pl.ds(q, n_steps, stride=N_BLOCKS), :] for l in range(LANE_TILES)], axis=1)


def _load_step(s_ref, t):
    row = pl.multiple_of(t * N_BLOCKS, N_BLOCKS)
    return [s_ref[l, pl.ds(row, N_BLOCKS), :] for l in range(LANE_TILES)]


def _store_step(s_ref, t, tiles):
    row = pl.multiple_of(t * N_BLOCKS, N_BLOCKS)
    for l in range(LANE_TILES):
        s_ref[l, pl.ds(row, N_BLOCKS), :] = tiles[l]


def _cmul_add(a, h, x, conj):
    re, im = [], []
    for l in range(RE_TILES):
        ar, ai, hr, hi = a[l], a[RE_TILES + l], h[l], h[RE_TILES + l]
        if conj:
            re.append(ar * hr + ai * hi + x[l])
            im.append(ar * hi - ai * hr + x[RE_TILES + l])
        else:
            re.append(ar * hr - ai * hi + x[l])
            im.append(ar * hi + ai * hr + x[RE_TILES + l])
    return re + im


def _s5_scan_fwd(u, abar, b_blk, c_blk, d_skip, n_ctx, reverse):
    n_rows, width = u.shape
    tc = SCAN_CHUNK
    n_chunks, n_ctx_chunks = n_rows // tc, n_ctx // tc
    with_skip = d_skip is not None

    def body(*refs):
        if with_skip:
            u_ref, a_ref, b_ref, c_ref, d_ref, y_ref, hb_ref, s_ref, h_ref = refs
        else:
            u_ref, a_ref, b_ref, c_ref, y_ref, hb_ref, s_ref, h_ref = refs
        j = pl.program_id(0)

        @pl.when(j == 0)
        def _():
            h_ref[...] = jnp.zeros_like(h_ref)

        hb_ref[0] = h_ref[...]
        for q in range(N_BLOCKS):
            _scatter_steps(s_ref, q, _dot(u_ref[:, q * 128:(q + 1) * 128], b_ref[q]))
        a = _tiles(a_ref[0]) + _tiles(a_ref[1])

        def step(s, h):
            t = tc - 1 - s if reverse else s
            h = _cmul_add(a, h, _load_step(s_ref, t), conj=False)
            _store_step(s_ref, t, h)
            return h

        h = lax.fori_loop(0, tc, step, _tiles(h_ref[...]))
        h_ref[...] = jnp.concatenate(h, axis=1)
        for q in range(N_BLOCKS):
            yq = _dot(_gather_steps(s_ref, q, tc), c_ref[q])
            if with_skip:
                yq = yq + d_ref[:, q * 128:(q + 1) * 128] * u_ref[:, q * 128:(q + 1) * 128]
            y_ref[:, q * 128:(q + 1) * 128] = yq

    chunk = functools.partial(_scan_chunk_of_step, n_chunks=n_chunks, n_ctx_chunks=n_ctx_chunks, reverse=reverse)
    full3 = lambda j: (0, 0, 0)
    in_specs = [pl.BlockSpec((tc, width), lambda j: (chunk(j), 0)),
                pl.BlockSpec((2, N_BLOCKS, HALF), full3),
                pl.BlockSpec((N_BLOCKS, 128, 2 * HALF), full3),
                pl.BlockSpec((N_BLOCKS, 2 * HALF, 128), full3)]
    args = [u, abar, b_blk.astype(BF16), c_blk.astype(BF16)]
    if with_skip:
        in_specs.append(pl.BlockSpec((1, width), lambda j: (0, 0)))
        args.append(d_skip.reshape(1, width))
    return pl.pallas_call(
        body, name="s5_scan_fwd_rev" if reverse else "s5_scan_fwd",
        grid=(n_chunks,),
        in_specs=in_specs,
        out_specs=[pl.BlockSpec((tc, width), lambda j: (chunk(j), 0)),
                   pl.BlockSpec((1, N_BLOCKS, 2 * HALF), lambda j: (chunk(j), 0, 0))],
        out_shape=[jax.ShapeDtypeStruct((n_rows, width), F32),
                   jax.ShapeDtypeStruct((n_chunks, N_BLOCKS, 2 * HALF), F32)],
        scratch_shapes=[pltpu.VMEM((LANE_TILES, tc * N_BLOCKS, 128), F32), pltpu.VMEM((N_BLOCKS, 2 * HALF), F32)],
        compiler_params=_cparams(1),
    )(*args)


def _s5_scan_bwd(u, dy, hb, abar, b_blk, c_blk, d_skip, n_ctx, reverse):
    n_rows, width = u.shape
    tc = SCAN_CHUNK
    n_chunks, n_ctx_chunks = n_rows // tc, n_ctx // tc
    with_skip = d_skip is not None

    def body(*refs):
        if with_skip:
            (u_ref, dy_ref, hb_ref, a_ref, b_ref, bt_ref, ct_ref, d_ref,
             du_ref, da_ref, db_ref, dct_ref, dd_ref, sh_ref, sg_ref, g_ref) = refs
        else:
            (u_ref, dy_ref, hb_ref, a_ref, b_ref, bt_ref, ct_ref,
             du_ref, da_ref, db_ref, dct_ref, sh_ref, sg_ref, g_ref) = refs
        j = pl.program_id(0)

        @pl.when(j == 0)
        def _():
            g_ref[...] = jnp.zeros_like(g_ref)
            da_ref[...] = jnp.zeros_like(da_ref)
            db_ref[...] = jnp.zeros_like(db_ref)
            dct_ref[...] = jnp.zeros_like(dct_ref)
            if with_skip:
                dd_ref[...] = jnp.zeros_like(dd_ref)

        for q in range(N_BLOCKS):
            _scatter_steps(sh_ref, q, _dot(u_ref[:, q * 128:(q + 1) * 128], b_ref[q]))
            _scatter_steps(sg_ref, q, _dot(dy_ref[:, q * 128:(q + 1) * 128], ct_ref[q]))
        a = _tiles(a_ref[0]) + _tiles(a_ref[1])
        time_of = (lambda s: tc - 1 - s) if reverse else (lambda s: s)

        def fwd_step(s, h):
            h = _cmul_add(a, h, _load_step(sh_ref, time_of(s)), conj=False)
            _store_step(sh_ref, time_of(s), h)
            return h

        h0 = _tiles(hb_ref[0])
        lax.fori_loop(0, tc, fwd_step, h0)

        def adj(t, h_prev, carry):
            g, da = carry
            g = _cmul_add(a, g, _load_step(sg_ref, t), conj=True)
            _store_step(sg_ref, t, g)
            da_re = [da[l] + g[l] * h_prev[l] + g[RE_TILES + l] * h_prev[RE_TILES + l] for l in range(RE_TILES)]
            da_im = [da[RE_TILES + l] + g[RE_TILES + l] * h_prev[l] - g[l] * h_prev[RE_TILES + l] for l in range(RE_TILES)]
            return g, da_re + da_im

        def bwd_step(i, carry):
            s = tc - 1 - i
            return adj(time_of(s), _load_step(sh_ref, time_of(s - 1)), carry)

        carry = (_tiles(g_ref[...]), _tiles(da_ref[0]) + _tiles(da_ref[1]))
        carry = lax.fori_loop(0, tc - 1, bwd_step, carry)
        g, da = adj(time_of(0), h0, carry)
        g_ref[...] = jnp.concatenate(g, axis=1)
        da_ref[0] = jnp.concatenate(da[:RE_TILES], axis=1)
        da_ref[1] = jnp.concatenate(da[RE_TILES:], axis=1)

        for q in range(N_BLOCKS):
            cols = slice(q * 128, (q + 1) * 128)
            uq, dyq = u_ref[:, cols], dy_ref[:, cols]
            gq = _gather_steps(sg_ref, q, tc)
            duq = _dot(gq, bt_ref[q])
            if with_skip:
                duq = duq + d_ref[:, cols] * dyq
                dd_ref[:, cols] += jnp.sum(dyq * uq, axis=0, keepdims=True)
            du_ref[:, cols] = duq
            db_ref[q] += _dot_t0(uq, gq)
            dct_ref[q] += _dot_t0(dyq, _gather_steps(sh_ref, q, tc))

    def chunk(j):
        return _scan_chunk_of_step(n_chunks - 1 - j, n_chunks, n_ctx_chunks, reverse)

    full2 = lambda j: (0, 0)
    full3 = lambda j: (0, 0, 0)
    row = pl.BlockSpec((tc, width), lambda j: (chunk(j), 0))
    in_specs = [row, row,
                pl.BlockSpec((1, N_BLOCKS, 2 * HALF), lambda j: (chunk(j), 0, 0)),
                pl.BlockSpec((2, N_BLOCKS, HALF), full3),
                pl.BlockSpec((N_BLOCKS, 128, 2 * HALF), full3),
                pl.BlockSpec((N_BLOCKS, 2 * HALF, 128), full3),
                pl.BlockSpec((N_BLOCKS, 128, 2 * HALF), full3)]
    args = [u, dy, hb, abar, b_blk.astype(BF16), jnp.swapaxes(b_blk, 1, 2).astype(BF16),
            jnp.swapaxes(c_blk, 1, 2).astype(BF16)]
    out_specs = [row,
                 pl.BlockSpec((2, N_BLOCKS, HALF), full3),
                 pl.BlockSpec((N_BLOCKS, 128, 2 * HALF), full3),
                 pl.BlockSpec((N_BLOCKS, 128, 2 * HALF), full3)]
    out_shape = [jax.ShapeDtypeStruct((n_rows, width), F32),
                 jax.ShapeDtypeStruct((2, N_BLOCKS, HALF), F32),
                 jax.ShapeDtypeStruct((N_BLOCKS, 128, 2 * HALF), F32),
                 jax.ShapeDtypeStruct((N_BLOCKS, 128, 2 * HALF), F32)]
    if with_skip:
        in_specs.append(pl.BlockSpec((1, width), full2))
        args.append(d_skip.reshape(1, width))
        out_specs.append(pl.BlockSpec((1, width), full2))
        out_shape.append(jax.ShapeDtypeStruct((1, width), F32))
    return pl.pallas_call(
        body, name="s5_scan_bwd_rev" if reverse else "s5_scan_bwd",
        grid=(n_chunks,),
        in_specs=in_specs, out_specs=out_specs, out_shape=out_shape,
        scratch_shapes=[pltpu.VMEM((LANE_TILES, tc * N_BLOCKS, 128), F32), pltpu.VMEM((LANE_TILES, tc * N_BLOCKS, 128), F32),
                        pltpu.VMEM((N_BLOCKS, 2 * HALF), F32)],
        compiler_params=_cparams(1),
    )(*args)


def _s5_dir_params(d, a_re, a_im, log_dt, b_re, b_im):
    return a_re[d], a_im[d], log_dt[d], b_re[d], b_im[d]


def _s5_forward(u, ssm, n_ctx):
    a_re, a_im, log_dt, b_re, b_im, c_re, c_im, d_skip = ssm
    outs, saved = [], []
    for d in range(2):
        prep = _s5_prep(*_s5_dir_params(d, a_re, a_im, log_dt, b_re, b_im))
        abar, b_blk, c_blk = _s5_blocks(*prep, c_re[d], c_im[d])
        y, hb = _s5_scan_fwd(u, abar, b_blk, c_blk, d_skip if d == 0 else None, n_ctx, reverse=(d == 1))
        outs.append(y)
        saved.append((hb, abar, b_blk, c_blk))
    return outs, saved


def _s5_backward(u, dy, ssm, saved, n_ctx):
    a_re, a_im, log_dt, b_re, b_im, c_re, c_im, d_skip = ssm
    dus, grads = [], [[] for _ in range(7)]
    d_d = None
    for d in range(2):
        hb, abar, b_blk, c_blk = saved[d]
        res = _s5_scan_bwd(u, dy, hb, abar, b_blk, c_blk, d_skip if d == 0 else None, n_ctx, reverse=(d == 1))
        if d == 0:
            du, d_abar, d_b_blk, d_ct_blk, d_d = res
        else:
            du, d_abar, d_b_blk, d_ct_blk = res
        dus.append(du)
        dbb_re, dbb_im, dc_re, dc_im = _s5_unblock(d_b_blk, d_ct_blk)
        _, vjp = jax.vjp(_s5_prep, *_s5_dir_params(d, a_re, a_im, log_dt, b_re, b_im))
        shape = (SSM_GROUPS, SSM_STATE)
        g5 = vjp((d_abar[0].reshape(shape), d_abar[1].reshape(shape), dbb_re, dbb_im))
        for k, g in enumerate(tuple(g5) + (dc_re, dc_im)):
            grads[k].append(g)
    grads = [jnp.stack(g) for g in grads]
    return dus, grads + [d_d.reshape(-1)]


INV_SQRT2 = 0.7071067811865476
INV_SQRT_2PI = 0.3989422804014327


def _rows(cols):
    return pl.BlockSpec((ROW_TILE, cols), lambda i: (i, 0))


def _rows_skip_ctx(cols):
    return pl.BlockSpec((ROW_TILE, cols), lambda i: (i + 1, 0))


def _rows_lat(cols):
    return pl.BlockSpec((ROW_TILE, cols), lambda i: (jnp.maximum(i - 1, 0), 0))


def _full(shape):
    nd = len(shape)
    return pl.BlockSpec(shape, lambda i: (0,) * nd)


def _seg(cols):
    return pl.BlockSpec((1, 1, cols), lambda i: (jnp.minimum(i, 1), 0, 0))


def _lat_seg(cols):
    return pl.BlockSpec((1, 1, cols), lambda i: (1, 0, 0))


def _sds(shape, dtype=F32):
    return jax.ShapeDtypeStruct(shape, dtype)


def _sum0(x):
    return jnp.sum(x, axis=0, keepdims=True)


def _sigmoid(x):
    return jax.nn.sigmoid(x)


def _rms_mod(x, g, scale, shift):
    r = lax.rsqrt(jnp.mean(x * x, axis=-1, keepdims=True) + NORM_EPS)
    return (x * r * g) * (1.0 + scale) + shift


def _rms_mod_bwd(x, g, scale, dh):
    r = lax.rsqrt(jnp.mean(x * x, axis=-1, keepdims=True) + NORM_EPS)
    n = x * r
    dyg = dh * (1.0 + scale)
    dn = dyg * g
    dx = r * (dn - n * jnp.mean(dn * n, axis=-1, keepdims=True))
    return dx, _sum0(dyg * n), _sum0(dh * (n * g)), _sum0(dh)


def _head_of_lane(width):
    return (jnp.arange(width)[:, None] // HEAD_DIM == jnp.arange(128)[None, :]).astype(BF16)


def _split_dot(t, w, transposed):
    hi = t.astype(BF16)
    lo = (t - hi.astype(F32)).astype(BF16)
    f = _dot_t1 if transposed else _dot
    return f(hi, w) + f(lo, w)


def _head_sums(t, hl):
    return _split_dot(_split_dot(t, hl, False), hl, True)


def _rope_partner(x):
    n = x.shape[1]
    lane = lax.broadcasted_iota(jnp.int32, x.shape, 1)
    return jnp.where((lane & 16) == 0, pltpu.roll(x, n - 16, 1), pltpu.roll(x, 16, 1))


def _lanes(tab, width):
    return jnp.tile(tab, (1, width // tab.shape[1]))


def _head_norm_rope(x, gain, cos, sin, hl):
    r = lax.rsqrt(_head_sums(x * x, hl) * (1.0 / HEAD_DIM) + NORM_EPS)
    y = x * r * gain
    return y * cos + _rope_partner(y) * sin


def _head_norm_rope_bwd(x, gain, cos, sin, hl, dout):
    dy = dout * cos + _rope_partner(dout * sin)
    r = lax.rsqrt(_head_sums(x * x, hl) * (1.0 / HEAD_DIM) + NORM_EPS)
    n = x * r
    dn = dy * gain
    dx = r * (dn - n * (_head_sums(dn * n, hl) * (1.0 / HEAD_DIM)))
    return dx, _sum0(dy * n)


def _rope_tables(n_ctx, n_lat):
    t = jnp.arange(n_lat)
    pos = jnp.stack([(t // GRID_W).astype(F32), (t % GRID_W).astype(F32)], axis=1)
    n_freq = HEAD_DIM // 4
    freqs = ROPE_THETA ** (-jnp.arange(n_freq, dtype=F32) / n_freq)
    ang = pos[:, :, None] * freqs[None, None, :]
    cos = jnp.repeat(jnp.cos(ang)[:, :, None, :], 2, axis=2).reshape(n_lat, HEAD_DIM)
    sin = jnp.sin(ang)
    sin = jnp.stack([-sin, sin], axis=2).reshape(n_lat, HEAD_DIM)
    cos = jnp.concatenate([jnp.ones((n_ctx, HEAD_DIM), F32), cos], axis=0)
    sin = jnp.concatenate([jnp.zeros((n_ctx, HEAD_DIM), F32), sin], axis=0)
    return jnp.tile(cos, (1, 2)), jnp.tile(sin, (1, 2))


def _ssm_in(xa, g, scale, shift, w_in):
    n_rows, d = xa.shape
    e = w_in.shape[1] // 2

    def body(x_ref, g_ref, sc_ref, sh_ref, w_ref, u_ref, z_ref):
        h = _rms_mod(x_ref[...], g_ref[...], sc_ref[0], sh_ref[0])
        proj = _dot(h, w_ref[...])
        u_ref[...] = proj[:, :e]
        z_ref[...] = proj[:, e:]

    return pl.pallas_call(
        body, name="ssm_in", grid=(n_rows // ROW_TILE,),
        in_specs=[_rows(d), _full((1, d)), _seg(d), _seg(d), _full(w_in.shape)],
        out_specs=[_rows(e), _rows(e)], out_shape=[_sds((n_rows, e)), _sds((n_rows, e))],
        compiler_params=_cparams(1),
    )(xa, g, scale, shift, w_in)


def _s5_post_math(y, z, w_glu, b_glu, w_out):
    er = lax.erf(y * INV_SQRT2)
    g = 0.5 * y * (1.0 + er)
    sg = _sigmoid(_dot(g, w_glu) + b_glu)
    g2 = g * sg
    sz = _sigmoid(z)
    silu_z = z * sz
    m = g2 * silu_z
    return er, g, sg, g2, sz, silu_z, m, _dot(m, w_out)


def _ssm_post(xa, y0, y1, z, gate, w_glu, b_glu, w_out):
    n_rows, d = xa.shape
    e = z.shape[1]

    def body(x_ref, y0_ref, y1_ref, z_ref, gt_ref, wg_ref, bg_ref, wo_ref, o_ref):
        out = _s5_post_math(y0_ref[...] + y1_ref[...], z_ref[...], wg_ref[...], bg_ref[...], wo_ref[...])[-1]
        o_ref[...] = x_ref[...] + gt_ref[0] * out

    return pl.pallas_call(
        body, name="ssm_post", grid=(n_rows // ROW_TILE,),
        in_specs=[_rows(d), _rows(e), _rows(e), _rows(e), _seg(d), _full(w_glu.shape), _full((1, e)), _full(w_out.shape)],
        out_specs=_rows(d), out_shape=_sds((n_rows, d)),
        compiler_params=_cparams(1),
    )(xa, y0, y1, z, gate, w_glu, b_glu, w_out)


def _init_acc(first, *refs):
    @pl.when(first)
    def _():
        for r in refs:
            r[...] = jnp.zeros_like(r)


def _ssm_post_bwd(dxa, y0, y1, z, gate, w_glu, b_glu, w_out, w_glu_t, w_out_t):
    n_rows, d = dxa.shape
    e = z.shape[1]

    def body(dx_ref, y0_ref, y1_ref, z_ref, gt_ref, wg_ref, bg_ref, wo_ref, wgt_ref, wot_ref,
             dy_ref, dz_ref, dgt_ref, dwo_ref, dwg_ref, dbg_ref):
        i = pl.program_id(0)
        _init_acc(i == 0, dwo_ref, dwg_ref, dbg_ref)
        _init_acc(i <= 1, dgt_ref)
        y, zz = y0_ref[...] + y1_ref[...], z_ref[...]
        er, g, sg, g2, sz, silu_z, m, out = _s5_post_math(y, zz, wg_ref[...], bg_ref[...], wo_ref[...])
        dxa_t = dx_ref[...]
        dgt_ref[0] += _sum0(dxa_t * out)
        dout = gt_ref[0] * dxa_t
        dm = _dot(dout, wot_ref[...])
        dwo_ref[...] += _dot_t0(m, dout)
        dg2 = dm * silu_z
        dz_ref[...] = dm * g2 * (sz * (1.0 + zz * (1.0 - sz)))
        dt = dg2 * g * sg * (1.0 - sg)
        dwg_ref[...] += _dot_t0(g, dt)
        dbg_ref[...] += _sum0(dt)
        dg = dg2 * sg + _dot(dt, wgt_ref[...])
        dy_ref[...] = dg * (0.5 * (1.0 + er) + y * jnp.exp(-0.5 * y * y) * INV_SQRT_2PI)

    return pl.pallas_call(
        body, name="ssm_post_bwd", grid=(n_rows // ROW_TILE,),
        in_specs=[_rows(d), _rows(e), _rows(e), _rows(e), _seg(d), _full(w_glu.shape), _full((1, e)), _full(w_out.shape),
                  _full(w_glu_t.shape), _full(w_out_t.shape)],
        out_specs=[_rows(e), _rows(e), _seg(d), _full(w_out.shape), _full(w_glu.shape), _full((1, e))],
        out_shape=[_sds((n_rows, e)), _sds((n_rows, e)), _sds((2, 1, d)), _sds(w_out.shape), _sds(w_glu.shape), _sds((1, e))],
        compiler_params=_cparams(1),
    )(dxa, y0, y1, z, gate, w_glu, b_glu, w_out, w_glu_t, w_out_t)


def _ssm_in_bwd(du0, du1, dz, xa, dxa_next, g, scale, shift, w_in_t):
    n_rows, d = xa.shape
    e = dz.shape[1]
    n_lat = n_rows - ROW_TILE

    def body(du0_ref, du1_ref, dz_ref, x_ref, dn_ref, g_ref, sc_ref, sh_ref, wt_ref,
             gx_ref, dw_ref, dg_ref, dsc_ref, dsh_ref):
        i = pl.program_id(0)
        _init_acc(i == 0, dw_ref, dg_ref)
        _init_acc(i <= 1, dsc_ref, dsh_ref)
        x = x_ref[...]
        h = _rms_mod(x, g_ref[...], sc_ref[0], sh_ref[0])
        dproj = jnp.concatenate([du0_ref[...] + du1_ref[...], dz_ref[...]], axis=1)
        dh = _dot(dproj, wt_ref[...])
        dw_ref[...] += _dot_t0(h, dproj)
        dx, dg, dsc, dsh = _rms_mod_bwd(x, g_ref[...], sc_ref[0], dh)
        dg_ref[...] += dg
        dsc_ref[0] += dsc
        dsh_ref[0] += dsh
        gx_ref[...] = dn_ref[...] + dx

    return pl.pallas_call(
        body, name="ssm_in_bwd", grid=(n_rows // ROW_TILE,),
        in_specs=[_rows(e), _rows(e), _rows(e), _rows(d), _rows(d), _full((1, d)), _seg(d), _seg(d), _full(w_in_t.shape)],
        out_specs=[_rows_lat(d), _full((d, 2 * e)), _full((1, d)), _seg(d), _seg(d)],
        out_shape=[_sds((n_lat, d)), _sds((d, 2 * e)), _sds((1, d)), _sds((2, 1, d)), _sds((2, 1, d))],
        compiler_params=_cparams(1),
    )(du0, du1, dz, xa, dxa_next, g, scale, shift, w_in_t)


Q_WIDTH = N_Q_HEADS * HEAD_DIM
SM_SCALE = 1.0 / math.sqrt(HEAD_DIM)


def _attn_in(xa, g, scale, shift, w_in, q_gain, k_gain, cos, sin):
    n_rows, d = xa.shape
    qk = Q_WIDTH + KV_WIDTH

    def body(x_ref, g_ref, sc_ref, sh_ref, w_ref, qg_ref, kg_ref, cos_ref, sin_ref, hq_ref, hk_ref,
             q_ref, k_ref, v_ref, z_ref, raw_ref):
        h = _rms_mod(x_ref[...], g_ref[...], sc_ref[0], sh_ref[0])
        proj = _dot(h, w_ref[...])
        q_raw, k_raw = proj[:, :Q_WIDTH], proj[:, Q_WIDTH:qk]
        cos, sin = cos_ref[...], sin_ref[...]
        q = _head_norm_rope(q_raw, qg_ref[...], _lanes(cos, Q_WIDTH), _lanes(sin, Q_WIDTH), hq_ref[...])
        k = _head_norm_rope(k_raw, kg_ref[...], _lanes(cos, KV_WIDTH), _lanes(sin, KV_WIDTH), hk_ref[...])
        q_ref[...] = (q * SM_SCALE).astype(BF16)
        k_ref[...] = k.astype(BF16)
        v_ref[...] = proj[:, qk:qk + KV_WIDTH].astype(BF16)
        z_ref[...] = proj[:, qk + KV_WIDTH:]
        raw_ref[...] = proj[:, :qk]

    return pl.pallas_call(
        body, name="attn_in", grid=(n_rows // ROW_TILE,),
        in_specs=[_rows(d), _full((1, d)), _seg(d), _seg(d), _full(w_in.shape), _full((1, Q_WIDTH)), _full((1, KV_WIDTH)),
                  _rows(128), _rows(128), _full((Q_WIDTH, 128)), _full((KV_WIDTH, 128))],
        out_specs=[_rows_lat(Q_WIDTH), _rows(KV_WIDTH), _rows(KV_WIDTH), _rows(Q_WIDTH), _rows(qk)],
        out_shape=[_sds((n_rows - ROW_TILE, Q_WIDTH), BF16), _sds((n_rows, KV_WIDTH), BF16), _sds((n_rows, KV_WIDTH), BF16),
                   _sds((n_rows, Q_WIDTH)), _sds((n_rows, qk))],
        compiler_params=_cparams(1),
    )(xa, g, scale, shift, w_in, q_gain, k_gain, cos, sin, _head_of_lane(Q_WIDTH), _head_of_lane(KV_WIDTH))


GROUP_WIDTH = KV_REP * HEAD_DIM


def _stack_heads(ref):
    return jnp.concatenate([ref[:, h * HEAD_DIM:(h + 1) * HEAD_DIM] for h in range(KV_REP)], axis=0)


def _unstack_heads(a_t, tq):
    return jnp.concatenate([a_t[:, h * tq:(h + 1) * tq].T for h in range(KV_REP)], axis=1)


def _kv_tile(n_keys):
    return 768 if n_keys % 768 == 0 else 256


def _q_tile(n_lat):
    return 512 if n_lat % 512 == 0 else 256


def _flash_fwd(q, k, v_t):
    n_keys, n_lat = k.shape[1], q.shape[0]
    tq, tk = _q_tile(n_lat), _kv_tile(n_keys)
    rows = KV_REP * tq
    n_kv, n_q = n_keys // tk, n_lat // tq

    def body(q_ref, k_ref, vt_ref, o_ref, lse_ref, q_s, m_ref, l_ref, acc_ref):
        kj = pl.program_id(2)

        @pl.when(kj == 0)
        def _():
            q_s[...] = _stack_heads(q_ref)
            m_ref[...] = jnp.full_like(m_ref, -jnp.inf)
            l_ref[...] = jnp.zeros_like(l_ref)
            acc_ref[...] = jnp.zeros_like(acc_ref)

        s_t = _dot_t1(k_ref[0], q_s[...])
        m_prev = m_ref[...]
        m_new = jnp.maximum(m_prev, jnp.max(s_t, axis=0, keepdims=True))
        alpha = jnp.exp(m_prev - m_new)
        p_t = jnp.exp(s_t - m_new)
        l_ref[...] = alpha * l_ref[...] + jnp.sum(p_t, axis=0, keepdims=True)
        acc_ref[...] = alpha * acc_ref[...] + _dot(vt_ref[0], p_t)
        m_ref[...] = m_new

        @pl.when(kj == n_kv - 1)
        def _():
            l = l_ref[...]
            o_ref[...] = _unstack_heads(acc_ref[...] / l, tq)
            lse_ref[0, 0] = m_ref[...] + jnp.log(l)

    return pl.pallas_call(
        body, name="flash_fwd", grid=(N_KV_HEADS, n_q, n_kv),
        in_specs=[pl.BlockSpec((tq, GROUP_WIDTH), lambda g, i, j: (i, g)),
                  pl.BlockSpec((1, tk, HEAD_DIM), lambda g, i, j: (g, j, 0)),
                  pl.BlockSpec((1, HEAD_DIM, tk), lambda g, i, j: (g, 0, j))],
        out_specs=[pl.BlockSpec((tq, GROUP_WIDTH), lambda g, i, j: (i, g)),
                   pl.BlockSpec((1, 1, 1, rows), lambda g, i, j: (g, i, 0, 0))],
        out_shape=[_sds((n_lat, Q_WIDTH)), _sds((N_KV_HEADS, n_q, 1, rows))],
        scratch_shapes=[pltpu.VMEM((rows, HEAD_DIM), BF16), pltpu.VMEM((1, rows), F32), pltpu.VMEM((1, rows), F32),
                        pltpu.VMEM((HEAD_DIM, rows), F32)],
        compiler_params=_cparams(3),
    )(q, k, v_t)


def _flash_bwd(q, k, k_t, v, do, lse_t, delta_t):
    n_keys, n_lat = k.shape[1], q.shape[0]
    tq, tk = _q_tile(n_lat), _kv_tile(n_keys)
    rows = KV_REP * tq
    n_kv, n_q = n_keys // tk, n_lat // tq

    def body(q_ref, k_ref, kt_ref, v_ref, do_ref, lse_ref, dl_ref, dq_ref, dk_ref, dv_ref, q_s, do_s, dq_acc):
        qi, kj = pl.program_id(1), pl.program_id(2)

        @pl.when(kj == 0)
        def _():
            q_s[...] = _stack_heads(q_ref)
            do_s[...] = _stack_heads(do_ref)
            dq_acc[...] = jnp.zeros_like(dq_acc)

        q, do = q_s[...], do_s[...]
        p_t = jnp.exp(_dot_t1(k_ref[0], q) - lse_ref[0, 0])
        dv = _dot(p_t, do)
        ds_t = p_t * (_dot_t1(v_ref[0], do) - dl_ref[0, 0])
        dk = _dot(ds_t, q)
        dq_acc[...] += _dot(kt_ref[0], ds_t)
        keys = pl.ds(pl.multiple_of(kj * tk, tk), tk)

        @pl.when(qi == 0)
        def _():
            dk_ref[0, keys, :] = dk
            dv_ref[0, keys, :] = dv

        @pl.when(qi > 0)
        def _():
            dk_ref[0, keys, :] += dk
            dv_ref[0, keys, :] += dv

        @pl.when(kj == n_kv - 1)
        def _():
            dq_ref[...] = _unstack_heads(dq_acc[...], tq)

    qspec = pl.BlockSpec((tq, GROUP_WIDTH), lambda g, i, j: (i, g))
    kspec = pl.BlockSpec((1, tk, HEAD_DIM), lambda g, i, j: (g, j, 0))
    rowspec = pl.BlockSpec((1, 1, 1, rows), lambda g, i, j: (g, i, 0, 0))
    kv_all = pl.BlockSpec((1, n_keys, HEAD_DIM), lambda g, i, j: (g, 0, 0))
    return pl.pallas_call(
        body, name="flash_bwd", grid=(N_KV_HEADS, n_q, n_kv),
        in_specs=[qspec, kspec, pl.BlockSpec((1, HEAD_DIM, tk), lambda g, i, j: (g, 0, j)), kspec, qspec, rowspec, rowspec],
        out_specs=[qspec, kv_all, kv_all],
        out_shape=[_sds((n_lat, Q_WIDTH)), _sds((N_KV_HEADS, n_keys, HEAD_DIM)), _sds((N_KV_HEADS, n_keys, HEAD_DIM))],
        scratch_shapes=[pltpu.VMEM((rows, HEAD_DIM), BF16), pltpu.VMEM((rows, HEAD_DIM), BF16),
                        pltpu.VMEM((HEAD_DIM, rows), F32)],
        compiler_params=_cparams(3),
    )(q, k, k_t, v, do, lse_t, delta_t)


def _to_lane_stacked(a, tq):
    n_lat = a.shape[0]
    a = a.reshape(n_lat // tq, tq, N_KV_HEADS, KV_REP).transpose(2, 0, 3, 1)
    return a.reshape(N_KV_HEADS, n_lat // tq, 1, KV_REP * tq)


def _attn_post_loss(o, z, xa, gate, w_out, w_out_t, final_g, target):
    n_lat, d = target.shape
    e = o.shape[1]
    head_of_lane = (jnp.arange(e)[:, None] // HEAD_DIM == jnp.arange(128)[None, :]).astype(BF16)

    def body(o_ref, z_ref, x_ref, gt_ref, w_ref, wt_ref, fg_ref, tg_ref, hl_ref,
             do_ref, dl_ref, dz_ref, dx_ref, loss_ref, dfg_ref, dgt_ref, dw_ref):
        _init_acc(pl.program_id(0) == 0, loss_ref, dfg_ref, dgt_ref, dw_ref)
        oo, zz, gate_t, fg = o_ref[...], z_ref[...], gt_ref[0], fg_ref[...]
        sz = _sigmoid(zz)
        silu_z = zz * sz
        m = oo * silu_z
        out = _dot(m, w_ref[...])
        x2 = x_ref[...] + gate_t * out
        r = lax.rsqrt(jnp.mean(x2 * x2, axis=-1, keepdims=True) + NORM_EPS)
        n = x2 * r
        err = n * fg - tg_ref[...]
        loss_ref[...] += 0.5 * jnp.sum(jnp.mean(err * err, axis=-1, keepdims=True), axis=0, keepdims=True)
        dy = err * (1.0 / d)
        dfg_ref[...] += _sum0(dy * n)
        dn = dy * fg
        dx2 = r * (dn - n * jnp.mean(dn * n, axis=-1, keepdims=True))
        dx_ref[...] = dx2
        dgt_ref[...] += _sum0(dx2 * out)
        dout = gate_t * dx2
        dw_ref[...] += _dot_t0(m, dout)
        dm = _dot(dout, wt_ref[...])
        do = dm * silu_z
        do_ref[...] = do.astype(BF16)
        prod = do * oo
        hi = prod.astype(BF16)
        lo = (prod - hi.astype(F32)).astype(BF16)
        dl_ref[...] = _dot(hi, hl_ref[...]) + _dot(lo, hl_ref[...])
        dz_ref[...] = dm * oo * (sz * (1.0 + zz * (1.0 - sz)))

    return pl.pallas_call(
        body, name="attn_post_loss", grid=(n_lat // ROW_TILE,),
        in_specs=[_rows(e), _rows_skip_ctx(e), _rows_skip_ctx(d), _lat_seg(d), _full(w_out.shape), _full(w_out_t.shape),
                  _full((1, d)), _rows(d), _full((e, 128))],
        out_specs=[_rows(e), _rows(128), _rows(e), _rows(d), _full((1, 1)), _full((1, d)), _full((1, d)), _full(w_out.shape)],
        out_shape=[_sds((n_lat, e), BF16), _sds((n_lat, 128)), _sds((n_lat, e)), _sds((n_lat, d)), _sds((1, 1)), _sds((1, d)),
                   _sds((1, d)), _sds(w_out.shape)],
        compiler_params=_cparams(1),
    )(o, z, xa, gate, w_out, w_out_t, final_g, target, head_of_lane)


def _attn_in_bwd(dq, dk, dv, dz, raw, xa, dx2, g, scale, shift, q_gain, k_gain, cos, sin, w_in_t):
    n_rows, d = xa.shape
    qk = Q_WIDTH + KV_WIDTH
    n_in = w_in_t.shape[0]

    def body(dq_ref, dk_ref, dv_ref, dz_ref, raw_ref, x_ref, dx2_ref, g_ref, sc_ref, sh_ref, qg_ref, kg_ref, cos_ref, sin_ref,
             wt_ref, hq_ref, hk_ref, dxa_ref, dw_ref, dqg_ref, dkg_ref, dg_ref, dsc_ref, dsh_ref):
        i = pl.program_id(0)
        _init_acc(i == 0, dw_ref, dqg_ref, dkg_ref, dg_ref)
        _init_acc(i <= 1, dsc_ref, dsh_ref)
        is_lat = (i > 0).astype(F32)
        x = x_ref[...]
        h = _rms_mod(x, g_ref[...], sc_ref[0], sh_ref[0])
        cos, sin = cos_ref[...], sin_ref[...]
        raw_t = raw_ref[...]
        dq_raw, dqg = _head_norm_rope_bwd(raw_t[:, :Q_WIDTH], qg_ref[...], _lanes(cos, Q_WIDTH), _lanes(sin, Q_WIDTH),
                                          hq_ref[...], dq_ref[...] * (SM_SCALE * is_lat))
        dk_raw, dkg = _head_norm_rope_bwd(raw_t[:, Q_WIDTH:], kg_ref[...], _lanes(cos, KV_WIDTH), _lanes(sin, KV_WIDTH),
                                          hk_ref[...], dk_ref[...])
        dqg_ref[...] += dqg
        dkg_ref[...] += dkg
        dproj = jnp.concatenate([dq_raw, dk_raw, dv_ref[...], dz_ref[...] * is_lat], axis=1)
        dh = _dot(dproj, wt_ref[...])
        dw_ref[...] += _dot_t0(h, dproj)
        dx, dg, dsc, dsh = _rms_mod_bwd(x, g_ref[...], sc_ref[0], dh)
        dg_ref[...] += dg
        dsc_ref[0] += dsc
        dsh_ref[0] += dsh
        dxa_ref[...] = dx + dx2_ref[...] * is_lat

    return pl.pallas_call(
        body, name="attn_in_bwd", grid=(n_rows // ROW_TILE,),
        in_specs=[_rows_lat(Q_WIDTH), _rows(KV_WIDTH), _rows(KV_WIDTH), _rows_lat(Q_WIDTH), _rows(qk), _rows(d), _rows_lat(d),
                  _full((1, d)), _seg(d), _seg(d), _full((1, Q_WIDTH)), _full((1, KV_WIDTH)), _rows(128), _rows(128),
                  _full(w_in_t.shape), _full((Q_WIDTH, 128)), _full((KV_WIDTH, 128))],
        out_specs=[_rows(d), _full((d, n_in)), _full((1, Q_WIDTH)), _full((1, KV_WIDTH)), _full((1, d)), _seg(d), _seg(d)],
        out_shape=[_sds((n_rows, d)), _sds((d, n_in)), _sds((1, Q_WIDTH)), _sds((1, KV_WIDTH)), _sds((1, d)),
                   _sds((2, 1, d)), _sds((2, 1, d))],
        compiler_params=_cparams(1),
    )(dq, dk, dv, dz, raw, xa, dx2, g, scale, shift, q_gain, k_gain, cos, sin, w_in_t,
      _head_of_lane(Q_WIDTH), _head_of_lane(KV_WIDTH))


def _heads_major(a, n_heads):
    return a.reshape(a.shape[0], n_heads, HEAD_DIM).transpose(1, 0, 2)


def _tokens_major(a):
    return a.transpose(1, 0, 2).reshape(a.shape[1], a.shape[0] * HEAD_DIM)


def _local_step(x, ctx, target, mods, norm_g, ssm, w_ssm_in, w_glu, b_glu, w_ssm_out, w_attn_in, q_norm, k_norm, w_attn_out,
                final_g):
    n_ctx, d = ctx.shape
    assert n_ctx == ROW_TILE
    n_lat = x.shape[0]
    (shift0, scale0, gate0), (shift1, scale1, gate1) = mods
    g0, g1, fg = norm_g[0:1], norm_g[1:2], final_g.reshape(1, d)
    b_glu = b_glu.reshape(1, -1)
    q_gain = jnp.tile(q_norm.reshape(1, HEAD_DIM), (1, N_Q_HEADS))
    k_gain = jnp.tile(k_norm.reshape(1, HEAD_DIM), (1, N_KV_HEADS))
    cos, sin = _rope_tables(n_ctx, n_lat)

    xa0 = jnp.concatenate([ctx, x], axis=0)
    u, z0 = _ssm_in(xa0, g0, scale0, shift0, w_ssm_in)
    (y0, y1), saved = _s5_forward(u, ssm, n_ctx)
    xa1 = _ssm_post(xa0, y0, y1, z0, gate0, w_glu, b_glu, w_ssm_out)

    q, k, v, z1, raw = _attn_in(xa1, g1, scale1, shift1, w_attn_in, q_gain, k_gain, cos, sin)
    k_h, v_h = _heads_major(k, N_KV_HEADS), _heads_major(v, N_KV_HEADS)
    tq = _q_tile(n_lat)
    o, lse_t = _flash_fwd(q, k_h, v_h.transpose(0, 2, 1))
    do, delta, dz1, dx2, loss, d_fg, d_gate1, d_w_attn_out = _attn_post_loss(
        o, z1, xa1, gate1, w_attn_out, w_attn_out.T, fg, target)

    dq, dk_h, dv_h = _flash_bwd(q, k_h, k_h.transpose(0, 2, 1), v_h, do, lse_t, _to_lane_stacked(delta[:, :N_Q_HEADS], tq))
    dxa1, d_w_attn_in, d_qg, d_kg, d_g1, d_scale1, d_shift1 = _attn_in_bwd(
        dq, _tokens_major(dk_h), _tokens_major(dv_h), dz1, raw, xa1, dx2, g1, scale1, shift1,
        q_gain, k_gain, cos, sin, w_attn_in.T)
    dy, dz0, d_gate0, d_w_ssm_out, d_w_glu, d_b_glu = _ssm_post_bwd(
        dxa1, y0, y1, z0, gate0, w_glu, b_glu, w_ssm_out, w_glu.T, w_ssm_out.T)
    (du0, du1), d_ssm = _s5_backward(u, dy, ssm, saved, n_ctx)
    grad_x, d_w_ssm_in, d_g0, d_scale0, d_shift0 = _ssm_in_bwd(du0, du1, dz0, xa0, dxa1, g0, scale0, shift0, w_ssm_in.T)

    d_gate1_seg = jnp.concatenate([jnp.zeros((1, 1, d), F32), d_gate1.reshape(1, 1, d)], axis=0)
    grads = dict(
        norm_g=jnp.concatenate([d_g0, d_g1], axis=0),
        ssm_w_in=d_w_ssm_in, ssm=d_ssm, ssm_w_glu=d_w_glu, ssm_b_glu=d_b_glu.reshape(-1), ssm_w_out=d_w_ssm_out,
        attn_w_in=d_w_attn_in,
        attn_q_norm=d_qg.reshape(N_Q_HEADS, HEAD_DIM).sum(0), attn_k_norm=d_kg.reshape(N_KV_HEADS, HEAD_DIM).sum(0),
        attn_w_out=d_w_attn_out, final_norm_g=d_fg.reshape(-1))
    d_mods = ((d_shift0, d_scale0, d_gate0), (d_shift1, d_scale1, d_gate1_seg))
    return loss[0, 0], grad_x, grads, d_mods


def _my_index():
    return 4 * lax.axis_index("x") + 2 * lax.axis_index("y") + lax.axis_index("c")


def _peer(k):
    mx, my, mc = lax.axis_index("x"), lax.axis_index("y"), lax.axis_index("c")
    px = 1 - mx if k & 4 else mx
    py = 1 - my if k & 2 else my
    pc = 1 - mc if k & 1 else mc
    return (px, py, pc), 4 * px + 2 * py + pc


HBM_SPEC = pl.BlockSpec(memory_space=pl.ANY)


def _exchange(x, name, all_to_all):
    block = x.shape[1:] if all_to_all else x.shape

    def body(x_ref, out_ref, send_sems, recv_sems, local_sem):
        me = _my_index()
        mine = pltpu.make_async_copy(x_ref.at[me] if all_to_all else x_ref, out_ref.at[me], local_sem)
        mine.start()
        sends = []
        for k in range(1, N_DEV):
            peer, peer_idx = _peer(k)
            cp = pltpu.make_async_remote_copy(
                src_ref=x_ref.at[peer_idx] if all_to_all else x_ref, dst_ref=out_ref.at[me],
                send_sem=send_sems.at[k - 1], recv_sem=recv_sems.at[k - 1], device_id=peer, device_id_type=MESH_IDS)
            cp.start()
            sends.append(cp)
        for k in range(1, N_DEV):
            peer, peer_idx = _peer(k)
            pltpu.make_async_remote_copy(
                src_ref=x_ref.at[me] if all_to_all else x_ref, dst_ref=out_ref.at[peer_idx],
                send_sem=send_sems.at[k - 1], recv_sem=recv_sems.at[k - 1], device_id=peer, device_id_type=MESH_IDS).wait_recv()
        for cp in sends:
            cp.wait_send()
        mine.wait()

    return pl.pallas_call(
        body, name=name, in_specs=[HBM_SPEC], out_specs=HBM_SPEC,
        out_shape=_sds((N_DEV,) + tuple(block), x.dtype),
        scratch_shapes=[pltpu.SemaphoreType.DMA((N_DEV - 1,)), pltpu.SemaphoreType.DMA((N_DEV - 1,)), pltpu.SemaphoreType.DMA],
    )(x)


MOD_ROWS = 16
CTX_ROW = N_DEV


def _mod_fwd(cond, w_shard, b_cols):
    n_layers, d, cols = w_shard.shape

    def body(c_ref, w_ref, b_ref, o_ref):
        c = c_ref[...]
        s = c * _sigmoid(c)
        for i in range(n_layers):
            o_ref[i] = _dot(s, w_ref[i]) + b_ref[i]

    return pl.pallas_call(
        body, name="mod_fwd", out_shape=_sds((n_layers, MOD_ROWS, cols)),
        compiler_params=pltpu.CompilerParams(vmem_limit_bytes=VMEM_LIMIT),
    )(cond, w_shard, b_cols.reshape(n_layers, 1, cols))


def _mod_bwd(cond, d_lat_cols, d_ctx_cols, w_shard):
    n_layers, d, cols = w_shard.shape

    def body(c_ref, dl_ref, dc_ref, w_ref, dw_ref, dcc_ref):
        c = c_ref[...]
        sg = _sigmoid(c)
        s = c * sg
        d_s = jnp.zeros((MOD_ROWS, d), F32)
        for i in range(n_layers):
            d_ctx = dc_ref[0, i]
            for j in range(1, N_DEV):
                d_ctx = d_ctx + dc_ref[j, i]
            dm = jnp.concatenate([dl_ref[i], d_ctx, jnp.zeros((MOD_ROWS - N_DEV - 1, cols), F32)], axis=0)
            dw_ref[i] = _dot_t0(s, dm)
            d_s = d_s + _dot_t1(dm, w_ref[i])
        d_c = d_s * (sg * (1.0 + c * (1.0 - sg)))
        dcc_ref[...] = d_c[CTX_ROW:CTX_ROW + 1]

    return pl.pallas_call(
        body, name="mod_bwd", out_shape=[_sds((n_layers, d, cols)), _sds((1, d))],
        compiler_params=pltpu.CompilerParams(vmem_limit_bytes=VMEM_LIMIT),
    )(cond, d_lat_cols, d_ctx_cols, w_shard)


ADAM_TILE = 512


def _adamw(w, g_parts, m, v, name):
    n_parts, n_rows, lanes = g_parts.shape
    c1 = 1.0 - ADAM_B1 ** ADAM_STEP
    c2 = 1.0 - ADAM_B2 ** ADAM_STEP

    def body(w_ref, g_ref, m_ref, v_ref, go_ref, d_ref, mo_ref, vo_ref):
        g = g_ref[0].astype(F32)
        for p in range(1, n_parts):
            g = g + g_ref[p].astype(F32)
        m_new = ADAM_B1 * m_ref[...] + (1.0 - ADAM_B1) * g
        v_new = ADAM_B2 * v_ref[...] + (1.0 - ADAM_B2) * (g * g)
        go_ref[...] = g
        mo_ref[...] = m_new
        vo_ref[...] = v_new
        d_ref[...] = -ADAM_LR * ((m_new / c1) / (jnp.sqrt(v_new / c2) + ADAM_EPS) + ADAM_WD * w_ref[...])

    row = pl.BlockSpec((ADAM_TILE, lanes), lambda i: (i, 0))
    return pl.pallas_call(
        body, name=name, grid=(n_rows // ADAM_TILE,),
        in_specs=[row, pl.BlockSpec((n_parts, ADAM_TILE, lanes), lambda i: (0, i, 0)), row, row],
        out_specs=[row] * 4, out_shape=[_sds((n_rows, lanes))] * 4,
        compiler_params=_cparams(1),
    )(w, g_parts, m, v)


def _sum_parts(parts):
    n_parts, n_rows, lanes = parts.shape

    def body(p_ref, o_ref):
        acc = p_ref[0]
        for p in range(1, n_parts):
            acc = acc + p_ref[p]
        o_ref[...] = acc

    return pl.pallas_call(body, name="sum_parts", out_shape=_sds((n_rows, lanes)))(parts)


def _pack(arrays, row_multiple):
    parts = []
    for a in arrays:
        flat = a.reshape(-1)
        parts.append(jnp.pad(flat, (0, (-flat.shape[0]) % 1024)))
    flat = jnp.concatenate(parts)
    flat = jnp.pad(flat, (0, (-flat.shape[0]) % (row_multiple * 128)))
    return flat.reshape(-1, 128)


def _unpack(packed, shapes):
    flat = packed.reshape(-1)
    out, pos = [], 0
    for s in shapes:
        n = math.prod(s)
        out.append(flat[pos:pos + n].reshape(s))
        pos += n + (-n) % 1024
    return out


WEIGHT_NAMES = ['c_ctx', 'w_mod', 'b_mod', 'norm_g', 'ssm_w_in', 'ssm_a_re', 'ssm_a_im', 'ssm_log_dt', 'ssm_b_re', 'ssm_b_im',
                'ssm_c_re', 'ssm_c_im', 'ssm_d', 'ssm_w_glu', 'ssm_b_glu', 'ssm_w_out', 'attn_w_in', 'attn_q_norm',
                'attn_k_norm', 'attn_w_out', 'final_norm_g']
SHARDED = ['ssm_w_in', 'ssm_w_glu', 'ssm_w_out', 'attn_w_in', 'attn_w_out']
COLUMN_SHARDED = ('ssm_w_in', 'attn_w_in')
REPLICATED = ['c_ctx', 'b_mod', 'norm_g', 'ssm_a_re', 'ssm_a_im', 'ssm_log_dt', 'ssm_b_re', 'ssm_b_im', 'ssm_c_re', 'ssm_c_im',
              'ssm_d', 'ssm_b_glu', 'attn_q_norm', 'attn_k_norm', 'final_norm_g']
SSM_NAMES = ['ssm_a_re', 'ssm_a_im', 'ssm_log_dt', 'ssm_b_re', 'ssm_b_im', 'ssm_c_re', 'ssm_c_im', 'ssm_d']


def _full_from_shards(gathered, name, shard_shape):
    rows, cols = shard_shape
    w = gathered.reshape(N_DEV, rows, cols)
    if name in COLUMN_SHARDED:
        return w.transpose(1, 0, 2).reshape(rows, N_DEV * cols)
    return w.reshape(N_DEV * rows, cols)


def _shards_from_full(g, name):
    if name in COLUMN_SHARDED:
        rows, cols = g.shape
        g = g.reshape(rows, N_DEV, cols // N_DEV).transpose(1, 0, 2)
    return g.reshape(N_DEV, -1, 128)


def kernel(x, c, ctx, c_ctx, w_mod, b_mod, norm_g, ssm_w_in, ssm_a_re, ssm_a_im, ssm_log_dt, ssm_b_re, ssm_b_im, ssm_c_re, ssm_c_im, ssm_d, ssm_w_glu, ssm_b_glu, ssm_w_out, attn_w_in, attn_q_norm, attn_k_norm, attn_w_out, final_norm_g, loss_target, m_c_ctx, m_w_mod, m_b_mod, m_norm_g, m_ssm_w_in, m_ssm_a_re, m_ssm_a_im, m_ssm_log_dt, m_ssm_b_re, m_ssm_b_im, m_ssm_c_re, m_ssm_c_im, m_ssm_d, m_ssm_w_glu, m_ssm_b_glu, m_ssm_w_out, m_attn_w_in, m_attn_q_norm, m_attn_k_norm, m_attn_w_out, m_final_norm_g, v_c_ctx, v_w_mod, v_b_mod, v_norm_g, v_ssm_w_in, v_ssm_a_re, v_ssm_a_im, v_ssm_log_dt, v_ssm_b_re, v_ssm_b_im, v_ssm_c_re, v_ssm_c_im, v_ssm_d, v_ssm_w_glu, v_ssm_b_glu, v_ssm_w_out, v_attn_w_in, v_attn_q_norm, v_attn_k_norm, v_attn_w_out, v_final_norm_g):
    env = dict(locals())
    weights = {n: env[n] for n in WEIGHT_NAMES}
    mom_m = {n: env["m_" + n] for n in WEIGHT_NAMES}
    mom_v = {n: env["v_" + n] for n in WEIGHT_NAMES}
    d = D_MODEL
    me = _my_index()
    mod_cols = w_mod.shape[-1]

    c_all = _exchange(c.reshape(8, d // 8), "gather_c", False).reshape(N_DEV, d)
    cond = jnp.concatenate([c_all, c_ctx.reshape(1, d), jnp.zeros((MOD_ROWS - N_DEV - 1, d), F32)], axis=0)
    shard_shapes = {n: weights[n].shape[1:] for n in SHARDED}
    w_gathered = _exchange(_pack([weights[n] for n in SHARDED], 1).astype(BF16), "gather_weights", False)
    full, pos = {}, 0
    for n in SHARDED:
        rows = math.prod(shard_shapes[n]) // 128
        full[n] = _full_from_shards(w_gathered[:, pos:pos + rows], n, shard_shapes[n])
        pos += rows

    b_cols = lax.dynamic_slice(b_mod, (0, me * mod_cols), (2, mod_cols))
    mod_shard = _mod_fwd(cond, w_mod, b_cols)
    mod_all = _exchange(mod_shard.reshape(2 * MOD_ROWS, mod_cols), "gather_mod", False)
    mod_full = mod_all.reshape(N_DEV, 2, MOD_ROWS, mod_cols).transpose(1, 2, 0, 3).reshape(2, MOD_ROWS, 3 * d)
    lat_rows = lax.dynamic_slice(mod_full, (0, me, 0), (2, 1, 3 * d))
    mods = []
    for i in range(2):
        seg = jnp.stack([mod_full[i, CTX_ROW:CTX_ROW + 1], lat_rows[i]])
        mods.append((seg[:, :, :d], seg[:, :, d:2 * d], seg[:, :, 2 * d:]))

    ssm = tuple(weights[n][0] for n in SSM_NAMES)
    loss, grad_x, g, d_mods = _local_step(
        x[0], ctx[0], loss_target[0], mods, norm_g, ssm, full['ssm_w_in'], full['ssm_w_glu'], ssm_b_glu[0], full['ssm_w_out'],
        full['attn_w_in'], attn_q_norm[0], attn_k_norm[0], full['attn_w_out'], final_norm_g)
    loss = lax.psum(loss, ("x", "y", "c"))

    d_rows = jnp.stack([jnp.concatenate(dm, axis=-1) for dm in d_mods])
    d_rows = jnp.concatenate([d_rows.reshape(4, 3 * d), jnp.zeros((4, 3 * d), F32)], axis=0)
    d_all = _exchange(d_rows, "gather_dmod", False)[:, :4].reshape(N_DEV, 2, 2, 3 * d)
    d_all = lax.dynamic_slice(d_all, (0, 0, 0, me * mod_cols), (N_DEV, 2, 2, mod_cols))
    d_w_mod, d_c_ctx = _mod_bwd(cond, d_all[:, :, 1].transpose(1, 0, 2), d_all[:, :, 0:1], w_mod)
    d_b_mod = jnp.stack([jnp.concatenate([t[0] + t[1] for t in dm], axis=-1).reshape(3 * d) for dm in d_mods])

    g_big = jnp.concatenate([_shards_from_full(g[n], n) for n in SHARDED], axis=1)
    g_big_parts = _exchange(g_big.astype(BF16), "scatter_grads", True)
    pack_big = lambda t: _pack([t[n] for n in SHARDED], ADAM_TILE)
    big = _adamw(pack_big(weights), g_big_parts, pack_big(mom_m), pack_big(mom_v), "adamw_sharded")
    big = [_unpack(t, [weights[n].shape for n in SHARDED]) for t in big]

    mod_res = _adamw(_pack([w_mod], ADAM_TILE), _pack([d_w_mod], ADAM_TILE)[None], _pack([m_w_mod], ADAM_TILE),
                     _pack([v_w_mod], ADAM_TILE), "adamw_w_mod")
    mod_res = [t.reshape(w_mod.shape) for t in mod_res]

    small = dict(zip(SSM_NAMES, g['ssm']))
    small.update(c_ctx=d_c_ctx, b_mod=d_b_mod, norm_g=g['norm_g'], ssm_b_glu=g['ssm_b_glu'], attn_q_norm=g['attn_q_norm'],
                 attn_k_norm=g['attn_k_norm'], final_norm_g=g['final_norm_g'])
    pack_small = lambda t: _pack([t[n] for n in REPLICATED], ADAM_TILE)
    g_small = pack_small(small)
    slices = _exchange(g_small.reshape(N_DEV, -1, 128), "scatter_small_grads", True)
    g_small = _exchange(_sum_parts(slices), "gather_small_grads", False).reshape(1, -1, 128)
    rep = _adamw(pack_small(weights), g_small, pack_small(mom_m), pack_small(mom_v), "adamw_replicated")
    rep = [_unpack(t, [weights[n].shape for n in REPLICATED]) for t in rep]

    results = []
    for kind in range(4):
        by_name = dict(zip(SHARDED, big[kind]))
        by_name.update(zip(REPLICATED, rep[kind]))
        by_name['w_mod'] = mod_res[kind]
        results.extend(by_name[n] for n in WEIGHT_NAMES)
    return (loss, grad_x[None], *results)
```

```python
import functools
import math

import jax
import jax.numpy as jnp
from jax import lax
from jax.experimental import pallas as pl
from jax.experimental.pallas import tpu as pltpu

F32 = jnp.float32
BF16 = jnp.bfloat16

N_DEV = 8
D_MODEL = 1024
NORM_EPS = 1e-6
SSM_GROUP = 16
SSM_GROUPS = 64
SSM_STATE = 64
GROUPS_PER_BLOCK = 8
N_BLOCKS = SSM_GROUPS // GROUPS_PER_BLOCK
HALF = GROUPS_PER_BLOCK * SSM_STATE
HEAD_DIM = 64
N_Q_HEADS = 16
N_KV_HEADS = 4
KV_REP = N_Q_HEADS // N_KV_HEADS
KV_WIDTH = N_KV_HEADS * HEAD_DIM
GRID_W = 64
ROPE_THETA = 10000.0
ADAM_LR, ADAM_B1, ADAM_B2, ADAM_EPS, ADAM_WD, ADAM_STEP = 0.001, 0.9, 0.999, 1e-08, 0.01, 10

ROW_TILE = 256
SCAN_CHUNK = 256
VMEM_LIMIT = 56 * 1024 * 1024
MESH_IDS = pl.DeviceIdType.MESH


def _cparams(n_axes):
    return pltpu.CompilerParams(dimension_semantics=("arbitrary",) * n_axes, vmem_limit_bytes=VMEM_LIMIT)


def _dot(a, b):
    return jnp.dot(a.astype(BF16), b.astype(BF16), preferred_element_type=F32)


def _dot_t0(a, b):
    return lax.dot_general(a.astype(BF16), b.astype(BF16), (((0,), (0,)), ((), ())), preferred_element_type=F32)


def _dot_t1(a, b):
    return lax.dot_general(a.astype(BF16), b.astype(BF16), (((1,), (1,)), ((), ())), preferred_element_type=F32)


def _s5_prep(a_re, a_im, log_dt, b_re, b_im):
    dt = jnp.exp(log_dt)[:, None]
    ldr, ldi = a_re * dt, a_im * dt
    mag = jnp.exp(ldr)
    abar_re, abar_im = mag * jnp.cos(ldi), mag * jnp.sin(ldi)
    den = a_re * a_re + a_im * a_im
    num_re, num_im = abar_re - 1.0, abar_im
    coef_re = (num_re * a_re + num_im * a_im) / den
    coef_im = (num_im * a_re - num_re * a_im) / den
    bbar_re = coef_re[..., None] * b_re - coef_im[..., None] * b_im
    bbar_im = coef_re[..., None] * b_im + coef_im[..., None] * b_re
    return abar_re, abar_im, bbar_re, bbar_im


def _s5_blocks(abar_re, abar_im, bbar_re, bbar_im, c_re, c_im):
    eye = jnp.eye(GROUPS_PER_BLOCK, dtype=F32)
    bb = jnp.stack([bbar_re, bbar_im]).reshape(2, N_BLOCKS, GROUPS_PER_BLOCK, SSM_STATE, SSM_GROUP)
    b_blk = jnp.einsum('rqgph,gk->qghrkp', bb, eye).reshape(N_BLOCKS, 128, 2 * HALF)
    cc = jnp.stack([c_re, -c_im]).reshape(2, N_BLOCKS, GROUPS_PER_BLOCK, SSM_GROUP, SSM_STATE)
    c_blk = jnp.einsum('rqghp,gk->qrgpkh', cc, eye).reshape(N_BLOCKS, 2 * HALF, 128)
    abar = jnp.stack([abar_re.reshape(N_BLOCKS, HALF), abar_im.reshape(N_BLOCKS, HALF)])
    return abar, b_blk, c_blk


def _s5_unblock(d_b_blk, d_ct_blk):
    db = d_b_blk.reshape(N_BLOCKS, GROUPS_PER_BLOCK, SSM_GROUP, 2, GROUPS_PER_BLOCK, SSM_STATE)
    db = jnp.einsum('qghrgp->rqgph', db).reshape(2, SSM_GROUPS, SSM_STATE, SSM_GROUP)
    dc = d_ct_blk.reshape(N_BLOCKS, GROUPS_PER_BLOCK, SSM_GROUP, 2, GROUPS_PER_BLOCK, SSM_STATE)
    dc = jnp.einsum('qghrgp->rqghp', dc).reshape(2, SSM_GROUPS, SSM_GROUP, SSM_STATE)
    return db[0], db[1], dc[0], -dc[1]


def _scan_chunk_of_step(j, n_chunks, n_ctx_chunks, reverse):
    if not reverse:
        return j
    return jnp.where(j < n_ctx_chunks, n_ctx_chunks - 1 - j, n_chunks - 1 - j + n_ctx_chunks)


LANE_TILES = 2 * HALF // 128
RE_TILES = HALF // 128


def _tiles(v):
    return [v[:, l * 128:(l + 1) * 128] for l in range(v.shape[1] // 128)]


def _scatter_steps(s_ref, q, x):
    for l in range(LANE_TILES):
        s_ref[l, pl.ds(q, x.shape[0], stride=N_BLOCKS), :] = x[:, l * 128:(l + 1) * 128]


def _gather_steps(s_ref, q, n_steps):
    return jnp.concatenate([s_ref[l, pl.ds(q, n_steps, stride=N_BLOCKS), :] for l in range(LANE_TILES)], axis=1)


def _load_step(s_ref, t):
    row = pl.multiple_of(t * N_BLOCKS, N_BLOCKS)
    return [s_ref[l, pl.ds(row, N_BLOCKS), :] for l in range(LANE_TILES)]


def _store_step(s_ref, t, tiles):
    row = pl.multiple_of(t * N_BLOCKS, N_BLOCKS)
    for l in range(LANE_TILES):
        s_ref[l, pl.ds(row, N_BLOCKS), :] = tiles[l]


def _cmul_add(a, h, x, conj):
    re, im = [], []
    for l in range(RE_TILES):
        ar, ai, hr, hi = a[l], a[RE_TILES + l], h[l], h[RE_TILES + l]
        if conj:
            re.append(ar * hr + ai * hi + x[l])
            im.append(ar * hi - ai * hr + x[RE_TILES + l])
        else:
            re.append(ar * hr - ai * hi + x[l])
            im.append(ar * hi + ai * hr + x[RE_TILES + l])
    return re + im


def _s5_scan_fwd(u, abar, b_blk, c_blk, d_skip, n_ctx, reverse):
    n_rows, width = u.shape
    tc = SCAN_CHUNK
    n_chunks, n_ctx_chunks = n_rows // tc, n_ctx // tc
    with_skip = d_skip is not None

    def body(*refs):
        if with_skip:
            u_ref, a_ref, b_ref, c_ref, d_ref, y_ref, hb_ref, s_ref, h_ref = refs
        else:
            u_ref, a_ref, b_ref, c_ref, y_ref, hb_ref, s_ref, h_ref = refs
        j = pl.program_id(0)

        @pl.when(j == 0)
        def _():
            h_ref[...] = jnp.zeros_like(h_ref)

        hb_ref[0] = h_ref[...]
        for q in range(N_BLOCKS):
            _scatter_steps(s_ref, q, _dot(u_ref[:, q * 128:(q + 1) * 128], b_ref[q]))
        a = _tiles(a_ref[0]) + _tiles(a_ref[1])

        def step(s, h):
            t = tc - 1 - s if reverse else s
            h = _cmul_add(a, h, _load_step(s_ref, t), conj=False)
            _store_step(s_ref, t, h)
            return h

        h = lax.fori_loop(0, tc, step, _tiles(h_ref[...]))
        h_ref[...] = jnp.concatenate(h, axis=1)
        for q in range(N_BLOCKS):
            yq = _dot(_gather_steps(s_ref, q, tc), c_ref[q])
            if with_skip:
                yq = yq + d_ref[:, q * 128:(q + 1) * 128] * u_ref[:, q * 128:(q + 1) * 128]
            y_ref[:, q * 128:(q + 1) * 128] = yq

    chunk = functools.partial(_scan_chunk_of_step, n_chunks=n_chunks, n_ctx_chunks=n_ctx_chunks, reverse=reverse)
    full3 = lambda j: (0, 0, 0)
    in_specs = [pl.BlockSpec((tc, width), lambda j: (chunk(j), 0)),
                pl.BlockSpec((2, N_BLOCKS, HALF), full3),
                pl.BlockSpec((N_BLOCKS, 128, 2 * HALF), full3),
                pl.BlockSpec((N_BLOCKS, 2 * HALF, 128), full3)]
    args = [u, abar, b_blk.astype(BF16), c_blk.astype(BF16)]
    if with_skip:
        in_specs.append(pl.BlockSpec((1, width), lambda j: (0, 0)))
        args.append(d_skip.reshape(1, width))
    return pl.pallas_call(
        body, name="s5_scan_fwd_rev" if reverse else "s5_scan_fwd",
        grid=(n_chunks,),
        in_specs=in_specs,
        out_specs=[pl.BlockSpec((tc, width), lambda j: (chunk(j), 0)),
                   pl.BlockSpec((1, N_BLOCKS, 2 * HALF), lambda j: (chunk(j), 0, 0))],
        out_shape=[jax.ShapeDtypeStruct((n_rows, width), F32),
                   jax.ShapeDtypeStruct((n_chunks, N_BLOCKS, 2 * HALF), F32)],
        scratch_shapes=[pltpu.VMEM((LANE_TILES, tc * N_BLOCKS, 128), F32), pltpu.VMEM((N_BLOCKS, 2 * HALF), F32)],
        compiler_params=_cparams(1),
    )(*args)


def _s5_scan_bwd(u, dy, hb, abar, b_blk, c_blk, d_skip, n_ctx, reverse):
    n_rows, width = u.shape
    tc = SCAN_CHUNK
    n_chunks, n_ctx_chunks = n_rows // tc, n_ctx // tc
    with_skip = d_skip is not None

    def body(*refs):
        if with_skip:
            (u_ref, dy_ref, hb_ref, a_ref, b_ref, bt_ref, ct_ref, d_ref,
             du_ref, da_ref, db_ref, dct_ref, dd_ref, sh_ref, sg_ref, g_ref) = refs
        else:
            (u_ref, dy_ref, hb_ref, a_ref, b_ref, bt_ref, ct_ref,
             du_ref, da_ref, db_ref, dct_ref, sh_ref, sg_ref, g_ref) = refs
        j = pl.program_id(0)

        @pl.when(j == 0)
        def _():
            g_ref[...] = jnp.zeros_like(g_ref)
            da_ref[...] = jnp.zeros_like(da_ref)
            db_ref[...] = jnp.zeros_like(db_ref)
            dct_ref[...] = jnp.zeros_like(dct_ref)
            if with_skip:
                dd_ref[...] = jnp.zeros_like(dd_ref)

        for q in range(N_BLOCKS):
            _scatter_steps(sh_ref, q, _dot(u_ref[:, q * 128:(q + 1) * 128], b_ref[q]))
            _scatter_steps(sg_ref, q, _dot(dy_ref[:, q * 128:(q + 1) * 128], ct_ref[q]))
        a = _tiles(a_ref[0]) + _tiles(a_ref[1])
        time_of = (lambda s: tc - 1 - s) if reverse else (lambda s: s)

        def fwd_step(s, h):
            h = _cmul_add(a, h, _load_step(sh_ref, time_of(s)), conj=False)
            _store_step(sh_ref, time_of(s), h)
            return h

        h0 = _tiles(hb_ref[0])
        lax.fori_loop(0, tc, fwd_step, h0)

        def adj(t, h_prev, carry):
            g, da = carry
            g = _cmul_add(a, g, _load_step(sg_ref, t), conj=True)
            _store_step(sg_ref, t, g)
            da_re = [da[l] + g[l] * h_prev[l] + g[RE_TILES + l] * h_prev[RE_TILES + l] for l in range(RE_TILES)]
            da_im = [da[RE_TILES + l] + g[RE_TILES + l] * h_prev[l] - g[l] * h_prev[RE_TILES + l] for l in range(RE_TILES)]
            return g, da_re + da_im

        def bwd_step(i, carry):
            s = tc - 1 - i
            return adj(time_of(s), _load_step(sh_ref, time_of(s - 1)), carry)

        carry = (_tiles(g_ref[...]), _tiles(da_ref[0]) + _tiles(da_ref[1]))
        carry = lax.fori_loop(0, tc - 1, bwd_step, carry)
        g, da = adj(time_of(0), h0, carry)
        g_ref[...] = jnp.concatenate(g, axis=1)
        da_ref[0] = jnp.concatenate(da[:RE_TILES], axis=1)
        da_ref[1] = jnp.concatenate(da[RE_TILES:], axis=1)

        for q in range(N_BLOCKS):
            cols = slice(q * 128, (q + 1) * 128)
            uq, dyq = u_ref[:, cols], dy_ref[:, cols]
            gq = _gather_steps(sg_ref, q, tc)
            duq = _dot(gq, bt_ref[q])
            if with_skip:
                duq = duq + d_ref[:, cols] * dyq
                dd_ref[:, cols] += jnp.sum(dyq * uq, axis=0, keepdims=True)
            du_ref[:, cols] = duq
            db_ref[q] += _dot_t0(uq, gq)
            dct_ref[q] += _dot_t0(dyq, _gather_steps(sh_ref, q, tc))

    def chunk(j):
        return _scan_chunk_of_step(n_chunks - 1 - j, n_chunks, n_ctx_chunks, reverse)

    full2 = lambda j: (0, 0)
    full3 = lambda j: (0, 0, 0)
    row = pl.BlockSpec((tc, width), lambda j: (chunk(j), 0))
    in_specs = [row, row,
                pl.BlockSpec((1, N_BLOCKS, 2 * HALF), lambda j: (chunk(j), 0, 0)),
                pl.BlockSpec((2, N_BLOCKS, HALF), full3),
                pl.BlockSpec((N_BLOCKS, 128, 2 * HALF), full3),
                pl.BlockSpec((N_BLOCKS, 2 * HALF, 128), full3),
                pl.BlockSpec((N_BLOCKS, 128, 2 * HALF), full3)]
    args = [u, dy, hb, abar, b_blk.astype(BF16), jnp.swapaxes(b_blk, 1, 2).astype(BF16),
            jnp.swapaxes(c_blk, 1, 2).astype(BF16)]
    out_specs = [row,
                 pl.BlockSpec((2, N_BLOCKS, HALF), full3),
                 pl.BlockSpec((N_BLOCKS, 128, 2 * HALF), full3),
                 pl.BlockSpec((N_BLOCKS, 128, 2 * HALF), full3)]
    out_shape = [jax.ShapeDtypeStruct((n_rows, width), F32),
                 jax.ShapeDtypeStruct((2, N_BLOCKS, HALF), F32),
                 jax.ShapeDtypeStruct((N_BLOCKS, 128, 2 * HALF), F32),
                 jax.ShapeDtypeStruct((N_BLOCKS, 128, 2 * HALF), F32)]
    if with_skip:
        in_specs.append(pl.BlockSpec((1, width), full2))
        args.append(d_skip.reshape(1, width))
        out_specs.append(pl.BlockSpec((1, width), full2))
        out_shape.append(jax.ShapeDtypeStruct((1, width), F32))
    return pl.pallas_call(
        body, name="s5_scan_bwd_rev" if reverse else "s5_scan_bwd",
        grid=(n_chunks,),
        in_specs=in_specs, out_specs=out_specs, out_shape=out_shape,
        scratch_shapes=[pltpu.VMEM((LANE_TILES, tc * N_BLOCKS, 128), F32), pltpu.VMEM((LANE_TILES, tc * N_BLOCKS, 128), F32),
                        pltpu.VMEM((N_BLOCKS, 2 * HALF), F32)],
        compiler_params=_cparams(1),
    )(*args)


def _s5_dir_params(d, a_re, a_im, log_dt, b_re, b_im):
    return a_re[d], a_im[d], log_dt[d], b_re[d], b_im[d]


def _s5_forward(u, ssm, n_ctx):
    a_re, a_im, log_dt, b_re, b_im, c_re, c_im, d_skip = ssm
    outs, saved = [], []
    for d in range(2):
        prep = _s5_prep(*_s5_dir_params(d, a_re, a_im, log_dt, b_re, b_im))
        abar, b_blk, c_blk = _s5_blocks(*prep, c_re[d], c_im[d])
        y, hb = _s5_scan_fwd(u, abar, b_blk, c_blk, d_skip if d == 0 else None, n_ctx, reverse=(d == 1))
        outs.append(y)
        saved.append((hb, abar, b_blk, c_blk))
    return outs, saved


def _s5_backward(u, dy, ssm, saved, n_ctx):
    a_re, a_im, log_dt, b_re, b_im, c_re, c_im, d_skip = ssm
    dus, grads = [], [[] for _ in range(7)]
    d_d = None
    for d in range(2):
        hb, abar, b_blk, c_blk = saved[d]
        res = _s5_scan_bwd(u, dy, hb, abar, b_blk, c_blk, d_skip if d == 0 else None, n_ctx, reverse=(d == 1))
        if d == 0:
            du, d_abar, d_b_blk, d_ct_blk, d_d = res
        else:
            du, d_abar, d_b_blk, d_ct_blk = res
        dus.append(du)
        dbb_re, dbb_im, dc_re, dc_im = _s5_unblock(d_b_blk, d_ct_blk)
        _, vjp = jax.vjp(_s5_prep, *_s5_dir_params(d, a_re, a_im, log_dt, b_re, b_im))
        shape = (SSM_GROUPS, SSM_STATE)
        g5 = vjp((d_abar[0].reshape(shape), d_abar[1].reshape(shape), dbb_re, dbb_im))
        for k, g in enumerate(tuple(g5) + (dc_re, dc_im)):
            grads[k].append(g)
    grads = [jnp.stack(g) for g in grads]
    return dus, grads + [d_d.reshape(-1)]


INV_SQRT2 = 0.7071067811865476
INV_SQRT_2PI = 0.3989422804014327


def _rows(cols):
    return pl.BlockSpec((ROW_TILE, cols), lambda i: (i, 0))


def _rows_skip_ctx(cols):
    return pl.BlockSpec((ROW_TILE, cols), lambda i: (i + 1, 0))


def _rows_lat(cols):
    return pl.BlockSpec((ROW_TILE, cols), lambda i: (jnp.maximum(i - 1, 0), 0))


def _full(shape):
    nd = len(shape)
    return pl.BlockSpec(shape, lambda i: (0,) * nd)


def _seg(cols):
    return pl.BlockSpec((1, 1, cols), lambda i: (jnp.minimum(i, 1), 0, 0))


def _lat_seg(cols):
    return pl.BlockSpec((1, 1, cols), lambda i: (1, 0, 0))


def _sds(shape, dtype=F32):
    return jax.ShapeDtypeStruct(shape, dtype)


def _sum0(x):
    return jnp.sum(x, axis=0, keepdims=True)


def _sigmoid(x):
    return jax.nn.sigmoid(x)


def _rms_mod(x, g, scale, shift):
    r = lax.rsqrt(jnp.mean(x * x, axis=-1, keepdims=True) + NORM_EPS)
    return (x * r * g) * (1.0 + scale) + shift


def _rms_mod_bwd(x, g, scale, dh):
    r = lax.rsqrt(jnp.mean(x * x, axis=-1, keepdims=True) + NORM_EPS)
    n = x * r
    dyg = dh * (1.0 + scale)
    dn = dyg * g
    dx = r * (dn - n * jnp.mean(dn * n, axis=-1, keepdims=True))
    return dx, _sum0(dyg * n), _sum0(dh * (n * g)), _sum0(dh)


def _head_of_lane(width):
    return (jnp.arange(width)[:, None] // HEAD_DIM == jnp.arange(128)[None, :]).astype(BF16)


def _split_dot(t, w, transposed):
    hi = t.astype(BF16)
    lo = (t - hi.astype(F32)).astype(BF16)
    f = _dot_t1 if transposed else _dot
    return f(hi, w) + f(lo, w)


def _head_sums(t, hl):
    return _split_dot(_split_dot(t, hl, False), hl, True)


def _rope_partner(x):
    n = x.shape[1]
    lane = lax.broadcasted_iota(jnp.int32, x.shape, 1)
    return jnp.where((lane & 16) == 0, pltpu.roll(x, n - 16, 1), pltpu.roll(x, 16, 1))


def _lanes(tab, width):
    return jnp.tile(tab, (1, width // tab.shape[1]))


def _head_norm_rope(x, gain, cos, sin, hl):
    r = lax.rsqrt(_head_sums(x * x, hl) * (1.0 / HEAD_DIM) + NORM_EPS)
    y = x * r * gain
    return y * cos + _rope_partner(y) * sin


def _head_norm_rope_bwd(x, gain, cos, sin, hl, dout):
    dy = dout * cos + _rope_partner(dout * sin)
    r = lax.rsqrt(_head_sums(x * x, hl) * (1.0 / HEAD_DIM) + NORM_EPS)
    n = x * r
    dn = dy * gain
    dx = r * (dn - n * (_head_sums(dn * n, hl) * (1.0 / HEAD_DIM)))
    return dx, _sum0(dy * n)


def _rope_tables(n_ctx, n_lat):
    t = jnp.arange(n_lat)
    pos = jnp.stack([(t // GRID_W).astype(F32), (t % GRID_W).astype(F32)], axis=1)
    n_freq = HEAD_DIM // 4
    freqs = ROPE_THETA ** (-jnp.arange(n_freq, dtype=F32) / n_freq)
    ang = pos[:, :, None] * freqs[None, None, :]
    cos = jnp.repeat(jnp.cos(ang)[:, :, None, :], 2, axis=2).reshape(n_lat, HEAD_DIM)
    sin = jnp.sin(ang)
    sin = jnp.stack([-sin, sin], axis=2).reshape(n_lat, HEAD_DIM)
    cos = jnp.concatenate([jnp.ones((n_ctx, HEAD_DIM), F32), cos], axis=0)
    sin = jnp.concatenate([jnp.zeros((n_ctx, HEAD_DIM), F32), sin], axis=0)
    return jnp.tile(cos, (1, 2)), jnp.tile(sin, (1, 2))


def _ssm_in(xa, g, scale, shift, w_in):
    n_rows, d = xa.shape
    e = w_in.shape[1] // 2

    def body(x_ref, g_ref, sc_ref, sh_ref, w_ref, u_ref, z_ref):
        h = _rms_mod(x_ref[...], g_ref[...], sc_ref[0], sh_ref[0])
        proj = _dot(h, w_ref[...])
        u_ref[...] = proj[:, :e]
        z_ref[...] = proj[:, e:]

    return pl.pallas_call(
        body, name="ssm_in", grid=(n_rows // ROW_TILE,),
        in_specs=[_rows(d), _full((1, d)), _seg(d), _seg(d), _full(w_in.shape)],
        out_specs=[_rows(e), _rows(e)], out_shape=[_sds((n_rows, e)), _sds((n_rows, e))],
        compiler_params=_cparams(1),
    )(xa, g, scale, shift, w_in)


def _s5_post_math(y, z, w_glu, b_glu, w_out):
    er = lax.erf(y * INV_SQRT2)
    g = 0.5 * y * (1.0 + er)
    sg = _sigmoid(_dot(g, w_glu) + b_glu)
    g2 = g * sg
    sz = _sigmoid(z)
    silu_z = z * sz
    m = g2 * silu_z
    return er, g, sg, g2, sz, silu_z, m, _dot(m, w_out)


def _ssm_post(xa, y0, y1, z, gate, w_glu, b_glu, w_out):
    n_rows, d = xa.shape
    e = z.shape[1]

    def body(x_ref, y0_ref, y1_ref, z_ref, gt_ref, wg_ref, bg_ref, wo_ref, o_ref):
        out = _s5_post_math(y0_ref[...] + y1_ref[...], z_ref[...], wg_ref[...], bg_ref[...], wo_ref[...])[-1]
        o_ref[...] = x_ref[...] + gt_ref[0] * out

    return pl.pallas_call(
        body, name="ssm_post", grid=(n_rows // ROW_TILE,),
        in_specs=[_rows(d), _rows(e), _rows(e), _rows(e), _seg(d), _full(w_glu.shape), _full((1, e)), _full(w_out.shape)],
        out_specs=_rows(d), out_shape=_sds((n_rows, d)),
        compiler_params=_cparams(1),
    )(xa, y0, y1, z, gate, w_glu, b_glu, w_out)


def _init_acc(first, *refs):
    @pl.when(first)
    def _():
        for r in refs:
            r[...] = jnp.zeros_like(r)


def _ssm_post_bwd(dxa, y0, y1, z, gate, w_glu, b_glu, w_out, w_glu_t, w_out_t):
    n_rows, d = dxa.shape
    e = z.shape[1]

    def body(dx_ref, y0_ref, y1_ref, z_ref, gt_ref, wg_ref, bg_ref, wo_ref, wgt_ref, wot_ref,
             dy_ref, dz_ref, dgt_ref, dwo_ref, dwg_ref, dbg_ref):
        i = pl.program_id(0)
        _init_acc(i == 0, dwo_ref, dwg_ref, dbg_ref)
        _init_acc(i <= 1, dgt_ref)
        y, zz = y0_ref[...] + y1_ref[...], z_ref[...]
        er, g, sg, g2, sz, silu_z, m, out = _s5_post_math(y, zz, wg_ref[...], bg_ref[...], wo_ref[...])
        dxa_t = dx_ref[...]
        dgt_ref[0] += _sum0(dxa_t * out)
        dout = gt_ref[0] * dxa_t
        dm = _dot(dout, wot_ref[...])
        dwo_ref[...] += _dot_t0(m, dout)
        dg2 = dm * silu_z
        dz_ref[...] = dm * g2 * (sz * (1.0 + zz * (1.0 - sz)))
        dt = dg2 * g * sg * (1.0 - sg)
        dwg_ref[...] += _dot_t0(g, dt)
        dbg_ref[...] += _sum0(dt)
        dg = dg2 * sg + _dot(dt, wgt_ref[...])
        dy_ref[...] = dg * (0.5 * (1.0 + er) + y * jnp.exp(-0.5 * y * y) * INV_SQRT_2PI)

    return pl.pallas_call(
        body, name="ssm_post_bwd", grid=(n_rows // ROW_TILE,),
        in_specs=[_rows(d), _rows(e), _rows(e), _rows(e), _seg(d), _full(w_glu.shape), _full((1, e)), _full(w_out.shape),
                  _full(w_glu_t.shape), _full(w_out_t.shape)],
        out_specs=[_rows(e), _rows(e), _seg(d), _full(w_out.shape), _full(w_glu.shape), _full((1, e))],
        out_shape=[_sds((n_rows, e)), _sds((n_rows, e)), _sds((2, 1, d)), _sds(w_out.shape), _sds(w_glu.shape), _sds((1, e))],
        compiler_params=_cparams(1),
    )(dxa, y0, y1, z, gate, w_glu, b_glu, w_out, w_glu_t, w_out_t)


def _ssm_in_bwd(du0, du1, dz, xa, dxa_next, g, scale, shift, w_in_t):
    n_rows, d = xa.shape
    e = dz.shape[1]
    n_lat = n_rows - ROW_TILE

    def body(du0_ref, du1_ref, dz_ref, x_ref, dn_ref, g_ref, sc_ref, sh_ref, wt_ref,
             gx_ref, dw_ref, dg_ref, dsc_ref, dsh_ref):
        i = pl.program_id(0)
        _init_acc(i == 0, dw_ref, dg_ref)
        _init_acc(i <= 1, dsc_ref, dsh_ref)
        x = x_ref[...]
        h = _rms_mod(x, g_ref[...], sc_ref[0], sh_ref[0])
        dproj = jnp.concatenate([du0_ref[...] + du1_ref[...], dz_ref[...]], axis=1)
        dh = _dot(dproj, wt_ref[...])
        dw_ref[...] += _dot_t0(h, dproj)
        dx, dg, dsc, dsh = _rms_mod_bwd(x, g_ref[...], sc_ref[0], dh)
        dg_ref[...] += dg
        dsc_ref[0] += dsc
        dsh_ref[0] += dsh
        gx_ref[...] = dn_ref[...] + dx

    return pl.pallas_call(
        body, name="ssm_in_bwd", grid=(n_rows // ROW_TILE,),
        in_specs=[_rows(e), _rows(e), _rows(e), _rows(d), _rows(d), _full((1, d)), _seg(d), _seg(d), _full(w_in_t.shape)],
        out_specs=[_rows_lat(d), _full((d, 2 * e)), _full((1, d)), _seg(d), _seg(d)],
        out_shape=[_sds((n_lat, d)), _sds((d, 2 * e)), _sds((1, d)), _sds((2, 1, d)), _sds((2, 1, d))],
        compiler_params=_cparams(1),
    )(du0, du1, dz, xa, dxa_next, g, scale, shift, w_in_t)


Q_WIDTH = N_Q_HEADS * HEAD_DIM
SM_SCALE = 1.0 / math.sqrt(HEAD_DIM)


def _attn_in(xa, g, scale, shift, w_in, q_gain, k_gain, cos, sin):
    n_rows, d = xa.shape
    qk = Q_WIDTH + KV_WIDTH

    def body(x_ref, g_ref, sc_ref, sh_ref, w_ref, qg_ref, kg_ref, cos_ref, sin_ref, hq_ref, hk_ref,
             q_ref, k_ref, v_ref, z_ref, raw_ref):
        h = _rms_mod(x_ref[...], g_ref[...], sc_ref[0], sh_ref[0])
        proj = _dot(h, w_ref[...])
        q_raw, k_raw = proj[:, :Q_WIDTH], proj[:, Q_WIDTH:qk]
        cos, sin = cos_ref[...], sin_ref[...]
        q = _head_norm_rope(q_raw, qg_ref[...], _lanes(cos, Q_WIDTH), _lanes(sin, Q_WIDTH), hq_ref[...])
        k = _head_norm_rope(k_raw, kg_ref[...], _lanes(cos, KV_WIDTH), _lanes(sin, KV_WIDTH), hk_ref[...])
        q_ref[...] = (q * SM_SCALE).astype(BF16)
        k_ref[...] = k.astype(BF16)
        v_ref[...] = proj[:, qk:qk + KV_WIDTH].astype(BF16)
        z_ref[...] = proj[:, qk + KV_WIDTH:]
        raw_ref[...] = proj[:, :qk]

    return pl.pallas_call(
        body, name="attn_in", grid=(n_rows // ROW_TILE,),
        in_specs=[_rows(d), _full((1, d)), _seg(d), _seg(d), _full(w_in.shape), _full((1, Q_WIDTH)), _full((1, KV_WIDTH)),
                  _rows(128), _rows(128), _full((Q_WIDTH, 128)), _full((KV_WIDTH, 128))],
        out_specs=[_rows_lat(Q_WIDTH), _rows(KV_WIDTH), _rows(KV_WIDTH), _rows(Q_WIDTH), _rows(qk)],
        out_shape=[_sds((n_rows - ROW_TILE, Q_WIDTH), BF16), _sds((n_rows, KV_WIDTH), BF16), _sds((n_rows, KV_WIDTH), BF16),
                   _sds((n_rows, Q_WIDTH)), _sds((n_rows, qk))],
        compiler_params=_cparams(1),
    )(xa, g, scale, shift, w_in, q_gain, k_gain, cos, sin, _head_of_lane(Q_WIDTH), _head_of_lane(KV_WIDTH))


GROUP_WIDTH = KV_REP * HEAD_DIM


def _stack_heads(ref):
    return jnp.concatenate([ref[:, h * HEAD_DIM:(h + 1) * HEAD_DIM] for h in range(KV_REP)], axis=0)


def _unstack_heads(a_t, tq):
    return jnp.concatenate([a_t[:, h * tq:(h + 1) * tq].T for h in range(KV_REP)], axis=1)


def _kv_tile(n_keys):
    return 768 if n_keys % 768 == 0 else 256


def _q_tile(n_lat):
    return 512 if n_lat % 512 == 0 else 256


def _flash_fwd(q, k, v_t):
    n_lat = q.shape[0]
    tq = _q_tile(n_lat)
    rows = KV_REP * tq
    n_kv, tk, n_q = k.shape[1], k.shape[2], n_lat // tq

    def body(q_ref, k_ref, vt_ref, o_ref, lse_ref):
        q = _stack_heads(q_ref)

        def step(j, carry):
            m_prev, l_prev, acc = carry
            s_t = _dot_t1(k_ref[0, j], q)
            m_new = jnp.maximum(m_prev, jnp.max(s_t, axis=0, keepdims=True))
            alpha = jnp.exp(m_prev - m_new)
            p_t = jnp.exp(s_t - m_new)
            l_new = alpha * l_prev + jnp.sum(p_t, axis=0, keepdims=True)
            return m_new, l_new, alpha * acc + _dot(vt_ref[0, j], p_t)

        init = (jnp.full((1, rows), -jnp.inf, F32), jnp.zeros((1, rows), F32), jnp.zeros((HEAD_DIM, rows), F32))
        m, l, acc = lax.fori_loop(0, n_kv, step, init)
        o_ref[...] = _unstack_heads(acc / l, tq)
        lse_ref[0, 0] = m + jnp.log(l)

    kv_all = lambda a: pl.BlockSpec((1,) + a.shape[1:], lambda g, i: (g, 0, 0, 0))
    return pl.pallas_call(
        body, name="flash_fwd", grid=(N_KV_HEADS, n_q),
        in_specs=[pl.BlockSpec((tq, GROUP_WIDTH), lambda g, i: (i, g)), kv_all(k), kv_all(v_t)],
        out_specs=[pl.BlockSpec((tq, GROUP_WIDTH), lambda g, i: (i, g)),
                   pl.BlockSpec((1, 1, 1, rows), lambda g, i: (g, i, 0, 0))],
        out_shape=[_sds((n_lat, Q_WIDTH)), _sds((N_KV_HEADS, n_q, 1, rows))],
        compiler_params=_cparams(2),
    )(q, k, v_t)


def _flash_bwd(q, k, k_t, v, do, lse_t, delta_t):
    n_lat = q.shape[0]
    tq = _q_tile(n_lat)
    rows = KV_REP * tq
    n_kv, tk, n_q = k.shape[1], k.shape[2], n_lat // tq

    def body(q_ref, k_ref, kt_ref, v_ref, do_ref, lse_ref, dl_ref, dq_ref, dk_ref, dv_ref):
        _init_acc(pl.program_id(1) == 0, dk_ref, dv_ref)
        q, do = _stack_heads(q_ref), _stack_heads(do_ref)
        lse, delta = lse_ref[0, 0], dl_ref[0, 0]

        def step(j, dq_acc):
            p_t = jnp.exp(_dot_t1(k_ref[0, j], q) - lse)
            dv_ref[0, j] += _dot(p_t, do)
            ds_t = p_t * (_dot_t1(v_ref[0, j], do) - delta)
            dk_ref[0, j] += _dot(ds_t, q)
            return dq_acc + _dot(kt_ref[0, j], ds_t)

        dq = lax.fori_loop(0, n_kv, step, jnp.zeros((HEAD_DIM, rows), F32))
        dq_ref[...] = _unstack_heads(dq, tq)

    qspec = pl.BlockSpec((tq, GROUP_WIDTH), lambda g, i: (i, g))
    rowspec = pl.BlockSpec((1, 1, 1, rows), lambda g, i: (g, i, 0, 0))
    kv_all = lambda a: pl.BlockSpec((1,) + a.shape[1:], lambda g, i: (g, 0, 0, 0))
    return pl.pallas_call(
        body, name="flash_bwd", grid=(N_KV_HEADS, n_q),
        in_specs=[qspec, kv_all(k), kv_all(k_t), kv_all(v), qspec, rowspec, rowspec],
        out_specs=[qspec, kv_all(k), kv_all(k)],
        out_shape=[_sds((n_lat, Q_WIDTH)), _sds(k.shape), _sds(k.shape)],
        compiler_params=_cparams(2),
    )(q, k, k_t, v, do, lse_t, delta_t)


def _to_lane_stacked(a, tq):
    n_lat = a.shape[0]
    a = a.reshape(n_lat // tq, tq, N_KV_HEADS, KV_REP).transpose(2, 0, 3, 1)
    return a.reshape(N_KV_HEADS, n_lat // tq, 1, KV_REP * tq)


def _attn_post_loss(o, z, xa, gate, w_out, w_out_t, final_g, target):
    n_lat, d = target.shape
    e = o.shape[1]
    head_of_lane = (jnp.arange(e)[:, None] // HEAD_DIM == jnp.arange(128)[None, :]).astype(BF16)

    def body(o_ref, z_ref, x_ref, gt_ref, w_ref, wt_ref, fg_ref, tg_ref, hl_ref,
             do_ref, dl_ref, dz_ref, dx_ref, loss_ref, dfg_ref, dgt_ref, dw_ref):
        _init_acc(pl.program_id(0) == 0, loss_ref, dfg_ref, dgt_ref, dw_ref)
        oo, zz, gate_t, fg = o_ref[...], z_ref[...], gt_ref[0], fg_ref[...]
        sz = _sigmoid(zz)
        silu_z = zz * sz
        m = oo * silu_z
        out = _dot(m, w_ref[...])
        x2 = x_ref[...] + gate_t * out
        r = lax.rsqrt(jnp.mean(x2 * x2, axis=-1, keepdims=True) + NORM_EPS)
        n = x2 * r
        err = n * fg - tg_ref[...]
        loss_ref[...] += 0.5 * jnp.sum(jnp.mean(err * err, axis=-1, keepdims=True), axis=0, keepdims=True)
        dy = err * (1.0 / d)
        dfg_ref[...] += _sum0(dy * n)
        dn = dy * fg
        dx2 = r * (dn - n * jnp.mean(dn * n, axis=-1, keepdims=True))
        dx_ref[...] = dx2
        dgt_ref[...] += _sum0(dx2 * out)
        dout = gate_t * dx2
        dw_ref[...] += _dot_t0(m, dout)
        dm = _dot(dout, wt_ref[...])
        do = dm * silu_z
        do_ref[...] = do.astype(BF16)
        prod = do * oo
        hi = prod.astype(BF16)
        lo = (prod - hi.astype(F32)).astype(BF16)
        dl_ref[...] = _dot(hi, hl_ref[...]) + _dot(lo, hl_ref[...])
        dz_ref[...] = dm * oo * (sz * (1.0 + zz * (1.0 - sz)))

    return pl.pallas_call(
        body, name="attn_post_loss", grid=(n_lat // ROW_TILE,),
        in_specs=[_rows(e), _rows_skip_ctx(e), _rows_skip_ctx(d), _lat_seg(d), _full(w_out.shape), _full(w_out_t.shape),
                  _full((1, d)), _rows(d), _full((e, 128))],
        out_specs=[_rows(e), _rows(128), _rows(e), _rows(d), _full((1, 1)), _full((1, d)), _full((1, d)), _full(w_out.shape)],
        out_shape=[_sds((n_lat, e), BF16), _sds((n_lat, 128)), _sds((n_lat, e)), _sds((n_lat, d)), _sds((1, 1)), _sds((1, d)),
                   _sds((1, d)), _sds(w_out.shape)],
        compiler_params=_cparams(1),
    )(o, z, xa, gate, w_out, w_out_t, final_g, target, head_of_lane)


def _attn_in_bwd(dq, dk, dv, dz, raw, xa, dx2, g, scale, shift, q_gain, k_gain, cos, sin, w_in_t):
    n_rows, d = xa.shape
    qk = Q_WIDTH + KV_WIDTH
    n_in = w_in_t.shape[0]

    def body(dq_ref, dk_ref, dv_ref, dz_ref, raw_ref, x_ref, dx2_ref, g_ref, sc_ref, sh_ref, qg_ref, kg_ref, cos_ref, sin_ref,
             wt_ref, hq_ref, hk_ref, dxa_ref, dw_ref, dqg_ref, dkg_ref, dg_ref, dsc_ref, dsh_ref):
        i = pl.program_id(0)
        _init_acc(i == 0, dw_ref, dqg_ref, dkg_ref, dg_ref)
        _init_acc(i <= 1, dsc_ref, dsh_ref)
        is_lat = (i > 0).astype(F32)
        x = x_ref[...]
        h = _rms_mod(x, g_ref[...], sc_ref[0], sh_ref[0])
        cos, sin = cos_ref[...], sin_ref[...]
        raw_t = raw_ref[...]
        dq_raw, dqg = _head_norm_rope_bwd(raw_t[:, :Q_WIDTH], qg_ref[...], _lanes(cos, Q_WIDTH), _lanes(sin, Q_WIDTH),
                                          hq_ref[...], dq_ref[...] * (SM_SCALE * is_lat))
        dk_raw, dkg = _head_norm_rope_bwd(raw_t[:, Q_WIDTH:], kg_ref[...], _lanes(cos, KV_WIDTH), _lanes(sin, KV_WIDTH),
                                          hk_ref[...], dk_ref[...])
        dqg_ref[...] += dqg
        dkg_ref[...] += dkg
        dproj = jnp.concatenate([dq_raw, dk_raw, dv_ref[...], dz_ref[...] * is_lat], axis=1)
        dh = _dot(dproj, wt_ref[...])
        dw_ref[...] += _dot_t0(h, dproj)
        dx, dg, dsc, dsh = _rms_mod_bwd(x, g_ref[...], sc_ref[0], dh)
        dg_ref[...] += dg
        dsc_ref[0] += dsc
        dsh_ref[0] += dsh
        dxa_ref[...] = dx + dx2_ref[...] * is_lat

    return pl.pallas_call(
        body, name="attn_in_bwd", grid=(n_rows // ROW_TILE,),
        in_specs=[_rows_lat(Q_WIDTH), _rows(KV_WIDTH), _rows(KV_WIDTH), _rows_lat(Q_WIDTH), _rows(qk), _rows(d), _rows_lat(d),
                  _full((1, d)), _seg(d), _seg(d), _full((1, Q_WIDTH)), _full((1, KV_WIDTH)), _rows(128), _rows(128),
                  _full(w_in_t.shape), _full((Q_WIDTH, 128)), _full((KV_WIDTH, 128))],
        out_specs=[_rows(d), _full((d, n_in)), _full((1, Q_WIDTH)), _full((1, KV_WIDTH)), _full((1, d)), _seg(d), _seg(d)],
        out_shape=[_sds((n_rows, d)), _sds((d, n_in)), _sds((1, Q_WIDTH)), _sds((1, KV_WIDTH)), _sds((1, d)),
                   _sds((2, 1, d)), _sds((2, 1, d))],
        compiler_params=_cparams(1),
    )(dq, dk, dv, dz, raw, xa, dx2, g, scale, shift, q_gain, k_gain, cos, sin, w_in_t,
      _head_of_lane(Q_WIDTH), _head_of_lane(KV_WIDTH))


def _heads_major(a, n_heads):
    return a.reshape(a.shape[0], n_heads, HEAD_DIM).transpose(1, 0, 2)


def _tokens_major(a):
    return a.transpose(1, 0, 2).reshape(a.shape[1], a.shape[0] * HEAD_DIM)


def _local_step(x, ctx, target, mods, norm_g, ssm, w_ssm_in, w_glu, b_glu, w_ssm_out, w_attn_in, q_norm, k_norm, w_attn_out,
                final_g):
    n_ctx, d = ctx.shape
    assert n_ctx == ROW_TILE
    n_lat = x.shape[0]
    (shift0, scale0, gate0), (shift1, scale1, gate1) = mods
    g0, g1, fg = norm_g[0:1], norm_g[1:2], final_g.reshape(1, d)
    b_glu = b_glu.reshape(1, -1)
    q_gain = jnp.tile(q_norm.reshape(1, HEAD_DIM), (1, N_Q_HEADS))
    k_gain = jnp.tile(k_norm.reshape(1, HEAD_DIM), (1, N_KV_HEADS))
    cos, sin = _rope_tables(n_ctx, n_lat)

    xa0 = jnp.concatenate([ctx, x], axis=0)
    u, z0 = _ssm_in(xa0, g0, scale0, shift0, w_ssm_in)
    (y0, y1), saved = _s5_forward(u, ssm, n_ctx)
    xa1 = _ssm_post(xa0, y0, y1, z0, gate0, w_glu, b_glu, w_ssm_out)

    q, k, v, z1, raw = _attn_in(xa1, g1, scale1, shift1, w_attn_in, q_gain, k_gain, cos, sin)
    tq, tk = _q_tile(n_lat), _kv_tile(n_ctx + n_lat)
    key_blocks = lambda a: _heads_major(a, N_KV_HEADS).reshape(N_KV_HEADS, -1, tk, HEAD_DIM)
    k_b, v_b = key_blocks(k), key_blocks(v)
    o, lse_t = _flash_fwd(q, k_b, v_b.transpose(0, 1, 3, 2))
    do, delta, dz1, dx2, loss, d_fg, d_gate1, d_w_attn_out = _attn_post_loss(
        o, z1, xa1, gate1, w_attn_out, w_attn_out.T, fg, target)

    dq, dk_b, dv_b = _flash_bwd(q, k_b, k_b.transpose(0, 1, 3, 2), v_b, do, lse_t, _to_lane_stacked(delta[:, :N_Q_HEADS], tq))
    keys_major = lambda a: _tokens_major(a.reshape(N_KV_HEADS, -1, HEAD_DIM))
    dxa1, d_w_attn_in, d_qg, d_kg, d_g1, d_scale1, d_shift1 = _attn_in_bwd(
        dq, keys_major(dk_b), keys_major(dv_b), dz1, raw, xa1, dx2, g1, scale1, shift1,
        q_gain, k_gain, cos, sin, w_attn_in.T)
    dy, dz0, d_gate0, d_w_ssm_out, d_w_glu, d_b_glu = _ssm_post_bwd(
        dxa1, y0, y1, z0, gate0, w_glu, b_glu, w_ssm_out, w_glu.T, w_ssm_out.T)
    (du0, du1), d_ssm = _s5_backward(u, dy, ssm, saved, n_ctx)
    grad_x, d_w_ssm_in, d_g0, d_scale0, d_shift0 = _ssm_in_bwd(du0, du1, dz0, xa0, dxa1, g0, scale0, shift0, w_ssm_in.T)

    d_gate1_seg = jnp.concatenate([jnp.zeros((1, 1, d), F32), d_gate1.reshape(1, 1, d)], axis=0)
    grads = dict(
        norm_g=jnp.concatenate([d_g0, d_g1], axis=0),
        ssm_w_in=d_w_ssm_in, ssm=d_ssm, ssm_w_glu=d_w_glu, ssm_b_glu=d_b_glu.reshape(-1), ssm_w_out=d_w_ssm_out,
        attn_w_in=d_w_attn_in,
        attn_q_norm=d_qg.reshape(N_Q_HEADS, HEAD_DIM).sum(0), attn_k_norm=d_kg.reshape(N_KV_HEADS, HEAD_DIM).sum(0),
        attn_w_out=d_w_attn_out, final_norm_g=d_fg.reshape(-1))
    d_mods = ((d_shift0, d_scale0, d_gate0), (d_shift1, d_scale1, d_gate1_seg))
    return loss[0, 0], grad_x, grads, d_mods


def _my_index():
    return 4 * lax.axis_index("x") + 2 * lax.axis_index("y") + lax.axis_index("c")


def _peer(k):
    mx, my, mc = lax.axis_index("x"), lax.axis_index("y"), lax.axis_index("c")
    px = 1 - mx if k & 4 else mx
    py = 1 - my if k & 2 else my
    pc = 1 - mc if k & 1 else mc
    return (px, py, pc), 4 * px + 2 * py + pc


HBM_SPEC = pl.BlockSpec(memory_space=pl.ANY)


def _exchange(x, name, all_to_all):
    block = x.shape[1:] if all_to_all else x.shape

    def body(x_ref, out_ref, send_sems, recv_sems, local_sem):
        me = _my_index()
        mine = pltpu.make_async_copy(x_ref.at[me] if all_to_all else x_ref, out_ref.at[me], local_sem)
        mine.start()
        sends = []
        for k in range(1, N_DEV):
            peer, peer_idx = _peer(k)
            cp = pltpu.make_async_remote_copy(
                src_ref=x_ref.at[peer_idx] if all_to_all else x_ref, dst_ref=out_ref.at[me],
                send_sem=send_sems.at[k - 1], recv_sem=recv_sems.at[k - 1], device_id=peer, device_id_type=MESH_IDS)
            cp.start()
            sends.append(cp)
        for k in range(1, N_DEV):
            peer, peer_idx = _peer(k)
            pltpu.make_async_remote_copy(
                src_ref=x_ref.at[me] if all_to_all else x_ref, dst_ref=out_ref.at[peer_idx],
                send_sem=send_sems.at[k - 1], recv_sem=recv_sems.at[k - 1], device_id=peer, device_id_type=MESH_IDS).wait_recv()
        for cp in sends:
            cp.wait_send()
        mine.wait()

    return pl.pallas_call(
        body, name=name, in_specs=[HBM_SPEC], out_specs=HBM_SPEC,
        out_shape=_sds((N_DEV,) + tuple(block), x.dtype),
        scratch_shapes=[pltpu.SemaphoreType.DMA((N_DEV - 1,)), pltpu.SemaphoreType.DMA((N_DEV - 1,)), pltpu.SemaphoreType.DMA],
    )(x)


MOD_ROWS = 16
CTX_ROW = N_DEV


def _mod_fwd(cond, w_shard, b_cols):
    n_layers, d, cols = w_shard.shape

    def body(c_ref, w_ref, b_ref, o_ref):
        c = c_ref[...]
        s = c * _sigmoid(c)
        for i in range(n_layers):
            o_ref[i] = _dot(s, w_ref[i]) + b_ref[i]

    return pl.pallas_call(
        body, name="mod_fwd", out_shape=_sds((n_layers, MOD_ROWS, cols)),
        compiler_params=pltpu.CompilerParams(vmem_limit_bytes=VMEM_LIMIT),
    )(cond, w_shard, b_cols.reshape(n_layers, 1, cols))


def _mod_bwd(cond, d_lat_cols, d_ctx_cols, w_shard):
    n_layers, d, cols = w_shard.shape

    def body(c_ref, dl_ref, dc_ref, w_ref, dw_ref, dcc_ref):
        c = c_ref[...]
        sg = _sigmoid(c)
        s = c * sg
        d_s = jnp.zeros((MOD_ROWS, d), F32)
        for i in range(n_layers):
            d_ctx = dc_ref[0, i]
            for j in range(1, N_DEV):
                d_ctx = d_ctx + dc_ref[j, i]
            dm = jnp.concatenate([dl_ref[i], d_ctx, jnp.zeros((MOD_ROWS - N_DEV - 1, cols), F32)], axis=0)
            dw_ref[i] = _dot_t0(s, dm)
            d_s = d_s + _dot_t1(dm, w_ref[i])
        d_c = d_s * (sg * (1.0 + c * (1.0 - sg)))
        dcc_ref[...] = d_c[CTX_ROW:CTX_ROW + 1]

    return pl.pallas_call(
        body, name="mod_bwd", out_shape=[_sds((n_layers, d, cols)), _sds((1, d))],
        compiler_params=pltpu.CompilerParams(vmem_limit_bytes=VMEM_LIMIT),
    )(cond, d_lat_cols, d_ctx_cols, w_shard)


ADAM_TILE = 512


def _adamw(w, g_parts, m, v, name):
    n_parts, n_rows, lanes = g_parts.shape
    c1 = 1.0 - ADAM_B1 ** ADAM_STEP
    c2 = 1.0 - ADAM_B2 ** ADAM_STEP

    def body(w_ref, g_ref, m_ref, v_ref, go_ref, d_ref, mo_ref, vo_ref):
        g = g_ref[0].astype(F32)
        for p in range(1, n_parts):
            g = g + g_ref[p].astype(F32)
        m_new = ADAM_B1 * m_ref[...] + (1.0 - ADAM_B1) * g
        v_new = ADAM_B2 * v_ref[...] + (1.0 - ADAM_B2) * (g * g)
        go_ref[...] = g
        mo_ref[...] = m_new
        vo_ref[...] = v_new
        d_ref[...] = -ADAM_LR * ((m_new / c1) / (jnp.sqrt(v_new / c2) + ADAM_EPS) + ADAM_WD * w_ref[...])

    row = pl.BlockSpec((ADAM_TILE, lanes), lambda i: (i, 0))
    return pl.pallas_call(
        body, name=name, grid=(n_rows // ADAM_TILE,),
        in_specs=[row, pl.BlockSpec((n_parts, ADAM_TILE, lanes), lambda i: (0, i, 0)), row, row],
        out_specs=[row] * 4, out_shape=[_sds((n_rows, lanes))] * 4,
        compiler_params=_cparams(1),
    )(w, g_parts, m, v)


def _sum_parts(parts):
    n_parts, n_rows, lanes = parts.shape

    def body(p_ref, o_ref):
        acc = p_ref[0]
        for p in range(1, n_parts):
            acc = acc + p_ref[p]
        o_ref[...] = acc

    return pl.pallas_call(body, name="sum_parts", out_shape=_sds((n_rows, lanes)))(parts)


def _pack(arrays, row_multiple):
    parts = []
    for a in arrays:
        flat = a.reshape(-1)
        parts.append(jnp.pad(flat, (0, (-flat.shape[0]) % 1024)))
    flat = jnp.concatenate(parts)
    flat = jnp.pad(flat, (0, (-flat.shape[0]) % (row_multiple * 128)))
    return flat.reshape(-1, 128)


def _unpack(packed, shapes):
    flat = packed.reshape(-1)
    out, pos = [], 0
    for s in shapes:
        n = math.prod(s)
        out.append(flat[pos:pos + n].reshape(s))
        pos += n + (-n) % 1024
    return out


WEIGHT_NAMES = ['c_ctx', 'w_mod', 'b_mod', 'norm_g', 'ssm_w_in', 'ssm_a_re', 'ssm_a_im', 'ssm_log_dt', 'ssm_b_re', 'ssm_b_im',
                'ssm_c_re', 'ssm_c_im', 'ssm_d', 'ssm_w_glu', 'ssm_b_glu', 'ssm_w_out', 'attn_w_in', 'attn_q_norm',
                'attn_k_norm', 'attn_w_out', 'final_norm_g']
SHARDED = ['ssm_w_in', 'ssm_w_glu', 'ssm_w_out', 'attn_w_in', 'attn_w_out']
COLUMN_SHARDED = ('ssm_w_in', 'attn_w_in')
REPLICATED = ['c_ctx', 'b_mod', 'norm_g', 'ssm_a_re', 'ssm_a_im', 'ssm_log_dt', 'ssm_b_re', 'ssm_b_im', 'ssm_c_re', 'ssm_c_im',
              'ssm_d', 'ssm_b_glu', 'attn_q_norm', 'attn_k_norm', 'final_norm_g']
SSM_NAMES = ['ssm_a_re', 'ssm_a_im', 'ssm_log_dt', 'ssm_b_re', 'ssm_b_im', 'ssm_c_re', 'ssm_c_im', 'ssm_d']


def _full_from_shards(gathered, name, shard_shape):
    rows, cols = shard_shape
    w = gathered.reshape(N_DEV, rows, cols)
    if name in COLUMN_SHARDED:
        return w.transpose(1, 0, 2).reshape(rows, N_DEV * cols)
    return w.reshape(N_DEV * rows, cols)


def _shards_from_full(g, name):
    if name in COLUMN_SHARDED:
        rows, cols = g.shape
        g = g.reshape(rows, N_DEV, cols // N_DEV).transpose(1, 0, 2)
    return g.reshape(N_DEV, -1, 128)


def kernel(x, c, ctx, c_ctx, w_mod, b_mod, norm_g, ssm_w_in, ssm_a_re, ssm_a_im, ssm_log_dt, ssm_b_re, ssm_b_im, ssm_c_re, ssm_c_im, ssm_d, ssm_w_glu, ssm_b_glu, ssm_w_out, attn_w_in, attn_q_norm, attn_k_norm, attn_w_out, final_norm_g, loss_target, m_c_ctx, m_w_mod, m_b_mod, m_norm_g, m_ssm_w_in, m_ssm_a_re, m_ssm_a_im, m_ssm_log_dt, m_ssm_b_re, m_ssm_b_im, m_ssm_c_re, m_ssm_c_im, m_ssm_d, m_ssm_w_glu, m_ssm_b_glu, m_ssm_w_out, m_attn_w_in, m_attn_q_norm, m_attn_k_norm, m_attn_w_out, m_final_norm_g, v_c_ctx, v_w_mod, v_b_mod, v_norm_g, v_ssm_w_in, v_ssm_a_re, v_ssm_a_im, v_ssm_log_dt, v_ssm_b_re, v_ssm_b_im, v_ssm_c_re, v_ssm_c_im, v_ssm_d, v_ssm_w_glu, v_ssm_b_glu, v_ssm_w_out, v_attn_w_in, v_attn_q_norm, v_attn_k_norm, v_attn_w_out, v_final_norm_g):
    env = dict(locals())
    weights = {n: env[n] for n in WEIGHT_NAMES}
    mom_m = {n: env["m_" + n] for n in WEIGHT_NAMES}
    mom_v = {n: env["v_" + n] for n in WEIGHT_NAMES}
    d = D_MODEL
    me = _my_index()
    mod_cols = w_mod.shape[-1]

    c_all = _exchange(c.reshape(8, d // 8), "gather_c", False).reshape(N_DEV, d)
    cond = jnp.concatenate([c_all, c_ctx.reshape(1, d), jnp.zeros((MOD_ROWS - N_DEV - 1, d), F32)], axis=0)
    shard_shapes = {n: weights[n].shape[1:] for n in SHARDED}
    w_gathered = _exchange(_pack([weights[n] for n in SHARDED], 1).astype(BF16), "gather_weights", False)
    full, pos = {}, 0
    for n in SHARDED:
        rows = math.prod(shard_shapes[n]) // 128
        full[n] = _full_from_shards(w_gathered[:, pos:pos + rows], n, shard_shapes[n])
        pos += rows

    b_cols = lax.dynamic_slice(b_mod, (0, me * mod_cols), (2, mod_cols))
    mod_shard = _mod_fwd(cond, w_mod, b_cols)
    mod_all = _exchange(mod_shard.reshape(2 * MOD_ROWS, mod_cols), "gather_mod", False)
    mod_full = mod_all.reshape(N_DEV, 2, MOD_ROWS, mod_cols).transpose(1, 2, 0, 3).reshape(2, MOD_ROWS, 3 * d)
    lat_rows = lax.dynamic_slice(mod_full, (0, me, 0), (2, 1, 3 * d))
    mods = []
    for i in range(2):
        seg = jnp.stack([mod_full[i, CTX_ROW:CTX_ROW + 1], lat_rows[i]])
        mods.append((seg[:, :, :d], seg[:, :, d:2 * d], seg[:, :, 2 * d:]))

    ssm = tuple(weights[n][0] for n in SSM_NAMES)
    loss, grad_x, g, d_mods = _local_step(
        x[0], ctx[0], loss_target[0], mods, norm_g, ssm, full['ssm_w_in'], full['ssm_w_glu'], ssm_b_glu[0], full['ssm_w_out'],
        full['attn_w_in'], attn_q_norm[0], attn_k_norm[0], full['attn_w_out'], final_norm_g)
    loss = lax.psum(loss, ("x", "y", "c"))

    d_rows = jnp.stack([jnp.concatenate(dm, axis=-1) for dm in d_mods])
    d_rows = jnp.concatenate([d_rows.reshape(4, 3 * d), jnp.zeros((4, 3 * d), F32)], axis=0)
    d_all = _exchange(d_rows, "gather_dmod", False)[:, :4].reshape(N_DEV, 2, 2, 3 * d)
    d_all = lax.dynamic_slice(d_all, (0, 0, 0, me * mod_cols), (N_DEV, 2, 2, mod_cols))
    d_w_mod, d_c_ctx = _mod_bwd(cond, d_all[:, :, 1].transpose(1, 0, 2), d_all[:, :, 0:1], w_mod)
    d_b_mod = jnp.stack([jnp.concatenate([t[0] + t[1] for t in dm], axis=-1).reshape(3 * d) for dm in d_mods])

    g_big = jnp.concatenate([_shards_from_full(g[n], n) for n in SHARDED], axis=1)
    g_big_parts = _exchange(g_big.astype(BF16), "scatter_grads", True)
    pack_big = lambda t: _pack([t[n] for n in SHARDED], ADAM_TILE)
    big = _adamw(pack_big(weights), g_big_parts, pack_big(mom_m), pack_big(mom_v), "adamw_sharded")
    big = [_unpack(t, [weights[n].shape for n in SHARDED]) for t in big]

    mod_res = _adamw(_pack([w_mod], ADAM_TILE), _pack([d_w_mod], ADAM_TILE)[None], _pack([m_w_mod], ADAM_TILE),
                     _pack([v_w_mod], ADAM_TILE), "adamw_w_mod")
    mod_res = [t.reshape(w_mod.shape) for t in mod_res]

    small = dict(zip(SSM_NAMES, g['ssm']))
    small.update(c_ctx=d_c_ctx, b_mod=d_b_mod, norm_g=g['norm_g'], ssm_b_glu=g['ssm_b_glu'], attn_q_norm=g['attn_q_norm'],
                 attn_k_norm=g['attn_k_norm'], final_norm_g=g['final_norm_g'])
    pack_small = lambda t: _pack([t[n] for n in REPLICATED], ADAM_TILE)
    g_small = pack_small(small)
    slices = _exchange(g_small.reshape(N_DEV, -1, 128), "scatter_small_grads", True)
    g_small = _exchange(_sum_parts(slices), "gather_small_grads", False).reshape(1, -1, 128)
    rep = _adamw(pack_small(weights), g_small, pack_small(mom_m), pack_small(mom_v), "adamw_replicated")
    rep = [_unpack(t, [weights[n].shape for n in REPLICATED]) for t in rep]

    results = []
    for kind in range(4):
        by_name = dict(zip(SHARDED, big[kind]))
        by_name.update(zip(REPLICATED, rep[kind]))
        by_name['w_mod'] = mod_res[kind]
        results.extend(by_name[n] for n in WEIGHT_NAMES)
    return (loss, grad_x[None], *results)
```

```python
import functools
import math

import jax
import jax.numpy as jnp
from jax import lax
from jax.experimental import pallas as pl
from jax.experimental.pallas import tpu as pltpu

F32 = jnp.float32
BF16 = jnp.bfloat16

N_DEV = 8
D_MODEL = 1024
NORM_EPS = 1e-6
SSM_GROUP = 16
SSM_GROUPS = 64
SSM_STATE = 64
GROUPS_PER_BLOCK = 8
N_BLOCKS = SSM_GROUPS // GROUPS_PER_BLOCK
HALF = GROUPS_PER_BLOCK * SSM_STATE
HEAD_DIM = 64
N_Q_HEADS = 16
N_KV_HEADS = 4
KV_REP = N_Q_HEADS // N_KV_HEADS
KV_WIDTH = N_KV_HEADS * HEAD_DIM
GRID_W = 64
ROPE_THETA = 10000.0
ADAM_LR, ADAM_B1, ADAM_B2, ADAM_EPS, ADAM_WD, ADAM_STEP = 0.001, 0.9, 0.999, 1e-08, 0.01, 10

ROW_TILE = 256
SCAN_CHUNK = 256
VMEM_LIMIT = 56 * 1024 * 1024
MESH_IDS = pl.DeviceIdType.MESH


def _cparams(n_axes):
    return pltpu.CompilerParams(dimension_semantics=("arbitrary",) * n_axes, vmem_limit_bytes=VMEM_LIMIT)


def _dot(a, b):
    return jnp.dot(a.astype(BF16), b.astype(BF16), preferred_element_type=F32)


def _dot_t0(a, b):
    return lax.dot_general(a.astype(BF16), b.astype(BF16), (((0,), (0,)), ((), ())), preferred_element_type=F32)


def _dot_t1(a, b):
    return lax.dot_general(a.astype(BF16), b.astype(BF16), (((1,), (1,)), ((), ())), preferred_element_type=F32)


def _s5_prep(a_re, a_im, log_dt, b_re, b_im):
    dt = jnp.exp(log_dt)[:, None]
    ldr, ldi = a_re * dt, a_im * dt
    mag = jnp.exp(ldr)
    abar_re, abar_im = mag * jnp.cos(ldi), mag * jnp.sin(ldi)
    den = a_re * a_re + a_im * a_im
    num_re, num_im = abar_re - 1.0, abar_im
    coef_re = (num_re * a_re + num_im * a_im) / den
    coef_im = (num_im * a_re - num_re * a_im) / den
    bbar_re = coef_re[..., None] * b_re - coef_im[..., None] * b_im
    bbar_im = coef_re[..., None] * b_im + coef_im[..., None] * b_re
    return abar_re, abar_im, bbar_re, bbar_im


def _s5_blocks(abar_re, abar_im, bbar_re, bbar_im, c_re, c_im):
    eye = jnp.eye(GROUPS_PER_BLOCK, dtype=F32)
    bb = jnp.stack([bbar_re, bbar_im]).reshape(2, N_BLOCKS, GROUPS_PER_BLOCK, SSM_STATE, SSM_GROUP)
    b_blk = jnp.einsum('rqgph,gk->qghrkp', bb, eye).reshape(N_BLOCKS, 128, 2 * HALF)
    cc = jnp.stack([c_re, -c_im]).reshape(2, N_BLOCKS, GROUPS_PER_BLOCK, SSM_GROUP, SSM_STATE)
    c_blk = jnp.einsum('rqghp,gk->qrgpkh', cc, eye).reshape(N_BLOCKS, 2 * HALF, 128)
    abar = jnp.stack([abar_re.reshape(N_BLOCKS, HALF), abar_im.reshape(N_BLOCKS, HALF)])
    return abar, b_blk, c_blk


def _s5_unblock(d_b_blk, d_ct_blk):
    db = d_b_blk.reshape(N_BLOCKS, GROUPS_PER_BLOCK, SSM_GROUP, 2, GROUPS_PER_BLOCK, SSM_STATE)
    db = jnp.einsum('qghrgp->rqgph', db).reshape(2, SSM_GROUPS, SSM_STATE, SSM_GROUP)
    dc = d_ct_blk.reshape(N_BLOCKS, GROUPS_PER_BLOCK, SSM_GROUP, 2, GROUPS_PER_BLOCK, SSM_STATE)
    dc = jnp.einsum('qghrgp->rqghp', dc).reshape(2, SSM_GROUPS, SSM_GROUP, SSM_STATE)
    return db[0], db[1], dc[0], -dc[1]


def _scan_chunk_of_step(j, n_chunks, n_ctx_chunks, reverse):
    if not reverse:
        return j
    return jnp.where(j < n_ctx_chunks, n_ctx_chunks - 1 - j, n_chunks - 1 - j + n_ctx_chunks)


LANE_TILES = 2 * HALF // 128
RE_TILES = HALF // 128


def _tiles(v):
    return [v[:, l * 128:(l + 1) * 128] for l in range(v.shape[1] // 128)]


def _scatter_steps(s_ref, q, x):
    for l in range(LANE_TILES):
        s_ref[l, pl.ds(q, x.shape[0], stride=N_BLOCKS), :] = x[:, l * 128:(l + 1) * 128]


def _gather_steps(s_ref, q, n_steps):
    return jnp.concatenate([s_ref[l, pl.ds(q, n_steps, stride=N_BLOCKS), :] for l in range(LANE_TILES)], axis=1)


def _load_step(s_ref, t):
    row = pl.multiple_of(t * N_BLOCKS, N_BLOCKS)
    return [s_ref[l, pl.ds(row, N_BLOCKS), :] for l in range(LANE_TILES)]


def _store_step(s_ref, t, tiles):
    row = pl.multiple_of(t * N_BLOCKS, N_BLOCKS)
    for l in range(LANE_TILES):
        s_ref[l, pl.ds(row, N_BLOCKS), :] = tiles[l]


def _cmul_add(a, h, x, conj):
    re, im = [], []
    for l in range(RE_TILES):
        ar, ai, hr, hi = a[l], a[RE_TILES + l], h[l], h[RE_TILES + l]
        if conj:
            re.append(ar * hr + ai * hi + x[l])
            im.append(ar * hi - ai * hr + x[RE_TILES + l])
        else:
            re.append(ar * hr - ai * hi + x[l])
            im.append(ar * hi + ai * hr + x[RE_TILES + l])
    return re + im


def _s5_scan_fwd(u, abar, b_blk, c_blk, d_skip, n_ctx, reverse, rider=None):
    n_rows, width = u.shape
    tc = SCAN_CHUNK
    n_chunks, n_ctx_chunks = n_rows // tc, n_ctx // tc
    with_skip = d_skip is not None

    def body(*refs):
        if rider is not None:
            x_ref, ride_out, sems = refs[4 + with_skip], refs[7 + with_skip], refs[-3:]
            _ride(rider, pl.program_id(0) == 0, pl.program_id(0) == n_chunks - 1, (x_ref, ride_out) + tuple(sems))
            refs = refs[:4 + with_skip] + refs[5 + with_skip:7 + with_skip] + refs[8 + with_skip:-3]
        if with_skip:
            u_ref, a_ref, b_ref, c_ref, d_ref, y_ref, hb_ref, s_ref, h_ref = refs
        else:
            u_ref, a_ref, b_ref, c_ref, y_ref, hb_ref, s_ref, h_ref = refs
        j = pl.program_id(0)

        @pl.when(j == 0)
        def _():
            h_ref[...] = jnp.zeros_like(h_ref)

        hb_ref[0] = h_ref[...]
        for q in range(N_BLOCKS):
            _scatter_steps(s_ref, q, _dot(u_ref[:, q * 128:(q + 1) * 128], b_ref[q]))
        a = _tiles(a_ref[0]) + _tiles(a_ref[1])

        def step(s, h):
            t = tc - 1 - s if reverse else s
            h = _cmul_add(a, h, _load_step(s_ref, t), conj=False)
            _store_step(s_ref, t, h)
            return h

        h = lax.fori_loop(0, tc, step, _tiles(h_ref[...]))
        h_ref[...] = jnp.concatenate(h, axis=1)
        for q in range(N_BLOCKS):
            yq = _dot(_gather_steps(s_ref, q, tc), c_ref[q])
            if with_skip:
                yq = yq + d_ref[:, q * 128:(q + 1) * 128] * u_ref[:, q * 128:(q + 1) * 128]
            y_ref[:, q * 128:(q + 1) * 128] = yq

    chunk = functools.partial(_scan_chunk_of_step, n_chunks=n_chunks, n_ctx_chunks=n_ctx_chunks, reverse=reverse)
    full3 = lambda j: (0, 0, 0)
    in_specs = [pl.BlockSpec((tc, width), lambda j: (chunk(j), 0)),
                pl.BlockSpec((2, N_BLOCKS, HALF), full3),
                pl.BlockSpec((N_BLOCKS, 128, 2 * HALF), full3),
                pl.BlockSpec((N_BLOCKS, 2 * HALF, 128), full3)]
    args = [u, abar, b_blk.astype(BF16), c_blk.astype(BF16)]
    if with_skip:
        in_specs.append(pl.BlockSpec((1, width), lambda j: (0, 0)))
        args.append(d_skip.reshape(1, width))
    out_specs = [pl.BlockSpec((tc, width), lambda j: (chunk(j), 0)),
                 pl.BlockSpec((1, N_BLOCKS, 2 * HALF), lambda j: (chunk(j), 0, 0))]
    out_shape = [_sds((n_rows, width)), _sds((n_chunks, N_BLOCKS, 2 * HALF))]
    scratch = [pltpu.VMEM((LANE_TILES, tc * N_BLOCKS, 128), F32), pltpu.VMEM((N_BLOCKS, 2 * HALF), F32)]
    if rider is not None:
        in_specs.append(HBM_SPEC)
        args.append(rider[0])
        out_specs.append(HBM_SPEC)
        out_shape.append(_exchange_out_shape(*rider))
        scratch += _exchange_semaphores()
    return pl.pallas_call(
        body, name="s5_scan_fwd_rev" if reverse else "s5_scan_fwd",
        grid=(n_chunks,), in_specs=in_specs, out_specs=out_specs, out_shape=out_shape, scratch_shapes=scratch,
        compiler_params=_cparams(1),
    )(*args)


def _s5_scan_bwd(u, dy, hb, abar, b_blk, c_blk, d_skip, n_ctx, reverse, rider=None):
    n_rows, width = u.shape
    tc = SCAN_CHUNK
    n_chunks, n_ctx_chunks = n_rows // tc, n_ctx // tc
    with_skip = d_skip is not None
    n_in, n_out = 7 + with_skip, 4 + with_skip

    def body(*refs):
        if rider is not None:
            x_ref, ride_out, sems = refs[n_in], refs[n_in + 1 + n_out], refs[-3:]
            _ride(rider, pl.program_id(0) == 0, pl.program_id(0) == n_chunks - 1, (x_ref, ride_out) + tuple(sems))
            refs = refs[:n_in] + refs[n_in + 1:n_in + 1 + n_out] + refs[n_in + 2 + n_out:-3]
        if with_skip:
            (u_ref, dy_ref, hb_ref, a_ref, b_ref, bt_ref, ct_ref, d_ref,
             du_ref, da_ref, db_ref, dct_ref, dd_ref, sh_ref, sg_ref, g_ref) = refs
        else:
            (u_ref, dy_ref, hb_ref, a_ref, b_ref, bt_ref, ct_ref,
             du_ref, da_ref, db_ref, dct_ref, sh_ref, sg_ref, g_ref) = refs
        j = pl.program_id(0)

        @pl.when(j == 0)
        def _():
            g_ref[...] = jnp.zeros_like(g_ref)
            da_ref[...] = jnp.zeros_like(da_ref)
            db_ref[...] = jnp.zeros_like(db_ref)
            dct_ref[...] = jnp.zeros_like(dct_ref)
            if with_skip:
                dd_ref[...] = jnp.zeros_like(dd_ref)

        for q in range(N_BLOCKS):
            _scatter_steps(sh_ref, q, _dot(u_ref[:, q * 128:(q + 1) * 128], b_ref[q]))
            _scatter_steps(sg_ref, q, _dot(dy_ref[:, q * 128:(q + 1) * 128], ct_ref[q]))
        a = _tiles(a_ref[0]) + _tiles(a_ref[1])
        time_of = (lambda s: tc - 1 - s) if reverse else (lambda s: s)

        def fwd_step(s, h):
            h = _cmul_add(a, h, _load_step(sh_ref, time_of(s)), conj=False)
            _store_step(sh_ref, time_of(s), h)
            return h

        h0 = _tiles(hb_ref[0])
        lax.fori_loop(0, tc, fwd_step, h0)

        def adj(t, h_prev, carry):
            g, da = carry
            g = _cmul_add(a, g, _load_step(sg_ref, t), conj=True)
            _store_step(sg_ref, t, g)
            da_re = [da[l] + g[l] * h_prev[l] + g[RE_TILES + l] * h_prev[RE_TILES + l] for l in range(RE_TILES)]
            da_im = [da[RE_TILES + l] + g[RE_TILES + l] * h_prev[l] - g[l] * h_prev[RE_TILES + l] for l in range(RE_TILES)]
            return g, da_re + da_im

        def bwd_step(i, carry):
            s = tc - 1 - i
            return adj(time_of(s), _load_step(sh_ref, time_of(s - 1)), carry)

        carry = (_tiles(g_ref[...]), _tiles(da_ref[0]) + _tiles(da_ref[1]))
        carry = lax.fori_loop(0, tc - 1, bwd_step, carry)
        g, da = adj(time_of(0), h0, carry)
        g_ref[...] = jnp.concatenate(g, axis=1)
        da_ref[0] = jnp.concatenate(da[:RE_TILES], axis=1)
        da_ref[1] = jnp.concatenate(da[RE_TILES:], axis=1)

        for q in range(N_BLOCKS):
            cols = slice(q * 128, (q + 1) * 128)
            uq, dyq = u_ref[:, cols], dy_ref[:, cols]
            gq = _gather_steps(sg_ref, q, tc)
            duq = _dot(gq, bt_ref[q])
            if with_skip:
                duq = duq + d_ref[:, cols] * dyq
                dd_ref[:, cols] += jnp.sum(dyq * uq, axis=0, keepdims=True)
            du_ref[:, cols] = duq
            db_ref[q] += _dot_t0(uq, gq)
            dct_ref[q] += _dot_t0(dyq, _gather_steps(sh_ref, q, tc))

    def chunk(j):
        return _scan_chunk_of_step(n_chunks - 1 - j, n_chunks, n_ctx_chunks, reverse)

    full2 = lambda j: (0, 0)
    full3 = lambda j: (0, 0, 0)
    row = pl.BlockSpec((tc, width), lambda j: (chunk(j), 0))
    in_specs = [row, row,
                pl.BlockSpec((1, N_BLOCKS, 2 * HALF), lambda j: (chunk(j), 0, 0)),
                pl.BlockSpec((2, N_BLOCKS, HALF), full3),
                pl.BlockSpec((N_BLOCKS, 128, 2 * HALF), full3),
                pl.BlockSpec((N_BLOCKS, 2 * HALF, 128), full3),
                pl.BlockSpec((N_BLOCKS, 128, 2 * HALF), full3)]
    args = [u, dy, hb, abar, b_blk.astype(BF16), jnp.swapaxes(b_blk, 1, 2).astype(BF16),
            jnp.swapaxes(c_blk, 1, 2).astype(BF16)]
    out_specs = [row,
                 pl.BlockSpec((2, N_BLOCKS, HALF), full3),
                 pl.BlockSpec((N_BLOCKS, 128, 2 * HALF), full3),
                 pl.BlockSpec((N_BLOCKS, 128, 2 * HALF), full3)]
    out_shape = [jax.ShapeDtypeStruct((n_rows, width), F32),
                 jax.ShapeDtypeStruct((2, N_BLOCKS, HALF), F32),
                 jax.ShapeDtypeStruct((N_BLOCKS, 128, 2 * HALF), F32),
                 jax.ShapeDtypeStruct((N_BLOCKS, 128, 2 * HALF), F32)]
    if with_skip:
        in_specs.append(pl.BlockSpec((1, width), full2))
        args.append(d_skip.reshape(1, width))
        out_specs.append(pl.BlockSpec((1, width), full2))
        out_shape.append(jax.ShapeDtypeStruct((1, width), F32))
    scratch = [pltpu.VMEM((LANE_TILES, tc * N_BLOCKS, 128), F32), pltpu.VMEM((LANE_TILES, tc * N_BLOCKS, 128), F32),
               pltpu.VMEM((N_BLOCKS, 2 * HALF), F32)]
    if rider is not None:
        in_specs.append(HBM_SPEC)
        args.append(rider[0])
        out_specs.append(HBM_SPEC)
        out_shape.append(_exchange_out_shape(*rider))
        scratch += _exchange_semaphores()
    return pl.pallas_call(
        body, name="s5_scan_bwd_rev" if reverse else "s5_scan_bwd",
        grid=(n_chunks,), in_specs=in_specs, out_specs=out_specs, out_shape=out_shape, scratch_shapes=scratch,
        compiler_params=_cparams(1),
    )(*args)


def _s5_dir_params(d, a_re, a_im, log_dt, b_re, b_im):
    return a_re[d], a_im[d], log_dt[d], b_re[d], b_im[d]


def _s5_forward(u, ssm, n_ctx, rider=None):
    a_re, a_im, log_dt, b_re, b_im, c_re, c_im, d_skip = ssm
    outs, saved, carried = [], [], None
    for d in range(2):
        prep = _s5_prep(*_s5_dir_params(d, a_re, a_im, log_dt, b_re, b_im))
        abar, b_blk, c_blk = _s5_blocks(*prep, c_re[d], c_im[d])
        res = _s5_scan_fwd(u, abar, b_blk, c_blk, d_skip if d == 0 else None, n_ctx, reverse=(d == 1),
                           rider=rider if d == 0 else None)
        if d == 0 and rider is not None:
            carried = res[2]
        outs.append(res[0])
        saved.append((res[1], abar, b_blk, c_blk))
    return outs, saved, carried


def _s5_backward(u, dy, ssm, saved, n_ctx, rider=None):
    a_re, a_im, log_dt, b_re, b_im, c_re, c_im, d_skip = ssm
    dus, grads = [], [[] for _ in range(7)]
    d_d, carried = None, None
    for d in range(2):
        hb, abar, b_blk, c_blk = saved[d]
        res = _s5_scan_bwd(u, dy, hb, abar, b_blk, c_blk, d_skip if d == 0 else None, n_ctx, reverse=(d == 1),
                           rider=rider if d == 0 else None)
        if d == 0 and rider is not None:
            carried, res = res[-1], res[:-1]
        if d == 0:
            du, d_abar, d_b_blk, d_ct_blk, d_d = res
        else:
            du, d_abar, d_b_blk, d_ct_blk = res
        dus.append(du)
        dbb_re, dbb_im, dc_re, dc_im = _s5_unblock(d_b_blk, d_ct_blk)
        _, vjp = jax.vjp(_s5_prep, *_s5_dir_params(d, a_re, a_im, log_dt, b_re, b_im))
        shape = (SSM_GROUPS, SSM_STATE)
        g5 = vjp((d_abar[0].reshape(shape), d_abar[1].reshape(shape), dbb_re, dbb_im))
        for k, g in enumerate(tuple(g5) + (dc_re, dc_im)):
            grads[k].append(g)
    grads = [jnp.stack(g) for g in grads]
    return dus, grads + [d_d.reshape(-1)], carried


INV_SQRT2 = 0.7071067811865476
INV_SQRT_2PI = 0.3989422804014327


def _rows(cols):
    return pl.BlockSpec((ROW_TILE, cols), lambda i: (i, 0))


def _rows_skip_ctx(cols):
    return pl.BlockSpec((ROW_TILE, cols), lambda i: (i + 1, 0))


def _rows_lat(cols):
    return pl.BlockSpec((ROW_TILE, cols), lambda i: (jnp.maximum(i - 1, 0), 0))


def _full(shape):
    nd = len(shape)
    return pl.BlockSpec(shape, lambda i: (0,) * nd)


def _seg(cols):
    return pl.BlockSpec((1, 1, cols), lambda i: (jnp.minimum(i, 1), 0, 0))


def _lat_seg(cols):
    return pl.BlockSpec((1, 1, cols), lambda i: (1, 0, 0))


def _sds(shape, dtype=F32):
    return jax.ShapeDtypeStruct(shape, dtype)


def _sum0(x):
    return jnp.sum(x, axis=0, keepdims=True)


def _sigmoid(x):
    return jax.nn.sigmoid(x)


def _rms_mod(x, g, scale, shift):
    r = lax.rsqrt(jnp.mean(x * x, axis=-1, keepdims=True) + NORM_EPS)
    return (x * r * g) * (1.0 + scale) + shift


def _rms_mod_bwd(x, g, scale, dh):
    r = lax.rsqrt(jnp.mean(x * x, axis=-1, keepdims=True) + NORM_EPS)
    n = x * r
    dyg = dh * (1.0 + scale)
    dn = dyg * g
    dx = r * (dn - n * jnp.mean(dn * n, axis=-1, keepdims=True))
    return dx, _sum0(dyg * n), _sum0(dh * (n * g)), _sum0(dh)


def _head_of_lane(width):
    return (jnp.arange(width)[:, None] // HEAD_DIM == jnp.arange(128)[None, :]).astype(BF16)


def _split_dot(t, w, transposed):
    hi = t.astype(BF16)
    lo = (t - hi.astype(F32)).astype(BF16)
    f = _dot_t1 if transposed else _dot
    return f(hi, w) + f(lo, w)


def _head_sums(t, hl):
    return _split_dot(_split_dot(t, hl, False), hl, True)


def _rope_partner(x):
    n = x.shape[1]
    lane = lax.broadcasted_iota(jnp.int32, x.shape, 1)
    return jnp.where((lane & 16) == 0, pltpu.roll(x, n - 16, 1), pltpu.roll(x, 16, 1))


def _lanes(tab, width):
    return jnp.tile(tab, (1, width // tab.shape[1]))


def _head_norm_rope(x, gain, cos, sin, hl):
    r = lax.rsqrt(_head_sums(x * x, hl) * (1.0 / HEAD_DIM) + NORM_EPS)
    y = x * r * gain
    return y * cos + _rope_partner(y) * sin


def _head_norm_rope_bwd(x, gain, cos, sin, hl, dout):
    dy = dout * cos + _rope_partner(dout * sin)
    r = lax.rsqrt(_head_sums(x * x, hl) * (1.0 / HEAD_DIM) + NORM_EPS)
    n = x * r
    dn = dy * gain
    dx = r * (dn - n * (_head_sums(dn * n, hl) * (1.0 / HEAD_DIM)))
    return dx, _sum0(dy * n)


def _rope_tables(n_ctx, n_lat):
    t = jnp.arange(n_lat)
    pos = jnp.stack([(t // GRID_W).astype(F32), (t % GRID_W).astype(F32)], axis=1)
    n_freq = HEAD_DIM // 4
    freqs = ROPE_THETA ** (-jnp.arange(n_freq, dtype=F32) / n_freq)
    ang = pos[:, :, None] * freqs[None, None, :]
    cos = jnp.repeat(jnp.cos(ang)[:, :, None, :], 2, axis=2).reshape(n_lat, HEAD_DIM)
    sin = jnp.sin(ang)
    sin = jnp.stack([-sin, sin], axis=2).reshape(n_lat, HEAD_DIM)
    cos = jnp.concatenate([jnp.ones((n_ctx, HEAD_DIM), F32), cos], axis=0)
    sin = jnp.concatenate([jnp.zeros((n_ctx, HEAD_DIM), F32), sin], axis=0)
    return jnp.tile(cos, (1, 2)), jnp.tile(sin, (1, 2))


def _ssm_in(xa, g, scale, shift, w_in):
    n_rows, d = xa.shape
    e = w_in.shape[1] // 2

    def body(x_ref, g_ref, sc_ref, sh_ref, w_ref, u_ref, z_ref):
        h = _rms_mod(x_ref[...], g_ref[...], sc_ref[0], sh_ref[0])
        proj = _dot(h, w_ref[...])
        u_ref[...] = proj[:, :e]
        z_ref[...] = proj[:, e:]

    return pl.pallas_call(
        body, name="ssm_in", grid=(n_rows // ROW_TILE,),
        in_specs=[_rows(d), _full((1, d)), _seg(d), _seg(d), _full(w_in.shape)],
        out_specs=[_rows(e), _rows(e)], out_shape=[_sds((n_rows, e)), _sds((n_rows, e))],
        compiler_params=_cparams(1),
    )(xa, g, scale, shift, w_in)


def _s5_post_math(y, z, w_glu, b_glu, w_out):
    er = lax.erf(y * INV_SQRT2)
    g = 0.5 * y * (1.0 + er)
    sg = _sigmoid(_dot(g, w_glu) + b_glu)
    g2 = g * sg
    sz = _sigmoid(z)
    silu_z = z * sz
    m = g2 * silu_z
    return er, g, sg, g2, sz, silu_z, m, _dot(m, w_out)


def _ssm_post(xa, y0, y1, z, gate, w_glu, b_glu, w_out):
    n_rows, d = xa.shape
    e = z.shape[1]

    def body(x_ref, y0_ref, y1_ref, z_ref, gt_ref, wg_ref, bg_ref, wo_ref, o_ref):
        out = _s5_post_math(y0_ref[...] + y1_ref[...], z_ref[...], wg_ref[...], bg_ref[...], wo_ref[...])[-1]
        o_ref[...] = x_ref[...] + gt_ref[0] * out

    return pl.pallas_call(
        body, name="ssm_post", grid=(n_rows // ROW_TILE,),
        in_specs=[_rows(d), _rows(e), _rows(e), _rows(e), _seg(d), _full(w_glu.shape), _full((1, e)), _full(w_out.shape)],
        out_specs=_rows(d), out_shape=_sds((n_rows, d)),
        compiler_params=_cparams(1),
    )(xa, y0, y1, z, gate, w_glu, b_glu, w_out)


def _init_acc(first, *refs):
    @pl.when(first)
    def _():
        for r in refs:
            r[...] = jnp.zeros_like(r)


def _ssm_post_bwd(dxa, y0, y1, z, gate, w_glu, b_glu, w_out, w_glu_t, w_out_t):
    n_rows, d = dxa.shape
    e = z.shape[1]

    def body(dx_ref, y0_ref, y1_ref, z_ref, gt_ref, wg_ref, bg_ref, wo_ref, wgt_ref, wot_ref,
             dy_ref, dz_ref, dgt_ref, dwo_ref, dwg_ref, dbg_ref):
        i = pl.program_id(0)
        _init_acc(i == 0, dwo_ref, dwg_ref, dbg_ref)
        _init_acc(i <= 1, dgt_ref)
        y, zz = y0_ref[...] + y1_ref[...], z_ref[...]
        er, g, sg, g2, sz, silu_z, m, out = _s5_post_math(y, zz, wg_ref[...], bg_ref[...], wo_ref[...])
        dxa_t = dx_ref[...]
        dgt_ref[0] += _sum0(dxa_t * out)
        dout = gt_ref[0] * dxa_t
        dm = _dot(dout, wot_ref[...])
        dwo_ref[...] += _dot_t0(m, dout)
        dg2 = dm * silu_z
        dz_ref[...] = dm * g2 * (sz * (1.0 + zz * (1.0 - sz)))
        dt = dg2 * g * sg * (1.0 - sg)
        dwg_ref[...] += _dot_t0(g, dt)
        dbg_ref[...] += _sum0(dt)
        dg = dg2 * sg + _dot(dt, wgt_ref[...])
        dy_ref[...] = dg * (0.5 * (1.0 + er) + y * jnp.exp(-0.5 * y * y) * INV_SQRT_2PI)

    return pl.pallas_call(
        body, name="ssm_post_bwd", grid=(n_rows // ROW_TILE,),
        in_specs=[_rows(d), _rows(e), _rows(e), _rows(e), _seg(d), _full(w_glu.shape), _full((1, e)), _full(w_out.shape),
                  _full(w_glu_t.shape), _full(w_out_t.shape)],
        out_specs=[_rows(e), _rows(e), _seg(d), _full(w_out.shape), _full(w_glu.shape), _full((1, e))],
        out_shape=[_sds((n_rows, e)), _sds((n_rows, e)), _sds((2, 1, d)), _sds(w_out.shape), _sds(w_glu.shape), _sds((1, e))],
        compiler_params=_cparams(1),
    )(dxa, y0, y1, z, gate, w_glu, b_glu, w_out, w_glu_t, w_out_t)


def _ssm_in_bwd(du0, du1, dz, xa, dxa_next, g, scale, shift, w_in_t):
    n_rows, d = xa.shape
    e = dz.shape[1]
    n_lat = n_rows - ROW_TILE

    def body(du0_ref, du1_ref, dz_ref, x_ref, dn_ref, g_ref, sc_ref, sh_ref, wt_ref,
             gx_ref, dw_ref, dg_ref, dsc_ref, dsh_ref):
        i = pl.program_id(0)
        _init_acc(i == 0, dw_ref, dg_ref)
        _init_acc(i <= 1, dsc_ref, dsh_ref)
        x = x_ref[...]
        h = _rms_mod(x, g_ref[...], sc_ref[0], sh_ref[0])
        dproj = jnp.concatenate([du0_ref[...] + du1_ref[...], dz_ref[...]], axis=1)
        dh = _dot(dproj, wt_ref[...])
        dw_ref[...] += _dot_t0(h, dproj)
        dx, dg, dsc, dsh = _rms_mod_bwd(x, g_ref[...], sc_ref[0], dh)
        dg_ref[...] += dg
        dsc_ref[0] += dsc
        dsh_ref[0] += dsh
        gx_ref[...] = dn_ref[...] + dx

    return pl.pallas_call(
        body, name="ssm_in_bwd", grid=(n_rows // ROW_TILE,),
        in_specs=[_rows(e), _rows(e), _rows(e), _rows(d), _rows(d), _full((1, d)), _seg(d), _seg(d), _full(w_in_t.shape)],
        out_specs=[_rows_lat(d), _full((d, 2 * e)), _full((1, d)), _seg(d), _seg(d)],
        out_shape=[_sds((n_lat, d)), _sds((d, 2 * e)), _sds((1, d)), _sds((2, 1, d)), _sds((2, 1, d))],
        compiler_params=_cparams(1),
    )(du0, du1, dz, xa, dxa_next, g, scale, shift, w_in_t)


Q_WIDTH = N_Q_HEADS * HEAD_DIM
SM_SCALE = 1.0 / math.sqrt(HEAD_DIM)


def _attn_in(xa, g, scale, shift, w_in, q_gain, k_gain, cos, sin):
    n_rows, d = xa.shape
    qk = Q_WIDTH + KV_WIDTH

    def body(x_ref, g_ref, sc_ref, sh_ref, w_ref, qg_ref, kg_ref, cos_ref, sin_ref, hq_ref, hk_ref,
             q_ref, k_ref, v_ref, z_ref, raw_ref):
        h = _rms_mod(x_ref[...], g_ref[...], sc_ref[0], sh_ref[0])
        proj = _dot(h, w_ref[...])
        q_raw, k_raw = proj[:, :Q_WIDTH], proj[:, Q_WIDTH:qk]
        cos, sin = cos_ref[...], sin_ref[...]
        q = _head_norm_rope(q_raw, qg_ref[...], _lanes(cos, Q_WIDTH), _lanes(sin, Q_WIDTH), hq_ref[...])
        k = _head_norm_rope(k_raw, kg_ref[...], _lanes(cos, KV_WIDTH), _lanes(sin, KV_WIDTH), hk_ref[...])
        q_ref[...] = (q * SM_SCALE).astype(BF16)
        k_ref[...] = k.astype(BF16)
        v_ref[...] = proj[:, qk:qk + KV_WIDTH].astype(BF16)
        z_ref[...] = proj[:, qk + KV_WIDTH:]
        raw_ref[...] = proj[:, :qk]

    return pl.pallas_call(
        body, name="attn_in", grid=(n_rows // ROW_TILE,),
        in_specs=[_rows(d), _full((1, d)), _seg(d), _seg(d), _full(w_in.shape), _full((1, Q_WIDTH)), _full((1, KV_WIDTH)),
                  _rows(128), _rows(128), _full((Q_WIDTH, 128)), _full((KV_WIDTH, 128))],
        out_specs=[_rows_lat(Q_WIDTH), _rows(KV_WIDTH), _rows(KV_WIDTH), _rows(Q_WIDTH), _rows(qk)],
        out_shape=[_sds((n_rows - ROW_TILE, Q_WIDTH), BF16), _sds((n_rows, KV_WIDTH), BF16), _sds((n_rows, KV_WIDTH), BF16),
                   _sds((n_rows, Q_WIDTH)), _sds((n_rows, qk))],
        compiler_params=_cparams(1),
    )(xa, g, scale, shift, w_in, q_gain, k_gain, cos, sin, _head_of_lane(Q_WIDTH), _head_of_lane(KV_WIDTH))


GROUP_WIDTH = KV_REP * HEAD_DIM


def _stack_heads(ref):
    return jnp.concatenate([ref[:, h * HEAD_DIM:(h + 1) * HEAD_DIM] for h in range(KV_REP)], axis=0)


def _unstack_heads(a_t, tq):
    return jnp.concatenate([a_t[:, h * tq:(h + 1) * tq].T for h in range(KV_REP)], axis=1)


def _kv_tile(n_keys):
    return 768 if n_keys % 768 == 0 else 256


def _q_tile(n_lat):
    return 512 if n_lat % 512 == 0 else 256


def _flash_fwd(q, k, v_t):
    n_lat = q.shape[0]
    tq = _q_tile(n_lat)
    rows = KV_REP * tq
    n_kv, tk, n_q = k.shape[1], k.shape[2], n_lat // tq

    v_rows = v_t.shape[2]

    def body(q_ref, k_ref, vt_ref, o_ref, lse_ref):
        q = _stack_heads(q_ref)

        def step(j, carry):
            m_prev, acc = carry
            s_t = _dot_t1(k_ref[0, j], q)
            m_new = jnp.maximum(m_prev, jnp.max(s_t, axis=0, keepdims=True))
            alpha = jnp.exp(m_prev - m_new)
            p_t = jnp.exp(s_t - m_new)
            return m_new, alpha * acc + _dot(vt_ref[0, j], p_t)

        init = (jnp.full((1, rows), -jnp.inf, F32), jnp.zeros((v_rows, rows), F32))
        m, acc = lax.fori_loop(0, n_kv, step, init)
        l = acc[HEAD_DIM:HEAD_DIM + 1]
        o_ref[...] = _unstack_heads(acc[:HEAD_DIM] / l, tq)
        lse_ref[0, 0] = m + jnp.log(l)

    kv_all = lambda a: pl.BlockSpec((1,) + a.shape[1:], lambda g, i: (g, 0, 0, 0))
    return pl.pallas_call(
        body, name="flash_fwd", grid=(N_KV_HEADS, n_q),
        in_specs=[pl.BlockSpec((tq, GROUP_WIDTH), lambda g, i: (i, g)), kv_all(k), kv_all(v_t)],
        out_specs=[pl.BlockSpec((tq, GROUP_WIDTH), lambda g, i: (i, g)),
                   pl.BlockSpec((1, 1, 1, rows), lambda g, i: (g, i, 0, 0))],
        out_shape=[_sds((n_lat, Q_WIDTH)), _sds((N_KV_HEADS, n_q, 1, rows))],
        compiler_params=_cparams(2),
    )(q, k, v_t)


def _flash_bwd(q, k, k_t, v, do, lse_t, delta_t):
    n_lat = q.shape[0]
    tq = _q_tile(n_lat)
    rows = KV_REP * tq
    n_kv, tk, n_q = k.shape[1], k.shape[2], n_lat // tq

    def body(q_ref, k_ref, kt_ref, v_ref, do_ref, lse_ref, dl_ref, dq_ref, dk_ref, dv_ref):
        _init_acc(pl.program_id(1) == 0, dk_ref, dv_ref)
        q, do = _stack_heads(q_ref), _stack_heads(do_ref)
        lse, delta = lse_ref[0, 0], dl_ref[0, 0]

        def step(j, dq_acc):
            p_t = jnp.exp(_dot_t1(k_ref[0, j], q) - lse)
            dv_ref[0, j] += _dot(p_t, do)
            ds_t = p_t * (_dot_t1(v_ref[0, j], do) - delta)
            dk_ref[0, j] += _dot(ds_t, q)
            return dq_acc + _dot(kt_ref[0, j], ds_t)

        dq = lax.fori_loop(0, n_kv, step, jnp.zeros((HEAD_DIM, rows), F32))
        dq_ref[...] = _unstack_heads(dq, tq)

    qspec = pl.BlockSpec((tq, GROUP_WIDTH), lambda g, i: (i, g))
    rowspec = pl.BlockSpec((1, 1, 1, rows), lambda g, i: (g, i, 0, 0))
    kv_all = lambda a: pl.BlockSpec((1,) + a.shape[1:], lambda g, i: (g, 0, 0, 0))
    return pl.pallas_call(
        body, name="flash_bwd", grid=(N_KV_HEADS, n_q),
        in_specs=[qspec, kv_all(k), kv_all(k_t), kv_all(v), qspec, rowspec, rowspec],
        out_specs=[qspec, kv_all(k), kv_all(k)],
        out_shape=[_sds((n_lat, Q_WIDTH)), _sds(k.shape), _sds(k.shape)],
        compiler_params=_cparams(2),
    )(q, k, k_t, v, do, lse_t, delta_t)


def _to_lane_stacked(a, tq):
    n_lat = a.shape[0]
    a = a.reshape(n_lat // tq, tq, N_KV_HEADS, KV_REP).transpose(2, 0, 3, 1)
    return a.reshape(N_KV_HEADS, n_lat // tq, 1, KV_REP * tq)


def _attn_post_loss(o, z, xa, gate, w_out, w_out_t, final_g, target):
    n_lat, d = target.shape
    e = o.shape[1]
    head_of_lane = (jnp.arange(e)[:, None] // HEAD_DIM == jnp.arange(128)[None, :]).astype(BF16)

    def body(o_ref, z_ref, x_ref, gt_ref, w_ref, wt_ref, fg_ref, tg_ref, hl_ref,
             do_ref, dl_ref, dz_ref, dx_ref, loss_ref, dfg_ref, dgt_ref, dw_ref):
        _init_acc(pl.program_id(0) == 0, loss_ref, dfg_ref, dgt_ref, dw_ref)
        oo, zz, gate_t, fg = o_ref[...], z_ref[...], gt_ref[0], fg_ref[...]
        sz = _sigmoid(zz)
        silu_z = zz * sz
        m = oo * silu_z
        out = _dot(m, w_ref[...])
        x2 = x_ref[...] + gate_t * out
        r = lax.rsqrt(jnp.mean(x2 * x2, axis=-1, keepdims=True) + NORM_EPS)
        n = x2 * r
        err = n * fg - tg_ref[...]
        loss_ref[...] += 0.5 * jnp.sum(jnp.mean(err * err, axis=-1, keepdims=True), axis=0, keepdims=True)
        dy = err * (1.0 / d)
        dfg_ref[...] += _sum0(dy * n)
        dn = dy * fg
        dx2 = r * (dn - n * jnp.mean(dn * n, axis=-1, keepdims=True))
        dx_ref[...] = dx2
        dgt_ref[...] += _sum0(dx2 * out)
        dout = gate_t * dx2
        dw_ref[...] += _dot_t0(m, dout)
        dm = _dot(dout, wt_ref[...])
        do = dm * silu_z
        do_ref[...] = do.astype(BF16)
        prod = do * oo
        hi = prod.astype(BF16)
        lo = (prod - hi.astype(F32)).astype(BF16)
        dl_ref[...] = _dot(hi, hl_ref[...]) + _dot(lo, hl_ref[...])
        dz_ref[...] = dm * oo * (sz * (1.0 + zz * (1.0 - sz)))

    return pl.pallas_call(
        body, name="attn_post_loss", grid=(n_lat // ROW_TILE,),
        in_specs=[_rows(e), _rows_skip_ctx(e), _rows_skip_ctx(d), _lat_seg(d), _full(w_out.shape), _full(w_out_t.shape),
                  _full((1, d)), _rows(d), _full((e, 128))],
        out_specs=[_rows(e), _rows(128), _rows(e), _rows(d), _full((1, 1)), _full((1, d)), _full((1, d)), _full(w_out.shape)],
        out_shape=[_sds((n_lat, e), BF16), _sds((n_lat, 128)), _sds((n_lat, e)), _sds((n_lat, d)), _sds((1, 1)), _sds((1, d)),
                   _sds((1, d)), _sds(w_out.shape)],
        compiler_params=_cparams(1),
    )(o, z, xa, gate, w_out, w_out_t, final_g, target, head_of_lane)


def _attn_in_bwd(dq, dk, dv, dz, raw, xa, dx2, g, scale, shift, q_gain, k_gain, cos, sin, w_in_t):
    n_rows, d = xa.shape
    qk = Q_WIDTH + KV_WIDTH
    n_in = w_in_t.shape[0]

    def body(dq_ref, dk_ref, dv_ref, dz_ref, raw_ref, x_ref, dx2_ref, g_ref, sc_ref, sh_ref, qg_ref, kg_ref, cos_ref, sin_ref,
             wt_ref, hq_ref, hk_ref, dxa_ref, dw_ref, dqg_ref, dkg_ref, dg_ref, dsc_ref, dsh_ref):
        i = pl.program_id(0)
        _init_acc(i == 0, dw_ref, dqg_ref, dkg_ref, dg_ref)
        _init_acc(i <= 1, dsc_ref, dsh_ref)
        is_lat = (i > 0).astype(F32)
        x = x_ref[...]
        h = _rms_mod(x, g_ref[...], sc_ref[0], sh_ref[0])
        cos, sin = cos_ref[...], sin_ref[...]
        raw_t = raw_ref[...]
        dq_raw, dqg = _head_norm_rope_bwd(raw_t[:, :Q_WIDTH], qg_ref[...], _lanes(cos, Q_WIDTH), _lanes(sin, Q_WIDTH),
                                          hq_ref[...], dq_ref[...] * (SM_SCALE * is_lat))
        dk_raw, dkg = _head_norm_rope_bwd(raw_t[:, Q_WIDTH:], kg_ref[...], _lanes(cos, KV_WIDTH), _lanes(sin, KV_WIDTH),
                                          hk_ref[...], dk_ref[...])
        dqg_ref[...] += dqg
        dkg_ref[...] += dkg
        dproj = jnp.concatenate([dq_raw, dk_raw, dv_ref[...], dz_ref[...] * is_lat], axis=1)
        dh = _dot(dproj, wt_ref[...])
        dw_ref[...] += _dot_t0(h, dproj)
        dx, dg, dsc, dsh = _rms_mod_bwd(x, g_ref[...], sc_ref[0], dh)
        dg_ref[...] += dg
        dsc_ref[0] += dsc
        dsh_ref[0] += dsh
        dxa_ref[...] = dx + dx2_ref[...] * is_lat

    return pl.pallas_call(
        body, name="attn_in_bwd", grid=(n_rows // ROW_TILE,),
        in_specs=[_rows_lat(Q_WIDTH), _rows(KV_WIDTH), _rows(KV_WIDTH), _rows_lat(Q_WIDTH), _rows(qk), _rows(d), _rows_lat(d),
                  _full((1, d)), _seg(d), _seg(d), _full((1, Q_WIDTH)), _full((1, KV_WIDTH)), _rows(128), _rows(128),
                  _full(w_in_t.shape), _full((Q_WIDTH, 128)), _full((KV_WIDTH, 128))],
        out_specs=[_rows(d), _full((d, n_in)), _full((1, Q_WIDTH)), _full((1, KV_WIDTH)), _full((1, d)), _seg(d), _seg(d)],
        out_shape=[_sds((n_rows, d)), _sds((d, n_in)), _sds((1, Q_WIDTH)), _sds((1, KV_WIDTH)), _sds((1, d)),
                   _sds((2, 1, d)), _sds((2, 1, d))],
        compiler_params=_cparams(1),
    )(dq, dk, dv, dz, raw, xa, dx2, g, scale, shift, q_gain, k_gain, cos, sin, w_in_t,
      _head_of_lane(Q_WIDTH), _head_of_lane(KV_WIDTH))


def _heads_major(a, n_heads):
    return a.reshape(a.shape[0], n_heads, HEAD_DIM).transpose(1, 0, 2)


def _tokens_major(a):
    return a.transpose(1, 0, 2).reshape(a.shape[1], a.shape[0] * HEAD_DIM)


def _local_step(x, ctx, target, mods, norm_g, ssm, w_ssm_in, w_glu, b_glu, w_ssm_out, w_attn_in, q_norm, k_norm, w_attn_out,
                final_g, attn_exchange=None):
    n_ctx, d = ctx.shape
    assert n_ctx == ROW_TILE
    n_lat = x.shape[0]
    (shift0, scale0, gate0), (shift1, scale1, gate1) = mods
    g0, g1, fg = norm_g[0:1], norm_g[1:2], final_g.reshape(1, d)
    b_glu = b_glu.reshape(1, -1)
    q_gain = jnp.tile(q_norm.reshape(1, HEAD_DIM), (1, N_Q_HEADS))
    k_gain = jnp.tile(k_norm.reshape(1, HEAD_DIM), (1, N_KV_HEADS))
    cos, sin = _rope_tables(n_ctx, n_lat)

    xa0 = jnp.concatenate([ctx, x], axis=0)
    u, z0 = _ssm_in(xa0, g0, scale0, shift0, w_ssm_in)
    (y0, y1), saved, gathered = _s5_forward(u, ssm, n_ctx, (attn_exchange[0], False) if attn_exchange else None)
    if attn_exchange:
        w_attn_in, w_attn_out = attn_exchange[1](gathered)
    xa1 = _ssm_post(xa0, y0, y1, z0, gate0, w_glu, b_glu, w_ssm_out)

    q, k, v, z1, raw = _attn_in(xa1, g1, scale1, shift1, w_attn_in, q_gain, k_gain, cos, sin)
    tq, tk = _q_tile(n_lat), _kv_tile(n_ctx + n_lat)
    key_blocks = lambda a: _heads_major(a, N_KV_HEADS).reshape(N_KV_HEADS, -1, tk, HEAD_DIM)
    k_b, v_b = key_blocks(k), key_blocks(v)
    v_t_ones = jnp.concatenate([v_b.transpose(0, 1, 3, 2), jnp.ones((N_KV_HEADS, k_b.shape[1], 16, tk), BF16)], axis=2)
    o, lse_t = _flash_fwd(q, k_b, v_t_ones)
    do, delta, dz1, dx2, loss, d_fg, d_gate1, d_w_attn_out = _attn_post_loss(
        o, z1, xa1, gate1, w_attn_out, w_attn_out.T, fg, target)

    dq, dk_b, dv_b = _flash_bwd(q, k_b, k_b.transpose(0, 1, 3, 2), v_b, do, lse_t, _to_lane_stacked(delta[:, :N_Q_HEADS], tq))
    keys_major = lambda a: _tokens_major(a.reshape(N_KV_HEADS, -1, HEAD_DIM))
    dxa1, d_w_attn_in, d_qg, d_kg, d_g1, d_scale1, d_shift1 = _attn_in_bwd(
        dq, keys_major(dk_b), keys_major(dv_b), dz1, raw, xa1, dx2, g1, scale1, shift1,
        q_gain, k_gain, cos, sin, w_attn_in.T)
    dy, dz0, d_gate0, d_w_ssm_out, d_w_glu, d_b_glu = _ssm_post_bwd(
        dxa1, y0, y1, z0, gate0, w_glu, b_glu, w_ssm_out, w_glu.T, w_ssm_out.T)
    rider = (attn_exchange[2](d_w_attn_in, d_w_attn_out), True) if attn_exchange else None
    (du0, du1), d_ssm, attn_parts = _s5_backward(u, dy, ssm, saved, n_ctx, rider)
    grad_x, d_w_ssm_in, d_g0, d_scale0, d_shift0 = _ssm_in_bwd(du0, du1, dz0, xa0, dxa1, g0, scale0, shift0, w_ssm_in.T)

    d_gate1_seg = jnp.concatenate([jnp.zeros((1, 1, d), F32), d_gate1.reshape(1, 1, d)], axis=0)
    grads = dict(
        norm_g=jnp.concatenate([d_g0, d_g1], axis=0),
        ssm_w_in=d_w_ssm_in, ssm=d_ssm, ssm_w_glu=d_w_glu, ssm_b_glu=d_b_glu.reshape(-1), ssm_w_out=d_w_ssm_out,
        attn_q_norm=d_qg.reshape(N_Q_HEADS, HEAD_DIM).sum(0), attn_k_norm=d_kg.reshape(N_KV_HEADS, HEAD_DIM).sum(0),
        final_norm_g=d_fg.reshape(-1))
    if attn_exchange:
        grads.update(attn_parts=attn_parts)
    else:
        grads.update(attn_w_in=d_w_attn_in, attn_w_out=d_w_attn_out)
    d_mods = ((d_shift0, d_scale0, d_gate0), (d_shift1, d_scale1, d_gate1_seg))
    return loss[0, 0], grad_x, grads, d_mods


def _my_index():
    return 4 * lax.axis_index("x") + 2 * lax.axis_index("y") + lax.axis_index("c")


def _peer(k):
    mx, my, mc = lax.axis_index("x"), lax.axis_index("y"), lax.axis_index("c")
    px = 1 - mx if k & 4 else mx
    py = 1 - my if k & 2 else my
    pc = 1 - mc if k & 1 else mc
    return (px, py, pc), 4 * px + 2 * py + pc


HBM_SPEC = pl.BlockSpec(memory_space=pl.ANY)


def _exchange(x, name, all_to_all):
    def body(x_ref, out_ref, send_sems, recv_sems, local_sem):
        _exchange_copies(all_to_all, x_ref, out_ref, send_sems, recv_sems, local_sem, start=True)
        _exchange_copies(all_to_all, x_ref, out_ref, send_sems, recv_sems, local_sem, start=False)

    return pl.pallas_call(
        body, name=name, in_specs=[HBM_SPEC], out_specs=HBM_SPEC,
        out_shape=_exchange_out_shape(x, all_to_all), scratch_shapes=_exchange_semaphores(),
    )(x)


def _exchange_out_shape(x, all_to_all):
    return _sds((N_DEV,) + tuple(x.shape[1:] if all_to_all else x.shape), x.dtype)


def _exchange_semaphores():
    return [pltpu.SemaphoreType.DMA((N_DEV - 1,)), pltpu.SemaphoreType.DMA((N_DEV - 1,)), pltpu.SemaphoreType.DMA]


def _exchange_copies(all_to_all, x_ref, out_ref, send_sems, recv_sems, local_sem, start):
    me = _my_index()
    mine = pltpu.make_async_copy(x_ref.at[me] if all_to_all else x_ref, out_ref.at[me], local_sem)
    if start:
        mine.start()
    for k in range(1, N_DEV):
        peer, peer_idx = _peer(k)
        send = pltpu.make_async_remote_copy(
            src_ref=x_ref.at[peer_idx] if all_to_all else x_ref, dst_ref=out_ref.at[me],
            send_sem=send_sems.at[k - 1], recv_sem=recv_sems.at[k - 1], device_id=peer, device_id_type=MESH_IDS)
        if start:
            send.start()
        else:
            pltpu.make_async_remote_copy(
                src_ref=x_ref.at[me] if all_to_all else x_ref, dst_ref=out_ref.at[peer_idx],
                send_sem=send_sems.at[k - 1], recv_sem=recv_sems.at[k - 1], device_id=peer,
                device_id_type=MESH_IDS).wait_recv()
            send.wait_send()
    if not start:
        mine.wait()


def _ride(rider, first, last, refs):
    @pl.when(first)
    def _():
        _exchange_copies(rider[1], *refs, start=True)

    @pl.when(last)
    def _():
        _exchange_copies(rider[1], *refs, start=False)


MOD_ROWS = 16
CTX_ROW = N_DEV


def _mod_fwd(cond, w_shard, b_cols):
    n_layers, d, cols = w_shard.shape

    def body(c_ref, w_ref, b_ref, o_ref):
        c = c_ref[...]
        s = c * _sigmoid(c)
        for i in range(n_layers):
            o_ref[i] = _dot(s, w_ref[i]) + b_ref[i]

    return pl.pallas_call(
        body, name="mod_fwd", out_shape=_sds((n_layers, MOD_ROWS, cols)),
        compiler_params=pltpu.CompilerParams(vmem_limit_bytes=VMEM_LIMIT),
    )(cond, w_shard, b_cols.reshape(n_layers, 1, cols))


def _mod_bwd(cond, d_lat_cols, d_ctx_cols, w_shard):
    n_layers, d, cols = w_shard.shape

    def body(c_ref, dl_ref, dc_ref, w_ref, dw_ref, dcc_ref):
        c = c_ref[...]
        sg = _sigmoid(c)
        s = c * sg
        d_s = jnp.zeros((MOD_ROWS, d), F32)
        for i in range(n_layers):
            d_ctx = dc_ref[0, i]
            for j in range(1, N_DEV):
                d_ctx = d_ctx + dc_ref[j, i]
            dm = jnp.concatenate([dl_ref[i], d_ctx, jnp.zeros((MOD_ROWS - N_DEV - 1, cols), F32)], axis=0)
            dw_ref[i] = _dot_t0(s, dm)
            d_s = d_s + _dot_t1(dm, w_ref[i])
        d_c = d_s * (sg * (1.0 + c * (1.0 - sg)))
        dcc_ref[...] = d_c[CTX_ROW:CTX_ROW + 1]

    return pl.pallas_call(
        body, name="mod_bwd", out_shape=[_sds((n_layers, d, cols)), _sds((1, d))],
        compiler_params=pltpu.CompilerParams(vmem_limit_bytes=VMEM_LIMIT),
    )(cond, d_lat_cols, d_ctx_cols, w_shard)


ADAM_TILE = 512


def _adamw(w, g_parts, m, v, name):
    n_parts, n_rows, lanes = g_parts.shape
    c1 = 1.0 - ADAM_B1 ** ADAM_STEP
    c2 = 1.0 - ADAM_B2 ** ADAM_STEP

    def body(w_ref, g_ref, m_ref, v_ref, go_ref, d_ref, mo_ref, vo_ref):
        g = g_ref[0].astype(F32)
        for p in range(1, n_parts):
            g = g + g_ref[p].astype(F32)
        m_new = ADAM_B1 * m_ref[...] + (1.0 - ADAM_B1) * g
        v_new = ADAM_B2 * v_ref[...] + (1.0 - ADAM_B2) * (g * g)
        go_ref[...] = g
        mo_ref[...] = m_new
        vo_ref[...] = v_new
        d_ref[...] = -ADAM_LR * ((m_new / c1) / (jnp.sqrt(v_new / c2) + ADAM_EPS) + ADAM_WD * w_ref[...])

    row = pl.BlockSpec((ADAM_TILE, lanes), lambda i: (i, 0))
    return pl.pallas_call(
        body, name=name, grid=(n_rows // ADAM_TILE,),
        in_specs=[row, pl.BlockSpec((n_parts, ADAM_TILE, lanes), lambda i: (0, i, 0)), row, row],
        out_specs=[row] * 4, out_shape=[_sds((n_rows, lanes))] * 4,
        compiler_params=_cparams(1),
    )(w, g_parts, m, v)


def _sum_parts(parts):
    n_parts, n_rows, lanes = parts.shape

    def body(p_ref, o_ref):
        acc = p_ref[0]
        for p in range(1, n_parts):
            acc = acc + p_ref[p]
        o_ref[...] = acc

    return pl.pallas_call(body, name="sum_parts", out_shape=_sds((n_rows, lanes)))(parts)


def _pack(arrays, row_multiple):
    parts = []
    for a in arrays:
        flat = a.reshape(-1)
        parts.append(jnp.pad(flat, (0, (-flat.shape[0]) % 1024)))
    flat = jnp.concatenate(parts)
    flat = jnp.pad(flat, (0, (-flat.shape[0]) % (row_multiple * 128)))
    return flat.reshape(-1, 128)


def _unpack(packed, shapes):
    flat = packed.reshape(-1)
    out, pos = [], 0
    for s in shapes:
        n = math.prod(s)
        out.append(flat[pos:pos + n].reshape(s))
        pos += n + (-n) % 1024
    return out


WEIGHT_NAMES = ['c_ctx', 'w_mod', 'b_mod', 'norm_g', 'ssm_w_in', 'ssm_a_re', 'ssm_a_im', 'ssm_log_dt', 'ssm_b_re', 'ssm_b_im',
                'ssm_c_re', 'ssm_c_im', 'ssm_d', 'ssm_w_glu', 'ssm_b_glu', 'ssm_w_out', 'attn_w_in', 'attn_q_norm',
                'attn_k_norm', 'attn_w_out', 'final_norm_g']
SSM_SHARDED = ['ssm_w_in', 'ssm_w_glu', 'ssm_w_out']
ATTN_SHARDED = ['attn_w_in', 'attn_w_out']
SHARDED = SSM_SHARDED + ATTN_SHARDED
COLUMN_SHARDED = ('ssm_w_in', 'attn_w_in')
REPLICATED = ['c_ctx', 'b_mod', 'norm_g', 'ssm_a_re', 'ssm_a_im', 'ssm_log_dt', 'ssm_b_re', 'ssm_b_im', 'ssm_c_re', 'ssm_c_im',
              'ssm_d', 'ssm_b_glu', 'attn_q_norm', 'attn_k_norm', 'final_norm_g']
SSM_NAMES = ['ssm_a_re', 'ssm_a_im', 'ssm_log_dt', 'ssm_b_re', 'ssm_b_im', 'ssm_c_re', 'ssm_c_im', 'ssm_d']


def _full_from_shards(gathered, name, shard_shape):
    rows, cols = shard_shape
    w = gathered.reshape(N_DEV, rows, cols)
    if name in COLUMN_SHARDED:
        return w.transpose(1, 0, 2).reshape(rows, N_DEV * cols)
    return w.reshape(N_DEV * rows, cols)


def _shards_from_full(g, name):
    if name in COLUMN_SHARDED:
        rows, cols = g.shape
        g = g.reshape(rows, N_DEV, cols // N_DEV).transpose(1, 0, 2)
    return g.reshape(N_DEV, -1, 128)


def kernel(x, c, ctx, c_ctx, w_mod, b_mod, norm_g, ssm_w_in, ssm_a_re, ssm_a_im, ssm_log_dt, ssm_b_re, ssm_b_im, ssm_c_re, ssm_c_im, ssm_d, ssm_w_glu, ssm_b_glu, ssm_w_out, attn_w_in, attn_q_norm, attn_k_norm, attn_w_out, final_norm_g, loss_target, m_c_ctx, m_w_mod, m_b_mod, m_norm_g, m_ssm_w_in, m_ssm_a_re, m_ssm_a_im, m_ssm_log_dt, m_ssm_b_re, m_ssm_b_im, m_ssm_c_re, m_ssm_c_im, m_ssm_d, m_ssm_w_glu, m_ssm_b_glu, m_ssm_w_out, m_attn_w_in, m_attn_q_norm, m_attn_k_norm, m_attn_w_out, m_final_norm_g, v_c_ctx, v_w_mod, v_b_mod, v_norm_g, v_ssm_w_in, v_ssm_a_re, v_ssm_a_im, v_ssm_log_dt, v_ssm_b_re, v_ssm_b_im, v_ssm_c_re, v_ssm_c_im, v_ssm_d, v_ssm_w_glu, v_ssm_b_glu, v_ssm_w_out, v_attn_w_in, v_attn_q_norm, v_attn_k_norm, v_attn_w_out, v_final_norm_g):
    env = dict(locals())
    weights = {n: env[n] for n in WEIGHT_NAMES}
    mom_m = {n: env["m_" + n] for n in WEIGHT_NAMES}
    mom_v = {n: env["v_" + n] for n in WEIGHT_NAMES}
    d = D_MODEL
    me = _my_index()
    mod_cols = w_mod.shape[-1]

    c_all = _exchange(c.reshape(8, d // 8), "gather_c", False).reshape(N_DEV, d)
    cond = jnp.concatenate([c_all, c_ctx.reshape(1, d), jnp.zeros((MOD_ROWS - N_DEV - 1, d), F32)], axis=0)
    shard_shapes = {n: weights[n].shape[1:] for n in SHARDED}
    pack_shards = lambda names: _pack([weights[n] for n in names], 1).astype(BF16)

    def unpack_full(gathered, names):
        full, pos = [], 0
        for n in names:
            rows = math.prod(shard_shapes[n]) // 128
            full.append(_full_from_shards(gathered[:, pos:pos + rows], n, shard_shapes[n]))
            pos += rows
        return full

    w_ssm_in, w_ssm_glu, w_ssm_out = unpack_full(_exchange(pack_shards(SSM_SHARDED), "gather_ssm_weights", False), SSM_SHARDED)
    attn_exchange = (
        pack_shards(ATTN_SHARDED),
        lambda gathered: unpack_full(gathered, ATTN_SHARDED),
        lambda *grads: jnp.concatenate([_shards_from_full(t, n) for t, n in zip(grads, ATTN_SHARDED)], axis=1).astype(BF16))

    b_cols = lax.dynamic_slice(b_mod, (0, me * mod_cols), (2, mod_cols))
    mod_shard = _mod_fwd(cond, w_mod, b_cols)
    mod_all = _exchange(mod_shard.reshape(2 * MOD_ROWS, mod_cols), "gather_mod", False)
    mod_full = mod_all.reshape(N_DEV, 2, MOD_ROWS, mod_cols).transpose(1, 2, 0, 3).reshape(2, MOD_ROWS, 3 * d)
    lat_rows = lax.dynamic_slice(mod_full, (0, me, 0), (2, 1, 3 * d))
    mods = []
    for i in range(2):
        seg = jnp.stack([mod_full[i, CTX_ROW:CTX_ROW + 1], lat_rows[i]])
        mods.append((seg[:, :, :d], seg[:, :, d:2 * d], seg[:, :, 2 * d:]))

    ssm = tuple(weights[n][0] for n in SSM_NAMES)
    loss, grad_x, g, d_mods = _local_step(
        x[0], ctx[0], loss_target[0], mods, norm_g, ssm, w_ssm_in, w_ssm_glu, ssm_b_glu[0], w_ssm_out,
        None, attn_q_norm[0], attn_k_norm[0], None, final_norm_g, attn_exchange)
    loss = lax.psum(loss, ("x", "y", "c"))

    d_rows = jnp.stack([jnp.concatenate(dm, axis=-1) for dm in d_mods])
    d_rows = jnp.concatenate([d_rows.reshape(4, 3 * d), jnp.zeros((4, 3 * d), F32)], axis=0)
    d_all = _exchange(d_rows, "gather_dmod", False)[:, :4].reshape(N_DEV, 2, 2, 3 * d)
    d_all = lax.dynamic_slice(d_all, (0, 0, 0, me * mod_cols), (N_DEV, 2, 2, mod_cols))
    d_w_mod, d_c_ctx = _mod_bwd(cond, d_all[:, :, 1].transpose(1, 0, 2), d_all[:, :, 0:1], w_mod)
    d_b_mod = jnp.stack([jnp.concatenate([t[0] + t[1] for t in dm], axis=-1).reshape(3 * d) for dm in d_mods])

    g_ssm = jnp.concatenate([_shards_from_full(g[n], n) for n in SSM_SHARDED], axis=1)
    g_big_parts = jnp.concatenate([_exchange(g_ssm.astype(BF16), "scatter_ssm_grads", True), g['attn_parts']], axis=1)
    pack_big = lambda t: _pack([t[n] for n in SHARDED], ADAM_TILE)
    big = _adamw(pack_big(weights), g_big_parts, pack_big(mom_m), pack_big(mom_v), "adamw_sharded")
    big = [_unpack(t, [weights[n].shape for n in SHARDED]) for t in big]

    mod_res = _adamw(_pack([w_mod], ADAM_TILE), _pack([d_w_mod], ADAM_TILE)[None], _pack([m_w_mod], ADAM_TILE),
                     _pack([v_w_mod], ADAM_TILE), "adamw_w_mod")
    mod_res = [t.reshape(w_mod.shape) for t in mod_res]

    small = dict(zip(SSM_NAMES, g['ssm']))
    small.update(c_ctx=d_c_ctx, b_mod=d_b_mod, norm_g=g['norm_g'], ssm_b_glu=g['ssm_b_glu'], attn_q_norm=g['attn_q_norm'],
                 attn_k_norm=g['attn_k_norm'], final_norm_g=g['final_norm_g'])
    pack_small = lambda t: _pack([t[n] for n in REPLICATED], ADAM_TILE)
    g_small = pack_small(small)
    slices = _exchange(g_small.reshape(N_DEV, -1, 128), "scatter_small_grads", True)
    g_small = _exchange(_sum_parts(slices), "gather_small_grads", False).reshape(1, -1, 128)
    rep = _adamw(pack_small(weights), g_small, pack_small(mom_m), pack_small(mom_v), "adamw_replicated")
    rep = [_unpack(t, [weights[n].shape for n in REPLICATED]) for t in rep]

    results = []
    for kind in range(4):
        by_name = dict(zip(SHARDED, big[kind]))
        by_name.update(zip(REPLICATED, rep[kind]))
        by_name['w_mod'] = mod_res[kind]
        results.extend(by_name[n] for n in WEIGHT_NAMES)
    return (loss, grad_x[None], *results)
```

```python
import functools
import math

import jax
import jax.numpy as jnp
from jax import lax
from jax.experimental import pallas as pl
from jax.experimental.pallas import tpu as pltpu

F32 = jnp.float32
BF16 = jnp.bfloat16

N_DEV = 8
D_MODEL = 1024
NORM_EPS = 1e-6
SSM_GROUP = 16
SSM_GROUPS = 64
SSM_STATE = 64
GROUPS_PER_BLOCK = 8
N_BLOCKS = SSM_GROUPS // GROUPS_PER_BLOCK
HALF = GROUPS_PER_BLOCK * SSM_STATE
HEAD_DIM = 64
N_Q_HEADS = 16
N_KV_HEADS = 4
KV_REP = N_Q_HEADS // N_KV_HEADS
KV_WIDTH = N_KV_HEADS * HEAD_DIM
GRID_W = 64
ROPE_THETA = 10000.0
ADAM_LR, ADAM_B1, ADAM_B2, ADAM_EPS, ADAM_WD, ADAM_STEP = 0.001, 0.9, 0.999, 1e-08, 0.01, 10

ROW_TILE = 256
SCAN_CHUNK = 256
VMEM_LIMIT = 56 * 1024 * 1024
MESH_IDS = pl.DeviceIdType.MESH


def _cparams(n_axes):
    return pltpu.CompilerParams(dimension_semantics=("arbitrary",) * n_axes, vmem_limit_bytes=VMEM_LIMIT)


def _dot(a, b):
    return jnp.dot(a.astype(BF16), b.astype(BF16), preferred_element_type=F32)


def _dot_t0(a, b):
    return lax.dot_general(a.astype(BF16), b.astype(BF16), (((0,), (0,)), ((), ())), preferred_element_type=F32)


def _dot_t1(a, b):
    return lax.dot_general(a.astype(BF16), b.astype(BF16), (((1,), (1,)), ((), ())), preferred_element_type=F32)


def _s5_prep(a_re, a_im, log_dt, b_re, b_im):
    dt = jnp.exp(log_dt)[:, None]
    ldr, ldi = a_re * dt, a_im * dt
    mag = jnp.exp(ldr)
    abar_re, abar_im = mag * jnp.cos(ldi), mag * jnp.sin(ldi)
    den = a_re * a_re + a_im * a_im
    num_re, num_im = abar_re - 1.0, abar_im
    coef_re = (num_re * a_re + num_im * a_im) / den
    coef_im = (num_im * a_re - num_re * a_im) / den
    bbar_re = coef_re[..., None] * b_re - coef_im[..., None] * b_im
    bbar_im = coef_re[..., None] * b_im + coef_im[..., None] * b_re
    return abar_re, abar_im, bbar_re, bbar_im


def _s5_blocks(abar_re, abar_im, bbar_re, bbar_im, c_re, c_im):
    eye = jnp.eye(GROUPS_PER_BLOCK, dtype=F32)
    bb = jnp.stack([bbar_re, bbar_im]).reshape(2, N_BLOCKS, GROUPS_PER_BLOCK, SSM_STATE, SSM_GROUP)
    b_blk = jnp.einsum('rqgph,gk->qghrkp', bb, eye).reshape(N_BLOCKS, 128, 2 * HALF)
    cc = jnp.stack([c_re, -c_im]).reshape(2, N_BLOCKS, GROUPS_PER_BLOCK, SSM_GROUP, SSM_STATE)
    c_blk = jnp.einsum('rqghp,gk->qrgpkh', cc, eye).reshape(N_BLOCKS, 2 * HALF, 128)
    abar = jnp.stack([abar_re.reshape(N_BLOCKS, HALF), abar_im.reshape(N_BLOCKS, HALF)])
    return abar, b_blk, c_blk


def _s5_unblock(d_b_blk, d_ct_blk):
    db = d_b_blk.reshape(N_BLOCKS, GROUPS_PER_BLOCK, SSM_GROUP, 2, GROUPS_PER_BLOCK, SSM_STATE)
    db = jnp.einsum('qghrgp->rqgph', db).reshape(2, SSM_GROUPS, SSM_STATE, SSM_GROUP)
    dc = d_ct_blk.reshape(N_BLOCKS, GROUPS_PER_BLOCK, SSM_GROUP, 2, GROUPS_PER_BLOCK, SSM_STATE)
    dc = jnp.einsum('qghrgp->rqghp', dc).reshape(2, SSM_GROUPS, SSM_GROUP, SSM_STATE)
    return db[0], db[1], dc[0], -dc[1]


def _scan_chunk_of_step(j, n_chunks, n_ctx_chunks, reverse):
    if not reverse:
        return j
    return jnp.where(j < n_ctx_chunks, n_ctx_chunks - 1 - j, n_chunks - 1 - j + n_ctx_chunks)


LANE_TILES = 2 * HALF // 128
RE_TILES = HALF // 128


def _tiles(v):
    return [v[:, l * 128:(l + 1) * 128] for l in range(v.shape[1] // 128)]


def _scatter_steps(s_ref, q, x):
    for l in range(LANE_TILES):
        s_ref[l, pl.ds(q, x.shape[0], stride=N_BLOCKS), :] = x[:, l * 128:(l + 1) * 128]


def _gather_steps(s_ref, q, n_steps):
    return jnp.concatenate([s_ref[l, pl.ds(q, n_steps, stride=N_BLOCKS), :] for l in range(LANE_TILES)], axis=1)


def _load_step(s_ref, t):
    row = pl.multiple_of(t * N_BLOCKS, N_BLOCKS)
    return [s_ref[l, pl.ds(row, N_BLOCKS), :] for l in range(LANE_TILES)]


def _store_step(s_ref, t, tiles):
    row = pl.multiple_of(t * N_BLOCKS, N_BLOCKS)
    for l in range(LANE_TILES):
        s_ref[l, pl.ds(row, N_BLOCKS), :] = tiles[l]


def _cmul_add(a, h, x, conj):
    re, im = [], []
    for l in range(RE_TILES):
        ar, ai, hr, hi = a[l], a[RE_TILES + l], h[l], h[RE_TILES + l]
        if conj:
            re.append(ar * hr + ai * hi + x[l])
            im.append(ar * hi - ai * hr + x[RE_TILES + l])
        else:
            re.append(ar * hr - ai * hi + x[l])
            im.append(ar * hi + ai * hr + x[RE_TILES + l])
    return re + im


def _s5_scan_fwd(u, abar, b_blk, c_blk, d_skip, n_ctx, reverse, rider=None):
    n_rows, width = u.shape
    tc = SCAN_CHUNK
    n_chunks, n_ctx_chunks = n_rows // tc, n_ctx // tc
    with_skip = d_skip is not None

    def body(*refs):
        if rider is not None:
            x_ref, ride_out, sems = refs[4 + with_skip], refs[7 + with_skip], refs[-3:]
            _ride(rider, pl.program_id(0) == 0, pl.program_id(0) == n_chunks - 1, (x_ref, ride_out) + tuple(sems))
            refs = refs[:4 + with_skip] + refs[5 + with_skip:7 + with_skip] + refs[8 + with_skip:-3]
        if with_skip:
            u_ref, a_ref, b_ref, c_ref, d_ref, y_ref, hb_ref, s_ref, h_ref = refs
        else:
            u_ref, a_ref, b_ref, c_ref, y_ref, hb_ref, s_ref, h_ref = refs
        j = pl.program_id(0)

        @pl.when(j == 0)
        def _():
            h_ref[...] = jnp.zeros_like(h_ref)

        hb_ref[0] = h_ref[...]
        for q in range(N_BLOCKS):
            _scatter_steps(s_ref, q, _dot(u_ref[:, q * 128:(q + 1) * 128], b_ref[q]))
        a = _tiles(a_ref[0]) + _tiles(a_ref[1])

        def step(s, h):
            t = tc - 1 - s if reverse else s
            h = _cmul_add(a, h, _load_step(s_ref, t), conj=False)
            _store_step(s_ref, t, h)
            return h

        h = lax.fori_loop(0, tc, step, _tiles(h_ref[...]))
        h_ref[...] = jnp.concatenate(h, axis=1)
        for q in range(N_BLOCKS):
            yq = _dot(_gather_steps(s_ref, q, tc), c_ref[q])
            if with_skip:
                yq = yq + d_ref[:, q * 128:(q + 1) * 128] * u_ref[:, q * 128:(q + 1) * 128]
            y_ref[:, q * 128:(q + 1) * 128] = yq

    chunk = functools.partial(_scan_chunk_of_step, n_chunks=n_chunks, n_ctx_chunks=n_ctx_chunks, reverse=reverse)
    full3 = lambda j: (0, 0, 0)
    in_specs = [pl.BlockSpec((tc, width), lambda j: (chunk(j), 0)),
                pl.BlockSpec((2, N_BLOCKS, HALF), full3),
                pl.BlockSpec((N_BLOCKS, 128, 2 * HALF), full3),
                pl.BlockSpec((N_BLOCKS, 2 * HALF, 128), full3)]
    args = [u, abar, b_blk.astype(BF16), c_blk.astype(BF16)]
    if with_skip:
        in_specs.append(pl.BlockSpec((1, width), lambda j: (0, 0)))
        args.append(d_skip.reshape(1, width))
    out_specs = [pl.BlockSpec((tc, width), lambda j: (chunk(j), 0)),
                 pl.BlockSpec((1, N_BLOCKS, 2 * HALF), lambda j: (chunk(j), 0, 0))]
    out_shape = [_sds((n_rows, width)), _sds((n_chunks, N_BLOCKS, 2 * HALF))]
    scratch = [pltpu.VMEM((LANE_TILES, tc * N_BLOCKS, 128), F32), pltpu.VMEM((N_BLOCKS, 2 * HALF), F32)]
    if rider is not None:
        in_specs.append(HBM_SPEC)
        args.append(rider[0])
        out_specs.append(HBM_SPEC)
        out_shape.append(_exchange_out_shape(*rider))
        scratch += _exchange_semaphores()
    return pl.pallas_call(
        body, name="s5_scan_fwd_rev" if reverse else "s5_scan_fwd",
        grid=(n_chunks,), in_specs=in_specs, out_specs=out_specs, out_shape=out_shape, scratch_shapes=scratch,
        compiler_params=_cparams(1),
    )(*args)


def _s5_scan_bwd(u, dy, hb, abar, b_blk, c_blk, d_skip, n_ctx, reverse, rider=None):
    n_rows, width = u.shape
    tc = SCAN_CHUNK
    n_chunks, n_ctx_chunks = n_rows // tc, n_ctx // tc
    with_skip = d_skip is not None
    n_in, n_out = 7 + with_skip, 4 + with_skip

    def body(*refs):
        if rider is not None:
            x_ref, ride_out, sems = refs[n_in], refs[n_in + 1 + n_out], refs[-3:]
            _ride(rider, pl.program_id(0) == 0, pl.program_id(0) == n_chunks - 1, (x_ref, ride_out) + tuple(sems))
            refs = refs[:n_in] + refs[n_in + 1:n_in + 1 + n_out] + refs[n_in + 2 + n_out:-3]
        if with_skip:
            (u_ref, dy_ref, hb_ref, a_ref, b_ref, bt_ref, ct_ref, d_ref,
             du_ref, da_ref, db_ref, dct_ref, dd_ref, sh_ref, sg_ref, g_ref) = refs
        else:
            (u_ref, dy_ref, hb_ref, a_ref, b_ref, bt_ref, ct_ref,
             du_ref, da_ref, db_ref, dct_ref, sh_ref, sg_ref, g_ref) = refs
        j = pl.program_id(0)

        @pl.when(j == 0)
        def _():
            g_ref[...] = jnp.zeros_like(g_ref)
            da_ref[...] = jnp.zeros_like(da_ref)
            db_ref[...] = jnp.zeros_like(db_ref)
            dct_ref[...] = jnp.zeros_like(dct_ref)
            if with_skip:
                dd_ref[...] = jnp.zeros_like(dd_ref)

        for q in range(N_BLOCKS):
            _scatter_steps(sh_ref, q, _dot(u_ref[:, q * 128:(q + 1) * 128], b_ref[q]))
            _scatter_steps(sg_ref, q, _dot(dy_ref[:, q * 128:(q + 1) * 128], ct_ref[q]))
        a = _tiles(a_ref[0]) + _tiles(a_ref[1])
        time_of = (lambda s: tc - 1 - s) if reverse else (lambda s: s)

        def fwd_step(s, h):
            h = _cmul_add(a, h, _load_step(sh_ref, time_of(s)), conj=False)
            _store_step(sh_ref, time_of(s), h)
            return h

        h0 = _tiles(hb_ref[0])
        lax.fori_loop(0, tc, fwd_step, h0)

        def adj(t, h_prev, carry):
            g, da = carry
            g = _cmul_add(a, g, _load_step(sg_ref, t), conj=True)
            _store_step(sg_ref, t, g)
            da_re = [da[l] + g[l] * h_prev[l] + g[RE_TILES + l] * h_prev[RE_TILES + l] for l in range(RE_TILES)]
            da_im = [da[RE_TILES + l] + g[RE_TILES + l] * h_prev[l] - g[l] * h_prev[RE_TILES + l] for l in range(RE_TILES)]
            return g, da_re + da_im

        def bwd_step(i, carry):
            s = tc - 1 - i
            return adj(time_of(s), _load_step(sh_ref, time_of(s - 1)), carry)

        carry = (_tiles(g_ref[...]), _tiles(da_ref[0]) + _tiles(da_ref[1]))
        carry = lax.fori_loop(0, tc - 1, bwd_step, carry)
        g, da = adj(time_of(0), h0, carry)
        g_ref[...] = jnp.concatenate(g, axis=1)
        da_ref[0] = jnp.concatenate(da[:RE_TILES], axis=1)
        da_ref[1] = jnp.concatenate(da[RE_TILES:], axis=1)

        for q in range(N_BLOCKS):
            cols = slice(q * 128, (q + 1) * 128)
            uq, dyq = u_ref[:, cols], dy_ref[:, cols]
            gq = _gather_steps(sg_ref, q, tc)
            duq = _dot(gq, bt_ref[q])
            if with_skip:
                duq = duq + d_ref[:, cols] * dyq
                dd_ref[:, cols] += jnp.sum(dyq * uq, axis=0, keepdims=True)
            du_ref[:, cols] = duq
            db_ref[q] += _dot_t0(uq, gq)
            dct_ref[q] += _dot_t0(dyq, _gather_steps(sh_ref, q, tc))

    def chunk(j):
        return _scan_chunk_of_step(n_chunks - 1 - j, n_chunks, n_ctx_chunks, reverse)

    full2 = lambda j: (0, 0)
    full3 = lambda j: (0, 0, 0)
    row = pl.BlockSpec((tc, width), lambda j: (chunk(j), 0))
    in_specs = [row, row,
                pl.BlockSpec((1, N_BLOCKS, 2 * HALF), lambda j: (chunk(j), 0, 0)),
                pl.BlockSpec((2, N_BLOCKS, HALF), full3),
                pl.BlockSpec((N_BLOCKS, 128, 2 * HALF), full3),
                pl.BlockSpec((N_BLOCKS, 2 * HALF, 128), full3),
                pl.BlockSpec((N_BLOCKS, 128, 2 * HALF), full3)]
    args = [u, dy, hb, abar, b_blk.astype(BF16), jnp.swapaxes(b_blk, 1, 2).astype(BF16),
            jnp.swapaxes(c_blk, 1, 2).astype(BF16)]
    out_specs = [row,
                 pl.BlockSpec((2, N_BLOCKS, HALF), full3),
                 pl.BlockSpec((N_BLOCKS, 128, 2 * HALF), full3),
                 pl.BlockSpec((N_BLOCKS, 128, 2 * HALF), full3)]
    out_shape = [jax.ShapeDtypeStruct((n_rows, width), F32),
                 jax.ShapeDtypeStruct((2, N_BLOCKS, HALF), F32),
                 jax.ShapeDtypeStruct((N_BLOCKS, 128, 2 * HALF), F32),
                 jax.ShapeDtypeStruct((N_BLOCKS, 128, 2 * HALF), F32)]
    if with_skip:
        in_specs.append(pl.BlockSpec((1, width), full2))
        args.append(d_skip.reshape(1, width))
        out_specs.append(pl.BlockSpec((1, width), full2))
        out_shape.append(jax.ShapeDtypeStruct((1, width), F32))
    scratch = [pltpu.VMEM((LANE_TILES, tc * N_BLOCKS, 128), F32), pltpu.VMEM((LANE_TILES, tc * N_BLOCKS, 128), F32),
               pltpu.VMEM((N_BLOCKS, 2 * HALF), F32)]
    if rider is not None:
        in_specs.append(HBM_SPEC)
        args.append(rider[0])
        out_specs.append(HBM_SPEC)
        out_shape.append(_exchange_out_shape(*rider))
        scratch += _exchange_semaphores()
    return pl.pallas_call(
        body, name="s5_scan_bwd_rev" if reverse else "s5_scan_bwd",
        grid=(n_chunks,), in_specs=in_specs, out_specs=out_specs, out_shape=out_shape, scratch_shapes=scratch,
        compiler_params=_cparams(1),
    )(*args)


def _s5_dir_params(d, a_re, a_im, log_dt, b_re, b_im):
    return a_re[d], a_im[d], log_dt[d], b_re[d], b_im[d]


def _s5_forward(u, ssm, n_ctx, riders=(None, None)):
    a_re, a_im, log_dt, b_re, b_im, c_re, c_im, d_skip = ssm
    outs, saved, carried = [], [], [None, None]
    for d in range(2):
        prep = _s5_prep(*_s5_dir_params(d, a_re, a_im, log_dt, b_re, b_im))
        abar, b_blk, c_blk = _s5_blocks(*prep, c_re[d], c_im[d])
        res = _s5_scan_fwd(u, abar, b_blk, c_blk, d_skip if d == 0 else None, n_ctx, reverse=(d == 1), rider=riders[d])
        if riders[d] is not None:
            carried[d] = res[2]
        outs.append(res[0])
        saved.append((res[1], abar, b_blk, c_blk))
    return outs, saved, carried


def _s5_backward(u, dy, ssm, saved, n_ctx, riders=(None, None)):
    a_re, a_im, log_dt, b_re, b_im, c_re, c_im, d_skip = ssm
    dus, grads = [], [[] for _ in range(7)]
    d_d, carried = None, [None, None]
    for d in range(2):
        hb, abar, b_blk, c_blk = saved[d]
        res = _s5_scan_bwd(u, dy, hb, abar, b_blk, c_blk, d_skip if d == 0 else None, n_ctx, reverse=(d == 1),
                           rider=riders[d])
        if riders[d] is not None:
            carried[d], res = res[-1], res[:-1]
        if d == 0:
            du, d_abar, d_b_blk, d_ct_blk, d_d = res
        else:
            du, d_abar, d_b_blk, d_ct_blk = res
        dus.append(du)
        dbb_re, dbb_im, dc_re, dc_im = _s5_unblock(d_b_blk, d_ct_blk)
        _, vjp = jax.vjp(_s5_prep, *_s5_dir_params(d, a_re, a_im, log_dt, b_re, b_im))
        shape = (SSM_GROUPS, SSM_STATE)
        g5 = vjp((d_abar[0].reshape(shape), d_abar[1].reshape(shape), dbb_re, dbb_im))
        for k, g in enumerate(tuple(g5) + (dc_re, dc_im)):
            grads[k].append(g)
    grads = [jnp.stack(g) for g in grads]
    return dus, grads + [d_d.reshape(-1)], carried


INV_SQRT2 = 0.7071067811865476
INV_SQRT_2PI = 0.3989422804014327


def _rows(cols):
    return pl.BlockSpec((ROW_TILE, cols), lambda i: (i, 0))


def _rows_skip_ctx(cols):
    return pl.BlockSpec((ROW_TILE, cols), lambda i: (i + 1, 0))


def _rows_lat(cols):
    return pl.BlockSpec((ROW_TILE, cols), lambda i: (jnp.maximum(i - 1, 0), 0))


def _full(shape):
    nd = len(shape)
    return pl.BlockSpec(shape, lambda i: (0,) * nd)


def _seg(cols):
    return pl.BlockSpec((1, 1, cols), lambda i: (jnp.minimum(i, 1), 0, 0))


def _lat_seg(cols):
    return pl.BlockSpec((1, 1, cols), lambda i: (1, 0, 0))


def _sds(shape, dtype=F32):
    return jax.ShapeDtypeStruct(shape, dtype)


def _sum0(x):
    return jnp.sum(x, axis=0, keepdims=True)


def _sigmoid(x):
    return jax.nn.sigmoid(x)


def _rms_mod(x, g, scale, shift):
    r = lax.rsqrt(jnp.mean(x * x, axis=-1, keepdims=True) + NORM_EPS)
    return (x * r * g) * (1.0 + scale) + shift


def _rms_mod_bwd(x, g, scale, dh):
    r = lax.rsqrt(jnp.mean(x * x, axis=-1, keepdims=True) + NORM_EPS)
    n = x * r
    dyg = dh * (1.0 + scale)
    dn = dyg * g
    dx = r * (dn - n * jnp.mean(dn * n, axis=-1, keepdims=True))
    return dx, _sum0(dyg * n), _sum0(dh * (n * g)), _sum0(dh)


def _head_of_lane(width):
    return (jnp.arange(width)[:, None] // HEAD_DIM == jnp.arange(128)[None, :]).astype(BF16)


def _split_dot(t, w, transposed):
    hi = t.astype(BF16)
    lo = (t - hi.astype(F32)).astype(BF16)
    f = _dot_t1 if transposed else _dot
    return f(hi, w) + f(lo, w)


def _head_sums(t, hl):
    return _split_dot(_split_dot(t, hl, False), hl, True)


def _rope_partner(x):
    n = x.shape[1]
    lane = lax.broadcasted_iota(jnp.int32, x.shape, 1)
    return jnp.where((lane & 16) == 0, pltpu.roll(x, n - 16, 1), pltpu.roll(x, 16, 1))


def _lanes(tab, width):
    return jnp.tile(tab, (1, width // tab.shape[1]))


def _head_norm_rope(x, gain, cos, sin, hl):
    r = lax.rsqrt(_head_sums(x * x, hl) * (1.0 / HEAD_DIM) + NORM_EPS)
    y = x * r * gain
    return y * cos + _rope_partner(y) * sin


def _head_norm_rope_bwd(x, gain, cos, sin, hl, dout):
    dy = dout * cos + _rope_partner(dout * sin)
    r = lax.rsqrt(_head_sums(x * x, hl) * (1.0 / HEAD_DIM) + NORM_EPS)
    n = x * r
    dn = dy * gain
    dx = r * (dn - n * (_head_sums(dn * n, hl) * (1.0 / HEAD_DIM)))
    return dx, _sum0(dy * n)


def _rope_tables(n_ctx, n_lat):
    t = jnp.arange(n_lat)
    pos = jnp.stack([(t // GRID_W).astype(F32), (t % GRID_W).astype(F32)], axis=1)
    n_freq = HEAD_DIM // 4
    freqs = ROPE_THETA ** (-jnp.arange(n_freq, dtype=F32) / n_freq)
    ang = pos[:, :, None] * freqs[None, None, :]
    cos = jnp.repeat(jnp.cos(ang)[:, :, None, :], 2, axis=2).reshape(n_lat, HEAD_DIM)
    sin = jnp.sin(ang)
    sin = jnp.stack([-sin, sin], axis=2).reshape(n_lat, HEAD_DIM)
    cos = jnp.concatenate([jnp.ones((n_ctx, HEAD_DIM), F32), cos], axis=0)
    sin = jnp.concatenate([jnp.zeros((n_ctx, HEAD_DIM), F32), sin], axis=0)
    return jnp.tile(cos, (1, 2)), jnp.tile(sin, (1, 2))


def _ssm_in(xa, g, scale, shift, w_in):
    n_rows, d = xa.shape
    e = w_in.shape[1] // 2

    def body(x_ref, g_ref, sc_ref, sh_ref, w_ref, u_ref, z_ref):
        h = _rms_mod(x_ref[...], g_ref[...], sc_ref[0], sh_ref[0])
        proj = _dot(h, w_ref[...])
        u_ref[...] = proj[:, :e]
        z_ref[...] = proj[:, e:]

    return pl.pallas_call(
        body, name="ssm_in", grid=(n_rows // ROW_TILE,),
        in_specs=[_rows(d), _full((1, d)), _seg(d), _seg(d), _full(w_in.shape)],
        out_specs=[_rows(e), _rows(e)], out_shape=[_sds((n_rows, e)), _sds((n_rows, e))],
        compiler_params=_cparams(1),
    )(xa, g, scale, shift, w_in)


def _s5_post_math(y, z, w_glu, b_glu, w_out):
    er = lax.erf(y * INV_SQRT2)
    g = 0.5 * y * (1.0 + er)
    sg = _sigmoid(_dot(g, w_glu) + b_glu)
    g2 = g * sg
    sz = _sigmoid(z)
    silu_z = z * sz
    m = g2 * silu_z
    return er, g, sg, g2, sz, silu_z, m, _dot(m, w_out)


def _ssm_post(xa, y0, y1, z, gate, w_glu, b_glu, w_out):
    n_rows, d = xa.shape
    e = z.shape[1]

    def body(x_ref, y0_ref, y1_ref, z_ref, gt_ref, wg_ref, bg_ref, wo_ref, o_ref):
        out = _s5_post_math(y0_ref[...] + y1_ref[...], z_ref[...], wg_ref[...], bg_ref[...], wo_ref[...])[-1]
        o_ref[...] = x_ref[...] + gt_ref[0] * out

    return pl.pallas_call(
        body, name="ssm_post", grid=(n_rows // ROW_TILE,),
        in_specs=[_rows(d), _rows(e), _rows(e), _rows(e), _seg(d), _full(w_glu.shape), _full((1, e)), _full(w_out.shape)],
        out_specs=_rows(d), out_shape=_sds((n_rows, d)),
        compiler_params=_cparams(1),
    )(xa, y0, y1, z, gate, w_glu, b_glu, w_out)


def _init_acc(first, *refs):
    @pl.when(first)
    def _():
        for r in refs:
            r[...] = jnp.zeros_like(r)


def _ssm_post_bwd(dxa, y0, y1, z, gate, w_glu, b_glu, w_out, w_glu_t, w_out_t):
    n_rows, d = dxa.shape
    e = z.shape[1]

    def body(dx_ref, y0_ref, y1_ref, z_ref, gt_ref, wg_ref, bg_ref, wo_ref, wgt_ref, wot_ref,
             dy_ref, dz_ref, dgt_ref, dwo_ref, dwg_ref, dbg_ref):
        i = pl.program_id(0)
        _init_acc(i == 0, dwo_ref, dwg_ref, dbg_ref)
        _init_acc(i <= 1, dgt_ref)
        y, zz = y0_ref[...] + y1_ref[...], z_ref[...]
        er, g, sg, g2, sz, silu_z, m, out = _s5_post_math(y, zz, wg_ref[...], bg_ref[...], wo_ref[...])
        dxa_t = dx_ref[...]
        dgt_ref[0] += _sum0(dxa_t * out)
        dout = gt_ref[0] * dxa_t
        dm = _dot(dout, wot_ref[...])
        dwo_ref[...] += _dot_t0(m, dout)
        dg2 = dm * silu_z
        dz_ref[...] = dm * g2 * (sz * (1.0 + zz * (1.0 - sz)))
        dt = dg2 * g * sg * (1.0 - sg)
        dwg_ref[...] += _dot_t0(g, dt)
        dbg_ref[...] += _sum0(dt)
        dg = dg2 * sg + _dot(dt, wgt_ref[...])
        dy_ref[...] = dg * (0.5 * (1.0 + er) + y * jnp.exp(-0.5 * y * y) * INV_SQRT_2PI)

    return pl.pallas_call(
        body, name="ssm_post_bwd", grid=(n_rows // ROW_TILE,),
        in_specs=[_rows(d), _rows(e), _rows(e), _rows(e), _seg(d), _full(w_glu.shape), _full((1, e)), _full(w_out.shape),
                  _full(w_glu_t.shape), _full(w_out_t.shape)],
        out_specs=[_rows(e), _rows(e), _seg(d), _full(w_out.shape), _full(w_glu.shape), _full((1, e))],
        out_shape=[_sds((n_rows, e)), _sds((n_rows, e)), _sds((2, 1, d)), _sds(w_out.shape), _sds(w_glu.shape), _sds((1, e))],
        compiler_params=_cparams(1),
    )(dxa, y0, y1, z, gate, w_glu, b_glu, w_out, w_glu_t, w_out_t)


def _ssm_in_bwd(du0, du1, dz, xa, dxa_next, g, scale, shift, w_in_t):
    n_rows, d = xa.shape
    e = dz.shape[1]
    n_lat = n_rows - ROW_TILE

    def body(du0_ref, du1_ref, dz_ref, x_ref, dn_ref, g_ref, sc_ref, sh_ref, wt_ref,
             gx_ref, dw_ref, dg_ref, dsc_ref, dsh_ref):
        i = pl.program_id(0)
        _init_acc(i == 0, dw_ref, dg_ref)
        _init_acc(i <= 1, dsc_ref, dsh_ref)
        x = x_ref[...]
        h = _rms_mod(x, g_ref[...], sc_ref[0], sh_ref[0])
        dproj = jnp.concatenate([du0_ref[...] + du1_ref[...], dz_ref[...]], axis=1)
        dh = _dot(dproj, wt_ref[...])
        dw_ref[...] += _dot_t0(h, dproj)
        dx, dg, dsc, dsh = _rms_mod_bwd(x, g_ref[...], sc_ref[0], dh)
        dg_ref[...] += dg
        dsc_ref[0] += dsc
        dsh_ref[0] += dsh
        gx_ref[...] = dn_ref[...] + dx

    return pl.pallas_call(
        body, name="ssm_in_bwd", grid=(n_rows // ROW_TILE,),
        in_specs=[_rows(e), _rows(e), _rows(e), _rows(d), _rows(d), _full((1, d)), _seg(d), _seg(d), _full(w_in_t.shape)],
        out_specs=[_rows_lat(d), _full((d, 2 * e)), _full((1, d)), _seg(d), _seg(d)],
        out_shape=[_sds((n_lat, d)), _sds((d, 2 * e)), _sds((1, d)), _sds((2, 1, d)), _sds((2, 1, d))],
        compiler_params=_cparams(1),
    )(du0, du1, dz, xa, dxa_next, g, scale, shift, w_in_t)


Q_WIDTH = N_Q_HEADS * HEAD_DIM
SM_SCALE = 1.0 / math.sqrt(HEAD_DIM)


def _attn_in(xa, g, scale, shift, w_in, q_gain, k_gain, cos, sin):
    n_rows, d = xa.shape
    qk = Q_WIDTH + KV_WIDTH

    def body(x_ref, g_ref, sc_ref, sh_ref, w_ref, qg_ref, kg_ref, cos_ref, sin_ref, hq_ref, hk_ref,
             q_ref, k_ref, v_ref, z_ref, raw_ref):
        h = _rms_mod(x_ref[...], g_ref[...], sc_ref[0], sh_ref[0])
        proj = _dot(h, w_ref[...])
        q_raw, k_raw = proj[:, :Q_WIDTH], proj[:, Q_WIDTH:qk]
        cos, sin = cos_ref[...], sin_ref[...]
        q = _head_norm_rope(q_raw, qg_ref[...], _lanes(cos, Q_WIDTH), _lanes(sin, Q_WIDTH), hq_ref[...])
        k = _head_norm_rope(k_raw, kg_ref[...], _lanes(cos, KV_WIDTH), _lanes(sin, KV_WIDTH), hk_ref[...])
        q_ref[...] = (q * SM_SCALE).astype(BF16)
        k_ref[...] = k.astype(BF16)
        v_ref[...] = proj[:, qk:qk + KV_WIDTH].astype(BF16)
        z_ref[...] = proj[:, qk + KV_WIDTH:]
        raw_ref[...] = proj[:, :qk]

    return pl.pallas_call(
        body, name="attn_in", grid=(n_rows // ROW_TILE,),
        in_specs=[_rows(d), _full((1, d)), _seg(d), _seg(d), _full(w_in.shape), _full((1, Q_WIDTH)), _full((1, KV_WIDTH)),
                  _rows(128), _rows(128), _full((Q_WIDTH, 128)), _full((KV_WIDTH, 128))],
        out_specs=[_rows_lat(Q_WIDTH), _rows(KV_WIDTH), _rows(KV_WIDTH), _rows(Q_WIDTH), _rows(qk)],
        out_shape=[_sds((n_rows - ROW_TILE, Q_WIDTH), BF16), _sds((n_rows, KV_WIDTH), BF16), _sds((n_rows, KV_WIDTH), BF16),
                   _sds((n_rows, Q_WIDTH)), _sds((n_rows, qk))],
        compiler_params=_cparams(1),
    )(xa, g, scale, shift, w_in, q_gain, k_gain, cos, sin, _head_of_lane(Q_WIDTH), _head_of_lane(KV_WIDTH))


GROUP_WIDTH = KV_REP * HEAD_DIM


def _stack_heads(ref):
    return jnp.concatenate([ref[:, h * HEAD_DIM:(h + 1) * HEAD_DIM] for h in range(KV_REP)], axis=0)


def _unstack_heads(a_t, tq):
    return jnp.concatenate([a_t[:, h * tq:(h + 1) * tq].T for h in range(KV_REP)], axis=1)


def _kv_tile(n_keys):
    return 768 if n_keys % 768 == 0 else 256


def _q_tile(n_lat):
    return 512 if n_lat % 512 == 0 else 256


def _flash_fwd(q, k, v_t):
    n_lat = q.shape[0]
    tq = _q_tile(n_lat)
    rows = KV_REP * tq
    n_kv, tk, n_q = k.shape[1], k.shape[2], n_lat // tq

    v_rows = v_t.shape[2]

    def body(q_ref, k_ref, vt_ref, o_ref, lse_ref):
        q = _stack_heads(q_ref)

        def step(j, carry):
            m_prev, acc = carry
            s_t = _dot_t1(k_ref[0, j], q)
            m_new = jnp.maximum(m_prev, jnp.max(s_t, axis=0, keepdims=True))
            alpha = jnp.exp(m_prev - m_new)
            p_t = jnp.exp(s_t - m_new)
            return m_new, alpha * acc + _dot(vt_ref[0, j], p_t)

        init = (jnp.full((1, rows), -jnp.inf, F32), jnp.zeros((v_rows, rows), F32))
        m, acc = lax.fori_loop(0, n_kv, step, init)
        l = acc[HEAD_DIM:HEAD_DIM + 1]
        o_ref[...] = _unstack_heads(acc[:HEAD_DIM] / l, tq)
        lse_ref[0, 0] = m + jnp.log(l)

    kv_all = lambda a: pl.BlockSpec((1,) + a.shape[1:], lambda g, i: (g, 0, 0, 0))
    return pl.pallas_call(
        body, name="flash_fwd", grid=(N_KV_HEADS, n_q),
        in_specs=[pl.BlockSpec((tq, GROUP_WIDTH), lambda g, i: (i, g)), kv_all(k), kv_all(v_t)],
        out_specs=[pl.BlockSpec((tq, GROUP_WIDTH), lambda g, i: (i, g)),
                   pl.BlockSpec((1, 1, 1, rows), lambda g, i: (g, i, 0, 0))],
        out_shape=[_sds((n_lat, Q_WIDTH)), _sds((N_KV_HEADS, n_q, 1, rows))],
        compiler_params=_cparams(2),
    )(q, k, v_t)


def _flash_bwd(q, k, k_t, v, do, lse_t, delta_t):
    n_lat = q.shape[0]
    tq = _q_tile(n_lat)
    rows = KV_REP * tq
    n_kv, tk, n_q = k.shape[1], k.shape[2], n_lat // tq

    def body(q_ref, k_ref, kt_ref, v_ref, do_ref, lse_ref, dl_ref, dq_ref, dk_ref, dv_ref):
        _init_acc(pl.program_id(1) == 0, dk_ref, dv_ref)
        q, do = _stack_heads(q_ref), _stack_heads(do_ref)
        lse, delta = lse_ref[0, 0], dl_ref[0, 0]

        def step(j, dq_acc):
            p_t = jnp.exp(_dot_t1(k_ref[0, j], q) - lse)
            dv_ref[0, j] += _dot(p_t, do)
            ds_t = p_t * (_dot_t1(v_ref[0, j], do) - delta)
            dk_ref[0, j] += _dot(ds_t, q)
            return dq_acc + _dot(kt_ref[0, j], ds_t)

        dq = lax.fori_loop(0, n_kv, step, jnp.zeros((HEAD_DIM, rows), F32))
        dq_ref[...] = _unstack_heads(dq, tq)

    qspec = pl.BlockSpec((tq, GROUP_WIDTH), lambda g, i: (i, g))
    rowspec = pl.BlockSpec((1, 1, 1, rows), lambda g, i: (g, i, 0, 0))
    kv_all = lambda a: pl.BlockSpec((1,) + a.shape[1:], lambda g, i: (g, 0, 0, 0))
    return pl.pallas_call(
        body, name="flash_bwd", grid=(N_KV_HEADS, n_q),
        in_specs=[qspec, kv_all(k), kv_all(k_t), kv_all(v), qspec, rowspec, rowspec],
        out_specs=[qspec, kv_all(k), kv_all(k)],
        out_shape=[_sds((n_lat, Q_WIDTH)), _sds(k.shape), _sds(k.shape)],
        compiler_params=_cparams(2),
    )(q, k, k_t, v, do, lse_t, delta_t)


def _to_lane_stacked(a, tq):
    n_lat = a.shape[0]
    a = a.reshape(n_lat // tq, tq, N_KV_HEADS, KV_REP).transpose(2, 0, 3, 1)
    return a.reshape(N_KV_HEADS, n_lat // tq, 1, KV_REP * tq)


def _attn_post_loss(o, z, xa, gate, w_out, w_out_t, final_g, target):
    n_lat, d = target.shape
    e = o.shape[1]
    head_of_lane = (jnp.arange(e)[:, None] // HEAD_DIM == jnp.arange(128)[None, :]).astype(BF16)

    def body(o_ref, z_ref, x_ref, gt_ref, w_ref, wt_ref, fg_ref, tg_ref, hl_ref,
             do_ref, dl_ref, dz_ref, dx_ref, loss_ref, dfg_ref, dgt_ref, dw_ref):
        _init_acc(pl.program_id(0) == 0, loss_ref, dfg_ref, dgt_ref, dw_ref)
        oo, zz, gate_t, fg = o_ref[...], z_ref[...], gt_ref[0], fg_ref[...]
        sz = _sigmoid(zz)
        silu_z = zz * sz
        m = oo * silu_z
        out = _dot(m, w_ref[...])
        x2 = x_ref[...] + gate_t * out
        r = lax.rsqrt(jnp.mean(x2 * x2, axis=-1, keepdims=True) + NORM_EPS)
        n = x2 * r
        err = n * fg - tg_ref[...]
        loss_ref[...] += 0.5 * jnp.sum(jnp.mean(err * err, axis=-1, keepdims=True), axis=0, keepdims=True)
        dy = err * (1.0 / d)
        dfg_ref[...] += _sum0(dy * n)
        dn = dy * fg
        dx2 = r * (dn - n * jnp.mean(dn * n, axis=-1, keepdims=True))
        dx_ref[...] = dx2
        dgt_ref[...] += _sum0(dx2 * out)
        dout = gate_t * dx2
        dw_ref[...] += _dot_t0(m, dout)
        dm = _dot(dout, wt_ref[...])
        do = dm * silu_z
        do_ref[...] = do.astype(BF16)
        prod = do * oo
        hi = prod.astype(BF16)
        lo = (prod - hi.astype(F32)).astype(BF16)
        dl_ref[...] = _dot(hi, hl_ref[...]) + _dot(lo, hl_ref[...])
        dz_ref[...] = dm * oo * (sz * (1.0 + zz * (1.0 - sz)))

    return pl.pallas_call(
        body, name="attn_post_loss", grid=(n_lat // ROW_TILE,),
        in_specs=[_rows(e), _rows_skip_ctx(e), _rows_skip_ctx(d), _lat_seg(d), _full(w_out.shape), _full(w_out_t.shape),
                  _full((1, d)), _rows(d), _full((e, 128))],
        out_specs=[_rows(e), _rows(128), _rows(e), _rows(d), _full((1, 1)), _full((1, d)), _full((1, d)), _full(w_out.shape)],
        out_shape=[_sds((n_lat, e), BF16), _sds((n_lat, 128)), _sds((n_lat, e)), _sds((n_lat, d)), _sds((1, 1)), _sds((1, d)),
                   _sds((1, d)), _sds(w_out.shape)],
        compiler_params=_cparams(1),
    )(o, z, xa, gate, w_out, w_out_t, final_g, target, head_of_lane)


def _attn_in_bwd(dq, dk, dv, dz, raw, xa, dx2, g, scale, shift, q_gain, k_gain, cos, sin, w_in_t):
    n_rows, d = xa.shape
    qk = Q_WIDTH + KV_WIDTH
    n_in = w_in_t.shape[0]

    def body(dq_ref, dk_ref, dv_ref, dz_ref, raw_ref, x_ref, dx2_ref, g_ref, sc_ref, sh_ref, qg_ref, kg_ref, cos_ref, sin_ref,
             wt_ref, hq_ref, hk_ref, dxa_ref, dw_ref, dqg_ref, dkg_ref, dg_ref, dsc_ref, dsh_ref):
        i = pl.program_id(0)
        _init_acc(i == 0, dw_ref, dqg_ref, dkg_ref, dg_ref)
        _init_acc(i <= 1, dsc_ref, dsh_ref)
        is_lat = (i > 0).astype(F32)
        x = x_ref[...]
        h = _rms_mod(x, g_ref[...], sc_ref[0], sh_ref[0])
        cos, sin = cos_ref[...], sin_ref[...]
        raw_t = raw_ref[...]
        dq_raw, dqg = _head_norm_rope_bwd(raw_t[:, :Q_WIDTH], qg_ref[...], _lanes(cos, Q_WIDTH), _lanes(sin, Q_WIDTH),
                                          hq_ref[...], dq_ref[...] * (SM_SCALE * is_lat))
        dk_raw, dkg = _head_norm_rope_bwd(raw_t[:, Q_WIDTH:], kg_ref[...], _lanes(cos, KV_WIDTH), _lanes(sin, KV_WIDTH),
                                          hk_ref[...], dk_ref[...])
        dqg_ref[...] += dqg
        dkg_ref[...] += dkg
        dproj = jnp.concatenate([dq_raw, dk_raw, dv_ref[...], dz_ref[...] * is_lat], axis=1)
        dh = _dot(dproj, wt_ref[...])
        dw_ref[...] += _dot_t0(h, dproj)
        dx, dg, dsc, dsh = _rms_mod_bwd(x, g_ref[...], sc_ref[0], dh)
        dg_ref[...] += dg
        dsc_ref[0] += dsc
        dsh_ref[0] += dsh
        dxa_ref[...] = dx + dx2_ref[...] * is_lat

    return pl.pallas_call(
        body, name="attn_in_bwd", grid=(n_rows // ROW_TILE,),
        in_specs=[_rows_lat(Q_WIDTH), _rows(KV_WIDTH), _rows(KV_WIDTH), _rows_lat(Q_WIDTH), _rows(qk), _rows(d), _rows_lat(d),
                  _full((1, d)), _seg(d), _seg(d), _full((1, Q_WIDTH)), _full((1, KV_WIDTH)), _rows(128), _rows(128),
                  _full(w_in_t.shape), _full((Q_WIDTH, 128)), _full((KV_WIDTH, 128))],
        out_specs=[_rows(d), _full((d, n_in)), _full((1, Q_WIDTH)), _full((1, KV_WIDTH)), _full((1, d)), _seg(d), _seg(d)],
        out_shape=[_sds((n_rows, d)), _sds((d, n_in)), _sds((1, Q_WIDTH)), _sds((1, KV_WIDTH)), _sds((1, d)),
                   _sds((2, 1, d)), _sds((2, 1, d))],
        compiler_params=_cparams(1),
    )(dq, dk, dv, dz, raw, xa, dx2, g, scale, shift, q_gain, k_gain, cos, sin, w_in_t,
      _head_of_lane(Q_WIDTH), _head_of_lane(KV_WIDTH))


def _heads_major(a, n_heads):
    return a.reshape(a.shape[0], n_heads, HEAD_DIM).transpose(1, 0, 2)


def _tokens_major(a):
    return a.transpose(1, 0, 2).reshape(a.shape[1], a.shape[0] * HEAD_DIM)


def _local_step(x, ctx, target, mods, norm_g, ssm, w_ssm_in, w_glu, b_glu, w_ssm_out, w_attn_in, q_norm, k_norm, w_attn_out,
                final_g, attn_exchange=None, post_exchange=None):
    n_ctx, d = ctx.shape
    assert n_ctx == ROW_TILE
    n_lat = x.shape[0]
    (shift0, scale0, gate0), (shift1, scale1, gate1) = mods
    g0, g1, fg = norm_g[0:1], norm_g[1:2], final_g.reshape(1, d)
    b_glu = b_glu.reshape(1, -1)
    q_gain = jnp.tile(q_norm.reshape(1, HEAD_DIM), (1, N_Q_HEADS))
    k_gain = jnp.tile(k_norm.reshape(1, HEAD_DIM), (1, N_KV_HEADS))
    cos, sin = _rope_tables(n_ctx, n_lat)

    xa0 = jnp.concatenate([ctx, x], axis=0)
    u, z0 = _ssm_in(xa0, g0, scale0, shift0, w_ssm_in)
    gather = lambda ex: (ex[0], False) if ex else None
    (y0, y1), saved, gathered = _s5_forward(u, ssm, n_ctx, (gather(attn_exchange), gather(post_exchange)))
    if attn_exchange:
        w_attn_in, w_attn_out = attn_exchange[1](gathered[0])
    if post_exchange:
        w_glu, w_ssm_out = post_exchange[1](gathered[1])
    xa1 = _ssm_post(xa0, y0, y1, z0, gate0, w_glu, b_glu, w_ssm_out)

    q, k, v, z1, raw = _attn_in(xa1, g1, scale1, shift1, w_attn_in, q_gain, k_gain, cos, sin)
    tq, tk = _q_tile(n_lat), _kv_tile(n_ctx + n_lat)
    key_blocks = lambda a: _heads_major(a, N_KV_HEADS).reshape(N_KV_HEADS, -1, tk, HEAD_DIM)
    k_b, v_b = key_blocks(k), key_blocks(v)
    v_t_ones = jnp.concatenate([v_b.transpose(0, 1, 3, 2), jnp.ones((N_KV_HEADS, k_b.shape[1], 16, tk), BF16)], axis=2)
    o, lse_t = _flash_fwd(q, k_b, v_t_ones)
    do, delta, dz1, dx2, loss, d_fg, d_gate1, d_w_attn_out = _attn_post_loss(
        o, z1, xa1, gate1, w_attn_out, w_attn_out.T, fg, target)

    dq, dk_b, dv_b = _flash_bwd(q, k_b, k_b.transpose(0, 1, 3, 2), v_b, do, lse_t, _to_lane_stacked(delta[:, :N_Q_HEADS], tq))
    keys_major = lambda a: _tokens_major(a.reshape(N_KV_HEADS, -1, HEAD_DIM))
    dxa1, d_w_attn_in, d_qg, d_kg, d_g1, d_scale1, d_shift1 = _attn_in_bwd(
        dq, keys_major(dk_b), keys_major(dv_b), dz1, raw, xa1, dx2, g1, scale1, shift1,
        q_gain, k_gain, cos, sin, w_attn_in.T)
    dy, dz0, d_gate0, d_w_ssm_out, d_w_glu, d_b_glu = _ssm_post_bwd(
        dxa1, y0, y1, z0, gate0, w_glu, b_glu, w_ssm_out, w_glu.T, w_ssm_out.T)
    scatter = lambda ex, *g: (ex[2](*g), True) if ex else None
    (du0, du1), d_ssm, parts = _s5_backward(
        u, dy, ssm, saved, n_ctx,
        (scatter(attn_exchange, d_w_attn_in, d_w_attn_out), scatter(post_exchange, d_w_glu, d_w_ssm_out)))
    grad_x, d_w_ssm_in, d_g0, d_scale0, d_shift0 = _ssm_in_bwd(du0, du1, dz0, xa0, dxa1, g0, scale0, shift0, w_ssm_in.T)

    d_gate1_seg = jnp.concatenate([jnp.zeros((1, 1, d), F32), d_gate1.reshape(1, 1, d)], axis=0)
    grads = dict(
        norm_g=jnp.concatenate([d_g0, d_g1], axis=0), ssm_w_in=d_w_ssm_in, ssm=d_ssm, ssm_b_glu=d_b_glu.reshape(-1),
        attn_q_norm=d_qg.reshape(N_Q_HEADS, HEAD_DIM).sum(0), attn_k_norm=d_kg.reshape(N_KV_HEADS, HEAD_DIM).sum(0),
        final_norm_g=d_fg.reshape(-1))
    if attn_exchange:
        grads.update(attn_parts=parts[0])
    else:
        grads.update(attn_w_in=d_w_attn_in, attn_w_out=d_w_attn_out)
    if post_exchange:
        grads.update(post_parts=parts[1])
    else:
        grads.update(ssm_w_glu=d_w_glu, ssm_w_out=d_w_ssm_out)
    d_mods = ((d_shift0, d_scale0, d_gate0), (d_shift1, d_scale1, d_gate1_seg))
    return loss[0, 0], grad_x, grads, d_mods


def _my_index():
    return 4 * lax.axis_index("x") + 2 * lax.axis_index("y") + lax.axis_index("c")


def _peer(k):
    mx, my, mc = lax.axis_index("x"), lax.axis_index("y"), lax.axis_index("c")
    px = 1 - mx if k & 4 else mx
    py = 1 - my if k & 2 else my
    pc = 1 - mc if k & 1 else mc
    return (px, py, pc), 4 * px + 2 * py + pc


HBM_SPEC = pl.BlockSpec(memory_space=pl.ANY)


def _exchange(x, name, all_to_all):
    def body(x_ref, out_ref, send_sems, recv_sems, local_sem):
        _exchange_copies(all_to_all, x_ref, out_ref, send_sems, recv_sems, local_sem, start=True)
        _exchange_copies(all_to_all, x_ref, out_ref, send_sems, recv_sems, local_sem, start=False)

    return pl.pallas_call(
        body, name=name, in_specs=[HBM_SPEC], out_specs=HBM_SPEC,
        out_shape=_exchange_out_shape(x, all_to_all), scratch_shapes=_exchange_semaphores(),
    )(x)


def _exchange_out_shape(x, all_to_all):
    return _sds((N_DEV,) + tuple(x.shape[1:] if all_to_all else x.shape), x.dtype)


def _exchange_semaphores():
    return [pltpu.SemaphoreType.DMA((N_DEV - 1,)), pltpu.SemaphoreType.DMA((N_DEV - 1,)), pltpu.SemaphoreType.DMA]


def _exchange_copies(all_to_all, x_ref, out_ref, send_sems, recv_sems, local_sem, start):
    me = _my_index()
    mine = pltpu.make_async_copy(x_ref.at[me] if all_to_all else x_ref, out_ref.at[me], local_sem)
    if start:
        mine.start()
    for k in range(1, N_DEV):
        peer, peer_idx = _peer(k)
        send = pltpu.make_async_remote_copy(
            src_ref=x_ref.at[peer_idx] if all_to_all else x_ref, dst_ref=out_ref.at[me],
            send_sem=send_sems.at[k - 1], recv_sem=recv_sems.at[k - 1], device_id=peer, device_id_type=MESH_IDS)
        if start:
            send.start()
        else:
            pltpu.make_async_remote_copy(
                src_ref=x_ref.at[me] if all_to_all else x_ref, dst_ref=out_ref.at[peer_idx],
                send_sem=send_sems.at[k - 1], recv_sem=recv_sems.at[k - 1], device_id=peer,
                device_id_type=MESH_IDS).wait_recv()
            send.wait_send()
    if not start:
        mine.wait()


def _ride(rider, first, last, refs):
    @pl.when(first)
    def _():
        _exchange_copies(rider[1], *refs, start=True)

    @pl.when(last)
    def _():
        _exchange_copies(rider[1], *refs, start=False)


MOD_ROWS = 16
CTX_ROW = N_DEV


def _mod_fwd(cond, w_shard, b_cols):
    n_layers, d, cols = w_shard.shape

    def body(c_ref, w_ref, b_ref, o_ref):
        c = c_ref[...]
        s = c * _sigmoid(c)
        for i in range(n_layers):
            o_ref[i] = _dot(s, w_ref[i]) + b_ref[i]

    return pl.pallas_call(
        body, name="mod_fwd", out_shape=_sds((n_layers, MOD_ROWS, cols)),
        compiler_params=pltpu.CompilerParams(vmem_limit_bytes=VMEM_LIMIT),
    )(cond, w_shard, b_cols.reshape(n_layers, 1, cols))


def _mod_bwd(cond, d_lat_cols, d_ctx_cols, w_shard):
    n_layers, d, cols = w_shard.shape

    def body(c_ref, dl_ref, dc_ref, w_ref, dw_ref, dcc_ref):
        c = c_ref[...]
        sg = _sigmoid(c)
        s = c * sg
        d_s = jnp.zeros((MOD_ROWS, d), F32)
        for i in range(n_layers):
            d_ctx = dc_ref[0, i]
            for j in range(1, N_DEV):
                d_ctx = d_ctx + dc_ref[j, i]
            dm = jnp.concatenate([dl_ref[i], d_ctx, jnp.zeros((MOD_ROWS - N_DEV - 1, cols), F32)], axis=0)
            dw_ref[i] = _dot_t0(s, dm)
            d_s = d_s + _dot_t1(dm, w_ref[i])
        d_c = d_s * (sg * (1.0 + c * (1.0 - sg)))
        dcc_ref[...] = d_c[CTX_ROW:CTX_ROW + 1]

    return pl.pallas_call(
        body, name="mod_bwd", out_shape=[_sds((n_layers, d, cols)), _sds((1, d))],
        compiler_params=pltpu.CompilerParams(vmem_limit_bytes=VMEM_LIMIT),
    )(cond, d_lat_cols, d_ctx_cols, w_shard)


ADAM_TILE = 512


def _adamw(w, g_parts, m, v, name):
    n_parts, n_rows, lanes = g_parts.shape
    c1 = 1.0 - ADAM_B1 ** ADAM_STEP
    c2 = 1.0 - ADAM_B2 ** ADAM_STEP

    def body(w_ref, g_ref, m_ref, v_ref, go_ref, d_ref, mo_ref, vo_ref):
        g = g_ref[0].astype(F32)
        for p in range(1, n_parts):
            g = g + g_ref[p].astype(F32)
        m_new = ADAM_B1 * m_ref[...] + (1.0 - ADAM_B1) * g
        v_new = ADAM_B2 * v_ref[...] + (1.0 - ADAM_B2) * (g * g)
        go_ref[...] = g
        mo_ref[...] = m_new
        vo_ref[...] = v_new
        d_ref[...] = -ADAM_LR * ((m_new / c1) / (jnp.sqrt(v_new / c2) + ADAM_EPS) + ADAM_WD * w_ref[...])

    row = pl.BlockSpec((ADAM_TILE, lanes), lambda i: (i, 0))
    return pl.pallas_call(
        body, name=name, grid=(n_rows // ADAM_TILE,),
        in_specs=[row, pl.BlockSpec((n_parts, ADAM_TILE, lanes), lambda i: (0, i, 0)), row, row],
        out_specs=[row] * 4, out_shape=[_sds((n_rows, lanes))] * 4,
        compiler_params=_cparams(1),
    )(w, g_parts, m, v)


def _sum_parts(parts):
    n_parts, n_rows, lanes = parts.shape

    def body(p_ref, o_ref):
        acc = p_ref[0]
        for p in range(1, n_parts):
            acc = acc + p_ref[p]
        o_ref[...] = acc

    return pl.pallas_call(body, name="sum_parts", out_shape=_sds((n_rows, lanes)))(parts)


def _pack(arrays, row_multiple):
    parts = []
    for a in arrays:
        flat = a.reshape(-1)
        parts.append(jnp.pad(flat, (0, (-flat.shape[0]) % 1024)))
    flat = jnp.concatenate(parts)
    flat = jnp.pad(flat, (0, (-flat.shape[0]) % (row_multiple * 128)))
    return flat.reshape(-1, 128)


def _unpack(packed, shapes):
    flat = packed.reshape(-1)
    out, pos = [], 0
    for s in shapes:
        n = math.prod(s)
        out.append(flat[pos:pos + n].reshape(s))
        pos += n + (-n) % 1024
    return out


WEIGHT_NAMES = ['c_ctx', 'w_mod', 'b_mod', 'norm_g', 'ssm_w_in', 'ssm_a_re', 'ssm_a_im', 'ssm_log_dt', 'ssm_b_re', 'ssm_b_im',
                'ssm_c_re', 'ssm_c_im', 'ssm_d', 'ssm_w_glu', 'ssm_b_glu', 'ssm_w_out', 'attn_w_in', 'attn_q_norm',
                'attn_k_norm', 'attn_w_out', 'final_norm_g']
FIRST_SHARDED = ['ssm_w_in']
POST_SHARDED = ['ssm_w_glu', 'ssm_w_out']
ATTN_SHARDED = ['attn_w_in', 'attn_w_out']
SHARDED = FIRST_SHARDED + POST_SHARDED + ATTN_SHARDED
COLUMN_SHARDED = ('ssm_w_in', 'attn_w_in')
REPLICATED = ['c_ctx', 'b_mod', 'norm_g', 'ssm_a_re', 'ssm_a_im', 'ssm_log_dt', 'ssm_b_re', 'ssm_b_im', 'ssm_c_re', 'ssm_c_im',
              'ssm_d', 'ssm_b_glu', 'attn_q_norm', 'attn_k_norm', 'final_norm_g']
SSM_NAMES = ['ssm_a_re', 'ssm_a_im', 'ssm_log_dt', 'ssm_b_re', 'ssm_b_im', 'ssm_c_re', 'ssm_c_im', 'ssm_d']


def _full_from_shards(gathered, name, shard_shape):
    rows, cols = shard_shape
    w = gathered.reshape(N_DEV, rows, cols)
    if name in COLUMN_SHARDED:
        return w.transpose(1, 0, 2).reshape(rows, N_DEV * cols)
    return w.reshape(N_DEV * rows, cols)


def _shards_from_full(g, name):
    if name in COLUMN_SHARDED:
        rows, cols = g.shape
        g = g.reshape(rows, N_DEV, cols // N_DEV).transpose(1, 0, 2)
    return g.reshape(N_DEV, -1, 128)


def kernel(x, c, ctx, c_ctx, w_mod, b_mod, norm_g, ssm_w_in, ssm_a_re, ssm_a_im, ssm_log_dt, ssm_b_re, ssm_b_im, ssm_c_re, ssm_c_im, ssm_d, ssm_w_glu, ssm_b_glu, ssm_w_out, attn_w_in, attn_q_norm, attn_k_norm, attn_w_out, final_norm_g, loss_target, m_c_ctx, m_w_mod, m_b_mod, m_norm_g, m_ssm_w_in, m_ssm_a_re, m_ssm_a_im, m_ssm_log_dt, m_ssm_b_re, m_ssm_b_im, m_ssm_c_re, m_ssm_c_im, m_ssm_d, m_ssm_w_glu, m_ssm_b_glu, m_ssm_w_out, m_attn_w_in, m_attn_q_norm, m_attn_k_norm, m_attn_w_out, m_final_norm_g, v_c_ctx, v_w_mod, v_b_mod, v_norm_g, v_ssm_w_in, v_ssm_a_re, v_ssm_a_im, v_ssm_log_dt, v_ssm_b_re, v_ssm_b_im, v_ssm_c_re, v_ssm_c_im, v_ssm_d, v_ssm_w_glu, v_ssm_b_glu, v_ssm_w_out, v_attn_w_in, v_attn_q_norm, v_attn_k_norm, v_attn_w_out, v_final_norm_g):
    env = dict(locals())
    weights = {n: env[n] for n in WEIGHT_NAMES}
    mom_m = {n: env["m_" + n] for n in WEIGHT_NAMES}
    mom_v = {n: env["v_" + n] for n in WEIGHT_NAMES}
    d = D_MODEL
    me = _my_index()
    mod_cols = w_mod.shape[-1]

    c_all = _exchange(c.reshape(8, d // 8), "gather_c", False).reshape(N_DEV, d)
    cond = jnp.concatenate([c_all, c_ctx.reshape(1, d), jnp.zeros((MOD_ROWS - N_DEV - 1, d), F32)], axis=0)
    shard_shapes = {n: weights[n].shape[1:] for n in SHARDED}
    pack_shards = lambda names: _pack([weights[n] for n in names], 1).astype(BF16)

    def unpack_full(gathered, names):
        full, pos = [], 0
        for n in names:
            rows = math.prod(shard_shapes[n]) // 128
            full.append(_full_from_shards(gathered[:, pos:pos + rows], n, shard_shapes[n]))
            pos += rows
        return full

    def exchange_of(names):
        return (pack_shards(names), lambda gathered: unpack_full(gathered, names),
                lambda *grads: jnp.concatenate([_shards_from_full(t, n) for t, n in zip(grads, names)], axis=1).astype(BF16))

    (w_ssm_in,) = unpack_full(_exchange(pack_shards(FIRST_SHARDED), "gather_ssm_w_in", False), FIRST_SHARDED)

    b_cols = lax.dynamic_slice(b_mod, (0, me * mod_cols), (2, mod_cols))
    mod_shard = _mod_fwd(cond, w_mod, b_cols)
    mod_all = _exchange(mod_shard.reshape(2 * MOD_ROWS, mod_cols), "gather_mod", False)
    mod_full = mod_all.reshape(N_DEV, 2, MOD_ROWS, mod_cols).transpose(1, 2, 0, 3).reshape(2, MOD_ROWS, 3 * d)
    lat_rows = lax.dynamic_slice(mod_full, (0, me, 0), (2, 1, 3 * d))
    mods = []
    for i in range(2):
        seg = jnp.stack([mod_full[i, CTX_ROW:CTX_ROW + 1], lat_rows[i]])
        mods.append((seg[:, :, :d], seg[:, :, d:2 * d], seg[:, :, 2 * d:]))

    ssm = tuple(weights[n][0] for n in SSM_NAMES)
    loss, grad_x, g, d_mods = _local_step(
        x[0], ctx[0], loss_target[0], mods, norm_g, ssm, w_ssm_in, None, ssm_b_glu[0], None,
        None, attn_q_norm[0], attn_k_norm[0], None, final_norm_g, exchange_of(ATTN_SHARDED), exchange_of(POST_SHARDED))
    loss = lax.psum(loss, ("x", "y", "c"))

    d_rows = jnp.stack([jnp.concatenate(dm, axis=-1) for dm in d_mods])
    d_rows = jnp.concatenate([d_rows.reshape(4, 3 * d), jnp.zeros((4, 3 * d), F32)], axis=0)
    d_all = _exchange(d_rows, "gather_dmod", False)[:, :4].reshape(N_DEV, 2, 2, 3 * d)
    d_all = lax.dynamic_slice(d_all, (0, 0, 0, me * mod_cols), (N_DEV, 2, 2, mod_cols))
    d_w_mod, d_c_ctx = _mod_bwd(cond, d_all[:, :, 1].transpose(1, 0, 2), d_all[:, :, 0:1], w_mod)
    d_b_mod = jnp.stack([jnp.concatenate([t[0] + t[1] for t in dm], axis=-1).reshape(3 * d) for dm in d_mods])

    first_parts = _exchange(_shards_from_full(g['ssm_w_in'], 'ssm_w_in').astype(BF16), "scatter_ssm_w_in_grads", True)
    g_big_parts = jnp.concatenate([first_parts, g['post_parts'], g['attn_parts']], axis=1)
    pack_big = lambda t: _pack([t[n] for n in SHARDED], ADAM_TILE)
    big = _adamw(pack_big(weights), g_big_parts, pack_big(mom_m), pack_big(mom_v), "adamw_sharded")
    big = [_unpack(t, [weights[n].shape for n in SHARDED]) for t in big]

    mod_res = _adamw(_pack([w_mod], ADAM_TILE), _pack([d_w_mod], ADAM_TILE)[None], _pack([m_w_mod], ADAM_TILE),
                     _pack([v_w_mod], ADAM_TILE), "adamw_w_mod")
    mod_res = [t.reshape(w_mod.shape) for t in mod_res]

    small = dict(zip(SSM_NAMES, g['ssm']))
    small.update(c_ctx=d_c_ctx, b_mod=d_b_mod, norm_g=g['norm_g'], ssm_b_glu=g['ssm_b_glu'], attn_q_norm=g['attn_q_norm'],
                 attn_k_norm=g['attn_k_norm'], final_norm_g=g['final_norm_g'])
    pack_small = lambda t: _pack([t[n] for n in REPLICATED], ADAM_TILE)
    g_small = pack_small(small)
    slices = _exchange(g_small.reshape(N_DEV, -1, 128), "scatter_small_grads", True)
    g_small = _exchange(_sum_parts(slices), "gather_small_grads", False).reshape(1, -1, 128)
    rep = _adamw(pack_small(weights), g_small, pack_small(mom_m), pack_small(mom_v), "adamw_replicated")
    rep = [_unpack(t, [weights[n].shape for n in REPLICATED]) for t in rep]

    results = []
    for kind in range(4):
        by_name = dict(zip(SHARDED, big[kind]))
        by_name.update(zip(REPLICATED, rep[kind]))
        by_name['w_mod'] = mod_res[kind]
        results.extend(by_name[n] for n in WEIGHT_NAMES)
    return (loss, grad_x[None], *results)
```

```python
import functools
import math

import jax
import jax.numpy as jnp
from jax import lax
from jax.experimental import pallas as pl
from jax.experimental.pallas import tpu as pltpu

F32 = jnp.float32
BF16 = jnp.bfloat16

N_DEV = 8
D_MODEL = 1024
NORM_EPS = 1e-6
SSM_GROUP = 16
SSM_GROUPS = 64
SSM_STATE = 64
GROUPS_PER_BLOCK = 8
N_BLOCKS = SSM_GROUPS // GROUPS_PER_BLOCK
HALF = GROUPS_PER_BLOCK * SSM_STATE
HEAD_DIM = 64
N_Q_HEADS = 16
N_KV_HEADS = 4
KV_REP = N_Q_HEADS // N_KV_HEADS
KV_WIDTH = N_KV_HEADS * HEAD_DIM
GRID_W = 64
ROPE_THETA = 10000.0
ADAM_LR, ADAM_B1, ADAM_B2, ADAM_EPS, ADAM_WD, ADAM_STEP = 0.001, 0.9, 0.999, 1e-08, 0.01, 10

ROW_TILE = 256
SCAN_CHUNK = 256
VMEM_LIMIT = 56 * 1024 * 1024
MESH_IDS = pl.DeviceIdType.MESH


def _cparams(n_axes):
    return pltpu.CompilerParams(dimension_semantics=("arbitrary",) * n_axes, vmem_limit_bytes=VMEM_LIMIT)


def _dot(a, b):
    return jnp.dot(a.astype(BF16), b.astype(BF16), preferred_element_type=F32)


def _dot_t0(a, b):
    return lax.dot_general(a.astype(BF16), b.astype(BF16), (((0,), (0,)), ((), ())), preferred_element_type=F32)


def _dot_t1(a, b):
    return lax.dot_general(a.astype(BF16), b.astype(BF16), (((1,), (1,)), ((), ())), preferred_element_type=F32)


def _s5_prep(a_re, a_im, log_dt, b_re, b_im):
    dt = jnp.exp(log_dt)[:, None]
    ldr, ldi = a_re * dt, a_im * dt
    mag = jnp.exp(ldr)
    abar_re, abar_im = mag * jnp.cos(ldi), mag * jnp.sin(ldi)
    den = a_re * a_re + a_im * a_im
    num_re, num_im = abar_re - 1.0, abar_im
    coef_re = (num_re * a_re + num_im * a_im) / den
    coef_im = (num_im * a_re - num_re * a_im) / den
    bbar_re = coef_re[..., None] * b_re - coef_im[..., None] * b_im
    bbar_im = coef_re[..., None] * b_im + coef_im[..., None] * b_re
    return abar_re, abar_im, bbar_re, bbar_im


def _s5_blocks(abar_re, abar_im, bbar_re, bbar_im, c_re, c_im):
    eye = jnp.eye(GROUPS_PER_BLOCK, dtype=F32)
    bb = jnp.stack([bbar_re, bbar_im]).reshape(2, N_BLOCKS, GROUPS_PER_BLOCK, SSM_STATE, SSM_GROUP)
    b_blk = jnp.einsum('rqgph,gk->qghrkp', bb, eye).reshape(N_BLOCKS, 128, 2 * HALF)
    cc = jnp.stack([c_re, -c_im]).reshape(2, N_BLOCKS, GROUPS_PER_BLOCK, SSM_GROUP, SSM_STATE)
    c_blk = jnp.einsum('rqghp,gk->qrgpkh', cc, eye).reshape(N_BLOCKS, 2 * HALF, 128)
    abar = jnp.stack([abar_re.reshape(N_BLOCKS, HALF), abar_im.reshape(N_BLOCKS, HALF)])
    return abar, b_blk, c_blk


def _s5_unblock(d_b_blk, d_ct_blk):
    db = d_b_blk.reshape(N_BLOCKS, GROUPS_PER_BLOCK, SSM_GROUP, 2, GROUPS_PER_BLOCK, SSM_STATE)
    db = jnp.einsum('qghrgp->rqgph', db).reshape(2, SSM_GROUPS, SSM_STATE, SSM_GROUP)
    dc = d_ct_blk.reshape(N_BLOCKS, GROUPS_PER_BLOCK, SSM_GROUP, 2, GROUPS_PER_BLOCK, SSM_STATE)
    dc = jnp.einsum('qghrgp->rqghp', dc).reshape(2, SSM_GROUPS, SSM_GROUP, SSM_STATE)
    return db[0], db[1], dc[0], -dc[1]


def _scan_chunk_of_step(j, n_chunks, n_ctx_chunks, reverse):
    if not reverse:
        return j
    return jnp.where(j < n_ctx_chunks, n_ctx_chunks - 1 - j, n_chunks - 1 - j + n_ctx_chunks)


LANE_TILES = 2 * HALF // 128
RE_TILES = HALF // 128


def _tiles(v):
    return [v[:, l * 128:(l + 1) * 128] for l in range(v.shape[1] // 128)]


def _scatter_steps(s_ref, q, x):
    for l in range(LANE_TILES):
        s_ref[l, pl.ds(q, x.shape[0], stride=N_BLOCKS), :] = x[:, l * 128:(l + 1) * 128]


def _gather_steps(s_ref, q, n_steps):
    return jnp.concatenate([s_ref[l, pl.ds(q, n_steps, stride=N_BLOCKS), :] for l in range(LANE_TILES)], axis=1)


def _load_step(s_ref, t):
    row = pl.multiple_of(t * N_BLOCKS, N_BLOCKS)
    return [s_ref[l, pl.ds(row, N_BLOCKS), :] for l in range(LANE_TILES)]


def _store_step(s_ref, t, tiles):
    row = pl.multiple_of(t * N_BLOCKS, N_BLOCKS)
    for l in range(LANE_TILES):
        s_ref[l, pl.ds(row, N_BLOCKS), :] = tiles[l]


def _cmul_add(a, h, x, conj):
    re, im = [], []
    for l in range(RE_TILES):
        ar, ai, hr, hi = a[l], a[RE_TILES + l], h[l], h[RE_TILES + l]
        if conj:
            re.append(ar * hr + ai * hi + x[l])
            im.append(ar * hi - ai * hr + x[RE_TILES + l])
        else:
            re.append(ar * hr - ai * hi + x[l])
            im.append(ar * hi + ai * hr + x[RE_TILES + l])
    return re + im


def _s5_scan_fwd(u, abar, b_blk, c_blk, d_skip, n_ctx, reverse, rider=None):
    n_rows, width = u.shape
    tc = SCAN_CHUNK
    n_chunks, n_ctx_chunks = n_rows // tc, n_ctx // tc
    with_skip = d_skip is not None

    def body(*refs):
        if rider is not None:
            x_ref, ride_out, sems = refs[4 + with_skip], refs[7 + with_skip], refs[-3:]
            _ride(rider, pl.program_id(0) == 0, pl.program_id(0) == n_chunks - 1, (x_ref, ride_out) + tuple(sems))
            refs = refs[:4 + with_skip] + refs[5 + with_skip:7 + with_skip] + refs[8 + with_skip:-3]
        if with_skip:
            u_ref, a_ref, b_ref, c_ref, d_ref, y_ref, hb_ref, s_ref, h_ref = refs
        else:
            u_ref, a_ref, b_ref, c_ref, y_ref, hb_ref, s_ref, h_ref = refs
        j = pl.program_id(0)

        @pl.when(j == 0)
        def _():
            h_ref[...] = jnp.zeros_like(h_ref)

        hb_ref[0] = h_ref[...]
        for q in range(N_BLOCKS):
            _scatter_steps(s_ref, q, _dot(u_ref[:, q * 128:(q + 1) * 128], b_ref[q]))
        a = _tiles(a_ref[0]) + _tiles(a_ref[1])

        def step(s, h):
            t = tc - 1 - s if reverse else s
            h = _cmul_add(a, h, _load_step(s_ref, t), conj=False)
            _store_step(s_ref, t, h)
            return h

        h = lax.fori_loop(0, tc, step, _tiles(h_ref[...]))
        h_ref[...] = jnp.concatenate(h, axis=1)
        for q in range(N_BLOCKS):
            yq = _dot(_gather_steps(s_ref, q, tc), c_ref[q])
            if with_skip:
                yq = yq + d_ref[:, q * 128:(q + 1) * 128] * u_ref[:, q * 128:(q + 1) * 128]
            y_ref[:, q * 128:(q + 1) * 128] = yq

    chunk = functools.partial(_scan_chunk_of_step, n_chunks=n_chunks, n_ctx_chunks=n_ctx_chunks, reverse=reverse)
    full3 = lambda j: (0, 0, 0)
    in_specs = [pl.BlockSpec((tc, width), lambda j: (chunk(j), 0)),
                pl.BlockSpec((2, N_BLOCKS, HALF), full3),
                pl.BlockSpec((N_BLOCKS, 128, 2 * HALF), full3),
                pl.BlockSpec((N_BLOCKS, 2 * HALF, 128), full3)]
    args = [u, abar, b_blk.astype(BF16), c_blk.astype(BF16)]
    if with_skip:
        in_specs.append(pl.BlockSpec((1, width), lambda j: (0, 0)))
        args.append(d_skip.reshape(1, width))
    out_specs = [pl.BlockSpec((tc, width), lambda j: (chunk(j), 0)),
                 pl.BlockSpec((1, N_BLOCKS, 2 * HALF), lambda j: (chunk(j), 0, 0))]
    out_shape = [_sds((n_rows, width)), _sds((n_chunks, N_BLOCKS, 2 * HALF))]
    scratch = [pltpu.VMEM((LANE_TILES, tc * N_BLOCKS, 128), F32), pltpu.VMEM((N_BLOCKS, 2 * HALF), F32)]
    if rider is not None:
        in_specs.append(HBM_SPEC)
        args.append(rider[0])
        out_specs.append(HBM_SPEC)
        out_shape.append(_exchange_out_shape(*rider))
        scratch += _exchange_semaphores()
    return pl.pallas_call(
        body, name="s5_scan_fwd_rev" if reverse else "s5_scan_fwd",
        grid=(n_chunks,), in_specs=in_specs, out_specs=out_specs, out_shape=out_shape, scratch_shapes=scratch,
        compiler_params=_cparams(1),
    )(*args)


def _s5_scan_bwd(u, dy, hb, abar, b_blk, c_blk, d_skip, n_ctx, reverse, rider=None):
    n_rows, width = u.shape
    tc = SCAN_CHUNK
    n_chunks, n_ctx_chunks = n_rows // tc, n_ctx // tc
    with_skip = d_skip is not None
    n_in, n_out = 7 + with_skip, 4 + with_skip

    def body(*refs):
        if rider is not None:
            x_ref, ride_out, sems = refs[n_in], refs[n_in + 1 + n_out], refs[-3:]
            _ride(rider, pl.program_id(0) == 0, pl.program_id(0) == n_chunks - 1, (x_ref, ride_out) + tuple(sems))
            refs = refs[:n_in] + refs[n_in + 1:n_in + 1 + n_out] + refs[n_in + 2 + n_out:-3]
        if with_skip:
            (u_ref, dy_ref, hb_ref, a_ref, b_ref, bt_ref, ct_ref, d_ref,
             du_ref, da_ref, db_ref, dct_ref, dd_ref, sh_ref, sg_ref, g_ref) = refs
        else:
            (u_ref, dy_ref, hb_ref, a_ref, b_ref, bt_ref, ct_ref,
             du_ref, da_ref, db_ref, dct_ref, sh_ref, sg_ref, g_ref) = refs
        j = pl.program_id(0)

        @pl.when(j == 0)
        def _():
            g_ref[...] = jnp.zeros_like(g_ref)
            da_ref[...] = jnp.zeros_like(da_ref)
            db_ref[...] = jnp.zeros_like(db_ref)
            dct_ref[...] = jnp.zeros_like(dct_ref)
            if with_skip:
                dd_ref[...] = jnp.zeros_like(dd_ref)

        for q in range(N_BLOCKS):
            _scatter_steps(sh_ref, q, _dot(u_ref[:, q * 128:(q + 1) * 128], b_ref[q]))
            _scatter_steps(sg_ref, q, _dot(dy_ref[:, q * 128:(q + 1) * 128], ct_ref[q]))
        a = _tiles(a_ref[0]) + _tiles(a_ref[1])
        time_of = (lambda s: tc - 1 - s) if reverse else (lambda s: s)

        def fwd_step(s, h):
            h = _cmul_add(a, h, _load_step(sh_ref, time_of(s)), conj=False)
            _store_step(sh_ref, time_of(s), h)
            return h

        h0 = _tiles(hb_ref[0])
        lax.fori_loop(0, tc, fwd_step, h0)

        def adj(t, h_prev, carry):
            g, da = carry
            g = _cmul_add(a, g, _load_step(sg_ref, t), conj=True)
            _store_step(sg_ref, t, g)
            da_re = [da[l] + g[l] * h_prev[l] + g[RE_TILES + l] * h_prev[RE_TILES + l] for l in range(RE_TILES)]
            da_im = [da[RE_TILES + l] + g[RE_TILES + l] * h_prev[l] - g[l] * h_prev[RE_TILES + l] for l in range(RE_TILES)]
            return g, da_re + da_im

        def bwd_step(i, carry):
            s = tc - 1 - i
            return adj(time_of(s), _load_step(sh_ref, time_of(s - 1)), carry)

        carry = (_tiles(g_ref[...]), _tiles(da_ref[0]) + _tiles(da_ref[1]))
        carry = lax.fori_loop(0, tc - 1, bwd_step, carry)
        g, da = adj(time_of(0), h0, carry)
        g_ref[...] = jnp.concatenate(g, axis=1)
        da_ref[0] = jnp.concatenate(da[:RE_TILES], axis=1)
        da_ref[1] = jnp.concatenate(da[RE_TILES:], axis=1)

        for q in range(N_BLOCKS):
            cols = slice(q * 128, (q + 1) * 128)
            uq, dyq = u_ref[:, cols], dy_ref[:, cols]
            gq = _gather_steps(sg_ref, q, tc)
            duq = _dot(gq, bt_ref[q])
            if with_skip:
                duq = duq + d_ref[:, cols] * dyq
                dd_ref[:, cols] += jnp.sum(dyq * uq, axis=0, keepdims=True)
            du_ref[:, cols] = duq
            db_ref[q] += _dot_t0(uq, gq)
            dct_ref[q] += _dot_t0(dyq, _gather_steps(sh_ref, q, tc))

    def chunk(j):
        return _scan_chunk_of_step(n_chunks - 1 - j, n_chunks, n_ctx_chunks, reverse)

    full2 = lambda j: (0, 0)
    full3 = lambda j: (0, 0, 0)
    row = pl.BlockSpec((tc, width), lambda j: (chunk(j), 0))
    in_specs = [row, row,
                pl.BlockSpec((1, N_BLOCKS, 2 * HALF), lambda j: (chunk(j), 0, 0)),
                pl.BlockSpec((2, N_BLOCKS, HALF), full3),
                pl.BlockSpec((N_BLOCKS, 128, 2 * HALF), full3),
                pl.BlockSpec((N_BLOCKS, 2 * HALF, 128), full3),
                pl.BlockSpec((N_BLOCKS, 128, 2 * HALF), full3)]
    args = [u, dy, hb, abar, b_blk.astype(BF16), jnp.swapaxes(b_blk, 1, 2).astype(BF16),
            jnp.swapaxes(c_blk, 1, 2).astype(BF16)]
    out_specs = [row,
                 pl.BlockSpec((2, N_BLOCKS, HALF), full3),
                 pl.BlockSpec((N_BLOCKS, 128, 2 * HALF), full3),
                 pl.BlockSpec((N_BLOCKS, 128, 2 * HALF), full3)]
    out_shape = [jax.ShapeDtypeStruct((n_rows, width), F32),
                 jax.ShapeDtypeStruct((2, N_BLOCKS, HALF), F32),
                 jax.ShapeDtypeStruct((N_BLOCKS, 128, 2 * HALF), F32),
                 jax.ShapeDtypeStruct((N_BLOCKS, 128, 2 * HALF), F32)]
    if with_skip:
        in_specs.append(pl.BlockSpec((1, width), full2))
        args.append(d_skip.reshape(1, width))
        out_specs.append(pl.BlockSpec((1, width), full2))
        out_shape.append(jax.ShapeDtypeStruct((1, width), F32))
    scratch = [pltpu.VMEM((LANE_TILES, tc * N_BLOCKS, 128), F32), pltpu.VMEM((LANE_TILES, tc * N_BLOCKS, 128), F32),
               pltpu.VMEM((N_BLOCKS, 2 * HALF), F32)]
    if rider is not None:
        in_specs.append(HBM_SPEC)
        args.append(rider[0])
        out_specs.append(HBM_SPEC)
        out_shape.append(_exchange_out_shape(*rider))
        scratch += _exchange_semaphores()
    return pl.pallas_call(
        body, name="s5_scan_bwd_rev" if reverse else "s5_scan_bwd",
        grid=(n_chunks,), in_specs=in_specs, out_specs=out_specs, out_shape=out_shape, scratch_shapes=scratch,
        compiler_params=_cparams(1),
    )(*args)


def _s5_dir_params(d, a_re, a_im, log_dt, b_re, b_im):
    return a_re[d], a_im[d], log_dt[d], b_re[d], b_im[d]


def _s5_forward(u, ssm, n_ctx, riders=(None, None)):
    a_re, a_im, log_dt, b_re, b_im, c_re, c_im, d_skip = ssm
    outs, saved, carried = [], [], [None, None]
    for d in range(2):
        prep = _s5_prep(*_s5_dir_params(d, a_re, a_im, log_dt, b_re, b_im))
        abar, b_blk, c_blk = _s5_blocks(*prep, c_re[d], c_im[d])
        res = _s5_scan_fwd(u, abar, b_blk, c_blk, d_skip if d == 0 else None, n_ctx, reverse=(d == 1), rider=riders[d])
        if riders[d] is not None:
            carried[d] = res[2]
        outs.append(res[0])
        saved.append((res[1], abar, b_blk, c_blk))
    return outs, saved, carried


def _s5_backward(u, dy, ssm, saved, n_ctx, riders=(None, None)):
    a_re, a_im, log_dt, b_re, b_im, c_re, c_im, d_skip = ssm
    dus, grads = [], [[] for _ in range(7)]
    d_d, carried = None, [None, None]
    for d in range(2):
        hb, abar, b_blk, c_blk = saved[d]
        res = _s5_scan_bwd(u, dy, hb, abar, b_blk, c_blk, d_skip if d == 0 else None, n_ctx, reverse=(d == 1),
                           rider=riders[d])
        if riders[d] is not None:
            carried[d], res = res[-1], res[:-1]
        if d == 0:
            du, d_abar, d_b_blk, d_ct_blk, d_d = res
        else:
            du, d_abar, d_b_blk, d_ct_blk = res
        dus.append(du)
        dbb_re, dbb_im, dc_re, dc_im = _s5_unblock(d_b_blk, d_ct_blk)
        _, vjp = jax.vjp(_s5_prep, *_s5_dir_params(d, a_re, a_im, log_dt, b_re, b_im))
        shape = (SSM_GROUPS, SSM_STATE)
        g5 = vjp((d_abar[0].reshape(shape), d_abar[1].reshape(shape), dbb_re, dbb_im))
        for k, g in enumerate(tuple(g5) + (dc_re, dc_im)):
            grads[k].append(g)
    grads = [jnp.stack(g) for g in grads]
    return dus, grads + [d_d.reshape(-1)], carried


INV_SQRT2 = 0.7071067811865476
INV_SQRT_2PI = 0.3989422804014327


def _rows(cols):
    return pl.BlockSpec((ROW_TILE, cols), lambda i: (i, 0))


def _rows_skip_ctx(cols):
    return pl.BlockSpec((ROW_TILE, cols), lambda i: (i + 1, 0))


def _rows_lat(cols):
    return pl.BlockSpec((ROW_TILE, cols), lambda i: (jnp.maximum(i - 1, 0), 0))


def _full(shape):
    nd = len(shape)
    return pl.BlockSpec(shape, lambda i: (0,) * nd)


def _seg(cols):
    return pl.BlockSpec((1, 1, cols), lambda i: (jnp.minimum(i, 1), 0, 0))


def _lat_seg(cols):
    return pl.BlockSpec((1, 1, cols), lambda i: (1, 0, 0))


def _sds(shape, dtype=F32):
    return jax.ShapeDtypeStruct(shape, dtype)


def _sum0(x):
    return jnp.sum(x, axis=0, keepdims=True)


def _sigmoid(x):
    return jax.nn.sigmoid(x)


def _rms_mod(x, g, scale, shift):
    r = lax.rsqrt(jnp.mean(x * x, axis=-1, keepdims=True) + NORM_EPS)
    return (x * r * g) * (1.0 + scale) + shift


def _rms_mod_bwd(x, g, scale, dh):
    r = lax.rsqrt(jnp.mean(x * x, axis=-1, keepdims=True) + NORM_EPS)
    n = x * r
    dyg = dh * (1.0 + scale)
    dn = dyg * g
    dx = r * (dn - n * jnp.mean(dn * n, axis=-1, keepdims=True))
    return dx, _sum0(dyg * n), _sum0(dh * (n * g)), _sum0(dh)


def _head_of_lane(width):
    return (jnp.arange(width)[:, None] // HEAD_DIM == jnp.arange(128)[None, :]).astype(BF16)


def _split_dot(t, w, transposed):
    hi = t.astype(BF16)
    lo = (t - hi.astype(F32)).astype(BF16)
    f = _dot_t1 if transposed else _dot
    return f(hi, w) + f(lo, w)


def _head_sums(t, hl):
    return _split_dot(_split_dot(t, hl, False), hl, True)


def _rope_partner(x):
    n = x.shape[1]
    lane = lax.broadcasted_iota(jnp.int32, x.shape, 1)
    return jnp.where((lane & 16) == 0, pltpu.roll(x, n - 16, 1), pltpu.roll(x, 16, 1))


def _lanes(tab, width):
    return jnp.tile(tab, (1, width // tab.shape[1]))


def _head_norm_rope(x, gain, cos, sin, hl):
    r = lax.rsqrt(_head_sums(x * x, hl) * (1.0 / HEAD_DIM) + NORM_EPS)
    y = x * r * gain
    return y * cos + _rope_partner(y) * sin


def _head_norm_rope_bwd(x, gain, cos, sin, hl, dout):
    dy = dout * cos + _rope_partner(dout * sin)
    r = lax.rsqrt(_head_sums(x * x, hl) * (1.0 / HEAD_DIM) + NORM_EPS)
    n = x * r
    dn = dy * gain
    dx = r * (dn - n * (_head_sums(dn * n, hl) * (1.0 / HEAD_DIM)))
    return dx, _sum0(dy * n)


def _rope_tables(n_ctx, n_lat):
    t = jnp.arange(n_lat)
    pos = jnp.stack([(t // GRID_W).astype(F32), (t % GRID_W).astype(F32)], axis=1)
    n_freq = HEAD_DIM // 4
    freqs = ROPE_THETA ** (-jnp.arange(n_freq, dtype=F32) / n_freq)
    ang = pos[:, :, None] * freqs[None, None, :]
    cos = jnp.repeat(jnp.cos(ang)[:, :, None, :], 2, axis=2).reshape(n_lat, HEAD_DIM)
    sin = jnp.sin(ang)
    sin = jnp.stack([-sin, sin], axis=2).reshape(n_lat, HEAD_DIM)
    cos = jnp.concatenate([jnp.ones((n_ctx, HEAD_DIM), F32), cos], axis=0)
    sin = jnp.concatenate([jnp.zeros((n_ctx, HEAD_DIM), F32), sin], axis=0)
    return jnp.tile(cos, (1, 2)), jnp.tile(sin, (1, 2))


def _ssm_in(xa, g, scale, shift, w_in):
    n_rows, d = xa.shape
    e = w_in.shape[1] // 2

    def body(x_ref, g_ref, sc_ref, sh_ref, w_ref, u_ref, z_ref):
        h = _rms_mod(x_ref[...], g_ref[...], sc_ref[0], sh_ref[0])
        proj = _dot(h, w_ref[...])
        u_ref[...] = proj[:, :e]
        z_ref[...] = proj[:, e:]

    return pl.pallas_call(
        body, name="ssm_in", grid=(n_rows // ROW_TILE,),
        in_specs=[_rows(d), _full((1, d)), _seg(d), _seg(d), _full(w_in.shape)],
        out_specs=[_rows(e), _rows(e)], out_shape=[_sds((n_rows, e)), _sds((n_rows, e))],
        compiler_params=_cparams(1),
    )(xa, g, scale, shift, w_in)


def _s5_post_math(y, z, w_glu, b_glu, w_out):
    er = lax.erf(y * INV_SQRT2)
    g = 0.5 * y * (1.0 + er)
    sg = _sigmoid(_dot(g, w_glu) + b_glu)
    g2 = g * sg
    sz = _sigmoid(z)
    silu_z = z * sz
    m = g2 * silu_z
    return er, g, sg, g2, sz, silu_z, m, _dot(m, w_out)


def _ssm_post(xa, y0, y1, z, gate, w_glu, b_glu, w_out):
    n_rows, d = xa.shape
    e = z.shape[1]

    def body(x_ref, y0_ref, y1_ref, z_ref, gt_ref, wg_ref, bg_ref, wo_ref, o_ref):
        out = _s5_post_math(y0_ref[...] + y1_ref[...], z_ref[...], wg_ref[...], bg_ref[...], wo_ref[...])[-1]
        o_ref[...] = x_ref[...] + gt_ref[0] * out

    return pl.pallas_call(
        body, name="ssm_post", grid=(n_rows // ROW_TILE,),
        in_specs=[_rows(d), _rows(e), _rows(e), _rows(e), _seg(d), _full(w_glu.shape), _full((1, e)), _full(w_out.shape)],
        out_specs=_rows(d), out_shape=_sds((n_rows, d)),
        compiler_params=_cparams(1),
    )(xa, y0, y1, z, gate, w_glu, b_glu, w_out)


def _init_acc(first, *refs):
    @pl.when(first)
    def _():
        for r in refs:
            r[...] = jnp.zeros_like(r)


def _ssm_post_bwd(dxa, y0, y1, z, gate, w_glu, b_glu, w_out, w_glu_t, w_out_t):
    n_rows, d = dxa.shape
    e = z.shape[1]

    def body(dx_ref, y0_ref, y1_ref, z_ref, gt_ref, wg_ref, bg_ref, wo_ref, wgt_ref, wot_ref,
             dy_ref, dz_ref, dgt_ref, dwo_ref, dwg_ref, dbg_ref):
        i = pl.program_id(0)
        _init_acc(i == 0, dwo_ref, dwg_ref, dbg_ref)
        _init_acc(i <= 1, dgt_ref)
        y, zz = y0_ref[...] + y1_ref[...], z_ref[...]
        er, g, sg, g2, sz, silu_z, m, out = _s5_post_math(y, zz, wg_ref[...], bg_ref[...], wo_ref[...])
        dxa_t = dx_ref[...]
        dgt_ref[0] += _sum0(dxa_t * out)
        dout = gt_ref[0] * dxa_t
        dm = _dot(dout, wot_ref[...])
        dwo_ref[...] += _dot_t0(m, dout)
        dg2 = dm * silu_z
        dz_ref[...] = dm * g2 * (sz * (1.0 + zz * (1.0 - sz)))
        dt = dg2 * g * sg * (1.0 - sg)
        dwg_ref[...] += _dot_t0(g, dt)
        dbg_ref[...] += _sum0(dt)
        dg = dg2 * sg + _dot(dt, wgt_ref[...])
        dy_ref[...] = dg * (0.5 * (1.0 + er) + y * jnp.exp(-0.5 * y * y) * INV_SQRT_2PI)

    return pl.pallas_call(
        body, name="ssm_post_bwd", grid=(n_rows // ROW_TILE,),
        in_specs=[_rows(d), _rows(e), _rows(e), _rows(e), _seg(d), _full(w_glu.shape), _full((1, e)), _full(w_out.shape),
                  _full(w_glu_t.shape), _full(w_out_t.shape)],
        out_specs=[_rows(e), _rows(e), _seg(d), _full(w_out.shape), _full(w_glu.shape), _full((1, e))],
        out_shape=[_sds((n_rows, e)), _sds((n_rows, e)), _sds((2, 1, d)), _sds(w_out.shape), _sds(w_glu.shape), _sds((1, e))],
        compiler_params=_cparams(1),
    )(dxa, y0, y1, z, gate, w_glu, b_glu, w_out, w_glu_t, w_out_t)


def _ssm_in_bwd(du0, du1, dz, xa, dxa_next, g, scale, shift, w_in_t):
    n_rows, d = xa.shape
    e = dz.shape[1]
    n_lat = n_rows - ROW_TILE

    def body(du0_ref, du1_ref, dz_ref, x_ref, dn_ref, g_ref, sc_ref, sh_ref, wt_ref,
             gx_ref, dw_ref, dg_ref, dsc_ref, dsh_ref):
        i = pl.program_id(0)
        _init_acc(i == 0, dw_ref, dg_ref)
        _init_acc(i <= 1, dsc_ref, dsh_ref)
        x = x_ref[...]
        h = _rms_mod(x, g_ref[...], sc_ref[0], sh_ref[0])
        dproj = jnp.concatenate([du0_ref[...] + du1_ref[...], dz_ref[...]], axis=1)
        dh = _dot(dproj, wt_ref[...])
        dw_ref[...] += _dot_t0(h, dproj)
        dx, dg, dsc, dsh = _rms_mod_bwd(x, g_ref[...], sc_ref[0], dh)
        dg_ref[...] += dg
        dsc_ref[0] += dsc
        dsh_ref[0] += dsh
        gx_ref[...] = dn_ref[...] + dx

    return pl.pallas_call(
        body, name="ssm_in_bwd", grid=(n_rows // ROW_TILE,),
        in_specs=[_rows(e), _rows(e), _rows(e), _rows(d), _rows(d), _full((1, d)), _seg(d), _seg(d), _full(w_in_t.shape)],
        out_specs=[_rows_lat(d), _full((d, 2 * e)), _full((1, d)), _seg(d), _seg(d)],
        out_shape=[_sds((n_lat, d)), _sds((d, 2 * e)), _sds((1, d)), _sds((2, 1, d)), _sds((2, 1, d))],
        compiler_params=_cparams(1),
    )(du0, du1, dz, xa, dxa_next, g, scale, shift, w_in_t)


Q_WIDTH = N_Q_HEADS * HEAD_DIM
SM_SCALE = 1.0 / math.sqrt(HEAD_DIM)


def _attn_in(xa, g, scale, shift, w_in, q_gain, k_gain, cos, sin):
    n_rows, d = xa.shape
    qk = Q_WIDTH + KV_WIDTH

    def body(x_ref, g_ref, sc_ref, sh_ref, w_ref, qg_ref, kg_ref, cos_ref, sin_ref, hq_ref, hk_ref,
             q_ref, k_ref, v_ref, z_ref, raw_ref):
        h = _rms_mod(x_ref[...], g_ref[...], sc_ref[0], sh_ref[0])
        proj = _dot(h, w_ref[...])
        q_raw, k_raw = proj[:, :Q_WIDTH], proj[:, Q_WIDTH:qk]
        cos, sin = cos_ref[...], sin_ref[...]
        q = _head_norm_rope(q_raw, qg_ref[...], _lanes(cos, Q_WIDTH), _lanes(sin, Q_WIDTH), hq_ref[...])
        k = _head_norm_rope(k_raw, kg_ref[...], _lanes(cos, KV_WIDTH), _lanes(sin, KV_WIDTH), hk_ref[...])
        q_ref[...] = (q * SM_SCALE).astype(BF16)
        k_ref[...] = k.astype(BF16)
        v_ref[...] = proj[:, qk:qk + KV_WIDTH].astype(BF16)
        z_ref[...] = proj[:, qk + KV_WIDTH:]
        raw_ref[...] = proj[:, :qk]

    return pl.pallas_call(
        body, name="attn_in", grid=(n_rows // ROW_TILE,),
        in_specs=[_rows(d), _full((1, d)), _seg(d), _seg(d), _full(w_in.shape), _full((1, Q_WIDTH)), _full((1, KV_WIDTH)),
                  _rows(128), _rows(128), _full((Q_WIDTH, 128)), _full((KV_WIDTH, 128))],
        out_specs=[_rows_lat(Q_WIDTH), _rows(KV_WIDTH), _rows(KV_WIDTH), _rows(Q_WIDTH), _rows(qk)],
        out_shape=[_sds((n_rows - ROW_TILE, Q_WIDTH), BF16), _sds((n_rows, KV_WIDTH), BF16), _sds((n_rows, KV_WIDTH), BF16),
                   _sds((n_rows, Q_WIDTH)), _sds((n_rows, qk))],
        compiler_params=_cparams(1),
    )(xa, g, scale, shift, w_in, q_gain, k_gain, cos, sin, _head_of_lane(Q_WIDTH), _head_of_lane(KV_WIDTH))


GROUP_WIDTH = KV_REP * HEAD_DIM


def _stack_heads(ref):
    return jnp.concatenate([ref[:, h * HEAD_DIM:(h + 1) * HEAD_DIM] for h in range(KV_REP)], axis=0)


def _unstack_heads(a_t, tq):
    return jnp.concatenate([a_t[:, h * tq:(h + 1) * tq].T for h in range(KV_REP)], axis=1)


def _kv_tile(n_keys):
    return 768 if n_keys % 768 == 0 else 256


def _kv_tile_fwd(n_keys):
    return 1408 if n_keys % 1408 == 0 else _kv_tile(n_keys)


def _q_tile(n_lat):
    return 512 if n_lat % 512 == 0 else 256


def _flash_fwd(q, k, v_t):
    n_lat = q.shape[0]
    tq = _q_tile(n_lat)
    rows = KV_REP * tq
    n_kv, tk, n_q = k.shape[1], k.shape[2], n_lat // tq

    v_rows = v_t.shape[2]

    def body(q_ref, k_ref, vt_ref, o_ref, lse_ref):
        q = _stack_heads(q_ref)

        def step(j, carry):
            m_prev, acc = carry
            s_t = _dot_t1(k_ref[0, j], q)
            m_new = jnp.maximum(m_prev, jnp.max(s_t, axis=0, keepdims=True))
            alpha = jnp.exp(m_prev - m_new)
            p_t = jnp.exp(s_t - m_new)
            return m_new, alpha * acc + _dot(vt_ref[0, j], p_t)

        init = (jnp.full((1, rows), -jnp.inf, F32), jnp.zeros((v_rows, rows), F32))
        m, acc = lax.fori_loop(0, n_kv, step, init)
        l = acc[HEAD_DIM:HEAD_DIM + 1]
        o_ref[...] = _unstack_heads(acc[:HEAD_DIM] / l, tq)
        lse_ref[0, 0] = m + jnp.log(l)

    kv_all = lambda a: pl.BlockSpec((1,) + a.shape[1:], lambda g, i: (g, 0, 0, 0))
    return pl.pallas_call(
        body, name="flash_fwd", grid=(N_KV_HEADS, n_q),
        in_specs=[pl.BlockSpec((tq, GROUP_WIDTH), lambda g, i: (i, g)), kv_all(k), kv_all(v_t)],
        out_specs=[pl.BlockSpec((tq, GROUP_WIDTH), lambda g, i: (i, g)),
                   pl.BlockSpec((1, 1, 1, rows), lambda g, i: (g, i, 0, 0))],
        out_shape=[_sds((n_lat, Q_WIDTH)), _sds((N_KV_HEADS, n_q, 1, rows))],
        compiler_params=_cparams(2),
    )(q, k, v_t)


def _flash_bwd(q, k, k_t, v, do, lse_t, delta_t):
    n_lat = q.shape[0]
    tq = _q_tile(n_lat)
    rows = KV_REP * tq
    n_kv, tk, n_q = k.shape[1], k.shape[2], n_lat // tq

    def body(q_ref, k_ref, kt_ref, v_ref, do_ref, lse_ref, dl_ref, dq_ref, dk_ref, dv_ref):
        _init_acc(pl.program_id(1) == 0, dk_ref, dv_ref)
        q, do = _stack_heads(q_ref), _stack_heads(do_ref)
        lse, delta = lse_ref[0, 0], dl_ref[0, 0]

        def step(j, dq_acc):
            p_t = jnp.exp(_dot_t1(k_ref[0, j], q) - lse)
            dv_ref[0, j] += _dot(p_t, do)
            ds_t = p_t * (_dot_t1(v_ref[0, j], do) - delta)
            dk_ref[0, j] += _dot(ds_t, q)
            return dq_acc + _dot(kt_ref[0, j], ds_t)

        dq = lax.fori_loop(0, n_kv, step, jnp.zeros((HEAD_DIM, rows), F32))
        dq_ref[...] = _unstack_heads(dq, tq)

    qspec = pl.BlockSpec((tq, GROUP_WIDTH), lambda g, i: (i, g))
    rowspec = pl.BlockSpec((1, 1, 1, rows), lambda g, i: (g, i, 0, 0))
    kv_all = lambda a: pl.BlockSpec((1,) + a.shape[1:], lambda g, i: (g, 0, 0, 0))
    return pl.pallas_call(
        body, name="flash_bwd", grid=(N_KV_HEADS, n_q),
        in_specs=[qspec, kv_all(k), kv_all(k_t), kv_all(v), qspec, rowspec, rowspec],
        out_specs=[qspec, kv_all(k), kv_all(k)],
        out_shape=[_sds((n_lat, Q_WIDTH)), _sds(k.shape), _sds(k.shape)],
        compiler_params=_cparams(2),
    )(q, k, k_t, v, do, lse_t, delta_t)


def _to_lane_stacked(a, tq):
    n_lat = a.shape[0]
    a = a.reshape(n_lat // tq, tq, N_KV_HEADS, KV_REP).transpose(2, 0, 3, 1)
    return a.reshape(N_KV_HEADS, n_lat // tq, 1, KV_REP * tq)


def _attn_post_loss(o, z, xa, gate, w_out, w_out_t, final_g, target):
    n_lat, d = target.shape
    e = o.shape[1]
    head_of_lane = (jnp.arange(e)[:, None] // HEAD_DIM == jnp.arange(128)[None, :]).astype(BF16)

    def body(o_ref, z_ref, x_ref, gt_ref, w_ref, wt_ref, fg_ref, tg_ref, hl_ref,
             do_ref, dl_ref, dz_ref, dx_ref, loss_ref, dfg_ref, dgt_ref, dw_ref):
        _init_acc(pl.program_id(0) == 0, loss_ref, dfg_ref, dgt_ref, dw_ref)
        oo, zz, gate_t, fg = o_ref[...], z_ref[...], gt_ref[0], fg_ref[...]
        sz = _sigmoid(zz)
        silu_z = zz * sz
        m = oo * silu_z
        out = _dot(m, w_ref[...])
        x2 = x_ref[...] + gate_t * out
        r = lax.rsqrt(jnp.mean(x2 * x2, axis=-1, keepdims=True) + NORM_EPS)
        n = x2 * r
        err = n * fg - tg_ref[...]
        loss_ref[...] += 0.5 * jnp.sum(jnp.mean(err * err, axis=-1, keepdims=True), axis=0, keepdims=True)
        dy = err * (1.0 / d)
        dfg_ref[...] += _sum0(dy * n)
        dn = dy * fg
        dx2 = r * (dn - n * jnp.mean(dn * n, axis=-1, keepdims=True))
        dx_ref[...] = dx2
        dgt_ref[...] += _sum0(dx2 * out)
        dout = gate_t * dx2
        dw_ref[...] += _dot_t0(m, dout)
        dm = _dot(dout, wt_ref[...])
        do = dm * silu_z
        do_ref[...] = do.astype(BF16)
        prod = do * oo
        hi = prod.astype(BF16)
        lo = (prod - hi.astype(F32)).astype(BF16)
        dl_ref[...] = _dot(hi, hl_ref[...]) + _dot(lo, hl_ref[...])
        dz_ref[...] = dm * oo * (sz * (1.0 + zz * (1.0 - sz)))

    return pl.pallas_call(
        body, name="attn_post_loss", grid=(n_lat // ROW_TILE,),
        in_specs=[_rows(e), _rows_skip_ctx(e), _rows_skip_ctx(d), _lat_seg(d), _full(w_out.shape), _full(w_out_t.shape),
                  _full((1, d)), _rows(d), _full((e, 128))],
        out_specs=[_rows(e), _rows(128), _rows(e), _rows(d), _full((1, 1)), _full((1, d)), _full((1, d)), _full(w_out.shape)],
        out_shape=[_sds((n_lat, e), BF16), _sds((n_lat, 128)), _sds((n_lat, e)), _sds((n_lat, d)), _sds((1, 1)), _sds((1, d)),
                   _sds((1, d)), _sds(w_out.shape)],
        compiler_params=_cparams(1),
    )(o, z, xa, gate, w_out, w_out_t, final_g, target, head_of_lane)


def _attn_in_bwd(dq, dk, dv, dz, raw, xa, dx2, g, scale, shift, q_gain, k_gain, cos, sin, w_in_t):
    n_rows, d = xa.shape
    qk = Q_WIDTH + KV_WIDTH
    n_in = w_in_t.shape[0]

    def body(dq_ref, dk_ref, dv_ref, dz_ref, raw_ref, x_ref, dx2_ref, g_ref, sc_ref, sh_ref, qg_ref, kg_ref, cos_ref, sin_ref,
             wt_ref, hq_ref, hk_ref, dxa_ref, dw_ref, dqg_ref, dkg_ref, dg_ref, dsc_ref, dsh_ref):
        i = pl.program_id(0)
        _init_acc(i == 0, dw_ref, dqg_ref, dkg_ref, dg_ref)
        _init_acc(i <= 1, dsc_ref, dsh_ref)
        is_lat = (i > 0).astype(F32)
        x = x_ref[...]
        h = _rms_mod(x, g_ref[...], sc_ref[0], sh_ref[0])
        cos, sin = cos_ref[...], sin_ref[...]
        raw_t = raw_ref[...]
        dq_raw, dqg = _head_norm_rope_bwd(raw_t[:, :Q_WIDTH], qg_ref[...], _lanes(cos, Q_WIDTH), _lanes(sin, Q_WIDTH),
                                          hq_ref[...], dq_ref[...] * (SM_SCALE * is_lat))
        dk_raw, dkg = _head_norm_rope_bwd(raw_t[:, Q_WIDTH:], kg_ref[...], _lanes(cos, KV_WIDTH), _lanes(sin, KV_WIDTH),
                                          hk_ref[...], dk_ref[...])
        dqg_ref[...] += dqg
        dkg_ref[...] += dkg
        dproj = jnp.concatenate([dq_raw, dk_raw, dv_ref[...], dz_ref[...] * is_lat], axis=1)
        dh = _dot(dproj, wt_ref[...])
        dw_ref[...] += _dot_t0(h, dproj)
        dx, dg, dsc, dsh = _rms_mod_bwd(x, g_ref[...], sc_ref[0], dh)
        dg_ref[...] += dg
        dsc_ref[0] += dsc
        dsh_ref[0] += dsh
        dxa_ref[...] = dx + dx2_ref[...] * is_lat

    return pl.pallas_call(
        body, name="attn_in_bwd", grid=(n_rows // ROW_TILE,),
        in_specs=[_rows_lat(Q_WIDTH), _rows(KV_WIDTH), _rows(KV_WIDTH), _rows_lat(Q_WIDTH), _rows(qk), _rows(d), _rows_lat(d),
                  _full((1, d)), _seg(d), _seg(d), _full((1, Q_WIDTH)), _full((1, KV_WIDTH)), _rows(128), _rows(128),
                  _full(w_in_t.shape), _full((Q_WIDTH, 128)), _full((KV_WIDTH, 128))],
        out_specs=[_rows(d), _full((d, n_in)), _full((1, Q_WIDTH)), _full((1, KV_WIDTH)), _full((1, d)), _seg(d), _seg(d)],
        out_shape=[_sds((n_rows, d)), _sds((d, n_in)), _sds((1, Q_WIDTH)), _sds((1, KV_WIDTH)), _sds((1, d)),
                   _sds((2, 1, d)), _sds((2, 1, d))],
        compiler_params=_cparams(1),
    )(dq, dk, dv, dz, raw, xa, dx2, g, scale, shift, q_gain, k_gain, cos, sin, w_in_t,
      _head_of_lane(Q_WIDTH), _head_of_lane(KV_WIDTH))


def _heads_major(a, n_heads):
    return a.reshape(a.shape[0], n_heads, HEAD_DIM).transpose(1, 0, 2)


def _tokens_major(a):
    return a.transpose(1, 0, 2).reshape(a.shape[1], a.shape[0] * HEAD_DIM)


def _local_step(x, ctx, target, mods, norm_g, ssm, w_ssm_in, w_glu, b_glu, w_ssm_out, w_attn_in, q_norm, k_norm, w_attn_out,
                final_g, attn_exchange=None, post_exchange=None):
    n_ctx, d = ctx.shape
    assert n_ctx == ROW_TILE
    n_lat = x.shape[0]
    (shift0, scale0, gate0), (shift1, scale1, gate1) = mods
    g0, g1, fg = norm_g[0:1], norm_g[1:2], final_g.reshape(1, d)
    b_glu = b_glu.reshape(1, -1)
    q_gain = jnp.tile(q_norm.reshape(1, HEAD_DIM), (1, N_Q_HEADS))
    k_gain = jnp.tile(k_norm.reshape(1, HEAD_DIM), (1, N_KV_HEADS))
    cos, sin = _rope_tables(n_ctx, n_lat)

    xa0 = jnp.concatenate([ctx, x], axis=0)
    u, z0 = _ssm_in(xa0, g0, scale0, shift0, w_ssm_in)
    gather = lambda ex: (ex[0], False) if ex else None
    (y0, y1), saved, gathered = _s5_forward(u, ssm, n_ctx, (gather(attn_exchange), gather(post_exchange)))
    if attn_exchange:
        w_attn_in, w_attn_out = attn_exchange[1](gathered[0])
    if post_exchange:
        w_glu, w_ssm_out = post_exchange[1](gathered[1])
    xa1 = _ssm_post(xa0, y0, y1, z0, gate0, w_glu, b_glu, w_ssm_out)

    q, k, v, z1, raw = _attn_in(xa1, g1, scale1, shift1, w_attn_in, q_gain, k_gain, cos, sin)
    tq, tk, tk_fwd = _q_tile(n_lat), _kv_tile(n_ctx + n_lat), _kv_tile_fwd(n_ctx + n_lat)
    k_h, v_h = _heads_major(k, N_KV_HEADS), _heads_major(v, N_KV_HEADS)
    k_b, v_b = k_h.reshape(N_KV_HEADS, -1, tk, HEAD_DIM), v_h.reshape(N_KV_HEADS, -1, tk, HEAD_DIM)
    v_t = v_h.reshape(N_KV_HEADS, -1, tk_fwd, HEAD_DIM).transpose(0, 1, 3, 2)
    v_t_ones = jnp.concatenate([v_t, jnp.ones(v_t.shape[:2] + (16, tk_fwd), BF16)], axis=2)
    o, lse_t = _flash_fwd(q, k_h.reshape(N_KV_HEADS, -1, tk_fwd, HEAD_DIM), v_t_ones)
    do, delta, dz1, dx2, loss, d_fg, d_gate1, d_w_attn_out = _attn_post_loss(
        o, z1, xa1, gate1, w_attn_out, w_attn_out.T, fg, target)

    dq, dk_b, dv_b = _flash_bwd(q, k_b, k_b.transpose(0, 1, 3, 2), v_b, do, lse_t, _to_lane_stacked(delta[:, :N_Q_HEADS], tq))
    keys_major = lambda a: _tokens_major(a.reshape(N_KV_HEADS, -1, HEAD_DIM))
    dxa1, d_w_attn_in, d_qg, d_kg, d_g1, d_scale1, d_shift1 = _attn_in_bwd(
        dq, keys_major(dk_b), keys_major(dv_b), dz1, raw, xa1, dx2, g1, scale1, shift1,
        q_gain, k_gain, cos, sin, w_attn_in.T)
    dy, dz0, d_gate0, d_w_ssm_out, d_w_glu, d_b_glu = _ssm_post_bwd(
        dxa1, y0, y1, z0, gate0, w_glu, b_glu, w_ssm_out, w_glu.T, w_ssm_out.T)
    scatter = lambda ex, *g: (ex[2](*g), True) if ex else None
    (du0, du1), d_ssm, parts = _s5_backward(
        u, dy, ssm, saved, n_ctx,
        (scatter(attn_exchange, d_w_attn_in, d_w_attn_out), scatter(post_exchange, d_w_glu, d_w_ssm_out)))
    grad_x, d_w_ssm_in, d_g0, d_scale0, d_shift0 = _ssm_in_bwd(du0, du1, dz0, xa0, dxa1, g0, scale0, shift0, w_ssm_in.T)

    d_gate1_seg = jnp.concatenate([jnp.zeros((1, 1, d), F32), d_gate1.reshape(1, 1, d)], axis=0)
    grads = dict(
        norm_g=jnp.concatenate([d_g0, d_g1], axis=0), ssm_w_in=d_w_ssm_in, ssm=d_ssm, ssm_b_glu=d_b_glu.reshape(-1),
        attn_q_norm=d_qg.reshape(N_Q_HEADS, HEAD_DIM).sum(0), attn_k_norm=d_kg.reshape(N_KV_HEADS, HEAD_DIM).sum(0),
        final_norm_g=d_fg.reshape(-1))
    if attn_exchange:
        grads.update(attn_parts=parts[0])
    else:
        grads.update(attn_w_in=d_w_attn_in, attn_w_out=d_w_attn_out)
    if post_exchange:
        grads.update(post_parts=parts[1])
    else:
        grads.update(ssm_w_glu=d_w_glu, ssm_w_out=d_w_ssm_out)
    d_mods = ((d_shift0, d_scale0, d_gate0), (d_shift1, d_scale1, d_gate1_seg))
    return loss[0, 0], grad_x, grads, d_mods


def _my_index():
    return 4 * lax.axis_index("x") + 2 * lax.axis_index("y") + lax.axis_index("c")


def _peer(k):
    mx, my, mc = lax.axis_index("x"), lax.axis_index("y"), lax.axis_index("c")
    px = 1 - mx if k & 4 else mx
    py = 1 - my if k & 2 else my
    pc = 1 - mc if k & 1 else mc
    return (px, py, pc), 4 * px + 2 * py + pc


HBM_SPEC = pl.BlockSpec(memory_space=pl.ANY)


def _exchange(x, name, all_to_all):
    def body(x_ref, out_ref, send_sems, recv_sems, local_sem):
        _exchange_copies(all_to_all, x_ref, out_ref, send_sems, recv_sems, local_sem, start=True)
        _exchange_copies(all_to_all, x_ref, out_ref, send_sems, recv_sems, local_sem, start=False)

    return pl.pallas_call(
        body, name=name, in_specs=[HBM_SPEC], out_specs=HBM_SPEC,
        out_shape=_exchange_out_shape(x, all_to_all), scratch_shapes=_exchange_semaphores(),
    )(x)


def _exchange_out_shape(x, all_to_all):
    return _sds((N_DEV,) + tuple(x.shape[1:] if all_to_all else x.shape), x.dtype)


def _exchange_semaphores():
    return [pltpu.SemaphoreType.DMA((N_DEV - 1,)), pltpu.SemaphoreType.DMA((N_DEV - 1,)), pltpu.SemaphoreType.DMA]


def _exchange_copies(all_to_all, x_ref, out_ref, send_sems, recv_sems, local_sem, start):
    me = _my_index()
    mine = pltpu.make_async_copy(x_ref.at[me] if all_to_all else x_ref, out_ref.at[me], local_sem)
    if start:
        mine.start()
    for k in range(1, N_DEV):
        peer, peer_idx = _peer(k)
        send = pltpu.make_async_remote_copy(
            src_ref=x_ref.at[peer_idx] if all_to_all else x_ref, dst_ref=out_ref.at[me],
            send_sem=send_sems.at[k - 1], recv_sem=recv_sems.at[k - 1], device_id=peer, device_id_type=MESH_IDS)
        if start:
            send.start()
        else:
            pltpu.make_async_remote_copy(
                src_ref=x_ref.at[me] if all_to_all else x_ref, dst_ref=out_ref.at[peer_idx],
                send_sem=send_sems.at[k - 1], recv_sem=recv_sems.at[k - 1], device_id=peer,
                device_id_type=MESH_IDS).wait_recv()
            send.wait_send()
    if not start:
        mine.wait()


def _ride(rider, first, last, refs):
    @pl.when(first)
    def _():
        _exchange_copies(rider[1], *refs, start=True)

    @pl.when(last)
    def _():
        _exchange_copies(rider[1], *refs, start=False)


MOD_ROWS = 16
CTX_ROW = N_DEV


def _mod_fwd(cond, w_shard, b_cols):
    n_layers, d, cols = w_shard.shape

    def body(c_ref, w_ref, b_ref, o_ref):
        c = c_ref[...]
        s = c * _sigmoid(c)
        for i in range(n_layers):
            o_ref[i] = _dot(s, w_ref[i]) + b_ref[i]

    return pl.pallas_call(
        body, name="mod_fwd", out_shape=_sds((n_layers, MOD_ROWS, cols)),
        compiler_params=pltpu.CompilerParams(vmem_limit_bytes=VMEM_LIMIT),
    )(cond, w_shard, b_cols.reshape(n_layers, 1, cols))


def _mod_bwd(cond, d_lat_cols, d_ctx_cols, w_shard):
    n_layers, d, cols = w_shard.shape

    def body(c_ref, dl_ref, dc_ref, w_ref, dw_ref, dcc_ref):
        c = c_ref[...]
        sg = _sigmoid(c)
        s = c * sg
        d_s = jnp.zeros((MOD_ROWS, d), F32)
        for i in range(n_layers):
            d_ctx = dc_ref[0, i]
            for j in range(1, N_DEV):
                d_ctx = d_ctx + dc_ref[j, i]
            dm = jnp.concatenate([dl_ref[i], d_ctx, jnp.zeros((MOD_ROWS - N_DEV - 1, cols), F32)], axis=0)
            dw_ref[i] = _dot_t0(s, dm)
            d_s = d_s + _dot_t1(dm, w_ref[i])
        d_c = d_s * (sg * (1.0 + c * (1.0 - sg)))
        dcc_ref[...] = d_c[CTX_ROW:CTX_ROW + 1]

    return pl.pallas_call(
        body, name="mod_bwd", out_shape=[_sds((n_layers, d, cols)), _sds((1, d))],
        compiler_params=pltpu.CompilerParams(vmem_limit_bytes=VMEM_LIMIT),
    )(cond, d_lat_cols, d_ctx_cols, w_shard)


ADAM_TILE = 512


def _adamw(w, g_parts, m, v, name):
    n_parts, n_rows, lanes = g_parts.shape
    c1 = 1.0 - ADAM_B1 ** ADAM_STEP
    c2 = 1.0 - ADAM_B2 ** ADAM_STEP

    def body(w_ref, g_ref, m_ref, v_ref, go_ref, d_ref, mo_ref, vo_ref):
        g = g_ref[0].astype(F32)
        for p in range(1, n_parts):
            g = g + g_ref[p].astype(F32)
        m_new = ADAM_B1 * m_ref[...] + (1.0 - ADAM_B1) * g
        v_new = ADAM_B2 * v_ref[...] + (1.0 - ADAM_B2) * (g * g)
        go_ref[...] = g
        mo_ref[...] = m_new
        vo_ref[...] = v_new
        d_ref[...] = -ADAM_LR * ((m_new / c1) / (jnp.sqrt(v_new / c2) + ADAM_EPS) + ADAM_WD * w_ref[...])

    row = pl.BlockSpec((ADAM_TILE, lanes), lambda i: (i, 0))
    return pl.pallas_call(
        body, name=name, grid=(n_rows // ADAM_TILE,),
        in_specs=[row, pl.BlockSpec((n_parts, ADAM_TILE, lanes), lambda i: (0, i, 0)), row, row],
        out_specs=[row] * 4, out_shape=[_sds((n_rows, lanes))] * 4,
        compiler_params=_cparams(1),
    )(w, g_parts, m, v)


def _sum_parts(parts):
    n_parts, n_rows, lanes = parts.shape

    def body(p_ref, o_ref):
        acc = p_ref[0]
        for p in range(1, n_parts):
            acc = acc + p_ref[p]
        o_ref[...] = acc

    return pl.pallas_call(body, name="sum_parts", out_shape=_sds((n_rows, lanes)))(parts)


def _pack(arrays, row_multiple):
    parts = []
    for a in arrays:
        flat = a.reshape(-1)
        parts.append(jnp.pad(flat, (0, (-flat.shape[0]) % 1024)))
    flat = jnp.concatenate(parts)
    flat = jnp.pad(flat, (0, (-flat.shape[0]) % (row_multiple * 128)))
    return flat.reshape(-1, 128)


def _unpack(packed, shapes):
    flat = packed.reshape(-1)
    out, pos = [], 0
    for s in shapes:
        n = math.prod(s)
        out.append(flat[pos:pos + n].reshape(s))
        pos += n + (-n) % 1024
    return out


WEIGHT_NAMES = ['c_ctx', 'w_mod', 'b_mod', 'norm_g', 'ssm_w_in', 'ssm_a_re', 'ssm_a_im', 'ssm_log_dt', 'ssm_b_re', 'ssm_b_im',
                'ssm_c_re', 'ssm_c_im', 'ssm_d', 'ssm_w_glu', 'ssm_b_glu', 'ssm_w_out', 'attn_w_in', 'attn_q_norm',
                'attn_k_norm', 'attn_w_out', 'final_norm_g']
FIRST_SHARDED = ['ssm_w_in']
POST_SHARDED = ['ssm_w_glu', 'ssm_w_out']
ATTN_SHARDED = ['attn_w_in', 'attn_w_out']
SHARDED = FIRST_SHARDED + POST_SHARDED + ATTN_SHARDED
COLUMN_SHARDED = ('ssm_w_in', 'attn_w_in')
REPLICATED = ['c_ctx', 'b_mod', 'norm_g', 'ssm_a_re', 'ssm_a_im', 'ssm_log_dt', 'ssm_b_re', 'ssm_b_im', 'ssm_c_re', 'ssm_c_im',
              'ssm_d', 'ssm_b_glu', 'attn_q_norm', 'attn_k_norm', 'final_norm_g']
SSM_NAMES = ['ssm_a_re', 'ssm_a_im', 'ssm_log_dt', 'ssm_b_re', 'ssm_b_im', 'ssm_c_re', 'ssm_c_im', 'ssm_d']


def _full_from_shards(gathered, name, shard_shape):
    rows, cols = shard_shape
    w = gathered.reshape(N_DEV, rows, cols)
    if name in COLUMN_SHARDED:
        return w.transpose(1, 0, 2).reshape(rows, N_DEV * cols)
    return w.reshape(N_DEV * rows, cols)


def _shards_from_full(g, name):
    if name in COLUMN_SHARDED:
        rows, cols = g.shape
        g = g.reshape(rows, N_DEV, cols // N_DEV).transpose(1, 0, 2)
    return g.reshape(N_DEV, -1, 128)


def kernel(x, c, ctx, c_ctx, w_mod, b_mod, norm_g, ssm_w_in, ssm_a_re, ssm_a_im, ssm_log_dt, ssm_b_re, ssm_b_im, ssm_c_re, ssm_c_im, ssm_d, ssm_w_glu, ssm_b_glu, ssm_w_out, attn_w_in, attn_q_norm, attn_k_norm, attn_w_out, final_norm_g, loss_target, m_c_ctx, m_w_mod, m_b_mod, m_norm_g, m_ssm_w_in, m_ssm_a_re, m_ssm_a_im, m_ssm_log_dt, m_ssm_b_re, m_ssm_b_im, m_ssm_c_re, m_ssm_c_im, m_ssm_d, m_ssm_w_glu, m_ssm_b_glu, m_ssm_w_out, m_attn_w_in, m_attn_q_norm, m_attn_k_norm, m_attn_w_out, m_final_norm_g, v_c_ctx, v_w_mod, v_b_mod, v_norm_g, v_ssm_w_in, v_ssm_a_re, v_ssm_a_im, v_ssm_log_dt, v_ssm_b_re, v_ssm_b_im, v_ssm_c_re, v_ssm_c_im, v_ssm_d, v_ssm_w_glu, v_ssm_b_glu, v_ssm_w_out, v_attn_w_in, v_attn_q_norm, v_attn_k_norm, v_attn_w_out, v_final_norm_g):
    env = dict(locals())
    weights = {n: env[n] for n in WEIGHT_NAMES}
    mom_m = {n: env["m_" + n] for n in WEIGHT_NAMES}
    mom_v = {n: env["v_" + n] for n in WEIGHT_NAMES}
    d = D_MODEL
    me = _my_index()
    mod_cols = w_mod.shape[-1]

    c_all = _exchange(c.reshape(8, d // 8), "gather_c", False).reshape(N_DEV, d)
    cond = jnp.concatenate([c_all, c_ctx.reshape(1, d), jnp.zeros((MOD_ROWS - N_DEV - 1, d), F32)], axis=0)
    shard_shapes = {n: weights[n].shape[1:] for n in SHARDED}
    pack_shards = lambda names: _pack([weights[n] for n in names], 1).astype(BF16)

    def unpack_full(gathered, names):
        full, pos = [], 0
        for n in names:
            rows = math.prod(shard_shapes[n]) // 128
            full.append(_full_from_shards(gathered[:, pos:pos + rows], n, shard_shapes[n]))
            pos += rows
        return full

    def exchange_of(names):
        return (pack_shards(names), lambda gathered: unpack_full(gathered, names),
                lambda *grads: jnp.concatenate([_shards_from_full(t, n) for t, n in zip(grads, names)], axis=1).astype(BF16))

    (w_ssm_in,) = unpack_full(_exchange(pack_shards(FIRST_SHARDED), "gather_ssm_w_in", False), FIRST_SHARDED)

    b_cols = lax.dynamic_slice(b_mod, (0, me * mod_cols), (2, mod_cols))
    mod_shard = _mod_fwd(cond, w_mod, b_cols)
    mod_all = _exchange(mod_shard.reshape(2 * MOD_ROWS, mod_cols), "gather_mod", False)
    mod_full = mod_all.reshape(N_DEV, 2, MOD_ROWS, mod_cols).transpose(1, 2, 0, 3).reshape(2, MOD_ROWS, 3 * d)
    lat_rows = lax.dynamic_slice(mod_full, (0, me, 0), (2, 1, 3 * d))
    mods = []
    for i in range(2):
        seg = jnp.stack([mod_full[i, CTX_ROW:CTX_ROW + 1], lat_rows[i]])
        mods.append((seg[:, :, :d], seg[:, :, d:2 * d], seg[:, :, 2 * d:]))

    ssm = tuple(weights[n][0] for n in SSM_NAMES)
    loss, grad_x, g, d_mods = _local_step(
        x[0], ctx[0], loss_target[0], mods, norm_g, ssm, w_ssm_in, None, ssm_b_glu[0], None,
        None, attn_q_norm[0], attn_k_norm[0], None, final_norm_g, exchange_of(ATTN_SHARDED), exchange_of(POST_SHARDED))

    d_rows = jnp.stack([jnp.concatenate(dm, axis=-1) for dm in d_mods])
    d_rows = jnp.concatenate([d_rows.reshape(4, 3 * d), jnp.zeros((4, 3 * d), F32)], axis=0)
    d_all = _exchange(d_rows, "gather_dmod", False)[:, :4].reshape(N_DEV, 2, 2, 3 * d)
    d_all = lax.dynamic_slice(d_all, (0, 0, 0, me * mod_cols), (N_DEV, 2, 2, mod_cols))
    d_w_mod, d_c_ctx = _mod_bwd(cond, d_all[:, :, 1].transpose(1, 0, 2), d_all[:, :, 0:1], w_mod)
    d_b_mod = jnp.stack([jnp.concatenate([t[0] + t[1] for t in dm], axis=-1).reshape(3 * d) for dm in d_mods])

    first_parts = _exchange(_shards_from_full(g['ssm_w_in'], 'ssm_w_in').astype(BF16), "scatter_ssm_w_in_grads", True)
    g_big_parts = jnp.concatenate([first_parts, g['post_parts'], g['attn_parts']], axis=1)
    pack_big = lambda t: _pack([t[n] for n in SHARDED], ADAM_TILE)
    big = _adamw(pack_big(weights), g_big_parts, pack_big(mom_m), pack_big(mom_v), "adamw_sharded")
    big = [_unpack(t, [weights[n].shape for n in SHARDED]) for t in big]

    mod_res = _adamw(_pack([w_mod], ADAM_TILE), _pack([d_w_mod], ADAM_TILE)[None], _pack([m_w_mod], ADAM_TILE),
                     _pack([v_w_mod], ADAM_TILE), "adamw_w_mod")
    mod_res = [t.reshape(w_mod.shape) for t in mod_res]

    small = dict(zip(SSM_NAMES, g['ssm']))
    small.update(c_ctx=d_c_ctx, b_mod=d_b_mod, norm_g=g['norm_g'], ssm_b_glu=g['ssm_b_glu'], attn_q_norm=g['attn_q_norm'],
                 attn_k_norm=g['attn_k_norm'], final_norm_g=g['final_norm_g'])
    pack_small = lambda t, last: _pack([t[n] for n in REPLICATED] + [last], ADAM_TILE)
    no_weight = jnp.zeros((1,), F32)
    g_small = pack_small(small, loss.reshape(1))
    slices = _exchange(g_small.reshape(N_DEV, -1, 128), "scatter_small_grads", True)
    g_small = _exchange(_sum_parts(slices), "gather_small_grads", False).reshape(1, -1, 128)
    rep = _adamw(pack_small(weights, no_weight), g_small, pack_small(mom_m, no_weight), pack_small(mom_v, no_weight),
                 "adamw_replicated")
    rep = [_unpack(t, [weights[n].shape for n in REPLICATED] + [(1,)]) for t in rep]
    loss = rep[0][-1][0]

    results = []
    for kind in range(4):
        by_name = dict(zip(SHARDED, big[kind]))
        by_name.update(zip(REPLICATED, rep[kind]))
        by_name['w_mod'] = mod_res[kind]
        results.extend(by_name[n] for n in WEIGHT_NAMES)
    return (loss, grad_x[None], *results)
```

```python
import functools
import math

import jax
import jax.numpy as jnp
from jax import lax
from jax.experimental import pallas as pl
from jax.experimental.pallas import tpu as pltpu

F32 = jnp.float32
BF16 = jnp.bfloat16

N_DEV = 8
D_MODEL = 1024
NORM_EPS = 1e-6
SSM_GROUP = 16
SSM_GROUPS = 64
SSM_STATE = 64
GROUPS_PER_BLOCK = 8
N_BLOCKS = SSM_GROUPS // GROUPS_PER_BLOCK
HALF = GROUPS_PER_BLOCK * SSM_STATE
HEAD_DIM = 64
N_Q_HEADS = 16
N_KV_HEADS = 4
KV_REP = N_Q_HEADS // N_KV_HEADS
KV_WIDTH = N_KV_HEADS * HEAD_DIM
GRID_W = 64
ROPE_THETA = 10000.0
ADAM_LR, ADAM_B1, ADAM_B2, ADAM_EPS, ADAM_WD, ADAM_STEP = 0.001, 0.9, 0.999, 1e-08, 0.01, 10

ROW_TILE = 256
SCAN_CHUNK = 256
VMEM_LIMIT = 56 * 1024 * 1024
MESH_IDS = pl.DeviceIdType.MESH


def _cparams(n_axes):
    return pltpu.CompilerParams(dimension_semantics=("arbitrary",) * n_axes, vmem_limit_bytes=VMEM_LIMIT)


def _dot(a, b):
    return jnp.dot(a.astype(BF16), b.astype(BF16), preferred_element_type=F32)


def _dot_t0(a, b):
    return lax.dot_general(a.astype(BF16), b.astype(BF16), (((0,), (0,)), ((), ())), preferred_element_type=F32)


def _dot_t1(a, b):
    return lax.dot_general(a.astype(BF16), b.astype(BF16), (((1,), (1,)), ((), ())), preferred_element_type=F32)


def _s5_prep(a_re, a_im, log_dt, b_re, b_im):
    dt = jnp.exp(log_dt)[:, None]
    ldr, ldi = a_re * dt, a_im * dt
    mag = jnp.exp(ldr)
    abar_re, abar_im = mag * jnp.cos(ldi), mag * jnp.sin(ldi)
    den = a_re * a_re + a_im * a_im
    num_re, num_im = abar_re - 1.0, abar_im
    coef_re = (num_re * a_re + num_im * a_im) / den
    coef_im = (num_im * a_re - num_re * a_im) / den
    bbar_re = coef_re[..., None] * b_re - coef_im[..., None] * b_im
    bbar_im = coef_re[..., None] * b_im + coef_im[..., None] * b_re
    return abar_re, abar_im, bbar_re, bbar_im


def _s5_blocks(abar_re, abar_im, bbar_re, bbar_im, c_re, c_im):
    eye = jnp.eye(GROUPS_PER_BLOCK, dtype=F32)
    bb = jnp.stack([bbar_re, bbar_im]).reshape(2, N_BLOCKS, GROUPS_PER_BLOCK, SSM_STATE, SSM_GROUP)
    b_blk = jnp.einsum('rqgph,gk->qghrkp', bb, eye).reshape(N_BLOCKS, 128, 2 * HALF)
    cc = jnp.stack([c_re, -c_im]).reshape(2, N_BLOCKS, GROUPS_PER_BLOCK, SSM_GROUP, SSM_STATE)
    c_blk = jnp.einsum('rqghp,gk->qrgpkh', cc, eye).reshape(N_BLOCKS, 2 * HALF, 128)
    abar = jnp.stack([abar_re.reshape(N_BLOCKS, HALF), abar_im.reshape(N_BLOCKS, HALF)])
    return abar, b_blk, c_blk


def _s5_unblock(d_b_blk, d_ct_blk):
    db = d_b_blk.reshape(N_BLOCKS, GROUPS_PER_BLOCK, SSM_GROUP, 2, GROUPS_PER_BLOCK, SSM_STATE)
    db = jnp.einsum('qghrgp->rqgph', db).reshape(2, SSM_GROUPS, SSM_STATE, SSM_GROUP)
    dc = d_ct_blk.reshape(N_BLOCKS, GROUPS_PER_BLOCK, SSM_GROUP, 2, GROUPS_PER_BLOCK, SSM_STATE)
    dc = jnp.einsum('qghrgp->rqghp', dc).reshape(2, SSM_GROUPS, SSM_GROUP, SSM_STATE)
    return db[0], db[1], dc[0], -dc[1]


def _scan_chunk_of_step(j, n_chunks, n_ctx_chunks, reverse):
    if not reverse:
        return j
    return jnp.where(j < n_ctx_chunks, n_ctx_chunks - 1 - j, n_chunks - 1 - j + n_ctx_chunks)


LANE_TILES = 2 * HALF // 128
RE_TILES = HALF // 128


def _tiles(v):
    return [v[:, l * 128:(l + 1) * 128] for l in range(v.shape[1] // 128)]


def _scatter_steps(s_ref, q, x):
    for l in range(LANE_TILES):
        s_ref[l, pl.ds(q, x.shape[0], stride=N_BLOCKS), :] = x[:, l * 128:(l + 1) * 128]


def _gather_steps(s_ref, q, n_steps):
    return jnp.concatenate([s_ref[l, pl.ds(q, n_steps, stride=N_BLOCKS), :] for l in range(LANE_TILES)], axis=1)


def _load_step(s_ref, t):
    row = pl.multiple_of(t * N_BLOCKS, N_BLOCKS)
    return [s_ref[l, pl.ds(row, N_BLOCKS), :] for l in range(LANE_TILES)]


def _store_step(s_ref, t, tiles):
    row = pl.multiple_of(t * N_BLOCKS, N_BLOCKS)
    for l in range(LANE_TILES):
        s_ref[l, pl.ds(row, N_BLOCKS), :] = tiles[l]


SCAN_UNROLL = 8
ADJOINT_UNROLL = 5


def _unrolled_loop(n_steps, unroll, step, carry):
    assert n_steps % unroll == 0

    def steps(i, c):
        for r in range(unroll):
            c = step(unroll * i + r, c)
        return c

    return lax.fori_loop(0, n_steps // unroll, steps, carry)


def _cmul_add(a, h, x, conj):
    re, im = [], []
    for l in range(RE_TILES):
        ar, ai, hr, hi = a[l], a[RE_TILES + l], h[l], h[RE_TILES + l]
        if conj:
            re.append(ar * hr + ai * hi + x[l])
            im.append(ar * hi - ai * hr + x[RE_TILES + l])
        else:
            re.append(ar * hr - ai * hi + x[l])
            im.append(ar * hi + ai * hr + x[RE_TILES + l])
    return re + im


def _s5_scan_fwd(u, abar, b_blk, c_blk, d_skip, n_ctx, reverse, rider=None):
    n_rows, width = u.shape
    tc = SCAN_CHUNK
    n_chunks, n_ctx_chunks = n_rows // tc, n_ctx // tc
    with_skip = d_skip is not None

    def body(*refs):
        if rider is not None:
            x_ref, ride_out, sems = refs[4 + with_skip], refs[7 + with_skip], refs[-3:]
            _ride(rider, pl.program_id(0) == 0, pl.program_id(0) == n_chunks - 1, (x_ref, ride_out) + tuple(sems))
            refs = refs[:4 + with_skip] + refs[5 + with_skip:7 + with_skip] + refs[8 + with_skip:-3]
        if with_skip:
            u_ref, a_ref, b_ref, c_ref, d_ref, y_ref, hb_ref, s_ref, h_ref = refs
        else:
            u_ref, a_ref, b_ref, c_ref, y_ref, hb_ref, s_ref, h_ref = refs
        j = pl.program_id(0)

        @pl.when(j == 0)
        def _():
            h_ref[...] = jnp.zeros_like(h_ref)

        hb_ref[0] = h_ref[...]
        for q in range(N_BLOCKS):
            _scatter_steps(s_ref, q, _dot(u_ref[:, q * 128:(q + 1) * 128], b_ref[q]))
        a = _tiles(a_ref[0]) + _tiles(a_ref[1])

        def step(s, h):
            t = tc - 1 - s if reverse else s
            h = _cmul_add(a, h, _load_step(s_ref, t), conj=False)
            _store_step(s_ref, t, h)
            return h

        h = _unrolled_loop(tc, SCAN_UNROLL, step, _tiles(h_ref[...]))
        h_ref[...] = jnp.concatenate(h, axis=1)
        for q in range(N_BLOCKS):
            yq = _dot(_gather_steps(s_ref, q, tc), c_ref[q])
            if with_skip:
                yq = yq + d_ref[:, q * 128:(q + 1) * 128] * u_ref[:, q * 128:(q + 1) * 128]
            y_ref[:, q * 128:(q + 1) * 128] = yq

    chunk = functools.partial(_scan_chunk_of_step, n_chunks=n_chunks, n_ctx_chunks=n_ctx_chunks, reverse=reverse)
    full3 = lambda j: (0, 0, 0)
    in_specs = [pl.BlockSpec((tc, width), lambda j: (chunk(j), 0)),
                pl.BlockSpec((2, N_BLOCKS, HALF), full3),
                pl.BlockSpec((N_BLOCKS, 128, 2 * HALF), full3),
                pl.BlockSpec((N_BLOCKS, 2 * HALF, 128), full3)]
    args = [u, abar, b_blk.astype(BF16), c_blk.astype(BF16)]
    if with_skip:
        in_specs.append(pl.BlockSpec((1, width), lambda j: (0, 0)))
        args.append(d_skip.reshape(1, width))
    out_specs = [pl.BlockSpec((tc, width), lambda j: (chunk(j), 0)),
                 pl.BlockSpec((1, N_BLOCKS, 2 * HALF), lambda j: (chunk(j), 0, 0))]
    out_shape = [_sds((n_rows, width)), _sds((n_chunks, N_BLOCKS, 2 * HALF))]
    scratch = [pltpu.VMEM((LANE_TILES, tc * N_BLOCKS, 128), F32), pltpu.VMEM((N_BLOCKS, 2 * HALF), F32)]
    if rider is not None:
        in_specs.append(HBM_SPEC)
        args.append(rider[0])
        out_specs.append(HBM_SPEC)
        out_shape.append(_exchange_out_shape(*rider))
        scratch += _exchange_semaphores()
    return pl.pallas_call(
        body, name="s5_scan_fwd_rev" if reverse else "s5_scan_fwd",
        grid=(n_chunks,), in_specs=in_specs, out_specs=out_specs, out_shape=out_shape, scratch_shapes=scratch,
        compiler_params=_cparams(1),
    )(*args)


def _s5_scan_bwd(u, dy, hb, abar, b_blk, c_blk, d_skip, n_ctx, reverse, rider=None):
    n_rows, width = u.shape
    tc = SCAN_CHUNK
    n_chunks, n_ctx_chunks = n_rows // tc, n_ctx // tc
    with_skip = d_skip is not None
    n_in, n_out = 7 + with_skip, 4 + with_skip

    def body(*refs):
        if rider is not None:
            x_ref, ride_out, sems = refs[n_in], refs[n_in + 1 + n_out], refs[-3:]
            _ride(rider, pl.program_id(0) == 0, pl.program_id(0) == n_chunks - 1, (x_ref, ride_out) + tuple(sems))
            refs = refs[:n_in] + refs[n_in + 1:n_in + 1 + n_out] + refs[n_in + 2 + n_out:-3]
        if with_skip:
            (u_ref, dy_ref, hb_ref, a_ref, b_ref, bt_ref, ct_ref, d_ref,
             du_ref, da_ref, db_ref, dct_ref, dd_ref, sh_ref, sg_ref, g_ref) = refs
        else:
            (u_ref, dy_ref, hb_ref, a_ref, b_ref, bt_ref, ct_ref,
             du_ref, da_ref, db_ref, dct_ref, sh_ref, sg_ref, g_ref) = refs
        j = pl.program_id(0)

        @pl.when(j == 0)
        def _():
            g_ref[...] = jnp.zeros_like(g_ref)
            da_ref[...] = jnp.zeros_like(da_ref)
            db_ref[...] = jnp.zeros_like(db_ref)
            dct_ref[...] = jnp.zeros_like(dct_ref)
            if with_skip:
                dd_ref[...] = jnp.zeros_like(dd_ref)

        for q in range(N_BLOCKS):
            _scatter_steps(sh_ref, q, _dot(u_ref[:, q * 128:(q + 1) * 128], b_ref[q]))
            _scatter_steps(sg_ref, q, _dot(dy_ref[:, q * 128:(q + 1) * 128], ct_ref[q]))
        a = _tiles(a_ref[0]) + _tiles(a_ref[1])
        time_of = (lambda s: tc - 1 - s) if reverse else (lambda s: s)

        def fwd_step(s, h):
            h = _cmul_add(a, h, _load_step(sh_ref, time_of(s)), conj=False)
            _store_step(sh_ref, time_of(s), h)
            return h

        h0 = _tiles(hb_ref[0])
        _unrolled_loop(tc, SCAN_UNROLL, fwd_step, h0)

        def adj(t, h_prev, carry):
            g, da = carry
            g = _cmul_add(a, g, _load_step(sg_ref, t), conj=True)
            _store_step(sg_ref, t, g)
            da_re = [da[l] + g[l] * h_prev[l] + g[RE_TILES + l] * h_prev[RE_TILES + l] for l in range(RE_TILES)]
            da_im = [da[RE_TILES + l] + g[RE_TILES + l] * h_prev[l] - g[l] * h_prev[RE_TILES + l] for l in range(RE_TILES)]
            return g, da_re + da_im

        def bwd_step(i, carry):
            s = tc - 1 - i
            return adj(time_of(s), _load_step(sh_ref, time_of(s - 1)), carry)

        carry = (_tiles(g_ref[...]), _tiles(da_ref[0]) + _tiles(da_ref[1]))
        carry = _unrolled_loop(tc - 1, ADJOINT_UNROLL, bwd_step, carry)
        g, da = adj(time_of(0), h0, carry)
        g_ref[...] = jnp.concatenate(g, axis=1)
        da_ref[0] = jnp.concatenate(da[:RE_TILES], axis=1)
        da_ref[1] = jnp.concatenate(da[RE_TILES:], axis=1)

        for q in range(N_BLOCKS):
            cols = slice(q * 128, (q + 1) * 128)
            uq, dyq = u_ref[:, cols], dy_ref[:, cols]
            gq = _gather_steps(sg_ref, q, tc)
            duq = _dot(gq, bt_ref[q])
            if with_skip:
                duq = duq + d_ref[:, cols] * dyq
                dd_ref[:, cols] += jnp.sum(dyq * uq, axis=0, keepdims=True)
            du_ref[:, cols] = duq
            db_ref[q] += _dot_t0(uq, gq)
            dct_ref[q] += _dot_t0(dyq, _gather_steps(sh_ref, q, tc))

    def chunk(j):
        return _scan_chunk_of_step(n_chunks - 1 - j, n_chunks, n_ctx_chunks, reverse)

    full2 = lambda j: (0, 0)
    full3 = lambda j: (0, 0, 0)
    row = pl.BlockSpec((tc, width), lambda j: (chunk(j), 0))
    in_specs = [row, row,
                pl.BlockSpec((1, N_BLOCKS, 2 * HALF), lambda j: (chunk(j), 0, 0)),
                pl.BlockSpec((2, N_BLOCKS, HALF), full3),
                pl.BlockSpec((N_BLOCKS, 128, 2 * HALF), full3),
                pl.BlockSpec((N_BLOCKS, 2 * HALF, 128), full3),
                pl.BlockSpec((N_BLOCKS, 128, 2 * HALF), full3)]
    args = [u, dy, hb, abar, b_blk.astype(BF16), jnp.swapaxes(b_blk, 1, 2).astype(BF16),
            jnp.swapaxes(c_blk, 1, 2).astype(BF16)]
    out_specs = [row,
                 pl.BlockSpec((2, N_BLOCKS, HALF), full3),
                 pl.BlockSpec((N_BLOCKS, 128, 2 * HALF), full3),
                 pl.BlockSpec((N_BLOCKS, 128, 2 * HALF), full3)]
    out_shape = [jax.ShapeDtypeStruct((n_rows, width), F32),
                 jax.ShapeDtypeStruct((2, N_BLOCKS, HALF), F32),
                 jax.ShapeDtypeStruct((N_BLOCKS, 128, 2 * HALF), F32),
                 jax.ShapeDtypeStruct((N_BLOCKS, 128, 2 * HALF), F32)]
    if with_skip:
        in_specs.append(pl.BlockSpec((1, width), full2))
        args.append(d_skip.reshape(1, width))
        out_specs.append(pl.BlockSpec((1, width), full2))
        out_shape.append(jax.ShapeDtypeStruct((1, width), F32))
    scratch = [pltpu.VMEM((LANE_TILES, tc * N_BLOCKS, 128), F32), pltpu.VMEM((LANE_TILES, tc * N_BLOCKS, 128), F32),
               pltpu.VMEM((N_BLOCKS, 2 * HALF), F32)]
    if rider is not None:
        in_specs.append(HBM_SPEC)
        args.append(rider[0])
        out_specs.append(HBM_SPEC)
        out_shape.append(_exchange_out_shape(*rider))
        scratch += _exchange_semaphores()
    return pl.pallas_call(
        body, name="s5_scan_bwd_rev" if reverse else "s5_scan_bwd",
        grid=(n_chunks,), in_specs=in_specs, out_specs=out_specs, out_shape=out_shape, scratch_shapes=scratch,
        compiler_params=_cparams(1),
    )(*args)


def _s5_dir_params(d, a_re, a_im, log_dt, b_re, b_im):
    return a_re[d], a_im[d], log_dt[d], b_re[d], b_im[d]


def _s5_forward(u, ssm, n_ctx, riders=(None, None)):
    a_re, a_im, log_dt, b_re, b_im, c_re, c_im, d_skip = ssm
    outs, saved, carried = [], [], [None, None]
    for d in range(2):
        prep = _s5_prep(*_s5_dir_params(d, a_re, a_im, log_dt, b_re, b_im))
        abar, b_blk, c_blk = _s5_blocks(*prep, c_re[d], c_im[d])
        res = _s5_scan_fwd(u, abar, b_blk, c_blk, d_skip if d == 0 else None, n_ctx, reverse=(d == 1), rider=riders[d])
        if riders[d] is not None:
            carried[d] = res[2]
        outs.append(res[0])
        saved.append((res[1], abar, b_blk, c_blk))
    return outs, saved, carried


def _s5_backward(u, dy, ssm, saved, n_ctx, riders=(None, None)):
    a_re, a_im, log_dt, b_re, b_im, c_re, c_im, d_skip = ssm
    dus, grads = [], [[] for _ in range(7)]
    d_d, carried = None, [None, None]
    for d in range(2):
        hb, abar, b_blk, c_blk = saved[d]
        res = _s5_scan_bwd(u, dy, hb, abar, b_blk, c_blk, d_skip if d == 0 else None, n_ctx, reverse=(d == 1),
                           rider=riders[d])
        if riders[d] is not None:
            carried[d], res = res[-1], res[:-1]
        if d == 0:
            du, d_abar, d_b_blk, d_ct_blk, d_d = res
        else:
            du, d_abar, d_b_blk, d_ct_blk = res
        dus.append(du)
        dbb_re, dbb_im, dc_re, dc_im = _s5_unblock(d_b_blk, d_ct_blk)
        _, vjp = jax.vjp(_s5_prep, *_s5_dir_params(d, a_re, a_im, log_dt, b_re, b_im))
        shape = (SSM_GROUPS, SSM_STATE)
        g5 = vjp((d_abar[0].reshape(shape), d_abar[1].reshape(shape), dbb_re, dbb_im))
        for k, g in enumerate(tuple(g5) + (dc_re, dc_im)):
            grads[k].append(g)
    grads = [jnp.stack(g) for g in grads]
    return dus, grads + [d_d.reshape(-1)], carried


INV_SQRT2 = 0.7071067811865476
INV_SQRT_2PI = 0.3989422804014327


def _rows(cols):
    return pl.BlockSpec((ROW_TILE, cols), lambda i: (i, 0))


def _rows_skip_ctx(cols):
    return pl.BlockSpec((ROW_TILE, cols), lambda i: (i + 1, 0))


def _rows_lat(cols):
    return pl.BlockSpec((ROW_TILE, cols), lambda i: (jnp.maximum(i - 1, 0), 0))


def _full(shape):
    nd = len(shape)
    return pl.BlockSpec(shape, lambda i: (0,) * nd)


def _seg(cols):
    return pl.BlockSpec((1, 1, cols), lambda i: (jnp.minimum(i, 1), 0, 0))


def _lat_seg(cols):
    return pl.BlockSpec((1, 1, cols), lambda i: (1, 0, 0))


def _sds(shape, dtype=F32):
    return jax.ShapeDtypeStruct(shape, dtype)


def _sum0(x):
    return jnp.sum(x, axis=0, keepdims=True)


def _sigmoid(x):
    return jax.nn.sigmoid(x)


def _rms_mod(x, g, scale, shift):
    r = lax.rsqrt(jnp.mean(x * x, axis=-1, keepdims=True) + NORM_EPS)
    return (x * r * g) * (1.0 + scale) + shift


def _rms_mod_bwd(x, g, scale, dh):
    r = lax.rsqrt(jnp.mean(x * x, axis=-1, keepdims=True) + NORM_EPS)
    n = x * r
    dyg = dh * (1.0 + scale)
    dn = dyg * g
    dx = r * (dn - n * jnp.mean(dn * n, axis=-1, keepdims=True))
    return dx, _sum0(dyg * n), _sum0(dh * (n * g)), _sum0(dh)


def _head_of_lane(width):
    return (jnp.arange(width)[:, None] // HEAD_DIM == jnp.arange(128)[None, :]).astype(BF16)


def _split_dot(t, w, transposed):
    hi = t.astype(BF16)
    lo = (t - hi.astype(F32)).astype(BF16)
    f = _dot_t1 if transposed else _dot
    return f(hi, w) + f(lo, w)


def _head_sums(t, hl):
    return _split_dot(_split_dot(t, hl, False), hl, True)


def _rope_partner(x):
    n = x.shape[1]
    lane = lax.broadcasted_iota(jnp.int32, x.shape, 1)
    return jnp.where((lane & 16) == 0, pltpu.roll(x, n - 16, 1), pltpu.roll(x, 16, 1))


def _lanes(tab, width):
    return jnp.tile(tab, (1, width // tab.shape[1]))


def _head_norm_rope(x, gain, cos, sin, hl):
    r = lax.rsqrt(_head_sums(x * x, hl) * (1.0 / HEAD_DIM) + NORM_EPS)
    y = x * r * gain
    return y * cos + _rope_partner(y) * sin


def _head_norm_rope_bwd(x, gain, cos, sin, hl, dout):
    dy = dout * cos + _rope_partner(dout * sin)
    r = lax.rsqrt(_head_sums(x * x, hl) * (1.0 / HEAD_DIM) + NORM_EPS)
    n = x * r
    dn = dy * gain
    dx = r * (dn - n * (_head_sums(dn * n, hl) * (1.0 / HEAD_DIM)))
    return dx, _sum0(dy * n)


def _rope_tables(n_ctx, n_lat):
    t = jnp.arange(n_lat)
    pos = jnp.stack([(t // GRID_W).astype(F32), (t % GRID_W).astype(F32)], axis=1)
    n_freq = HEAD_DIM // 4
    freqs = ROPE_THETA ** (-jnp.arange(n_freq, dtype=F32) / n_freq)
    ang = pos[:, :, None] * freqs[None, None, :]
    cos = jnp.repeat(jnp.cos(ang)[:, :, None, :], 2, axis=2).reshape(n_lat, HEAD_DIM)
    sin = jnp.sin(ang)
    sin = jnp.stack([-sin, sin], axis=2).reshape(n_lat, HEAD_DIM)
    cos = jnp.concatenate([jnp.ones((n_ctx, HEAD_DIM), F32), cos], axis=0)
    sin = jnp.concatenate([jnp.zeros((n_ctx, HEAD_DIM), F32), sin], axis=0)
    return jnp.tile(cos, (1, 2)), jnp.tile(sin, (1, 2))


def _ssm_in(xa, g, scale, shift, w_in):
    n_rows, d = xa.shape
    e = w_in.shape[1] // 2

    def body(x_ref, g_ref, sc_ref, sh_ref, w_ref, u_ref, z_ref):
        h = _rms_mod(x_ref[...], g_ref[...], sc_ref[0], sh_ref[0])
        proj = _dot(h, w_ref[...])
        u_ref[...] = proj[:, :e]
        z_ref[...] = proj[:, e:]

    return pl.pallas_call(
        body, name="ssm_in", grid=(n_rows // ROW_TILE,),
        in_specs=[_rows(d), _full((1, d)), _seg(d), _seg(d), _full(w_in.shape)],
        out_specs=[_rows(e), _rows(e)], out_shape=[_sds((n_rows, e)), _sds((n_rows, e))],
        compiler_params=_cparams(1),
    )(xa, g, scale, shift, w_in)


def _s5_post_math(y, z, w_glu, b_glu, w_out):
    er = lax.erf(y * INV_SQRT2)
    g = 0.5 * y * (1.0 + er)
    sg = _sigmoid(_dot(g, w_glu) + b_glu)
    g2 = g * sg
    sz = _sigmoid(z)
    silu_z = z * sz
    m = g2 * silu_z
    return er, g, sg, g2, sz, silu_z, m, _dot(m, w_out)


def _ssm_post(xa, y0, y1, z, gate, w_glu, b_glu, w_out):
    n_rows, d = xa.shape
    e = z.shape[1]

    def body(x_ref, y0_ref, y1_ref, z_ref, gt_ref, wg_ref, bg_ref, wo_ref, o_ref):
        out = _s5_post_math(y0_ref[...] + y1_ref[...], z_ref[...], wg_ref[...], bg_ref[...], wo_ref[...])[-1]
        o_ref[...] = x_ref[...] + gt_ref[0] * out

    return pl.pallas_call(
        body, name="ssm_post", grid=(n_rows // ROW_TILE,),
        in_specs=[_rows(d), _rows(e), _rows(e), _rows(e), _seg(d), _full(w_glu.shape), _full((1, e)), _full(w_out.shape)],
        out_specs=_rows(d), out_shape=_sds((n_rows, d)),
        compiler_params=_cparams(1),
    )(xa, y0, y1, z, gate, w_glu, b_glu, w_out)


def _init_acc(first, *refs):
    @pl.when(first)
    def _():
        for r in refs:
            r[...] = jnp.zeros_like(r)


def _ssm_post_bwd(dxa, y0, y1, z, gate, w_glu, b_glu, w_out, w_glu_t, w_out_t):
    n_rows, d = dxa.shape
    e = z.shape[1]

    def body(dx_ref, y0_ref, y1_ref, z_ref, gt_ref, wg_ref, bg_ref, wo_ref, wgt_ref, wot_ref,
             dy_ref, dz_ref, dgt_ref, dwo_ref, dwg_ref, dbg_ref):
        i = pl.program_id(0)
        _init_acc(i == 0, dwo_ref, dwg_ref, dbg_ref)
        _init_acc(i <= 1, dgt_ref)
        y, zz = y0_ref[...] + y1_ref[...], z_ref[...]
        er, g, sg, g2, sz, silu_z, m, out = _s5_post_math(y, zz, wg_ref[...], bg_ref[...], wo_ref[...])
        dxa_t = dx_ref[...]
        dgt_ref[0] += _sum0(dxa_t * out)
        dout = gt_ref[0] * dxa_t
        dm = _dot(dout, wot_ref[...])
        dwo_ref[...] += _dot_t0(m, dout)
        dg2 = dm * silu_z
        dz_ref[...] = dm * g2 * (sz * (1.0 + zz * (1.0 - sz)))
        dt = dg2 * g * sg * (1.0 - sg)
        dwg_ref[...] += _dot_t0(g, dt)
        dbg_ref[...] += _sum0(dt)
        dg = dg2 * sg + _dot(dt, wgt_ref[...])
        dy_ref[...] = dg * (0.5 * (1.0 + er) + y * jnp.exp(-0.5 * y * y) * INV_SQRT_2PI)

    return pl.pallas_call(
        body, name="ssm_post_bwd", grid=(n_rows // ROW_TILE,),
        in_specs=[_rows(d), _rows(e), _rows(e), _rows(e), _seg(d), _full(w_glu.shape), _full((1, e)), _full(w_out.shape),
                  _full(w_glu_t.shape), _full(w_out_t.shape)],
        out_specs=[_rows(e), _rows(e), _seg(d), _full(w_out.shape), _full(w_glu.shape), _full((1, e))],
        out_shape=[_sds((n_rows, e)), _sds((n_rows, e)), _sds((2, 1, d)), _sds(w_out.shape), _sds(w_glu.shape), _sds((1, e))],
        compiler_params=_cparams(1),
    )(dxa, y0, y1, z, gate, w_glu, b_glu, w_out, w_glu_t, w_out_t)


def _ssm_in_bwd(du0, du1, dz, xa, dxa_next, g, scale, shift, w_in_t):
    n_rows, d = xa.shape
    e = dz.shape[1]
    n_lat = n_rows - ROW_TILE

    def body(du0_ref, du1_ref, dz_ref, x_ref, dn_ref, g_ref, sc_ref, sh_ref, wt_ref,
             gx_ref, dw_ref, dg_ref, dsc_ref, dsh_ref):
        i = pl.program_id(0)
        _init_acc(i == 0, dw_ref, dg_ref)
        _init_acc(i <= 1, dsc_ref, dsh_ref)
        x = x_ref[...]
        h = _rms_mod(x, g_ref[...], sc_ref[0], sh_ref[0])
        dproj = jnp.concatenate([du0_ref[...] + du1_ref[...], dz_ref[...]], axis=1)
        dh = _dot(dproj, wt_ref[...])
        dw_ref[...] += _dot_t0(h, dproj)
        dx, dg, dsc, dsh = _rms_mod_bwd(x, g_ref[...], sc_ref[0], dh)
        dg_ref[...] += dg
        dsc_ref[0] += dsc
        dsh_ref[0] += dsh
        gx_ref[...] = dn_ref[...] + dx

    return pl.pallas_call(
        body, name="ssm_in_bwd", grid=(n_rows // ROW_TILE,),
        in_specs=[_rows(e), _rows(e), _rows(e), _rows(d), _rows(d), _full((1, d)), _seg(d), _seg(d), _full(w_in_t.shape)],
        out_specs=[_rows_lat(d), _full((d, 2 * e)), _full((1, d)), _seg(d), _seg(d)],
        out_shape=[_sds((n_lat, d)), _sds((d, 2 * e)), _sds((1, d)), _sds((2, 1, d)), _sds((2, 1, d))],
        compiler_params=_cparams(1),
    )(du0, du1, dz, xa, dxa_next, g, scale, shift, w_in_t)


Q_WIDTH = N_Q_HEADS * HEAD_DIM
SM_SCALE = 1.0 / math.sqrt(HEAD_DIM)


def _attn_in(xa, g, scale, shift, w_in, q_gain, k_gain, cos, sin):
    n_rows, d = xa.shape
    qk = Q_WIDTH + KV_WIDTH

    def body(x_ref, g_ref, sc_ref, sh_ref, w_ref, qg_ref, kg_ref, cos_ref, sin_ref, hq_ref, hk_ref,
             q_ref, k_ref, v_ref, z_ref, raw_ref):
        h = _rms_mod(x_ref[...], g_ref[...], sc_ref[0], sh_ref[0])
        proj = _dot(h, w_ref[...])
        q_raw, k_raw = proj[:, :Q_WIDTH], proj[:, Q_WIDTH:qk]
        cos, sin = cos_ref[...], sin_ref[...]
        q = _head_norm_rope(q_raw, qg_ref[...], _lanes(cos, Q_WIDTH), _lanes(sin, Q_WIDTH), hq_ref[...])
        k = _head_norm_rope(k_raw, kg_ref[...], _lanes(cos, KV_WIDTH), _lanes(sin, KV_WIDTH), hk_ref[...])
        q_ref[...] = (q * SM_SCALE).astype(BF16)
        k_ref[...] = k.astype(BF16)
        v_ref[...] = proj[:, qk:qk + KV_WIDTH].astype(BF16)
        z_ref[...] = proj[:, qk + KV_WIDTH:]
        raw_ref[...] = proj[:, :qk]

    return pl.pallas_call(
        body, name="attn_in", grid=(n_rows // ROW_TILE,),
        in_specs=[_rows(d), _full((1, d)), _seg(d), _seg(d), _full(w_in.shape), _full((1, Q_WIDTH)), _full((1, KV_WIDTH)),
                  _rows(128), _rows(128), _full((Q_WIDTH, 128)), _full((KV_WIDTH, 128))],
        out_specs=[_rows_lat(Q_WIDTH), _rows(KV_WIDTH), _rows(KV_WIDTH), _rows(Q_WIDTH), _rows(qk)],
        out_shape=[_sds((n_rows - ROW_TILE, Q_WIDTH), BF16), _sds((n_rows, KV_WIDTH), BF16), _sds((n_rows, KV_WIDTH), BF16),
                   _sds((n_rows, Q_WIDTH)), _sds((n_rows, qk))],
        compiler_params=_cparams(1),
    )(xa, g, scale, shift, w_in, q_gain, k_gain, cos, sin, _head_of_lane(Q_WIDTH), _head_of_lane(KV_WIDTH))


GROUP_WIDTH = KV_REP * HEAD_DIM


def _stack_heads(ref):
    return jnp.concatenate([ref[:, h * HEAD_DIM:(h + 1) * HEAD_DIM] for h in range(KV_REP)], axis=0)


def _unstack_heads(a_t, tq):
    return jnp.concatenate([a_t[:, h * tq:(h + 1) * tq].T for h in range(KV_REP)], axis=1)


def _kv_tile(n_keys):
    return 768 if n_keys % 768 == 0 else 256


def _kv_tile_fwd(n_keys):
    return 1408 if n_keys % 1408 == 0 else _kv_tile(n_keys)


def _q_tile(n_lat):
    return 512 if n_lat % 512 == 0 else 256


def _flash_fwd(q, k, v_t):
    n_lat = q.shape[0]
    tq = _q_tile(n_lat)
    rows = KV_REP * tq
    n_kv, tk, n_q = k.shape[1], k.shape[2], n_lat // tq

    v_rows = v_t.shape[2]

    def body(q_ref, k_ref, vt_ref, o_ref, lse_ref):
        q = _stack_heads(q_ref)

        def step(j, carry):
            m_prev, acc = carry
            s_t = _dot_t1(k_ref[0, j], q)
            m_new = jnp.maximum(m_prev, jnp.max(s_t, axis=0, keepdims=True))
            alpha = jnp.exp(m_prev - m_new)
            p_t = jnp.exp(s_t - m_new)
            return m_new, alpha * acc + _dot(vt_ref[0, j], p_t)

        init = (jnp.full((1, rows), -jnp.inf, F32), jnp.zeros((v_rows, rows), F32))
        m, acc = lax.fori_loop(0, n_kv, step, init)
        l = acc[HEAD_DIM:HEAD_DIM + 1]
        o_ref[...] = _unstack_heads(acc[:HEAD_DIM] / l, tq)
        lse_ref[0, 0] = m + jnp.log(l)

    kv_all = lambda a: pl.BlockSpec((1,) + a.shape[1:], lambda g, i: (g, 0, 0, 0))
    return pl.pallas_call(
        body, name="flash_fwd", grid=(N_KV_HEADS, n_q),
        in_specs=[pl.BlockSpec((tq, GROUP_WIDTH), lambda g, i: (i, g)), kv_all(k), kv_all(v_t)],
        out_specs=[pl.BlockSpec((tq, GROUP_WIDTH), lambda g, i: (i, g)),
                   pl.BlockSpec((1, 1, 1, rows), lambda g, i: (g, i, 0, 0))],
        out_shape=[_sds((n_lat, Q_WIDTH)), _sds((N_KV_HEADS, n_q, 1, rows))],
        compiler_params=_cparams(2),
    )(q, k, v_t)


def _flash_bwd(q, k, k_t, v, do, lse_t, delta_t):
    n_lat = q.shape[0]
    tq = _q_tile(n_lat)
    rows = KV_REP * tq
    n_kv, tk, n_q = k.shape[1], k.shape[2], n_lat // tq

    def body(q_ref, k_ref, kt_ref, v_ref, do_ref, lse_ref, dl_ref, dq_ref, dk_ref, dv_ref):
        _init_acc(pl.program_id(1) == 0, dk_ref, dv_ref)
        q, do = _stack_heads(q_ref), _stack_heads(do_ref)
        lse, delta = lse_ref[0, 0], dl_ref[0, 0]

        def step(j, dq_acc):
            p_t = jnp.exp(_dot_t1(k_ref[0, j], q) - lse)
            dv_ref[0, j] += _dot(p_t, do)
            ds_t = p_t * (_dot_t1(v_ref[0, j], do) - delta)
            dk_ref[0, j] += _dot(ds_t, q)
            return dq_acc + _dot(kt_ref[0, j], ds_t)

        dq = lax.fori_loop(0, n_kv, step, jnp.zeros((HEAD_DIM, rows), F32))
        dq_ref[...] = _unstack_heads(dq, tq)

    qspec = pl.BlockSpec((tq, GROUP_WIDTH), lambda g, i: (i, g))
    rowspec = pl.BlockSpec((1, 1, 1, rows), lambda g, i: (g, i, 0, 0))
    kv_all = lambda a: pl.BlockSpec((1,) + a.shape[1:], lambda g, i: (g, 0, 0, 0))
    return pl.pallas_call(
        body, name="flash_bwd", grid=(N_KV_HEADS, n_q),
        in_specs=[qspec, kv_all(k), kv_all(k_t), kv_all(v), qspec, rowspec, rowspec],
        out_specs=[qspec, kv_all(k), kv_all(k)],
        out_shape=[_sds((n_lat, Q_WIDTH)), _sds(k.shape), _sds(k.shape)],
        compiler_params=_cparams(2),
    )(q, k, k_t, v, do, lse_t, delta_t)


def _to_lane_stacked(a, tq):
    n_lat = a.shape[0]
    a = a.reshape(n_lat // tq, tq, N_KV_HEADS, KV_REP).transpose(2, 0, 3, 1)
    return a.reshape(N_KV_HEADS, n_lat // tq, 1, KV_REP * tq)


def _attn_post_loss(o, z, xa, gate, w_out, w_out_t, final_g, target):
    n_lat, d = target.shape
    e = o.shape[1]
    head_of_lane = (jnp.arange(e)[:, None] // HEAD_DIM == jnp.arange(128)[None, :]).astype(BF16)

    def body(o_ref, z_ref, x_ref, gt_ref, w_ref, wt_ref, fg_ref, tg_ref, hl_ref,
             do_ref, dl_ref, dz_ref, dx_ref, loss_ref, dfg_ref, dgt_ref, dw_ref):
        _init_acc(pl.program_id(0) == 0, loss_ref, dfg_ref, dgt_ref, dw_ref)
        oo, zz, gate_t, fg = o_ref[...], z_ref[...], gt_ref[0], fg_ref[...]
        sz = _sigmoid(zz)
        silu_z = zz * sz
        m = oo * silu_z
        out = _dot(m, w_ref[...])
        x2 = x_ref[...] + gate_t * out
        r = lax.rsqrt(jnp.mean(x2 * x2, axis=-1, keepdims=True) + NORM_EPS)
        n = x2 * r
        err = n * fg - tg_ref[...]
        loss_ref[...] += 0.5 * jnp.sum(jnp.mean(err * err, axis=-1, keepdims=True), axis=0, keepdims=True)
        dy = err * (1.0 / d)
        dfg_ref[...] += _sum0(dy * n)
        dn = dy * fg
        dx2 = r * (dn - n * jnp.mean(dn * n, axis=-1, keepdims=True))
        dx_ref[...] = dx2
        dgt_ref[...] += _sum0(dx2 * out)
        dout = gate_t * dx2
        dw_ref[...] += _dot_t0(m, dout)
        dm = _dot(dout, wt_ref[...])
        do = dm * silu_z
        do_ref[...] = do.astype(BF16)
        prod = do * oo
        hi = prod.astype(BF16)
        lo = (prod - hi.astype(F32)).astype(BF16)
        dl_ref[...] = _dot(hi, hl_ref[...]) + _dot(lo, hl_ref[...])
        dz_ref[...] = dm * oo * (sz * (1.0 + zz * (1.0 - sz)))

    return pl.pallas_call(
        body, name="attn_post_loss", grid=(n_lat // ROW_TILE,),
        in_specs=[_rows(e), _rows_skip_ctx(e), _rows_skip_ctx(d), _lat_seg(d), _full(w_out.shape), _full(w_out_t.shape),
                  _full((1, d)), _rows(d), _full((e, 128))],
        out_specs=[_rows(e), _rows(128), _rows(e), _rows(d), _full((1, 1)), _full((1, d)), _full((1, d)), _full(w_out.shape)],
        out_shape=[_sds((n_lat, e), BF16), _sds((n_lat, 128)), _sds((n_lat, e)), _sds((n_lat, d)), _sds((1, 1)), _sds((1, d)),
                   _sds((1, d)), _sds(w_out.shape)],
        compiler_params=_cparams(1),
    )(o, z, xa, gate, w_out, w_out_t, final_g, target, head_of_lane)


def _attn_in_bwd(dq, dk, dv, dz, raw, xa, dx2, g, scale, shift, q_gain, k_gain, cos, sin, w_in_t):
    n_rows, d = xa.shape
    qk = Q_WIDTH + KV_WIDTH
    n_in = w_in_t.shape[0]

    def body(dq_ref, dk_ref, dv_ref, dz_ref, raw_ref, x_ref, dx2_ref, g_ref, sc_ref, sh_ref, qg_ref, kg_ref, cos_ref, sin_ref,
             wt_ref, hq_ref, hk_ref, dxa_ref, dw_ref, dqg_ref, dkg_ref, dg_ref, dsc_ref, dsh_ref):
        i = pl.program_id(0)
        _init_acc(i == 0, dw_ref, dqg_ref, dkg_ref, dg_ref)
        _init_acc(i <= 1, dsc_ref, dsh_ref)
        is_lat = (i > 0).astype(F32)
        x = x_ref[...]
        h = _rms_mod(x, g_ref[...], sc_ref[0], sh_ref[0])
        cos, sin = cos_ref[...], sin_ref[...]
        raw_t = raw_ref[...]
        dq_raw, dqg = _head_norm_rope_bwd(raw_t[:, :Q_WIDTH], qg_ref[...], _lanes(cos, Q_WIDTH), _lanes(sin, Q_WIDTH),
                                          hq_ref[...], dq_ref[...] * (SM_SCALE * is_lat))
        dk_raw, dkg = _head_norm_rope_bwd(raw_t[:, Q_WIDTH:], kg_ref[...], _lanes(cos, KV_WIDTH), _lanes(sin, KV_WIDTH),
                                          hk_ref[...], dk_ref[...])
        dqg_ref[...] += dqg
        dkg_ref[...] += dkg
        dproj = jnp.concatenate([dq_raw, dk_raw, dv_ref[...], dz_ref[...] * is_lat], axis=1)
        dh = _dot(dproj, wt_ref[...])
        dw_ref[...] += _dot_t0(h, dproj)
        dx, dg, dsc, dsh = _rms_mod_bwd(x, g_ref[...], sc_ref[0], dh)
        dg_ref[...] += dg
        dsc_ref[0] += dsc
        dsh_ref[0] += dsh
        dxa_ref[...] = dx + dx2_ref[...] * is_lat

    return pl.pallas_call(
        body, name="attn_in_bwd", grid=(n_rows // ROW_TILE,),
        in_specs=[_rows_lat(Q_WIDTH), _rows(KV_WIDTH), _rows(KV_WIDTH), _rows_lat(Q_WIDTH), _rows(qk), _rows(d), _rows_lat(d),
                  _full((1, d)), _seg(d), _seg(d), _full((1, Q_WIDTH)), _full((1, KV_WIDTH)), _rows(128), _rows(128),
                  _full(w_in_t.shape), _full((Q_WIDTH, 128)), _full((KV_WIDTH, 128))],
        out_specs=[_rows(d), _full((d, n_in)), _full((1, Q_WIDTH)), _full((1, KV_WIDTH)), _full((1, d)), _seg(d), _seg(d)],
        out_shape=[_sds((n_rows, d)), _sds((d, n_in)), _sds((1, Q_WIDTH)), _sds((1, KV_WIDTH)), _sds((1, d)),
                   _sds((2, 1, d)), _sds((2, 1, d))],
        compiler_params=_cparams(1),
    )(dq, dk, dv, dz, raw, xa, dx2, g, scale, shift, q_gain, k_gain, cos, sin, w_in_t,
      _head_of_lane(Q_WIDTH), _head_of_lane(KV_WIDTH))


def _heads_major(a, n_heads):
    return a.reshape(a.shape[0], n_heads, HEAD_DIM).transpose(1, 0, 2)


def _tokens_major(a):
    return a.transpose(1, 0, 2).reshape(a.shape[1], a.shape[0] * HEAD_DIM)


def _local_step(x, ctx, target, mods, norm_g, ssm, w_ssm_in, w_glu, b_glu, w_ssm_out, w_attn_in, q_norm, k_norm, w_attn_out,
                final_g, attn_exchange=None, post_exchange=None):
    n_ctx, d = ctx.shape
    assert n_ctx == ROW_TILE
    n_lat = x.shape[0]
    (shift0, scale0, gate0), (shift1, scale1, gate1) = mods
    g0, g1, fg = norm_g[0:1], norm_g[1:2], final_g.reshape(1, d)
    b_glu = b_glu.reshape(1, -1)
    q_gain = jnp.tile(q_norm.reshape(1, HEAD_DIM), (1, N_Q_HEADS))
    k_gain = jnp.tile(k_norm.reshape(1, HEAD_DIM), (1, N_KV_HEADS))
    cos, sin = _rope_tables(n_ctx, n_lat)

    xa0 = jnp.concatenate([ctx, x], axis=0)
    u, z0 = _ssm_in(xa0, g0, scale0, shift0, w_ssm_in)
    gather = lambda ex: (ex[0], False) if ex else None
    (y0, y1), saved, gathered = _s5_forward(u, ssm, n_ctx, (gather(attn_exchange), gather(post_exchange)))
    if attn_exchange:
        w_attn_in, w_attn_out = attn_exchange[1](gathered[0])
    if post_exchange:
        w_glu, w_ssm_out = post_exchange[1](gathered[1])
    xa1 = _ssm_post(xa0, y0, y1, z0, gate0, w_glu, b_glu, w_ssm_out)

    q, k, v, z1, raw = _attn_in(xa1, g1, scale1, shift1, w_attn_in, q_gain, k_gain, cos, sin)
    tq, tk, tk_fwd = _q_tile(n_lat), _kv_tile(n_ctx + n_lat), _kv_tile_fwd(n_ctx + n_lat)
    k_h, v_h = _heads_major(k, N_KV_HEADS), _heads_major(v, N_KV_HEADS)
    k_b, v_b = k_h.reshape(N_KV_HEADS, -1, tk, HEAD_DIM), v_h.reshape(N_KV_HEADS, -1, tk, HEAD_DIM)
    v_t = v_h.reshape(N_KV_HEADS, -1, tk_fwd, HEAD_DIM).transpose(0, 1, 3, 2)
    v_t_ones = jnp.concatenate([v_t, jnp.ones(v_t.shape[:2] + (16, tk_fwd), BF16)], axis=2)
    o, lse_t = _flash_fwd(q, k_h.reshape(N_KV_HEADS, -1, tk_fwd, HEAD_DIM), v_t_ones)
    do, delta, dz1, dx2, loss, d_fg, d_gate1, d_w_attn_out = _attn_post_loss(
        o, z1, xa1, gate1, w_attn_out, w_attn_out.T, fg, target)

    dq, dk_b, dv_b = _flash_bwd(q, k_b, k_b.transpose(0, 1, 3, 2), v_b, do, lse_t, _to_lane_stacked(delta[:, :N_Q_HEADS], tq))
    keys_major = lambda a: _tokens_major(a.reshape(N_KV_HEADS, -1, HEAD_DIM))
    dxa1, d_w_attn_in, d_qg, d_kg, d_g1, d_scale1, d_shift1 = _attn_in_bwd(
        dq, keys_major(dk_b), keys_major(dv_b), dz1, raw, xa1, dx2, g1, scale1, shift1,
        q_gain, k_gain, cos, sin, w_attn_in.T)
    dy, dz0, d_gate0, d_w_ssm_out, d_w_glu, d_b_glu = _ssm_post_bwd(
        dxa1, y0, y1, z0, gate0, w_glu, b_glu, w_ssm_out, w_glu.T, w_ssm_out.T)
    scatter = lambda ex, *g: (ex[2](*g), True) if ex else None
    (du0, du1), d_ssm, parts = _s5_backward(
        u, dy, ssm, saved, n_ctx,
        (scatter(attn_exchange, d_w_attn_in, d_w_attn_out), scatter(post_exchange, d_w_glu, d_w_ssm_out)))
    grad_x, d_w_ssm_in, d_g0, d_scale0, d_shift0 = _ssm_in_bwd(du0, du1, dz0, xa0, dxa1, g0, scale0, shift0, w_ssm_in.T)

    d_gate1_seg = jnp.concatenate([jnp.zeros((1, 1, d), F32), d_gate1.reshape(1, 1, d)], axis=0)
    grads = dict(
        norm_g=jnp.concatenate([d_g0, d_g1], axis=0), ssm_w_in=d_w_ssm_in, ssm=d_ssm, ssm_b_glu=d_b_glu.reshape(-1),
        attn_q_norm=d_qg.reshape(N_Q_HEADS, HEAD_DIM).sum(0), attn_k_norm=d_kg.reshape(N_KV_HEADS, HEAD_DIM).sum(0),
        final_norm_g=d_fg.reshape(-1))
    if attn_exchange:
        grads.update(attn_parts=parts[0])
    else:
        grads.update(attn_w_in=d_w_attn_in, attn_w_out=d_w_attn_out)
    if post_exchange:
        grads.update(post_parts=parts[1])
    else:
        grads.update(ssm_w_glu=d_w_glu, ssm_w_out=d_w_ssm_out)
    d_mods = ((d_shift0, d_scale0, d_gate0), (d_shift1, d_scale1, d_gate1_seg))
    return loss[0, 0], grad_x, grads, d_mods


def _my_index():
    return 4 * lax.axis_index("x") + 2 * lax.axis_index("y") + lax.axis_index("c")


def _peer(k):
    mx, my, mc = lax.axis_index("x"), lax.axis_index("y"), lax.axis_index("c")
    px = 1 - mx if k & 4 else mx
    py = 1 - my if k & 2 else my
    pc = 1 - mc if k & 1 else mc
    return (px, py, pc), 4 * px + 2 * py + pc


HBM_SPEC = pl.BlockSpec(memory_space=pl.ANY)


def _exchange(x, name, all_to_all):
    def body(x_ref, out_ref, send_sems, recv_sems, local_sem):
        _exchange_copies(all_to_all, x_ref, out_ref, send_sems, recv_sems, local_sem, start=True)
        _exchange_copies(all_to_all, x_ref, out_ref, send_sems, recv_sems, local_sem, start=False)

    return pl.pallas_call(
        body, name=name, in_specs=[HBM_SPEC], out_specs=HBM_SPEC,
        out_shape=_exchange_out_shape(x, all_to_all), scratch_shapes=_exchange_semaphores(),
    )(x)


def _exchange_out_shape(x, all_to_all):
    return _sds((N_DEV,) + tuple(x.shape[1:] if all_to_all else x.shape), x.dtype)


def _exchange_semaphores():
    return [pltpu.SemaphoreType.DMA((N_DEV - 1,)), pltpu.SemaphoreType.DMA((N_DEV - 1,)), pltpu.SemaphoreType.DMA]


def _exchange_copies(all_to_all, x_ref, out_ref, send_sems, recv_sems, local_sem, start):
    me = _my_index()
    mine = pltpu.make_async_copy(x_ref.at[me] if all_to_all else x_ref, out_ref.at[me], local_sem)
    if start:
        mine.start()
    for k in range(1, N_DEV):
        peer, peer_idx = _peer(k)
        send = pltpu.make_async_remote_copy(
            src_ref=x_ref.at[peer_idx] if all_to_all else x_ref, dst_ref=out_ref.at[me],
            send_sem=send_sems.at[k - 1], recv_sem=recv_sems.at[k - 1], device_id=peer, device_id_type=MESH_IDS)
        if start:
            send.start()
        else:
            pltpu.make_async_remote_copy(
                src_ref=x_ref.at[me] if all_to_all else x_ref, dst_ref=out_ref.at[peer_idx],
                send_sem=send_sems.at[k - 1], recv_sem=recv_sems.at[k - 1], device_id=peer,
                device_id_type=MESH_IDS).wait_recv()
            send.wait_send()
    if not start:
        mine.wait()


def _ride(rider, first, last, refs):
    @pl.when(first)
    def _():
        _exchange_copies(rider[1], *refs, start=True)

    @pl.when(last)
    def _():
        _exchange_copies(rider[1], *refs, start=False)


MOD_ROWS = 16
CTX_ROW = N_DEV


def _mod_fwd(cond, w_shard, b_cols):
    n_layers, d, cols = w_shard.shape

    def body(c_ref, w_ref, b_ref, o_ref):
        c = c_ref[...]
        s = c * _sigmoid(c)
        for i in range(n_layers):
            o_ref[i] = _dot(s, w_ref[i]) + b_ref[i]

    return pl.pallas_call(
        body, name="mod_fwd", out_shape=_sds((n_layers, MOD_ROWS, cols)),
        compiler_params=pltpu.CompilerParams(vmem_limit_bytes=VMEM_LIMIT),
    )(cond, w_shard, b_cols.reshape(n_layers, 1, cols))


def _mod_bwd(cond, d_lat_cols, d_ctx_cols, w_shard):
    n_layers, d, cols = w_shard.shape

    def body(c_ref, dl_ref, dc_ref, w_ref, dw_ref, dcc_ref):
        c = c_ref[...]
        sg = _sigmoid(c)
        s = c * sg
        d_s = jnp.zeros((MOD_ROWS, d), F32)
        for i in range(n_layers):
            d_ctx = dc_ref[0, i]
            for j in range(1, N_DEV):
                d_ctx = d_ctx + dc_ref[j, i]
            dm = jnp.concatenate([dl_ref[i], d_ctx, jnp.zeros((MOD_ROWS - N_DEV - 1, cols), F32)], axis=0)
            dw_ref[i] = _dot_t0(s, dm)
            d_s = d_s + _dot_t1(dm, w_ref[i])
        d_c = d_s * (sg * (1.0 + c * (1.0 - sg)))
        dcc_ref[...] = d_c[CTX_ROW:CTX_ROW + 1]

    return pl.pallas_call(
        body, name="mod_bwd", out_shape=[_sds((n_layers, d, cols)), _sds((1, d))],
        compiler_params=pltpu.CompilerParams(vmem_limit_bytes=VMEM_LIMIT),
    )(cond, d_lat_cols, d_ctx_cols, w_shard)


ADAM_TILE = 512


def _adamw(w, g_parts, m, v, name):
    n_parts, n_rows, lanes = g_parts.shape
    c1 = 1.0 - ADAM_B1 ** ADAM_STEP
    c2 = 1.0 - ADAM_B2 ** ADAM_STEP

    def body(w_ref, g_ref, m_ref, v_ref, go_ref, d_ref, mo_ref, vo_ref):
        g = g_ref[0].astype(F32)
        for p in range(1, n_parts):
            g = g + g_ref[p].astype(F32)
        m_new = ADAM_B1 * m_ref[...] + (1.0 - ADAM_B1) * g
        v_new = ADAM_B2 * v_ref[...] + (1.0 - ADAM_B2) * (g * g)
        go_ref[...] = g
        mo_ref[...] = m_new
        vo_ref[...] = v_new
        d_ref[...] = -ADAM_LR * ((m_new / c1) / (jnp.sqrt(v_new / c2) + ADAM_EPS) + ADAM_WD * w_ref[...])

    row = pl.BlockSpec((ADAM_TILE, lanes), lambda i: (i, 0))
    return pl.pallas_call(
        body, name=name, grid=(n_rows // ADAM_TILE,),
        in_specs=[row, pl.BlockSpec((n_parts, ADAM_TILE, lanes), lambda i: (0, i, 0)), row, row],
        out_specs=[row] * 4, out_shape=[_sds((n_rows, lanes))] * 4,
        compiler_params=_cparams(1),
    )(w, g_parts, m, v)


def _sum_parts(parts):
    n_parts, n_rows, lanes = parts.shape

    def body(p_ref, o_ref):
        acc = p_ref[0]
        for p in range(1, n_parts):
            acc = acc + p_ref[p]
        o_ref[...] = acc

    return pl.pallas_call(body, name="sum_parts", out_shape=_sds((n_rows, lanes)))(parts)


def _pack(arrays, row_multiple):
    parts = []
    for a in arrays:
        flat = a.reshape(-1)
        parts.append(jnp.pad(flat, (0, (-flat.shape[0]) % 1024)))
    flat = jnp.concatenate(parts)
    flat = jnp.pad(flat, (0, (-flat.shape[0]) % (row_multiple * 128)))
    return flat.reshape(-1, 128)


def _unpack(packed, shapes):
    flat = packed.reshape(-1)
    out, pos = [], 0
    for s in shapes:
        n = math.prod(s)
        out.append(flat[pos:pos + n].reshape(s))
        pos += n + (-n) % 1024
    return out


WEIGHT_NAMES = ['c_ctx', 'w_mod', 'b_mod', 'norm_g', 'ssm_w_in', 'ssm_a_re', 'ssm_a_im', 'ssm_log_dt', 'ssm_b_re', 'ssm_b_im',
                'ssm_c_re', 'ssm_c_im', 'ssm_d', 'ssm_w_glu', 'ssm_b_glu', 'ssm_w_out', 'attn_w_in', 'attn_q_norm',
                'attn_k_norm', 'attn_w_out', 'final_norm_g']
FIRST_SHARDED = ['ssm_w_in']
POST_SHARDED = ['ssm_w_glu', 'ssm_w_out']
ATTN_SHARDED = ['attn_w_in', 'attn_w_out']
SHARDED = FIRST_SHARDED + POST_SHARDED + ATTN_SHARDED
COLUMN_SHARDED = ('ssm_w_in', 'attn_w_in')
REPLICATED = ['c_ctx', 'b_mod', 'norm_g', 'ssm_a_re', 'ssm_a_im', 'ssm_log_dt', 'ssm_b_re', 'ssm_b_im', 'ssm_c_re', 'ssm_c_im',
              'ssm_d', 'ssm_b_glu', 'attn_q_norm', 'attn_k_norm', 'final_norm_g']
SSM_NAMES = ['ssm_a_re', 'ssm_a_im', 'ssm_log_dt', 'ssm_b_re', 'ssm_b_im', 'ssm_c_re', 'ssm_c_im', 'ssm_d']


def _full_from_shards(gathered, name, shard_shape):
    rows, cols = shard_shape
    w = gathered.reshape(N_DEV, rows, cols)
    if name in COLUMN_SHARDED:
        return w.transpose(1, 0, 2).reshape(rows, N_DEV * cols)
    return w.reshape(N_DEV * rows, cols)


def _shards_from_full(g, name):
    if name in COLUMN_SHARDED:
        rows, cols = g.shape
        g = g.reshape(rows, N_DEV, cols // N_DEV).transpose(1, 0, 2)
    return g.reshape(N_DEV, -1, 128)


def kernel(x, c, ctx, c_ctx, w_mod, b_mod, norm_g, ssm_w_in, ssm_a_re, ssm_a_im, ssm_log_dt, ssm_b_re, ssm_b_im, ssm_c_re, ssm_c_im, ssm_d, ssm_w_glu, ssm_b_glu, ssm_w_out, attn_w_in, attn_q_norm, attn_k_norm, attn_w_out, final_norm_g, loss_target, m_c_ctx, m_w_mod, m_b_mod, m_norm_g, m_ssm_w_in, m_ssm_a_re, m_ssm_a_im, m_ssm_log_dt, m_ssm_b_re, m_ssm_b_im, m_ssm_c_re, m_ssm_c_im, m_ssm_d, m_ssm_w_glu, m_ssm_b_glu, m_ssm_w_out, m_attn_w_in, m_attn_q_norm, m_attn_k_norm, m_attn_w_out, m_final_norm_g, v_c_ctx, v_w_mod, v_b_mod, v_norm_g, v_ssm_w_in, v_ssm_a_re, v_ssm_a_im, v_ssm_log_dt, v_ssm_b_re, v_ssm_b_im, v_ssm_c_re, v_ssm_c_im, v_ssm_d, v_ssm_w_glu, v_ssm_b_glu, v_ssm_w_out, v_attn_w_in, v_attn_q_norm, v_attn_k_norm, v_attn_w_out, v_final_norm_g):
    env = dict(locals())
    weights = {n: env[n] for n in WEIGHT_NAMES}
    mom_m = {n: env["m_" + n] for n in WEIGHT_NAMES}
    mom_v = {n: env["v_" + n] for n in WEIGHT_NAMES}
    d = D_MODEL
    me = _my_index()
    mod_cols = w_mod.shape[-1]

    c_all = _exchange(c.reshape(8, d // 8), "gather_c", False).reshape(N_DEV, d)
    cond = jnp.concatenate([c_all, c_ctx.reshape(1, d), jnp.zeros((MOD_ROWS - N_DEV - 1, d), F32)], axis=0)
    shard_shapes = {n: weights[n].shape[1:] for n in SHARDED}
    pack_shards = lambda names: _pack([weights[n] for n in names], 1).astype(BF16)

    def unpack_full(gathered, names):
        full, pos = [], 0
        for n in names:
            rows = math.prod(shard_shapes[n]) // 128
            full.append(_full_from_shards(gathered[:, pos:pos + rows], n, shard_shapes[n]))
            pos += rows
        return full

    def exchange_of(names):
        return (pack_shards(names), lambda gathered: unpack_full(gathered, names),
                lambda *grads: jnp.concatenate([_shards_from_full(t, n) for t, n in zip(grads, names)], axis=1).astype(BF16))

    (w_ssm_in,) = unpack_full(_exchange(pack_shards(FIRST_SHARDED), "gather_ssm_w_in", False), FIRST_SHARDED)

    b_cols = lax.dynamic_slice(b_mod, (0, me * mod_cols), (2, mod_cols))
    mod_shard = _mod_fwd(cond, w_mod, b_cols)
    mod_all = _exchange(mod_shard.reshape(2 * MOD_ROWS, mod_cols), "gather_mod", False)
    mod_full = mod_all.reshape(N_DEV, 2, MOD_ROWS, mod_cols).transpose(1, 2, 0, 3).reshape(2, MOD_ROWS, 3 * d)
    lat_rows = lax.dynamic_slice(mod_full, (0, me, 0), (2, 1, 3 * d))
    mods = []
    for i in range(2):
        seg = jnp.stack([mod_full[i, CTX_ROW:CTX_ROW + 1], lat_rows[i]])
        mods.append((seg[:, :, :d], seg[:, :, d:2 * d], seg[:, :, 2 * d:]))

    ssm = tuple(weights[n][0] for n in SSM_NAMES)
    loss, grad_x, g, d_mods = _local_step(
        x[0], ctx[0], loss_target[0], mods, norm_g, ssm, w_ssm_in, None, ssm_b_glu[0], None,
        None, attn_q_norm[0], attn_k_norm[0], None, final_norm_g, exchange_of(ATTN_SHARDED), exchange_of(POST_SHARDED))

    d_rows = jnp.stack([jnp.concatenate(dm, axis=-1) for dm in d_mods])
    d_rows = jnp.concatenate([d_rows.reshape(4, 3 * d), jnp.zeros((4, 3 * d), F32)], axis=0)
    d_all = _exchange(d_rows, "gather_dmod", False)[:, :4].reshape(N_DEV, 2, 2, 3 * d)
    d_all = lax.dynamic_slice(d_all, (0, 0, 0, me * mod_cols), (N_DEV, 2, 2, mod_cols))
    d_w_mod, d_c_ctx = _mod_bwd(cond, d_all[:, :, 1].transpose(1, 0, 2), d_all[:, :, 0:1], w_mod)
    d_b_mod = jnp.stack([jnp.concatenate([t[0] + t[1] for t in dm], axis=-1).reshape(3 * d) for dm in d_mods])

    first_parts = _exchange(_shards_from_full(g['ssm_w_in'], 'ssm_w_in').astype(BF16), "scatter_ssm_w_in_grads", True)
    g_big_parts = jnp.concatenate([first_parts, g['post_parts'], g['attn_parts']], axis=1)
    pack_big = lambda t: _pack([t[n] for n in SHARDED], ADAM_TILE)
    big = _adamw(pack_big(weights), g_big_parts, pack_big(mom_m), pack_big(mom_v), "adamw_sharded")
    big = [_unpack(t, [weights[n].shape for n in SHARDED]) for t in big]

    mod_res = _adamw(_pack([w_mod], ADAM_TILE), _pack([d_w_mod], ADAM_TILE)[None], _pack([m_w_mod], ADAM_TILE),
                     _pack([v_w_mod], ADAM_TILE), "adamw_w_mod")
    mod_res = [t.reshape(w_mod.shape) for t in mod_res]

    small = dict(zip(SSM_NAMES, g['ssm']))
    small.update(c_ctx=d_c_ctx, b_mod=d_b_mod, norm_g=g['norm_g'], ssm_b_glu=g['ssm_b_glu'], attn_q_norm=g['attn_q_norm'],
                 attn_k_norm=g['attn_k_norm'], final_norm_g=g['final_norm_g'])
    pack_small = lambda t, last: _pack([t[n] for n in REPLICATED] + [last], ADAM_TILE)
    no_weight = jnp.zeros((1,), F32)
    g_small = pack_small(small, loss.reshape(1))
    slices = _exchange(g_small.reshape(N_DEV, -1, 128), "scatter_small_grads", True)
    g_small = _exchange(_sum_parts(slices), "gather_small_grads", False).reshape(1, -1, 128)
    rep = _adamw(pack_small(weights, no_weight), g_small, pack_small(mom_m, no_weight), pack_small(mom_v, no_weight),
                 "adamw_replicated")
    rep = [_unpack(t, [weights[n].shape for n in REPLICATED] + [(1,)]) for t in rep]
    loss = rep[0][-1][0]

    results = []
    for kind in range(4):
        by_name = dict(zip(SHARDED, big[kind]))
        by_name.update(zip(REPLICATED, rep[kind]))
        by_name['w_mod'] = mod_res[kind]
        results.extend(by_name[n] for n in WEIGHT_NAMES)
    return (loss, grad_x[None], *results)
```

```python
import functools
import math

import jax
import jax.numpy as jnp
from jax import lax
from jax.experimental import pallas as pl
from jax.experimental.pallas import tpu as pltpu

F32 = jnp.float32
BF16 = jnp.bfloat16

N_DEV = 8
D_MODEL = 1024
NORM_EPS = 1e-6
SSM_GROUP = 16
SSM_GROUPS = 64
SSM_STATE = 64
GROUPS_PER_BLOCK = 8
N_BLOCKS = SSM_GROUPS // GROUPS_PER_BLOCK
HALF = GROUPS_PER_BLOCK * SSM_STATE
HEAD_DIM = 64
N_Q_HEADS = 16
N_KV_HEADS = 4
KV_REP = N_Q_HEADS // N_KV_HEADS
KV_WIDTH = N_KV_HEADS * HEAD_DIM
GRID_W = 64
ROPE_THETA = 10000.0
ADAM_LR, ADAM_B1, ADAM_B2, ADAM_EPS, ADAM_WD, ADAM_STEP = 0.001, 0.9, 0.999, 1e-08, 0.01, 10

ROW_TILE = 256
SCAN_CHUNK = 256
VMEM_LIMIT = 56 * 1024 * 1024
MESH_IDS = pl.DeviceIdType.MESH


def _cparams(n_axes):
    return pltpu.CompilerParams(dimension_semantics=("arbitrary",) * n_axes, vmem_limit_bytes=VMEM_LIMIT)


def _dot(a, b):
    return jnp.dot(a.astype(BF16), b.astype(BF16), preferred_element_type=F32)


def _dot_t0(a, b):
    return lax.dot_general(a.astype(BF16), b.astype(BF16), (((0,), (0,)), ((), ())), preferred_element_type=F32)


def _dot_t1(a, b):
    return lax.dot_general(a.astype(BF16), b.astype(BF16), (((1,), (1,)), ((), ())), preferred_element_type=F32)


def _s5_prep(a_re, a_im, log_dt, b_re, b_im):
    dt = jnp.exp(log_dt)[:, None]
    ldr, ldi = a_re * dt, a_im * dt
    mag = jnp.exp(ldr)
    abar_re, abar_im = mag * jnp.cos(ldi), mag * jnp.sin(ldi)
    den = a_re * a_re + a_im * a_im
    num_re, num_im = abar_re - 1.0, abar_im
    coef_re = (num_re * a_re + num_im * a_im) / den
    coef_im = (num_im * a_re - num_re * a_im) / den
    bbar_re = coef_re[..., None] * b_re - coef_im[..., None] * b_im
    bbar_im = coef_re[..., None] * b_im + coef_im[..., None] * b_re
    return abar_re, abar_im, bbar_re, bbar_im


def _s5_blocks(abar_re, abar_im, bbar_re, bbar_im, c_re, c_im):
    eye = jnp.eye(GROUPS_PER_BLOCK, dtype=F32)
    bb = jnp.stack([bbar_re, bbar_im]).reshape(2, N_BLOCKS, GROUPS_PER_BLOCK, SSM_STATE, SSM_GROUP)
    b_blk = jnp.einsum('rqgph,gk->qghrkp', bb, eye).reshape(N_BLOCKS, 128, 2 * HALF)
    cc = jnp.stack([c_re, -c_im]).reshape(2, N_BLOCKS, GROUPS_PER_BLOCK, SSM_GROUP, SSM_STATE)
    c_blk = jnp.einsum('rqghp,gk->qrgpkh', cc, eye).reshape(N_BLOCKS, 2 * HALF, 128)
    abar = jnp.stack([abar_re.reshape(N_BLOCKS, HALF), abar_im.reshape(N_BLOCKS, HALF)])
    return abar, b_blk, c_blk


def _s5_unblock(d_b_blk, d_ct_blk):
    db = d_b_blk.reshape(N_BLOCKS, GROUPS_PER_BLOCK, SSM_GROUP, 2, GROUPS_PER_BLOCK, SSM_STATE)
    db = jnp.einsum('qghrgp->rqgph', db).reshape(2, SSM_GROUPS, SSM_STATE, SSM_GROUP)
    dc = d_ct_blk.reshape(N_BLOCKS, GROUPS_PER_BLOCK, SSM_GROUP, 2, GROUPS_PER_BLOCK, SSM_STATE)
    dc = jnp.einsum('qghrgp->rqghp', dc).reshape(2, SSM_GROUPS, SSM_GROUP, SSM_STATE)
    return db[0], db[1], dc[0], -dc[1]


def _scan_chunk_of_step(j, n_chunks, n_ctx_chunks, reverse):
    if not reverse:
        return j
    return jnp.where(j < n_ctx_chunks, n_ctx_chunks - 1 - j, n_chunks - 1 - j + n_ctx_chunks)


LANE_TILES = 2 * HALF // 128
RE_TILES = HALF // 128


def _tiles(v):
    return [v[:, l * 128:(l + 1) * 128] for l in range(v.shape[1] // 128)]


def _scatter_steps(s_ref, q, x):
    for l in range(LANE_TILES):
        s_ref[l, pl.ds(q, x.shape[0], stride=N_BLOCKS), :] = x[:, l * 128:(l + 1) * 128]


def _gather_steps(s_ref, q, n_steps):
    return jnp.concatenate([s_ref[l, pl.ds(q, n_steps, stride=N_BLOCKS), :] for l in range(LANE_TILES)], axis=1)


def _load_step(s_ref, t):
    row = pl.multiple_of(t * N_BLOCKS, N_BLOCKS)
    return [s_ref[l, pl.ds(row, N_BLOCKS), :] for l in range(LANE_TILES)]


def _store_step(s_ref, t, tiles):
    row = pl.multiple_of(t * N_BLOCKS, N_BLOCKS)
    for l in range(LANE_TILES):
        s_ref[l, pl.ds(row, N_BLOCKS), :] = tiles[l]


SCAN_UNROLL = 8
ADJOINT_UNROLL = 5


def _unrolled_loop(n_steps, unroll, step, carry):
    assert n_steps % unroll == 0

    def steps(i, c):
        for r in range(unroll):
            c = step(unroll * i + r, c)
        return c

    return lax.fori_loop(0, n_steps // unroll, steps, carry)


def _cmul_add(a, h, x, conj):
    re, im = [], []
    for l in range(RE_TILES):
        ar, ai, hr, hi = a[l], a[RE_TILES + l], h[l], h[RE_TILES + l]
        if conj:
            re.append(ar * hr + ai * hi + x[l])
            im.append(ar * hi - ai * hr + x[RE_TILES + l])
        else:
            re.append(ar * hr - ai * hi + x[l])
            im.append(ar * hi + ai * hr + x[RE_TILES + l])
    return re + im


def _s5_scan_fwd(u, abar, b_blk, c_blk, d_skip, n_ctx, reverse, rider=None):
    n_rows, width = u.shape
    tc = SCAN_CHUNK
    n_chunks, n_ctx_chunks = n_rows // tc, n_ctx // tc
    with_skip = d_skip is not None

    def body(*refs):
        if rider is not None:
            x_ref, ride_out, sems = refs[4 + with_skip], refs[7 + with_skip], refs[-3:]
            _ride(rider, pl.program_id(0) == 0, pl.program_id(0) == n_chunks - 1, (x_ref, ride_out) + tuple(sems))
            refs = refs[:4 + with_skip] + refs[5 + with_skip:7 + with_skip] + refs[8 + with_skip:-3]
        if with_skip:
            u_ref, a_ref, b_ref, c_ref, d_ref, y_ref, hb_ref, s_ref, h_ref = refs
        else:
            u_ref, a_ref, b_ref, c_ref, y_ref, hb_ref, s_ref, h_ref = refs
        j = pl.program_id(0)

        @pl.when(j == 0)
        def _():
            h_ref[...] = jnp.zeros_like(h_ref)

        hb_ref[0] = h_ref[...]
        for q in range(N_BLOCKS):
            _scatter_steps(s_ref, q, _dot(u_ref[:, q * 128:(q + 1) * 128], b_ref[q]))
        a = _tiles(a_ref[0]) + _tiles(a_ref[1])

        def step(s, h):
            t = tc - 1 - s if reverse else s
            h = _cmul_add(a, h, _load_step(s_ref, t), conj=False)
            _store_step(s_ref, t, h)
            return h

        h = _unrolled_loop(tc, SCAN_UNROLL, step, _tiles(h_ref[...]))
        h_ref[...] = jnp.concatenate(h, axis=1)
        for q in range(N_BLOCKS):
            yq = _dot(_gather_steps(s_ref, q, tc), c_ref[q])
            if with_skip:
                yq = yq + d_ref[:, q * 128:(q + 1) * 128] * u_ref[:, q * 128:(q + 1) * 128]
            y_ref[:, q * 128:(q + 1) * 128] = yq

    chunk = functools.partial(_scan_chunk_of_step, n_chunks=n_chunks, n_ctx_chunks=n_ctx_chunks, reverse=reverse)
    full3 = lambda j: (0, 0, 0)
    in_specs = [pl.BlockSpec((tc, width), lambda j: (chunk(j), 0)),
                pl.BlockSpec((2, N_BLOCKS, HALF), full3),
                pl.BlockSpec((N_BLOCKS, 128, 2 * HALF), full3),
                pl.BlockSpec((N_BLOCKS, 2 * HALF, 128), full3)]
    args = [u, abar, b_blk.astype(BF16), c_blk.astype(BF16)]
    if with_skip:
        in_specs.append(pl.BlockSpec((1, width), lambda j: (0, 0)))
        args.append(d_skip.reshape(1, width))
    out_specs = [pl.BlockSpec((tc, width), lambda j: (chunk(j), 0)),
                 pl.BlockSpec((1, N_BLOCKS, 2 * HALF), lambda j: (chunk(j), 0, 0))]
    out_shape = [_sds((n_rows, width)), _sds((n_chunks, N_BLOCKS, 2 * HALF))]
    scratch = [pltpu.VMEM((LANE_TILES, tc * N_BLOCKS, 128), F32), pltpu.VMEM((N_BLOCKS, 2 * HALF), F32)]
    if rider is not None:
        in_specs.append(HBM_SPEC)
        args.append(rider[0])
        out_specs.append(HBM_SPEC)
        out_shape.append(_exchange_out_shape(*rider))
        scratch += _exchange_semaphores()
    return pl.pallas_call(
        body, name="s5_scan_fwd_rev" if reverse else "s5_scan_fwd",
        grid=(n_chunks,), in_specs=in_specs, out_specs=out_specs, out_shape=out_shape, scratch_shapes=scratch,
        compiler_params=_cparams(1),
    )(*args)


def _s5_scan_bwd(u, dy, hb, abar, b_blk, c_blk, d_skip, n_ctx, reverse, rider=None):
    n_rows, width = u.shape
    tc = SCAN_CHUNK
    n_chunks, n_ctx_chunks = n_rows // tc, n_ctx // tc
    with_skip = d_skip is not None
    n_in, n_out = 7 + with_skip, 4 + with_skip

    def body(*refs):
        if rider is not None:
            x_ref, ride_out, sems = refs[n_in], refs[n_in + 1 + n_out], refs[-3:]
            _ride(rider, pl.program_id(0) == 0, pl.program_id(0) == n_chunks - 1, (x_ref, ride_out) + tuple(sems))
            refs = refs[:n_in] + refs[n_in + 1:n_in + 1 + n_out] + refs[n_in + 2 + n_out:-3]
        if with_skip:
            (u_ref, dy_ref, hb_ref, a_ref, b_ref, bt_ref, ct_ref, d_ref,
             du_ref, da_ref, db_ref, dct_ref, dd_ref, sh_ref, sg_ref, g_ref) = refs
        else:
            (u_ref, dy_ref, hb_ref, a_ref, b_ref, bt_ref, ct_ref,
             du_ref, da_ref, db_ref, dct_ref, sh_ref, sg_ref, g_ref) = refs
        j = pl.program_id(0)

        @pl.when(j == 0)
        def _():
            g_ref[...] = jnp.zeros_like(g_ref)
            da_ref[...] = jnp.zeros_like(da_ref)
            db_ref[...] = jnp.zeros_like(db_ref)
            dct_ref[...] = jnp.zeros_like(dct_ref)
            if with_skip:
                dd_ref[...] = jnp.zeros_like(dd_ref)

        for q in range(N_BLOCKS):
            _scatter_steps(sh_ref, q, _dot(u_ref[:, q * 128:(q + 1) * 128], b_ref[q]))
            _scatter_steps(sg_ref, q, _dot(dy_ref[:, q * 128:(q + 1) * 128], ct_ref[q]))
        a = _tiles(a_ref[0]) + _tiles(a_ref[1])
        time_of = (lambda s: tc - 1 - s) if reverse else (lambda s: s)

        def fwd_step(s, h):
            h = _cmul_add(a, h, _load_step(sh_ref, time_of(s)), conj=False)
            _store_step(sh_ref, time_of(s), h)
            return h

        h0 = _tiles(hb_ref[0])
        _unrolled_loop(tc, SCAN_UNROLL, fwd_step, h0)

        def adj(t, h_prev, carry):
            g, da = carry
            g = _cmul_add(a, g, _load_step(sg_ref, t), conj=True)
            _store_step(sg_ref, t, g)
            da_re = [da[l] + g[l] * h_prev[l] + g[RE_TILES + l] * h_prev[RE_TILES + l] for l in range(RE_TILES)]
            da_im = [da[RE_TILES + l] + g[RE_TILES + l] * h_prev[l] - g[l] * h_prev[RE_TILES + l] for l in range(RE_TILES)]
            return g, da_re + da_im

        def bwd_step(i, carry):
            s = tc - 1 - i
            return adj(time_of(s), _load_step(sh_ref, time_of(s - 1)), carry)

        carry = (_tiles(g_ref[...]), _tiles(da_ref[0]) + _tiles(da_ref[1]))
        carry = _unrolled_loop(tc - 1, ADJOINT_UNROLL, bwd_step, carry)
        g, da = adj(time_of(0), h0, carry)
        g_ref[...] = jnp.concatenate(g, axis=1)
        da_ref[0] = jnp.concatenate(da[:RE_TILES], axis=1)
        da_ref[1] = jnp.concatenate(da[RE_TILES:], axis=1)

        for q in range(N_BLOCKS):
            cols = slice(q * 128, (q + 1) * 128)
            uq, dyq = u_ref[:, cols], dy_ref[:, cols]
            gq = _gather_steps(sg_ref, q, tc)
            duq = _dot(gq, bt_ref[q])
            if with_skip:
                duq = duq + d_ref[:, cols] * dyq
                dd_ref[:, cols] += jnp.sum(dyq * uq, axis=0, keepdims=True)
            du_ref[:, cols] = duq
            db_ref[q] += _dot_t0(uq, gq)
            dct_ref[q] += _dot_t0(dyq, _gather_steps(sh_ref, q, tc))

    def chunk(j):
        return _scan_chunk_of_step(n_chunks - 1 - j, n_chunks, n_ctx_chunks, reverse)

    full2 = lambda j: (0, 0)
    full3 = lambda j: (0, 0, 0)
    row = pl.BlockSpec((tc, width), lambda j: (chunk(j), 0))
    in_specs = [row, row,
                pl.BlockSpec((1, N_BLOCKS, 2 * HALF), lambda j: (chunk(j), 0, 0)),
                pl.BlockSpec((2, N_BLOCKS, HALF), full3),
                pl.BlockSpec((N_BLOCKS, 128, 2 * HALF), full3),
                pl.BlockSpec((N_BLOCKS, 2 * HALF, 128), full3),
                pl.BlockSpec((N_BLOCKS, 128, 2 * HALF), full3)]
    args = [u, dy, hb, abar, b_blk.astype(BF16), jnp.swapaxes(b_blk, 1, 2).astype(BF16),
            jnp.swapaxes(c_blk, 1, 2).astype(BF16)]
    out_specs = [row,
                 pl.BlockSpec((2, N_BLOCKS, HALF), full3),
                 pl.BlockSpec((N_BLOCKS, 128, 2 * HALF), full3),
                 pl.BlockSpec((N_BLOCKS, 128, 2 * HALF), full3)]
    out_shape = [jax.ShapeDtypeStruct((n_rows, width), F32),
                 jax.ShapeDtypeStruct((2, N_BLOCKS, HALF), F32),
                 jax.ShapeDtypeStruct((N_BLOCKS, 128, 2 * HALF), F32),
                 jax.ShapeDtypeStruct((N_BLOCKS, 128, 2 * HALF), F32)]
    if with_skip:
        in_specs.append(pl.BlockSpec((1, width), full2))
        args.append(d_skip.reshape(1, width))
        out_specs.append(pl.BlockSpec((1, width), full2))
        out_shape.append(jax.ShapeDtypeStruct((1, width), F32))
    scratch = [pltpu.VMEM((LANE_TILES, tc * N_BLOCKS, 128), F32), pltpu.VMEM((LANE_TILES, tc * N_BLOCKS, 128), F32),
               pltpu.VMEM((N_BLOCKS, 2 * HALF), F32)]
    if rider is not None:
        in_specs.append(HBM_SPEC)
        args.append(rider[0])
        out_specs.append(HBM_SPEC)
        out_shape.append(_exchange_out_shape(*rider))
        scratch += _exchange_semaphores()
    return pl.pallas_call(
        body, name="s5_scan_bwd_rev" if reverse else "s5_scan_bwd",
        grid=(n_chunks,), in_specs=in_specs, out_specs=out_specs, out_shape=out_shape, scratch_shapes=scratch,
        compiler_params=_cparams(1),
    )(*args)


def _s5_dir_params(d, a_re, a_im, log_dt, b_re, b_im):
    return a_re[d], a_im[d], log_dt[d], b_re[d], b_im[d]


def _s5_forward(u, ssm, n_ctx, riders=(None, None)):
    a_re, a_im, log_dt, b_re, b_im, c_re, c_im, d_skip = ssm
    outs, saved, carried = [], [], [None, None]
    for d in range(2):
        prep = _s5_prep(*_s5_dir_params(d, a_re, a_im, log_dt, b_re, b_im))
        abar, b_blk, c_blk = _s5_blocks(*prep, c_re[d], c_im[d])
        res = _s5_scan_fwd(u, abar, b_blk, c_blk, d_skip if d == 0 else None, n_ctx, reverse=(d == 1), rider=riders[d])
        if riders[d] is not None:
            carried[d] = res[2]
        outs.append(res[0])
        saved.append((res[1], abar, b_blk, c_blk))
    return outs, saved, carried


def _s5_backward(u, dy, ssm, saved, n_ctx, riders=(None, None)):
    a_re, a_im, log_dt, b_re, b_im, c_re, c_im, d_skip = ssm
    dus, grads = [], [[] for _ in range(7)]
    d_d, carried = None, [None, None]
    for d in range(2):
        hb, abar, b_blk, c_blk = saved[d]
        res = _s5_scan_bwd(u, dy, hb, abar, b_blk, c_blk, d_skip if d == 0 else None, n_ctx, reverse=(d == 1),
                           rider=riders[d])
        if riders[d] is not None:
            carried[d], res = res[-1], res[:-1]
        if d == 0:
            du, d_abar, d_b_blk, d_ct_blk, d_d = res
        else:
            du, d_abar, d_b_blk, d_ct_blk = res
        dus.append(du)
        dbb_re, dbb_im, dc_re, dc_im = _s5_unblock(d_b_blk, d_ct_blk)
        _, vjp = jax.vjp(_s5_prep, *_s5_dir_params(d, a_re, a_im, log_dt, b_re, b_im))
        shape = (SSM_GROUPS, SSM_STATE)
        g5 = vjp((d_abar[0].reshape(shape), d_abar[1].reshape(shape), dbb_re, dbb_im))
        for k, g in enumerate(tuple(g5) + (dc_re, dc_im)):
            grads[k].append(g)
    grads = [jnp.stack(g) for g in grads]
    return dus, grads + [d_d.reshape(-1)], carried


INV_SQRT2 = 0.7071067811865476
INV_SQRT_2PI = 0.3989422804014327


def _rows(cols):
    return pl.BlockSpec((ROW_TILE, cols), lambda i: (i, 0))


def _rows_skip_ctx(cols):
    return pl.BlockSpec((ROW_TILE, cols), lambda i: (i + 1, 0))


def _rows_lat(cols):
    return pl.BlockSpec((ROW_TILE, cols), lambda i: (jnp.maximum(i - 1, 0), 0))


def _full(shape):
    nd = len(shape)
    return pl.BlockSpec(shape, lambda i: (0,) * nd)


def _seg(cols):
    return pl.BlockSpec((1, 1, cols), lambda i: (jnp.minimum(i, 1), 0, 0))


def _lat_seg(cols):
    return pl.BlockSpec((1, 1, cols), lambda i: (1, 0, 0))


def _sds(shape, dtype=F32):
    return jax.ShapeDtypeStruct(shape, dtype)


def _sum0(x):
    return jnp.sum(x, axis=0, keepdims=True)


def _sigmoid(x):
    return jax.nn.sigmoid(x)


def _rms_mod(x, g, scale, shift):
    r = lax.rsqrt(jnp.mean(x * x, axis=-1, keepdims=True) + NORM_EPS)
    return (x * r * g) * (1.0 + scale) + shift


def _rms_mod_bwd(x, g, scale, dh):
    r = lax.rsqrt(jnp.mean(x * x, axis=-1, keepdims=True) + NORM_EPS)
    n = x * r
    dyg = dh * (1.0 + scale)
    dn = dyg * g
    dx = r * (dn - n * jnp.mean(dn * n, axis=-1, keepdims=True))
    return dx, _sum0(dyg * n), _sum0(dh * (n * g)), _sum0(dh)


def _head_of_lane(width):
    return (jnp.arange(width)[:, None] // HEAD_DIM == jnp.arange(128)[None, :]).astype(BF16)


def _split_dot(t, w, transposed):
    hi = t.astype(BF16)
    lo = (t - hi.astype(F32)).astype(BF16)
    f = _dot_t1 if transposed else _dot
    return f(hi, w) + f(lo, w)


def _head_sums(t, hl):
    return _split_dot(_split_dot(t, hl, False), hl, True)


def _rope_partner(x):
    n = x.shape[1]
    lane = lax.broadcasted_iota(jnp.int32, x.shape, 1)
    return jnp.where((lane & 16) == 0, pltpu.roll(x, n - 16, 1), pltpu.roll(x, 16, 1))


def _lanes(tab, width):
    return jnp.tile(tab, (1, width // tab.shape[1]))


def _head_norm_rope(x, gain, cos, sin, hl):
    r = lax.rsqrt(_head_sums(x * x, hl) * (1.0 / HEAD_DIM) + NORM_EPS)
    y = x * r * gain
    return y * cos + _rope_partner(y) * sin


def _head_norm_rope_bwd(x, gain, cos, sin, hl, dout):
    dy = dout * cos + _rope_partner(dout * sin)
    r = lax.rsqrt(_head_sums(x * x, hl) * (1.0 / HEAD_DIM) + NORM_EPS)
    n = x * r
    dn = dy * gain
    dx = r * (dn - n * (_head_sums(dn * n, hl) * (1.0 / HEAD_DIM)))
    return dx, _sum0(dy * n)


def _rope_tables(n_ctx, n_lat):
    t = jnp.arange(n_lat)
    pos = jnp.stack([(t // GRID_W).astype(F32), (t % GRID_W).astype(F32)], axis=1)
    n_freq = HEAD_DIM // 4
    freqs = ROPE_THETA ** (-jnp.arange(n_freq, dtype=F32) / n_freq)
    ang = pos[:, :, None] * freqs[None, None, :]
    cos = jnp.repeat(jnp.cos(ang)[:, :, None, :], 2, axis=2).reshape(n_lat, HEAD_DIM)
    sin = jnp.sin(ang)
    sin = jnp.stack([-sin, sin], axis=2).reshape(n_lat, HEAD_DIM)
    cos = jnp.concatenate([jnp.ones((n_ctx, HEAD_DIM), F32), cos], axis=0)
    sin = jnp.concatenate([jnp.zeros((n_ctx, HEAD_DIM), F32), sin], axis=0)
    return jnp.tile(cos, (1, 2)), jnp.tile(sin, (1, 2))


def _ssm_in(xa, g, scale, shift, w_in):
    n_rows, d = xa.shape
    e = w_in.shape[1] // 2

    def body(x_ref, g_ref, sc_ref, sh_ref, w_ref, u_ref, z_ref):
        h = _rms_mod(x_ref[...], g_ref[...], sc_ref[0], sh_ref[0])
        proj = _dot(h, w_ref[...])
        u_ref[...] = proj[:, :e]
        z_ref[...] = proj[:, e:]

    return pl.pallas_call(
        body, name="ssm_in", grid=(n_rows // ROW_TILE,),
        in_specs=[_rows(d), _full((1, d)), _seg(d), _seg(d), _full(w_in.shape)],
        out_specs=[_rows(e), _rows(e)], out_shape=[_sds((n_rows, e)), _sds((n_rows, e))],
        compiler_params=_cparams(1),
    )(xa, g, scale, shift, w_in)


def _s5_post_math(y, z, w_glu, b_glu, w_out):
    er = lax.erf(y * INV_SQRT2)
    g = 0.5 * y * (1.0 + er)
    sg = _sigmoid(_dot(g, w_glu) + b_glu)
    g2 = g * sg
    sz = _sigmoid(z)
    silu_z = z * sz
    m = g2 * silu_z
    return er, g, sg, g2, sz, silu_z, m, _dot(m, w_out)


def _ssm_post(xa, y0, y1, z, gate, w_glu, b_glu, w_out):
    n_rows, d = xa.shape
    e = z.shape[1]

    def body(x_ref, y0_ref, y1_ref, z_ref, gt_ref, wg_ref, bg_ref, wo_ref, o_ref):
        out = _s5_post_math(y0_ref[...] + y1_ref[...], z_ref[...], wg_ref[...], bg_ref[...], wo_ref[...])[-1]
        o_ref[...] = x_ref[...] + gt_ref[0] * out

    return pl.pallas_call(
        body, name="ssm_post", grid=(n_rows // ROW_TILE,),
        in_specs=[_rows(d), _rows(e), _rows(e), _rows(e), _seg(d), _full(w_glu.shape), _full((1, e)), _full(w_out.shape)],
        out_specs=_rows(d), out_shape=_sds((n_rows, d)),
        compiler_params=_cparams(1),
    )(xa, y0, y1, z, gate, w_glu, b_glu, w_out)


def _init_acc(first, *refs):
    @pl.when(first)
    def _():
        for r in refs:
            r[...] = jnp.zeros_like(r)


def _ssm_post_bwd(dxa, y0, y1, z, gate, w_glu, b_glu, w_out, w_glu_t, w_out_t):
    n_rows, d = dxa.shape
    e = z.shape[1]

    def body(dx_ref, y0_ref, y1_ref, z_ref, gt_ref, wg_ref, bg_ref, wo_ref, wgt_ref, wot_ref,
             dy_ref, dz_ref, dgt_ref, dwo_ref, dwg_ref, dbg_ref):
        i = pl.program_id(0)
        _init_acc(i == 0, dwo_ref, dwg_ref, dbg_ref)
        _init_acc(i <= 1, dgt_ref)
        y, zz = y0_ref[...] + y1_ref[...], z_ref[...]
        er, g, sg, g2, sz, silu_z, m, out = _s5_post_math(y, zz, wg_ref[...], bg_ref[...], wo_ref[...])
        dxa_t = dx_ref[...]
        dgt_ref[0] += _sum0(dxa_t * out)
        dout = gt_ref[0] * dxa_t
        dm = _dot(dout, wot_ref[...])
        dwo_ref[...] += _dot_t0(m, dout)
        dg2 = dm * silu_z
        dz_ref[...] = dm * g2 * (sz * (1.0 + zz * (1.0 - sz)))
        dt = dg2 * g * sg * (1.0 - sg)
        dwg_ref[...] += _dot_t0(g, dt)
        dbg_ref[...] += _sum0(dt)
        dg = dg2 * sg + _dot(dt, wgt_ref[...])
        dy_ref[...] = dg * (0.5 * (1.0 + er) + y * jnp.exp(-0.5 * y * y) * INV_SQRT_2PI)

    return pl.pallas_call(
        body, name="ssm_post_bwd", grid=(n_rows // ROW_TILE,),
        in_specs=[_rows(d), _rows(e), _rows(e), _rows(e), _seg(d), _full(w_glu.shape), _full((1, e)), _full(w_out.shape),
                  _full(w_glu_t.shape), _full(w_out_t.shape)],
        out_specs=[_rows(e), _rows(e), _seg(d), _full(w_out.shape), _full(w_glu.shape), _full((1, e))],
        out_shape=[_sds((n_rows, e)), _sds((n_rows, e)), _sds((2, 1, d)), _sds(w_out.shape), _sds(w_glu.shape), _sds((1, e))],
        compiler_params=_cparams(1),
    )(dxa, y0, y1, z, gate, w_glu, b_glu, w_out, w_glu_t, w_out_t)


def _ssm_in_bwd(du0, du1, dz, xa, dxa_next, g, scale, shift, w_in_t):
    n_rows, d = xa.shape
    e = dz.shape[1]
    n_lat = n_rows - ROW_TILE

    def body(du0_ref, du1_ref, dz_ref, x_ref, dn_ref, g_ref, sc_ref, sh_ref, wt_ref,
             gx_ref, dw_ref, dg_ref, dsc_ref, dsh_ref):
        i = pl.program_id(0)
        _init_acc(i == 0, dw_ref, dg_ref)
        _init_acc(i <= 1, dsc_ref, dsh_ref)
        x = x_ref[...]
        h = _rms_mod(x, g_ref[...], sc_ref[0], sh_ref[0])
        dproj = jnp.concatenate([du0_ref[...] + du1_ref[...], dz_ref[...]], axis=1)
        dh = _dot(dproj, wt_ref[...])
        dw_ref[...] += _dot_t0(h, dproj)
        dx, dg, dsc, dsh = _rms_mod_bwd(x, g_ref[...], sc_ref[0], dh)
        dg_ref[...] += dg
        dsc_ref[0] += dsc
        dsh_ref[0] += dsh
        gx_ref[...] = dn_ref[...] + dx

    return pl.pallas_call(
        body, name="ssm_in_bwd", grid=(n_rows // ROW_TILE,),
        in_specs=[_rows(e), _rows(e), _rows(e), _rows(d), _rows(d), _full((1, d)), _seg(d), _seg(d), _full(w_in_t.shape)],
        out_specs=[_rows_lat(d), _full((d, 2 * e)), _full((1, d)), _seg(d), _seg(d)],
        out_shape=[_sds((n_lat, d)), _sds((d, 2 * e)), _sds((1, d)), _sds((2, 1, d)), _sds((2, 1, d))],
        compiler_params=_cparams(1),
    )(du0, du1, dz, xa, dxa_next, g, scale, shift, w_in_t)


Q_WIDTH = N_Q_HEADS * HEAD_DIM
SM_SCALE = 1.0 / math.sqrt(HEAD_DIM)


def _attn_in(xa, g, scale, shift, w_in, q_gain, k_gain, cos, sin):
    n_rows, d = xa.shape
    qk = Q_WIDTH + KV_WIDTH

    def body(x_ref, g_ref, sc_ref, sh_ref, w_ref, qg_ref, kg_ref, cos_ref, sin_ref, hq_ref, hk_ref,
             q_ref, k_ref, v_ref, z_ref, raw_ref):
        h = _rms_mod(x_ref[...], g_ref[...], sc_ref[0], sh_ref[0])
        proj = _dot(h, w_ref[...])
        q_raw, k_raw = proj[:, :Q_WIDTH], proj[:, Q_WIDTH:qk]
        cos, sin = cos_ref[...], sin_ref[...]
        q = _head_norm_rope(q_raw, qg_ref[...], _lanes(cos, Q_WIDTH), _lanes(sin, Q_WIDTH), hq_ref[...])
        k = _head_norm_rope(k_raw, kg_ref[...], _lanes(cos, KV_WIDTH), _lanes(sin, KV_WIDTH), hk_ref[...])
        q_ref[...] = (q * SM_SCALE).astype(BF16)
        k_ref[...] = k.astype(BF16)
        v_ref[...] = proj[:, qk:qk + KV_WIDTH].astype(BF16)
        z_ref[...] = proj[:, qk + KV_WIDTH:]
        raw_ref[...] = proj[:, :qk]

    return pl.pallas_call(
        body, name="attn_in", grid=(n_rows // ROW_TILE,),
        in_specs=[_rows(d), _full((1, d)), _seg(d), _seg(d), _full(w_in.shape), _full((1, Q_WIDTH)), _full((1, KV_WIDTH)),
                  _rows(128), _rows(128), _full((Q_WIDTH, 128)), _full((KV_WIDTH, 128))],
        out_specs=[_rows_lat(Q_WIDTH), _rows(KV_WIDTH), _rows(KV_WIDTH), _rows(Q_WIDTH), _rows(qk)],
        out_shape=[_sds((n_rows - ROW_TILE, Q_WIDTH), BF16), _sds((n_rows, KV_WIDTH), BF16), _sds((n_rows, KV_WIDTH), BF16),
                   _sds((n_rows, Q_WIDTH)), _sds((n_rows, qk))],
        compiler_params=_cparams(1),
    )(xa, g, scale, shift, w_in, q_gain, k_gain, cos, sin, _head_of_lane(Q_WIDTH), _head_of_lane(KV_WIDTH))


GROUP_WIDTH = KV_REP * HEAD_DIM


def _stack_heads(ref):
    return jnp.concatenate([ref[:, h * HEAD_DIM:(h + 1) * HEAD_DIM] for h in range(KV_REP)], axis=0)


def _unstack_heads(a_t, tq):
    return jnp.concatenate([a_t[:, h * tq:(h + 1) * tq].T for h in range(KV_REP)], axis=1)


def _kv_tile(n_keys):
    return 768 if n_keys % 768 == 0 else 256


def _kv_tile_fwd(n_keys):
    return 1408 if n_keys % 1408 == 0 else _kv_tile(n_keys)


def _q_tile(n_lat):
    return 512 if n_lat % 512 == 0 else 256


def _flash_fwd(q, k, v_t):
    n_lat = q.shape[0]
    tq = _q_tile(n_lat)
    rows = KV_REP * tq
    n_kv, tk, n_q = k.shape[1], k.shape[2], n_lat // tq

    v_rows = v_t.shape[2]

    def body(q_ref, k_ref, vt_ref, o_ref, lse_ref):
        q = _stack_heads(q_ref)

        def step(j, carry):
            m_prev, acc = carry
            s_t = _dot_t1(k_ref[0, j], q)
            m_new = jnp.maximum(m_prev, jnp.max(s_t, axis=0, keepdims=True))
            alpha = jnp.exp(m_prev - m_new)
            p_t = jnp.exp(s_t - m_new)
            return m_new, alpha * acc + _dot(vt_ref[0, j], p_t)

        init = (jnp.full((1, rows), -jnp.inf, F32), jnp.zeros((v_rows, rows), F32))
        m, acc = lax.fori_loop(0, n_kv, step, init)
        l = acc[HEAD_DIM:HEAD_DIM + 1]
        o_ref[...] = _unstack_heads(acc[:HEAD_DIM] / l, tq)
        lse_ref[0, 0] = m + jnp.log(l)

    kv_all = lambda a: pl.BlockSpec((1,) + a.shape[1:], lambda g, i: (g, 0, 0, 0))
    return pl.pallas_call(
        body, name="flash_fwd", grid=(N_KV_HEADS, n_q),
        in_specs=[pl.BlockSpec((tq, GROUP_WIDTH), lambda g, i: (i, g)), kv_all(k), kv_all(v_t)],
        out_specs=[pl.BlockSpec((tq, GROUP_WIDTH), lambda g, i: (i, g)),
                   pl.BlockSpec((1, 1, 1, rows), lambda g, i: (g, i, 0, 0))],
        out_shape=[_sds((n_lat, Q_WIDTH)), _sds((N_KV_HEADS, n_q, 1, rows))],
        compiler_params=_cparams(2),
    )(q, k, v_t)


def _flash_bwd(q, k, k_t, v, do, lse_t, delta_t):
    n_lat = q.shape[0]
    tq = _q_tile(n_lat)
    rows = KV_REP * tq
    n_kv, tk, n_q = k.shape[1], k.shape[2], n_lat // tq

    def body(q_ref, k_ref, kt_ref, v_ref, do_ref, lse_ref, dl_ref, dq_ref, dk_ref, dv_ref):
        _init_acc(pl.program_id(1) == 0, dk_ref, dv_ref)
        q, do = _stack_heads(q_ref), _stack_heads(do_ref)
        lse, delta = lse_ref[0, 0], dl_ref[0, 0]

        def step(j, dq_acc):
            p_t = jnp.exp(_dot_t1(k_ref[0, j], q) - lse)
            dv_ref[0, j] += _dot(p_t, do)
            ds_t = p_t * (_dot_t1(v_ref[0, j], do) - delta)
            dk_ref[0, j] += _dot(ds_t, q)
            return dq_acc + _dot(kt_ref[0, j], ds_t)

        dq = lax.fori_loop(0, n_kv, step, jnp.zeros((HEAD_DIM, rows), F32))
        dq_ref[...] = _unstack_heads(dq, tq)

    qspec = pl.BlockSpec((tq, GROUP_WIDTH), lambda g, i: (i, g))
    rowspec = pl.BlockSpec((1, 1, 1, rows), lambda g, i: (g, i, 0, 0))
    kv_all = lambda a: pl.BlockSpec((1,) + a.shape[1:], lambda g, i: (g, 0, 0, 0))
    return pl.pallas_call(
        body, name="flash_bwd", grid=(N_KV_HEADS, n_q),
        in_specs=[qspec, kv_all(k), kv_all(k_t), kv_all(v), qspec, rowspec, rowspec],
        out_specs=[qspec, kv_all(k), kv_all(k)],
        out_shape=[_sds((n_lat, Q_WIDTH)), _sds(k.shape), _sds(k.shape)],
        compiler_params=_cparams(2),
    )(q, k, k_t, v, do, lse_t, delta_t)


def _to_lane_stacked(a, tq):
    n_lat = a.shape[0]
    a = a.reshape(n_lat // tq, tq, N_KV_HEADS, KV_REP).transpose(2, 0, 3, 1)
    return a.reshape(N_KV_HEADS, n_lat // tq, 1, KV_REP * tq)


def _attn_post_loss(o, z, xa, gate, w_out, w_out_t, final_g, target):
    n_lat, d = target.shape
    e = o.shape[1]
    head_of_lane = (jnp.arange(e)[:, None] // HEAD_DIM == jnp.arange(128)[None, :]).astype(BF16)

    def body(o_ref, z_ref, x_ref, gt_ref, w_ref, wt_ref, fg_ref, tg_ref, hl_ref,
             do_ref, dl_ref, dz_ref, dx_ref, loss_ref, dfg_ref, dgt_ref, dw_ref):
        _init_acc(pl.program_id(0) == 0, loss_ref, dfg_ref, dgt_ref, dw_ref)
        oo, zz, gate_t, fg = o_ref[...], z_ref[...], gt_ref[0], fg_ref[...]
        sz = _sigmoid(zz)
        silu_z = zz * sz
        m = oo * silu_z
        out = _dot(m, w_ref[...])
        x2 = x_ref[...] + gate_t * out
        r = lax.rsqrt(jnp.mean(x2 * x2, axis=-1, keepdims=True) + NORM_EPS)
        n = x2 * r
        err = n * fg - tg_ref[...]
        loss_ref[...] += 0.5 * jnp.sum(jnp.mean(err * err, axis=-1, keepdims=True), axis=0, keepdims=True)
        dy = err * (1.0 / d)
        dfg_ref[...] += _sum0(dy * n)
        dn = dy * fg
        dx2 = r * (dn - n * jnp.mean(dn * n, axis=-1, keepdims=True))
        dx_ref[...] = dx2
        dgt_ref[...] += _sum0(dx2 * out)
        dout = gate_t * dx2
        dw_ref[...] += _dot_t0(m, dout)
        dm = _dot(dout, wt_ref[...])
        do = dm * silu_z
        do_ref[...] = do.astype(BF16)
        prod = do * oo
        hi = prod.astype(BF16)
        lo = (prod - hi.astype(F32)).astype(BF16)
        dl_ref[...] = _dot(hi, hl_ref[...]) + _dot(lo, hl_ref[...])
        dz_ref[...] = dm * oo * (sz * (1.0 + zz * (1.0 - sz)))

    return pl.pallas_call(
        body, name="attn_post_loss", grid=(n_lat // ROW_TILE,),
        in_specs=[_rows(e), _rows_skip_ctx(e), _rows_skip_ctx(d), _lat_seg(d), _full(w_out.shape), _full(w_out_t.shape),
                  _full((1, d)), _rows(d), _full((e, 128))],
        out_specs=[_rows(e), _rows(128), _rows(e), _rows(d), _full((1, 1)), _full((1, d)), _full((1, d)), _full(w_out.shape)],
        out_shape=[_sds((n_lat, e), BF16), _sds((n_lat, 128)), _sds((n_lat, e)), _sds((n_lat, d)), _sds((1, 1)), _sds((1, d)),
                   _sds((1, d)), _sds(w_out.shape)],
        compiler_params=_cparams(1),
    )(o, z, xa, gate, w_out, w_out_t, final_g, target, head_of_lane)


def _attn_in_bwd(dq, dk, dv, dz, raw, xa, dx2, g, scale, shift, q_gain, k_gain, cos, sin, w_in_t):
    n_rows, d = xa.shape
    qk = Q_WIDTH + KV_WIDTH
    n_in = w_in_t.shape[0]

    def body(dq_ref, dk_ref, dv_ref, dz_ref, raw_ref, x_ref, dx2_ref, g_ref, sc_ref, sh_ref, qg_ref, kg_ref, cos_ref, sin_ref,
             wt_ref, hq_ref, hk_ref, dxa_ref, dw_ref, dqg_ref, dkg_ref, dg_ref, dsc_ref, dsh_ref):
        i = pl.program_id(0)
        _init_acc(i == 0, dw_ref, dqg_ref, dkg_ref, dg_ref)
        _init_acc(i <= 1, dsc_ref, dsh_ref)
        is_lat = (i > 0).astype(F32)
        x = x_ref[...]
        h = _rms_mod(x, g_ref[...], sc_ref[0], sh_ref[0])
        cos, sin = cos_ref[...], sin_ref[...]
        raw_t = raw_ref[...]
        dq_raw, dqg = _head_norm_rope_bwd(raw_t[:, :Q_WIDTH], qg_ref[...], _lanes(cos, Q_WIDTH), _lanes(sin, Q_WIDTH),
                                          hq_ref[...], dq_ref[...] * (SM_SCALE * is_lat))
        dk_raw, dkg = _head_norm_rope_bwd(raw_t[:, Q_WIDTH:], kg_ref[...], _lanes(cos, KV_WIDTH), _lanes(sin, KV_WIDTH),
                                          hk_ref[...], dk_ref[...])
        dqg_ref[...] += dqg
        dkg_ref[...] += dkg
        dproj = jnp.concatenate([dq_raw, dk_raw, dv_ref[...], dz_ref[...] * is_lat], axis=1)
        dh = _dot(dproj, wt_ref[...])
        dw_ref[...] += _dot_t0(h, dproj)
        dx, dg, dsc, dsh = _rms_mod_bwd(x, g_ref[...], sc_ref[0], dh)
        dg_ref[...] += dg
        dsc_ref[0] += dsc
        dsh_ref[0] += dsh
        dxa_ref[...] = dx + dx2_ref[...] * is_lat

    return pl.pallas_call(
        body, name="attn_in_bwd", grid=(n_rows // ROW_TILE,),
        in_specs=[_rows_lat(Q_WIDTH), _rows(KV_WIDTH), _rows(KV_WIDTH), _rows_lat(Q_WIDTH), _rows(qk), _rows(d), _rows_lat(d),
                  _full((1, d)), _seg(d), _seg(d), _full((1, Q_WIDTH)), _full((1, KV_WIDTH)), _rows(128), _rows(128),
                  _full(w_in_t.shape), _full((Q_WIDTH, 128)), _full((KV_WIDTH, 128))],
        out_specs=[_rows(d), _full((d, n_in)), _full((1, Q_WIDTH)), _full((1, KV_WIDTH)), _full((1, d)), _seg(d), _seg(d)],
        out_shape=[_sds((n_rows, d)), _sds((d, n_in)), _sds((1, Q_WIDTH)), _sds((1, KV_WIDTH)), _sds((1, d)),
                   _sds((2, 1, d)), _sds((2, 1, d))],
        compiler_params=_cparams(1),
    )(dq, dk, dv, dz, raw, xa, dx2, g, scale, shift, q_gain, k_gain, cos, sin, w_in_t,
      _head_of_lane(Q_WIDTH), _head_of_lane(KV_WIDTH))


def _heads_major(a, n_heads):
    return a.reshape(a.shape[0], n_heads, HEAD_DIM).transpose(1, 0, 2)


def _tokens_major(a):
    return a.transpose(1, 0, 2).reshape(a.shape[1], a.shape[0] * HEAD_DIM)


def _local_step(x, ctx, target, mods, norm_g, ssm, w_ssm_in, w_glu, b_glu, w_ssm_out, w_attn_in, q_norm, k_norm, w_attn_out,
                final_g, attn_exchange=None, post_exchange=None):
    n_ctx, d = ctx.shape
    assert n_ctx == ROW_TILE
    n_lat = x.shape[0]
    (shift0, scale0, gate0), (shift1, scale1, gate1) = mods
    g0, g1, fg = norm_g[0:1], norm_g[1:2], final_g.reshape(1, d)
    b_glu = b_glu.reshape(1, -1)
    q_gain = jnp.tile(q_norm.reshape(1, HEAD_DIM), (1, N_Q_HEADS))
    k_gain = jnp.tile(k_norm.reshape(1, HEAD_DIM), (1, N_KV_HEADS))
    cos, sin = _rope_tables(n_ctx, n_lat)

    xa0 = jnp.concatenate([ctx, x], axis=0)
    u, z0 = _ssm_in(xa0, g0, scale0, shift0, w_ssm_in)
    gather = lambda ex: (ex[0], False) if ex else None
    (y0, y1), saved, gathered = _s5_forward(u, ssm, n_ctx, (gather(attn_exchange), gather(post_exchange)))
    if attn_exchange:
        w_attn_in, w_attn_out = attn_exchange[1](gathered[0])
    if post_exchange:
        w_glu, w_ssm_out = post_exchange[1](gathered[1])
    xa1 = _ssm_post(xa0, y0, y1, z0, gate0, w_glu, b_glu, w_ssm_out)

    q, k, v, z1, raw = _attn_in(xa1, g1, scale1, shift1, w_attn_in, q_gain, k_gain, cos, sin)
    tq, tk, tk_fwd = _q_tile(n_lat), _kv_tile(n_ctx + n_lat), _kv_tile_fwd(n_ctx + n_lat)
    k_h, v_h = _heads_major(k, N_KV_HEADS), _heads_major(v, N_KV_HEADS)
    k_b, v_b = k_h.reshape(N_KV_HEADS, -1, tk, HEAD_DIM), v_h.reshape(N_KV_HEADS, -1, tk, HEAD_DIM)
    v_t = v_h.reshape(N_KV_HEADS, -1, tk_fwd, HEAD_DIM).transpose(0, 1, 3, 2)
    v_t_ones = jnp.concatenate([v_t, jnp.ones(v_t.shape[:2] + (16, tk_fwd), BF16)], axis=2)
    o, lse_t = _flash_fwd(q, k_h.reshape(N_KV_HEADS, -1, tk_fwd, HEAD_DIM), v_t_ones)
    do, delta, dz1, dx2, loss, d_fg, d_gate1, d_w_attn_out = _attn_post_loss(
        o, z1, xa1, gate1, w_attn_out, w_attn_out.T, fg, target)

    dq, dk_b, dv_b = _flash_bwd(q, k_b, k_b.transpose(0, 1, 3, 2), v_b, do, lse_t, _to_lane_stacked(delta[:, :N_Q_HEADS], tq))
    keys_major = lambda a: _tokens_major(a.reshape(N_KV_HEADS, -1, HEAD_DIM))
    dxa1, d_w_attn_in, d_qg, d_kg, d_g1, d_scale1, d_shift1 = _attn_in_bwd(
        dq, keys_major(dk_b), keys_major(dv_b), dz1, raw, xa1, dx2, g1, scale1, shift1,
        q_gain, k_gain, cos, sin, w_attn_in.T)
    dy, dz0, d_gate0, d_w_ssm_out, d_w_glu, d_b_glu = _ssm_post_bwd(
        dxa1, y0, y1, z0, gate0, w_glu, b_glu, w_ssm_out, w_glu.T, w_ssm_out.T)
    scatter = lambda ex, *g: (ex[2](*g), True) if ex else None
    (du0, du1), d_ssm, parts = _s5_backward(
        u, dy, ssm, saved, n_ctx,
        (scatter(attn_exchange, d_w_attn_in, d_w_attn_out), scatter(post_exchange, d_w_glu, d_w_ssm_out)))
    grad_x, d_w_ssm_in, d_g0, d_scale0, d_shift0 = _ssm_in_bwd(du0, du1, dz0, xa0, dxa1, g0, scale0, shift0, w_ssm_in.T)

    d_gate1_seg = jnp.concatenate([jnp.zeros((1, 1, d), F32), d_gate1.reshape(1, 1, d)], axis=0)
    grads = dict(
        norm_g=jnp.concatenate([d_g0, d_g1], axis=0), ssm_w_in=d_w_ssm_in, ssm=d_ssm, ssm_b_glu=d_b_glu.reshape(-1),
        attn_q_norm=d_qg.reshape(N_Q_HEADS, HEAD_DIM).sum(0), attn_k_norm=d_kg.reshape(N_KV_HEADS, HEAD_DIM).sum(0),
        final_norm_g=d_fg.reshape(-1))
    if attn_exchange:
        grads.update(attn_parts=parts[0])
    else:
        grads.update(attn_w_in=d_w_attn_in, attn_w_out=d_w_attn_out)
    if post_exchange:
        grads.update(post_parts=parts[1])
    else:
        grads.update(ssm_w_glu=d_w_glu, ssm_w_out=d_w_ssm_out)
    d_mods = ((d_shift0, d_scale0, d_gate0), (d_shift1, d_scale1, d_gate1_seg))
    return loss[0, 0], grad_x, grads, d_mods


def _my_index():
    return 4 * lax.axis_index("x") + 2 * lax.axis_index("y") + lax.axis_index("c")


def _peer(k):
    mx, my, mc = lax.axis_index("x"), lax.axis_index("y"), lax.axis_index("c")
    px = 1 - mx if k & 4 else mx
    py = 1 - my if k & 2 else my
    pc = 1 - mc if k & 1 else mc
    return (px, py, pc), 4 * px + 2 * py + pc


HBM_SPEC = pl.BlockSpec(memory_space=pl.ANY)


def _exchange(x, name, all_to_all):
    def body(x_ref, out_ref, send_sems, recv_sems, local_sem):
        _exchange_copies(all_to_all, x_ref, out_ref, send_sems, recv_sems, local_sem, start=True)
        _exchange_copies(all_to_all, x_ref, out_ref, send_sems, recv_sems, local_sem, start=False)

    return pl.pallas_call(
        body, name=name, in_specs=[HBM_SPEC], out_specs=HBM_SPEC,
        out_shape=_exchange_out_shape(x, all_to_all), scratch_shapes=_exchange_semaphores(),
    )(x)


def _exchange_out_shape(x, all_to_all):
    return _sds((N_DEV,) + tuple(x.shape[1:] if all_to_all else x.shape), x.dtype)


def _exchange_semaphores():
    return [pltpu.SemaphoreType.DMA((N_DEV - 1,)), pltpu.SemaphoreType.DMA((N_DEV - 1,)), pltpu.SemaphoreType.DMA]


def _exchange_copies(all_to_all, x_ref, out_ref, send_sems, recv_sems, local_sem, start):
    me = _my_index()
    mine = pltpu.make_async_copy(x_ref.at[me] if all_to_all else x_ref, out_ref.at[me], local_sem)
    if start:
        mine.start()
    for k in range(1, N_DEV):
        peer, peer_idx = _peer(k)
        send = pltpu.make_async_remote_copy(
            src_ref=x_ref.at[peer_idx] if all_to_all else x_ref, dst_ref=out_ref.at[me],
            send_sem=send_sems.at[k - 1], recv_sem=recv_sems.at[k - 1], device_id=peer, device_id_type=MESH_IDS)
        if start:
            send.start()
        else:
            pltpu.make_async_remote_copy(
                src_ref=x_ref.at[me] if all_to_all else x_ref, dst_ref=out_ref.at[peer_idx],
                send_sem=send_sems.at[k - 1], recv_sem=recv_sems.at[k - 1], device_id=peer,
                device_id_type=MESH_IDS).wait_recv()
            send.wait_send()
    if not start:
        mine.wait()


def _ride(rider, first, last, refs):
    @pl.when(first)
    def _():
        _exchange_copies(rider[1], *refs, start=True)

    @pl.when(last)
    def _():
        _exchange_copies(rider[1], *refs, start=False)


MOD_ROWS = 16
CTX_ROW = N_DEV


def _mod_fwd(cond, w_shard, b_cols):
    n_layers, d, cols = w_shard.shape

    def body(c_ref, w_ref, b_ref, o_ref):
        c = c_ref[...]
        s = c * _sigmoid(c)
        for i in range(n_layers):
            o_ref[i] = _dot(s, w_ref[i]) + b_ref[i]

    return pl.pallas_call(
        body, name="mod_fwd", out_shape=_sds((n_layers, MOD_ROWS, cols)),
        compiler_params=pltpu.CompilerParams(vmem_limit_bytes=VMEM_LIMIT),
    )(cond, w_shard, b_cols.reshape(n_layers, 1, cols))


def _mod_bwd(cond, d_lat_cols, d_ctx_cols, w_shard):
    n_layers, d, cols = w_shard.shape

    def body(c_ref, dl_ref, dc_ref, w_ref, dw_ref, dcc_ref):
        c = c_ref[...]
        sg = _sigmoid(c)
        s = c * sg
        d_s = jnp.zeros((MOD_ROWS, d), F32)
        for i in range(n_layers):
            d_ctx = dc_ref[0, i]
            for j in range(1, N_DEV):
                d_ctx = d_ctx + dc_ref[j, i]
            dm = jnp.concatenate([dl_ref[i], d_ctx, jnp.zeros((MOD_ROWS - N_DEV - 1, cols), F32)], axis=0)
            dw_ref[i] = _dot_t0(s, dm)
            d_s = d_s + _dot_t1(dm, w_ref[i])
        d_c = d_s * (sg * (1.0 + c * (1.0 - sg)))
        dcc_ref[...] = d_c[CTX_ROW:CTX_ROW + 1]

    return pl.pallas_call(
        body, name="mod_bwd", out_shape=[_sds((n_layers, d, cols)), _sds((1, d))],
        compiler_params=pltpu.CompilerParams(vmem_limit_bytes=VMEM_LIMIT),
    )(cond, d_lat_cols, d_ctx_cols, w_shard)


ADAM_TILE = 512


def _adamw(w, g_parts, m, v, name):
    n_parts, n_rows, lanes = g_parts.shape
    tile = min(ADAM_TILE, n_rows)
    assert n_rows % tile == 0
    c1 = 1.0 - ADAM_B1 ** ADAM_STEP
    c2 = 1.0 - ADAM_B2 ** ADAM_STEP

    def body(w_ref, g_ref, m_ref, v_ref, go_ref, d_ref, mo_ref, vo_ref):
        g = g_ref[0].astype(F32)
        for p in range(1, n_parts):
            g = g + g_ref[p].astype(F32)
        m_new = ADAM_B1 * m_ref[...] + (1.0 - ADAM_B1) * g
        v_new = ADAM_B2 * v_ref[...] + (1.0 - ADAM_B2) * (g * g)
        go_ref[...] = g
        mo_ref[...] = m_new
        vo_ref[...] = v_new
        d_ref[...] = -ADAM_LR * ((m_new / c1) / (jnp.sqrt(v_new / c2) + ADAM_EPS) + ADAM_WD * w_ref[...])

    row = pl.BlockSpec((tile, lanes), lambda i: (i, 0))
    return pl.pallas_call(
        body, name=name, grid=(n_rows // tile,),
        in_specs=[row, pl.BlockSpec((n_parts, tile, lanes), lambda i: (0, i, 0)), row, row],
        out_specs=[row] * 4, out_shape=[_sds((n_rows, lanes))] * 4,
        compiler_params=_cparams(1),
    )(w, g_parts, m, v)


def _sum_parts(parts):
    n_parts, n_rows, lanes = parts.shape

    def body(p_ref, o_ref):
        acc = p_ref[0]
        for p in range(1, n_parts):
            acc = acc + p_ref[p]
        o_ref[...] = acc

    return pl.pallas_call(body, name="sum_parts", out_shape=_sds((n_rows, lanes)))(parts)


def _pack(arrays, row_multiple):
    parts = []
    for a in arrays:
        flat = a.reshape(-1)
        parts.append(jnp.pad(flat, (0, (-flat.shape[0]) % 1024)))
    flat = jnp.concatenate(parts)
    flat = jnp.pad(flat, (0, (-flat.shape[0]) % (row_multiple * 128)))
    return flat.reshape(-1, 128)


def _unpack(packed, shapes):
    flat = packed.reshape(-1)
    out, pos = [], 0
    for s in shapes:
        n = math.prod(s)
        out.append(flat[pos:pos + n].reshape(s))
        pos += n + (-n) % 1024
    return out


WEIGHT_NAMES = ['c_ctx', 'w_mod', 'b_mod', 'norm_g', 'ssm_w_in', 'ssm_a_re', 'ssm_a_im', 'ssm_log_dt', 'ssm_b_re', 'ssm_b_im',
                'ssm_c_re', 'ssm_c_im', 'ssm_d', 'ssm_w_glu', 'ssm_b_glu', 'ssm_w_out', 'attn_w_in', 'attn_q_norm',
                'attn_k_norm', 'attn_w_out', 'final_norm_g']
FIRST_SHARDED = ['ssm_w_in']
POST_SHARDED = ['ssm_w_glu', 'ssm_w_out']
ATTN_SHARDED = ['attn_w_in', 'attn_w_out']
SHARDED = FIRST_SHARDED + POST_SHARDED + ATTN_SHARDED
COLUMN_SHARDED = ('ssm_w_in', 'attn_w_in')
REPLICATED = ['c_ctx', 'b_mod', 'norm_g', 'ssm_a_re', 'ssm_a_im', 'ssm_log_dt', 'ssm_b_re', 'ssm_b_im', 'ssm_c_re', 'ssm_c_im',
              'ssm_d', 'ssm_b_glu', 'attn_q_norm', 'attn_k_norm', 'final_norm_g']
SSM_NAMES = ['ssm_a_re', 'ssm_a_im', 'ssm_log_dt', 'ssm_b_re', 'ssm_b_im', 'ssm_c_re', 'ssm_c_im', 'ssm_d']


def _full_from_shards(gathered, name, shard_shape):
    rows, cols = shard_shape
    w = gathered.reshape(N_DEV, rows, cols)
    if name in COLUMN_SHARDED:
        return w.transpose(1, 0, 2).reshape(rows, N_DEV * cols)
    return w.reshape(N_DEV * rows, cols)


def _shards_from_full(g, name):
    if name in COLUMN_SHARDED:
        rows, cols = g.shape
        g = g.reshape(rows, N_DEV, cols // N_DEV).transpose(1, 0, 2)
    return g.reshape(N_DEV, -1, 128)


def kernel(x, c, ctx, c_ctx, w_mod, b_mod, norm_g, ssm_w_in, ssm_a_re, ssm_a_im, ssm_log_dt, ssm_b_re, ssm_b_im, ssm_c_re, ssm_c_im, ssm_d, ssm_w_glu, ssm_b_glu, ssm_w_out, attn_w_in, attn_q_norm, attn_k_norm, attn_w_out, final_norm_g, loss_target, m_c_ctx, m_w_mod, m_b_mod, m_norm_g, m_ssm_w_in, m_ssm_a_re, m_ssm_a_im, m_ssm_log_dt, m_ssm_b_re, m_ssm_b_im, m_ssm_c_re, m_ssm_c_im, m_ssm_d, m_ssm_w_glu, m_ssm_b_glu, m_ssm_w_out, m_attn_w_in, m_attn_q_norm, m_attn_k_norm, m_attn_w_out, m_final_norm_g, v_c_ctx, v_w_mod, v_b_mod, v_norm_g, v_ssm_w_in, v_ssm_a_re, v_ssm_a_im, v_ssm_log_dt, v_ssm_b_re, v_ssm_b_im, v_ssm_c_re, v_ssm_c_im, v_ssm_d, v_ssm_w_glu, v_ssm_b_glu, v_ssm_w_out, v_attn_w_in, v_attn_q_norm, v_attn_k_norm, v_attn_w_out, v_final_norm_g):
    env = dict(locals())
    weights = {n: env[n] for n in WEIGHT_NAMES}
    mom_m = {n: env["m_" + n] for n in WEIGHT_NAMES}
    mom_v = {n: env["v_" + n] for n in WEIGHT_NAMES}
    d = D_MODEL
    me = _my_index()
    mod_cols = w_mod.shape[-1]

    c_all = _exchange(c.reshape(8, d // 8), "gather_c", False).reshape(N_DEV, d)
    cond = jnp.concatenate([c_all, c_ctx.reshape(1, d), jnp.zeros((MOD_ROWS - N_DEV - 1, d), F32)], axis=0)
    shard_shapes = {n: weights[n].shape[1:] for n in SHARDED}
    pack_shards = lambda names: _pack([weights[n] for n in names], 1).astype(BF16)

    def unpack_full(gathered, names):
        full, pos = [], 0
        for n in names:
            rows = math.prod(shard_shapes[n]) // 128
            full.append(_full_from_shards(gathered[:, pos:pos + rows], n, shard_shapes[n]))
            pos += rows
        return full

    def exchange_of(names):
        return (pack_shards(names), lambda gathered: unpack_full(gathered, names),
                lambda *grads: jnp.concatenate([_shards_from_full(t, n) for t, n in zip(grads, names)], axis=1).astype(BF16))

    (w_ssm_in,) = unpack_full(_exchange(pack_shards(FIRST_SHARDED), "gather_ssm_w_in", False), FIRST_SHARDED)

    b_cols = lax.dynamic_slice(b_mod, (0, me * mod_cols), (2, mod_cols))
    mod_shard = _mod_fwd(cond, w_mod, b_cols)
    mod_all = _exchange(mod_shard.reshape(2 * MOD_ROWS, mod_cols), "gather_mod", False)
    mod_full = mod_all.reshape(N_DEV, 2, MOD_ROWS, mod_cols).transpose(1, 2, 0, 3).reshape(2, MOD_ROWS, 3 * d)
    lat_rows = lax.dynamic_slice(mod_full, (0, me, 0), (2, 1, 3 * d))
    mods = []
    for i in range(2):
        seg = jnp.stack([mod_full[i, CTX_ROW:CTX_ROW + 1], lat_rows[i]])
        mods.append((seg[:, :, :d], seg[:, :, d:2 * d], seg[:, :, 2 * d:]))

    ssm = tuple(weights[n][0] for n in SSM_NAMES)
    loss, grad_x, g, d_mods = _local_step(
        x[0], ctx[0], loss_target[0], mods, norm_g, ssm, w_ssm_in, None, ssm_b_glu[0], None,
        None, attn_q_norm[0], attn_k_norm[0], None, final_norm_g, exchange_of(ATTN_SHARDED), exchange_of(POST_SHARDED))

    d_rows = jnp.stack([jnp.concatenate(dm, axis=-1) for dm in d_mods])
    d_rows = jnp.concatenate([d_rows.reshape(4, 3 * d), jnp.zeros((4, 3 * d), F32)], axis=0)
    d_all = _exchange(d_rows, "gather_dmod", False)[:, :4].reshape(N_DEV, 2, 2, 3 * d)
    d_all = lax.dynamic_slice(d_all, (0, 0, 0, me * mod_cols), (N_DEV, 2, 2, mod_cols))
    d_w_mod, d_c_ctx = _mod_bwd(cond, d_all[:, :, 1].transpose(1, 0, 2), d_all[:, :, 0:1], w_mod)
    d_b_mod = jnp.stack([jnp.concatenate([t[0] + t[1] for t in dm], axis=-1).reshape(3 * d) for dm in d_mods])

    first_parts = _exchange(_shards_from_full(g['ssm_w_in'], 'ssm_w_in').astype(BF16), "scatter_ssm_w_in_grads", True)
    parts_of = {}
    for names, parts in ((FIRST_SHARDED, first_parts), (POST_SHARDED, g['post_parts']), (ATTN_SHARDED, g['attn_parts'])):
        pos = 0
        for n in names:
            rows = math.prod(shard_shapes[n]) // 128
            parts_of[n] = parts[:, pos:pos + rows].reshape((N_DEV,) + shard_shapes[n])
            pos += rows

    def update(n, g_parts):
        as_2d = lambda t: t.reshape(-1, t.shape[-1])
        res = _adamw(as_2d(weights[n]), g_parts, as_2d(mom_m[n]), as_2d(mom_v[n]), "adamw_" + n)
        return [t.reshape(weights[n].shape) for t in res]

    big = {n: update(n, parts_of[n]) for n in SHARDED}
    big['w_mod'] = update('w_mod', d_w_mod.reshape(1, -1, mod_cols))

    small = dict(zip(SSM_NAMES, g['ssm']))
    small.update(c_ctx=d_c_ctx, b_mod=d_b_mod, norm_g=g['norm_g'], ssm_b_glu=g['ssm_b_glu'], attn_q_norm=g['attn_q_norm'],
                 attn_k_norm=g['attn_k_norm'], final_norm_g=g['final_norm_g'])
    pack_small = lambda t, last: _pack([t[n] for n in REPLICATED] + [last], ADAM_TILE)
    no_weight = jnp.zeros((1,), F32)
    g_small = pack_small(small, loss.reshape(1))
    slices = _exchange(g_small.reshape(N_DEV, -1, 128), "scatter_small_grads", True)
    g_small = _exchange(_sum_parts(slices), "gather_small_grads", False).reshape(1, -1, 128)
    rep = _adamw(pack_small(weights, no_weight), g_small, pack_small(mom_m, no_weight), pack_small(mom_v, no_weight),
                 "adamw_replicated")
    rep = [_unpack(t, [weights[n].shape for n in REPLICATED] + [(1,)]) for t in rep]
    loss = rep[0][-1][0]

    results = []
    for kind in range(4):
        by_name = {n: res[kind] for n, res in big.items()}
        by_name.update(zip(REPLICATED, rep[kind]))
        results.extend(by_name[n] for n in WEIGHT_NAMES)
    return (loss, grad_x[None], *results)
```

```python
import functools
import math

import jax
import jax.numpy as jnp
from jax import lax
from jax.experimental import pallas as pl
from jax.experimental.pallas import tpu as pltpu

F32 = jnp.float32
BF16 = jnp.bfloat16

N_DEV = 8
D_MODEL = 1024
NORM_EPS = 1e-6
SSM_GROUP = 16
SSM_GROUPS = 64
SSM_STATE = 64
GROUPS_PER_BLOCK = 8
N_BLOCKS = SSM_GROUPS // GROUPS_PER_BLOCK
HALF = GROUPS_PER_BLOCK * SSM_STATE
HEAD_DIM = 64
N_Q_HEADS = 16
N_KV_HEADS = 4
KV_REP = N_Q_HEADS // N_KV_HEADS
KV_WIDTH = N_KV_HEADS * HEAD_DIM
GRID_W = 64
ROPE_THETA = 10000.0
ADAM_LR, ADAM_B1, ADAM_B2, ADAM_EPS, ADAM_WD, ADAM_STEP = 0.001, 0.9, 0.999, 1e-08, 0.01, 10

ROW_TILE = 256
SCAN_CHUNK = 256
VMEM_LIMIT = 56 * 1024 * 1024
MESH_IDS = pl.DeviceIdType.MESH


def _cparams(n_axes):
    return pltpu.CompilerParams(dimension_semantics=("arbitrary",) * n_axes, vmem_limit_bytes=VMEM_LIMIT)


def _dot(a, b):
    return jnp.dot(a.astype(BF16), b.astype(BF16), preferred_element_type=F32)


def _dot_t0(a, b):
    return lax.dot_general(a.astype(BF16), b.astype(BF16), (((0,), (0,)), ((), ())), preferred_element_type=F32)


def _dot_t1(a, b):
    return lax.dot_general(a.astype(BF16), b.astype(BF16), (((1,), (1,)), ((), ())), preferred_element_type=F32)


def _s5_prep(a_re, a_im, log_dt, b_re, b_im):
    dt = jnp.exp(log_dt)[:, None]
    ldr, ldi = a_re * dt, a_im * dt
    mag = jnp.exp(ldr)
    abar_re, abar_im = mag * jnp.cos(ldi), mag * jnp.sin(ldi)
    den = a_re * a_re + a_im * a_im
    num_re, num_im = abar_re - 1.0, abar_im
    coef_re = (num_re * a_re + num_im * a_im) / den
    coef_im = (num_im * a_re - num_re * a_im) / den
    bbar_re = coef_re[..., None] * b_re - coef_im[..., None] * b_im
    bbar_im = coef_re[..., None] * b_im + coef_im[..., None] * b_re
    return abar_re, abar_im, bbar_re, bbar_im


def _s5_blocks(abar_re, abar_im, bbar_re, bbar_im, c_re, c_im):
    eye = jnp.eye(GROUPS_PER_BLOCK, dtype=F32)
    bb = jnp.stack([bbar_re, bbar_im]).reshape(2, N_BLOCKS, GROUPS_PER_BLOCK, SSM_STATE, SSM_GROUP)
    b_blk = jnp.einsum('rqgph,gk->qghrkp', bb, eye).reshape(N_BLOCKS, 128, 2 * HALF)
    cc = jnp.stack([c_re, -c_im]).reshape(2, N_BLOCKS, GROUPS_PER_BLOCK, SSM_GROUP, SSM_STATE)
    c_blk = jnp.einsum('rqghp,gk->qrgpkh', cc, eye).reshape(N_BLOCKS, 2 * HALF, 128)
    abar = jnp.stack([abar_re.reshape(N_BLOCKS, HALF), abar_im.reshape(N_BLOCKS, HALF)])
    return abar, b_blk, c_blk


def _s5_unblock(d_b_blk, d_ct_blk):
    db = d_b_blk.reshape(N_BLOCKS, GROUPS_PER_BLOCK, SSM_GROUP, 2, GROUPS_PER_BLOCK, SSM_STATE)
    db = jnp.einsum('qghrgp->rqgph', db).reshape(2, SSM_GROUPS, SSM_STATE, SSM_GROUP)
    dc = d_ct_blk.reshape(N_BLOCKS, GROUPS_PER_BLOCK, SSM_GROUP, 2, GROUPS_PER_BLOCK, SSM_STATE)
    dc = jnp.einsum('qghrgp->rqghp', dc).reshape(2, SSM_GROUPS, SSM_GROUP, SSM_STATE)
    return db[0], db[1], dc[0], -dc[1]


def _scan_chunk_of_step(j, n_chunks, n_ctx_chunks, reverse):
    if not reverse:
        return j
    return jnp.where(j < n_ctx_chunks, n_ctx_chunks - 1 - j, n_chunks - 1 - j + n_ctx_chunks)


LANE_TILES = 2 * HALF // 128
RE_TILES = HALF // 128


def _tiles(v):
    return [v[:, l * 128:(l + 1) * 128] for l in range(v.shape[1] // 128)]


def _scatter_steps(s_ref, q, x):
    for l in range(LANE_TILES):
        s_ref[l, pl.ds(q, x.shape[0], stride=N_BLOCKS), :] = x[:, l * 128:(l + 1) * 128]


def _gather_steps(s_ref, q, n_steps):
    return jnp.concatenate([s_ref[l, pl.ds(q, n_steps, stride=N_BLOCKS), :] for l in range(LANE_TILES)], axis=1)


def _load_step(s_ref, t):
    row = pl.multiple_of(t * N_BLOCKS, N_BLOCKS)
    return [s_ref[l, pl.ds(row, N_BLOCKS), :] for l in range(LANE_TILES)]


def _store_step(s_ref, t, tiles):
    row = pl.multiple_of(t * N_BLOCKS, N_BLOCKS)
    for l in range(LANE_TILES):
        s_ref[l, pl.ds(row, N_BLOCKS), :] = tiles[l]


SCAN_UNROLL = 8
ADJOINT_UNROLL = 15


def _unrolled_loop(n_steps, unroll, step, carry):
    assert n_steps % unroll == 0

    def steps(i, c):
        for r in range(unroll):
            c = step(unroll * i + r, c)
        return c

    return lax.fori_loop(0, n_steps // unroll, steps, carry)


def _cmul_add(a, h, x, conj):
    re, im = [], []
    for l in range(RE_TILES):
        ar, ai, hr, hi = a[l], a[RE_TILES + l], h[l], h[RE_TILES + l]
        if conj:
            re.append(ar * hr + ai * hi + x[l])
            im.append(ar * hi - ai * hr + x[RE_TILES + l])
        else:
            re.append(ar * hr - ai * hi + x[l])
            im.append(ar * hi + ai * hr + x[RE_TILES + l])
    return re + im


def _s5_scan_fwd(u, abar, b_blk, c_blk, d_skip, n_ctx, reverse, rider=None):
    n_rows, width = u.shape
    tc = SCAN_CHUNK
    n_chunks, n_ctx_chunks = n_rows // tc, n_ctx // tc
    with_skip = d_skip is not None

    def body(*refs):
        if rider is not None:
            x_ref, ride_out, sems = refs[4 + with_skip], refs[7 + with_skip], refs[-3:]
            _ride(rider, pl.program_id(0) == 0, pl.program_id(0) == n_chunks - 1, (x_ref, ride_out) + tuple(sems))
            refs = refs[:4 + with_skip] + refs[5 + with_skip:7 + with_skip] + refs[8 + with_skip:-3]
        if with_skip:
            u_ref, a_ref, b_ref, c_ref, d_ref, y_ref, hb_ref, s_ref, h_ref = refs
        else:
            u_ref, a_ref, b_ref, c_ref, y_ref, hb_ref, s_ref, h_ref = refs
        j = pl.program_id(0)

        @pl.when(j == 0)
        def _():
            h_ref[...] = jnp.zeros_like(h_ref)

        hb_ref[0] = h_ref[...]
        for q in range(N_BLOCKS):
            _scatter_steps(s_ref, q, _dot(u_ref[:, q * 128:(q + 1) * 128], b_ref[q]))
        a = _tiles(a_ref[0]) + _tiles(a_ref[1])

        def step(s, h):
            t = tc - 1 - s if reverse else s
            h = _cmul_add(a, h, _load_step(s_ref, t), conj=False)
            _store_step(s_ref, t, h)
            return h

        h = _unrolled_loop(tc, SCAN_UNROLL, step, _tiles(h_ref[...]))
        h_ref[...] = jnp.concatenate(h, axis=1)
        for q in range(N_BLOCKS):
            yq = _dot(_gather_steps(s_ref, q, tc), c_ref[q])
            if with_skip:
                yq = yq + d_ref[:, q * 128:(q + 1) * 128] * u_ref[:, q * 128:(q + 1) * 128]
            y_ref[:, q * 128:(q + 1) * 128] = yq

    chunk = functools.partial(_scan_chunk_of_step, n_chunks=n_chunks, n_ctx_chunks=n_ctx_chunks, reverse=reverse)
    full3 = lambda j: (0, 0, 0)
    in_specs = [pl.BlockSpec((tc, width), lambda j: (chunk(j), 0)),
                pl.BlockSpec((2, N_BLOCKS, HALF), full3),
                pl.BlockSpec((N_BLOCKS, 128, 2 * HALF), full3),
                pl.BlockSpec((N_BLOCKS, 2 * HALF, 128), full3)]
    args = [u, abar, b_blk.astype(BF16), c_blk.astype(BF16)]
    if with_skip:
        in_specs.append(pl.BlockSpec((1, width), lambda j: (0, 0)))
        args.append(d_skip.reshape(1, width))
    out_specs = [pl.BlockSpec((tc, width), lambda j: (chunk(j), 0)),
                 pl.BlockSpec((1, N_BLOCKS, 2 * HALF), lambda j: (chunk(j), 0, 0))]
    out_shape = [_sds((n_rows, width)), _sds((n_chunks, N_BLOCKS, 2 * HALF))]
    scratch = [pltpu.VMEM((LANE_TILES, tc * N_BLOCKS, 128), F32), pltpu.VMEM((N_BLOCKS, 2 * HALF), F32)]
    if rider is not None:
        in_specs.append(HBM_SPEC)
        args.append(rider[0])
        out_specs.append(HBM_SPEC)
        out_shape.append(_exchange_out_shape(*rider))
        scratch += _exchange_semaphores()
    return pl.pallas_call(
        body, name="s5_scan_fwd_rev" if reverse else "s5_scan_fwd",
        grid=(n_chunks,), in_specs=in_specs, out_specs=out_specs, out_shape=out_shape, scratch_shapes=scratch,
        compiler_params=_cparams(1),
    )(*args)


def _s5_scan_bwd(u, dy, hb, abar, b_blk, c_blk, d_skip, n_ctx, reverse, rider=None):
    n_rows, width = u.shape
    tc = SCAN_CHUNK
    n_chunks, n_ctx_chunks = n_rows // tc, n_ctx // tc
    with_skip = d_skip is not None
    n_in, n_out = 7 + with_skip, 4 + with_skip

    def body(*refs):
        if rider is not None:
            x_ref, ride_out, sems = refs[n_in], refs[n_in + 1 + n_out], refs[-3:]
            _ride(rider, pl.program_id(0) == 0, pl.program_id(0) == n_chunks - 1, (x_ref, ride_out) + tuple(sems))
            refs = refs[:n_in] + refs[n_in + 1:n_in + 1 + n_out] + refs[n_in + 2 + n_out:-3]
        if with_skip:
            (u_ref, dy_ref, hb_ref, a_ref, b_ref, bt_ref, ct_ref, d_ref,
             du_ref, da_ref, db_ref, dct_ref, dd_ref, sh_ref, sg_ref, g_ref) = refs
        else:
            (u_ref, dy_ref, hb_ref, a_ref, b_ref, bt_ref, ct_ref,
             du_ref, da_ref, db_ref, dct_ref, sh_ref, sg_ref, g_ref) = refs
        j = pl.program_id(0)

        @pl.when(j == 0)
        def _():
            g_ref[...] = jnp.zeros_like(g_ref)
            da_ref[...] = jnp.zeros_like(da_ref)
            db_ref[...] = jnp.zeros_like(db_ref)
            dct_ref[...] = jnp.zeros_like(dct_ref)
            if with_skip:
                dd_ref[...] = jnp.zeros_like(dd_ref)

        for q in range(N_BLOCKS):
            _scatter_steps(sh_ref, q, _dot(u_ref[:, q * 128:(q + 1) * 128], b_ref[q]))
            _scatter_steps(sg_ref, q, _dot(dy_ref[:, q * 128:(q + 1) * 128], ct_ref[q]))
        a = _tiles(a_ref[0]) + _tiles(a_ref[1])
        time_of = (lambda s: tc - 1 - s) if reverse else (lambda s: s)

        def fwd_step(s, h):
            h = _cmul_add(a, h, _load_step(sh_ref, time_of(s)), conj=False)
            _store_step(sh_ref, time_of(s), h)
            return h

        h0 = _tiles(hb_ref[0])
        _unrolled_loop(tc, SCAN_UNROLL, fwd_step, h0)

        def adj(t, h_prev, carry):
            g, da = carry
            g = _cmul_add(a, g, _load_step(sg_ref, t), conj=True)
            _store_step(sg_ref, t, g)
            da_re = [da[l] + g[l] * h_prev[l] + g[RE_TILES + l] * h_prev[RE_TILES + l] for l in range(RE_TILES)]
            da_im = [da[RE_TILES + l] + g[RE_TILES + l] * h_prev[l] - g[l] * h_prev[RE_TILES + l] for l in range(RE_TILES)]
            return g, da_re + da_im

        def bwd_step(i, carry):
            s = tc - 1 - i
            return adj(time_of(s), _load_step(sh_ref, time_of(s - 1)), carry)

        carry = (_tiles(g_ref[...]), _tiles(da_ref[0]) + _tiles(da_ref[1]))
        carry = _unrolled_loop(tc - 1, ADJOINT_UNROLL, bwd_step, carry)
        g, da = adj(time_of(0), h0, carry)
        g_ref[...] = jnp.concatenate(g, axis=1)
        da_ref[0] = jnp.concatenate(da[:RE_TILES], axis=1)
        da_ref[1] = jnp.concatenate(da[RE_TILES:], axis=1)

        for q in range(N_BLOCKS):
            cols = slice(q * 128, (q + 1) * 128)
            uq, dyq = u_ref[:, cols], dy_ref[:, cols]
            gq = _gather_steps(sg_ref, q, tc)
            duq = _dot(gq, bt_ref[q])
            if with_skip:
                duq = duq + d_ref[:, cols] * dyq
                dd_ref[:, cols] += jnp.sum(dyq * uq, axis=0, keepdims=True)
            du_ref[:, cols] = duq
            db_ref[q] += _dot_t0(uq, gq)
            dct_ref[q] += _dot_t0(dyq, _gather_steps(sh_ref, q, tc))

    def chunk(j):
        return _scan_chunk_of_step(n_chunks - 1 - j, n_chunks, n_ctx_chunks, reverse)

    full2 = lambda j: (0, 0)
    full3 = lambda j: (0, 0, 0)
    row = pl.BlockSpec((tc, width), lambda j: (chunk(j), 0))
    in_specs = [row, row,
                pl.BlockSpec((1, N_BLOCKS, 2 * HALF), lambda j: (chunk(j), 0, 0)),
                pl.BlockSpec((2, N_BLOCKS, HALF), full3),
                pl.BlockSpec((N_BLOCKS, 128, 2 * HALF), full3),
                pl.BlockSpec((N_BLOCKS, 2 * HALF, 128), full3),
                pl.BlockSpec((N_BLOCKS, 128, 2 * HALF), full3)]
    args = [u, dy, hb, abar, b_blk.astype(BF16), jnp.swapaxes(b_blk, 1, 2).astype(BF16),
            jnp.swapaxes(c_blk, 1, 2).astype(BF16)]
    out_specs = [row,
                 pl.BlockSpec((2, N_BLOCKS, HALF), full3),
                 pl.BlockSpec((N_BLOCKS, 128, 2 * HALF), full3),
                 pl.BlockSpec((N_BLOCKS, 128, 2 * HALF), full3)]
    out_shape = [jax.ShapeDtypeStruct((n_rows, width), F32),
                 jax.ShapeDtypeStruct((2, N_BLOCKS, HALF), F32),
                 jax.ShapeDtypeStruct((N_BLOCKS, 128, 2 * HALF), F32),
                 jax.ShapeDtypeStruct((N_BLOCKS, 128, 2 * HALF), F32)]
    if with_skip:
        in_specs.append(pl.BlockSpec((1, width), full2))
        args.append(d_skip.reshape(1, width))
        out_specs.append(pl.BlockSpec((1, width), full2))
        out_shape.append(jax.ShapeDtypeStruct((1, width), F32))
    scratch = [pltpu.VMEM((LANE_TILES, tc * N_BLOCKS, 128), F32), pltpu.VMEM((LANE_TILES, tc * N_BLOCKS, 128), F32),
               pltpu.VMEM((N_BLOCKS, 2 * HALF), F32)]
    if rider is not None:
        in_specs.append(HBM_SPEC)
        args.append(rider[0])
        out_specs.append(HBM_SPEC)
        out_shape.append(_exchange_out_shape(*rider))
        scratch += _exchange_semaphores()
    return pl.pallas_call(
        body, name="s5_scan_bwd_rev" if reverse else "s5_scan_bwd",
        grid=(n_chunks,), in_specs=in_specs, out_specs=out_specs, out_shape=out_shape, scratch_shapes=scratch,
        compiler_params=_cparams(1),
    )(*args)


def _s5_dir_params(d, a_re, a_im, log_dt, b_re, b_im):
    return a_re[d], a_im[d], log_dt[d], b_re[d], b_im[d]


def _s5_forward(u, ssm, n_ctx, riders=(None, None)):
    a_re, a_im, log_dt, b_re, b_im, c_re, c_im, d_skip = ssm
    outs, saved, carried = [], [], [None, None]
    for d in range(2):
        prep = _s5_prep(*_s5_dir_params(d, a_re, a_im, log_dt, b_re, b_im))
        abar, b_blk, c_blk = _s5_blocks(*prep, c_re[d], c_im[d])
        res = _s5_scan_fwd(u, abar, b_blk, c_blk, d_skip if d == 0 else None, n_ctx, reverse=(d == 1), rider=riders[d])
        if riders[d] is not None:
            carried[d] = res[2]
        outs.append(res[0])
        saved.append((res[1], abar, b_blk, c_blk))
    return outs, saved, carried


def _s5_backward(u, dy, ssm, saved, n_ctx, riders=(None, None)):
    a_re, a_im, log_dt, b_re, b_im, c_re, c_im, d_skip = ssm
    dus, grads = [], [[] for _ in range(7)]
    d_d, carried = None, [None, None]
    for d in range(2):
        hb, abar, b_blk, c_blk = saved[d]
        res = _s5_scan_bwd(u, dy, hb, abar, b_blk, c_blk, d_skip if d == 0 else None, n_ctx, reverse=(d == 1),
                           rider=riders[d])
        if riders[d] is not None:
            carried[d], res = res[-1], res[:-1]
        if d == 0:
            du, d_abar, d_b_blk, d_ct_blk, d_d = res
        else:
            du, d_abar, d_b_blk, d_ct_blk = res
        dus.append(du)
        dbb_re, dbb_im, dc_re, dc_im = _s5_unblock(d_b_blk, d_ct_blk)
        _, vjp = jax.vjp(_s5_prep, *_s5_dir_params(d, a_re, a_im, log_dt, b_re, b_im))
        shape = (SSM_GROUPS, SSM_STATE)
        g5 = vjp((d_abar[0].reshape(shape), d_abar[1].reshape(shape), dbb_re, dbb_im))
        for k, g in enumerate(tuple(g5) + (dc_re, dc_im)):
            grads[k].append(g)
    grads = [jnp.stack(g) for g in grads]
    return dus, grads + [d_d.reshape(-1)], carried


INV_SQRT2 = 0.7071067811865476
INV_SQRT_2PI = 0.3989422804014327


def _rows(cols):
    return pl.BlockSpec((ROW_TILE, cols), lambda i: (i, 0))


def _rows_skip_ctx(cols):
    return pl.BlockSpec((ROW_TILE, cols), lambda i: (i + 1, 0))


def _rows_lat(cols):
    return pl.BlockSpec((ROW_TILE, cols), lambda i: (jnp.maximum(i - 1, 0), 0))


def _full(shape):
    nd = len(shape)
    return pl.BlockSpec(shape, lambda i: (0,) * nd)


def _seg(cols):
    return pl.BlockSpec((1, 1, cols), lambda i: (jnp.minimum(i, 1), 0, 0))


def _lat_seg(cols):
    return pl.BlockSpec((1, 1, cols), lambda i: (1, 0, 0))


def _sds(shape, dtype=F32):
    return jax.ShapeDtypeStruct(shape, dtype)


def _sum0(x):
    return jnp.sum(x, axis=0, keepdims=True)


def _sigmoid(x):
    return jax.nn.sigmoid(x)


def _rms_mod(x, g, scale, shift):
    r = lax.rsqrt(jnp.mean(x * x, axis=-1, keepdims=True) + NORM_EPS)
    return (x * r * g) * (1.0 + scale) + shift


def _rms_mod_bwd(x, g, scale, dh):
    r = lax.rsqrt(jnp.mean(x * x, axis=-1, keepdims=True) + NORM_EPS)
    n = x * r
    dyg = dh * (1.0 + scale)
    dn = dyg * g
    dx = r * (dn - n * jnp.mean(dn * n, axis=-1, keepdims=True))
    return dx, _sum0(dyg * n), _sum0(dh * (n * g)), _sum0(dh)


def _head_of_lane(width):
    return (jnp.arange(width)[:, None] // HEAD_DIM == jnp.arange(128)[None, :]).astype(BF16)


def _split_dot(t, w, transposed):
    hi = t.astype(BF16)
    lo = (t - hi.astype(F32)).astype(BF16)
    f = _dot_t1 if transposed else _dot
    return f(hi, w) + f(lo, w)


def _head_sums(t, hl):
    return _split_dot(_split_dot(t, hl, False), hl, True)


def _rope_partner(x):
    n = x.shape[1]
    lane = lax.broadcasted_iota(jnp.int32, x.shape, 1)
    return jnp.where((lane & 16) == 0, pltpu.roll(x, n - 16, 1), pltpu.roll(x, 16, 1))


def _lanes(tab, width):
    return jnp.tile(tab, (1, width // tab.shape[1]))


def _head_norm_rope(x, gain, cos, sin, hl):
    r = lax.rsqrt(_head_sums(x * x, hl) * (1.0 / HEAD_DIM) + NORM_EPS)
    y = x * r * gain
    return y * cos + _rope_partner(y) * sin


def _head_norm_rope_bwd(x, gain, cos, sin, hl, dout):
    dy = dout * cos + _rope_partner(dout * sin)
    r = lax.rsqrt(_head_sums(x * x, hl) * (1.0 / HEAD_DIM) + NORM_EPS)
    n = x * r
    dn = dy * gain
    dx = r * (dn - n * (_head_sums(dn * n, hl) * (1.0 / HEAD_DIM)))
    return dx, _sum0(dy * n)


def _rope_tables(n_ctx, n_lat):
    t = jnp.arange(n_lat)
    pos = jnp.stack([(t // GRID_W).astype(F32), (t % GRID_W).astype(F32)], axis=1)
    n_freq = HEAD_DIM // 4
    freqs = ROPE_THETA ** (-jnp.arange(n_freq, dtype=F32) / n_freq)
    ang = pos[:, :, None] * freqs[None, None, :]
    cos = jnp.repeat(jnp.cos(ang)[:, :, None, :], 2, axis=2).reshape(n_lat, HEAD_DIM)
    sin = jnp.sin(ang)
    sin = jnp.stack([-sin, sin], axis=2).reshape(n_lat, HEAD_DIM)
    cos = jnp.concatenate([jnp.ones((n_ctx, HEAD_DIM), F32), cos], axis=0)
    sin = jnp.concatenate([jnp.zeros((n_ctx, HEAD_DIM), F32), sin], axis=0)
    return jnp.tile(cos, (1, 2)), jnp.tile(sin, (1, 2))


def _ssm_in(xa, g, scale, shift, w_in):
    n_rows, d = xa.shape
    e = w_in.shape[1] // 2

    def body(x_ref, g_ref, sc_ref, sh_ref, w_ref, u_ref, z_ref):
        h = _rms_mod(x_ref[...], g_ref[...], sc_ref[0], sh_ref[0])
        proj = _dot(h, w_ref[...])
        u_ref[...] = proj[:, :e]
        z_ref[...] = proj[:, e:]

    return pl.pallas_call(
        body, name="ssm_in", grid=(n_rows // ROW_TILE,),
        in_specs=[_rows(d), _full((1, d)), _seg(d), _seg(d), _full(w_in.shape)],
        out_specs=[_rows(e), _rows(e)], out_shape=[_sds((n_rows, e)), _sds((n_rows, e))],
        compiler_params=_cparams(1),
    )(xa, g, scale, shift, w_in)


def _s5_post_math(y, z, w_glu, b_glu, w_out):
    er = lax.erf(y * INV_SQRT2)
    g = 0.5 * y * (1.0 + er)
    sg = _sigmoid(_dot(g, w_glu) + b_glu)
    g2 = g * sg
    sz = _sigmoid(z)
    silu_z = z * sz
    m = g2 * silu_z
    return er, g, sg, g2, sz, silu_z, m, _dot(m, w_out)


def _ssm_post(xa, y0, y1, z, gate, w_glu, b_glu, w_out):
    n_rows, d = xa.shape
    e = z.shape[1]

    def body(x_ref, y0_ref, y1_ref, z_ref, gt_ref, wg_ref, bg_ref, wo_ref, o_ref):
        out = _s5_post_math(y0_ref[...] + y1_ref[...], z_ref[...], wg_ref[...], bg_ref[...], wo_ref[...])[-1]
        o_ref[...] = x_ref[...] + gt_ref[0] * out

    return pl.pallas_call(
        body, name="ssm_post", grid=(n_rows // ROW_TILE,),
        in_specs=[_rows(d), _rows(e), _rows(e), _rows(e), _seg(d), _full(w_glu.shape), _full((1, e)), _full(w_out.shape)],
        out_specs=_rows(d), out_shape=_sds((n_rows, d)),
        compiler_params=_cparams(1),
    )(xa, y0, y1, z, gate, w_glu, b_glu, w_out)


def _init_acc(first, *refs):
    @pl.when(first)
    def _():
        for r in refs:
            r[...] = jnp.zeros_like(r)


def _ssm_post_bwd(dxa, y0, y1, z, gate, w_glu, b_glu, w_out):
    n_rows, d = dxa.shape
    e = z.shape[1]

    def body(dx_ref, y0_ref, y1_ref, z_ref, gt_ref, wg_ref, bg_ref, wo_ref,
             dy_ref, dz_ref, dgt_ref, dwo_ref, dwg_ref, dbg_ref):
        i = pl.program_id(0)
        _init_acc(i == 0, dwo_ref, dwg_ref, dbg_ref)
        _init_acc(i <= 1, dgt_ref)
        y, zz = y0_ref[...] + y1_ref[...], z_ref[...]
        er, g, sg, g2, sz, silu_z, m, out = _s5_post_math(y, zz, wg_ref[...], bg_ref[...], wo_ref[...])
        dxa_t = dx_ref[...]
        dgt_ref[0] += _sum0(dxa_t * out)
        dout = gt_ref[0] * dxa_t
        dm = _dot_t1(dout, wo_ref[...])
        dwo_ref[...] += _dot_t0(m, dout)
        dg2 = dm * silu_z
        dz_ref[...] = dm * g2 * (sz * (1.0 + zz * (1.0 - sz)))
        dt = dg2 * g * sg * (1.0 - sg)
        dwg_ref[...] += _dot_t0(g, dt)
        dbg_ref[...] += _sum0(dt)
        dg = dg2 * sg + _dot_t1(dt, wg_ref[...])
        dy_ref[...] = dg * (0.5 * (1.0 + er) + y * jnp.exp(-0.5 * y * y) * INV_SQRT_2PI)

    return pl.pallas_call(
        body, name="ssm_post_bwd", grid=(n_rows // ROW_TILE,),
        in_specs=[_rows(d), _rows(e), _rows(e), _rows(e), _seg(d), _full(w_glu.shape), _full((1, e)), _full(w_out.shape)],
        out_specs=[_rows(e), _rows(e), _seg(d), _full(w_out.shape), _full(w_glu.shape), _full((1, e))],
        out_shape=[_sds((n_rows, e)), _sds((n_rows, e)), _sds((2, 1, d)), _sds(w_out.shape), _sds(w_glu.shape), _sds((1, e))],
        compiler_params=_cparams(1),
    )(dxa, y0, y1, z, gate, w_glu, b_glu, w_out)


def _ssm_in_bwd(du0, du1, dz, xa, dxa_next, g, scale, shift, w_in):
    n_rows, d = xa.shape
    e = dz.shape[1]
    n_lat = n_rows - ROW_TILE

    def body(du0_ref, du1_ref, dz_ref, x_ref, dn_ref, g_ref, sc_ref, sh_ref, w_ref,
             gx_ref, dw_ref, dg_ref, dsc_ref, dsh_ref):
        i = pl.program_id(0)
        _init_acc(i == 0, dw_ref, dg_ref)
        _init_acc(i <= 1, dsc_ref, dsh_ref)
        x = x_ref[...]
        h = _rms_mod(x, g_ref[...], sc_ref[0], sh_ref[0])
        dproj = jnp.concatenate([du0_ref[...] + du1_ref[...], dz_ref[...]], axis=1)
        dh = _dot_t1(dproj, w_ref[...])
        dw_ref[...] += _dot_t0(h, dproj)
        dx, dg, dsc, dsh = _rms_mod_bwd(x, g_ref[...], sc_ref[0], dh)
        dg_ref[...] += dg
        dsc_ref[0] += dsc
        dsh_ref[0] += dsh
        gx_ref[...] = dn_ref[...] + dx

    return pl.pallas_call(
        body, name="ssm_in_bwd", grid=(n_rows // ROW_TILE,),
        in_specs=[_rows(e), _rows(e), _rows(e), _rows(d), _rows(d), _full((1, d)), _seg(d), _seg(d), _full(w_in.shape)],
        out_specs=[_rows_lat(d), _full((d, 2 * e)), _full((1, d)), _seg(d), _seg(d)],
        out_shape=[_sds((n_lat, d)), _sds((d, 2 * e)), _sds((1, d)), _sds((2, 1, d)), _sds((2, 1, d))],
        compiler_params=_cparams(1),
    )(du0, du1, dz, xa, dxa_next, g, scale, shift, w_in)


Q_WIDTH = N_Q_HEADS * HEAD_DIM
SM_SCALE = 1.0 / math.sqrt(HEAD_DIM)


def _attn_in(xa, g, scale, shift, w_in, q_gain, k_gain, cos, sin):
    n_rows, d = xa.shape
    qk = Q_WIDTH + KV_WIDTH

    def body(x_ref, g_ref, sc_ref, sh_ref, w_ref, qg_ref, kg_ref, cos_ref, sin_ref, hq_ref, hk_ref,
             q_ref, k_ref, v_ref, z_ref, raw_ref):
        h = _rms_mod(x_ref[...], g_ref[...], sc_ref[0], sh_ref[0])
        proj = _dot(h, w_ref[...])
        q_raw, k_raw = proj[:, :Q_WIDTH], proj[:, Q_WIDTH:qk]
        cos, sin = cos_ref[...], sin_ref[...]
        q = _head_norm_rope(q_raw, qg_ref[...], _lanes(cos, Q_WIDTH), _lanes(sin, Q_WIDTH), hq_ref[...])
        k = _head_norm_rope(k_raw, kg_ref[...], _lanes(cos, KV_WIDTH), _lanes(sin, KV_WIDTH), hk_ref[...])
        q_ref[...] = (q * SM_SCALE).astype(BF16)
        k_ref[...] = k.astype(BF16)
        v_ref[...] = proj[:, qk:qk + KV_WIDTH].astype(BF16)
        z_ref[...] = proj[:, qk + KV_WIDTH:]
        raw_ref[...] = proj[:, :qk]

    return pl.pallas_call(
        body, name="attn_in", grid=(n_rows // ROW_TILE,),
        in_specs=[_rows(d), _full((1, d)), _seg(d), _seg(d), _full(w_in.shape), _full((1, Q_WIDTH)), _full((1, KV_WIDTH)),
                  _rows(128), _rows(128), _full((Q_WIDTH, 128)), _full((KV_WIDTH, 128))],
        out_specs=[_rows_lat(Q_WIDTH), _rows(KV_WIDTH), _rows(KV_WIDTH), _rows(Q_WIDTH), _rows(qk)],
        out_shape=[_sds((n_rows - ROW_TILE, Q_WIDTH), BF16), _sds((n_rows, KV_WIDTH), BF16), _sds((n_rows, KV_WIDTH), BF16),
                   _sds((n_rows, Q_WIDTH)), _sds((n_rows, qk))],
        compiler_params=_cparams(1),
    )(xa, g, scale, shift, w_in, q_gain, k_gain, cos, sin, _head_of_lane(Q_WIDTH), _head_of_lane(KV_WIDTH))


GROUP_WIDTH = KV_REP * HEAD_DIM


def _stack_heads(ref):
    return jnp.concatenate([ref[:, h * HEAD_DIM:(h + 1) * HEAD_DIM] for h in range(KV_REP)], axis=0)


def _unstack_heads(a_t, tq):
    return jnp.concatenate([a_t[:, h * tq:(h + 1) * tq].T for h in range(KV_REP)], axis=1)


def _kv_tile(n_keys):
    return 768 if n_keys % 768 == 0 else 256


def _kv_tile_fwd(n_keys):
    return 1408 if n_keys % 1408 == 0 else _kv_tile(n_keys)


def _q_tile(n_lat):
    return 512 if n_lat % 512 == 0 else 256


def _flash_fwd(q, k, v_t):
    n_lat = q.shape[0]
    tq = _q_tile(n_lat)
    rows = KV_REP * tq
    n_kv, tk, n_q = k.shape[1], k.shape[2], n_lat // tq

    v_rows = v_t.shape[2]

    def body(q_ref, k_ref, vt_ref, o_ref, lse_ref):
        q = _stack_heads(q_ref)

        def step(j, carry):
            m_prev, acc = carry
            s_t = _dot_t1(k_ref[0, j], q)
            m_new = jnp.maximum(m_prev, jnp.max(s_t, axis=0, keepdims=True))
            alpha = jnp.exp(m_prev - m_new)
            p_t = jnp.exp(s_t - m_new)
            return m_new, alpha * acc + _dot(vt_ref[0, j], p_t)

        init = (jnp.full((1, rows), -jnp.inf, F32), jnp.zeros((v_rows, rows), F32))
        m, acc = lax.fori_loop(0, n_kv, step, init)
        l = acc[HEAD_DIM:HEAD_DIM + 1]
        o_ref[...] = _unstack_heads(acc[:HEAD_DIM] / l, tq)
        lse_ref[0, 0] = m + jnp.log(l)

    kv_all = lambda a: pl.BlockSpec((1,) + a.shape[1:], lambda g, i: (g, 0, 0, 0))
    return pl.pallas_call(
        body, name="flash_fwd", grid=(N_KV_HEADS, n_q),
        in_specs=[pl.BlockSpec((tq, GROUP_WIDTH), lambda g, i: (i, g)), kv_all(k), kv_all(v_t)],
        out_specs=[pl.BlockSpec((tq, GROUP_WIDTH), lambda g, i: (i, g)),
                   pl.BlockSpec((1, 1, 1, rows), lambda g, i: (g, i, 0, 0))],
        out_shape=[_sds((n_lat, Q_WIDTH)), _sds((N_KV_HEADS, n_q, 1, rows))],
        compiler_params=_cparams(2),
    )(q, k, v_t)


def _flash_bwd(q, k, k_t, v, do, lse_t, delta_t):
    n_lat = q.shape[0]
    tq = _q_tile(n_lat)
    rows = KV_REP * tq
    n_kv, tk, n_q = k.shape[1], k.shape[2], n_lat // tq

    def body(q_ref, k_ref, kt_ref, v_ref, do_ref, lse_ref, dl_ref, dq_ref, dk_ref, dv_ref):
        _init_acc(pl.program_id(1) == 0, dk_ref, dv_ref)
        q, do = _stack_heads(q_ref), _stack_heads(do_ref)
        lse, delta = lse_ref[0, 0], dl_ref[0, 0]

        def step(j, dq_acc):
            p_t = jnp.exp(_dot_t1(k_ref[0, j], q) - lse)
            dv_ref[0, j] += _dot(p_t, do)
            ds_t = p_t * (_dot_t1(v_ref[0, j], do) - delta)
            dk_ref[0, j] += _dot(ds_t, q)
            return dq_acc + _dot(kt_ref[0, j], ds_t)

        dq = lax.fori_loop(0, n_kv, step, jnp.zeros((HEAD_DIM, rows), F32))
        dq_ref[...] = _unstack_heads(dq, tq)

    qspec = pl.BlockSpec((tq, GROUP_WIDTH), lambda g, i: (i, g))
    rowspec = pl.BlockSpec((1, 1, 1, rows), lambda g, i: (g, i, 0, 0))
    kv_all = lambda a: pl.BlockSpec((1,) + a.shape[1:], lambda g, i: (g, 0, 0, 0))
    return pl.pallas_call(
        body, name="flash_bwd", grid=(N_KV_HEADS, n_q),
        in_specs=[qspec, kv_all(k), kv_all(k_t), kv_all(v), qspec, rowspec, rowspec],
        out_specs=[qspec, kv_all(k), kv_all(k)],
        out_shape=[_sds((n_lat, Q_WIDTH)), _sds(k.shape), _sds(k.shape)],
        compiler_params=_cparams(2),
    )(q, k, k_t, v, do, lse_t, delta_t)


def _to_lane_stacked(a, tq):
    n_lat = a.shape[0]
    a = a.reshape(n_lat // tq, tq, N_KV_HEADS, KV_REP).transpose(2, 0, 3, 1)
    return a.reshape(N_KV_HEADS, n_lat // tq, 1, KV_REP * tq)


def _attn_post_loss(o, z, xa, gate, w_out, final_g, target):
    n_lat, d = target.shape
    e = o.shape[1]
    head_of_lane = (jnp.arange(e)[:, None] // HEAD_DIM == jnp.arange(128)[None, :]).astype(BF16)

    def body(o_ref, z_ref, x_ref, gt_ref, w_ref, fg_ref, tg_ref, hl_ref,
             do_ref, dl_ref, dz_ref, dx_ref, loss_ref, dfg_ref, dgt_ref, dw_ref):
        _init_acc(pl.program_id(0) == 0, loss_ref, dfg_ref, dgt_ref, dw_ref)
        oo, zz, gate_t, fg = o_ref[...], z_ref[...], gt_ref[0], fg_ref[...]
        sz = _sigmoid(zz)
        silu_z = zz * sz
        m = oo * silu_z
        out = _dot(m, w_ref[...])
        x2 = x_ref[...] + gate_t * out
        r = lax.rsqrt(jnp.mean(x2 * x2, axis=-1, keepdims=True) + NORM_EPS)
        n = x2 * r
        err = n * fg - tg_ref[...]
        loss_ref[...] += 0.5 * jnp.sum(jnp.mean(err * err, axis=-1, keepdims=True), axis=0, keepdims=True)
        dy = err * (1.0 / d)
        dfg_ref[...] += _sum0(dy * n)
        dn = dy * fg
        dx2 = r * (dn - n * jnp.mean(dn * n, axis=-1, keepdims=True))
        dx_ref[...] = dx2
        dgt_ref[...] += _sum0(dx2 * out)
        dout = gate_t * dx2
        dw_ref[...] += _dot_t0(m, dout)
        dm = _dot_t1(dout, w_ref[...])
        do = dm * silu_z
        do_ref[...] = do.astype(BF16)
        prod = do * oo
        hi = prod.astype(BF16)
        lo = (prod - hi.astype(F32)).astype(BF16)
        dl_ref[...] = _dot(hi, hl_ref[...]) + _dot(lo, hl_ref[...])
        dz_ref[...] = dm * oo * (sz * (1.0 + zz * (1.0 - sz)))

    return pl.pallas_call(
        body, name="attn_post_loss", grid=(n_lat // ROW_TILE,),
        in_specs=[_rows(e), _rows_skip_ctx(e), _rows_skip_ctx(d), _lat_seg(d), _full(w_out.shape),
                  _full((1, d)), _rows(d), _full((e, 128))],
        out_specs=[_rows(e), _rows(128), _rows(e), _rows(d), _full((1, 1)), _full((1, d)), _full((1, d)), _full(w_out.shape)],
        out_shape=[_sds((n_lat, e), BF16), _sds((n_lat, 128)), _sds((n_lat, e)), _sds((n_lat, d)), _sds((1, 1)), _sds((1, d)),
                   _sds((1, d)), _sds(w_out.shape)],
        compiler_params=_cparams(1),
    )(o, z, xa, gate, w_out, final_g, target, head_of_lane)


def _attn_in_bwd(dq, dk, dv, dz, raw, xa, dx2, g, scale, shift, q_gain, k_gain, cos, sin, w_in):
    n_rows, d = xa.shape
    qk = Q_WIDTH + KV_WIDTH
    n_in = w_in.shape[1]

    def body(dq_ref, dk_ref, dv_ref, dz_ref, raw_ref, x_ref, dx2_ref, g_ref, sc_ref, sh_ref, qg_ref, kg_ref, cos_ref, sin_ref,
             w_ref, hq_ref, hk_ref, dxa_ref, dw_ref, dqg_ref, dkg_ref, dg_ref, dsc_ref, dsh_ref):
        i = pl.program_id(0)
        _init_acc(i == 0, dw_ref, dqg_ref, dkg_ref, dg_ref)
        _init_acc(i <= 1, dsc_ref, dsh_ref)
        is_lat = (i > 0).astype(F32)
        x = x_ref[...]
        h = _rms_mod(x, g_ref[...], sc_ref[0], sh_ref[0])
        cos, sin = cos_ref[...], sin_ref[...]
        raw_t = raw_ref[...]
        dq_raw, dqg = _head_norm_rope_bwd(raw_t[:, :Q_WIDTH], qg_ref[...], _lanes(cos, Q_WIDTH), _lanes(sin, Q_WIDTH),
                                          hq_ref[...], dq_ref[...] * (SM_SCALE * is_lat))
        dk_raw, dkg = _head_norm_rope_bwd(raw_t[:, Q_WIDTH:], kg_ref[...], _lanes(cos, KV_WIDTH), _lanes(sin, KV_WIDTH),
                                          hk_ref[...], dk_ref[...])
        dqg_ref[...] += dqg
        dkg_ref[...] += dkg
        dproj = jnp.concatenate([dq_raw, dk_raw, dv_ref[...], dz_ref[...] * is_lat], axis=1)
        dh = _dot_t1(dproj, w_ref[...])
        dw_ref[...] += _dot_t0(h, dproj)
        dx, dg, dsc, dsh = _rms_mod_bwd(x, g_ref[...], sc_ref[0], dh)
        dg_ref[...] += dg
        dsc_ref[0] += dsc
        dsh_ref[0] += dsh
        dxa_ref[...] = dx + dx2_ref[...] * is_lat

    return pl.pallas_call(
        body, name="attn_in_bwd", grid=(n_rows // ROW_TILE,),
        in_specs=[_rows_lat(Q_WIDTH), _rows(KV_WIDTH), _rows(KV_WIDTH), _rows_lat(Q_WIDTH), _rows(qk), _rows(d), _rows_lat(d),
                  _full((1, d)), _seg(d), _seg(d), _full((1, Q_WIDTH)), _full((1, KV_WIDTH)), _rows(128), _rows(128),
                  _full(w_in.shape), _full((Q_WIDTH, 128)), _full((KV_WIDTH, 128))],
        out_specs=[_rows(d), _full((d, n_in)), _full((1, Q_WIDTH)), _full((1, KV_WIDTH)), _full((1, d)), _seg(d), _seg(d)],
        out_shape=[_sds((n_rows, d)), _sds((d, n_in)), _sds((1, Q_WIDTH)), _sds((1, KV_WIDTH)), _sds((1, d)),
                   _sds((2, 1, d)), _sds((2, 1, d))],
        compiler_params=_cparams(1),
    )(dq, dk, dv, dz, raw, xa, dx2, g, scale, shift, q_gain, k_gain, cos, sin, w_in,
      _head_of_lane(Q_WIDTH), _head_of_lane(KV_WIDTH))


def _heads_major(a, n_heads):
    return a.reshape(a.shape[0], n_heads, HEAD_DIM).transpose(1, 0, 2)


def _tokens_major(a):
    return a.transpose(1, 0, 2).reshape(a.shape[1], a.shape[0] * HEAD_DIM)


def _local_step(x, ctx, target, mods, norm_g, ssm, w_ssm_in, w_glu, b_glu, w_ssm_out, w_attn_in, q_norm, k_norm, w_attn_out,
                final_g, attn_exchange=None, post_exchange=None):
    n_ctx, d = ctx.shape
    assert n_ctx == ROW_TILE
    n_lat = x.shape[0]
    (shift0, scale0, gate0), (shift1, scale1, gate1) = mods
    g0, g1, fg = norm_g[0:1], norm_g[1:2], final_g.reshape(1, d)
    b_glu = b_glu.reshape(1, -1)
    q_gain = jnp.tile(q_norm.reshape(1, HEAD_DIM), (1, N_Q_HEADS))
    k_gain = jnp.tile(k_norm.reshape(1, HEAD_DIM), (1, N_KV_HEADS))
    cos, sin = _rope_tables(n_ctx, n_lat)

    xa0 = jnp.concatenate([ctx, x], axis=0)
    u, z0 = _ssm_in(xa0, g0, scale0, shift0, w_ssm_in)
    gather = lambda ex: (ex[0], False) if ex else None
    (y0, y1), saved, gathered = _s5_forward(u, ssm, n_ctx, (gather(attn_exchange), gather(post_exchange)))
    if attn_exchange:
        w_attn_in, w_attn_out = attn_exchange[1](gathered[0])
    if post_exchange:
        w_glu, w_ssm_out = post_exchange[1](gathered[1])
    xa1 = _ssm_post(xa0, y0, y1, z0, gate0, w_glu, b_glu, w_ssm_out)

    q, k, v, z1, raw = _attn_in(xa1, g1, scale1, shift1, w_attn_in, q_gain, k_gain, cos, sin)
    tq, tk, tk_fwd = _q_tile(n_lat), _kv_tile(n_ctx + n_lat), _kv_tile_fwd(n_ctx + n_lat)
    k_h, v_h = _heads_major(k, N_KV_HEADS), _heads_major(v, N_KV_HEADS)
    k_b, v_b = k_h.reshape(N_KV_HEADS, -1, tk, HEAD_DIM), v_h.reshape(N_KV_HEADS, -1, tk, HEAD_DIM)
    v_t = v_h.reshape(N_KV_HEADS, -1, tk_fwd, HEAD_DIM).transpose(0, 1, 3, 2)
    v_t_ones = jnp.concatenate([v_t, jnp.ones(v_t.shape[:2] + (16, tk_fwd), BF16)], axis=2)
    o, lse_t = _flash_fwd(q, k_h.reshape(N_KV_HEADS, -1, tk_fwd, HEAD_DIM), v_t_ones)
    do, delta, dz1, dx2, loss, d_fg, d_gate1, d_w_attn_out = _attn_post_loss(
        o, z1, xa1, gate1, w_attn_out, fg, target)

    dq, dk_b, dv_b = _flash_bwd(q, k_b, k_b.transpose(0, 1, 3, 2), v_b, do, lse_t, _to_lane_stacked(delta[:, :N_Q_HEADS], tq))
    keys_major = lambda a: _tokens_major(a.reshape(N_KV_HEADS, -1, HEAD_DIM))
    dxa1, d_w_attn_in, d_qg, d_kg, d_g1, d_scale1, d_shift1 = _attn_in_bwd(
        dq, keys_major(dk_b), keys_major(dv_b), dz1, raw, xa1, dx2, g1, scale1, shift1,
        q_gain, k_gain, cos, sin, w_attn_in)
    dy, dz0, d_gate0, d_w_ssm_out, d_w_glu, d_b_glu = _ssm_post_bwd(
        dxa1, y0, y1, z0, gate0, w_glu, b_glu, w_ssm_out)
    scatter = lambda ex, *g: (ex[2](*g), True) if ex else None
    (du0, du1), d_ssm, parts = _s5_backward(
        u, dy, ssm, saved, n_ctx,
        (scatter(attn_exchange, d_w_attn_in, d_w_attn_out), scatter(post_exchange, d_w_glu, d_w_ssm_out)))
    grad_x, d_w_ssm_in, d_g0, d_scale0, d_shift0 = _ssm_in_bwd(du0, du1, dz0, xa0, dxa1, g0, scale0, shift0, w_ssm_in)

    d_gate1_seg = jnp.concatenate([jnp.zeros((1, 1, d), F32), d_gate1.reshape(1, 1, d)], axis=0)
    grads = dict(
        norm_g=jnp.concatenate([d_g0, d_g1], axis=0), ssm_w_in=d_w_ssm_in, ssm=d_ssm, ssm_b_glu=d_b_glu.reshape(-1),
        attn_q_norm=d_qg.reshape(N_Q_HEADS, HEAD_DIM).sum(0), attn_k_norm=d_kg.reshape(N_KV_HEADS, HEAD_DIM).sum(0),
        final_norm_g=d_fg.reshape(-1))
    if attn_exchange:
        grads.update(attn_parts=parts[0])
    else:
        grads.update(attn_w_in=d_w_attn_in, attn_w_out=d_w_attn_out)
    if post_exchange:
        grads.update(post_parts=parts[1])
    else:
        grads.update(ssm_w_glu=d_w_glu, ssm_w_out=d_w_ssm_out)
    d_mods = ((d_shift0, d_scale0, d_gate0), (d_shift1, d_scale1, d_gate1_seg))
    return loss[0, 0], grad_x, grads, d_mods


def _my_index():
    return 4 * lax.axis_index("x") + 2 * lax.axis_index("y") + lax.axis_index("c")


def _peer(k):
    mx, my, mc = lax.axis_index("x"), lax.axis_index("y"), lax.axis_index("c")
    px = 1 - mx if k & 4 else mx
    py = 1 - my if k & 2 else my
    pc = 1 - mc if k & 1 else mc
    return (px, py, pc), 4 * px + 2 * py + pc


HBM_SPEC = pl.BlockSpec(memory_space=pl.ANY)


def _exchange(x, name, all_to_all):
    def body(x_ref, out_ref, send_sems, recv_sems, local_sem):
        _exchange_copies(all_to_all, x_ref, out_ref, send_sems, recv_sems, local_sem, start=True)
        _exchange_copies(all_to_all, x_ref, out_ref, send_sems, recv_sems, local_sem, start=False)

    return pl.pallas_call(
        body, name=name, in_specs=[HBM_SPEC], out_specs=HBM_SPEC,
        out_shape=_exchange_out_shape(x, all_to_all), scratch_shapes=_exchange_semaphores(),
    )(x)


def _exchange_out_shape(x, all_to_all):
    return _sds((N_DEV,) + tuple(x.shape[1:] if all_to_all else x.shape), x.dtype)


def _exchange_semaphores():
    return [pltpu.SemaphoreType.DMA((N_DEV - 1,)), pltpu.SemaphoreType.DMA((N_DEV - 1,)), pltpu.SemaphoreType.DMA]


def _exchange_copies(all_to_all, x_ref, out_ref, send_sems, recv_sems, local_sem, start):
    me = _my_index()
    mine = pltpu.make_async_copy(x_ref.at[me] if all_to_all else x_ref, out_ref.at[me], local_sem)
    if start:
        mine.start()
    for k in range(1, N_DEV):
        peer, peer_idx = _peer(k)
        send = pltpu.make_async_remote_copy(
            src_ref=x_ref.at[peer_idx] if all_to_all else x_ref, dst_ref=out_ref.at[me],
            send_sem=send_sems.at[k - 1], recv_sem=recv_sems.at[k - 1], device_id=peer, device_id_type=MESH_IDS)
        if start:
            send.start()
        else:
            pltpu.make_async_remote_copy(
                src_ref=x_ref.at[me] if all_to_all else x_ref, dst_ref=out_ref.at[peer_idx],
                send_sem=send_sems.at[k - 1], recv_sem=recv_sems.at[k - 1], device_id=peer,
                device_id_type=MESH_IDS).wait_recv()
            send.wait_send()
    if not start:
        mine.wait()


def _ride(rider, first, last, refs):
    @pl.when(first)
    def _():
        _exchange_copies(rider[1], *refs, start=True)

    @pl.when(last)
    def _():
        _exchange_copies(rider[1], *refs, start=False)


MOD_ROWS = 16
CTX_ROW = N_DEV


def _mod_fwd(cond, w_shard, b_cols):
    n_layers, d, cols = w_shard.shape

    def body(c_ref, w_ref, b_ref, o_ref):
        c = c_ref[...]
        s = c * _sigmoid(c)
        for i in range(n_layers):
            o_ref[i] = _dot(s, w_ref[i]) + b_ref[i]

    return pl.pallas_call(
        body, name="mod_fwd", out_shape=_sds((n_layers, MOD_ROWS, cols)),
        compiler_params=pltpu.CompilerParams(vmem_limit_bytes=VMEM_LIMIT),
    )(cond, w_shard, b_cols.reshape(n_layers, 1, cols))


def _mod_bwd(cond, d_lat_cols, d_ctx_cols, w_shard):
    n_layers, d, cols = w_shard.shape

    def body(c_ref, dl_ref, dc_ref, w_ref, dw_ref, dcc_ref):
        c = c_ref[...]
        sg = _sigmoid(c)
        s = c * sg
        d_s = jnp.zeros((MOD_ROWS, d), F32)
        for i in range(n_layers):
            d_ctx = dc_ref[0, i]
            for j in range(1, N_DEV):
                d_ctx = d_ctx + dc_ref[j, i]
            dm = jnp.concatenate([dl_ref[i], d_ctx, jnp.zeros((MOD_ROWS - N_DEV - 1, cols), F32)], axis=0)
            dw_ref[i] = _dot_t0(s, dm)
            d_s = d_s + _dot_t1(dm, w_ref[i])
        d_c = d_s * (sg * (1.0 + c * (1.0 - sg)))
        dcc_ref[...] = d_c[CTX_ROW:CTX_ROW + 1]

    return pl.pallas_call(
        body, name="mod_bwd", out_shape=[_sds((n_layers, d, cols)), _sds((1, d))],
        compiler_params=pltpu.CompilerParams(vmem_limit_bytes=VMEM_LIMIT),
    )(cond, d_lat_cols, d_ctx_cols, w_shard)


ADAM_TILE = 512


def _adamw(w, g_parts, m, v, name):
    n_parts, n_rows, lanes = g_parts.shape
    tile = min(ADAM_TILE, n_rows)
    assert n_rows % tile == 0
    c1 = 1.0 - ADAM_B1 ** ADAM_STEP
    c2 = 1.0 - ADAM_B2 ** ADAM_STEP

    def body(w_ref, g_ref, m_ref, v_ref, go_ref, d_ref, mo_ref, vo_ref):
        g = g_ref[0].astype(F32)
        for p in range(1, n_parts):
            g = g + g_ref[p].astype(F32)
        m_new = ADAM_B1 * m_ref[...] + (1.0 - ADAM_B1) * g
        v_new = ADAM_B2 * v_ref[...] + (1.0 - ADAM_B2) * (g * g)
        go_ref[...] = g
        mo_ref[...] = m_new
        vo_ref[...] = v_new
        d_ref[...] = -ADAM_LR * ((m_new / c1) / (jnp.sqrt(v_new / c2) + ADAM_EPS) + ADAM_WD * w_ref[...])

    row = pl.BlockSpec((tile, lanes), lambda i: (i, 0))
    return pl.pallas_call(
        body, name=name, grid=(n_rows // tile,),
        in_specs=[row, pl.BlockSpec((n_parts, tile, lanes), lambda i: (0, i, 0)), row, row],
        out_specs=[row] * 4, out_shape=[_sds((n_rows, lanes))] * 4,
        compiler_params=_cparams(1),
    )(w, g_parts, m, v)


def _sum_parts(parts):
    n_parts, n_rows, lanes = parts.shape

    def body(p_ref, o_ref):
        acc = p_ref[0]
        for p in range(1, n_parts):
            acc = acc + p_ref[p]
        o_ref[...] = acc

    return pl.pallas_call(body, name="sum_parts", out_shape=_sds((n_rows, lanes)))(parts)


def _pack(arrays, row_multiple):
    parts = []
    for a in arrays:
        flat = a.reshape(-1)
        parts.append(jnp.pad(flat, (0, (-flat.shape[0]) % 1024)))
    flat = jnp.concatenate(parts)
    flat = jnp.pad(flat, (0, (-flat.shape[0]) % (row_multiple * 128)))
    return flat.reshape(-1, 128)


def _unpack(packed, shapes):
    flat = packed.reshape(-1)
    out, pos = [], 0
    for s in shapes:
        n = math.prod(s)
        out.append(flat[pos:pos + n].reshape(s))
        pos += n + (-n) % 1024
    return out


WEIGHT_NAMES = ['c_ctx', 'w_mod', 'b_mod', 'norm_g', 'ssm_w_in', 'ssm_a_re', 'ssm_a_im', 'ssm_log_dt', 'ssm_b_re', 'ssm_b_im',
                'ssm_c_re', 'ssm_c_im', 'ssm_d', 'ssm_w_glu', 'ssm_b_glu', 'ssm_w_out', 'attn_w_in', 'attn_q_norm',
                'attn_k_norm', 'attn_w_out', 'final_norm_g']
FIRST_SHARDED = ['ssm_w_in']
POST_SHARDED = ['ssm_w_glu', 'ssm_w_out']
ATTN_SHARDED = ['attn_w_in', 'attn_w_out']
SHARDED = FIRST_SHARDED + POST_SHARDED + ATTN_SHARDED
COLUMN_SHARDED = ('ssm_w_in', 'attn_w_in')
REPLICATED = ['c_ctx', 'b_mod', 'norm_g', 'ssm_a_re', 'ssm_a_im', 'ssm_log_dt', 'ssm_b_re', 'ssm_b_im', 'ssm_c_re', 'ssm_c_im',
              'ssm_d', 'ssm_b_glu', 'attn_q_norm', 'attn_k_norm', 'final_norm_g']
SSM_NAMES = ['ssm_a_re', 'ssm_a_im', 'ssm_log_dt', 'ssm_b_re', 'ssm_b_im', 'ssm_c_re', 'ssm_c_im', 'ssm_d']


def _full_from_shards(gathered, name, shard_shape):
    rows, cols = shard_shape
    w = gathered.reshape(N_DEV, rows, cols)
    if name in COLUMN_SHARDED:
        return w.transpose(1, 0, 2).reshape(rows, N_DEV * cols)
    return w.reshape(N_DEV * rows, cols)


def _shards_from_full(g, name):
    if name in COLUMN_SHARDED:
        rows, cols = g.shape
        g = g.reshape(rows, N_DEV, cols // N_DEV).transpose(1, 0, 2)
    return g.reshape(N_DEV, -1, 128)


def kernel(x, c, ctx, c_ctx, w_mod, b_mod, norm_g, ssm_w_in, ssm_a_re, ssm_a_im, ssm_log_dt, ssm_b_re, ssm_b_im, ssm_c_re, ssm_c_im, ssm_d, ssm_w_glu, ssm_b_glu, ssm_w_out, attn_w_in, attn_q_norm, attn_k_norm, attn_w_out, final_norm_g, loss_target, m_c_ctx, m_w_mod, m_b_mod, m_norm_g, m_ssm_w_in, m_ssm_a_re, m_ssm_a_im, m_ssm_log_dt, m_ssm_b_re, m_ssm_b_im, m_ssm_c_re, m_ssm_c_im, m_ssm_d, m_ssm_w_glu, m_ssm_b_glu, m_ssm_w_out, m_attn_w_in, m_attn_q_norm, m_attn_k_norm, m_attn_w_out, m_final_norm_g, v_c_ctx, v_w_mod, v_b_mod, v_norm_g, v_ssm_w_in, v_ssm_a_re, v_ssm_a_im, v_ssm_log_dt, v_ssm_b_re, v_ssm_b_im, v_ssm_c_re, v_ssm_c_im, v_ssm_d, v_ssm_w_glu, v_ssm_b_glu, v_ssm_w_out, v_attn_w_in, v_attn_q_norm, v_attn_k_norm, v_attn_w_out, v_final_norm_g):
    env = dict(locals())
    weights = {n: env[n] for n in WEIGHT_NAMES}
    mom_m = {n: env["m_" + n] for n in WEIGHT_NAMES}
    mom_v = {n: env["v_" + n] for n in WEIGHT_NAMES}
    d = D_MODEL
    me = _my_index()
    mod_cols = w_mod.shape[-1]

    c_all = _exchange(c.reshape(8, d // 8), "gather_c", False).reshape(N_DEV, d)
    cond = jnp.concatenate([c_all, c_ctx.reshape(1, d), jnp.zeros((MOD_ROWS - N_DEV - 1, d), F32)], axis=0)
    shard_shapes = {n: weights[n].shape[1:] for n in SHARDED}
    pack_shards = lambda names: _pack([weights[n] for n in names], 1).astype(BF16)

    def unpack_full(gathered, names):
        full, pos = [], 0
        for n in names:
            rows = math.prod(shard_shapes[n]) // 128
            full.append(_full_from_shards(gathered[:, pos:pos + rows], n, shard_shapes[n]))
            pos += rows
        return full

    def exchange_of(names):
        return (pack_shards(names), lambda gathered: unpack_full(gathered, names),
                lambda *grads: jnp.concatenate([_shards_from_full(t, n) for t, n in zip(grads, names)], axis=1).astype(BF16))

    (w_ssm_in,) = unpack_full(_exchange(pack_shards(FIRST_SHARDED), "gather_ssm_w_in", False), FIRST_SHARDED)

    b_cols = lax.dynamic_slice(b_mod, (0, me * mod_cols), (2, mod_cols))
    mod_shard = _mod_fwd(cond, w_mod, b_cols)
    mod_all = _exchange(mod_shard.reshape(2 * MOD_ROWS, mod_cols), "gather_mod", False)
    mod_full = mod_all.reshape(N_DEV, 2, MOD_ROWS, mod_cols).transpose(1, 2, 0, 3).reshape(2, MOD_ROWS, 3 * d)
    lat_rows = lax.dynamic_slice(mod_full, (0, me, 0), (2, 1, 3 * d))
    mods = []
    for i in range(2):
        seg = jnp.stack([mod_full[i, CTX_ROW:CTX_ROW + 1], lat_rows[i]])
        mods.append((seg[:, :, :d], seg[:, :, d:2 * d], seg[:, :, 2 * d:]))

    ssm = tuple(weights[n][0] for n in SSM_NAMES)
    loss, grad_x, g, d_mods = _local_step(
        x[0], ctx[0], loss_target[0], mods, norm_g, ssm, w_ssm_in, None, ssm_b_glu[0], None,
        None, attn_q_norm[0], attn_k_norm[0], None, final_norm_g, exchange_of(ATTN_SHARDED), exchange_of(POST_SHARDED))

    d_rows = jnp.stack([jnp.concatenate(dm, axis=-1) for dm in d_mods])
    d_rows = jnp.concatenate([d_rows.reshape(4, 3 * d), jnp.zeros((4, 3 * d), F32)], axis=0)
    d_all = _exchange(d_rows, "gather_dmod", False)[:, :4].reshape(N_DEV, 2, 2, 3 * d)
    d_all = lax.dynamic_slice(d_all, (0, 0, 0, me * mod_cols), (N_DEV, 2, 2, mod_cols))
    d_w_mod, d_c_ctx = _mod_bwd(cond, d_all[:, :, 1].transpose(1, 0, 2), d_all[:, :, 0:1], w_mod)
    d_b_mod = jnp.stack([jnp.concatenate([t[0] + t[1] for t in dm], axis=-1).reshape(3 * d) for dm in d_mods])

    first_parts = _exchange(_shards_from_full(g['ssm_w_in'], 'ssm_w_in').astype(BF16), "scatter_ssm_w_in_grads", True)
    parts_of = {}
    for names, parts in ((FIRST_SHARDED, first_parts), (POST_SHARDED, g['post_parts']), (ATTN_SHARDED, g['attn_parts'])):
        pos = 0
        for n in names:
            rows = math.prod(shard_shapes[n]) // 128
            parts_of[n] = parts[:, pos:pos + rows].reshape((N_DEV,) + shard_shapes[n])
            pos += rows

    def update(n, g_parts):
        as_2d = lambda t: t.reshape(-1, t.shape[-1])
        res = _adamw(as_2d(weights[n]), g_parts, as_2d(mom_m[n]), as_2d(mom_v[n]), "adamw_" + n)
        return [t.reshape(weights[n].shape) for t in res]

    big = {n: update(n, parts_of[n]) for n in SHARDED}
    big['w_mod'] = update('w_mod', d_w_mod.reshape(1, -1, mod_cols))

    small = dict(zip(SSM_NAMES, g['ssm']))
    small.update(c_ctx=d_c_ctx, b_mod=d_b_mod, norm_g=g['norm_g'], ssm_b_glu=g['ssm_b_glu'], attn_q_norm=g['attn_q_norm'],
                 attn_k_norm=g['attn_k_norm'], final_norm_g=g['final_norm_g'])
    pack_small = lambda t, last: _pack([t[n] for n in REPLICATED] + [last], ADAM_TILE)
    no_weight = jnp.zeros((1,), F32)
    g_small = pack_small(small, loss.reshape(1))
    slices = _exchange(g_small.reshape(N_DEV, -1, 128), "scatter_small_grads", True)
    g_small = _exchange(_sum_parts(slices), "gather_small_grads", False).reshape(1, -1, 128)
    rep = _adamw(pack_small(weights, no_weight), g_small, pack_small(mom_m, no_weight), pack_small(mom_v, no_weight),
                 "adamw_replicated")
    rep = [_unpack(t, [weights[n].shape for n in REPLICATED] + [(1,)]) for t in rep]
    loss = rep[0][-1][0]

    results = []
    for kind in range(4):
        by_name = {n: res[kind] for n, res in big.items()}
        by_name.update(zip(REPLICATED, rep[kind]))
        results.extend(by_name[n] for n in WEIGHT_NAMES)
    return (loss, grad_x[None], *results)
```

```python
import functools
import math

import jax
import jax.numpy as jnp
from jax import lax
from jax.experimental import pallas as pl
from jax.experimental.pallas import tpu as pltpu

F32 = jnp.float32
BF16 = jnp.bfloat16

N_DEV = 8
D_MODEL = 1024
NORM_EPS = 1e-6
SSM_GROUP = 16
SSM_GROUPS = 64
SSM_STATE = 64
GROUPS_PER_BLOCK = 8
N_BLOCKS = SSM_GROUPS // GROUPS_PER_BLOCK
HALF = GROUPS_PER_BLOCK * SSM_STATE
HEAD_DIM = 64
N_Q_HEADS = 16
N_KV_HEADS = 4
KV_REP = N_Q_HEADS // N_KV_HEADS
KV_WIDTH = N_KV_HEADS * HEAD_DIM
GRID_W = 64
ROPE_THETA = 10000.0
ADAM_LR, ADAM_B1, ADAM_B2, ADAM_EPS, ADAM_WD, ADAM_STEP = 0.001, 0.9, 0.999, 1e-08, 0.01, 10

ROW_TILE = 256
SCAN_CHUNK = 256
VMEM_LIMIT = 56 * 1024 * 1024
MESH_IDS = pl.DeviceIdType.MESH


def _cparams(n_axes):
    return pltpu.CompilerParams(dimension_semantics=("arbitrary",) * n_axes, vmem_limit_bytes=VMEM_LIMIT)


def _dot(a, b):
    return jnp.dot(a.astype(BF16), b.astype(BF16), preferred_element_type=F32)


def _dot_t0(a, b):
    return lax.dot_general(a.astype(BF16), b.astype(BF16), (((0,), (0,)), ((), ())), preferred_element_type=F32)


def _dot_t1(a, b):
    return lax.dot_general(a.astype(BF16), b.astype(BF16), (((1,), (1,)), ((), ())), preferred_element_type=F32)


def _s5_prep(a_re, a_im, log_dt, b_re, b_im):
    dt = jnp.exp(log_dt)[:, None]
    ldr, ldi = a_re * dt, a_im * dt
    mag = jnp.exp(ldr)
    abar_re, abar_im = mag * jnp.cos(ldi), mag * jnp.sin(ldi)
    den = a_re * a_re + a_im * a_im
    num_re, num_im = abar_re - 1.0, abar_im
    coef_re = (num_re * a_re + num_im * a_im) / den
    coef_im = (num_im * a_re - num_re * a_im) / den
    bbar_re = coef_re[..., None] * b_re - coef_im[..., None] * b_im
    bbar_im = coef_re[..., None] * b_im + coef_im[..., None] * b_re
    return abar_re, abar_im, bbar_re, bbar_im


def _s5_blocks(abar_re, abar_im, bbar_re, bbar_im, c_re, c_im):
    eye = jnp.eye(GROUPS_PER_BLOCK, dtype=F32)
    bb = jnp.stack([bbar_re, bbar_im]).reshape(2, N_BLOCKS, GROUPS_PER_BLOCK, SSM_STATE, SSM_GROUP)
    b_blk = jnp.einsum('rqgph,gk->qghrkp', bb, eye).reshape(N_BLOCKS, 128, 2 * HALF)
    cc = jnp.stack([c_re, -c_im]).reshape(2, N_BLOCKS, GROUPS_PER_BLOCK, SSM_GROUP, SSM_STATE)
    c_blk = jnp.einsum('rqghp,gk->qrgpkh', cc, eye).reshape(N_BLOCKS, 2 * HALF, 128)
    abar = jnp.stack([abar_re.reshape(N_BLOCKS, HALF), abar_im.reshape(N_BLOCKS, HALF)])
    return abar, b_blk, c_blk


def _s5_unblock(d_b_blk, d_ct_blk):
    db = d_b_blk.reshape(N_BLOCKS, GROUPS_PER_BLOCK, SSM_GROUP, 2, GROUPS_PER_BLOCK, SSM_STATE)
    db = jnp.einsum('qghrgp->rqgph', db).reshape(2, SSM_GROUPS, SSM_STATE, SSM_GROUP)
    dc = d_ct_blk.reshape(N_BLOCKS, GROUPS_PER_BLOCK, SSM_GROUP, 2, GROUPS_PER_BLOCK, SSM_STATE)
    dc = jnp.einsum('qghrgp->rqghp', dc).reshape(2, SSM_GROUPS, SSM_GROUP, SSM_STATE)
    return db[0], db[1], dc[0], -dc[1]


def _scan_chunk_of_step(j, n_chunks, n_ctx_chunks, reverse):
    if not reverse:
        return j
    return jnp.where(j < n_ctx_chunks, n_ctx_chunks - 1 - j, n_chunks - 1 - j + n_ctx_chunks)


LANE_TILES = 2 * HALF // 128
RE_TILES = HALF // 128


def _tiles(v):
    return [v[:, l * 128:(l + 1) * 128] for l in range(v.shape[1] // 128)]


def _scatter_steps(s_ref, q, x):
    for l in range(LANE_TILES):
        s_ref[l, pl.ds(q, x.shape[0], stride=N_BLOCKS), :] = x[:, l * 128:(l + 1) * 128]


def _gather_steps(s_ref, q, n_steps):
    return jnp.concatenate([s_ref[l, pl.ds(q, n_steps, stride=N_BLOCKS), :] for l in range(LANE_TILES)], axis=1)


def _load_step(s_ref, t):
    row = pl.multiple_of(t * N_BLOCKS, N_BLOCKS)
    return [s_ref[l, pl.ds(row, N_BLOCKS), :] for l in range(LANE_TILES)]


def _store_step(s_ref, t, tiles):
    row = pl.multiple_of(t * N_BLOCKS, N_BLOCKS)
    for l in range(LANE_TILES):
        s_ref[l, pl.ds(row, N_BLOCKS), :] = tiles[l]


SCAN_UNROLL = 8
ADJOINT_UNROLL = 15


def _unrolled_loop(n_steps, unroll, step, carry):
    assert n_steps % unroll == 0

    def steps(i, c):
        for r in range(unroll):
            c = step(unroll * i + r, c)
        return c

    return lax.fori_loop(0, n_steps // unroll, steps, carry)


def _cmul_add(a, h, x, conj):
    re, im = [], []
    for l in range(RE_TILES):
        ar, ai, hr, hi = a[l], a[RE_TILES + l], h[l], h[RE_TILES + l]
        if conj:
            re.append(ar * hr + ai * hi + x[l])
            im.append(ar * hi - ai * hr + x[RE_TILES + l])
        else:
            re.append(ar * hr - ai * hi + x[l])
            im.append(ar * hi + ai * hr + x[RE_TILES + l])
    return re + im


def _s5_scan_fwd(u, abar, b_blk, c_blk, d_skip, n_ctx, reverse, rider=None):
    n_rows, width = u.shape
    tc = SCAN_CHUNK
    n_chunks, n_ctx_chunks = n_rows // tc, n_ctx // tc
    with_skip = d_skip is not None

    def body(*refs):
        if rider is not None:
            x_ref, ride_out, sems = refs[4 + with_skip], refs[7 + with_skip], refs[-3:]
            _ride(rider, pl.program_id(0) == 0, pl.program_id(0) == n_chunks - 1, (x_ref, ride_out) + tuple(sems))
            refs = refs[:4 + with_skip] + refs[5 + with_skip:7 + with_skip] + refs[8 + with_skip:-3]
        if with_skip:
            u_ref, a_ref, b_ref, c_ref, d_ref, y_ref, hb_ref, s_ref, h_ref = refs
        else:
            u_ref, a_ref, b_ref, c_ref, y_ref, hb_ref, s_ref, h_ref = refs
        j = pl.program_id(0)

        @pl.when(j == 0)
        def _():
            h_ref[...] = jnp.zeros_like(h_ref)

        hb_ref[0] = h_ref[...]
        for q in range(N_BLOCKS):
            _scatter_steps(s_ref, q, _dot(u_ref[:, q * 128:(q + 1) * 128], b_ref[q]))
        a = _tiles(a_ref[0]) + _tiles(a_ref[1])

        def step(s, h):
            t = tc - 1 - s if reverse else s
            h = _cmul_add(a, h, _load_step(s_ref, t), conj=False)
            _store_step(s_ref, t, h)
            return h

        h = _unrolled_loop(tc, SCAN_UNROLL, step, _tiles(h_ref[...]))
        h_ref[...] = jnp.concatenate(h, axis=1)
        for q in range(N_BLOCKS):
            yq = _dot(_gather_steps(s_ref, q, tc), c_ref[q])
            if with_skip:
                yq = yq + d_ref[:, q * 128:(q + 1) * 128] * u_ref[:, q * 128:(q + 1) * 128]
            y_ref[:, q * 128:(q + 1) * 128] = yq

    chunk = functools.partial(_scan_chunk_of_step, n_chunks=n_chunks, n_ctx_chunks=n_ctx_chunks, reverse=reverse)
    full3 = lambda j: (0, 0, 0)
    in_specs = [pl.BlockSpec((tc, width), lambda j: (chunk(j), 0)),
                pl.BlockSpec((2, N_BLOCKS, HALF), full3),
                pl.BlockSpec((N_BLOCKS, 128, 2 * HALF), full3),
                pl.BlockSpec((N_BLOCKS, 2 * HALF, 128), full3)]
    args = [u, abar, b_blk.astype(BF16), c_blk.astype(BF16)]
    if with_skip:
        in_specs.append(pl.BlockSpec((1, width), lambda j: (0, 0)))
        args.append(d_skip.reshape(1, width))
    out_specs = [pl.BlockSpec((tc, width), lambda j: (chunk(j), 0)),
                 pl.BlockSpec((1, N_BLOCKS, 2 * HALF), lambda j: (chunk(j), 0, 0))]
    out_shape = [_sds((n_rows, width)), _sds((n_chunks, N_BLOCKS, 2 * HALF))]
    scratch = [pltpu.VMEM((LANE_TILES, tc * N_BLOCKS, 128), F32), pltpu.VMEM((N_BLOCKS, 2 * HALF), F32)]
    if rider is not None:
        in_specs.append(HBM_SPEC)
        args.append(rider[0])
        out_specs.append(HBM_SPEC)
        out_shape.append(_exchange_out_shape(*rider))
        scratch += _exchange_semaphores()
    return pl.pallas_call(
        body, name="s5_scan_fwd_rev" if reverse else "s5_scan_fwd",
        grid=(n_chunks,), in_specs=in_specs, out_specs=out_specs, out_shape=out_shape, scratch_shapes=scratch,
        compiler_params=_cparams(1),
    )(*args)


def _s5_scan_bwd(u, dy, hb, abar, b_blk, c_blk, d_skip, n_ctx, reverse, rider=None):
    n_rows, width = u.shape
    tc = SCAN_CHUNK
    n_chunks, n_ctx_chunks = n_rows // tc, n_ctx // tc
    with_skip = d_skip is not None
    n_in, n_out = 7 + with_skip, 4 + with_skip

    def body(*refs):
        if rider is not None:
            x_ref, ride_out, sems = refs[n_in], refs[n_in + 1 + n_out], refs[-3:]
            _ride(rider, pl.program_id(0) == 0, pl.program_id(0) == n_chunks - 1, (x_ref, ride_out) + tuple(sems))
            refs = refs[:n_in] + refs[n_in + 1:n_in + 1 + n_out] + refs[n_in + 2 + n_out:-3]
        if with_skip:
            (u_ref, dy_ref, hb_ref, a_ref, b_ref, bt_ref, ct_ref, d_ref,
             du_ref, da_ref, db_ref, dct_ref, dd_ref, sh_ref, sg_ref, g_ref) = refs
        else:
            (u_ref, dy_ref, hb_ref, a_ref, b_ref, bt_ref, ct_ref,
             du_ref, da_ref, db_ref, dct_ref, sh_ref, sg_ref, g_ref) = refs
        j = pl.program_id(0)

        @pl.when(j == 0)
        def _():
            g_ref[...] = jnp.zeros_like(g_ref)
            da_ref[...] = jnp.zeros_like(da_ref)
            db_ref[...] = jnp.zeros_like(db_ref)
            dct_ref[...] = jnp.zeros_like(dct_ref)
            if with_skip:
                dd_ref[...] = jnp.zeros_like(dd_ref)

        for q in range(N_BLOCKS):
            _scatter_steps(sh_ref, q, _dot(u_ref[:, q * 128:(q + 1) * 128], b_ref[q]))
            _scatter_steps(sg_ref, q, _dot(dy_ref[:, q * 128:(q + 1) * 128], ct_ref[q]))
        a = _tiles(a_ref[0]) + _tiles(a_ref[1])
        time_of = (lambda s: tc - 1 - s) if reverse else (lambda s: s)

        def fwd_step(s, h):
            h = _cmul_add(a, h, _load_step(sh_ref, time_of(s)), conj=False)
            _store_step(sh_ref, time_of(s), h)
            return h

        h0 = _tiles(hb_ref[0])
        _unrolled_loop(tc, SCAN_UNROLL, fwd_step, h0)

        def adj(t, h_prev, carry):
            g, da = carry
            g = _cmul_add(a, g, _load_step(sg_ref, t), conj=True)
            _store_step(sg_ref, t, g)
            da_re = [da[l] + g[l] * h_prev[l] + g[RE_TILES + l] * h_prev[RE_TILES + l] for l in range(RE_TILES)]
            da_im = [da[RE_TILES + l] + g[RE_TILES + l] * h_prev[l] - g[l] * h_prev[RE_TILES + l] for l in range(RE_TILES)]
            return g, da_re + da_im

        def bwd_step(i, carry):
            s = tc - 1 - i
            return adj(time_of(s), _load_step(sh_ref, time_of(s - 1)), carry)

        carry = (_tiles(g_ref[...]), _tiles(da_ref[0]) + _tiles(da_ref[1]))
        carry = _unrolled_loop(tc - 1, ADJOINT_UNROLL, bwd_step, carry)
        g, da = adj(time_of(0), h0, carry)
        g_ref[...] = jnp.concatenate(g, axis=1)
        da_ref[0] = jnp.concatenate(da[:RE_TILES], axis=1)
        da_ref[1] = jnp.concatenate(da[RE_TILES:], axis=1)

        for q in range(N_BLOCKS):
            cols = slice(q * 128, (q + 1) * 128)
            uq, dyq = u_ref[:, cols], dy_ref[:, cols]
            gq = _gather_steps(sg_ref, q, tc)
            duq = _dot(gq, bt_ref[q])
            if with_skip:
                duq = duq + d_ref[:, cols] * dyq
                dd_ref[:, cols] += jnp.sum(dyq * uq, axis=0, keepdims=True)
            du_ref[:, cols] = duq
            db_ref[q] += _dot_t0(uq, gq)
            dct_ref[q] += _dot_t0(dyq, _gather_steps(sh_ref, q, tc))

    def chunk(j):
        return _scan_chunk_of_step(n_chunks - 1 - j, n_chunks, n_ctx_chunks, reverse)

    full2 = lambda j: (0, 0)
    full3 = lambda j: (0, 0, 0)
    row = pl.BlockSpec((tc, width), lambda j: (chunk(j), 0))
    in_specs = [row, row,
                pl.BlockSpec((1, N_BLOCKS, 2 * HALF), lambda j: (chunk(j), 0, 0)),
                pl.BlockSpec((2, N_BLOCKS, HALF), full3),
                pl.BlockSpec((N_BLOCKS, 128, 2 * HALF), full3),
                pl.BlockSpec((N_BLOCKS, 2 * HALF, 128), full3),
                pl.BlockSpec((N_BLOCKS, 128, 2 * HALF), full3)]
    args = [u, dy, hb, abar, b_blk.astype(BF16), jnp.swapaxes(b_blk, 1, 2).astype(BF16),
            jnp.swapaxes(c_blk, 1, 2).astype(BF16)]
    out_specs = [row,
                 pl.BlockSpec((2, N_BLOCKS, HALF), full3),
                 pl.BlockSpec((N_BLOCKS, 128, 2 * HALF), full3),
                 pl.BlockSpec((N_BLOCKS, 128, 2 * HALF), full3)]
    out_shape = [jax.ShapeDtypeStruct((n_rows, width), F32),
                 jax.ShapeDtypeStruct((2, N_BLOCKS, HALF), F32),
                 jax.ShapeDtypeStruct((N_BLOCKS, 128, 2 * HALF), F32),
                 jax.ShapeDtypeStruct((N_BLOCKS, 128, 2 * HALF), F32)]
    if with_skip:
        in_specs.append(pl.BlockSpec((1, width), full2))
        args.append(d_skip.reshape(1, width))
        out_specs.append(pl.BlockSpec((1, width), full2))
        out_shape.append(jax.ShapeDtypeStruct((1, width), F32))
    scratch = [pltpu.VMEM((LANE_TILES, tc * N_BLOCKS, 128), F32), pltpu.VMEM((LANE_TILES, tc * N_BLOCKS, 128), F32),
               pltpu.VMEM((N_BLOCKS, 2 * HALF), F32)]
    if rider is not None:
        in_specs.append(HBM_SPEC)
        args.append(rider[0])
        out_specs.append(HBM_SPEC)
        out_shape.append(_exchange_out_shape(*rider))
        scratch += _exchange_semaphores()
    return pl.pallas_call(
        body, name="s5_scan_bwd_rev" if reverse else "s5_scan_bwd",
        grid=(n_chunks,), in_specs=in_specs, out_specs=out_specs, out_shape=out_shape, scratch_shapes=scratch,
        compiler_params=_cparams(1),
    )(*args)


def _s5_dir_params(d, a_re, a_im, log_dt, b_re, b_im):
    return a_re[d], a_im[d], log_dt[d], b_re[d], b_im[d]


def _s5_forward(u, ssm, n_ctx, riders=(None, None)):
    a_re, a_im, log_dt, b_re, b_im, c_re, c_im, d_skip = ssm
    outs, saved, carried = [], [], [None, None]
    for d in range(2):
        prep = _s5_prep(*_s5_dir_params(d, a_re, a_im, log_dt, b_re, b_im))
        abar, b_blk, c_blk = _s5_blocks(*prep, c_re[d], c_im[d])
        res = _s5_scan_fwd(u, abar, b_blk, c_blk, d_skip if d == 0 else None, n_ctx, reverse=(d == 1), rider=riders[d])
        if riders[d] is not None:
            carried[d] = res[2]
        outs.append(res[0])
        saved.append((res[1], abar, b_blk, c_blk))
    return outs, saved, carried


def _s5_backward(u, dy, ssm, saved, n_ctx, riders=(None, None)):
    a_re, a_im, log_dt, b_re, b_im, c_re, c_im, d_skip = ssm
    dus, grads = [], [[] for _ in range(7)]
    d_d, carried = None, [None, None]
    for d in range(2):
        hb, abar, b_blk, c_blk = saved[d]
        res = _s5_scan_bwd(u, dy, hb, abar, b_blk, c_blk, d_skip if d == 0 else None, n_ctx, reverse=(d == 1),
                           rider=riders[d])
        if riders[d] is not None:
            carried[d], res = res[-1], res[:-1]
        if d == 0:
            du, d_abar, d_b_blk, d_ct_blk, d_d = res
        else:
            du, d_abar, d_b_blk, d_ct_blk = res
        dus.append(du)
        dbb_re, dbb_im, dc_re, dc_im = _s5_unblock(d_b_blk, d_ct_blk)
        _, vjp = jax.vjp(_s5_prep, *_s5_dir_params(d, a_re, a_im, log_dt, b_re, b_im))
        shape = (SSM_GROUPS, SSM_STATE)
        g5 = vjp((d_abar[0].reshape(shape), d_abar[1].reshape(shape), dbb_re, dbb_im))
        for k, g in enumerate(tuple(g5) + (dc_re, dc_im)):
            grads[k].append(g)
    grads = [jnp.stack(g) for g in grads]
    return dus, grads + [d_d.reshape(-1)], carried


INV_SQRT2 = 0.7071067811865476
INV_SQRT_2PI = 0.3989422804014327


def _rows(cols):
    return pl.BlockSpec((ROW_TILE, cols), lambda i: (i, 0))


def _rows_skip_ctx(cols):
    return pl.BlockSpec((ROW_TILE, cols), lambda i: (i + 1, 0))


def _rows_lat(cols):
    return pl.BlockSpec((ROW_TILE, cols), lambda i: (jnp.maximum(i - 1, 0), 0))


def _full(shape):
    nd = len(shape)
    return pl.BlockSpec(shape, lambda i: (0,) * nd)


def _seg(cols):
    return pl.BlockSpec((1, 1, cols), lambda i: (jnp.minimum(i, 1), 0, 0))


def _lat_seg(cols):
    return pl.BlockSpec((1, 1, cols), lambda i: (1, 0, 0))


def _sds(shape, dtype=F32):
    return jax.ShapeDtypeStruct(shape, dtype)


def _sum0(x):
    return jnp.sum(x, axis=0, keepdims=True)


def _sigmoid(x):
    return jax.nn.sigmoid(x)


def _rms_mod(x, g, scale, shift):
    r = lax.rsqrt(jnp.mean(x * x, axis=-1, keepdims=True) + NORM_EPS)
    return (x * r * g) * (1.0 + scale) + shift


def _rms_mod_bwd(x, g, scale, dh):
    r = lax.rsqrt(jnp.mean(x * x, axis=-1, keepdims=True) + NORM_EPS)
    n = x * r
    dyg = dh * (1.0 + scale)
    dn = dyg * g
    dx = r * (dn - n * jnp.mean(dn * n, axis=-1, keepdims=True))
    return dx, _sum0(dyg * n), _sum0(dh * (n * g)), _sum0(dh)


def _head_of_lane(width):
    return (jnp.arange(width)[:, None] // HEAD_DIM == jnp.arange(128)[None, :]).astype(BF16)


def _split_dot(t, w, transposed):
    hi = t.astype(BF16)
    lo = (t - hi.astype(F32)).astype(BF16)
    f = _dot_t1 if transposed else _dot
    return f(hi, w) + f(lo, w)


def _head_sums(t, hl):
    return _split_dot(_split_dot(t, hl, False), hl, True)


def _rope_partner(x):
    n = x.shape[1]
    lane = lax.broadcasted_iota(jnp.int32, x.shape, 1)
    return jnp.where((lane & 16) == 0, pltpu.roll(x, n - 16, 1), pltpu.roll(x, 16, 1))


def _lanes(tab, width):
    return jnp.tile(tab, (1, width // tab.shape[1]))


def _head_norm_rope(x, gain, cos, sin, hl):
    r = lax.rsqrt(_head_sums(x * x, hl) * (1.0 / HEAD_DIM) + NORM_EPS)
    y = x * r * gain
    return y * cos + _rope_partner(y) * sin


def _head_norm_rope_bwd(x, gain, cos, sin, hl, dout):
    dy = dout * cos + _rope_partner(dout * sin)
    r = lax.rsqrt(_head_sums(x * x, hl) * (1.0 / HEAD_DIM) + NORM_EPS)
    n = x * r
    dn = dy * gain
    dx = r * (dn - n * (_head_sums(dn * n, hl) * (1.0 / HEAD_DIM)))
    return dx, _sum0(dy * n)


def _rope_tables(n_ctx, n_lat):
    t = jnp.arange(n_lat)
    pos = jnp.stack([(t // GRID_W).astype(F32), (t % GRID_W).astype(F32)], axis=1)
    n_freq = HEAD_DIM // 4
    freqs = ROPE_THETA ** (-jnp.arange(n_freq, dtype=F32) / n_freq)
    ang = pos[:, :, None] * freqs[None, None, :]
    cos = jnp.repeat(jnp.cos(ang)[:, :, None, :], 2, axis=2).reshape(n_lat, HEAD_DIM)
    sin = jnp.sin(ang)
    sin = jnp.stack([-sin, sin], axis=2).reshape(n_lat, HEAD_DIM)
    cos = jnp.concatenate([jnp.ones((n_ctx, HEAD_DIM), F32), cos], axis=0)
    sin = jnp.concatenate([jnp.zeros((n_ctx, HEAD_DIM), F32), sin], axis=0)
    return jnp.tile(cos, (1, 2)), jnp.tile(sin, (1, 2))


def _token_tile(ctx_ref, x_ref):
    return jnp.where(pl.program_id(0) == 0, ctx_ref[...], x_ref[...])


def _ssm_in(ctx, x, g, scale, shift, w_in):
    n_rows, d = ctx.shape[0] + x.shape[0], x.shape[1]
    e = w_in.shape[1] // 2

    def body(c_ref, x_ref, g_ref, sc_ref, sh_ref, w_ref, u_ref, z_ref):
        h = _rms_mod(_token_tile(c_ref, x_ref), g_ref[...], sc_ref[0], sh_ref[0])
        proj = _dot(h, w_ref[...])
        u_ref[...] = proj[:, :e]
        z_ref[...] = proj[:, e:]

    return pl.pallas_call(
        body, name="ssm_in", grid=(n_rows // ROW_TILE,),
        in_specs=[_full(ctx.shape), _rows_lat(d), _full((1, d)), _seg(d), _seg(d), _full(w_in.shape)],
        out_specs=[_rows(e), _rows(e)], out_shape=[_sds((n_rows, e)), _sds((n_rows, e))],
        compiler_params=_cparams(1),
    )(ctx, x, g, scale, shift, w_in)


def _s5_post_math(y, z, w_glu, b_glu, w_out):
    er = lax.erf(y * INV_SQRT2)
    g = 0.5 * y * (1.0 + er)
    sg = _sigmoid(_dot(g, w_glu) + b_glu)
    g2 = g * sg
    sz = _sigmoid(z)
    silu_z = z * sz
    m = g2 * silu_z
    return er, g, sg, g2, sz, silu_z, m, _dot(m, w_out)


def _ssm_post(ctx, x, y0, y1, z, gate, w_glu, b_glu, w_out):
    n_rows, d = ctx.shape[0] + x.shape[0], x.shape[1]
    e = z.shape[1]

    def body(c_ref, x_ref, y0_ref, y1_ref, z_ref, gt_ref, wg_ref, bg_ref, wo_ref, o_ref):
        out = _s5_post_math(y0_ref[...] + y1_ref[...], z_ref[...], wg_ref[...], bg_ref[...], wo_ref[...])[-1]
        o_ref[...] = _token_tile(c_ref, x_ref) + gt_ref[0] * out

    return pl.pallas_call(
        body, name="ssm_post", grid=(n_rows // ROW_TILE,),
        in_specs=[_full(ctx.shape), _rows_lat(d), _rows(e), _rows(e), _rows(e), _seg(d), _full(w_glu.shape), _full((1, e)),
                  _full(w_out.shape)],
        out_specs=_rows(d), out_shape=_sds((n_rows, d)),
        compiler_params=_cparams(1),
    )(ctx, x, y0, y1, z, gate, w_glu, b_glu, w_out)


def _init_acc(first, *refs):
    @pl.when(first)
    def _():
        for r in refs:
            r[...] = jnp.zeros_like(r)


def _ssm_post_bwd(dxa, y0, y1, z, gate, w_glu, b_glu, w_out):
    n_rows, d = dxa.shape
    e = z.shape[1]

    def body(dx_ref, y0_ref, y1_ref, z_ref, gt_ref, wg_ref, bg_ref, wo_ref,
             dy_ref, dz_ref, dgt_ref, dwo_ref, dwg_ref, dbg_ref):
        i = pl.program_id(0)
        _init_acc(i == 0, dwo_ref, dwg_ref, dbg_ref)
        _init_acc(i <= 1, dgt_ref)
        y, zz = y0_ref[...] + y1_ref[...], z_ref[...]
        er, g, sg, g2, sz, silu_z, m, out = _s5_post_math(y, zz, wg_ref[...], bg_ref[...], wo_ref[...])
        dxa_t = dx_ref[...]
        dgt_ref[0] += _sum0(dxa_t * out)
        dout = gt_ref[0] * dxa_t
        dm = _dot_t1(dout, wo_ref[...])
        dwo_ref[...] += _dot_t0(m, dout)
        dg2 = dm * silu_z
        dz_ref[...] = dm * g2 * (sz * (1.0 + zz * (1.0 - sz)))
        dt = dg2 * g * sg * (1.0 - sg)
        dwg_ref[...] += _dot_t0(g, dt)
        dbg_ref[...] += _sum0(dt)
        dg = dg2 * sg + _dot_t1(dt, wg_ref[...])
        dy_ref[...] = dg * (0.5 * (1.0 + er) + y * jnp.exp(-0.5 * y * y) * INV_SQRT_2PI)

    return pl.pallas_call(
        body, name="ssm_post_bwd", grid=(n_rows // ROW_TILE,),
        in_specs=[_rows(d), _rows(e), _rows(e), _rows(e), _seg(d), _full(w_glu.shape), _full((1, e)), _full(w_out.shape)],
        out_specs=[_rows(e), _rows(e), _seg(d), _full(w_out.shape), _full(w_glu.shape), _full((1, e))],
        out_shape=[_sds((n_rows, e)), _sds((n_rows, e)), _sds((2, 1, d)), _sds(w_out.shape), _sds(w_glu.shape), _sds((1, e))],
        compiler_params=_cparams(1),
    )(dxa, y0, y1, z, gate, w_glu, b_glu, w_out)


def _ssm_in_bwd(du0, du1, dz, ctx, x_lat, dxa_next, g, scale, shift, w_in):
    n_lat, d = x_lat.shape
    n_rows = ctx.shape[0] + n_lat
    e = dz.shape[1]

    def body(du0_ref, du1_ref, dz_ref, c_ref, x_ref, dn_ref, g_ref, sc_ref, sh_ref, w_ref,
             gx_ref, dw_ref, dg_ref, dsc_ref, dsh_ref):
        i = pl.program_id(0)
        _init_acc(i == 0, dw_ref, dg_ref)
        _init_acc(i <= 1, dsc_ref, dsh_ref)
        x = _token_tile(c_ref, x_ref)
        h = _rms_mod(x, g_ref[...], sc_ref[0], sh_ref[0])
        dproj = jnp.concatenate([du0_ref[...] + du1_ref[...], dz_ref[...]], axis=1)
        dh = _dot_t1(dproj, w_ref[...])
        dw_ref[...] += _dot_t0(h, dproj)
        dx, dg, dsc, dsh = _rms_mod_bwd(x, g_ref[...], sc_ref[0], dh)
        dg_ref[...] += dg
        dsc_ref[0] += dsc
        dsh_ref[0] += dsh
        gx_ref[...] = dn_ref[...] + dx

    return pl.pallas_call(
        body, name="ssm_in_bwd", grid=(n_rows // ROW_TILE,),
        in_specs=[_rows(e), _rows(e), _rows(e), _full(ctx.shape), _rows_lat(d), _rows(d), _full((1, d)), _seg(d), _seg(d),
                  _full(w_in.shape)],
        out_specs=[_rows_lat(d), _full((d, 2 * e)), _full((1, d)), _seg(d), _seg(d)],
        out_shape=[_sds((n_lat, d)), _sds((d, 2 * e)), _sds((1, d)), _sds((2, 1, d)), _sds((2, 1, d))],
        compiler_params=_cparams(1),
    )(du0, du1, dz, ctx, x_lat, dxa_next, g, scale, shift, w_in)


Q_WIDTH = N_Q_HEADS * HEAD_DIM
SM_SCALE = 1.0 / math.sqrt(HEAD_DIM)


def _attn_in(xa, g, scale, shift, w_in, q_gain, k_gain, cos, sin):
    n_rows, d = xa.shape
    qk = Q_WIDTH + KV_WIDTH

    def body(x_ref, g_ref, sc_ref, sh_ref, w_ref, qg_ref, kg_ref, cos_ref, sin_ref, hq_ref, hk_ref,
             q_ref, k_ref, v_ref, z_ref, raw_ref):
        h = _rms_mod(x_ref[...], g_ref[...], sc_ref[0], sh_ref[0])
        proj = _dot(h, w_ref[...])
        q_raw, k_raw = proj[:, :Q_WIDTH], proj[:, Q_WIDTH:qk]
        cos, sin = cos_ref[...], sin_ref[...]
        q = _head_norm_rope(q_raw, qg_ref[...], _lanes(cos, Q_WIDTH), _lanes(sin, Q_WIDTH), hq_ref[...])
        k = _head_norm_rope(k_raw, kg_ref[...], _lanes(cos, KV_WIDTH), _lanes(sin, KV_WIDTH), hk_ref[...])
        q_ref[...] = (q * SM_SCALE).astype(BF16)
        k_ref[...] = k.astype(BF16)
        v_ref[...] = proj[:, qk:qk + KV_WIDTH].astype(BF16)
        z_ref[...] = proj[:, qk + KV_WIDTH:]
        raw_ref[...] = proj[:, :qk]

    return pl.pallas_call(
        body, name="attn_in", grid=(n_rows // ROW_TILE,),
        in_specs=[_rows(d), _full((1, d)), _seg(d), _seg(d), _full(w_in.shape), _full((1, Q_WIDTH)), _full((1, KV_WIDTH)),
                  _rows(128), _rows(128), _full((Q_WIDTH, 128)), _full((KV_WIDTH, 128))],
        out_specs=[_rows_lat(Q_WIDTH), _rows(KV_WIDTH), _rows(KV_WIDTH), _rows(Q_WIDTH), _rows(qk)],
        out_shape=[_sds((n_rows - ROW_TILE, Q_WIDTH), BF16), _sds((n_rows, KV_WIDTH), BF16), _sds((n_rows, KV_WIDTH), BF16),
                   _sds((n_rows, Q_WIDTH)), _sds((n_rows, qk))],
        compiler_params=_cparams(1),
    )(xa, g, scale, shift, w_in, q_gain, k_gain, cos, sin, _head_of_lane(Q_WIDTH), _head_of_lane(KV_WIDTH))


GROUP_WIDTH = KV_REP * HEAD_DIM


def _stack_heads(ref):
    return jnp.concatenate([ref[:, h * HEAD_DIM:(h + 1) * HEAD_DIM] for h in range(KV_REP)], axis=0)


def _unstack_heads(a_t, tq):
    return jnp.concatenate([a_t[:, h * tq:(h + 1) * tq].T for h in range(KV_REP)], axis=1)


def _kv_tile(n_keys):
    return 768 if n_keys % 768 == 0 else 256


def _kv_tile_fwd(n_keys):
    return 1408 if n_keys % 1408 == 0 else _kv_tile(n_keys)


def _q_tile(n_lat):
    return 512 if n_lat % 512 == 0 else 256


def _flash_fwd(q, k, v_t):
    n_lat = q.shape[0]
    tq = _q_tile(n_lat)
    rows = KV_REP * tq
    n_kv, tk, n_q = k.shape[1], k.shape[2], n_lat // tq

    v_rows = v_t.shape[2]

    def body(q_ref, k_ref, vt_ref, o_ref, lse_ref):
        q = _stack_heads(q_ref)

        def step(j, carry):
            m_prev, acc = carry
            s_t = _dot_t1(k_ref[0, j], q)
            m_new = jnp.maximum(m_prev, jnp.max(s_t, axis=0, keepdims=True))
            alpha = jnp.exp(m_prev - m_new)
            p_t = jnp.exp(s_t - m_new)
            return m_new, alpha * acc + _dot(vt_ref[0, j], p_t)

        init = (jnp.full((1, rows), -jnp.inf, F32), jnp.zeros((v_rows, rows), F32))
        m, acc = lax.fori_loop(0, n_kv, step, init)
        l = acc[HEAD_DIM:HEAD_DIM + 1]
        o_ref[...] = _unstack_heads(acc[:HEAD_DIM] / l, tq)
        lse_ref[0, 0] = m + jnp.log(l)

    kv_all = lambda a: pl.BlockSpec((1,) + a.shape[1:], lambda g, i: (g, 0, 0, 0))
    return pl.pallas_call(
        body, name="flash_fwd", grid=(N_KV_HEADS, n_q),
        in_specs=[pl.BlockSpec((tq, GROUP_WIDTH), lambda g, i: (i, g)), kv_all(k), kv_all(v_t)],
        out_specs=[pl.BlockSpec((tq, GROUP_WIDTH), lambda g, i: (i, g)),
                   pl.BlockSpec((1, 1, 1, rows), lambda g, i: (g, i, 0, 0))],
        out_shape=[_sds((n_lat, Q_WIDTH)), _sds((N_KV_HEADS, n_q, 1, rows))],
        compiler_params=_cparams(2),
    )(q, k, v_t)


def _flash_bwd(q, k, k_t, v, do, lse_t, delta_t):
    n_lat = q.shape[0]
    tq = _q_tile(n_lat)
    rows = KV_REP * tq
    n_kv, tk, n_q = k.shape[1], k.shape[2], n_lat // tq

    def body(q_ref, k_ref, kt_ref, v_ref, do_ref, lse_ref, dl_ref, dq_ref, dk_ref, dv_ref):
        _init_acc(pl.program_id(1) == 0, dk_ref, dv_ref)
        q, do = _stack_heads(q_ref), _stack_heads(do_ref)
        lse, delta = lse_ref[0, 0], dl_ref[0, 0]

        def step(j, dq_acc):
            p_t = jnp.exp(_dot_t1(k_ref[0, j], q) - lse)
            dv_ref[0, j] += _dot(p_t, do)
            ds_t = p_t * (_dot_t1(v_ref[0, j], do) - delta)
            dk_ref[0, j] += _dot(ds_t, q)
            return dq_acc + _dot(kt_ref[0, j], ds_t)

        dq = lax.fori_loop(0, n_kv, step, jnp.zeros((HEAD_DIM, rows), F32))
        dq_ref[...] = _unstack_heads(dq, tq)

    qspec = pl.BlockSpec((tq, GROUP_WIDTH), lambda g, i: (i, g))
    rowspec = pl.BlockSpec((1, 1, 1, rows), lambda g, i: (g, i, 0, 0))
    kv_all = lambda a: pl.BlockSpec((1,) + a.shape[1:], lambda g, i: (g, 0, 0, 0))
    return pl.pallas_call(
        body, name="flash_bwd", grid=(N_KV_HEADS, n_q),
        in_specs=[qspec, kv_all(k), kv_all(k_t), kv_all(v), qspec, rowspec, rowspec],
        out_specs=[qspec, kv_all(k), kv_all(k)],
        out_shape=[_sds((n_lat, Q_WIDTH)), _sds(k.shape), _sds(k.shape)],
        compiler_params=_cparams(2),
    )(q, k, k_t, v, do, lse_t, delta_t)


def _to_lane_stacked(a, tq):
    n_lat = a.shape[0]
    a = a.reshape(n_lat // tq, tq, N_KV_HEADS, KV_REP).transpose(2, 0, 3, 1)
    return a.reshape(N_KV_HEADS, n_lat // tq, 1, KV_REP * tq)


def _attn_post_loss(o, z, xa, gate, w_out, final_g, target):
    n_lat, d = target.shape
    e = o.shape[1]
    head_of_lane = (jnp.arange(e)[:, None] // HEAD_DIM == jnp.arange(128)[None, :]).astype(BF16)

    def body(o_ref, z_ref, x_ref, gt_ref, w_ref, fg_ref, tg_ref, hl_ref,
             do_ref, dl_ref, dz_ref, dx_ref, loss_ref, dfg_ref, dgt_ref, dw_ref):
        _init_acc(pl.program_id(0) == 0, loss_ref, dfg_ref, dgt_ref, dw_ref)
        oo, zz, gate_t, fg = o_ref[...], z_ref[...], gt_ref[0], fg_ref[...]
        sz = _sigmoid(zz)
        silu_z = zz * sz
        m = oo * silu_z
        out = _dot(m, w_ref[...])
        x2 = x_ref[...] + gate_t * out
        r = lax.rsqrt(jnp.mean(x2 * x2, axis=-1, keepdims=True) + NORM_EPS)
        n = x2 * r
        err = n * fg - tg_ref[...]
        loss_ref[...] += 0.5 * jnp.sum(jnp.mean(err * err, axis=-1, keepdims=True), axis=0, keepdims=True)
        dy = err * (1.0 / d)
        dfg_ref[...] += _sum0(dy * n)
        dn = dy * fg
        dx2 = r * (dn - n * jnp.mean(dn * n, axis=-1, keepdims=True))
        dx_ref[...] = dx2
        dgt_ref[...] += _sum0(dx2 * out)
        dout = gate_t * dx2
        dw_ref[...] += _dot_t0(m, dout)
        dm = _dot_t1(dout, w_ref[...])
        do = dm * silu_z
        do_ref[...] = do.astype(BF16)
        prod = do * oo
        hi = prod.astype(BF16)
        lo = (prod - hi.astype(F32)).astype(BF16)
        dl_ref[...] = _dot(hi, hl_ref[...]) + _dot(lo, hl_ref[...])
        dz_ref[...] = dm * oo * (sz * (1.0 + zz * (1.0 - sz)))

    return pl.pallas_call(
        body, name="attn_post_loss", grid=(n_lat // ROW_TILE,),
        in_specs=[_rows(e), _rows_skip_ctx(e), _rows_skip_ctx(d), _lat_seg(d), _full(w_out.shape),
                  _full((1, d)), _rows(d), _full((e, 128))],
        out_specs=[_rows(e), _rows(128), _rows(e), _rows(d), _full((1, 1)), _full((1, d)), _full((1, d)), _full(w_out.shape)],
        out_shape=[_sds((n_lat, e), BF16), _sds((n_lat, 128)), _sds((n_lat, e)), _sds((n_lat, d)), _sds((1, 1)), _sds((1, d)),
                   _sds((1, d)), _sds(w_out.shape)],
        compiler_params=_cparams(1),
    )(o, z, xa, gate, w_out, final_g, target, head_of_lane)


def _attn_in_bwd(dq, dk, dv, dz, raw, xa, dx2, g, scale, shift, q_gain, k_gain, cos, sin, w_in):
    n_rows, d = xa.shape
    qk = Q_WIDTH + KV_WIDTH
    n_in = w_in.shape[1]

    def body(dq_ref, dk_ref, dv_ref, dz_ref, raw_ref, x_ref, dx2_ref, g_ref, sc_ref, sh_ref, qg_ref, kg_ref, cos_ref, sin_ref,
             w_ref, hq_ref, hk_ref, dxa_ref, dw_ref, dqg_ref, dkg_ref, dg_ref, dsc_ref, dsh_ref):
        i = pl.program_id(0)
        _init_acc(i == 0, dw_ref, dqg_ref, dkg_ref, dg_ref)
        _init_acc(i <= 1, dsc_ref, dsh_ref)
        is_lat = (i > 0).astype(F32)
        x = x_ref[...]
        h = _rms_mod(x, g_ref[...], sc_ref[0], sh_ref[0])
        cos, sin = cos_ref[...], sin_ref[...]
        raw_t = raw_ref[...]
        dq_raw, dqg = _head_norm_rope_bwd(raw_t[:, :Q_WIDTH], qg_ref[...], _lanes(cos, Q_WIDTH), _lanes(sin, Q_WIDTH),
                                          hq_ref[...], dq_ref[...] * (SM_SCALE * is_lat))
        dk_raw, dkg = _head_norm_rope_bwd(raw_t[:, Q_WIDTH:], kg_ref[...], _lanes(cos, KV_WIDTH), _lanes(sin, KV_WIDTH),
                                          hk_ref[...], dk_ref[...])
        dqg_ref[...] += dqg
        dkg_ref[...] += dkg
        dproj = jnp.concatenate([dq_raw, dk_raw, dv_ref[...], dz_ref[...] * is_lat], axis=1)
        dh = _dot_t1(dproj, w_ref[...])
        dw_ref[...] += _dot_t0(h, dproj)
        dx, dg, dsc, dsh = _rms_mod_bwd(x, g_ref[...], sc_ref[0], dh)
        dg_ref[...] += dg
        dsc_ref[0] += dsc
        dsh_ref[0] += dsh
        dxa_ref[...] = dx + dx2_ref[...] * is_lat

    return pl.pallas_call(
        body, name="attn_in_bwd", grid=(n_rows // ROW_TILE,),
        in_specs=[_rows_lat(Q_WIDTH), _rows(KV_WIDTH), _rows(KV_WIDTH), _rows_lat(Q_WIDTH), _rows(qk), _rows(d), _rows_lat(d),
                  _full((1, d)), _seg(d), _seg(d), _full((1, Q_WIDTH)), _full((1, KV_WIDTH)), _rows(128), _rows(128),
                  _full(w_in.shape), _full((Q_WIDTH, 128)), _full((KV_WIDTH, 128))],
        out_specs=[_rows(d), _full((d, n_in)), _full((1, Q_WIDTH)), _full((1, KV_WIDTH)), _full((1, d)), _seg(d), _seg(d)],
        out_shape=[_sds((n_rows, d)), _sds((d, n_in)), _sds((1, Q_WIDTH)), _sds((1, KV_WIDTH)), _sds((1, d)),
                   _sds((2, 1, d)), _sds((2, 1, d))],
        compiler_params=_cparams(1),
    )(dq, dk, dv, dz, raw, xa, dx2, g, scale, shift, q_gain, k_gain, cos, sin, w_in,
      _head_of_lane(Q_WIDTH), _head_of_lane(KV_WIDTH))


def _heads_major(a, n_heads):
    return a.reshape(a.shape[0], n_heads, HEAD_DIM).transpose(1, 0, 2)


def _tokens_major(a):
    return a.transpose(1, 0, 2).reshape(a.shape[1], a.shape[0] * HEAD_DIM)


def _local_step(x, ctx, target, mods, norm_g, ssm, w_ssm_in, w_glu, b_glu, w_ssm_out, w_attn_in, q_norm, k_norm, w_attn_out,
                final_g, attn_exchange=None, post_exchange=None):
    n_ctx, d = ctx.shape
    assert n_ctx == ROW_TILE
    n_lat = x.shape[0]
    (shift0, scale0, gate0), (shift1, scale1, gate1) = mods
    g0, g1, fg = norm_g[0:1], norm_g[1:2], final_g.reshape(1, d)
    b_glu = b_glu.reshape(1, -1)
    q_gain = jnp.tile(q_norm.reshape(1, HEAD_DIM), (1, N_Q_HEADS))
    k_gain = jnp.tile(k_norm.reshape(1, HEAD_DIM), (1, N_KV_HEADS))
    cos, sin = _rope_tables(n_ctx, n_lat)

    u, z0 = _ssm_in(ctx, x, g0, scale0, shift0, w_ssm_in)
    gather = lambda ex: (ex[0], False) if ex else None
    (y0, y1), saved, gathered = _s5_forward(u, ssm, n_ctx, (gather(attn_exchange), gather(post_exchange)))
    if attn_exchange:
        w_attn_in, w_attn_out = attn_exchange[1](gathered[0])
    if post_exchange:
        w_glu, w_ssm_out = post_exchange[1](gathered[1])
    xa1 = _ssm_post(ctx, x, y0, y1, z0, gate0, w_glu, b_glu, w_ssm_out)

    q, k, v, z1, raw = _attn_in(xa1, g1, scale1, shift1, w_attn_in, q_gain, k_gain, cos, sin)
    tq, tk, tk_fwd = _q_tile(n_lat), _kv_tile(n_ctx + n_lat), _kv_tile_fwd(n_ctx + n_lat)
    k_h, v_h = _heads_major(k, N_KV_HEADS), _heads_major(v, N_KV_HEADS)
    k_b, v_b = k_h.reshape(N_KV_HEADS, -1, tk, HEAD_DIM), v_h.reshape(N_KV_HEADS, -1, tk, HEAD_DIM)
    v_t = v_h.reshape(N_KV_HEADS, -1, tk_fwd, HEAD_DIM).transpose(0, 1, 3, 2)
    v_t_ones = jnp.concatenate([v_t, jnp.ones(v_t.shape[:2] + (16, tk_fwd), BF16)], axis=2)
    o, lse_t = _flash_fwd(q, k_h.reshape(N_KV_HEADS, -1, tk_fwd, HEAD_DIM), v_t_ones)
    do, delta, dz1, dx2, loss, d_fg, d_gate1, d_w_attn_out = _attn_post_loss(
        o, z1, xa1, gate1, w_attn_out, fg, target)

    dq, dk_b, dv_b = _flash_bwd(q, k_b, k_b.transpose(0, 1, 3, 2), v_b, do, lse_t, _to_lane_stacked(delta[:, :N_Q_HEADS], tq))
    keys_major = lambda a: _tokens_major(a.reshape(N_KV_HEADS, -1, HEAD_DIM))
    dxa1, d_w_attn_in, d_qg, d_kg, d_g1, d_scale1, d_shift1 = _attn_in_bwd(
        dq, keys_major(dk_b), keys_major(dv_b), dz1, raw, xa1, dx2, g1, scale1, shift1,
        q_gain, k_gain, cos, sin, w_attn_in)
    dy, dz0, d_gate0, d_w_ssm_out, d_w_glu, d_b_glu = _ssm_post_bwd(
        dxa1, y0, y1, z0, gate0, w_glu, b_glu, w_ssm_out)
    scatter = lambda ex, *g: (ex[2](*g), True) if ex else None
    (du0, du1), d_ssm, parts = _s5_backward(
        u, dy, ssm, saved, n_ctx,
        (scatter(attn_exchange, d_w_attn_in, d_w_attn_out), scatter(post_exchange, d_w_glu, d_w_ssm_out)))
    grad_x, d_w_ssm_in, d_g0, d_scale0, d_shift0 = _ssm_in_bwd(du0, du1, dz0, ctx, x, dxa1, g0, scale0, shift0, w_ssm_in)

    d_gate1_seg = jnp.concatenate([jnp.zeros((1, 1, d), F32), d_gate1.reshape(1, 1, d)], axis=0)
    grads = dict(
        norm_g=jnp.concatenate([d_g0, d_g1], axis=0), ssm_w_in=d_w_ssm_in, ssm=d_ssm, ssm_b_glu=d_b_glu.reshape(-1),
        attn_q_norm=d_qg.reshape(N_Q_HEADS, HEAD_DIM).sum(0), attn_k_norm=d_kg.reshape(N_KV_HEADS, HEAD_DIM).sum(0),
        final_norm_g=d_fg.reshape(-1))
    if attn_exchange:
        grads.update(attn_parts=parts[0])
    else:
        grads.update(attn_w_in=d_w_attn_in, attn_w_out=d_w_attn_out)
    if post_exchange:
        grads.update(post_parts=parts[1])
    else:
        grads.update(ssm_w_glu=d_w_glu, ssm_w_out=d_w_ssm_out)
    d_mods = ((d_shift0, d_scale0, d_gate0), (d_shift1, d_scale1, d_gate1_seg))
    return loss[0, 0], grad_x, grads, d_mods


def _my_index():
    return 4 * lax.axis_index("x") + 2 * lax.axis_index("y") + lax.axis_index("c")


def _peer(k):
    mx, my, mc = lax.axis_index("x"), lax.axis_index("y"), lax.axis_index("c")
    px = 1 - mx if k & 4 else mx
    py = 1 - my if k & 2 else my
    pc = 1 - mc if k & 1 else mc
    return (px, py, pc), 4 * px + 2 * py + pc


HBM_SPEC = pl.BlockSpec(memory_space=pl.ANY)


def _exchange(x, name, all_to_all):
    def body(x_ref, out_ref, send_sems, recv_sems, local_sem):
        _exchange_copies(all_to_all, x_ref, out_ref, send_sems, recv_sems, local_sem, start=True)
        _exchange_copies(all_to_all, x_ref, out_ref, send_sems, recv_sems, local_sem, start=False)

    return pl.pallas_call(
        body, name=name, in_specs=[HBM_SPEC], out_specs=HBM_SPEC,
        out_shape=_exchange_out_shape(x, all_to_all), scratch_shapes=_exchange_semaphores(),
    )(x)


def _exchange_out_shape(x, all_to_all):
    return _sds((N_DEV,) + tuple(x.shape[1:] if all_to_all else x.shape), x.dtype)


def _exchange_semaphores():
    return [pltpu.SemaphoreType.DMA((N_DEV - 1,)), pltpu.SemaphoreType.DMA((N_DEV - 1,)), pltpu.SemaphoreType.DMA]


def _exchange_copies(all_to_all, x_ref, out_ref, send_sems, recv_sems, local_sem, start):
    me = _my_index()
    mine = pltpu.make_async_copy(x_ref.at[me] if all_to_all else x_ref, out_ref.at[me], local_sem)
    if start:
        mine.start()
    for k in range(1, N_DEV):
        peer, peer_idx = _peer(k)
        send = pltpu.make_async_remote_copy(
            src_ref=x_ref.at[peer_idx] if all_to_all else x_ref, dst_ref=out_ref.at[me],
            send_sem=send_sems.at[k - 1], recv_sem=recv_sems.at[k - 1], device_id=peer, device_id_type=MESH_IDS)
        if start:
            send.start()
        else:
            pltpu.make_async_remote_copy(
                src_ref=x_ref.at[me] if all_to_all else x_ref, dst_ref=out_ref.at[peer_idx],
                send_sem=send_sems.at[k - 1], recv_sem=recv_sems.at[k - 1], device_id=peer,
                device_id_type=MESH_IDS).wait_recv()
            send.wait_send()
    if not start:
        mine.wait()


def _ride(rider, first, last, refs):
    @pl.when(first)
    def _():
        _exchange_copies(rider[1], *refs, start=True)

    @pl.when(last)
    def _():
        _exchange_copies(rider[1], *refs, start=False)


MOD_ROWS = 16
CTX_ROW = N_DEV


def _mod_fwd(cond, w_shard, b_cols):
    n_layers, d, cols = w_shard.shape

    def body(c_ref, w_ref, b_ref, o_ref):
        c = c_ref[...]
        s = c * _sigmoid(c)
        for i in range(n_layers):
            o_ref[i] = _dot(s, w_ref[i]) + b_ref[i]

    return pl.pallas_call(
        body, name="mod_fwd", out_shape=_sds((n_layers, MOD_ROWS, cols)),
        compiler_params=pltpu.CompilerParams(vmem_limit_bytes=VMEM_LIMIT),
    )(cond, w_shard, b_cols.reshape(n_layers, 1, cols))


def _mod_bwd(cond, d_lat_cols, d_ctx_cols, w_shard):
    n_layers, d, cols = w_shard.shape

    def body(c_ref, dl_ref, dc_ref, w_ref, dw_ref, dcc_ref):
        c = c_ref[...]
        sg = _sigmoid(c)
        s = c * sg
        d_s = jnp.zeros((MOD_ROWS, d), F32)
        for i in range(n_layers):
            d_ctx = dc_ref[0, i]
            for j in range(1, N_DEV):
                d_ctx = d_ctx + dc_ref[j, i]
            dm = jnp.concatenate([dl_ref[i], d_ctx, jnp.zeros((MOD_ROWS - N_DEV - 1, cols), F32)], axis=0)
            dw_ref[i] = _dot_t0(s, dm)
            d_s = d_s + _dot_t1(dm, w_ref[i])
        d_c = d_s * (sg * (1.0 + c * (1.0 - sg)))
        dcc_ref[...] = d_c[CTX_ROW:CTX_ROW + 1]

    return pl.pallas_call(
        body, name="mod_bwd", out_shape=[_sds((n_layers, d, cols)), _sds((1, d))],
        compiler_params=pltpu.CompilerParams(vmem_limit_bytes=VMEM_LIMIT),
    )(cond, d_lat_cols, d_ctx_cols, w_shard)


ADAM_TILE = 512


def _adamw(w, g_parts, m, v, name):
    n_parts, n_rows, lanes = g_parts.shape
    tile = min(ADAM_TILE, n_rows)
    assert n_rows % tile == 0
    c1 = 1.0 - ADAM_B1 ** ADAM_STEP
    c2 = 1.0 - ADAM_B2 ** ADAM_STEP

    def body(w_ref, g_ref, m_ref, v_ref, go_ref, d_ref, mo_ref, vo_ref):
        g = g_ref[0].astype(F32)
        for p in range(1, n_parts):
            g = g + g_ref[p].astype(F32)
        m_new = ADAM_B1 * m_ref[...] + (1.0 - ADAM_B1) * g
        v_new = ADAM_B2 * v_ref[...] + (1.0 - ADAM_B2) * (g * g)
        go_ref[...] = g
        mo_ref[...] = m_new
        vo_ref[...] = v_new
        d_ref[...] = -ADAM_LR * ((m_new / c1) / (jnp.sqrt(v_new / c2) + ADAM_EPS) + ADAM_WD * w_ref[...])

    row = pl.BlockSpec((tile, lanes), lambda i: (i, 0))
    return pl.pallas_call(
        body, name=name, grid=(n_rows // tile,),
        in_specs=[row, pl.BlockSpec((n_parts, tile, lanes), lambda i: (0, i, 0)), row, row],
        out_specs=[row] * 4, out_shape=[_sds((n_rows, lanes))] * 4,
        compiler_params=_cparams(1),
    )(w, g_parts, m, v)


def _sum_parts(parts):
    n_parts, n_rows, lanes = parts.shape

    def body(p_ref, o_ref):
        acc = p_ref[0]
        for p in range(1, n_parts):
            acc = acc + p_ref[p]
        o_ref[...] = acc

    return pl.pallas_call(body, name="sum_parts", out_shape=_sds((n_rows, lanes)))(parts)


def _pack(arrays, row_multiple):
    parts = []
    for a in arrays:
        flat = a.reshape(-1)
        parts.append(jnp.pad(flat, (0, (-flat.shape[0]) % 1024)))
    flat = jnp.concatenate(parts)
    flat = jnp.pad(flat, (0, (-flat.shape[0]) % (row_multiple * 128)))
    return flat.reshape(-1, 128)


def _unpack(packed, shapes):
    flat = packed.reshape(-1)
    out, pos = [], 0
    for s in shapes:
        n = math.prod(s)
        out.append(flat[pos:pos + n].reshape(s))
        pos += n + (-n) % 1024
    return out


WEIGHT_NAMES = ['c_ctx', 'w_mod', 'b_mod', 'norm_g', 'ssm_w_in', 'ssm_a_re', 'ssm_a_im', 'ssm_log_dt', 'ssm_b_re', 'ssm_b_im',
                'ssm_c_re', 'ssm_c_im', 'ssm_d', 'ssm_w_glu', 'ssm_b_glu', 'ssm_w_out', 'attn_w_in', 'attn_q_norm',
                'attn_k_norm', 'attn_w_out', 'final_norm_g']
FIRST_SHARDED = ['ssm_w_in']
POST_SHARDED = ['ssm_w_glu', 'ssm_w_out']
ATTN_SHARDED = ['attn_w_in', 'attn_w_out']
SHARDED = FIRST_SHARDED + POST_SHARDED + ATTN_SHARDED
COLUMN_SHARDED = ('ssm_w_in', 'attn_w_in')
REPLICATED = ['c_ctx', 'b_mod', 'norm_g', 'ssm_a_re', 'ssm_a_im', 'ssm_log_dt', 'ssm_b_re', 'ssm_b_im', 'ssm_c_re', 'ssm_c_im',
              'ssm_d', 'ssm_b_glu', 'attn_q_norm', 'attn_k_norm', 'final_norm_g']
SSM_NAMES = ['ssm_a_re', 'ssm_a_im', 'ssm_log_dt', 'ssm_b_re', 'ssm_b_im', 'ssm_c_re', 'ssm_c_im', 'ssm_d']


def _full_from_shards(gathered, name, shard_shape):
    rows, cols = shard_shape
    w = gathered.reshape(N_DEV, rows, cols)
    if name in COLUMN_SHARDED:
        return w.transpose(1, 0, 2).reshape(rows, N_DEV * cols)
    return w.reshape(N_DEV * rows, cols)


def _shards_from_full(g, name):
    if name in COLUMN_SHARDED:
        rows, cols = g.shape
        g = g.reshape(rows, N_DEV, cols // N_DEV).transpose(1, 0, 2)
    return g.reshape(N_DEV, -1, 128)


def kernel(x, c, ctx, c_ctx, w_mod, b_mod, norm_g, ssm_w_in, ssm_a_re, ssm_a_im, ssm_log_dt, ssm_b_re, ssm_b_im, ssm_c_re, ssm_c_im, ssm_d, ssm_w_glu, ssm_b_glu, ssm_w_out, attn_w_in, attn_q_norm, attn_k_norm, attn_w_out, final_norm_g, loss_target, m_c_ctx, m_w_mod, m_b_mod, m_norm_g, m_ssm_w_in, m_ssm_a_re, m_ssm_a_im, m_ssm_log_dt, m_ssm_b_re, m_ssm_b_im, m_ssm_c_re, m_ssm_c_im, m_ssm_d, m_ssm_w_glu, m_ssm_b_glu, m_ssm_w_out, m_attn_w_in, m_attn_q_norm, m_attn_k_norm, m_attn_w_out, m_final_norm_g, v_c_ctx, v_w_mod, v_b_mod, v_norm_g, v_ssm_w_in, v_ssm_a_re, v_ssm_a_im, v_ssm_log_dt, v_ssm_b_re, v_ssm_b_im, v_ssm_c_re, v_ssm_c_im, v_ssm_d, v_ssm_w_glu, v_ssm_b_glu, v_ssm_w_out, v_attn_w_in, v_attn_q_norm, v_attn_k_norm, v_attn_w_out, v_final_norm_g):
    env = dict(locals())
    weights = {n: env[n] for n in WEIGHT_NAMES}
    mom_m = {n: env["m_" + n] for n in WEIGHT_NAMES}
    mom_v = {n: env["v_" + n] for n in WEIGHT_NAMES}
    d = D_MODEL
    me = _my_index()
    mod_cols = w_mod.shape[-1]

    c_all = _exchange(c.reshape(8, d // 8), "gather_c", False).reshape(N_DEV, d)
    cond = jnp.concatenate([c_all, c_ctx.reshape(1, d), jnp.zeros((MOD_ROWS - N_DEV - 1, d), F32)], axis=0)
    shard_shapes = {n: weights[n].shape[1:] for n in SHARDED}
    pack_shards = lambda names: _pack([weights[n] for n in names], 1).astype(BF16)

    def unpack_full(gathered, names):
        full, pos = [], 0
        for n in names:
            rows = math.prod(shard_shapes[n]) // 128
            full.append(_full_from_shards(gathered[:, pos:pos + rows], n, shard_shapes[n]))
            pos += rows
        return full

    def exchange_of(names):
        return (pack_shards(names), lambda gathered: unpack_full(gathered, names),
                lambda *grads: jnp.concatenate([_shards_from_full(t, n) for t, n in zip(grads, names)], axis=1).astype(BF16))

    (w_ssm_in,) = unpack_full(_exchange(pack_shards(FIRST_SHARDED), "gather_ssm_w_in", False), FIRST_SHARDED)

    b_cols = lax.dynamic_slice(b_mod, (0, me * mod_cols), (2, mod_cols))
    mod_shard = _mod_fwd(cond, w_mod, b_cols)
    mod_all = _exchange(mod_shard.reshape(2 * MOD_ROWS, mod_cols), "gather_mod", False)
    mod_full = mod_all.reshape(N_DEV, 2, MOD_ROWS, mod_cols).transpose(1, 2, 0, 3).reshape(2, MOD_ROWS, 3 * d)
    lat_rows = lax.dynamic_slice(mod_full, (0, me, 0), (2, 1, 3 * d))
    mods = []
    for i in range(2):
        seg = jnp.stack([mod_full[i, CTX_ROW:CTX_ROW + 1], lat_rows[i]])
        mods.append((seg[:, :, :d], seg[:, :, d:2 * d], seg[:, :, 2 * d:]))

    ssm = tuple(weights[n][0] for n in SSM_NAMES)
    loss, grad_x, g, d_mods = _local_step(
        x[0], ctx[0], loss_target[0], mods, norm_g, ssm, w_ssm_in, None, ssm_b_glu[0], None,
        None, attn_q_norm[0], attn_k_norm[0], None, final_norm_g, exchange_of(ATTN_SHARDED), exchange_of(POST_SHARDED))

    d_rows = jnp.stack([jnp.concatenate(dm, axis=-1) for dm in d_mods])
    d_rows = jnp.concatenate([d_rows.reshape(4, 3 * d), jnp.zeros((4, 3 * d), F32)], axis=0)
    d_all = _exchange(d_rows, "gather_dmod", False)[:, :4].reshape(N_DEV, 2, 2, 3 * d)
    d_all = lax.dynamic_slice(d_all, (0, 0, 0, me * mod_cols), (N_DEV, 2, 2, mod_cols))
    d_w_mod, d_c_ctx = _mod_bwd(cond, d_all[:, :, 1].transpose(1, 0, 2), d_all[:, :, 0:1], w_mod)
    d_b_mod = jnp.stack([jnp.concatenate([t[0] + t[1] for t in dm], axis=-1).reshape(3 * d) for dm in d_mods])

    first_parts = _exchange(_shards_from_full(g['ssm_w_in'], 'ssm_w_in').astype(BF16), "scatter_ssm_w_in_grads", True)
    parts_of = {}
    for names, parts in ((FIRST_SHARDED, first_parts), (POST_SHARDED, g['post_parts']), (ATTN_SHARDED, g['attn_parts'])):
        pos = 0
        for n in names:
            rows = math.prod(shard_shapes[n]) // 128
            parts_of[n] = parts[:, pos:pos + rows].reshape((N_DEV,) + shard_shapes[n])
            pos += rows

    def update(n, g_parts):
        as_2d = lambda t: t.reshape(-1, t.shape[-1])
        res = _adamw(as_2d(weights[n]), g_parts, as_2d(mom_m[n]), as_2d(mom_v[n]), "adamw_" + n)
        return [t.reshape(weights[n].shape) for t in res]

    big = {n: update(n, parts_of[n]) for n in SHARDED}
    big['w_mod'] = update('w_mod', d_w_mod.reshape(1, -1, mod_cols))

    small = dict(zip(SSM_NAMES, g['ssm']))
    small.update(c_ctx=d_c_ctx, b_mod=d_b_mod, norm_g=g['norm_g'], ssm_b_glu=g['ssm_b_glu'], attn_q_norm=g['attn_q_norm'],
                 attn_k_norm=g['attn_k_norm'], final_norm_g=g['final_norm_g'])
    pack_small = lambda t, last: _pack([t[n] for n in REPLICATED] + [last], ADAM_TILE)
    no_weight = jnp.zeros((1,), F32)
    g_small = pack_small(small, loss.reshape(1))
    slices = _exchange(g_small.reshape(N_DEV, -1, 128), "scatter_small_grads", True)
    g_small = _exchange(_sum_parts(slices), "gather_small_grads", False).reshape(1, -1, 128)
    rep = _adamw(pack_small(weights, no_weight), g_small, pack_small(mom_m, no_weight), pack_small(mom_v, no_weight),
                 "adamw_replicated")
    rep = [_unpack(t, [weights[n].shape for n in REPLICATED] + [(1,)]) for t in rep]
    loss = rep[0][-1][0]

    results = []
    for kind in range(4):
        by_name = {n: res[kind] for n, res in big.items()}
        by_name.update(zip(REPLICATED, rep[kind]))
        results.extend(by_name[n] for n in WEIGHT_NAMES)
    return (loss, grad_x[None], *results)
```

```python
import functools
import math

import jax
import jax.numpy as jnp
from jax import lax
from jax.experimental import pallas as pl
from jax.experimental.pallas import tpu as pltpu

F32 = jnp.float32
BF16 = jnp.bfloat16

N_DEV = 8
D_MODEL = 1024
NORM_EPS = 1e-6
SSM_GROUP = 16
SSM_GROUPS = 64
SSM_STATE = 64
GROUPS_PER_BLOCK = 8
N_BLOCKS = SSM_GROUPS // GROUPS_PER_BLOCK
HALF = GROUPS_PER_BLOCK * SSM_STATE
HEAD_DIM = 64
N_Q_HEADS = 16
N_KV_HEADS = 4
KV_REP = N_Q_HEADS // N_KV_HEADS
KV_WIDTH = N_KV_HEADS * HEAD_DIM
GRID_W = 64
ROPE_THETA = 10000.0
ADAM_LR, ADAM_B1, ADAM_B2, ADAM_EPS, ADAM_WD, ADAM_STEP = 0.001, 0.9, 0.999, 1e-08, 0.01, 10

ROW_TILE = 256
SCAN_CHUNK = 256
VMEM_LIMIT = 56 * 1024 * 1024
MESH_IDS = pl.DeviceIdType.MESH


def _cparams(n_axes):
    return pltpu.CompilerParams(dimension_semantics=("arbitrary",) * n_axes, vmem_limit_bytes=VMEM_LIMIT)


def _dot(a, b):
    return jnp.dot(a.astype(BF16), b.astype(BF16), preferred_element_type=F32)


def _dot_t0(a, b):
    return lax.dot_general(a.astype(BF16), b.astype(BF16), (((0,), (0,)), ((), ())), preferred_element_type=F32)


def _dot_t1(a, b):
    return lax.dot_general(a.astype(BF16), b.astype(BF16), (((1,), (1,)), ((), ())), preferred_element_type=F32)


def _s5_prep(a_re, a_im, log_dt, b_re, b_im):
    dt = jnp.exp(log_dt)[:, None]
    ldr, ldi = a_re * dt, a_im * dt
    mag = jnp.exp(ldr)
    abar_re, abar_im = mag * jnp.cos(ldi), mag * jnp.sin(ldi)
    den = a_re * a_re + a_im * a_im
    num_re, num_im = abar_re - 1.0, abar_im
    coef_re = (num_re * a_re + num_im * a_im) / den
    coef_im = (num_im * a_re - num_re * a_im) / den
    bbar_re = coef_re[..., None] * b_re - coef_im[..., None] * b_im
    bbar_im = coef_re[..., None] * b_im + coef_im[..., None] * b_re
    return abar_re, abar_im, bbar_re, bbar_im


def _s5_blocks(abar_re, abar_im, bbar_re, bbar_im, c_re, c_im):
    eye = jnp.eye(GROUPS_PER_BLOCK, dtype=F32)
    bb = jnp.stack([bbar_re, bbar_im]).reshape(2, N_BLOCKS, GROUPS_PER_BLOCK, SSM_STATE, SSM_GROUP)
    b_blk = jnp.einsum('rqgph,gk->qghrkp', bb, eye).reshape(N_BLOCKS, 128, 2 * HALF)
    cc = jnp.stack([c_re, -c_im]).reshape(2, N_BLOCKS, GROUPS_PER_BLOCK, SSM_GROUP, SSM_STATE)
    c_blk = jnp.einsum('rqghp,gk->qrgpkh', cc, eye).reshape(N_BLOCKS, 2 * HALF, 128)
    abar = jnp.stack([abar_re.reshape(N_BLOCKS, HALF), abar_im.reshape(N_BLOCKS, HALF)])
    return abar, b_blk, c_blk


def _s5_unblock(d_b_blk, d_ct_blk):
    db = d_b_blk.reshape(N_BLOCKS, GROUPS_PER_BLOCK, SSM_GROUP, 2, GROUPS_PER_BLOCK, SSM_STATE)
    db = jnp.einsum('qghrgp->rqgph', db).reshape(2, SSM_GROUPS, SSM_STATE, SSM_GROUP)
    dc = d_ct_blk.reshape(N_BLOCKS, GROUPS_PER_BLOCK, SSM_GROUP, 2, GROUPS_PER_BLOCK, SSM_STATE)
    dc = jnp.einsum('qghrgp->rqghp', dc).reshape(2, SSM_GROUPS, SSM_GROUP, SSM_STATE)
    return db[0], db[1], dc[0], -dc[1]


def _scan_chunk_of_step(j, n_chunks, n_ctx_chunks, reverse):
    if not reverse:
        return j
    return jnp.where(j < n_ctx_chunks, n_ctx_chunks - 1 - j, n_chunks - 1 - j + n_ctx_chunks)


LANE_TILES = 2 * HALF // 128
RE_TILES = HALF // 128


def _tiles(v):
    return [v[:, l * 128:(l + 1) * 128] for l in range(v.shape[1] // 128)]


def _scatter_steps(s_ref, q, x):
    for l in range(LANE_TILES):
        s_ref[l, pl.ds(q, x.shape[0], stride=N_BLOCKS), :] = x[:, l * 128:(l + 1) * 128]


def _gather_steps(s_ref, q, n_steps):
    return jnp.concatenate([s_ref[l, pl.ds(q, n_steps, stride=N_BLOCKS), :] for l in range(LANE_TILES)], axis=1)


def _load_step(s_ref, t):
    row = pl.multiple_of(t * N_BLOCKS, N_BLOCKS)
    return [s_ref[l, pl.ds(row, N_BLOCKS), :] for l in range(LANE_TILES)]


def _store_step(s_ref, t, tiles):
    row = pl.multiple_of(t * N_BLOCKS, N_BLOCKS)
    for l in range(LANE_TILES):
        s_ref[l, pl.ds(row, N_BLOCKS), :] = tiles[l]


SCAN_UNROLL = 8
ADJOINT_UNROLL = 15


def _unrolled_loop(n_steps, unroll, step, carry):
    assert n_steps % unroll == 0

    def steps(i, c):
        for r in range(unroll):
            c = step(unroll * i + r, c)
        return c

    return lax.fori_loop(0, n_steps // unroll, steps, carry)


def _cmul_add(a, h, x, conj):
    re, im = [], []
    for l in range(RE_TILES):
        ar, ai, hr, hi = a[l], a[RE_TILES + l], h[l], h[RE_TILES + l]
        if conj:
            re.append(ar * hr + ai * hi + x[l])
            im.append(ar * hi - ai * hr + x[RE_TILES + l])
        else:
            re.append(ar * hr - ai * hi + x[l])
            im.append(ar * hi + ai * hr + x[RE_TILES + l])
    return re + im


def _s5_scan_fwd(u, abar, b_blk, c_blk, d_skip, n_ctx, reverse, rider=None):
    n_rows, width = u.shape
    tc = SCAN_CHUNK
    n_chunks, n_ctx_chunks = n_rows // tc, n_ctx // tc
    with_skip = d_skip is not None

    def body(*refs):
        if rider is not None:
            x_ref, ride_out, sems = refs[4 + with_skip], refs[7 + with_skip], refs[-3:]
            _ride(rider, pl.program_id(0) == 0, pl.program_id(0) == n_chunks - 1, (x_ref, ride_out) + tuple(sems))
            refs = refs[:4 + with_skip] + refs[5 + with_skip:7 + with_skip] + refs[8 + with_skip:-3]
        if with_skip:
            u_ref, a_ref, b_ref, c_ref, d_ref, y_ref, hb_ref, s_ref, h_ref = refs
        else:
            u_ref, a_ref, b_ref, c_ref, y_ref, hb_ref, s_ref, h_ref = refs
        j = pl.program_id(0)

        @pl.when(j == 0)
        def _():
            h_ref[...] = jnp.zeros_like(h_ref)

        hb_ref[0] = h_ref[...]
        for q in range(N_BLOCKS):
            _scatter_steps(s_ref, q, _dot(u_ref[:, q * 128:(q + 1) * 128], b_ref[q]))
        a = _tiles(a_ref[0]) + _tiles(a_ref[1])

        def step(s, h):
            t = tc - 1 - s if reverse else s
            h = _cmul_add(a, h, _load_step(s_ref, t), conj=False)
            _store_step(s_ref, t, h)
            return h

        h = _unrolled_loop(tc, SCAN_UNROLL, step, _tiles(h_ref[...]))
        h_ref[...] = jnp.concatenate(h, axis=1)
        for q in range(N_BLOCKS):
            yq = _dot(_gather_steps(s_ref, q, tc), c_ref[q])
            if with_skip:
                yq = yq + d_ref[:, q * 128:(q + 1) * 128] * u_ref[:, q * 128:(q + 1) * 128]
            y_ref[:, q * 128:(q + 1) * 128] = yq

    chunk = functools.partial(_scan_chunk_of_step, n_chunks=n_chunks, n_ctx_chunks=n_ctx_chunks, reverse=reverse)
    full3 = lambda j: (0, 0, 0)
    in_specs = [pl.BlockSpec((tc, width), lambda j: (chunk(j), 0)),
                pl.BlockSpec((2, N_BLOCKS, HALF), full3),
                pl.BlockSpec((N_BLOCKS, 128, 2 * HALF), full3),
                pl.BlockSpec((N_BLOCKS, 2 * HALF, 128), full3)]
    args = [u, abar, b_blk.astype(BF16), c_blk.astype(BF16)]
    if with_skip:
        in_specs.append(pl.BlockSpec((1, width), lambda j: (0, 0)))
        args.append(d_skip.reshape(1, width))
    out_specs = [pl.BlockSpec((tc, width), lambda j: (chunk(j), 0)),
                 pl.BlockSpec((1, N_BLOCKS, 2 * HALF), lambda j: (chunk(j), 0, 0))]
    out_shape = [_sds((n_rows, width)), _sds((n_chunks, N_BLOCKS, 2 * HALF))]
    scratch = [pltpu.VMEM((LANE_TILES, tc * N_BLOCKS, 128), F32), pltpu.VMEM((N_BLOCKS, 2 * HALF), F32)]
    if rider is not None:
        in_specs.append(HBM_SPEC)
        args.append(rider[0])
        out_specs.append(HBM_SPEC)
        out_shape.append(_exchange_out_shape(*rider))
        scratch += _exchange_semaphores()
    return pl.pallas_call(
        body, name="s5_scan_fwd_rev" if reverse else "s5_scan_fwd",
        grid=(n_chunks,), in_specs=in_specs, out_specs=out_specs, out_shape=out_shape, scratch_shapes=scratch,
        compiler_params=_cparams(1),
    )(*args)


def _s5_scan_bwd(u, dy, hb, abar, b_blk, c_blk, d_skip, n_ctx, reverse, rider=None):
    n_rows, width = u.shape
    tc = SCAN_CHUNK
    n_chunks, n_ctx_chunks = n_rows // tc, n_ctx // tc
    with_skip = d_skip is not None
    n_in, n_out = 7 + with_skip, 4 + with_skip

    def body(*refs):
        if rider is not None:
            x_ref, ride_out, sems = refs[n_in], refs[n_in + 1 + n_out], refs[-3:]
            _ride(rider, pl.program_id(0) == 0, pl.program_id(0) == n_chunks - 1, (x_ref, ride_out) + tuple(sems))
            refs = refs[:n_in] + refs[n_in + 1:n_in + 1 + n_out] + refs[n_in + 2 + n_out:-3]
        if with_skip:
            (u_ref, dy_ref, hb_ref, a_ref, b_ref, bt_ref, ct_ref, d_ref,
             du_ref, da_ref, db_ref, dct_ref, dd_ref, sh_ref, sg_ref, g_ref) = refs
        else:
            (u_ref, dy_ref, hb_ref, a_ref, b_ref, bt_ref, ct_ref,
             du_ref, da_ref, db_ref, dct_ref, sh_ref, sg_ref, g_ref) = refs
        j = pl.program_id(0)

        @pl.when(j == 0)
        def _():
            g_ref[...] = jnp.zeros_like(g_ref)
            da_ref[...] = jnp.zeros_like(da_ref)
            db_ref[...] = jnp.zeros_like(db_ref)
            dct_ref[...] = jnp.zeros_like(dct_ref)
            if with_skip:
                dd_ref[...] = jnp.zeros_like(dd_ref)

        for q in range(N_BLOCKS):
            _scatter_steps(sh_ref, q, _dot(u_ref[:, q * 128:(q + 1) * 128], b_ref[q]))
            _scatter_steps(sg_ref, q, _dot(dy_ref[:, q * 128:(q + 1) * 128], ct_ref[q]))
        a = _tiles(a_ref[0]) + _tiles(a_ref[1])
        time_of = (lambda s: tc - 1 - s) if reverse else (lambda s: s)

        def fwd_step(s, h):
            h = _cmul_add(a, h, _load_step(sh_ref, time_of(s)), conj=False)
            _store_step(sh_ref, time_of(s), h)
            return h

        h0 = _tiles(hb_ref[0])
        _unrolled_loop(tc, SCAN_UNROLL, fwd_step, h0)

        def adj(t, h_prev, carry):
            g, da = carry
            g = _cmul_add(a, g, _load_step(sg_ref, t), conj=True)
            _store_step(sg_ref, t, g)
            da_re = [da[l] + g[l] * h_prev[l] + g[RE_TILES + l] * h_prev[RE_TILES + l] for l in range(RE_TILES)]
            da_im = [da[RE_TILES + l] + g[RE_TILES + l] * h_prev[l] - g[l] * h_prev[RE_TILES + l] for l in range(RE_TILES)]
            return g, da_re + da_im

        def bwd_step(i, carry):
            s = tc - 1 - i
            return adj(time_of(s), _load_step(sh_ref, time_of(s - 1)), carry)

        carry = (_tiles(g_ref[...]), _tiles(da_ref[0]) + _tiles(da_ref[1]))
        carry = _unrolled_loop(tc - 1, ADJOINT_UNROLL, bwd_step, carry)
        g, da = adj(time_of(0), h0, carry)
        g_ref[...] = jnp.concatenate(g, axis=1)
        da_ref[0] = jnp.concatenate(da[:RE_TILES], axis=1)
        da_ref[1] = jnp.concatenate(da[RE_TILES:], axis=1)

        for q in range(N_BLOCKS):
            cols = slice(q * 128, (q + 1) * 128)
            uq, dyq = u_ref[:, cols], dy_ref[:, cols]
            gq = _gather_steps(sg_ref, q, tc)
            duq = _dot(gq, bt_ref[q])
            if with_skip:
                duq = duq + d_ref[:, cols] * dyq
                dd_ref[:, cols] += jnp.sum(dyq * uq, axis=0, keepdims=True)
            du_ref[:, cols] = duq
            db_ref[q] += _dot_t0(uq, gq)
            dct_ref[q] += _dot_t0(dyq, _gather_steps(sh_ref, q, tc))

    def chunk(j):
        return _scan_chunk_of_step(n_chunks - 1 - j, n_chunks, n_ctx_chunks, reverse)

    full2 = lambda j: (0, 0)
    full3 = lambda j: (0, 0, 0)
    row = pl.BlockSpec((tc, width), lambda j: (chunk(j), 0))
    in_specs = [row, row,
                pl.BlockSpec((1, N_BLOCKS, 2 * HALF), lambda j: (chunk(j), 0, 0)),
                pl.BlockSpec((2, N_BLOCKS, HALF), full3),
                pl.BlockSpec((N_BLOCKS, 128, 2 * HALF), full3),
                pl.BlockSpec((N_BLOCKS, 2 * HALF, 128), full3),
                pl.BlockSpec((N_BLOCKS, 128, 2 * HALF), full3)]
    args = [u, dy, hb, abar, b_blk.astype(BF16), jnp.swapaxes(b_blk, 1, 2).astype(BF16),
            jnp.swapaxes(c_blk, 1, 2).astype(BF16)]
    out_specs = [row,
                 pl.BlockSpec((2, N_BLOCKS, HALF), full3),
                 pl.BlockSpec((N_BLOCKS, 128, 2 * HALF), full3),
                 pl.BlockSpec((N_BLOCKS, 128, 2 * HALF), full3)]
    out_shape = [jax.ShapeDtypeStruct((n_rows, width), F32),
                 jax.ShapeDtypeStruct((2, N_BLOCKS, HALF), F32),
                 jax.ShapeDtypeStruct((N_BLOCKS, 128, 2 * HALF), F32),
                 jax.ShapeDtypeStruct((N_BLOCKS, 128, 2 * HALF), F32)]
    if with_skip:
        in_specs.append(pl.BlockSpec((1, width), full2))
        args.append(d_skip.reshape(1, width))
        out_specs.append(pl.BlockSpec((1, width), full2))
        out_shape.append(jax.ShapeDtypeStruct((1, width), F32))
    scratch = [pltpu.VMEM((LANE_TILES, tc * N_BLOCKS, 128), F32), pltpu.VMEM((LANE_TILES, tc * N_BLOCKS, 128), F32),
               pltpu.VMEM((N_BLOCKS, 2 * HALF), F32)]
    if rider is not None:
        in_specs.append(HBM_SPEC)
        args.append(rider[0])
        out_specs.append(HBM_SPEC)
        out_shape.append(_exchange_out_shape(*rider))
        scratch += _exchange_semaphores()
    return pl.pallas_call(
        body, name="s5_scan_bwd_rev" if reverse else "s5_scan_bwd",
        grid=(n_chunks,), in_specs=in_specs, out_specs=out_specs, out_shape=out_shape, scratch_shapes=scratch,
        compiler_params=_cparams(1),
    )(*args)


def _s5_dir_params(d, a_re, a_im, log_dt, b_re, b_im):
    return a_re[d], a_im[d], log_dt[d], b_re[d], b_im[d]


def _s5_forward(u, ssm, n_ctx, riders=(None, None)):
    a_re, a_im, log_dt, b_re, b_im, c_re, c_im, d_skip = ssm
    outs, saved, carried = [], [], [None, None]
    for d in range(2):
        prep = _s5_prep(*_s5_dir_params(d, a_re, a_im, log_dt, b_re, b_im))
        abar, b_blk, c_blk = _s5_blocks(*prep, c_re[d], c_im[d])
        res = _s5_scan_fwd(u, abar, b_blk, c_blk, d_skip if d == 0 else None, n_ctx, reverse=(d == 1), rider=riders[d])
        if riders[d] is not None:
            carried[d] = res[2]
        outs.append(res[0])
        saved.append((res[1], abar, b_blk, c_blk))
    return outs, saved, carried


def _s5_backward(u, dy, ssm, saved, n_ctx, riders=(None, None)):
    a_re, a_im, log_dt, b_re, b_im, c_re, c_im, d_skip = ssm
    dus, grads = [], [[] for _ in range(7)]
    d_d, carried = None, [None, None]
    for d in range(2):
        hb, abar, b_blk, c_blk = saved[d]
        res = _s5_scan_bwd(u, dy, hb, abar, b_blk, c_blk, d_skip if d == 0 else None, n_ctx, reverse=(d == 1),
                           rider=riders[d])
        if riders[d] is not None:
            carried[d], res = res[-1], res[:-1]
        if d == 0:
            du, d_abar, d_b_blk, d_ct_blk, d_d = res
        else:
            du, d_abar, d_b_blk, d_ct_blk = res
        dus.append(du)
        dbb_re, dbb_im, dc_re, dc_im = _s5_unblock(d_b_blk, d_ct_blk)
        _, vjp = jax.vjp(_s5_prep, *_s5_dir_params(d, a_re, a_im, log_dt, b_re, b_im))
        shape = (SSM_GROUPS, SSM_STATE)
        g5 = vjp((d_abar[0].reshape(shape), d_abar[1].reshape(shape), dbb_re, dbb_im))
        for k, g in enumerate(tuple(g5) + (dc_re, dc_im)):
            grads[k].append(g)
    grads = [jnp.stack(g) for g in grads]
    return dus, grads + [d_d.reshape(-1)], carried


INV_SQRT2 = 0.7071067811865476
INV_SQRT_2PI = 0.3989422804014327


def _rows(cols):
    return pl.BlockSpec((ROW_TILE, cols), lambda i: (i, 0))


def _rows_skip_ctx(cols):
    return pl.BlockSpec((ROW_TILE, cols), lambda i: (i + 1, 0))


def _rows_lat(cols):
    return pl.BlockSpec((ROW_TILE, cols), lambda i: (jnp.maximum(i - 1, 0), 0))


def _full(shape):
    nd = len(shape)
    return pl.BlockSpec(shape, lambda i: (0,) * nd)


def _seg(cols):
    return pl.BlockSpec((1, 1, cols), lambda i: (jnp.minimum(i, 1), 0, 0))


def _lat_seg(cols):
    return pl.BlockSpec((1, 1, cols), lambda i: (1, 0, 0))


def _sds(shape, dtype=F32):
    return jax.ShapeDtypeStruct(shape, dtype)


def _sum0(x):
    return jnp.sum(x, axis=0, keepdims=True)


def _sigmoid(x):
    return jax.nn.sigmoid(x)


def _rms_mod(x, g, scale, shift):
    r = lax.rsqrt(jnp.mean(x * x, axis=-1, keepdims=True) + NORM_EPS)
    return (x * r * g) * (1.0 + scale) + shift


def _rms_mod_bwd(x, g, scale, dh):
    r = lax.rsqrt(jnp.mean(x * x, axis=-1, keepdims=True) + NORM_EPS)
    n = x * r
    dyg = dh * (1.0 + scale)
    dn = dyg * g
    dx = r * (dn - n * jnp.mean(dn * n, axis=-1, keepdims=True))
    return dx, _sum0(dyg * n), _sum0(dh * (n * g)), _sum0(dh)


def _head_of_lane(width):
    return (jnp.arange(width)[:, None] // HEAD_DIM == jnp.arange(128)[None, :]).astype(BF16)


def _split_dot(t, w, transposed):
    hi = t.astype(BF16)
    lo = (t - hi.astype(F32)).astype(BF16)
    f = _dot_t1 if transposed else _dot
    return f(hi, w) + f(lo, w)


def _head_sums(t, hl):
    return _split_dot(_split_dot(t, hl, False), hl, True)


def _rope_partner(x):
    n = x.shape[1]
    lane = lax.broadcasted_iota(jnp.int32, x.shape, 1)
    return jnp.where((lane & 16) == 0, pltpu.roll(x, n - 16, 1), pltpu.roll(x, 16, 1))


def _lanes(tab, width):
    return jnp.tile(tab, (1, width // tab.shape[1]))


def _head_norm_rope(x, gain, cos, sin, hl):
    r = lax.rsqrt(_head_sums(x * x, hl) * (1.0 / HEAD_DIM) + NORM_EPS)
    y = x * r * gain
    return y * cos + _rope_partner(y) * sin


def _head_norm_rope_bwd(x, gain, cos, sin, hl, dout):
    dy = dout * cos + _rope_partner(dout * sin)
    r = lax.rsqrt(_head_sums(x * x, hl) * (1.0 / HEAD_DIM) + NORM_EPS)
    n = x * r
    dn = dy * gain
    dx = r * (dn - n * (_head_sums(dn * n, hl) * (1.0 / HEAD_DIM)))
    return dx, _sum0(dy * n)


def _rope_tables(n_ctx, n_lat):
    t = jnp.arange(n_lat)
    pos = jnp.stack([(t // GRID_W).astype(F32), (t % GRID_W).astype(F32)], axis=1)
    n_freq = HEAD_DIM // 4
    freqs = ROPE_THETA ** (-jnp.arange(n_freq, dtype=F32) / n_freq)
    ang = pos[:, :, None] * freqs[None, None, :]
    cos = jnp.repeat(jnp.cos(ang)[:, :, None, :], 2, axis=2).reshape(n_lat, HEAD_DIM)
    sin = jnp.sin(ang)
    sin = jnp.stack([-sin, sin], axis=2).reshape(n_lat, HEAD_DIM)
    cos = jnp.concatenate([jnp.ones((n_ctx, HEAD_DIM), F32), cos], axis=0)
    sin = jnp.concatenate([jnp.zeros((n_ctx, HEAD_DIM), F32), sin], axis=0)
    return jnp.tile(cos, (1, 2)), jnp.tile(sin, (1, 2))


def _token_tile(ctx_ref, x_ref):
    return jnp.where(pl.program_id(0) == 0, ctx_ref[...], x_ref[...])


def _ssm_in(ctx, x, g, scale, shift, w_in):
    n_rows, d = ctx.shape[0] + x.shape[0], x.shape[1]
    e = w_in.shape[1] // 2

    def body(c_ref, x_ref, g_ref, sc_ref, sh_ref, w_ref, u_ref, z_ref):
        h = _rms_mod(_token_tile(c_ref, x_ref), g_ref[...], sc_ref[0], sh_ref[0])
        proj = _dot(h, w_ref[...])
        u_ref[...] = proj[:, :e]
        z_ref[...] = proj[:, e:]

    return pl.pallas_call(
        body, name="ssm_in", grid=(n_rows // ROW_TILE,),
        in_specs=[_full(ctx.shape), _rows_lat(d), _full((1, d)), _seg(d), _seg(d), _full(w_in.shape)],
        out_specs=[_rows(e), _rows(e)], out_shape=[_sds((n_rows, e)), _sds((n_rows, e))],
        compiler_params=_cparams(1),
    )(ctx, x, g, scale, shift, w_in)


def _s5_post_math(y, z, w_glu, b_glu, w_out):
    er = lax.erf(y * INV_SQRT2)
    g = 0.5 * y * (1.0 + er)
    sg = _sigmoid(_dot(g, w_glu) + b_glu)
    g2 = g * sg
    sz = _sigmoid(z)
    silu_z = z * sz
    m = g2 * silu_z
    return er, g, sg, g2, sz, silu_z, m, _dot(m, w_out)


def _ssm_post(ctx, x, y0, y1, z, gate, w_glu, b_glu, w_out):
    n_rows, d = ctx.shape[0] + x.shape[0], x.shape[1]
    e = z.shape[1]

    def body(c_ref, x_ref, y0_ref, y1_ref, z_ref, gt_ref, wg_ref, bg_ref, wo_ref, o_ref):
        out = _s5_post_math(y0_ref[...] + y1_ref[...], z_ref[...], wg_ref[...], bg_ref[...], wo_ref[...])[-1]
        o_ref[...] = _token_tile(c_ref, x_ref) + gt_ref[0] * out

    return pl.pallas_call(
        body, name="ssm_post", grid=(n_rows // ROW_TILE,),
        in_specs=[_full(ctx.shape), _rows_lat(d), _rows(e), _rows(e), _rows(e), _seg(d), _full(w_glu.shape), _full((1, e)),
                  _full(w_out.shape)],
        out_specs=_rows(d), out_shape=_sds((n_rows, d)),
        compiler_params=_cparams(1),
    )(ctx, x, y0, y1, z, gate, w_glu, b_glu, w_out)


def _init_acc(first, *refs):
    @pl.when(first)
    def _():
        for r in refs:
            r[...] = jnp.zeros_like(r)


def _ssm_post_bwd(dxa, y0, y1, z, gate, w_glu, b_glu, w_out):
    n_rows, d = dxa.shape
    e = z.shape[1]

    def body(dx_ref, y0_ref, y1_ref, z_ref, gt_ref, wg_ref, bg_ref, wo_ref,
             dy_ref, dz_ref, dgt_ref, dwo_ref, dwg_ref, dbg_ref):
        i = pl.program_id(0)
        _init_acc(i == 0, dwo_ref, dwg_ref, dbg_ref)
        _init_acc(i <= 1, dgt_ref)
        y, zz = y0_ref[...] + y1_ref[...], z_ref[...]
        er, g, sg, g2, sz, silu_z, m, out = _s5_post_math(y, zz, wg_ref[...], bg_ref[...], wo_ref[...])
        dxa_t = dx_ref[...]
        dgt_ref[0] += _sum0(dxa_t * out)
        dout = gt_ref[0] * dxa_t
        dm = _dot_t1(dout, wo_ref[...])
        dwo_ref[...] += _dot_t0(m, dout)
        dg2 = dm * silu_z
        dz_ref[...] = dm * g2 * (sz * (1.0 + zz * (1.0 - sz)))
        dt = dg2 * g * sg * (1.0 - sg)
        dwg_ref[...] += _dot_t0(g, dt)
        dbg_ref[...] += _sum0(dt)
        dg = dg2 * sg + _dot_t1(dt, wg_ref[...])
        dy_ref[...] = dg * (0.5 * (1.0 + er) + y * jnp.exp(-0.5 * y * y) * INV_SQRT_2PI)

    return pl.pallas_call(
        body, name="ssm_post_bwd", grid=(n_rows // ROW_TILE,),
        in_specs=[_rows(d), _rows(e), _rows(e), _rows(e), _seg(d), _full(w_glu.shape), _full((1, e)), _full(w_out.shape)],
        out_specs=[_rows(e), _rows(e), _seg(d), _full(w_out.shape), _full(w_glu.shape), _full((1, e))],
        out_shape=[_sds((n_rows, e)), _sds((n_rows, e)), _sds((2, 1, d)), _sds(w_out.shape), _sds(w_glu.shape), _sds((1, e))],
        compiler_params=_cparams(1),
    )(dxa, y0, y1, z, gate, w_glu, b_glu, w_out)


def _ssm_in_bwd(du0, du1, dz, ctx, x_lat, dxa_next, g, scale, shift, w_in):
    n_lat, d = x_lat.shape
    n_rows = ctx.shape[0] + n_lat
    e = dz.shape[1]

    def body(du0_ref, du1_ref, dz_ref, c_ref, x_ref, dn_ref, g_ref, sc_ref, sh_ref, w_ref,
             gx_ref, dw_ref, dg_ref, dsc_ref, dsh_ref):
        i = pl.program_id(0)
        _init_acc(i == 0, dw_ref, dg_ref)
        _init_acc(i <= 1, dsc_ref, dsh_ref)
        x = _token_tile(c_ref, x_ref)
        h = _rms_mod(x, g_ref[...], sc_ref[0], sh_ref[0])
        dproj = jnp.concatenate([du0_ref[...] + du1_ref[...], dz_ref[...]], axis=1)
        dh = _dot_t1(dproj, w_ref[...])
        dw_ref[...] += _dot_t0(h, dproj)
        dx, dg, dsc, dsh = _rms_mod_bwd(x, g_ref[...], sc_ref[0], dh)
        dg_ref[...] += dg
        dsc_ref[0] += dsc
        dsh_ref[0] += dsh
        gx_ref[...] = dn_ref[...] + dx

    return pl.pallas_call(
        body, name="ssm_in_bwd", grid=(n_rows // ROW_TILE,),
        in_specs=[_rows(e), _rows(e), _rows(e), _full(ctx.shape), _rows_lat(d), _rows(d), _full((1, d)), _seg(d), _seg(d),
                  _full(w_in.shape)],
        out_specs=[_rows_lat(d), _full((d, 2 * e)), _full((1, d)), _seg(d), _seg(d)],
        out_shape=[_sds((n_lat, d)), _sds((d, 2 * e)), _sds((1, d)), _sds((2, 1, d)), _sds((2, 1, d))],
        compiler_params=_cparams(1),
    )(du0, du1, dz, ctx, x_lat, dxa_next, g, scale, shift, w_in)


Q_WIDTH = N_Q_HEADS * HEAD_DIM
SM_SCALE = 1.0 / math.sqrt(HEAD_DIM)


def _attn_in(xa, g, scale, shift, w_in, q_gain, k_gain, cos, sin):
    n_rows, d = xa.shape
    qk = Q_WIDTH + KV_WIDTH

    def body(x_ref, g_ref, sc_ref, sh_ref, w_ref, qg_ref, kg_ref, cos_ref, sin_ref, hq_ref, hk_ref,
             q_ref, k_ref, v_ref, z_ref, raw_ref):
        h = _rms_mod(x_ref[...], g_ref[...], sc_ref[0], sh_ref[0])
        proj = _dot(h, w_ref[...])
        q_raw, k_raw = proj[:, :Q_WIDTH], proj[:, Q_WIDTH:qk]
        cos, sin = cos_ref[...], sin_ref[...]
        q = _head_norm_rope(q_raw, qg_ref[...], _lanes(cos, Q_WIDTH), _lanes(sin, Q_WIDTH), hq_ref[...])
        k = _head_norm_rope(k_raw, kg_ref[...], _lanes(cos, KV_WIDTH), _lanes(sin, KV_WIDTH), hk_ref[...])
        q_ref[...] = (q * SM_SCALE).astype(BF16)
        k_ref[...] = k.astype(BF16)
        v_ref[...] = proj[:, qk:qk + KV_WIDTH].astype(BF16)
        z_ref[...] = proj[:, qk + KV_WIDTH:]
        raw_ref[...] = proj[:, :qk]

    return pl.pallas_call(
        body, name="attn_in", grid=(n_rows // ROW_TILE,),
        in_specs=[_rows(d), _full((1, d)), _seg(d), _seg(d), _full(w_in.shape), _full((1, Q_WIDTH)), _full((1, KV_WIDTH)),
                  _rows(128), _rows(128), _full((Q_WIDTH, 128)), _full((KV_WIDTH, 128))],
        out_specs=[_rows_lat(Q_WIDTH), _rows(KV_WIDTH), _rows(KV_WIDTH), _rows(Q_WIDTH), _rows(qk)],
        out_shape=[_sds((n_rows - ROW_TILE, Q_WIDTH), BF16), _sds((n_rows, KV_WIDTH), BF16), _sds((n_rows, KV_WIDTH), BF16),
                   _sds((n_rows, Q_WIDTH)), _sds((n_rows, qk))],
        compiler_params=_cparams(1),
    )(xa, g, scale, shift, w_in, q_gain, k_gain, cos, sin, _head_of_lane(Q_WIDTH), _head_of_lane(KV_WIDTH))


GROUP_WIDTH = KV_REP * HEAD_DIM


def _stack_heads(ref):
    return jnp.concatenate([ref[:, h * HEAD_DIM:(h + 1) * HEAD_DIM] for h in range(KV_REP)], axis=0)


def _unstack_heads(a_t, tq):
    return jnp.concatenate([a_t[:, h * tq:(h + 1) * tq].T for h in range(KV_REP)], axis=1)


def _kv_tile(n_keys):
    return 768 if n_keys % 768 == 0 else 256


def _kv_tile_fwd(n_keys):
    return 1408 if n_keys % 1408 == 0 else _kv_tile(n_keys)


def _q_tile(n_lat):
    return 512 if n_lat % 512 == 0 else 256


def _flash_fwd(q, k, v_t):
    n_lat = q.shape[0]
    tq = _q_tile(n_lat)
    rows = KV_REP * tq
    n_kv, tk, n_q = k.shape[1], k.shape[2], n_lat // tq

    v_rows = v_t.shape[2]

    def body(q_ref, k_ref, vt_ref, o_ref, lse_ref):
        q = _stack_heads(q_ref)

        def step(j, carry):
            m_prev, acc = carry
            s_t = _dot_t1(k_ref[0, j], q)
            m_new = jnp.maximum(m_prev, jnp.max(s_t, axis=0, keepdims=True))
            alpha = jnp.exp(m_prev - m_new)
            p_t = jnp.exp(s_t - m_new)
            return m_new, alpha * acc + _dot(vt_ref[0, j], p_t)

        init = (jnp.full((1, rows), -jnp.inf, F32), jnp.zeros((v_rows, rows), F32))
        m, acc = lax.fori_loop(0, n_kv, step, init)
        l = acc[HEAD_DIM:HEAD_DIM + 1]
        o_ref[...] = _unstack_heads(acc[:HEAD_DIM] / l, tq)
        lse_ref[0, 0] = m + jnp.log(l)

    kv_all = lambda a: pl.BlockSpec((1,) + a.shape[1:], lambda g, i: (g, 0, 0, 0))
    return pl.pallas_call(
        body, name="flash_fwd", grid=(N_KV_HEADS, n_q),
        in_specs=[pl.BlockSpec((tq, GROUP_WIDTH), lambda g, i: (i, g)), kv_all(k), kv_all(v_t)],
        out_specs=[pl.BlockSpec((tq, GROUP_WIDTH), lambda g, i: (i, g)),
                   pl.BlockSpec((1, 1, 1, rows), lambda g, i: (g, i, 0, 0))],
        out_shape=[_sds((n_lat, Q_WIDTH)), _sds((N_KV_HEADS, n_q, 1, rows))],
        compiler_params=_cparams(2),
    )(q, k, v_t)


def _flash_bwd(q, k, k_t, v, do, lse_t, delta_t):
    n_lat = q.shape[0]
    tq = _q_tile(n_lat)
    rows = KV_REP * tq
    n_kv, tk, n_q = k.shape[1], k.shape[2], n_lat // tq

    def body(q_ref, k_ref, kt_ref, v_ref, do_ref, lse_ref, dl_ref, dq_ref, dk_ref, dv_ref):
        _init_acc(pl.program_id(1) == 0, dk_ref, dv_ref)
        q, do = _stack_heads(q_ref), _stack_heads(do_ref)
        lse, delta = lse_ref[0, 0], dl_ref[0, 0]

        def step(j, dq_acc):
            p_t = jnp.exp(_dot_t1(k_ref[0, j], q) - lse)
            dv_ref[0, j] += _dot(p_t, do)
            ds_t = p_t * (_dot_t1(v_ref[0, j], do) - delta)
            dk_ref[0, j] += _dot(ds_t, q)
            return dq_acc + _dot(kt_ref[0, j], ds_t)

        dq = lax.fori_loop(0, n_kv, step, jnp.zeros((HEAD_DIM, rows), F32))
        dq_ref[...] = _unstack_heads(dq, tq)

    qspec = pl.BlockSpec((tq, GROUP_WIDTH), lambda g, i: (i, g))
    rowspec = pl.BlockSpec((1, 1, 1, rows), lambda g, i: (g, i, 0, 0))
    kv_all = lambda a: pl.BlockSpec((1,) + a.shape[1:], lambda g, i: (g, 0, 0, 0))
    return pl.pallas_call(
        body, name="flash_bwd", grid=(N_KV_HEADS, n_q),
        in_specs=[qspec, kv_all(k), kv_all(k_t), kv_all(v), qspec, rowspec, rowspec],
        out_specs=[qspec, kv_all(k), kv_all(k)],
        out_shape=[_sds((n_lat, Q_WIDTH)), _sds(k.shape), _sds(k.shape)],
        compiler_params=_cparams(2),
    )(q, k, k_t, v, do, lse_t, delta_t)


def _to_lane_stacked(a, tq):
    n_lat = a.shape[0]
    a = a.reshape(n_lat // tq, tq, N_KV_HEADS, KV_REP).transpose(2, 0, 3, 1)
    return a.reshape(N_KV_HEADS, n_lat // tq, 1, KV_REP * tq)


def _attn_post_loss(o, z, xa, gate, w_out, final_g, target):
    n_lat, d = target.shape
    e = o.shape[1]
    head_of_lane = (jnp.arange(e)[:, None] // HEAD_DIM == jnp.arange(128)[None, :]).astype(BF16)

    def body(o_ref, z_ref, x_ref, gt_ref, w_ref, fg_ref, tg_ref, hl_ref,
             do_ref, dl_ref, dz_ref, dx_ref, loss_ref, dfg_ref, dgt_ref, dw_ref):
        _init_acc(pl.program_id(0) == 0, loss_ref, dfg_ref, dgt_ref, dw_ref)
        oo, zz, gate_t, fg = o_ref[...], z_ref[...], gt_ref[0], fg_ref[...]
        sz = _sigmoid(zz)
        silu_z = zz * sz
        m = oo * silu_z
        out = _dot(m, w_ref[...])
        x2 = x_ref[...] + gate_t * out
        r = lax.rsqrt(jnp.mean(x2 * x2, axis=-1, keepdims=True) + NORM_EPS)
        n = x2 * r
        err = n * fg - tg_ref[...]
        loss_ref[...] += 0.5 * jnp.sum(jnp.mean(err * err, axis=-1, keepdims=True), axis=0, keepdims=True)
        dy = err * (1.0 / d)
        dfg_ref[...] += _sum0(dy * n)
        dn = dy * fg
        dx2 = r * (dn - n * jnp.mean(dn * n, axis=-1, keepdims=True))
        dx_ref[...] = dx2
        dgt_ref[...] += _sum0(dx2 * out)
        dout = gate_t * dx2
        dw_ref[...] += _dot_t0(m, dout)
        dm = _dot_t1(dout, w_ref[...])
        do = dm * silu_z
        do_ref[...] = do.astype(BF16)
        prod = do * oo
        hi = prod.astype(BF16)
        lo = (prod - hi.astype(F32)).astype(BF16)
        dl_ref[...] = _dot(hi, hl_ref[...]) + _dot(lo, hl_ref[...])
        dz_ref[...] = dm * oo * (sz * (1.0 + zz * (1.0 - sz)))

    return pl.pallas_call(
        body, name="attn_post_loss", grid=(n_lat // ROW_TILE,),
        in_specs=[_rows(e), _rows_skip_ctx(e), _rows_skip_ctx(d), _lat_seg(d), _full(w_out.shape),
                  _full((1, d)), _rows(d), _full((e, 128))],
        out_specs=[_rows(e), _rows(128), _rows(e), _rows(d), _full((1, 1)), _full((1, d)), _full((1, d)), _full(w_out.shape)],
        out_shape=[_sds((n_lat, e), BF16), _sds((n_lat, 128)), _sds((n_lat, e)), _sds((n_lat, d)), _sds((1, 1)), _sds((1, d)),
                   _sds((1, d)), _sds(w_out.shape)],
        compiler_params=_cparams(1),
    )(o, z, xa, gate, w_out, final_g, target, head_of_lane)


def _attn_in_bwd(dq, dk, dv, dz, raw, xa, dx2, g, scale, shift, q_gain, k_gain, cos, sin, w_in):
    n_rows, d = xa.shape
    qk = Q_WIDTH + KV_WIDTH
    n_in = w_in.shape[1]

    def body(dq_ref, dk_ref, dv_ref, dz_ref, raw_ref, x_ref, dx2_ref, g_ref, sc_ref, sh_ref, qg_ref, kg_ref, cos_ref, sin_ref,
             w_ref, hq_ref, hk_ref, dxa_ref, dw_ref, dqg_ref, dkg_ref, dg_ref, dsc_ref, dsh_ref):
        i = pl.program_id(0)
        _init_acc(i == 0, dw_ref, dqg_ref, dkg_ref, dg_ref)
        _init_acc(i <= 1, dsc_ref, dsh_ref)
        is_lat = (i > 0).astype(F32)
        x = x_ref[...]
        h = _rms_mod(x, g_ref[...], sc_ref[0], sh_ref[0])
        cos, sin = cos_ref[...], sin_ref[...]
        raw_t = raw_ref[...]
        dq_raw, dqg = _head_norm_rope_bwd(raw_t[:, :Q_WIDTH], qg_ref[...], _lanes(cos, Q_WIDTH), _lanes(sin, Q_WIDTH),
                                          hq_ref[...], dq_ref[...] * (SM_SCALE * is_lat))
        dk_raw, dkg = _head_norm_rope_bwd(raw_t[:, Q_WIDTH:], kg_ref[...], _lanes(cos, KV_WIDTH), _lanes(sin, KV_WIDTH),
                                          hk_ref[...], dk_ref[...])
        dqg_ref[...] += dqg
        dkg_ref[...] += dkg
        dproj = jnp.concatenate([dq_raw, dk_raw, dv_ref[...], dz_ref[...] * is_lat], axis=1)
        dh = _dot_t1(dproj, w_ref[...])
        dw_ref[...] += _dot_t0(h, dproj)
        dx, dg, dsc, dsh = _rms_mod_bwd(x, g_ref[...], sc_ref[0], dh)
        dg_ref[...] += dg
        dsc_ref[0] += dsc
        dsh_ref[0] += dsh
        dxa_ref[...] = dx + dx2_ref[...] * is_lat

    return pl.pallas_call(
        body, name="attn_in_bwd", grid=(n_rows // ROW_TILE,),
        in_specs=[_rows_lat(Q_WIDTH), _rows(KV_WIDTH), _rows(KV_WIDTH), _rows_lat(Q_WIDTH), _rows(qk), _rows(d), _rows_lat(d),
                  _full((1, d)), _seg(d), _seg(d), _full((1, Q_WIDTH)), _full((1, KV_WIDTH)), _rows(128), _rows(128),
                  _full(w_in.shape), _full((Q_WIDTH, 128)), _full((KV_WIDTH, 128))],
        out_specs=[_rows(d), _full((d, n_in)), _full((1, Q_WIDTH)), _full((1, KV_WIDTH)), _full((1, d)), _seg(d), _seg(d)],
        out_shape=[_sds((n_rows, d)), _sds((d, n_in)), _sds((1, Q_WIDTH)), _sds((1, KV_WIDTH)), _sds((1, d)),
                   _sds((2, 1, d)), _sds((2, 1, d))],
        compiler_params=_cparams(1),
    )(dq, dk, dv, dz, raw, xa, dx2, g, scale, shift, q_gain, k_gain, cos, sin, w_in,
      _head_of_lane(Q_WIDTH), _head_of_lane(KV_WIDTH))


def _heads_major(a, n_heads):
    return a.reshape(a.shape[0], n_heads, HEAD_DIM).transpose(1, 0, 2)


def _tokens_major(a):
    return a.transpose(1, 0, 2).reshape(a.shape[1], a.shape[0] * HEAD_DIM)


def _local_step(x, ctx, target, mods, norm_g, ssm, w_ssm_in, w_glu, b_glu, w_ssm_out, w_attn_in, q_norm, k_norm, w_attn_out,
                final_g, attn_exchange=None, post_exchange=None):
    n_ctx, d = ctx.shape
    assert n_ctx == ROW_TILE
    n_lat = x.shape[0]
    (shift0, scale0, gate0), (shift1, scale1, gate1) = mods
    g0, g1, fg = norm_g[0:1], norm_g[1:2], final_g.reshape(1, d)
    b_glu = b_glu.reshape(1, -1)
    q_gain = jnp.tile(q_norm.reshape(1, HEAD_DIM), (1, N_Q_HEADS))
    k_gain = jnp.tile(k_norm.reshape(1, HEAD_DIM), (1, N_KV_HEADS))
    cos, sin = _rope_tables(n_ctx, n_lat)

    u, z0 = _ssm_in(ctx, x, g0, scale0, shift0, w_ssm_in)
    gather = lambda ex: (ex[0], False) if ex else None
    (y0, y1), saved, gathered = _s5_forward(u, ssm, n_ctx, (gather(attn_exchange), gather(post_exchange)))
    if attn_exchange:
        w_attn_in, w_attn_out = attn_exchange[1](gathered[0])
    if post_exchange:
        w_glu, w_ssm_out = post_exchange[1](gathered[1])
    xa1 = _ssm_post(ctx, x, y0, y1, z0, gate0, w_glu, b_glu, w_ssm_out)

    q, k, v, z1, raw = _attn_in(xa1, g1, scale1, shift1, w_attn_in, q_gain, k_gain, cos, sin)
    tq, tk, tk_fwd = _q_tile(n_lat), _kv_tile(n_ctx + n_lat), _kv_tile_fwd(n_ctx + n_lat)
    k_h, v_h = _heads_major(k, N_KV_HEADS), _heads_major(v, N_KV_HEADS)
    k_b, v_b = k_h.reshape(N_KV_HEADS, -1, tk, HEAD_DIM), v_h.reshape(N_KV_HEADS, -1, tk, HEAD_DIM)
    v_t = v_h.reshape(N_KV_HEADS, -1, tk_fwd, HEAD_DIM).transpose(0, 1, 3, 2)
    v_t_ones = jnp.concatenate([v_t, jnp.ones(v_t.shape[:2] + (16, tk_fwd), BF16)], axis=2)
    o, lse_t = _flash_fwd(q, k_h.reshape(N_KV_HEADS, -1, tk_fwd, HEAD_DIM), v_t_ones)
    do, delta, dz1, dx2, loss, d_fg, d_gate1, d_w_attn_out = _attn_post_loss(
        o, z1, xa1, gate1, w_attn_out, fg, target)

    dq, dk_b, dv_b = _flash_bwd(q, k_b, k_b.transpose(0, 1, 3, 2), v_b, do, lse_t, _to_lane_stacked(delta[:, :N_Q_HEADS], tq))
    keys_major = lambda a: _tokens_major(a.reshape(N_KV_HEADS, -1, HEAD_DIM))
    dxa1, d_w_attn_in, d_qg, d_kg, d_g1, d_scale1, d_shift1 = _attn_in_bwd(
        dq, keys_major(dk_b), keys_major(dv_b), dz1, raw, xa1, dx2, g1, scale1, shift1,
        q_gain, k_gain, cos, sin, w_attn_in)
    dy, dz0, d_gate0, d_w_ssm_out, d_w_glu, d_b_glu = _ssm_post_bwd(
        dxa1, y0, y1, z0, gate0, w_glu, b_glu, w_ssm_out)
    scatter = lambda ex, *g: (ex[2](*g), True) if ex else None
    (du0, du1), d_ssm, parts = _s5_backward(
        u, dy, ssm, saved, n_ctx,
        (scatter(attn_exchange, d_w_attn_in, d_w_attn_out), scatter(post_exchange, d_w_glu, d_w_ssm_out)))
    grad_x, d_w_ssm_in, d_g0, d_scale0, d_shift0 = _ssm_in_bwd(du0, du1, dz0, ctx, x, dxa1, g0, scale0, shift0, w_ssm_in)

    d_gate1_seg = jnp.concatenate([jnp.zeros((1, 1, d), F32), d_gate1.reshape(1, 1, d)], axis=0)
    grads = dict(
        norm_g=jnp.concatenate([d_g0, d_g1], axis=0), ssm_w_in=d_w_ssm_in, ssm=d_ssm, ssm_b_glu=d_b_glu.reshape(-1),
        attn_q_norm=d_qg.reshape(N_Q_HEADS, HEAD_DIM).sum(0), attn_k_norm=d_kg.reshape(N_KV_HEADS, HEAD_DIM).sum(0),
        final_norm_g=d_fg.reshape(-1))
    if attn_exchange:
        grads.update(attn_parts=parts[0])
    else:
        grads.update(attn_w_in=d_w_attn_in, attn_w_out=d_w_attn_out)
    if post_exchange:
        grads.update(post_parts=parts[1])
    else:
        grads.update(ssm_w_glu=d_w_glu, ssm_w_out=d_w_ssm_out)
    d_mods = ((d_shift0, d_scale0, d_gate0), (d_shift1, d_scale1, d_gate1_seg))
    return loss[0, 0], grad_x, grads, d_mods


def _my_index():
    return 4 * lax.axis_index("x") + 2 * lax.axis_index("y") + lax.axis_index("c")


def _peer(k):
    mx, my, mc = lax.axis_index("x"), lax.axis_index("y"), lax.axis_index("c")
    px = 1 - mx if k & 4 else mx
    py = 1 - my if k & 2 else my
    pc = 1 - mc if k & 1 else mc
    return (px, py, pc), 4 * px + 2 * py + pc


HBM_SPEC = pl.BlockSpec(memory_space=pl.ANY)


def _exchange(x, name, all_to_all):
    def body(x_ref, out_ref, send_sems, recv_sems, local_sem):
        _exchange_copies(all_to_all, x_ref, out_ref, send_sems, recv_sems, local_sem, start=True)
        _exchange_copies(all_to_all, x_ref, out_ref, send_sems, recv_sems, local_sem, start=False)

    return pl.pallas_call(
        body, name=name, in_specs=[HBM_SPEC], out_specs=HBM_SPEC,
        out_shape=_exchange_out_shape(x, all_to_all), scratch_shapes=_exchange_semaphores(),
    )(x)


def _exchange_pair(xa, xa_all_to_all, xb, xb_all_to_all, name):
    def body(xa_ref, xb_ref, oa_ref, ob_ref, sa, ra, la, sb, rb, lb):
        for start in (True, False):
            _exchange_copies(xa_all_to_all, xa_ref, oa_ref, sa, ra, la, start=start)
            _exchange_copies(xb_all_to_all, xb_ref, ob_ref, sb, rb, lb, start=start)

    return pl.pallas_call(
        body, name=name, in_specs=[HBM_SPEC, HBM_SPEC], out_specs=[HBM_SPEC, HBM_SPEC],
        out_shape=[_exchange_out_shape(xa, xa_all_to_all), _exchange_out_shape(xb, xb_all_to_all)],
        scratch_shapes=_exchange_semaphores() + _exchange_semaphores(),
    )(xa, xb)


def _exchange_out_shape(x, all_to_all):
    return _sds((N_DEV,) + tuple(x.shape[1:] if all_to_all else x.shape), x.dtype)


def _exchange_semaphores():
    return [pltpu.SemaphoreType.DMA((N_DEV - 1,)), pltpu.SemaphoreType.DMA((N_DEV - 1,)), pltpu.SemaphoreType.DMA]


def _exchange_copies(all_to_all, x_ref, out_ref, send_sems, recv_sems, local_sem, start):
    me = _my_index()
    mine = pltpu.make_async_copy(x_ref.at[me] if all_to_all else x_ref, out_ref.at[me], local_sem)
    if start:
        mine.start()
    for k in range(1, N_DEV):
        peer, peer_idx = _peer(k)
        send = pltpu.make_async_remote_copy(
            src_ref=x_ref.at[peer_idx] if all_to_all else x_ref, dst_ref=out_ref.at[me],
            send_sem=send_sems.at[k - 1], recv_sem=recv_sems.at[k - 1], device_id=peer, device_id_type=MESH_IDS)
        if start:
            send.start()
        else:
            pltpu.make_async_remote_copy(
                src_ref=x_ref.at[me] if all_to_all else x_ref, dst_ref=out_ref.at[peer_idx],
                send_sem=send_sems.at[k - 1], recv_sem=recv_sems.at[k - 1], device_id=peer,
                device_id_type=MESH_IDS).wait_recv()
            send.wait_send()
    if not start:
        mine.wait()


def _ride(rider, first, last, refs):
    @pl.when(first)
    def _():
        _exchange_copies(rider[1], *refs, start=True)

    @pl.when(last)
    def _():
        _exchange_copies(rider[1], *refs, start=False)


MOD_ROWS = 16
CTX_ROW = N_DEV


def _mod_fwd(cond, w_shard, b_cols):
    n_layers, d, cols = w_shard.shape

    def body(c_ref, w_ref, b_ref, o_ref):
        c = c_ref[...]
        s = c * _sigmoid(c)
        for i in range(n_layers):
            o_ref[i] = _dot(s, w_ref[i]) + b_ref[i]

    return pl.pallas_call(
        body, name="mod_fwd", out_shape=_sds((n_layers, MOD_ROWS, cols)),
        compiler_params=pltpu.CompilerParams(vmem_limit_bytes=VMEM_LIMIT),
    )(cond, w_shard, b_cols.reshape(n_layers, 1, cols))


def _mod_bwd(cond, d_lat_cols, d_ctx_cols, w_shard):
    n_layers, d, cols = w_shard.shape

    def body(c_ref, dl_ref, dc_ref, w_ref, dw_ref, dcc_ref):
        c = c_ref[...]
        sg = _sigmoid(c)
        s = c * sg
        d_s = jnp.zeros((MOD_ROWS, d), F32)
        for i in range(n_layers):
            d_ctx = dc_ref[0, i]
            for j in range(1, N_DEV):
                d_ctx = d_ctx + dc_ref[j, i]
            dm = jnp.concatenate([dl_ref[i], d_ctx, jnp.zeros((MOD_ROWS - N_DEV - 1, cols), F32)], axis=0)
            dw_ref[i] = _dot_t0(s, dm)
            d_s = d_s + _dot_t1(dm, w_ref[i])
        d_c = d_s * (sg * (1.0 + c * (1.0 - sg)))
        dcc_ref[...] = d_c[CTX_ROW:CTX_ROW + 1]

    return pl.pallas_call(
        body, name="mod_bwd", out_shape=[_sds((n_layers, d, cols)), _sds((1, d))],
        compiler_params=pltpu.CompilerParams(vmem_limit_bytes=VMEM_LIMIT),
    )(cond, d_lat_cols, d_ctx_cols, w_shard)


ADAM_TILE = 512


def _adamw(w, g_parts, m, v, name):
    n_parts, n_rows, lanes = g_parts.shape
    tile = min(ADAM_TILE, n_rows)
    assert n_rows % tile == 0
    c1 = 1.0 - ADAM_B1 ** ADAM_STEP
    c2 = 1.0 - ADAM_B2 ** ADAM_STEP

    def body(w_ref, g_ref, m_ref, v_ref, go_ref, d_ref, mo_ref, vo_ref):
        g = g_ref[0].astype(F32)
        for p in range(1, n_parts):
            g = g + g_ref[p].astype(F32)
        m_new = ADAM_B1 * m_ref[...] + (1.0 - ADAM_B1) * g
        v_new = ADAM_B2 * v_ref[...] + (1.0 - ADAM_B2) * (g * g)
        go_ref[...] = g
        mo_ref[...] = m_new
        vo_ref[...] = v_new
        d_ref[...] = -ADAM_LR * ((m_new / c1) / (jnp.sqrt(v_new / c2) + ADAM_EPS) + ADAM_WD * w_ref[...])

    row = pl.BlockSpec((tile, lanes), lambda i: (i, 0))
    return pl.pallas_call(
        body, name=name, grid=(n_rows // tile,),
        in_specs=[row, pl.BlockSpec((n_parts, tile, lanes), lambda i: (0, i, 0)), row, row],
        out_specs=[row] * 4, out_shape=[_sds((n_rows, lanes))] * 4,
        compiler_params=_cparams(1),
    )(w, g_parts, m, v)


def _sum_parts(parts):
    n_parts, n_rows, lanes = parts.shape

    def body(p_ref, o_ref):
        acc = p_ref[0]
        for p in range(1, n_parts):
            acc = acc + p_ref[p]
        o_ref[...] = acc

    return pl.pallas_call(body, name="sum_parts", out_shape=_sds((n_rows, lanes)))(parts)


def _pack(arrays, row_multiple):
    parts = []
    for a in arrays:
        flat = a.reshape(-1)
        parts.append(jnp.pad(flat, (0, (-flat.shape[0]) % 1024)))
    flat = jnp.concatenate(parts)
    flat = jnp.pad(flat, (0, (-flat.shape[0]) % (row_multiple * 128)))
    return flat.reshape(-1, 128)


def _unpack(packed, shapes):
    flat = packed.reshape(-1)
    out, pos = [], 0
    for s in shapes:
        n = math.prod(s)
        out.append(flat[pos:pos + n].reshape(s))
        pos += n + (-n) % 1024
    return out


WEIGHT_NAMES = ['c_ctx', 'w_mod', 'b_mod', 'norm_g', 'ssm_w_in', 'ssm_a_re', 'ssm_a_im', 'ssm_log_dt', 'ssm_b_re', 'ssm_b_im',
                'ssm_c_re', 'ssm_c_im', 'ssm_d', 'ssm_w_glu', 'ssm_b_glu', 'ssm_w_out', 'attn_w_in', 'attn_q_norm',
                'attn_k_norm', 'attn_w_out', 'final_norm_g']
FIRST_SHARDED = ['ssm_w_in']
POST_SHARDED = ['ssm_w_glu', 'ssm_w_out']
ATTN_SHARDED = ['attn_w_in', 'attn_w_out']
SHARDED = FIRST_SHARDED + POST_SHARDED + ATTN_SHARDED
COLUMN_SHARDED = ('ssm_w_in', 'attn_w_in')
REPLICATED = ['c_ctx', 'b_mod', 'norm_g', 'ssm_a_re', 'ssm_a_im', 'ssm_log_dt', 'ssm_b_re', 'ssm_b_im', 'ssm_c_re', 'ssm_c_im',
              'ssm_d', 'ssm_b_glu', 'attn_q_norm', 'attn_k_norm', 'final_norm_g']
SSM_NAMES = ['ssm_a_re', 'ssm_a_im', 'ssm_log_dt', 'ssm_b_re', 'ssm_b_im', 'ssm_c_re', 'ssm_c_im', 'ssm_d']


def _full_from_shards(gathered, name, shard_shape):
    rows, cols = shard_shape
    w = gathered.reshape(N_DEV, rows, cols)
    if name in COLUMN_SHARDED:
        return w.transpose(1, 0, 2).reshape(rows, N_DEV * cols)
    return w.reshape(N_DEV * rows, cols)


def _shards_from_full(g, name):
    if name in COLUMN_SHARDED:
        rows, cols = g.shape
        g = g.reshape(rows, N_DEV, cols // N_DEV).transpose(1, 0, 2)
    return g.reshape(N_DEV, -1, 128)


def kernel(x, c, ctx, c_ctx, w_mod, b_mod, norm_g, ssm_w_in, ssm_a_re, ssm_a_im, ssm_log_dt, ssm_b_re, ssm_b_im, ssm_c_re, ssm_c_im, ssm_d, ssm_w_glu, ssm_b_glu, ssm_w_out, attn_w_in, attn_q_norm, attn_k_norm, attn_w_out, final_norm_g, loss_target, m_c_ctx, m_w_mod, m_b_mod, m_norm_g, m_ssm_w_in, m_ssm_a_re, m_ssm_a_im, m_ssm_log_dt, m_ssm_b_re, m_ssm_b_im, m_ssm_c_re, m_ssm_c_im, m_ssm_d, m_ssm_w_glu, m_ssm_b_glu, m_ssm_w_out, m_attn_w_in, m_attn_q_norm, m_attn_k_norm, m_attn_w_out, m_final_norm_g, v_c_ctx, v_w_mod, v_b_mod, v_norm_g, v_ssm_w_in, v_ssm_a_re, v_ssm_a_im, v_ssm_log_dt, v_ssm_b_re, v_ssm_b_im, v_ssm_c_re, v_ssm_c_im, v_ssm_d, v_ssm_w_glu, v_ssm_b_glu, v_ssm_w_out, v_attn_w_in, v_attn_q_norm, v_attn_k_norm, v_attn_w_out, v_final_norm_g):
    env = dict(locals())
    weights = {n: env[n] for n in WEIGHT_NAMES}
    mom_m = {n: env["m_" + n] for n in WEIGHT_NAMES}
    mom_v = {n: env["v_" + n] for n in WEIGHT_NAMES}
    d = D_MODEL
    me = _my_index()
    mod_cols = w_mod.shape[-1]

    shard_shapes = {n: weights[n].shape[1:] for n in SHARDED}
    pack_shards = lambda names: _pack([weights[n] for n in names], 1).astype(BF16)

    def unpack_full(gathered, names):
        full, pos = [], 0
        for n in names:
            rows = math.prod(shard_shapes[n]) // 128
            full.append(_full_from_shards(gathered[:, pos:pos + rows], n, shard_shapes[n]))
            pos += rows
        return full

    def exchange_of(names):
        return (pack_shards(names), lambda gathered: unpack_full(gathered, names),
                lambda *grads: jnp.concatenate([_shards_from_full(t, n) for t, n in zip(grads, names)], axis=1).astype(BF16))

    c_all, w_first = _exchange_pair(c.reshape(8, d // 8), False, pack_shards(FIRST_SHARDED), False, "gather_c_and_ssm_w_in")
    cond = jnp.concatenate([c_all.reshape(N_DEV, d), c_ctx.reshape(1, d), jnp.zeros((MOD_ROWS - N_DEV - 1, d), F32)], axis=0)
    (w_ssm_in,) = unpack_full(w_first, FIRST_SHARDED)

    b_cols = lax.dynamic_slice(b_mod, (0, me * mod_cols), (2, mod_cols))
    mod_shard = _mod_fwd(cond, w_mod, b_cols)
    mod_all = _exchange(mod_shard.reshape(2 * MOD_ROWS, mod_cols), "gather_mod", False)
    mod_full = mod_all.reshape(N_DEV, 2, MOD_ROWS, mod_cols).transpose(1, 2, 0, 3).reshape(2, MOD_ROWS, 3 * d)
    lat_rows = lax.dynamic_slice(mod_full, (0, me, 0), (2, 1, 3 * d))
    mods = []
    for i in range(2):
        seg = jnp.stack([mod_full[i, CTX_ROW:CTX_ROW + 1], lat_rows[i]])
        mods.append((seg[:, :, :d], seg[:, :, d:2 * d], seg[:, :, 2 * d:]))

    ssm = tuple(weights[n][0] for n in SSM_NAMES)
    loss, grad_x, g, d_mods = _local_step(
        x[0], ctx[0], loss_target[0], mods, norm_g, ssm, w_ssm_in, None, ssm_b_glu[0], None,
        None, attn_q_norm[0], attn_k_norm[0], None, final_norm_g, exchange_of(ATTN_SHARDED), exchange_of(POST_SHARDED))

    d_rows = jnp.stack([jnp.concatenate(dm, axis=-1) for dm in d_mods])
    d_rows = jnp.concatenate([d_rows.reshape(4, 3 * d), jnp.zeros((4, 3 * d), F32)], axis=0)
    d_all, first_parts = _exchange_pair(d_rows, False, _shards_from_full(g['ssm_w_in'], 'ssm_w_in').astype(BF16), True,
                                        "gather_dmod_and_scatter_ssm_w_in_grads")
    d_all = d_all[:, :4].reshape(N_DEV, 2, 2, 3 * d)
    d_all = lax.dynamic_slice(d_all, (0, 0, 0, me * mod_cols), (N_DEV, 2, 2, mod_cols))
    d_w_mod, d_c_ctx = _mod_bwd(cond, d_all[:, :, 1].transpose(1, 0, 2), d_all[:, :, 0:1], w_mod)
    d_b_mod = jnp.stack([jnp.concatenate([t[0] + t[1] for t in dm], axis=-1).reshape(3 * d) for dm in d_mods])

    parts_of = {}
    for names, parts in ((FIRST_SHARDED, first_parts), (POST_SHARDED, g['post_parts']), (ATTN_SHARDED, g['attn_parts'])):
        pos = 0
        for n in names:
            rows = math.prod(shard_shapes[n]) // 128
            parts_of[n] = parts[:, pos:pos + rows].reshape((N_DEV,) + shard_shapes[n])
            pos += rows

    def update(n, g_parts):
        as_2d = lambda t: t.reshape(-1, t.shape[-1])
        res = _adamw(as_2d(weights[n]), g_parts, as_2d(mom_m[n]), as_2d(mom_v[n]), "adamw_" + n)
        return [t.reshape(weights[n].shape) for t in res]

    big = {n: update(n, parts_of[n]) for n in SHARDED}
    big['w_mod'] = update('w_mod', d_w_mod.reshape(1, -1, mod_cols))

    small = dict(zip(SSM_NAMES, g['ssm']))
    small.update(c_ctx=d_c_ctx, b_mod=d_b_mod, norm_g=g['norm_g'], ssm_b_glu=g['ssm_b_glu'], attn_q_norm=g['attn_q_norm'],
                 attn_k_norm=g['attn_k_norm'], final_norm_g=g['final_norm_g'])
    pack_small = lambda t, last: _pack([t[n] for n in REPLICATED] + [last], ADAM_TILE)
    no_weight = jnp.zeros((1,), F32)
    g_small = pack_small(small, loss.reshape(1))
    slices = _exchange(g_small.reshape(N_DEV, -1, 128), "scatter_small_grads", True)
    g_small = _exchange(_sum_parts(slices), "gather_small_grads", False).reshape(1, -1, 128)
    rep = _adamw(pack_small(weights, no_weight), g_small, pack_small(mom_m, no_weight), pack_small(mom_v, no_weight),
                 "adamw_replicated")
    rep = [_unpack(t, [weights[n].shape for n in REPLICATED] + [(1,)]) for t in rep]
    loss = rep[0][-1][0]

    results = []
    for kind in range(4):
        by_name = {n: res[kind] for n, res in big.items()}
        by_name.update(zip(REPLICATED, rep[kind]))
        results.extend(by_name[n] for n in WEIGHT_NAMES)
    return (loss, grad_x[None], *results)
```

```python
import functools
import math

import jax
import jax.numpy as jnp
from jax import lax
from jax.experimental import pallas as pl
from jax.experimental.pallas import tpu as pltpu

F32 = jnp.float32
BF16 = jnp.bfloat16

N_DEV = 8
D_MODEL = 1024
NORM_EPS = 1e-6
SSM_GROUP = 16
SSM_GROUPS = 64
SSM_STATE = 64
GROUPS_PER_BLOCK = 8
N_BLOCKS = SSM_GROUPS // GROUPS_PER_BLOCK
HALF = GROUPS_PER_BLOCK * SSM_STATE
HEAD_DIM = 64
N_Q_HEADS = 16
N_KV_HEADS = 4
KV_REP = N_Q_HEADS // N_KV_HEADS
KV_WIDTH = N_KV_HEADS * HEAD_DIM
GRID_W = 64
ROPE_THETA = 10000.0
ADAM_LR, ADAM_B1, ADAM_B2, ADAM_EPS, ADAM_WD, ADAM_STEP = 0.001, 0.9, 0.999, 1e-08, 0.01, 10

ROW_TILE = 256
SCAN_CHUNK = 256
VMEM_LIMIT = 56 * 1024 * 1024
MESH_IDS = pl.DeviceIdType.MESH


def _cparams(n_axes):
    return pltpu.CompilerParams(dimension_semantics=("arbitrary",) * n_axes, vmem_limit_bytes=VMEM_LIMIT)


def _dot(a, b):
    return jnp.dot(a.astype(BF16), b.astype(BF16), preferred_element_type=F32)


def _dot_t0(a, b):
    return lax.dot_general(a.astype(BF16), b.astype(BF16), (((0,), (0,)), ((), ())), preferred_element_type=F32)


def _dot_t1(a, b):
    return lax.dot_general(a.astype(BF16), b.astype(BF16), (((1,), (1,)), ((), ())), preferred_element_type=F32)


def _s5_prep(a_re, a_im, log_dt, b_re, b_im):
    dt = jnp.exp(log_dt)[:, None]
    ldr, ldi = a_re * dt, a_im * dt
    mag = jnp.exp(ldr)
    abar_re, abar_im = mag * jnp.cos(ldi), mag * jnp.sin(ldi)
    den = a_re * a_re + a_im * a_im
    num_re, num_im = abar_re - 1.0, abar_im
    coef_re = (num_re * a_re + num_im * a_im) / den
    coef_im = (num_im * a_re - num_re * a_im) / den
    bbar_re = coef_re[..., None] * b_re - coef_im[..., None] * b_im
    bbar_im = coef_re[..., None] * b_im + coef_im[..., None] * b_re
    return abar_re, abar_im, bbar_re, bbar_im


def _s5_blocks(abar_re, abar_im, bbar_re, bbar_im, c_re, c_im):
    eye = jnp.eye(GROUPS_PER_BLOCK, dtype=F32)
    bb = jnp.stack([bbar_re, bbar_im]).reshape(2, N_BLOCKS, GROUPS_PER_BLOCK, SSM_STATE, SSM_GROUP)
    b_blk = jnp.einsum('rqgph,gk->qghrkp', bb, eye).reshape(N_BLOCKS, 128, 2 * HALF)
    cc = jnp.stack([c_re, -c_im]).reshape(2, N_BLOCKS, GROUPS_PER_BLOCK, SSM_GROUP, SSM_STATE)
    c_blk = jnp.einsum('rqghp,gk->qrgpkh', cc, eye).reshape(N_BLOCKS, 2 * HALF, 128)
    abar = jnp.stack([abar_re.reshape(N_BLOCKS, HALF), abar_im.reshape(N_BLOCKS, HALF)])
    return abar, b_blk, c_blk


def _s5_unblock(d_b_blk, d_ct_blk):
    db = d_b_blk.reshape(N_BLOCKS, GROUPS_PER_BLOCK, SSM_GROUP, 2, GROUPS_PER_BLOCK, SSM_STATE)
    db = jnp.einsum('qghrgp->rqgph', db).reshape(2, SSM_GROUPS, SSM_STATE, SSM_GROUP)
    dc = d_ct_blk.reshape(N_BLOCKS, GROUPS_PER_BLOCK, SSM_GROUP, 2, GROUPS_PER_BLOCK, SSM_STATE)
    dc = jnp.einsum('qghrgp->rqghp', dc).reshape(2, SSM_GROUPS, SSM_GROUP, SSM_STATE)
    return db[0], db[1], dc[0], -dc[1]


def _scan_chunk_of_step(j, n_chunks, n_ctx_chunks, reverse):
    if not reverse:
        return j
    return jnp.where(j < n_ctx_chunks, n_ctx_chunks - 1 - j, n_chunks - 1 - j + n_ctx_chunks)


LANE_TILES = 2 * HALF // 128
RE_TILES = HALF // 128


def _tiles(v):
    return [v[:, l * 128:(l + 1) * 128] for l in range(v.shape[1] // 128)]


def _scatter_steps(s_ref, q, x):
    for l in range(LANE_TILES):
        s_ref[l, pl.ds(q, x.shape[0], stride=N_BLOCKS), :] = x[:, l * 128:(l + 1) * 128]


def _gather_steps(s_ref, q, n_steps):
    return jnp.concatenate([s_ref[l, pl.ds(q, n_steps, stride=N_BLOCKS), :] for l in range(LANE_TILES)], axis=1)


def _load_step(s_ref, t):
    row = pl.multiple_of(t * N_BLOCKS, N_BLOCKS)
    return [s_ref[l, pl.ds(row, N_BLOCKS), :] for l in range(LANE_TILES)]


def _store_step(s_ref, t, tiles):
    row = pl.multiple_of(t * N_BLOCKS, N_BLOCKS)
    for l in range(LANE_TILES):
        s_ref[l, pl.ds(row, N_BLOCKS), :] = tiles[l]


SCAN_UNROLL = 8
ADJOINT_UNROLL = 15


def _unrolled_loop(n_steps, unroll, step, carry):
    assert n_steps % unroll == 0

    def steps(i, c):
        for r in range(unroll):
            c = step(unroll * i + r, c)
        return c

    return lax.fori_loop(0, n_steps // unroll, steps, carry)


def _cmul_add(a, h, x, conj):
    re, im = [], []
    for l in range(RE_TILES):
        ar, ai, hr, hi = a[l], a[RE_TILES + l], h[l], h[RE_TILES + l]
        if conj:
            re.append(ar * hr + ai * hi + x[l])
            im.append(ar * hi - ai * hr + x[RE_TILES + l])
        else:
            re.append(ar * hr - ai * hi + x[l])
            im.append(ar * hi + ai * hr + x[RE_TILES + l])
    return re + im


def _s5_scan_fwd(u, abar, b_blk, c_blk, d_skip, n_ctx, reverse, rider=None):
    n_rows, width = u.shape
    tc = SCAN_CHUNK
    n_chunks, n_ctx_chunks = n_rows // tc, n_ctx // tc
    with_skip = d_skip is not None

    def body(*refs):
        if rider is not None:
            x_ref, ride_out, sems = refs[4 + with_skip], refs[7 + with_skip], refs[-3:]
            _ride(rider, pl.program_id(0) == 0, pl.program_id(0) == n_chunks - 1, (x_ref, ride_out) + tuple(sems))
            refs = refs[:4 + with_skip] + refs[5 + with_skip:7 + with_skip] + refs[8 + with_skip:-3]
        if with_skip:
            u_ref, a_ref, b_ref, c_ref, d_ref, y_ref, hb_ref, s_ref, h_ref = refs
        else:
            u_ref, a_ref, b_ref, c_ref, y_ref, hb_ref, s_ref, h_ref = refs
        j = pl.program_id(0)

        @pl.when(j == 0)
        def _():
            h_ref[...] = jnp.zeros_like(h_ref)

        hb_ref[0] = h_ref[...]
        for q in range(N_BLOCKS):
            _scatter_steps(s_ref, q, _dot(u_ref[:, q * 128:(q + 1) * 128], b_ref[q]))
        a = _tiles(a_ref[0]) + _tiles(a_ref[1])

        def step(s, h):
            t = tc - 1 - s if reverse else s
            h = _cmul_add(a, h, _load_step(s_ref, t), conj=False)
            _store_step(s_ref, t, h)
            return h

        h = _unrolled_loop(tc, SCAN_UNROLL, step, _tiles(h_ref[...]))
        h_ref[...] = jnp.concatenate(h, axis=1)
        for q in range(N_BLOCKS):
            yq = _dot(_gather_steps(s_ref, q, tc), c_ref[q])
            if with_skip:
                yq = yq + d_ref[:, q * 128:(q + 1) * 128] * u_ref[:, q * 128:(q + 1) * 128]
            y_ref[:, q * 128:(q + 1) * 128] = yq

    chunk = functools.partial(_scan_chunk_of_step, n_chunks=n_chunks, n_ctx_chunks=n_ctx_chunks, reverse=reverse)
    full3 = lambda j: (0, 0, 0)
    in_specs = [pl.BlockSpec((tc, width), lambda j: (chunk(j), 0)),
                pl.BlockSpec((2, N_BLOCKS, HALF), full3),
                pl.BlockSpec((N_BLOCKS, 128, 2 * HALF), full3),
                pl.BlockSpec((N_BLOCKS, 2 * HALF, 128), full3)]
    args = [u, abar, b_blk.astype(BF16), c_blk.astype(BF16)]
    if with_skip:
        in_specs.append(pl.BlockSpec((1, width), lambda j: (0, 0)))
        args.append(d_skip.reshape(1, width))
    out_specs = [pl.BlockSpec((tc, width), lambda j: (chunk(j), 0)),
                 pl.BlockSpec((1, N_BLOCKS, 2 * HALF), lambda j: (chunk(j), 0, 0))]
    out_shape = [_sds((n_rows, width)), _sds((n_chunks, N_BLOCKS, 2 * HALF))]
    scratch = [pltpu.VMEM((LANE_TILES, tc * N_BLOCKS, 128), F32), pltpu.VMEM((N_BLOCKS, 2 * HALF), F32)]
    if rider is not None:
        in_specs.append(HBM_SPEC)
        args.append(rider[0])
        out_specs.append(HBM_SPEC)
        out_shape.append(_exchange_out_shape(*rider))
        scratch += _exchange_semaphores()
    return pl.pallas_call(
        body, name="s5_scan_fwd_rev" if reverse else "s5_scan_fwd",
        grid=(n_chunks,), in_specs=in_specs, out_specs=out_specs, out_shape=out_shape, scratch_shapes=scratch,
        compiler_params=_cparams(1),
    )(*args)


def _s5_scan_bwd(u, dy, hb, abar, b_blk, c_blk, d_skip, n_ctx, reverse, rider=None):
    n_rows, width = u.shape
    tc = SCAN_CHUNK
    n_chunks, n_ctx_chunks = n_rows // tc, n_ctx // tc
    with_skip = d_skip is not None
    n_in, n_out = 7 + with_skip, 4 + with_skip

    def body(*refs):
        if rider is not None:
            x_ref, ride_out, sems = refs[n_in], refs[n_in + 1 + n_out], refs[-3:]
            _ride(rider, pl.program_id(0) == 0, pl.program_id(0) == n_chunks - 1, (x_ref, ride_out) + tuple(sems))
            refs = refs[:n_in] + refs[n_in + 1:n_in + 1 + n_out] + refs[n_in + 2 + n_out:-3]
        if with_skip:
            (u_ref, dy_ref, hb_ref, a_ref, b_ref, bt_ref, ct_ref, d_ref,
             du_ref, da_ref, db_ref, dct_ref, dd_ref, sh_ref, sg_ref, g_ref) = refs
        else:
            (u_ref, dy_ref, hb_ref, a_ref, b_ref, bt_ref, ct_ref,
             du_ref, da_ref, db_ref, dct_ref, sh_ref, sg_ref, g_ref) = refs
        j = pl.program_id(0)

        @pl.when(j == 0)
        def _():
            g_ref[...] = jnp.zeros_like(g_ref)
            da_ref[...] = jnp.zeros_like(da_ref)
            db_ref[...] = jnp.zeros_like(db_ref)
            dct_ref[...] = jnp.zeros_like(dct_ref)
            if with_skip:
                dd_ref[...] = jnp.zeros_like(dd_ref)

        for q in range(N_BLOCKS):
            _scatter_steps(sh_ref, q, _dot(u_ref[:, q * 128:(q + 1) * 128], b_ref[q]))
            _scatter_steps(sg_ref, q, _dot(dy_ref[:, q * 128:(q + 1) * 128], ct_ref[q]))
        a = _tiles(a_ref[0]) + _tiles(a_ref[1])
        time_of = (lambda s: tc - 1 - s) if reverse else (lambda s: s)

        def fwd_step(s, h):
            h = _cmul_add(a, h, _load_step(sh_ref, time_of(s)), conj=False)
            _store_step(sh_ref, time_of(s), h)
            return h

        h0 = _tiles(hb_ref[0])
        _unrolled_loop(tc, SCAN_UNROLL, fwd_step, h0)

        def adj(t, h_prev, carry):
            g, da = carry
            g = _cmul_add(a, g, _load_step(sg_ref, t), conj=True)
            _store_step(sg_ref, t, g)
            da_re = [da[l] + g[l] * h_prev[l] + g[RE_TILES + l] * h_prev[RE_TILES + l] for l in range(RE_TILES)]
            da_im = [da[RE_TILES + l] + g[RE_TILES + l] * h_prev[l] - g[l] * h_prev[RE_TILES + l] for l in range(RE_TILES)]
            return g, da_re + da_im

        def bwd_step(i, carry):
            s = tc - 1 - i
            return adj(time_of(s), _load_step(sh_ref, time_of(s - 1)), carry)

        carry = (_tiles(g_ref[...]), _tiles(da_ref[0]) + _tiles(da_ref[1]))
        carry = _unrolled_loop(tc - 1, ADJOINT_UNROLL, bwd_step, carry)
        g, da = adj(time_of(0), h0, carry)
        g_ref[...] = jnp.concatenate(g, axis=1)
        da_ref[0] = jnp.concatenate(da[:RE_TILES], axis=1)
        da_ref[1] = jnp.concatenate(da[RE_TILES:], axis=1)

        for q in range(N_BLOCKS):
            cols = slice(q * 128, (q + 1) * 128)
            uq, dyq = u_ref[:, cols], dy_ref[:, cols]
            gq = _gather_steps(sg_ref, q, tc)
            duq = _dot(gq, bt_ref[q])
            if with_skip:
                duq = duq + d_ref[:, cols] * dyq
                dd_ref[:, cols] += jnp.sum(dyq * uq, axis=0, keepdims=True)
            du_ref[:, cols] = duq
            db_ref[q] += _dot_t0(uq, gq)
            dct_ref[q] += _dot_t0(dyq, _gather_steps(sh_ref, q, tc))

    def chunk(j):
        return _scan_chunk_of_step(n_chunks - 1 - j, n_chunks, n_ctx_chunks, reverse)

    full2 = lambda j: (0, 0)
    full3 = lambda j: (0, 0, 0)
    row = pl.BlockSpec((tc, width), lambda j: (chunk(j), 0))
    in_specs = [row, row,
                pl.BlockSpec((1, N_BLOCKS, 2 * HALF), lambda j: (chunk(j), 0, 0)),
                pl.BlockSpec((2, N_BLOCKS, HALF), full3),
                pl.BlockSpec((N_BLOCKS, 128, 2 * HALF), full3),
                pl.BlockSpec((N_BLOCKS, 2 * HALF, 128), full3),
                pl.BlockSpec((N_BLOCKS, 128, 2 * HALF), full3)]
    args = [u, dy, hb, abar, b_blk.astype(BF16), jnp.swapaxes(b_blk, 1, 2).astype(BF16),
            jnp.swapaxes(c_blk, 1, 2).astype(BF16)]
    out_specs = [row,
                 pl.BlockSpec((2, N_BLOCKS, HALF), full3),
                 pl.BlockSpec((N_BLOCKS, 128, 2 * HALF), full3),
                 pl.BlockSpec((N_BLOCKS, 128, 2 * HALF), full3)]
    out_shape = [jax.ShapeDtypeStruct((n_rows, width), F32),
                 jax.ShapeDtypeStruct((2, N_BLOCKS, HALF), F32),
                 jax.ShapeDtypeStruct((N_BLOCKS, 128, 2 * HALF), F32),
                 jax.ShapeDtypeStruct((N_BLOCKS, 128, 2 * HALF), F32)]
    if with_skip:
        in_specs.append(pl.BlockSpec((1, width), full2))
        args.append(d_skip.reshape(1, width))
        out_specs.append(pl.BlockSpec((1, width), full2))
        out_shape.append(jax.ShapeDtypeStruct((1, width), F32))
    scratch = [pltpu.VMEM((LANE_TILES, tc * N_BLOCKS, 128), F32), pltpu.VMEM((LANE_TILES, tc * N_BLOCKS, 128), F32),
               pltpu.VMEM((N_BLOCKS, 2 * HALF), F32)]
    if rider is not None:
        in_specs.append(HBM_SPEC)
        args.append(rider[0])
        out_specs.append(HBM_SPEC)
        out_shape.append(_exchange_out_shape(*rider))
        scratch += _exchange_semaphores()
    return pl.pallas_call(
        body, name="s5_scan_bwd_rev" if reverse else "s5_scan_bwd",
        grid=(n_chunks,), in_specs=in_specs, out_specs=out_specs, out_shape=out_shape, scratch_shapes=scratch,
        compiler_params=_cparams(1),
    )(*args)


def _s5_dir_params(d, a_re, a_im, log_dt, b_re, b_im):
    return a_re[d], a_im[d], log_dt[d], b_re[d], b_im[d]


def _s5_forward(u, ssm, n_ctx, riders=(None, None)):
    a_re, a_im, log_dt, b_re, b_im, c_re, c_im, d_skip = ssm
    outs, saved, carried = [], [], [None, None]
    for d in range(2):
        prep = _s5_prep(*_s5_dir_params(d, a_re, a_im, log_dt, b_re, b_im))
        abar, b_blk, c_blk = _s5_blocks(*prep, c_re[d], c_im[d])
        res = _s5_scan_fwd(u, abar, b_blk, c_blk, d_skip if d == 0 else None, n_ctx, reverse=(d == 1), rider=riders[d])
        if riders[d] is not None:
            carried[d] = res[2]
        outs.append(res[0])
        saved.append((res[1], abar, b_blk, c_blk))
    return outs, saved, carried


def _s5_backward(u, dy, ssm, saved, n_ctx, riders=(None, None)):
    a_re, a_im, log_dt, b_re, b_im, c_re, c_im, d_skip = ssm
    dus, grads = [], [[] for _ in range(7)]
    d_d, carried = None, [None, None]
    for d in range(2):
        hb, abar, b_blk, c_blk = saved[d]
        res = _s5_scan_bwd(u, dy, hb, abar, b_blk, c_blk, d_skip if d == 0 else None, n_ctx, reverse=(d == 1),
                           rider=riders[d])
        if riders[d] is not None:
            carried[d], res = res[-1], res[:-1]
        if d == 0:
            du, d_abar, d_b_blk, d_ct_blk, d_d = res
        else:
            du, d_abar, d_b_blk, d_ct_blk = res
        dus.append(du)
        dbb_re, dbb_im, dc_re, dc_im = _s5_unblock(d_b_blk, d_ct_blk)
        _, vjp = jax.vjp(_s5_prep, *_s5_dir_params(d, a_re, a_im, log_dt, b_re, b_im))
        shape = (SSM_GROUPS, SSM_STATE)
        g5 = vjp((d_abar[0].reshape(shape), d_abar[1].reshape(shape), dbb_re, dbb_im))
        for k, g in enumerate(tuple(g5) + (dc_re, dc_im)):
            grads[k].append(g)
    grads = [jnp.stack(g) for g in grads]
    return dus, grads + [d_d.reshape(-1)], carried


INV_SQRT2 = 0.7071067811865476
INV_SQRT_2PI = 0.3989422804014327


def _rows(cols):
    return pl.BlockSpec((ROW_TILE, cols), lambda i: (i, 0))


def _rows_skip_ctx(cols):
    return pl.BlockSpec((ROW_TILE, cols), lambda i: (i + 1, 0))


def _rows_lat(cols):
    return pl.BlockSpec((ROW_TILE, cols), lambda i: (jnp.maximum(i - 1, 0), 0))


def _full(shape):
    nd = len(shape)
    return pl.BlockSpec(shape, lambda i: (0,) * nd)


def _seg(cols):
    return pl.BlockSpec((1, 1, cols), lambda i: (jnp.minimum(i, 1), 0, 0))


def _lat_seg(cols):
    return pl.BlockSpec((1, 1, cols), lambda i: (1, 0, 0))


def _sds(shape, dtype=F32):
    return jax.ShapeDtypeStruct(shape, dtype)


def _sum0(x):
    return jnp.sum(x, axis=0, keepdims=True)


def _sigmoid(x):
    return jax.nn.sigmoid(x)


def _rms_mod(x, g, scale, shift):
    r = lax.rsqrt(jnp.mean(x * x, axis=-1, keepdims=True) + NORM_EPS)
    return (x * r * g) * (1.0 + scale) + shift


def _rms_mod_bwd(x, g, scale, dh):
    r = lax.rsqrt(jnp.mean(x * x, axis=-1, keepdims=True) + NORM_EPS)
    n = x * r
    dyg = dh * (1.0 + scale)
    dn = dyg * g
    dx = r * (dn - n * jnp.mean(dn * n, axis=-1, keepdims=True))
    return dx, _sum0(dyg * n), _sum0(dh * (n * g)), _sum0(dh)


def _head_of_lane(width):
    return (jnp.arange(width)[:, None] // HEAD_DIM == jnp.arange(128)[None, :]).astype(BF16)


def _split_dot(t, w, transposed):
    hi = t.astype(BF16)
    lo = (t - hi.astype(F32)).astype(BF16)
    f = _dot_t1 if transposed else _dot
    return f(hi, w) + f(lo, w)


def _head_sums(t, hl):
    return _split_dot(_split_dot(t, hl, False), hl, True)


def _rope_partner(x):
    n = x.shape[1]
    lane = lax.broadcasted_iota(jnp.int32, x.shape, 1)
    return jnp.where((lane & 16) == 0, pltpu.roll(x, n - 16, 1), pltpu.roll(x, 16, 1))


def _lanes(tab, width):
    return jnp.tile(tab, (1, width // tab.shape[1]))


def _head_norm_rope(x, gain, cos, sin, hl):
    r = lax.rsqrt(_head_sums(x * x, hl) * (1.0 / HEAD_DIM) + NORM_EPS)
    y = x * r * gain
    return y * cos + _rope_partner(y) * sin


def _head_norm_rope_bwd(x, gain, cos, sin, hl, dout):
    dy = dout * cos + _rope_partner(dout * sin)
    r = lax.rsqrt(_head_sums(x * x, hl) * (1.0 / HEAD_DIM) + NORM_EPS)
    n = x * r
    dn = dy * gain
    dx = r * (dn - n * (_head_sums(dn * n, hl) * (1.0 / HEAD_DIM)))
    return dx, _sum0(dy * n)


def _rope_tables(n_ctx, n_lat):
    t = jnp.arange(n_lat)
    pos = jnp.stack([(t // GRID_W).astype(F32), (t % GRID_W).astype(F32)], axis=1)
    n_freq = HEAD_DIM // 4
    freqs = ROPE_THETA ** (-jnp.arange(n_freq, dtype=F32) / n_freq)
    ang = pos[:, :, None] * freqs[None, None, :]
    cos = jnp.repeat(jnp.cos(ang)[:, :, None, :], 2, axis=2).reshape(n_lat, HEAD_DIM)
    sin = jnp.sin(ang)
    sin = jnp.stack([-sin, sin], axis=2).reshape(n_lat, HEAD_DIM)
    cos = jnp.concatenate([jnp.ones((n_ctx, HEAD_DIM), F32), cos], axis=0)
    sin = jnp.concatenate([jnp.zeros((n_ctx, HEAD_DIM), F32), sin], axis=0)
    return jnp.tile(cos, (1, 2)), jnp.tile(sin, (1, 2))


def _token_tile(ctx_ref, x_ref):
    return jnp.where(pl.program_id(0) == 0, ctx_ref[...], x_ref[...])


def _ssm_in(ctx, x, g, scale, shift, w_in):
    n_rows, d = ctx.shape[0] + x.shape[0], x.shape[1]
    e = w_in.shape[1] // 2

    def body(c_ref, x_ref, g_ref, sc_ref, sh_ref, w_ref, u_ref, z_ref):
        h = _rms_mod(_token_tile(c_ref, x_ref), g_ref[...], sc_ref[0], sh_ref[0])
        proj = _dot(h, w_ref[...])
        u_ref[...] = proj[:, :e]
        z_ref[...] = proj[:, e:]

    return pl.pallas_call(
        body, name="ssm_in", grid=(n_rows // ROW_TILE,),
        in_specs=[_full(ctx.shape), _rows_lat(d), _full((1, d)), _seg(d), _seg(d), _full(w_in.shape)],
        out_specs=[_rows(e), _rows(e)], out_shape=[_sds((n_rows, e)), _sds((n_rows, e))],
        compiler_params=_cparams(1),
    )(ctx, x, g, scale, shift, w_in)


def _s5_post_math(y, z, w_glu, b_glu, w_out):
    er = lax.erf(y * INV_SQRT2)
    g = 0.5 * y * (1.0 + er)
    sg = _sigmoid(_dot(g, w_glu) + b_glu)
    g2 = g * sg
    sz = _sigmoid(z)
    silu_z = z * sz
    m = g2 * silu_z
    return er, g, sg, g2, sz, silu_z, m, _dot(m, w_out)


def _ssm_post(ctx, x, y0, y1, z, gate, w_glu, b_glu, w_out):
    n_rows, d = ctx.shape[0] + x.shape[0], x.shape[1]
    e = z.shape[1]

    def body(c_ref, x_ref, y0_ref, y1_ref, z_ref, gt_ref, wg_ref, bg_ref, wo_ref, o_ref):
        out = _s5_post_math(y0_ref[...] + y1_ref[...], z_ref[...], wg_ref[...], bg_ref[...], wo_ref[...])[-1]
        o_ref[...] = _token_tile(c_ref, x_ref) + gt_ref[0] * out

    return pl.pallas_call(
        body, name="ssm_post", grid=(n_rows // ROW_TILE,),
        in_specs=[_full(ctx.shape), _rows_lat(d), _rows(e), _rows(e), _rows(e), _seg(d), _full(w_glu.shape), _full((1, e)),
                  _full(w_out.shape)],
        out_specs=_rows(d), out_shape=_sds((n_rows, d)),
        compiler_params=_cparams(1),
    )(ctx, x, y0, y1, z, gate, w_glu, b_glu, w_out)


def _init_acc(first, *refs):
    @pl.when(first)
    def _():
        for r in refs:
            r[...] = jnp.zeros_like(r)


def _ssm_post_bwd(dxa, y0, y1, z, gate, w_glu, b_glu, w_out):
    n_rows, d = dxa.shape
    e = z.shape[1]

    def body(dx_ref, y0_ref, y1_ref, z_ref, gt_ref, wg_ref, bg_ref, wo_ref,
             dy_ref, dz_ref, dgt_ref, dwo_ref, dwg_ref, dbg_ref):
        i = pl.program_id(0)
        _init_acc(i == 0, dwo_ref, dwg_ref, dbg_ref)
        _init_acc(i <= 1, dgt_ref)
        y, zz = y0_ref[...] + y1_ref[...], z_ref[...]
        er, g, sg, g2, sz, silu_z, m, out = _s5_post_math(y, zz, wg_ref[...], bg_ref[...], wo_ref[...])
        dxa_t = dx_ref[...]
        dgt_ref[0] += _sum0(dxa_t * out)
        dout = gt_ref[0] * dxa_t
        dm = _dot_t1(dout, wo_ref[...])
        dwo_ref[...] += _dot_t0(m, dout)
        dg2 = dm * silu_z
        dz_ref[...] = dm * g2 * (sz * (1.0 + zz * (1.0 - sz)))
        dt = dg2 * g * sg * (1.0 - sg)
        dwg_ref[...] += _dot_t0(g, dt)
        dbg_ref[...] += _sum0(dt)
        dg = dg2 * sg + _dot_t1(dt, wg_ref[...])
        dy_ref[...] = dg * (0.5 * (1.0 + er) + y * jnp.exp(-0.5 * y * y) * INV_SQRT_2PI)

    return pl.pallas_call(
        body, name="ssm_post_bwd", grid=(n_rows // ROW_TILE,),
        in_specs=[_rows(d), _rows(e), _rows(e), _rows(e), _seg(d), _full(w_glu.shape), _full((1, e)), _full(w_out.shape)],
        out_specs=[_rows(e), _rows(e), _seg(d), _full(w_out.shape), _full(w_glu.shape), _full((1, e))],
        out_shape=[_sds((n_rows, e)), _sds((n_rows, e)), _sds((2, 1, d)), _sds(w_out.shape), _sds(w_glu.shape), _sds((1, e))],
        compiler_params=_cparams(1),
    )(dxa, y0, y1, z, gate, w_glu, b_glu, w_out)


def _ssm_in_bwd(du0, du1, dz, ctx, x_lat, dxa_next, g, scale, shift, w_in):
    n_lat, d = x_lat.shape
    n_rows = ctx.shape[0] + n_lat
    e = dz.shape[1]

    def body(du0_ref, du1_ref, dz_ref, c_ref, x_ref, dn_ref, g_ref, sc_ref, sh_ref, w_ref,
             gx_ref, dw_ref, dg_ref, dsc_ref, dsh_ref):
        i = pl.program_id(0)
        _init_acc(i == 0, dw_ref, dg_ref)
        _init_acc(i <= 1, dsc_ref, dsh_ref)
        x = _token_tile(c_ref, x_ref)
        h = _rms_mod(x, g_ref[...], sc_ref[0], sh_ref[0])
        dproj = jnp.concatenate([du0_ref[...] + du1_ref[...], dz_ref[...]], axis=1)
        dh = _dot_t1(dproj, w_ref[...])
        dw_ref[...] += _dot_t0(h, dproj)
        dx, dg, dsc, dsh = _rms_mod_bwd(x, g_ref[...], sc_ref[0], dh)
        dg_ref[...] += dg
        dsc_ref[0] += dsc
        dsh_ref[0] += dsh
        gx_ref[...] = dn_ref[...] + dx

    return pl.pallas_call(
        body, name="ssm_in_bwd", grid=(n_rows // ROW_TILE,),
        in_specs=[_rows(e), _rows(e), _rows(e), _full(ctx.shape), _rows_lat(d), _rows(d), _full((1, d)), _seg(d), _seg(d),
                  _full(w_in.shape)],
        out_specs=[_rows_lat(d), _full((d, 2 * e)), _full((1, d)), _seg(d), _seg(d)],
        out_shape=[_sds((n_lat, d)), _sds((d, 2 * e)), _sds((1, d)), _sds((2, 1, d)), _sds((2, 1, d))],
        compiler_params=_cparams(1),
    )(du0, du1, dz, ctx, x_lat, dxa_next, g, scale, shift, w_in)


Q_WIDTH = N_Q_HEADS * HEAD_DIM
SM_SCALE = 1.0 / math.sqrt(HEAD_DIM)


def _attn_in(xa, g, scale, shift, w_in, q_gain, k_gain, cos, sin):
    n_rows, d = xa.shape
    qk = Q_WIDTH + KV_WIDTH

    def body(x_ref, g_ref, sc_ref, sh_ref, w_ref, qg_ref, kg_ref, cos_ref, sin_ref, hq_ref, hk_ref,
             q_ref, k_ref, v_ref, z_ref, raw_ref):
        h = _rms_mod(x_ref[...], g_ref[...], sc_ref[0], sh_ref[0])
        proj = _dot(h, w_ref[...])
        q_raw, k_raw = proj[:, :Q_WIDTH], proj[:, Q_WIDTH:qk]
        cos, sin = cos_ref[...], sin_ref[...]
        q = _head_norm_rope(q_raw, qg_ref[...], _lanes(cos, Q_WIDTH), _lanes(sin, Q_WIDTH), hq_ref[...])
        k = _head_norm_rope(k_raw, kg_ref[...], _lanes(cos, KV_WIDTH), _lanes(sin, KV_WIDTH), hk_ref[...])
        q_ref[...] = (q * SM_SCALE).astype(BF16)
        k_ref[...] = k.astype(BF16)
        v_ref[...] = proj[:, qk:qk + KV_WIDTH].astype(BF16)
        z_ref[...] = proj[:, qk + KV_WIDTH:]
        raw_ref[...] = proj[:, :qk]

    return pl.pallas_call(
        body, name="attn_in", grid=(n_rows // ROW_TILE,),
        in_specs=[_rows(d), _full((1, d)), _seg(d), _seg(d), _full(w_in.shape), _full((1, Q_WIDTH)), _full((1, KV_WIDTH)),
                  _rows(128), _rows(128), _full((Q_WIDTH, 128)), _full((KV_WIDTH, 128))],
        out_specs=[_rows_lat(Q_WIDTH), _rows(KV_WIDTH), _rows(KV_WIDTH), _rows(Q_WIDTH), _rows(qk)],
        out_shape=[_sds((n_rows - ROW_TILE, Q_WIDTH), BF16), _sds((n_rows, KV_WIDTH), BF16), _sds((n_rows, KV_WIDTH), BF16),
                   _sds((n_rows, Q_WIDTH)), _sds((n_rows, qk))],
        compiler_params=_cparams(1),
    )(xa, g, scale, shift, w_in, q_gain, k_gain, cos, sin, _head_of_lane(Q_WIDTH), _head_of_lane(KV_WIDTH))


GROUP_WIDTH = KV_REP * HEAD_DIM


def _stack_heads(ref):
    return jnp.concatenate([ref[:, h * HEAD_DIM:(h + 1) * HEAD_DIM] for h in range(KV_REP)], axis=0)


def _unstack_heads(a_t, tq):
    return jnp.concatenate([a_t[:, h * tq:(h + 1) * tq].T for h in range(KV_REP)], axis=1)


def _kv_tile(n_keys):
    return 768 if n_keys % 768 == 0 else 256


def _kv_tile_fwd(n_keys):
    return 1408 if n_keys % 1408 == 0 else _kv_tile(n_keys)


def _q_tile(n_lat):
    return 512 if n_lat % 512 == 0 else 256


def _flash_fwd(q, k, v_t):
    n_lat = q.shape[0]
    tq = _q_tile(n_lat)
    rows = KV_REP * tq
    n_kv, tk, n_q = k.shape[1], k.shape[2], n_lat // tq

    v_rows = v_t.shape[2]

    def body(q_ref, k_ref, vt_ref, o_ref, lse_ref):
        q = _stack_heads(q_ref)

        def step(j, carry):
            m_prev, acc = carry
            s_t = _dot_t1(k_ref[0, j], q)
            m_new = jnp.maximum(m_prev, jnp.max(s_t, axis=0, keepdims=True))
            alpha = jnp.exp(m_prev - m_new)
            p_t = jnp.exp(s_t - m_new)
            return m_new, alpha * acc + _dot(vt_ref[0, j], p_t)

        init = (jnp.full((1, rows), -jnp.inf, F32), jnp.zeros((v_rows, rows), F32))
        m, acc = lax.fori_loop(0, n_kv, step, init)
        l = acc[HEAD_DIM:HEAD_DIM + 1]
        o_ref[...] = _unstack_heads(acc[:HEAD_DIM] / l, tq)
        lse_ref[0, 0] = m + jnp.log(l)

    kv_all = lambda a: pl.BlockSpec((1,) + a.shape[1:], lambda g, i: (g, 0, 0, 0))
    return pl.pallas_call(
        body, name="flash_fwd", grid=(N_KV_HEADS, n_q),
        in_specs=[pl.BlockSpec((tq, GROUP_WIDTH), lambda g, i: (i, g)), kv_all(k), kv_all(v_t)],
        out_specs=[pl.BlockSpec((tq, GROUP_WIDTH), lambda g, i: (i, g)),
                   pl.BlockSpec((1, 1, 1, rows), lambda g, i: (g, i, 0, 0))],
        out_shape=[_sds((n_lat, Q_WIDTH)), _sds((N_KV_HEADS, n_q, 1, rows))],
        compiler_params=_cparams(2),
    )(q, k, v_t)


def _flash_bwd(q, k, k_t, v, do, lse_t, delta_t):
    n_lat = q.shape[0]
    tq = _q_tile(n_lat)
    rows = KV_REP * tq
    n_kv, tk, n_q = k.shape[1], k.shape[2], n_lat // tq

    def body(q_ref, k_ref, kt_ref, v_ref, do_ref, lse_ref, dl_ref, dq_ref, dk_ref, dv_ref):
        _init_acc(pl.program_id(1) == 0, dk_ref, dv_ref)
        q, do = _stack_heads(q_ref), _stack_heads(do_ref)
        lse, delta = lse_ref[0, 0], dl_ref[0, 0]

        def step(j, dq_acc):
            p_t = jnp.exp(_dot_t1(k_ref[0, j], q) - lse)
            dv_ref[0, j] += _dot(p_t, do)
            ds_t = p_t * (_dot_t1(v_ref[0, j], do) - delta)
            dk_ref[0, j] += _dot(ds_t, q)
            return dq_acc + _dot(kt_ref[0, j], ds_t)

        dq = lax.fori_loop(0, n_kv, step, jnp.zeros((HEAD_DIM, rows), F32))
        dq_ref[...] = _unstack_heads(dq, tq)

    qspec = pl.BlockSpec((tq, GROUP_WIDTH), lambda g, i: (i, g))
    rowspec = pl.BlockSpec((1, 1, 1, rows), lambda g, i: (g, i, 0, 0))
    kv_all = lambda a: pl.BlockSpec((1,) + a.shape[1:], lambda g, i: (g, 0, 0, 0))
    return pl.pallas_call(
        body, name="flash_bwd", grid=(N_KV_HEADS, n_q),
        in_specs=[qspec, kv_all(k), kv_all(k_t), kv_all(v), qspec, rowspec, rowspec],
        out_specs=[qspec, kv_all(k), kv_all(k)],
        out_shape=[_sds((n_lat, Q_WIDTH)), _sds(k.shape), _sds(k.shape)],
        compiler_params=_cparams(2),
    )(q, k, k_t, v, do, lse_t, delta_t)


def _to_lane_stacked(a, tq):
    n_lat = a.shape[0]
    a = a.reshape(n_lat // tq, tq, N_KV_HEADS, KV_REP).transpose(2, 0, 3, 1)
    return a.reshape(N_KV_HEADS, n_lat // tq, 1, KV_REP * tq)


def _attn_post_loss(o, z, xa, gate, w_out, final_g, target):
    n_lat, d = target.shape
    e = o.shape[1]
    head_of_lane = (jnp.arange(e)[:, None] // HEAD_DIM == jnp.arange(128)[None, :]).astype(BF16)

    def body(o_ref, z_ref, x_ref, gt_ref, w_ref, fg_ref, tg_ref, hl_ref,
             do_ref, dl_ref, dz_ref, dx_ref, loss_ref, dfg_ref, dgt_ref, dw_ref):
        _init_acc(pl.program_id(0) == 0, loss_ref, dfg_ref, dgt_ref, dw_ref)
        oo, zz, gate_t, fg = o_ref[...], z_ref[...], gt_ref[0], fg_ref[...]
        sz = _sigmoid(zz)
        silu_z = zz * sz
        m = oo * silu_z
        out = _dot(m, w_ref[...])
        x2 = x_ref[...] + gate_t * out
        r = lax.rsqrt(jnp.mean(x2 * x2, axis=-1, keepdims=True) + NORM_EPS)
        n = x2 * r
        err = n * fg - tg_ref[...]
        loss_ref[...] += 0.5 * jnp.sum(jnp.mean(err * err, axis=-1, keepdims=True), axis=0, keepdims=True)
        dy = err * (1.0 / d)
        dfg_ref[...] += _sum0(dy * n)
        dn = dy * fg
        dx2 = r * (dn - n * jnp.mean(dn * n, axis=-1, keepdims=True))
        dx_ref[...] = dx2
        dgt_ref[...] += _sum0(dx2 * out)
        dout = gate_t * dx2
        dw_ref[...] += _dot_t0(m, dout)
        dm = _dot_t1(dout, w_ref[...])
        do = dm * silu_z
        do_ref[...] = do.astype(BF16)
        prod = do * oo
        hi = prod.astype(BF16)
        lo = (prod - hi.astype(F32)).astype(BF16)
        dl_ref[...] = _dot(hi, hl_ref[...]) + _dot(lo, hl_ref[...])
        dz_ref[...] = dm * oo * (sz * (1.0 + zz * (1.0 - sz)))

    return pl.pallas_call(
        body, name="attn_post_loss", grid=(n_lat // ROW_TILE,),
        in_specs=[_rows(e), _rows_skip_ctx(e), _rows_skip_ctx(d), _lat_seg(d), _full(w_out.shape),
                  _full((1, d)), _rows(d), _full((e, 128))],
        out_specs=[_rows(e), _rows(128), _rows(e), _rows(d), _full((1, 1)), _full((1, d)), _full((1, d)), _full(w_out.shape)],
        out_shape=[_sds((n_lat, e), BF16), _sds((n_lat, 128)), _sds((n_lat, e)), _sds((n_lat, d)), _sds((1, 1)), _sds((1, d)),
                   _sds((1, d)), _sds(w_out.shape)],
        compiler_params=_cparams(1),
    )(o, z, xa, gate, w_out, final_g, target, head_of_lane)


def _attn_in_bwd(dq, dk, dv, dz, raw, xa, dx2, g, scale, shift, q_gain, k_gain, cos, sin, w_in):
    n_rows, d = xa.shape
    qk = Q_WIDTH + KV_WIDTH
    n_in = w_in.shape[1]

    def body(dq_ref, dk_ref, dv_ref, dz_ref, raw_ref, x_ref, dx2_ref, g_ref, sc_ref, sh_ref, qg_ref, kg_ref, cos_ref, sin_ref,
             w_ref, hq_ref, hk_ref, dxa_ref, dw_ref, dqg_ref, dkg_ref, dg_ref, dsc_ref, dsh_ref):
        i = pl.program_id(0)
        _init_acc(i == 0, dw_ref, dqg_ref, dkg_ref, dg_ref)
        _init_acc(i <= 1, dsc_ref, dsh_ref)
        is_lat = (i > 0).astype(F32)
        x = x_ref[...]
        h = _rms_mod(x, g_ref[...], sc_ref[0], sh_ref[0])
        cos, sin = cos_ref[...], sin_ref[...]
        raw_t = raw_ref[...]
        dq_raw, dqg = _head_norm_rope_bwd(raw_t[:, :Q_WIDTH], qg_ref[...], _lanes(cos, Q_WIDTH), _lanes(sin, Q_WIDTH),
                                          hq_ref[...], dq_ref[...] * (SM_SCALE * is_lat))
        dk_raw, dkg = _head_norm_rope_bwd(raw_t[:, Q_WIDTH:], kg_ref[...], _lanes(cos, KV_WIDTH), _lanes(sin, KV_WIDTH),
                                          hk_ref[...], dk_ref[...])
        dqg_ref[...] += dqg
        dkg_ref[...] += dkg
        dproj = jnp.concatenate([dq_raw, dk_raw, dv_ref[...], dz_ref[...] * is_lat], axis=1)
        dh = _dot_t1(dproj, w_ref[...])
        dw_ref[...] += _dot_t0(h, dproj)
        dx, dg, dsc, dsh = _rms_mod_bwd(x, g_ref[...], sc_ref[0], dh)
        dg_ref[...] += dg
        dsc_ref[0] += dsc
        dsh_ref[0] += dsh
        dxa_ref[...] = dx + dx2_ref[...] * is_lat

    return pl.pallas_call(
        body, name="attn_in_bwd", grid=(n_rows // ROW_TILE,),
        in_specs=[_rows_lat(Q_WIDTH), _rows(KV_WIDTH), _rows(KV_WIDTH), _rows_lat(Q_WIDTH), _rows(qk), _rows(d), _rows_lat(d),
                  _full((1, d)), _seg(d), _seg(d), _full((1, Q_WIDTH)), _full((1, KV_WIDTH)), _rows(128), _rows(128),
                  _full(w_in.shape), _full((Q_WIDTH, 128)), _full((KV_WIDTH, 128))],
        out_specs=[_rows(d), _full((d, n_in)), _full((1, Q_WIDTH)), _full((1, KV_WIDTH)), _full((1, d)), _seg(d), _seg(d)],
        out_shape=[_sds((n_rows, d)), _sds((d, n_in)), _sds((1, Q_WIDTH)), _sds((1, KV_WIDTH)), _sds((1, d)),
                   _sds((2, 1, d)), _sds((2, 1, d))],
        compiler_params=_cparams(1),
    )(dq, dk, dv, dz, raw, xa, dx2, g, scale, shift, q_gain, k_gain, cos, sin, w_in,
      _head_of_lane(Q_WIDTH), _head_of_lane(KV_WIDTH))


def _heads_major(a, n_heads):
    return a.reshape(a.shape[0], n_heads, HEAD_DIM).transpose(1, 0, 2)


def _tokens_major(a):
    return a.transpose(1, 0, 2).reshape(a.shape[1], a.shape[0] * HEAD_DIM)


def _local_step(x, ctx, target, mods, norm_g, ssm, w_ssm_in, w_glu, b_glu, w_ssm_out, w_attn_in, q_norm, k_norm, w_attn_out,
                final_g, attn_exchange=None, post_exchange=None):
    n_ctx, d = ctx.shape
    assert n_ctx == ROW_TILE
    n_lat = x.shape[0]
    (shift0, scale0, gate0), (shift1, scale1, gate1) = mods
    g0, g1, fg = norm_g[0:1], norm_g[1:2], final_g.reshape(1, d)
    b_glu = b_glu.reshape(1, -1)
    q_gain = jnp.tile(q_norm.reshape(1, HEAD_DIM), (1, N_Q_HEADS))
    k_gain = jnp.tile(k_norm.reshape(1, HEAD_DIM), (1, N_KV_HEADS))
    cos, sin = _rope_tables(n_ctx, n_lat)

    u, z0 = _ssm_in(ctx, x, g0, scale0, shift0, w_ssm_in)
    gather = lambda ex: (ex[0], False) if ex else None
    (y0, y1), saved, gathered = _s5_forward(u, ssm, n_ctx, (gather(attn_exchange), gather(post_exchange)))
    if attn_exchange:
        w_attn_in, w_attn_out = attn_exchange[1](gathered[0])
    if post_exchange:
        w_glu, w_ssm_out = post_exchange[1](gathered[1])
    xa1 = _ssm_post(ctx, x, y0, y1, z0, gate0, w_glu, b_glu, w_ssm_out)

    q, k, v, z1, raw = _attn_in(xa1, g1, scale1, shift1, w_attn_in, q_gain, k_gain, cos, sin)
    tq, tk, tk_fwd = _q_tile(n_lat), _kv_tile(n_ctx + n_lat), _kv_tile_fwd(n_ctx + n_lat)
    k_h, v_h = _heads_major(k, N_KV_HEADS), _heads_major(v, N_KV_HEADS)
    k_b, v_b = k_h.reshape(N_KV_HEADS, -1, tk, HEAD_DIM), v_h.reshape(N_KV_HEADS, -1, tk, HEAD_DIM)
    v_t = v_h.reshape(N_KV_HEADS, -1, tk_fwd, HEAD_DIM).transpose(0, 1, 3, 2)
    v_t_ones = jnp.concatenate([v_t, jnp.ones(v_t.shape[:2] + (16, tk_fwd), BF16)], axis=2)
    o, lse_t = _flash_fwd(q, k_h.reshape(N_KV_HEADS, -1, tk_fwd, HEAD_DIM), v_t_ones)
    do, delta, dz1, dx2, loss, d_fg, d_gate1, d_w_attn_out = _attn_post_loss(
        o, z1, xa1, gate1, w_attn_out, fg, target)

    dq, dk_b, dv_b = _flash_bwd(q, k_b, k_b.transpose(0, 1, 3, 2), v_b, do, lse_t, _to_lane_stacked(delta[:, :N_Q_HEADS], tq))
    keys_major = lambda a: _tokens_major(a.reshape(N_KV_HEADS, -1, HEAD_DIM))
    dxa1, d_w_attn_in, d_qg, d_kg, d_g1, d_scale1, d_shift1 = _attn_in_bwd(
        dq, keys_major(dk_b), keys_major(dv_b), dz1, raw, xa1, dx2, g1, scale1, shift1,
        q_gain, k_gain, cos, sin, w_attn_in)
    dy, dz0, d_gate0, d_w_ssm_out, d_w_glu, d_b_glu = _ssm_post_bwd(
        dxa1, y0, y1, z0, gate0, w_glu, b_glu, w_ssm_out)
    scatter = lambda ex, *g: (ex[2](*g), True) if ex else None
    (du0, du1), d_ssm, parts = _s5_backward(
        u, dy, ssm, saved, n_ctx,
        (scatter(attn_exchange, d_w_attn_in, d_w_attn_out), scatter(post_exchange, d_w_glu, d_w_ssm_out)))
    grad_x, d_w_ssm_in, d_g0, d_scale0, d_shift0 = _ssm_in_bwd(du0, du1, dz0, ctx, x, dxa1, g0, scale0, shift0, w_ssm_in)

    d_gate1_seg = jnp.concatenate([jnp.zeros((1, 1, d), F32), d_gate1.reshape(1, 1, d)], axis=0)
    grads = dict(
        norm_g=jnp.concatenate([d_g0, d_g1], axis=0), ssm_w_in=d_w_ssm_in, ssm=d_ssm, ssm_b_glu=d_b_glu.reshape(-1),
        attn_q_norm=d_qg.reshape(N_Q_HEADS, HEAD_DIM).sum(0), attn_k_norm=d_kg.reshape(N_KV_HEADS, HEAD_DIM).sum(0),
        final_norm_g=d_fg.reshape(-1))
    if attn_exchange:
        grads.update(attn_parts=parts[0])
    else:
        grads.update(attn_w_in=d_w_attn_in, attn_w_out=d_w_attn_out)
    if post_exchange:
        grads.update(post_parts=parts[1])
    else:
        grads.update(ssm_w_glu=d_w_glu, ssm_w_out=d_w_ssm_out)
    d_mods = ((d_shift0, d_scale0, d_gate0), (d_shift1, d_scale1, d_gate1_seg))
    return loss[0, 0], grad_x, grads, d_mods


def _my_index():
    return 4 * lax.axis_index("x") + 2 * lax.axis_index("y") + lax.axis_index("c")


def _peer(k):
    mx, my, mc = lax.axis_index("x"), lax.axis_index("y"), lax.axis_index("c")
    px = 1 - mx if k & 4 else mx
    py = 1 - my if k & 2 else my
    pc = 1 - mc if k & 1 else mc
    return (px, py, pc), 4 * px + 2 * py + pc


HBM_SPEC = pl.BlockSpec(memory_space=pl.ANY)


def _exchange(x, name, all_to_all):
    def body(x_ref, out_ref, send_sems, recv_sems, local_sem):
        _exchange_copies(all_to_all, x_ref, out_ref, send_sems, recv_sems, local_sem, start=True)
        _exchange_copies(all_to_all, x_ref, out_ref, send_sems, recv_sems, local_sem, start=False)

    return pl.pallas_call(
        body, name=name, in_specs=[HBM_SPEC], out_specs=HBM_SPEC,
        out_shape=_exchange_out_shape(x, all_to_all), scratch_shapes=_exchange_semaphores(),
    )(x)


def _all_gather_two_level(x, name):
    def body(x_ref, out_ref, send_sems, recv_sems, local_sem):
        mx, my, mc = lax.axis_index("x"), lax.axis_index("y"), lax.axis_index("c")
        me, sibling = (mx, my, mc), (mx, my, 1 - mc)
        chips = [(1 - mx, my), (mx, 1 - my), (1 - mx, 1 - my)]

        def block(px, py, pc):
            return out_ref.at[4 * px + 2 * py + pc]

        def copy(k, owner, to, src=None):
            return pltpu.make_async_remote_copy(
                src_ref=block(*owner) if src is None else src, dst_ref=block(*owner),
                send_sem=send_sems.at[k], recv_sem=recv_sems.at[k], device_id=to, device_id_type=MESH_IDS)

        mine = pltpu.make_async_copy(x_ref, block(*me), local_sem)
        mine.start()
        first = [copy(0, me, sibling, src=x_ref)] + [copy(1 + j, me, (*chip, mc), src=x_ref) for j, chip in enumerate(chips)]
        for cp in first:
            cp.start()
        passed = [copy(4 + j, (*chip, mc), sibling) for j, chip in enumerate(chips)]
        for j, chip in enumerate(chips):
            copy(1 + j, (*chip, mc), me).wait_recv()
            passed[j].start()
        copy(0, sibling, me).wait_recv()
        for j, chip in enumerate(chips):
            copy(4 + j, (*chip, 1 - mc), me).wait_recv()
        for cp in first + passed:
            cp.wait_send()
        mine.wait()

    return pl.pallas_call(
        body, name=name, in_specs=[HBM_SPEC], out_specs=HBM_SPEC,
        out_shape=_exchange_out_shape(x, False), scratch_shapes=_exchange_semaphores(),
    )(x)


def _exchange_pair(xa, xa_all_to_all, xb, xb_all_to_all, name):
    def body(xa_ref, xb_ref, oa_ref, ob_ref, sa, ra, la, sb, rb, lb):
        for start in (True, False):
            _exchange_copies(xa_all_to_all, xa_ref, oa_ref, sa, ra, la, start=start)
            _exchange_copies(xb_all_to_all, xb_ref, ob_ref, sb, rb, lb, start=start)

    return pl.pallas_call(
        body, name=name, in_specs=[HBM_SPEC, HBM_SPEC], out_specs=[HBM_SPEC, HBM_SPEC],
        out_shape=[_exchange_out_shape(xa, xa_all_to_all), _exchange_out_shape(xb, xb_all_to_all)],
        scratch_shapes=_exchange_semaphores() + _exchange_semaphores(),
    )(xa, xb)


def _exchange_out_shape(x, all_to_all):
    return _sds((N_DEV,) + tuple(x.shape[1:] if all_to_all else x.shape), x.dtype)


def _exchange_semaphores():
    return [pltpu.SemaphoreType.DMA((N_DEV - 1,)), pltpu.SemaphoreType.DMA((N_DEV - 1,)), pltpu.SemaphoreType.DMA]


def _exchange_copies(all_to_all, x_ref, out_ref, send_sems, recv_sems, local_sem, start):
    me = _my_index()
    mine = pltpu.make_async_copy(x_ref.at[me] if all_to_all else x_ref, out_ref.at[me], local_sem)
    if start:
        mine.start()
    for k in range(1, N_DEV):
        peer, peer_idx = _peer(k)
        send = pltpu.make_async_remote_copy(
            src_ref=x_ref.at[peer_idx] if all_to_all else x_ref, dst_ref=out_ref.at[me],
            send_sem=send_sems.at[k - 1], recv_sem=recv_sems.at[k - 1], device_id=peer, device_id_type=MESH_IDS)
        if start:
            send.start()
        else:
            pltpu.make_async_remote_copy(
                src_ref=x_ref.at[me] if all_to_all else x_ref, dst_ref=out_ref.at[peer_idx],
                send_sem=send_sems.at[k - 1], recv_sem=recv_sems.at[k - 1], device_id=peer,
                device_id_type=MESH_IDS).wait_recv()
            send.wait_send()
    if not start:
        mine.wait()


def _ride(rider, first, last, refs):
    @pl.when(first)
    def _():
        _exchange_copies(rider[1], *refs, start=True)

    @pl.when(last)
    def _():
        _exchange_copies(rider[1], *refs, start=False)


MOD_ROWS = 16
CTX_ROW = N_DEV


def _mod_fwd(cond, w_shard, b_cols):
    n_layers, d, cols = w_shard.shape

    def body(c_ref, w_ref, b_ref, o_ref):
        c = c_ref[...]
        s = c * _sigmoid(c)
        for i in range(n_layers):
            o_ref[i] = _dot(s, w_ref[i]) + b_ref[i]

    return pl.pallas_call(
        body, name="mod_fwd", out_shape=_sds((n_layers, MOD_ROWS, cols)),
        compiler_params=pltpu.CompilerParams(vmem_limit_bytes=VMEM_LIMIT),
    )(cond, w_shard, b_cols.reshape(n_layers, 1, cols))


def _mod_bwd(cond, d_lat_cols, d_ctx_cols, w_shard):
    n_layers, d, cols = w_shard.shape

    def body(c_ref, dl_ref, dc_ref, w_ref, dw_ref, dcc_ref):
        c = c_ref[...]
        sg = _sigmoid(c)
        s = c * sg
        d_s = jnp.zeros((MOD_ROWS, d), F32)
        for i in range(n_layers):
            d_ctx = dc_ref[0, i]
            for j in range(1, N_DEV):
                d_ctx = d_ctx + dc_ref[j, i]
            dm = jnp.concatenate([dl_ref[i], d_ctx, jnp.zeros((MOD_ROWS - N_DEV - 1, cols), F32)], axis=0)
            dw_ref[i] = _dot_t0(s, dm)
            d_s = d_s + _dot_t1(dm, w_ref[i])
        d_c = d_s * (sg * (1.0 + c * (1.0 - sg)))
        dcc_ref[...] = d_c[CTX_ROW:CTX_ROW + 1]

    return pl.pallas_call(
        body, name="mod_bwd", out_shape=[_sds((n_layers, d, cols)), _sds((1, d))],
        compiler_params=pltpu.CompilerParams(vmem_limit_bytes=VMEM_LIMIT),
    )(cond, d_lat_cols, d_ctx_cols, w_shard)


ADAM_TILE = 512


def _adamw(w, g_parts, m, v, name):
    n_parts, n_rows, lanes = g_parts.shape
    tile = min(ADAM_TILE, n_rows)
    assert n_rows % tile == 0
    c1 = 1.0 - ADAM_B1 ** ADAM_STEP
    c2 = 1.0 - ADAM_B2 ** ADAM_STEP

    def body(w_ref, g_ref, m_ref, v_ref, go_ref, d_ref, mo_ref, vo_ref):
        g = g_ref[0].astype(F32)
        for p in range(1, n_parts):
            g = g + g_ref[p].astype(F32)
        m_new = ADAM_B1 * m_ref[...] + (1.0 - ADAM_B1) * g
        v_new = ADAM_B2 * v_ref[...] + (1.0 - ADAM_B2) * (g * g)
        go_ref[...] = g
        mo_ref[...] = m_new
        vo_ref[...] = v_new
        d_ref[...] = -ADAM_LR * ((m_new / c1) / (jnp.sqrt(v_new / c2) + ADAM_EPS) + ADAM_WD * w_ref[...])

    row = pl.BlockSpec((tile, lanes), lambda i: (i, 0))
    return pl.pallas_call(
        body, name=name, grid=(n_rows // tile,),
        in_specs=[row, pl.BlockSpec((n_parts, tile, lanes), lambda i: (0, i, 0)), row, row],
        out_specs=[row] * 4, out_shape=[_sds((n_rows, lanes))] * 4,
        compiler_params=_cparams(1),
    )(w, g_parts, m, v)


def _sum_parts(parts):
    n_parts, n_rows, lanes = parts.shape

    def body(p_ref, o_ref):
        acc = p_ref[0]
        for p in range(1, n_parts):
            acc = acc + p_ref[p]
        o_ref[...] = acc

    return pl.pallas_call(body, name="sum_parts", out_shape=_sds((n_rows, lanes)))(parts)


def _pack(arrays, row_multiple):
    parts = []
    for a in arrays:
        flat = a.reshape(-1)
        parts.append(jnp.pad(flat, (0, (-flat.shape[0]) % 1024)))
    flat = jnp.concatenate(parts)
    flat = jnp.pad(flat, (0, (-flat.shape[0]) % (row_multiple * 128)))
    return flat.reshape(-1, 128)


def _unpack(packed, shapes):
    flat = packed.reshape(-1)
    out, pos = [], 0
    for s in shapes:
        n = math.prod(s)
        out.append(flat[pos:pos + n].reshape(s))
        pos += n + (-n) % 1024
    return out


WEIGHT_NAMES = ['c_ctx', 'w_mod', 'b_mod', 'norm_g', 'ssm_w_in', 'ssm_a_re', 'ssm_a_im', 'ssm_log_dt', 'ssm_b_re', 'ssm_b_im',
                'ssm_c_re', 'ssm_c_im', 'ssm_d', 'ssm_w_glu', 'ssm_b_glu', 'ssm_w_out', 'attn_w_in', 'attn_q_norm',
                'attn_k_norm', 'attn_w_out', 'final_norm_g']
FIRST_SHARDED = ['ssm_w_in']
POST_SHARDED = ['ssm_w_glu', 'ssm_w_out']
ATTN_SHARDED = ['attn_w_in', 'attn_w_out']
SHARDED = FIRST_SHARDED + POST_SHARDED + ATTN_SHARDED
COLUMN_SHARDED = ('ssm_w_in', 'attn_w_in')
REPLICATED = ['c_ctx', 'b_mod', 'norm_g', 'ssm_a_re', 'ssm_a_im', 'ssm_log_dt', 'ssm_b_re', 'ssm_b_im', 'ssm_c_re', 'ssm_c_im',
              'ssm_d', 'ssm_b_glu', 'attn_q_norm', 'attn_k_norm', 'final_norm_g']
SSM_NAMES = ['ssm_a_re', 'ssm_a_im', 'ssm_log_dt', 'ssm_b_re', 'ssm_b_im', 'ssm_c_re', 'ssm_c_im', 'ssm_d']


def _full_from_shards(gathered, name, shard_shape):
    rows, cols = shard_shape
    w = gathered.reshape(N_DEV, rows, cols)
    if name in COLUMN_SHARDED:
        return w.transpose(1, 0, 2).reshape(rows, N_DEV * cols)
    return w.reshape(N_DEV * rows, cols)


def _shards_from_full(g, name):
    if name in COLUMN_SHARDED:
        rows, cols = g.shape
        g = g.reshape(rows, N_DEV, cols // N_DEV).transpose(1, 0, 2)
    return g.reshape(N_DEV, -1, 128)


def kernel(x, c, ctx, c_ctx, w_mod, b_mod, norm_g, ssm_w_in, ssm_a_re, ssm_a_im, ssm_log_dt, ssm_b_re, ssm_b_im, ssm_c_re, ssm_c_im, ssm_d, ssm_w_glu, ssm_b_glu, ssm_w_out, attn_w_in, attn_q_norm, attn_k_norm, attn_w_out, final_norm_g, loss_target, m_c_ctx, m_w_mod, m_b_mod, m_norm_g, m_ssm_w_in, m_ssm_a_re, m_ssm_a_im, m_ssm_log_dt, m_ssm_b_re, m_ssm_b_im, m_ssm_c_re, m_ssm_c_im, m_ssm_d, m_ssm_w_glu, m_ssm_b_glu, m_ssm_w_out, m_attn_w_in, m_attn_q_norm, m_attn_k_norm, m_attn_w_out, m_final_norm_g, v_c_ctx, v_w_mod, v_b_mod, v_norm_g, v_ssm_w_in, v_ssm_a_re, v_ssm_a_im, v_ssm_log_dt, v_ssm_b_re, v_ssm_b_im, v_ssm_c_re, v_ssm_c_im, v_ssm_d, v_ssm_w_glu, v_ssm_b_glu, v_ssm_w_out, v_attn_w_in, v_attn_q_norm, v_attn_k_norm, v_attn_w_out, v_final_norm_g):
    env = dict(locals())
    weights = {n: env[n] for n in WEIGHT_NAMES}
    mom_m = {n: env["m_" + n] for n in WEIGHT_NAMES}
    mom_v = {n: env["v_" + n] for n in WEIGHT_NAMES}
    d = D_MODEL
    me = _my_index()
    mod_cols = w_mod.shape[-1]

    shard_shapes = {n: weights[n].shape[1:] for n in SHARDED}
    pack_shards = lambda names: _pack([weights[n] for n in names], 1).astype(BF16)

    def unpack_full(gathered, names):
        full, pos = [], 0
        for n in names:
            rows = math.prod(shard_shapes[n]) // 128
            full.append(_full_from_shards(gathered[:, pos:pos + rows], n, shard_shapes[n]))
            pos += rows
        return full

    def exchange_of(names):
        return (pack_shards(names), lambda gathered: unpack_full(gathered, names),
                lambda *grads: jnp.concatenate([_shards_from_full(t, n) for t, n in zip(grads, names)], axis=1).astype(BF16))

    c_all = _exchange(c.reshape(8, d // 8), "gather_c", False)
    w_first = _all_gather_two_level(pack_shards(FIRST_SHARDED), "gather_ssm_w_in")
    cond = jnp.concatenate([c_all.reshape(N_DEV, d), c_ctx.reshape(1, d), jnp.zeros((MOD_ROWS - N_DEV - 1, d), F32)], axis=0)
    (w_ssm_in,) = unpack_full(w_first, FIRST_SHARDED)

    b_cols = lax.dynamic_slice(b_mod, (0, me * mod_cols), (2, mod_cols))
    mod_shard = _mod_fwd(cond, w_mod, b_cols)
    mod_all = _exchange(mod_shard.reshape(2 * MOD_ROWS, mod_cols), "gather_mod", False)
    mod_full = mod_all.reshape(N_DEV, 2, MOD_ROWS, mod_cols).transpose(1, 2, 0, 3).reshape(2, MOD_ROWS, 3 * d)
    lat_rows = lax.dynamic_slice(mod_full, (0, me, 0), (2, 1, 3 * d))
    mods = []
    for i in range(2):
        seg = jnp.stack([mod_full[i, CTX_ROW:CTX_ROW + 1], lat_rows[i]])
        mods.append((seg[:, :, :d], seg[:, :, d:2 * d], seg[:, :, 2 * d:]))

    ssm = tuple(weights[n][0] for n in SSM_NAMES)
    loss, grad_x, g, d_mods = _local_step(
        x[0], ctx[0], loss_target[0], mods, norm_g, ssm, w_ssm_in, None, ssm_b_glu[0], None,
        None, attn_q_norm[0], attn_k_norm[0], None, final_norm_g, exchange_of(ATTN_SHARDED), exchange_of(POST_SHARDED))

    d_rows = jnp.stack([jnp.concatenate(dm, axis=-1) for dm in d_mods])
    d_rows = jnp.concatenate([d_rows.reshape(4, 3 * d), jnp.zeros((4, 3 * d), F32)], axis=0)
    d_all, first_parts = _exchange_pair(d_rows, False, _shards_from_full(g['ssm_w_in'], 'ssm_w_in').astype(BF16), True,
                                        "gather_dmod_and_scatter_ssm_w_in_grads")
    d_all = d_all[:, :4].reshape(N_DEV, 2, 2, 3 * d)
    d_all = lax.dynamic_slice(d_all, (0, 0, 0, me * mod_cols), (N_DEV, 2, 2, mod_cols))
    d_w_mod, d_c_ctx = _mod_bwd(cond, d_all[:, :, 1].transpose(1, 0, 2), d_all[:, :, 0:1], w_mod)
    d_b_mod = jnp.stack([jnp.concatenate([t[0] + t[1] for t in dm], axis=-1).reshape(3 * d) for dm in d_mods])

    parts_of = {}
    for names, parts in ((FIRST_SHARDED, first_parts), (POST_SHARDED, g['post_parts']), (ATTN_SHARDED, g['attn_parts'])):
        pos = 0
        for n in names:
            rows = math.prod(shard_shapes[n]) // 128
            parts_of[n] = parts[:, pos:pos + rows].reshape((N_DEV,) + shard_shapes[n])
            pos += rows

    def update(n, g_parts):
        as_2d = lambda t: t.reshape(-1, t.shape[-1])
        res = _adamw(as_2d(weights[n]), g_parts, as_2d(mom_m[n]), as_2d(mom_v[n]), "adamw_" + n)
        return [t.reshape(weights[n].shape) for t in res]

    big = {n: update(n, parts_of[n]) for n in SHARDED}
    big['w_mod'] = update('w_mod', d_w_mod.reshape(1, -1, mod_cols))

    small = dict(zip(SSM_NAMES, g['ssm']))
    small.update(c_ctx=d_c_ctx, b_mod=d_b_mod, norm_g=g['norm_g'], ssm_b_glu=g['ssm_b_glu'], attn_q_norm=g['attn_q_norm'],
                 attn_k_norm=g['attn_k_norm'], final_norm_g=g['final_norm_g'])
    pack_small = lambda t, last: _pack([t[n] for n in REPLICATED] + [last], ADAM_TILE)
    no_weight = jnp.zeros((1,), F32)
    g_small = pack_small(small, loss.reshape(1))
    slices = _exchange(g_small.reshape(N_DEV, -1, 128), "scatter_small_grads", True)
    g_small = _exchange(_sum_parts(slices), "gather_small_grads", False).reshape(1, -1, 128)
    rep = _adamw(pack_small(weights, no_weight), g_small, pack_small(mom_m, no_weight), pack_small(mom_v, no_weight),
                 "adamw_replicated")
    rep = [_unpack(t, [weights[n].shape for n in REPLICATED] + [(1,)]) for t in rep]
    loss = rep[0][-1][0]

    results = []
    for kind in range(4):
        by_name = {n: res[kind] for n, res in big.items()}
        by_name.update(zip(REPLICATED, rep[kind]))
        results.extend(by_name[n] for n in WEIGHT_NAMES)
    return (loss, grad_x[None], *results)
```

```python
import functools
import math

import jax
import jax.numpy as jnp
from jax import lax
from jax.experimental import pallas as pl
from jax.experimental.pallas import tpu as pltpu

F32 = jnp.float32
BF16 = jnp.bfloat16

N_DEV = 8
D_MODEL = 1024
NORM_EPS = 1e-6
SSM_GROUP = 16
SSM_GROUPS = 64
SSM_STATE = 64
GROUPS_PER_BLOCK = 8
N_BLOCKS = SSM_GROUPS // GROUPS_PER_BLOCK
HALF = GROUPS_PER_BLOCK * SSM_STATE
HEAD_DIM = 64
N_Q_HEADS = 16
N_KV_HEADS = 4
KV_REP = N_Q_HEADS // N_KV_HEADS
KV_WIDTH = N_KV_HEADS * HEAD_DIM
GRID_W = 64
ROPE_THETA = 10000.0
ADAM_LR, ADAM_B1, ADAM_B2, ADAM_EPS, ADAM_WD, ADAM_STEP = 0.001, 0.9, 0.999, 1e-08, 0.01, 10

ROW_TILE = 256
SCAN_CHUNK = 256
VMEM_LIMIT = 56 * 1024 * 1024
MESH_IDS = pl.DeviceIdType.MESH


def _cparams(n_axes):
    return pltpu.CompilerParams(dimension_semantics=("arbitrary",) * n_axes, vmem_limit_bytes=VMEM_LIMIT)


def _dot(a, b):
    return jnp.dot(a.astype(BF16), b.astype(BF16), preferred_element_type=F32)


def _dot_t0(a, b):
    return lax.dot_general(a.astype(BF16), b.astype(BF16), (((0,), (0,)), ((), ())), preferred_element_type=F32)


def _dot_t1(a, b):
    return lax.dot_general(a.astype(BF16), b.astype(BF16), (((1,), (1,)), ((), ())), preferred_element_type=F32)


def _s5_prep(a_re, a_im, log_dt, b_re, b_im):
    dt = jnp.exp(log_dt)[:, None]
    ldr, ldi = a_re * dt, a_im * dt
    mag = jnp.exp(ldr)
    abar_re, abar_im = mag * jnp.cos(ldi), mag * jnp.sin(ldi)
    den = a_re * a_re + a_im * a_im
    num_re, num_im = abar_re - 1.0, abar_im
    coef_re = (num_re * a_re + num_im * a_im) / den
    coef_im = (num_im * a_re - num_re * a_im) / den
    bbar_re = coef_re[..., None] * b_re - coef_im[..., None] * b_im
    bbar_im = coef_re[..., None] * b_im + coef_im[..., None] * b_re
    return abar_re, abar_im, bbar_re, bbar_im


def _s5_blocks(abar_re, abar_im, bbar_re, bbar_im, c_re, c_im):
    eye = jnp.eye(GROUPS_PER_BLOCK, dtype=F32)
    bb = jnp.stack([bbar_re, bbar_im]).reshape(2, N_BLOCKS, GROUPS_PER_BLOCK, SSM_STATE, SSM_GROUP)
    b_blk = jnp.einsum('rqgph,gk->qghrkp', bb, eye).reshape(N_BLOCKS, 128, 2 * HALF)
    cc = jnp.stack([c_re, -c_im]).reshape(2, N_BLOCKS, GROUPS_PER_BLOCK, SSM_GROUP, SSM_STATE)
    c_blk = jnp.einsum('rqghp,gk->qrgpkh', cc, eye).reshape(N_BLOCKS, 2 * HALF, 128)
    abar = jnp.stack([abar_re.reshape(N_BLOCKS, HALF), abar_im.reshape(N_BLOCKS, HALF)])
    return abar, b_blk, c_blk


def _s5_unblock(d_b_blk, d_ct_blk):
    db = d_b_blk.reshape(N_BLOCKS, GROUPS_PER_BLOCK, SSM_GROUP, 2, GROUPS_PER_BLOCK, SSM_STATE)
    db = jnp.einsum('qghrgp->rqgph', db).reshape(2, SSM_GROUPS, SSM_STATE, SSM_GROUP)
    dc = d_ct_blk.reshape(N_BLOCKS, GROUPS_PER_BLOCK, SSM_GROUP, 2, GROUPS_PER_BLOCK, SSM_STATE)
    dc = jnp.einsum('qghrgp->rqghp', dc).reshape(2, SSM_GROUPS, SSM_GROUP, SSM_STATE)
    return db[0], db[1], dc[0], -dc[1]


def _scan_chunk_of_step(j, n_chunks, n_ctx_chunks, reverse):
    if not reverse:
        return j
    return jnp.where(j < n_ctx_chunks, n_ctx_chunks - 1 - j, n_chunks - 1 - j + n_ctx_chunks)


LANE_TILES = 2 * HALF // 128
RE_TILES = HALF // 128


def _tiles(v):
    return [v[:, l * 128:(l + 1) * 128] for l in range(v.shape[1] // 128)]


def _scatter_steps(s_ref, q, x):
    for l in range(LANE_TILES):
        s_ref[l, pl.ds(q, x.shape[0], stride=N_BLOCKS), :] = x[:, l * 128:(l + 1) * 128]


def _gather_steps(s_ref, q, n_steps):
    return jnp.concatenate([s_ref[l, pl.ds(q, n_steps, stride=N_BLOCKS), :] for l in range(LANE_TILES)], axis=1)


def _load_step(s_ref, t):
    row = pl.multiple_of(t * N_BLOCKS, N_BLOCKS)
    return [s_ref[l, pl.ds(row, N_BLOCKS), :] for l in range(LANE_TILES)]


def _store_step(s_ref, t, tiles):
    row = pl.multiple_of(t * N_BLOCKS, N_BLOCKS)
    for l in range(LANE_TILES):
        s_ref[l, pl.ds(row, N_BLOCKS), :] = tiles[l]


SCAN_UNROLL = 8
ADJOINT_UNROLL = 15


def _unrolled_loop(n_steps, unroll, step, carry):
    assert n_steps % unroll == 0

    def steps(i, c):
        for r in range(unroll):
            c = step(unroll * i + r, c)
        return c

    return lax.fori_loop(0, n_steps // unroll, steps, carry)


def _cmul_add(a, h, x, conj):
    re, im = [], []
    for l in range(RE_TILES):
        ar, ai, hr, hi = a[l], a[RE_TILES + l], h[l], h[RE_TILES + l]
        if conj:
            re.append(ar * hr + ai * hi + x[l])
            im.append(ar * hi - ai * hr + x[RE_TILES + l])
        else:
            re.append(ar * hr - ai * hi + x[l])
            im.append(ar * hi + ai * hr + x[RE_TILES + l])
    return re + im


def _s5_scan_fwd(u, abar, b_blk, c_blk, d_skip, n_ctx, reverse, rider=None):
    n_rows, width = u.shape
    tc = SCAN_CHUNK
    n_chunks, n_ctx_chunks = n_rows // tc, n_ctx // tc
    with_skip = d_skip is not None

    def body(*refs):
        if rider is not None:
            x_ref, ride_out, sems = refs[4 + with_skip], refs[7 + with_skip], refs[-3:]
            _ride(rider, pl.program_id(0) == 0, pl.program_id(0) == n_chunks - 1, (x_ref, ride_out) + tuple(sems))
            refs = refs[:4 + with_skip] + refs[5 + with_skip:7 + with_skip] + refs[8 + with_skip:-3]
        if with_skip:
            u_ref, a_ref, b_ref, c_ref, d_ref, y_ref, hb_ref, s_ref, h_ref = refs
        else:
            u_ref, a_ref, b_ref, c_ref, y_ref, hb_ref, s_ref, h_ref = refs
        j = pl.program_id(0)

        @pl.when(j == 0)
        def _():
            h_ref[...] = jnp.zeros_like(h_ref)

        hb_ref[0] = h_ref[...]
        for q in range(N_BLOCKS):
            _scatter_steps(s_ref, q, _dot(u_ref[:, q * 128:(q + 1) * 128], b_ref[q]))
        a = _tiles(a_ref[0]) + _tiles(a_ref[1])

        def step(s, h):
            t = tc - 1 - s if reverse else s
            h = _cmul_add(a, h, _load_step(s_ref, t), conj=False)
            _store_step(s_ref, t, h)
            return h

        h = _unrolled_loop(tc, SCAN_UNROLL, step, _tiles(h_ref[...]))
        h_ref[...] = jnp.concatenate(h, axis=1)
        for q in range(N_BLOCKS):
            yq = _dot(_gather_steps(s_ref, q, tc), c_ref[q])
            if with_skip:
                yq = yq + d_ref[:, q * 128:(q + 1) * 128] * u_ref[:, q * 128:(q + 1) * 128]
            y_ref[:, q * 128:(q + 1) * 128] = yq

    chunk = functools.partial(_scan_chunk_of_step, n_chunks=n_chunks, n_ctx_chunks=n_ctx_chunks, reverse=reverse)
    full3 = lambda j: (0, 0, 0)
    in_specs = [pl.BlockSpec((tc, width), lambda j: (chunk(j), 0)),
                pl.BlockSpec((2, N_BLOCKS, HALF), full3),
                pl.BlockSpec((N_BLOCKS, 128, 2 * HALF), full3),
                pl.BlockSpec((N_BLOCKS, 2 * HALF, 128), full3)]
    args = [u, abar, b_blk.astype(BF16), c_blk.astype(BF16)]
    if with_skip:
        in_specs.append(pl.BlockSpec((1, width), lambda j: (0, 0)))
        args.append(d_skip.reshape(1, width))
    out_specs = [pl.BlockSpec((tc, width), lambda j: (chunk(j), 0)),
                 pl.BlockSpec((1, N_BLOCKS, 2 * HALF), lambda j: (chunk(j), 0, 0))]
    out_shape = [_sds((n_rows, width)), _sds((n_chunks, N_BLOCKS, 2 * HALF))]
    scratch = [pltpu.VMEM((LANE_TILES, tc * N_BLOCKS, 128), F32), pltpu.VMEM((N_BLOCKS, 2 * HALF), F32)]
    if rider is not None:
        in_specs.append(HBM_SPEC)
        args.append(rider[0])
        out_specs.append(HBM_SPEC)
        out_shape.append(_exchange_out_shape(*rider))
        scratch += _exchange_semaphores()
    return pl.pallas_call(
        body, name="s5_scan_fwd_rev" if reverse else "s5_scan_fwd",
        grid=(n_chunks,), in_specs=in_specs, out_specs=out_specs, out_shape=out_shape, scratch_shapes=scratch,
        compiler_params=_cparams(1),
    )(*args)


def _s5_scan_bwd(u, dy, hb, abar, b_blk, c_blk, d_skip, n_ctx, reverse, rider=None):
    n_rows, width = u.shape
    tc = SCAN_CHUNK
    n_chunks, n_ctx_chunks = n_rows // tc, n_ctx // tc
    with_skip = d_skip is not None
    n_in, n_out = 7 + with_skip, 4 + with_skip

    def body(*refs):
        if rider is not None:
            x_ref, ride_out, sems = refs[n_in], refs[n_in + 1 + n_out], refs[-3:]
            _ride(rider, pl.program_id(0) == 0, pl.program_id(0) == n_chunks - 1, (x_ref, ride_out) + tuple(sems))
            refs = refs[:n_in] + refs[n_in + 1:n_in + 1 + n_out] + refs[n_in + 2 + n_out:-3]
        if with_skip:
            (u_ref, dy_ref, hb_ref, a_ref, b_ref, bt_ref, ct_ref, d_ref,
             du_ref, da_ref, db_ref, dct_ref, dd_ref, sh_ref, sg_ref, g_ref) = refs
        else:
            (u_ref, dy_ref, hb_ref, a_ref, b_ref, bt_ref, ct_ref,
             du_ref, da_ref, db_ref, dct_ref, sh_ref, sg_ref, g_ref) = refs
        j = pl.program_id(0)

        @pl.when(j == 0)
        def _():
            g_ref[...] = jnp.zeros_like(g_ref)
            da_ref[...] = jnp.zeros_like(da_ref)
            db_ref[...] = jnp.zeros_like(db_ref)
            dct_ref[...] = jnp.zeros_like(dct_ref)
            if with_skip:
                dd_ref[...] = jnp.zeros_like(dd_ref)

        for q in range(N_BLOCKS):
            _scatter_steps(sh_ref, q, _dot(u_ref[:, q * 128:(q + 1) * 128], b_ref[q]))
            _scatter_steps(sg_ref, q, _dot(dy_ref[:, q * 128:(q + 1) * 128], ct_ref[q]))
        a = _tiles(a_ref[0]) + _tiles(a_ref[1])
        time_of = (lambda s: tc - 1 - s) if reverse else (lambda s: s)

        def fwd_step(s, h):
            h = _cmul_add(a, h, _load_step(sh_ref, time_of(s)), conj=False)
            _store_step(sh_ref, time_of(s), h)
            return h

        h0 = _tiles(hb_ref[0])
        _unrolled_loop(tc, SCAN_UNROLL, fwd_step, h0)

        def adj(t, h_prev, carry):
            g, da = carry
            g = _cmul_add(a, g, _load_step(sg_ref, t), conj=True)
            _store_step(sg_ref, t, g)
            da_re = [da[l] + g[l] * h_prev[l] + g[RE_TILES + l] * h_prev[RE_TILES + l] for l in range(RE_TILES)]
            da_im = [da[RE_TILES + l] + g[RE_TILES + l] * h_prev[l] - g[l] * h_prev[RE_TILES + l] for l in range(RE_TILES)]
            return g, da_re + da_im

        def bwd_step(i, carry):
            s = tc - 1 - i
            return adj(time_of(s), _load_step(sh_ref, time_of(s - 1)), carry)

        carry = (_tiles(g_ref[...]), _tiles(da_ref[0]) + _tiles(da_ref[1]))
        carry = _unrolled_loop(tc - 1, ADJOINT_UNROLL, bwd_step, carry)
        g, da = adj(time_of(0), h0, carry)
        g_ref[...] = jnp.concatenate(g, axis=1)
        da_ref[0] = jnp.concatenate(da[:RE_TILES], axis=1)
        da_ref[1] = jnp.concatenate(da[RE_TILES:], axis=1)

        for q in range(N_BLOCKS):
            cols = slice(q * 128, (q + 1) * 128)
            uq, dyq = u_ref[:, cols], dy_ref[:, cols]
            gq = _gather_steps(sg_ref, q, tc)
            duq = _dot(gq, bt_ref[q])
            if with_skip:
                duq = duq + d_ref[:, cols] * dyq
                dd_ref[:, cols] += jnp.sum(dyq * uq, axis=0, keepdims=True)
            du_ref[:, cols] = duq
            db_ref[q] += _dot_t0(uq, gq)
            dct_ref[q] += _dot_t0(dyq, _gather_steps(sh_ref, q, tc))

    def chunk(j):
        return _scan_chunk_of_step(n_chunks - 1 - j, n_chunks, n_ctx_chunks, reverse)

    full2 = lambda j: (0, 0)
    full3 = lambda j: (0, 0, 0)
    row = pl.BlockSpec((tc, width), lambda j: (chunk(j), 0))
    in_specs = [row, row,
                pl.BlockSpec((1, N_BLOCKS, 2 * HALF), lambda j: (chunk(j), 0, 0)),
                pl.BlockSpec((2, N_BLOCKS, HALF), full3),
                pl.BlockSpec((N_BLOCKS, 128, 2 * HALF), full3),
                pl.BlockSpec((N_BLOCKS, 2 * HALF, 128), full3),
                pl.BlockSpec((N_BLOCKS, 128, 2 * HALF), full3)]
    args = [u, dy, hb, abar, b_blk.astype(BF16), jnp.swapaxes(b_blk, 1, 2).astype(BF16),
            jnp.swapaxes(c_blk, 1, 2).astype(BF16)]
    out_specs = [row,
                 pl.BlockSpec((2, N_BLOCKS, HALF), full3),
                 pl.BlockSpec((N_BLOCKS, 128, 2 * HALF), full3),
                 pl.BlockSpec((N_BLOCKS, 128, 2 * HALF), full3)]
    out_shape = [jax.ShapeDtypeStruct((n_rows, width), F32),
                 jax.ShapeDtypeStruct((2, N_BLOCKS, HALF), F32),
                 jax.ShapeDtypeStruct((N_BLOCKS, 128, 2 * HALF), F32),
                 jax.ShapeDtypeStruct((N_BLOCKS, 128, 2 * HALF), F32)]
    if with_skip:
        in_specs.append(pl.BlockSpec((1, width), full2))
        args.append(d_skip.reshape(1, width))
        out_specs.append(pl.BlockSpec((1, width), full2))
        out_shape.append(jax.ShapeDtypeStruct((1, width), F32))
    scratch = [pltpu.VMEM((LANE_TILES, tc * N_BLOCKS, 128), F32), pltpu.VMEM((LANE_TILES, tc * N_BLOCKS, 128), F32),
               pltpu.VMEM((N_BLOCKS, 2 * HALF), F32)]
    if rider is not None:
        in_specs.append(HBM_SPEC)
        args.append(rider[0])
        out_specs.append(HBM_SPEC)
        out_shape.append(_exchange_out_shape(*rider))
        scratch += _exchange_semaphores()
    return pl.pallas_call(
        body, name="s5_scan_bwd_rev" if reverse else "s5_scan_bwd",
        grid=(n_chunks,), in_specs=in_specs, out_specs=out_specs, out_shape=out_shape, scratch_shapes=scratch,
        compiler_params=_cparams(1),
    )(*args)


def _s5_dir_params(d, a_re, a_im, log_dt, b_re, b_im):
    return a_re[d], a_im[d], log_dt[d], b_re[d], b_im[d]


def _s5_forward(u, ssm, n_ctx, riders=(None, None)):
    a_re, a_im, log_dt, b_re, b_im, c_re, c_im, d_skip = ssm
    outs, saved, carried = [], [], [None, None]
    for d in range(2):
        prep = _s5_prep(*_s5_dir_params(d, a_re, a_im, log_dt, b_re, b_im))
        abar, b_blk, c_blk = _s5_blocks(*prep, c_re[d], c_im[d])
        res = _s5_scan_fwd(u, abar, b_blk, c_blk, d_skip if d == 0 else None, n_ctx, reverse=(d == 1), rider=riders[d])
        if riders[d] is not None:
            carried[d] = res[2]
        outs.append(res[0])
        saved.append((res[1], abar, b_blk, c_blk))
    return outs, saved, carried


def _s5_backward(u, dy, ssm, saved, n_ctx, riders=(None, None)):
    a_re, a_im, log_dt, b_re, b_im, c_re, c_im, d_skip = ssm
    dus, grads = [], [[] for _ in range(7)]
    d_d, carried = None, [None, None]
    for d in range(2):
        hb, abar, b_blk, c_blk = saved[d]
        res = _s5_scan_bwd(u, dy, hb, abar, b_blk, c_blk, d_skip if d == 0 else None, n_ctx, reverse=(d == 1),
                           rider=riders[d])
        if riders[d] is not None:
            carried[d], res = res[-1], res[:-1]
        if d == 0:
            du, d_abar, d_b_blk, d_ct_blk, d_d = res
        else:
            du, d_abar, d_b_blk, d_ct_blk = res
        dus.append(du)
        dbb_re, dbb_im, dc_re, dc_im = _s5_unblock(d_b_blk, d_ct_blk)
        _, vjp = jax.vjp(_s5_prep, *_s5_dir_params(d, a_re, a_im, log_dt, b_re, b_im))
        shape = (SSM_GROUPS, SSM_STATE)
        g5 = vjp((d_abar[0].reshape(shape), d_abar[1].reshape(shape), dbb_re, dbb_im))
        for k, g in enumerate(tuple(g5) + (dc_re, dc_im)):
            grads[k].append(g)
    grads = [jnp.stack(g) for g in grads]
    return dus, grads + [d_d.reshape(-1)], carried


INV_SQRT2 = 0.7071067811865476
INV_SQRT_2PI = 0.3989422804014327


def _rows(cols):
    return pl.BlockSpec((ROW_TILE, cols), lambda i: (i, 0))


def _rows_skip_ctx(cols):
    return pl.BlockSpec((ROW_TILE, cols), lambda i: (i + 1, 0))


def _rows_lat(cols):
    return pl.BlockSpec((ROW_TILE, cols), lambda i: (jnp.maximum(i - 1, 0), 0))


def _full(shape):
    nd = len(shape)
    return pl.BlockSpec(shape, lambda i: (0,) * nd)


def _seg(cols):
    return pl.BlockSpec((1, 1, cols), lambda i: (jnp.minimum(i, 1), 0, 0))


def _lat_seg(cols):
    return pl.BlockSpec((1, 1, cols), lambda i: (1, 0, 0))


def _sds(shape, dtype=F32):
    return jax.ShapeDtypeStruct(shape, dtype)


def _sum0(x):
    return jnp.sum(x, axis=0, keepdims=True)


def _sigmoid(x):
    return jax.nn.sigmoid(x)


def _rms_mod(x, g, scale, shift):
    r = lax.rsqrt(jnp.mean(x * x, axis=-1, keepdims=True) + NORM_EPS)
    return (x * r * g) * (1.0 + scale) + shift


def _rms_mod_bwd(x, g, scale, dh):
    r = lax.rsqrt(jnp.mean(x * x, axis=-1, keepdims=True) + NORM_EPS)
    n = x * r
    dyg = dh * (1.0 + scale)
    dn = dyg * g
    dx = r * (dn - n * jnp.mean(dn * n, axis=-1, keepdims=True))
    return dx, _sum0(dyg * n), _sum0(dh * (n * g)), _sum0(dh)


def _head_of_lane(width):
    return (jnp.arange(width)[:, None] // HEAD_DIM == jnp.arange(128)[None, :]).astype(BF16)


def _split_dot(t, w, transposed):
    hi = t.astype(BF16)
    lo = (t - hi.astype(F32)).astype(BF16)
    f = _dot_t1 if transposed else _dot
    return f(hi, w) + f(lo, w)


def _head_sums(t, hl):
    return _split_dot(_split_dot(t, hl, False), hl, True)


def _rope_partner(x):
    n = x.shape[1]
    lane = lax.broadcasted_iota(jnp.int32, x.shape, 1)
    return jnp.where((lane & 16) == 0, pltpu.roll(x, n - 16, 1), pltpu.roll(x, 16, 1))


def _lanes(tab, width):
    return jnp.tile(tab, (1, width // tab.shape[1]))


def _head_norm_rope(x, gain, cos, sin, hl):
    r = lax.rsqrt(_head_sums(x * x, hl) * (1.0 / HEAD_DIM) + NORM_EPS)
    y = x * r * gain
    return y * cos + _rope_partner(y) * sin


def _head_norm_rope_bwd(x, gain, cos, sin, hl, dout):
    dy = dout * cos + _rope_partner(dout * sin)
    r = lax.rsqrt(_head_sums(x * x, hl) * (1.0 / HEAD_DIM) + NORM_EPS)
    n = x * r
    dn = dy * gain
    dx = r * (dn - n * (_head_sums(dn * n, hl) * (1.0 / HEAD_DIM)))
    return dx, _sum0(dy * n)


def _rope_tables(n_ctx, n_lat):
    t = jnp.arange(n_lat)
    pos = jnp.stack([(t // GRID_W).astype(F32), (t % GRID_W).astype(F32)], axis=1)
    n_freq = HEAD_DIM // 4
    freqs = ROPE_THETA ** (-jnp.arange(n_freq, dtype=F32) / n_freq)
    ang = pos[:, :, None] * freqs[None, None, :]
    cos = jnp.repeat(jnp.cos(ang)[:, :, None, :], 2, axis=2).reshape(n_lat, HEAD_DIM)
    sin = jnp.sin(ang)
    sin = jnp.stack([-sin, sin], axis=2).reshape(n_lat, HEAD_DIM)
    cos = jnp.concatenate([jnp.ones((n_ctx, HEAD_DIM), F32), cos], axis=0)
    sin = jnp.concatenate([jnp.zeros((n_ctx, HEAD_DIM), F32), sin], axis=0)
    return jnp.tile(cos, (1, 2)), jnp.tile(sin, (1, 2))


def _token_tile(ctx_ref, x_ref):
    return jnp.where(pl.program_id(0) == 0, ctx_ref[...], x_ref[...])


def _ssm_in(ctx, x, g, scale, shift, w_in):
    n_rows, d = ctx.shape[0] + x.shape[0], x.shape[1]
    e = w_in.shape[1] // 2

    def body(c_ref, x_ref, g_ref, sc_ref, sh_ref, w_ref, u_ref, z_ref):
        h = _rms_mod(_token_tile(c_ref, x_ref), g_ref[...], sc_ref[0], sh_ref[0])
        proj = _dot(h, w_ref[...])
        u_ref[...] = proj[:, :e]
        z_ref[...] = proj[:, e:]

    return pl.pallas_call(
        body, name="ssm_in", grid=(n_rows // ROW_TILE,),
        in_specs=[_full(ctx.shape), _rows_lat(d), _full((1, d)), _seg(d), _seg(d), _full(w_in.shape)],
        out_specs=[_rows(e), _rows(e)], out_shape=[_sds((n_rows, e)), _sds((n_rows, e))],
        compiler_params=_cparams(1),
    )(ctx, x, g, scale, shift, w_in)


def _s5_post_math(y, z, w_glu, b_glu, w_out):
    er = lax.erf(y * INV_SQRT2)
    g = 0.5 * y * (1.0 + er)
    sg = _sigmoid(_dot(g, w_glu) + b_glu)
    g2 = g * sg
    sz = _sigmoid(z)
    silu_z = z * sz
    m = g2 * silu_z
    return er, g, sg, g2, sz, silu_z, m, _dot(m, w_out)


def _ssm_post(ctx, x, y0, y1, z, gate, w_glu, b_glu, w_out):
    n_rows, d = ctx.shape[0] + x.shape[0], x.shape[1]
    e = z.shape[1]

    def body(c_ref, x_ref, y0_ref, y1_ref, z_ref, gt_ref, wg_ref, bg_ref, wo_ref, o_ref):
        out = _s5_post_math(y0_ref[...] + y1_ref[...], z_ref[...], wg_ref[...], bg_ref[...], wo_ref[...])[-1]
        o_ref[...] = _token_tile(c_ref, x_ref) + gt_ref[0] * out

    return pl.pallas_call(
        body, name="ssm_post", grid=(n_rows // ROW_TILE,),
        in_specs=[_full(ctx.shape), _rows_lat(d), _rows(e), _rows(e), _rows(e), _seg(d), _full(w_glu.shape), _full((1, e)),
                  _full(w_out.shape)],
        out_specs=_rows(d), out_shape=_sds((n_rows, d)),
        compiler_params=_cparams(1),
    )(ctx, x, y0, y1, z, gate, w_glu, b_glu, w_out)


def _init_acc(first, *refs):
    @pl.when(first)
    def _():
        for r in refs:
            r[...] = jnp.zeros_like(r)


def _ssm_post_bwd(dxa, y0, y1, z, gate, w_glu, b_glu, w_out):
    n_rows, d = dxa.shape
    e = z.shape[1]

    def body(dx_ref, y0_ref, y1_ref, z_ref, gt_ref, wg_ref, bg_ref, wo_ref,
             dy_ref, dz_ref, dgt_ref, dwo_ref, dwg_ref, dbg_ref):
        i = pl.program_id(0)
        _init_acc(i == 0, dwo_ref, dwg_ref, dbg_ref)
        _init_acc(i <= 1, dgt_ref)
        y, zz = y0_ref[...] + y1_ref[...], z_ref[...]
        er, g, sg, g2, sz, silu_z, m, out = _s5_post_math(y, zz, wg_ref[...], bg_ref[...], wo_ref[...])
        dxa_t = dx_ref[...]
        dgt_ref[0] += _sum0(dxa_t * out)
        dout = gt_ref[0] * dxa_t
        dm = _dot_t1(dout, wo_ref[...])
        dwo_ref[...] += _dot_t0(m, dout)
        dg2 = dm * silu_z
        dz_ref[...] = dm * g2 * (sz * (1.0 + zz * (1.0 - sz)))
        dt = dg2 * g * sg * (1.0 - sg)
        dwg_ref[...] += _dot_t0(g, dt)
        dbg_ref[...] += _sum0(dt)
        dg = dg2 * sg + _dot_t1(dt, wg_ref[...])
        dy_ref[...] = dg * (0.5 * (1.0 + er) + y * jnp.exp(-0.5 * y * y) * INV_SQRT_2PI)

    return pl.pallas_call(
        body, name="ssm_post_bwd", grid=(n_rows // ROW_TILE,),
        in_specs=[_rows(d), _rows(e), _rows(e), _rows(e), _seg(d), _full(w_glu.shape), _full((1, e)), _full(w_out.shape)],
        out_specs=[_rows(e), _rows(e), _seg(d), _full(w_out.shape), _full(w_glu.shape), _full((1, e))],
        out_shape=[_sds((n_rows, e)), _sds((n_rows, e)), _sds((2, 1, d)), _sds(w_out.shape), _sds(w_glu.shape), _sds((1, e))],
        compiler_params=_cparams(1),
    )(dxa, y0, y1, z, gate, w_glu, b_glu, w_out)


def _ssm_in_bwd(du0, du1, dz, ctx, x_lat, dxa_next, g, scale, shift, w_in):
    n_lat, d = x_lat.shape
    n_rows = ctx.shape[0] + n_lat
    e = dz.shape[1]

    def body(du0_ref, du1_ref, dz_ref, c_ref, x_ref, dn_ref, g_ref, sc_ref, sh_ref, w_ref,
             gx_ref, dw_ref, dg_ref, dsc_ref, dsh_ref):
        i = pl.program_id(0)
        _init_acc(i == 0, dw_ref, dg_ref)
        _init_acc(i <= 1, dsc_ref, dsh_ref)
        x = _token_tile(c_ref, x_ref)
        h = _rms_mod(x, g_ref[...], sc_ref[0], sh_ref[0])
        dproj = jnp.concatenate([du0_ref[...] + du1_ref[...], dz_ref[...]], axis=1)
        dh = _dot_t1(dproj, w_ref[...])
        dw_ref[...] += _dot_t0(h, dproj)
        dx, dg, dsc, dsh = _rms_mod_bwd(x, g_ref[...], sc_ref[0], dh)
        dg_ref[...] += dg
        dsc_ref[0] += dsc
        dsh_ref[0] += dsh
        gx_ref[...] = dn_ref[...] + dx

    return pl.pallas_call(
        body, name="ssm_in_bwd", grid=(n_rows // ROW_TILE,),
        in_specs=[_rows(e), _rows(e), _rows(e), _full(ctx.shape), _rows_lat(d), _rows(d), _full((1, d)), _seg(d), _seg(d),
                  _full(w_in.shape)],
        out_specs=[_rows_lat(d), _full((d, 2 * e)), _full((1, d)), _seg(d), _seg(d)],
        out_shape=[_sds((n_lat, d)), _sds((d, 2 * e)), _sds((1, d)), _sds((2, 1, d)), _sds((2, 1, d))],
        compiler_params=_cparams(1),
    )(du0, du1, dz, ctx, x_lat, dxa_next, g, scale, shift, w_in)


Q_WIDTH = N_Q_HEADS * HEAD_DIM
SM_SCALE = 1.0 / math.sqrt(HEAD_DIM)


def _attn_in(xa, g, scale, shift, w_in, q_gain, k_gain, cos, sin):
    n_rows, d = xa.shape
    qk = Q_WIDTH + KV_WIDTH

    def body(x_ref, g_ref, sc_ref, sh_ref, w_ref, qg_ref, kg_ref, cos_ref, sin_ref, hq_ref, hk_ref,
             q_ref, k_ref, v_ref, z_ref, raw_ref):
        h = _rms_mod(x_ref[...], g_ref[...], sc_ref[0], sh_ref[0])
        proj = _dot(h, w_ref[...])
        q_raw, k_raw = proj[:, :Q_WIDTH], proj[:, Q_WIDTH:qk]
        cos, sin = cos_ref[...], sin_ref[...]
        q = _head_norm_rope(q_raw, qg_ref[...], _lanes(cos, Q_WIDTH), _lanes(sin, Q_WIDTH), hq_ref[...])
        k = _head_norm_rope(k_raw, kg_ref[...], _lanes(cos, KV_WIDTH), _lanes(sin, KV_WIDTH), hk_ref[...])
        q_ref[...] = (q * SM_SCALE).astype(BF16)
        k_ref[...] = k.astype(BF16)
        v_ref[...] = proj[:, qk:qk + KV_WIDTH].astype(BF16)
        z_ref[...] = proj[:, qk + KV_WIDTH:]
        raw_ref[...] = proj[:, :qk]

    return pl.pallas_call(
        body, name="attn_in", grid=(n_rows // ROW_TILE,),
        in_specs=[_rows(d), _full((1, d)), _seg(d), _seg(d), _full(w_in.shape), _full((1, Q_WIDTH)), _full((1, KV_WIDTH)),
                  _rows(128), _rows(128), _full((Q_WIDTH, 128)), _full((KV_WIDTH, 128))],
        out_specs=[_rows_lat(Q_WIDTH), _rows(KV_WIDTH), _rows(KV_WIDTH), _rows(Q_WIDTH), _rows(qk)],
        out_shape=[_sds((n_rows - ROW_TILE, Q_WIDTH), BF16), _sds((n_rows, KV_WIDTH), BF16), _sds((n_rows, KV_WIDTH), BF16),
                   _sds((n_rows, Q_WIDTH)), _sds((n_rows, qk))],
        compiler_params=_cparams(1),
    )(xa, g, scale, shift, w_in, q_gain, k_gain, cos, sin, _head_of_lane(Q_WIDTH), _head_of_lane(KV_WIDTH))


GROUP_WIDTH = KV_REP * HEAD_DIM


def _stack_heads(ref):
    return jnp.concatenate([ref[:, h * HEAD_DIM:(h + 1) * HEAD_DIM] for h in range(KV_REP)], axis=0)


def _unstack_heads(a_t, tq):
    return jnp.concatenate([a_t[:, h * tq:(h + 1) * tq].T for h in range(KV_REP)], axis=1)


def _kv_tile(n_keys):
    return 768 if n_keys % 768 == 0 else 256


def _kv_tile_fwd(n_keys):
    return 1408 if n_keys % 1408 == 0 else _kv_tile(n_keys)


def _q_tile(n_lat):
    return 512 if n_lat % 512 == 0 else 256


def _flash_fwd(q, k, v_t):
    n_lat = q.shape[0]
    tq = _q_tile(n_lat)
    rows = KV_REP * tq
    n_kv, tk, n_q = k.shape[1], k.shape[2], n_lat // tq

    v_rows = v_t.shape[2]

    def body(q_ref, k_ref, vt_ref, o_ref, lse_ref):
        q = _stack_heads(q_ref)

        def step(j, carry):
            m_prev, acc = carry
            s_t = _dot_t1(k_ref[0, j], q)
            m_new = jnp.maximum(m_prev, jnp.max(s_t, axis=0, keepdims=True))
            alpha = jnp.exp(m_prev - m_new)
            p_t = jnp.exp(s_t - m_new)
            return m_new, alpha * acc + _dot(vt_ref[0, j], p_t)

        init = (jnp.full((1, rows), -jnp.inf, F32), jnp.zeros((v_rows, rows), F32))
        m, acc = lax.fori_loop(0, n_kv, step, init)
        l = acc[HEAD_DIM:HEAD_DIM + 1]
        o_ref[...] = _unstack_heads(acc[:HEAD_DIM] / l, tq)
        lse_ref[0, 0] = m + jnp.log(l)

    kv_all = lambda a: pl.BlockSpec((1,) + a.shape[1:], lambda g, i: (g, 0, 0, 0))
    return pl.pallas_call(
        body, name="flash_fwd", grid=(N_KV_HEADS, n_q),
        in_specs=[pl.BlockSpec((tq, GROUP_WIDTH), lambda g, i: (i, g)), kv_all(k), kv_all(v_t)],
        out_specs=[pl.BlockSpec((tq, GROUP_WIDTH), lambda g, i: (i, g)),
                   pl.BlockSpec((1, 1, 1, rows), lambda g, i: (g, i, 0, 0))],
        out_shape=[_sds((n_lat, Q_WIDTH)), _sds((N_KV_HEADS, n_q, 1, rows))],
        compiler_params=_cparams(2),
    )(q, k, v_t)


def _flash_bwd(q, k, k_t, v, do, lse_t, delta_t):
    n_lat = q.shape[0]
    tq = _q_tile(n_lat)
    rows = KV_REP * tq
    n_kv, tk, n_q = k.shape[1], k.shape[2], n_lat // tq

    def body(q_ref, k_ref, kt_ref, v_ref, do_ref, lse_ref, dl_ref, dq_ref, dk_ref, dv_ref):
        _init_acc(pl.program_id(1) == 0, dk_ref, dv_ref)
        q, do = _stack_heads(q_ref), _stack_heads(do_ref)
        lse, delta = lse_ref[0, 0], dl_ref[0, 0]

        def step(j, dq_acc):
            p_t = jnp.exp(_dot_t1(k_ref[0, j], q) - lse)
            dv_ref[0, j] += _dot(p_t, do)
            ds_t = p_t * (_dot_t1(v_ref[0, j], do) - delta)
            dk_ref[0, j] += _dot(ds_t, q)
            return dq_acc + _dot(kt_ref[0, j], ds_t)

        dq = lax.fori_loop(0, n_kv, step, jnp.zeros((HEAD_DIM, rows), F32))
        dq_ref[...] = _unstack_heads(dq, tq)

    qspec = pl.BlockSpec((tq, GROUP_WIDTH), lambda g, i: (i, g))
    rowspec = pl.BlockSpec((1, 1, 1, rows), lambda g, i: (g, i, 0, 0))
    kv_all = lambda a: pl.BlockSpec((1,) + a.shape[1:], lambda g, i: (g, 0, 0, 0))
    return pl.pallas_call(
        body, name="flash_bwd", grid=(N_KV_HEADS, n_q),
        in_specs=[qspec, kv_all(k), kv_all(k_t), kv_all(v), qspec, rowspec, rowspec],
        out_specs=[qspec, kv_all(k), kv_all(k)],
        out_shape=[_sds((n_lat, Q_WIDTH)), _sds(k.shape), _sds(k.shape)],
        compiler_params=_cparams(2),
    )(q, k, k_t, v, do, lse_t, delta_t)


def _to_lane_stacked(a, tq):
    n_lat = a.shape[0]
    a = a.reshape(n_lat // tq, tq, N_KV_HEADS, KV_REP).transpose(2, 0, 3, 1)
    return a.reshape(N_KV_HEADS, n_lat // tq, 1, KV_REP * tq)


def _attn_post_loss(o, z, xa, gate, w_out, final_g, target):
    n_lat, d = target.shape
    e = o.shape[1]
    head_of_lane = (jnp.arange(e)[:, None] // HEAD_DIM == jnp.arange(128)[None, :]).astype(BF16)

    def body(o_ref, z_ref, x_ref, gt_ref, w_ref, fg_ref, tg_ref, hl_ref,
             do_ref, dl_ref, dz_ref, dx_ref, loss_ref, dfg_ref, dgt_ref, dw_ref):
        _init_acc(pl.program_id(0) == 0, loss_ref, dfg_ref, dgt_ref, dw_ref)
        oo, zz, gate_t, fg = o_ref[...], z_ref[...], gt_ref[0], fg_ref[...]
        sz = _sigmoid(zz)
        silu_z = zz * sz
        m = oo * silu_z
        out = _dot(m, w_ref[...])
        x2 = x_ref[...] + gate_t * out
        r = lax.rsqrt(jnp.mean(x2 * x2, axis=-1, keepdims=True) + NORM_EPS)
        n = x2 * r
        err = n * fg - tg_ref[...]
        loss_ref[...] += 0.5 * jnp.sum(jnp.mean(err * err, axis=-1, keepdims=True), axis=0, keepdims=True)
        dy = err * (1.0 / d)
        dfg_ref[...] += _sum0(dy * n)
        dn = dy * fg
        dx2 = r * (dn - n * jnp.mean(dn * n, axis=-1, keepdims=True))
        dx_ref[...] = dx2
        dgt_ref[...] += _sum0(dx2 * out)
        dout = gate_t * dx2
        dw_ref[...] += _dot_t0(m, dout)
        dm = _dot_t1(dout, w_ref[...])
        do = dm * silu_z
        do_ref[...] = do.astype(BF16)
        prod = do * oo
        hi = prod.astype(BF16)
        lo = (prod - hi.astype(F32)).astype(BF16)
        dl_ref[...] = _dot(hi, hl_ref[...]) + _dot(lo, hl_ref[...])
        dz_ref[...] = dm * oo * (sz * (1.0 + zz * (1.0 - sz)))

    return pl.pallas_call(
        body, name="attn_post_loss", grid=(n_lat // ROW_TILE,),
        in_specs=[_rows(e), _rows_skip_ctx(e), _rows_skip_ctx(d), _lat_seg(d), _full(w_out.shape),
                  _full((1, d)), _rows(d), _full((e, 128))],
        out_specs=[_rows(e), _rows(128), _rows(e), _rows(d), _full((1, 1)), _full((1, d)), _full((1, d)), _full(w_out.shape)],
        out_shape=[_sds((n_lat, e), BF16), _sds((n_lat, 128)), _sds((n_lat, e)), _sds((n_lat, d)), _sds((1, 1)), _sds((1, d)),
                   _sds((1, d)), _sds(w_out.shape)],
        compiler_params=_cparams(1),
    )(o, z, xa, gate, w_out, final_g, target, head_of_lane)


def _attn_in_bwd(dq, dk, dv, dz, raw, xa, dx2, g, scale, shift, q_gain, k_gain, cos, sin, w_in):
    n_rows, d = xa.shape
    qk = Q_WIDTH + KV_WIDTH
    n_in = w_in.shape[1]

    def body(dq_ref, dk_ref, dv_ref, dz_ref, raw_ref, x_ref, dx2_ref, g_ref, sc_ref, sh_ref, qg_ref, kg_ref, cos_ref, sin_ref,
             w_ref, hq_ref, hk_ref, dxa_ref, dw_ref, dqg_ref, dkg_ref, dg_ref, dsc_ref, dsh_ref):
        i = pl.program_id(0)
        _init_acc(i == 0, dw_ref, dqg_ref, dkg_ref, dg_ref)
        _init_acc(i <= 1, dsc_ref, dsh_ref)
        is_lat = (i > 0).astype(F32)
        x = x_ref[...]
        h = _rms_mod(x, g_ref[...], sc_ref[0], sh_ref[0])
        cos, sin = cos_ref[...], sin_ref[...]
        raw_t = raw_ref[...]
        dq_raw, dqg = _head_norm_rope_bwd(raw_t[:, :Q_WIDTH], qg_ref[...], _lanes(cos, Q_WIDTH), _lanes(sin, Q_WIDTH),
                                          hq_ref[...], dq_ref[...] * (SM_SCALE * is_lat))
        dk_raw, dkg = _head_norm_rope_bwd(raw_t[:, Q_WIDTH:], kg_ref[...], _lanes(cos, KV_WIDTH), _lanes(sin, KV_WIDTH),
                                          hk_ref[...], dk_ref[...])
        dqg_ref[...] += dqg
        dkg_ref[...] += dkg
        dproj = jnp.concatenate([dq_raw, dk_raw, dv_ref[...], dz_ref[...] * is_lat], axis=1)
        dh = _dot_t1(dproj, w_ref[...])
        dw_ref[...] += _dot_t0(h, dproj)
        dx, dg, dsc, dsh = _rms_mod_bwd(x, g_ref[...], sc_ref[0], dh)
        dg_ref[...] += dg
        dsc_ref[0] += dsc
        dsh_ref[0] += dsh
        dxa_ref[...] = dx + dx2_ref[...] * is_lat

    return pl.pallas_call(
        body, name="attn_in_bwd", grid=(n_rows // ROW_TILE,),
        in_specs=[_rows_lat(Q_WIDTH), _rows(KV_WIDTH), _rows(KV_WIDTH), _rows_lat(Q_WIDTH), _rows(qk), _rows(d), _rows_lat(d),
                  _full((1, d)), _seg(d), _seg(d), _full((1, Q_WIDTH)), _full((1, KV_WIDTH)), _rows(128), _rows(128),
                  _full(w_in.shape), _full((Q_WIDTH, 128)), _full((KV_WIDTH, 128))],
        out_specs=[_rows(d), _full((d, n_in)), _full((1, Q_WIDTH)), _full((1, KV_WIDTH)), _full((1, d)), _seg(d), _seg(d)],
        out_shape=[_sds((n_rows, d)), _sds((d, n_in)), _sds((1, Q_WIDTH)), _sds((1, KV_WIDTH)), _sds((1, d)),
                   _sds((2, 1, d)), _sds((2, 1, d))],
        compiler_params=_cparams(1),
    )(dq, dk, dv, dz, raw, xa, dx2, g, scale, shift, q_gain, k_gain, cos, sin, w_in,
      _head_of_lane(Q_WIDTH), _head_of_lane(KV_WIDTH))


def _heads_major(a, n_heads):
    return a.reshape(a.shape[0], n_heads, HEAD_DIM).transpose(1, 0, 2)


def _tokens_major(a):
    return a.transpose(1, 0, 2).reshape(a.shape[1], a.shape[0] * HEAD_DIM)


def _local_step(x, ctx, target, mods, norm_g, ssm, w_ssm_in, w_glu, b_glu, w_ssm_out, w_attn_in, q_norm, k_norm, w_attn_out,
                final_g, attn_exchange=None, post_exchange=None):
    n_ctx, d = ctx.shape
    assert n_ctx == ROW_TILE
    n_lat = x.shape[0]
    (shift0, scale0, gate0), (shift1, scale1, gate1) = mods
    g0, g1, fg = norm_g[0:1], norm_g[1:2], final_g.reshape(1, d)
    b_glu = b_glu.reshape(1, -1)
    q_gain = jnp.tile(q_norm.reshape(1, HEAD_DIM), (1, N_Q_HEADS))
    k_gain = jnp.tile(k_norm.reshape(1, HEAD_DIM), (1, N_KV_HEADS))
    cos, sin = _rope_tables(n_ctx, n_lat)

    u, z0 = _ssm_in(ctx, x, g0, scale0, shift0, w_ssm_in)
    gather = lambda ex: (ex[0], False) if ex else None
    (y0, y1), saved, gathered = _s5_forward(u, ssm, n_ctx, (gather(attn_exchange), gather(post_exchange)))
    if attn_exchange:
        w_attn_in, w_attn_out = attn_exchange[1](gathered[0])
    if post_exchange:
        w_glu, w_ssm_out = post_exchange[1](gathered[1])
    xa1 = _ssm_post(ctx, x, y0, y1, z0, gate0, w_glu, b_glu, w_ssm_out)

    q, k, v, z1, raw = _attn_in(xa1, g1, scale1, shift1, w_attn_in, q_gain, k_gain, cos, sin)
    tq, tk, tk_fwd = _q_tile(n_lat), _kv_tile(n_ctx + n_lat), _kv_tile_fwd(n_ctx + n_lat)
    k_h, v_h = _heads_major(k, N_KV_HEADS), _heads_major(v, N_KV_HEADS)
    k_b, v_b = k_h.reshape(N_KV_HEADS, -1, tk, HEAD_DIM), v_h.reshape(N_KV_HEADS, -1, tk, HEAD_DIM)
    v_t = v_h.reshape(N_KV_HEADS, -1, tk_fwd, HEAD_DIM).transpose(0, 1, 3, 2)
    v_t_ones = jnp.concatenate([v_t, jnp.ones(v_t.shape[:2] + (16, tk_fwd), BF16)], axis=2)
    o, lse_t = _flash_fwd(q, k_h.reshape(N_KV_HEADS, -1, tk_fwd, HEAD_DIM), v_t_ones)
    do, delta, dz1, dx2, loss, d_fg, d_gate1, d_w_attn_out = _attn_post_loss(
        o, z1, xa1, gate1, w_attn_out, fg, target)

    dq, dk_b, dv_b = _flash_bwd(q, k_b, k_b.transpose(0, 1, 3, 2), v_b, do, lse_t, _to_lane_stacked(delta[:, :N_Q_HEADS], tq))
    keys_major = lambda a: _tokens_major(a.reshape(N_KV_HEADS, -1, HEAD_DIM))
    dxa1, d_w_attn_in, d_qg, d_kg, d_g1, d_scale1, d_shift1 = _attn_in_bwd(
        dq, keys_major(dk_b), keys_major(dv_b), dz1, raw, xa1, dx2, g1, scale1, shift1,
        q_gain, k_gain, cos, sin, w_attn_in)
    dy, dz0, d_gate0, d_w_ssm_out, d_w_glu, d_b_glu = _ssm_post_bwd(
        dxa1, y0, y1, z0, gate0, w_glu, b_glu, w_ssm_out)
    scatter = lambda ex, *g: (ex[2](*g), True) if ex else None
    (du0, du1), d_ssm, parts = _s5_backward(
        u, dy, ssm, saved, n_ctx,
        (scatter(attn_exchange, d_w_attn_in, d_w_attn_out), scatter(post_exchange, d_w_glu, d_w_ssm_out)))
    grad_x, d_w_ssm_in, d_g0, d_scale0, d_shift0 = _ssm_in_bwd(du0, du1, dz0, ctx, x, dxa1, g0, scale0, shift0, w_ssm_in)

    d_gate1_seg = jnp.concatenate([jnp.zeros((1, 1, d), F32), d_gate1.reshape(1, 1, d)], axis=0)
    grads = dict(
        norm_g=jnp.concatenate([d_g0, d_g1], axis=0), ssm_w_in=d_w_ssm_in, ssm=d_ssm, ssm_b_glu=d_b_glu.reshape(-1),
        attn_q_norm=d_qg.reshape(N_Q_HEADS, HEAD_DIM).sum(0), attn_k_norm=d_kg.reshape(N_KV_HEADS, HEAD_DIM).sum(0),
        final_norm_g=d_fg.reshape(-1))
    if attn_exchange:
        grads.update(attn_parts=parts[0])
    else:
        grads.update(attn_w_in=d_w_attn_in, attn_w_out=d_w_attn_out)
    if post_exchange:
        grads.update(post_parts=parts[1])
    else:
        grads.update(ssm_w_glu=d_w_glu, ssm_w_out=d_w_ssm_out)
    d_mods = ((d_shift0, d_scale0, d_gate0), (d_shift1, d_scale1, d_gate1_seg))
    return loss[0, 0], grad_x, grads, d_mods


def _my_index():
    return 4 * lax.axis_index("x") + 2 * lax.axis_index("y") + lax.axis_index("c")


def _peer(k):
    mx, my, mc = lax.axis_index("x"), lax.axis_index("y"), lax.axis_index("c")
    px = 1 - mx if k & 4 else mx
    py = 1 - my if k & 2 else my
    pc = 1 - mc if k & 1 else mc
    return (px, py, pc), 4 * px + 2 * py + pc


HBM_SPEC = pl.BlockSpec(memory_space=pl.ANY)


def _exchange(x, name, all_to_all):
    def body(x_ref, out_ref, send_sems, recv_sems, local_sem):
        _exchange_copies(all_to_all, x_ref, out_ref, send_sems, recv_sems, local_sem, start=True)
        _exchange_copies(all_to_all, x_ref, out_ref, send_sems, recv_sems, local_sem, start=False)

    return pl.pallas_call(
        body, name=name, in_specs=[HBM_SPEC], out_specs=HBM_SPEC,
        out_shape=_exchange_out_shape(x, all_to_all), scratch_shapes=_exchange_semaphores(),
    )(x)


def _all_gather_two_level(x, name):
    def body(x_ref, out_ref, send_sems, recv_sems, local_sem):
        mx, my, mc = lax.axis_index("x"), lax.axis_index("y"), lax.axis_index("c")
        me, sibling = (mx, my, mc), (mx, my, 1 - mc)
        chips = [(1 - mx, my), (mx, 1 - my), (1 - mx, 1 - my)]

        def block(px, py, pc):
            return out_ref.at[4 * px + 2 * py + pc]

        def copy(k, owner, to, src=None):
            return pltpu.make_async_remote_copy(
                src_ref=block(*owner) if src is None else src, dst_ref=block(*owner),
                send_sem=send_sems.at[k], recv_sem=recv_sems.at[k], device_id=to, device_id_type=MESH_IDS)

        mine = pltpu.make_async_copy(x_ref, block(*me), local_sem)
        mine.start()
        first = [copy(0, me, sibling, src=x_ref)] + [copy(1 + j, me, (*chip, mc), src=x_ref) for j, chip in enumerate(chips)]
        for cp in first:
            cp.start()
        passed = [copy(4 + j, (*chip, mc), sibling) for j, chip in enumerate(chips)]
        for j, chip in enumerate(chips):
            copy(1 + j, (*chip, mc), me).wait_recv()
            passed[j].start()
        copy(0, sibling, me).wait_recv()
        for j, chip in enumerate(chips):
            copy(4 + j, (*chip, 1 - mc), me).wait_recv()
        for cp in first + passed:
            cp.wait_send()
        mine.wait()

    return pl.pallas_call(
        body, name=name, in_specs=[HBM_SPEC], out_specs=HBM_SPEC,
        out_shape=_exchange_out_shape(x, False), scratch_shapes=_exchange_semaphores(),
    )(x)


def _exchange_pair(xa, xa_all_to_all, xb, xb_all_to_all, name):
    def body(xa_ref, xb_ref, oa_ref, ob_ref, sa, ra, la, sb, rb, lb):
        for start in (True, False):
            _exchange_copies(xa_all_to_all, xa_ref, oa_ref, sa, ra, la, start=start)
            _exchange_copies(xb_all_to_all, xb_ref, ob_ref, sb, rb, lb, start=start)

    return pl.pallas_call(
        body, name=name, in_specs=[HBM_SPEC, HBM_SPEC], out_specs=[HBM_SPEC, HBM_SPEC],
        out_shape=[_exchange_out_shape(xa, xa_all_to_all), _exchange_out_shape(xb, xb_all_to_all)],
        scratch_shapes=_exchange_semaphores() + _exchange_semaphores(),
    )(xa, xb)


def _exchange_out_shape(x, all_to_all):
    return _sds((N_DEV,) + tuple(x.shape[1:] if all_to_all else x.shape), x.dtype)


def _exchange_semaphores():
    return [pltpu.SemaphoreType.DMA((N_DEV - 1,)), pltpu.SemaphoreType.DMA((N_DEV - 1,)), pltpu.SemaphoreType.DMA]


def _exchange_copies(all_to_all, x_ref, out_ref, send_sems, recv_sems, local_sem, start):
    me = _my_index()
    mine = pltpu.make_async_copy(x_ref.at[me] if all_to_all else x_ref, out_ref.at[me], local_sem)
    if start:
        mine.start()
    for k in range(1, N_DEV):
        peer, peer_idx = _peer(k)
        send = pltpu.make_async_remote_copy(
            src_ref=x_ref.at[peer_idx] if all_to_all else x_ref, dst_ref=out_ref.at[me],
            send_sem=send_sems.at[k - 1], recv_sem=recv_sems.at[k - 1], device_id=peer, device_id_type=MESH_IDS)
        if start:
            send.start()
        else:
            pltpu.make_async_remote_copy(
                src_ref=x_ref.at[me] if all_to_all else x_ref, dst_ref=out_ref.at[peer_idx],
                send_sem=send_sems.at[k - 1], recv_sem=recv_sems.at[k - 1], device_id=peer,
                device_id_type=MESH_IDS).wait_recv()
            send.wait_send()
    if not start:
        mine.wait()


def _ride(rider, first, last, refs):
    @pl.when(first)
    def _():
        _exchange_copies(rider[1], *refs, start=True)

    @pl.when(last)
    def _():
        _exchange_copies(rider[1], *refs, start=False)


MOD_ROWS = 16
CTX_ROW = N_DEV


def _mod_fwd(cond, w_shard, b_cols):
    n_layers, d, cols = w_shard.shape

    def body(c_ref, w_ref, b_ref, o_ref):
        c = c_ref[...]
        s = c * _sigmoid(c)
        for i in range(n_layers):
            o_ref[i] = _dot(s, w_ref[i]) + b_ref[i]

    return pl.pallas_call(
        body, name="mod_fwd", out_shape=_sds((n_layers, MOD_ROWS, cols)),
        compiler_params=pltpu.CompilerParams(vmem_limit_bytes=VMEM_LIMIT),
    )(cond, w_shard, b_cols.reshape(n_layers, 1, cols))


def _mod_bwd(cond, d_lat_cols, d_ctx_cols, w_shard):
    n_layers, d, cols = w_shard.shape

    def body(c_ref, dl_ref, dc_ref, w_ref, dw_ref, dcc_ref):
        c = c_ref[...]
        sg = _sigmoid(c)
        s = c * sg
        d_s = jnp.zeros((MOD_ROWS, d), F32)
        for i in range(n_layers):
            d_ctx = dc_ref[0, i]
            for j in range(1, N_DEV):
                d_ctx = d_ctx + dc_ref[j, i]
            dm = jnp.concatenate([dl_ref[i], d_ctx, jnp.zeros((MOD_ROWS - N_DEV - 1, cols), F32)], axis=0)
            dw_ref[i] = _dot_t0(s, dm)
            d_s = d_s + _dot_t1(dm, w_ref[i])
        d_c = d_s * (sg * (1.0 + c * (1.0 - sg)))
        dcc_ref[...] = d_c[CTX_ROW:CTX_ROW + 1]

    return pl.pallas_call(
        body, name="mod_bwd", out_shape=[_sds((n_layers, d, cols)), _sds((1, d))],
        compiler_params=pltpu.CompilerParams(vmem_limit_bytes=VMEM_LIMIT),
    )(cond, d_lat_cols, d_ctx_cols, w_shard)


ADAM_TILE = 512


def _adamw(w, g_parts, m, v, name):
    n_parts, n_rows, lanes = g_parts.shape
    tile = min(ADAM_TILE, n_rows)
    assert n_rows % tile == 0
    c1 = 1.0 - ADAM_B1 ** ADAM_STEP
    c2 = 1.0 - ADAM_B2 ** ADAM_STEP

    def body(w_ref, g_ref, m_ref, v_ref, go_ref, d_ref, mo_ref, vo_ref):
        g = g_ref[0].astype(F32)
        for p in range(1, n_parts):
            g = g + g_ref[p].astype(F32)
        m_new = ADAM_B1 * m_ref[...] + (1.0 - ADAM_B1) * g
        v_new = ADAM_B2 * v_ref[...] + (1.0 - ADAM_B2) * (g * g)
        go_ref[...] = g
        mo_ref[...] = m_new
        vo_ref[...] = v_new
        d_ref[...] = -ADAM_LR * ((m_new / c1) / (jnp.sqrt(v_new / c2) + ADAM_EPS) + ADAM_WD * w_ref[...])

    row = pl.BlockSpec((tile, lanes), lambda i: (i, 0))
    return pl.pallas_call(
        body, name=name, grid=(n_rows // tile,),
        in_specs=[row, pl.BlockSpec((n_parts, tile, lanes), lambda i: (0, i, 0)), row, row],
        out_specs=[row] * 4, out_shape=[_sds((n_rows, lanes))] * 4,
        compiler_params=_cparams(1),
    )(w, g_parts, m, v)


def _sum_parts(parts):
    n_parts, n_rows, lanes = parts.shape

    def body(p_ref, o_ref):
        acc = p_ref[0]
        for p in range(1, n_parts):
            acc = acc + p_ref[p]
        o_ref[...] = acc

    return pl.pallas_call(body, name="sum_parts", out_shape=_sds((n_rows, lanes)))(parts)


def _pack(arrays, row_multiple):
    parts = []
    for a in arrays:
        flat = a.reshape(-1)
        parts.append(jnp.pad(flat, (0, (-flat.shape[0]) % 1024)))
    flat = jnp.concatenate(parts)
    flat = jnp.pad(flat, (0, (-flat.shape[0]) % (row_multiple * 128)))
    return flat.reshape(-1, 128)


def _unpack(packed, shapes):
    flat = packed.reshape(-1)
    out, pos = [], 0
    for s in shapes:
        n = math.prod(s)
        out.append(flat[pos:pos + n].reshape(s))
        pos += n + (-n) % 1024
    return out


WEIGHT_NAMES = ['c_ctx', 'w_mod', 'b_mod', 'norm_g', 'ssm_w_in', 'ssm_a_re', 'ssm_a_im', 'ssm_log_dt', 'ssm_b_re', 'ssm_b_im',
                'ssm_c_re', 'ssm_c_im', 'ssm_d', 'ssm_w_glu', 'ssm_b_glu', 'ssm_w_out', 'attn_w_in', 'attn_q_norm',
                'attn_k_norm', 'attn_w_out', 'final_norm_g']
FIRST_SHARDED = ['ssm_w_in']
POST_SHARDED = ['ssm_w_glu', 'ssm_w_out']
ATTN_SHARDED = ['attn_w_in', 'attn_w_out']
SHARDED = FIRST_SHARDED + POST_SHARDED + ATTN_SHARDED
COLUMN_SHARDED = ('ssm_w_in', 'attn_w_in')
REPLICATED = ['c_ctx', 'b_mod', 'norm_g', 'ssm_a_re', 'ssm_a_im', 'ssm_log_dt', 'ssm_b_re', 'ssm_b_im', 'ssm_c_re', 'ssm_c_im',
              'ssm_d', 'ssm_b_glu', 'attn_q_norm', 'attn_k_norm', 'final_norm_g']
SSM_NAMES = ['ssm_a_re', 'ssm_a_im', 'ssm_log_dt', 'ssm_b_re', 'ssm_b_im', 'ssm_c_re', 'ssm_c_im', 'ssm_d']


def _full_from_shards(gathered, name, shard_shape):
    rows, cols = shard_shape
    w = gathered.reshape(N_DEV, rows, cols)
    if name in COLUMN_SHARDED:
        return w.transpose(1, 0, 2).reshape(rows, N_DEV * cols)
    return w.reshape(N_DEV * rows, cols)


def _shards_from_full(g, name):
    if name in COLUMN_SHARDED:
        rows, cols = g.shape
        g = g.reshape(rows, N_DEV, cols // N_DEV).transpose(1, 0, 2)
    return g.reshape(N_DEV, -1, 128)


def kernel(x, c, ctx, c_ctx, w_mod, b_mod, norm_g, ssm_w_in, ssm_a_re, ssm_a_im, ssm_log_dt, ssm_b_re, ssm_b_im, ssm_c_re, ssm_c_im, ssm_d, ssm_w_glu, ssm_b_glu, ssm_w_out, attn_w_in, attn_q_norm, attn_k_norm, attn_w_out, final_norm_g, loss_target, m_c_ctx, m_w_mod, m_b_mod, m_norm_g, m_ssm_w_in, m_ssm_a_re, m_ssm_a_im, m_ssm_log_dt, m_ssm_b_re, m_ssm_b_im, m_ssm_c_re, m_ssm_c_im, m_ssm_d, m_ssm_w_glu, m_ssm_b_glu, m_ssm_w_out, m_attn_w_in, m_attn_q_norm, m_attn_k_norm, m_attn_w_out, m_final_norm_g, v_c_ctx, v_w_mod, v_b_mod, v_norm_g, v_ssm_w_in, v_ssm_a_re, v_ssm_a_im, v_ssm_log_dt, v_ssm_b_re, v_ssm_b_im, v_ssm_c_re, v_ssm_c_im, v_ssm_d, v_ssm_w_glu, v_ssm_b_glu, v_ssm_w_out, v_attn_w_in, v_attn_q_norm, v_attn_k_norm, v_attn_w_out, v_final_norm_g):
    env = dict(locals())
    weights = {n: env[n] for n in WEIGHT_NAMES}
    mom_m = {n: env["m_" + n] for n in WEIGHT_NAMES}
    mom_v = {n: env["v_" + n] for n in WEIGHT_NAMES}
    d = D_MODEL
    me = _my_index()
    mod_cols = w_mod.shape[-1]

    shard_shapes = {n: weights[n].shape[1:] for n in SHARDED}
    pack_shards = lambda names: _pack([weights[n] for n in names], 1).astype(BF16)

    def unpack_full(gathered, names):
        full, pos = [], 0
        for n in names:
            rows = math.prod(shard_shapes[n]) // 128
            full.append(_full_from_shards(gathered[:, pos:pos + rows], n, shard_shapes[n]))
            pos += rows
        return full

    def exchange_of(names):
        return (pack_shards(names), lambda gathered: unpack_full(gathered, names),
                lambda *grads: jnp.concatenate([_shards_from_full(t, n) for t, n in zip(grads, names)], axis=1).astype(BF16))

    c_all = _exchange(c.reshape(8, d // 8), "gather_c", False)
    w_first = _all_gather_two_level(pack_shards(FIRST_SHARDED), "gather_ssm_w_in")
    cond = jnp.concatenate([c_all.reshape(N_DEV, d), c_ctx.reshape(1, d), jnp.zeros((MOD_ROWS - N_DEV - 1, d), F32)], axis=0)
    (w_ssm_in,) = unpack_full(w_first, FIRST_SHARDED)

    b_cols = lax.dynamic_slice(b_mod, (0, me * mod_cols), (2, mod_cols))
    mod_shard = _mod_fwd(cond, w_mod, b_cols)
    mod_all = _exchange(mod_shard.reshape(2 * MOD_ROWS, mod_cols), "gather_mod", False)
    mod_full = mod_all.reshape(N_DEV, 2, MOD_ROWS, mod_cols).transpose(1, 2, 0, 3).reshape(2, MOD_ROWS, 3 * d)
    lat_rows = lax.dynamic_slice(mod_full, (0, me, 0), (2, 1, 3 * d))
    mods = []
    for i in range(2):
        seg = jnp.stack([mod_full[i, CTX_ROW:CTX_ROW + 1], lat_rows[i]])
        mods.append((seg[:, :, :d], seg[:, :, d:2 * d], seg[:, :, 2 * d:]))

    ssm = tuple(weights[n][0] for n in SSM_NAMES)
    loss, grad_x, g, d_mods = _local_step(
        x[0], ctx[0], loss_target[0], mods, norm_g, ssm, w_ssm_in, None, ssm_b_glu[0], None,
        None, attn_q_norm[0], attn_k_norm[0], None, final_norm_g, exchange_of(ATTN_SHARDED), exchange_of(POST_SHARDED))

    d_rows = jnp.stack([jnp.concatenate(dm, axis=-1) for dm in d_mods])
    d_rows = jnp.concatenate([d_rows.reshape(4, 3 * d), jnp.zeros((4, 3 * d), F32)], axis=0)
    d_all, first_parts = _exchange_pair(d_rows, False, _shards_from_full(g['ssm_w_in'], 'ssm_w_in').astype(BF16), True,
                                        "gather_dmod_and_scatter_ssm_w_in_grads")
    d_all = d_all[:, :4].reshape(N_DEV, 2, 2, 3 * d)
    d_all = lax.dynamic_slice(d_all, (0, 0, 0, me * mod_cols), (N_DEV, 2, 2, mod_cols))
    d_w_mod, d_c_ctx = _mod_bwd(cond, d_all[:, :, 1].transpose(1, 0, 2), d_all[:, :, 0:1], w_mod)
    d_b_mod = jnp.stack([jnp.concatenate([t[0] + t[1] for t in dm], axis=-1).reshape(3 * d) for dm in d_mods])

    parts_of = {}
    for names, parts in ((FIRST_SHARDED, first_parts), (POST_SHARDED, g['post_parts']), (ATTN_SHARDED, g['attn_parts'])):
        pos = 0
        for n in names:
            rows = math.prod(shard_shapes[n]) // 128
            parts_of[n] = parts[:, pos:pos + rows].reshape((N_DEV,) + shard_shapes[n])
            pos += rows

    def update(n, g_parts):
        as_2d = lambda t: t.reshape(-1, t.shape[-1])
        res = _adamw(as_2d(weights[n]), g_parts, as_2d(mom_m[n]), as_2d(mom_v[n]), "adamw_" + n)
        return [t.reshape(weights[n].shape) for t in res]

    big = {n: update(n, parts_of[n]) for n in SHARDED}
    big['w_mod'] = update('w_mod', d_w_mod.reshape(1, -1, mod_cols))

    small = dict(zip(SSM_NAMES, g['ssm']))
    small.update(c_ctx=d_c_ctx, b_mod=d_b_mod, norm_g=g['norm_g'], ssm_b_glu=g['ssm_b_glu'], attn_q_norm=g['attn_q_norm'],
                 attn_k_norm=g['attn_k_norm'], final_norm_g=g['final_norm_g'])
    pack_small = lambda t, last: _pack([t[n] for n in REPLICATED] + [last], ADAM_TILE)
    no_weight = jnp.zeros((1,), F32)
    g_small = pack_small(small, loss.reshape(1))
    slices = _exchange(g_small.reshape(N_DEV, -1, 128), "scatter_small_grads", True)
    g_small = _all_gather_two_level(_sum_parts(slices), "gather_small_grads").reshape(1, -1, 128)
    rep = _adamw(pack_small(weights, no_weight), g_small, pack_small(mom_m, no_weight), pack_small(mom_v, no_weight),
                 "adamw_replicated")
    rep = [_unpack(t, [weights[n].shape for n in REPLICATED] + [(1,)]) for t in rep]
    loss = rep[0][-1][0]

    results = []
    for kind in range(4):
        by_name = {n: res[kind] for n, res in big.items()}
        by_name.update(zip(REPLICATED, rep[kind]))
        results.extend(by_name[n] for n in WEIGHT_NAMES)
    return (loss, grad_x[None], *results)
```

```python
import functools
import math

import jax
import jax.numpy as jnp
from jax import lax
from jax.experimental import pallas as pl
from jax.experimental.pallas import tpu as pltpu

F32 = jnp.float32
BF16 = jnp.bfloat16

N_DEV = 8
D_MODEL = 1024
NORM_EPS = 1e-6
SSM_GROUP = 16
SSM_GROUPS = 64
SSM_STATE = 64
GROUPS_PER_BLOCK = 8
N_BLOCKS = SSM_GROUPS // GROUPS_PER_BLOCK
HALF = GROUPS_PER_BLOCK * SSM_STATE
HEAD_DIM = 64
N_Q_HEADS = 16
N_KV_HEADS = 4
KV_REP = N_Q_HEADS // N_KV_HEADS
KV_WIDTH = N_KV_HEADS * HEAD_DIM
GRID_W = 64
ROPE_THETA = 10000.0
ADAM_LR, ADAM_B1, ADAM_B2, ADAM_EPS, ADAM_WD, ADAM_STEP = 0.001, 0.9, 0.999, 1e-08, 0.01, 10

ROW_TILE = 256
SCAN_CHUNK = 256
VMEM_LIMIT = 56 * 1024 * 1024
MESH_IDS = pl.DeviceIdType.MESH


def _cparams(n_axes):
    return pltpu.CompilerParams(dimension_semantics=("arbitrary",) * n_axes, vmem_limit_bytes=VMEM_LIMIT)


def _dot(a, b):
    return jnp.dot(a.astype(BF16), b.astype(BF16), preferred_element_type=F32)


def _dot_t0(a, b):
    return lax.dot_general(a.astype(BF16), b.astype(BF16), (((0,), (0,)), ((), ())), preferred_element_type=F32)


def _dot_t1(a, b):
    return lax.dot_general(a.astype(BF16), b.astype(BF16), (((1,), (1,)), ((), ())), preferred_element_type=F32)


def _s5_prep(a_re, a_im, log_dt, b_re, b_im):
    dt = jnp.exp(log_dt)[:, None]
    ldr, ldi = a_re * dt, a_im * dt
    mag = jnp.exp(ldr)
    abar_re, abar_im = mag * jnp.cos(ldi), mag * jnp.sin(ldi)
    den = a_re * a_re + a_im * a_im
    num_re, num_im = abar_re - 1.0, abar_im
    coef_re = (num_re * a_re + num_im * a_im) / den
    coef_im = (num_im * a_re - num_re * a_im) / den
    bbar_re = coef_re[..., None] * b_re - coef_im[..., None] * b_im
    bbar_im = coef_re[..., None] * b_im + coef_im[..., None] * b_re
    return abar_re, abar_im, bbar_re, bbar_im


def _s5_blocks(abar_re, abar_im, bbar_re, bbar_im, c_re, c_im):
    eye = jnp.eye(GROUPS_PER_BLOCK, dtype=F32)
    bb = jnp.stack([bbar_re, bbar_im]).reshape(2, N_BLOCKS, GROUPS_PER_BLOCK, SSM_STATE, SSM_GROUP)
    b_blk = jnp.einsum('rqgph,gk->qghrkp', bb, eye).reshape(N_BLOCKS, 128, 2 * HALF)
    cc = jnp.stack([c_re, -c_im]).reshape(2, N_BLOCKS, GROUPS_PER_BLOCK, SSM_GROUP, SSM_STATE)
    c_blk = jnp.einsum('rqghp,gk->qrgpkh', cc, eye).reshape(N_BLOCKS, 2 * HALF, 128)
    abar = jnp.stack([abar_re.reshape(N_BLOCKS, HALF), abar_im.reshape(N_BLOCKS, HALF)])
    return abar, b_blk, c_blk


def _s5_unblock(d_b_blk, d_ct_blk):
    db = d_b_blk.reshape(N_BLOCKS, GROUPS_PER_BLOCK, SSM_GROUP, 2, GROUPS_PER_BLOCK, SSM_STATE)
    db = jnp.einsum('qghrgp->rqgph', db).reshape(2, SSM_GROUPS, SSM_STATE, SSM_GROUP)
    dc = d_ct_blk.reshape(N_BLOCKS, GROUPS_PER_BLOCK, SSM_GROUP, 2, GROUPS_PER_BLOCK, SSM_STATE)
    dc = jnp.einsum('qghrgp->rqghp', dc).reshape(2, SSM_GROUPS, SSM_GROUP, SSM_STATE)
    return db[0], db[1], dc[0], -dc[1]


def _scan_chunk_of_step(j, n_chunks, n_ctx_chunks, reverse):
    if not reverse:
        return j
    return jnp.where(j < n_ctx_chunks, n_ctx_chunks - 1 - j, n_chunks - 1 - j + n_ctx_chunks)


LANE_TILES = 2 * HALF // 128
RE_TILES = HALF // 128


def _tiles(v):
    return [v[:, l * 128:(l + 1) * 128] for l in range(v.shape[1] // 128)]


def _scatter_steps(s_ref, q, x):
    for l in range(LANE_TILES):
        s_ref[l, pl.ds(q, x.shape[0], stride=N_BLOCKS), :] = x[:, l * 128:(l + 1) * 128]


def _gather_steps(s_ref, q, n_steps):
    return jnp.concatenate([s_ref[l, pl.ds(q, n_steps, stride=N_BLOCKS), :] for l in range(LANE_TILES)], axis=1)


def _load_step(s_ref, t):
    row = pl.multiple_of(t * N_BLOCKS, N_BLOCKS)
    return [s_ref[l, pl.ds(row, N_BLOCKS), :] for l in range(LANE_TILES)]


def _store_step(s_ref, t, tiles):
    row = pl.multiple_of(t * N_BLOCKS, N_BLOCKS)
    for l in range(LANE_TILES):
        s_ref[l, pl.ds(row, N_BLOCKS), :] = tiles[l]


SCAN_UNROLL = 8
ADJOINT_UNROLL = 15


def _unrolled_loop(n_steps, unroll, step, carry):
    assert n_steps % unroll == 0

    def steps(i, c):
        for r in range(unroll):
            c = step(unroll * i + r, c)
        return c

    return lax.fori_loop(0, n_steps // unroll, steps, carry)


def _cmul_add(a, h, x, conj):
    re, im = [], []
    for l in range(RE_TILES):
        ar, ai, hr, hi = a[l], a[RE_TILES + l], h[l], h[RE_TILES + l]
        if conj:
            re.append(ar * hr + ai * hi + x[l])
            im.append(ar * hi - ai * hr + x[RE_TILES + l])
        else:
            re.append(ar * hr - ai * hi + x[l])
            im.append(ar * hi + ai * hr + x[RE_TILES + l])
    return re + im


def _s5_scan_fwd(u, abar, b_blk, c_blk, d_skip, n_ctx, reverse, rider=None):
    n_rows, width = u.shape
    tc = SCAN_CHUNK
    n_chunks, n_ctx_chunks = n_rows // tc, n_ctx // tc
    with_skip = d_skip is not None

    def body(*refs):
        if rider is not None:
            x_ref, ride_out, sems = refs[4 + with_skip], refs[7 + with_skip], refs[-3:]
            _ride(rider, pl.program_id(0) == 0, pl.program_id(0) == n_chunks - 1, (x_ref, ride_out) + tuple(sems))
            refs = refs[:4 + with_skip] + refs[5 + with_skip:7 + with_skip] + refs[8 + with_skip:-3]
        if with_skip:
            u_ref, a_ref, b_ref, c_ref, d_ref, y_ref, hb_ref, s_ref, h_ref = refs
        else:
            u_ref, a_ref, b_ref, c_ref, y_ref, hb_ref, s_ref, h_ref = refs
        j = pl.program_id(0)

        @pl.when(j == 0)
        def _():
            h_ref[...] = jnp.zeros_like(h_ref)

        hb_ref[0] = h_ref[...]
        for q in range(N_BLOCKS):
            _scatter_steps(s_ref, q, _dot(u_ref[:, q * 128:(q + 1) * 128], b_ref[q]))
        a = _tiles(a_ref[0]) + _tiles(a_ref[1])

        def step(s, h):
            t = tc - 1 - s if reverse else s
            h = _cmul_add(a, h, _load_step(s_ref, t), conj=False)
            _store_step(s_ref, t, h)
            return h

        h = _unrolled_loop(tc, SCAN_UNROLL, step, _tiles(h_ref[...]))
        h_ref[...] = jnp.concatenate(h, axis=1)
        for q in range(N_BLOCKS):
            yq = _dot(_gather_steps(s_ref, q, tc), c_ref[q])
            if with_skip:
                yq = yq + d_ref[:, q * 128:(q + 1) * 128] * u_ref[:, q * 128:(q + 1) * 128]
            y_ref[:, q * 128:(q + 1) * 128] = yq

    chunk = functools.partial(_scan_chunk_of_step, n_chunks=n_chunks, n_ctx_chunks=n_ctx_chunks, reverse=reverse)
    full3 = lambda j: (0, 0, 0)
    in_specs = [pl.BlockSpec((tc, width), lambda j: (chunk(j), 0)),
                pl.BlockSpec((2, N_BLOCKS, HALF), full3),
                pl.BlockSpec((N_BLOCKS, 128, 2 * HALF), full3),
                pl.BlockSpec((N_BLOCKS, 2 * HALF, 128), full3)]
    args = [u, abar, b_blk.astype(BF16), c_blk.astype(BF16)]
    if with_skip:
        in_specs.append(pl.BlockSpec((1, width), lambda j: (0, 0)))
        args.append(d_skip.reshape(1, width))
    out_specs = [pl.BlockSpec((tc, width), lambda j: (chunk(j), 0)),
                 pl.BlockSpec((1, N_BLOCKS, 2 * HALF), lambda j: (chunk(j), 0, 0))]
    out_shape = [_sds((n_rows, width)), _sds((n_chunks, N_BLOCKS, 2 * HALF))]
    scratch = [pltpu.VMEM((LANE_TILES, tc * N_BLOCKS, 128), F32), pltpu.VMEM((N_BLOCKS, 2 * HALF), F32)]
    if rider is not None:
        in_specs.append(HBM_SPEC)
        args.append(rider[0])
        out_specs.append(HBM_SPEC)
        out_shape.append(_exchange_out_shape(*rider))
        scratch += _exchange_semaphores()
    return pl.pallas_call(
        body, name="s5_scan_fwd_rev" if reverse else "s5_scan_fwd",
        grid=(n_chunks,), in_specs=in_specs, out_specs=out_specs, out_shape=out_shape, scratch_shapes=scratch,
        compiler_params=_cparams(1),
    )(*args)


def _s5_scan_bwd(u, dy, hb, abar, b_blk, c_blk, d_skip, n_ctx, reverse, rider=None):
    n_rows, width = u.shape
    tc = SCAN_CHUNK
    n_chunks, n_ctx_chunks = n_rows // tc, n_ctx // tc
    with_skip = d_skip is not None
    n_in, n_out = 7 + with_skip, 4 + with_skip

    def body(*refs):
        if rider is not None:
            x_ref, ride_out, sems = refs[n_in], refs[n_in + 1 + n_out], refs[-3:]
            _ride(rider, pl.program_id(0) == 0, pl.program_id(0) == n_chunks - 1, (x_ref, ride_out) + tuple(sems))
            refs = refs[:n_in] + refs[n_in + 1:n_in + 1 + n_out] + refs[n_in + 2 + n_out:-3]
        if with_skip:
            (u_ref, dy_ref, hb_ref, a_ref, b_ref, bt_ref, ct_ref, d_ref,
             du_ref, da_ref, db_ref, dct_ref, dd_ref, sh_ref, sg_ref, g_ref) = refs
        else:
            (u_ref, dy_ref, hb_ref, a_ref, b_ref, bt_ref, ct_ref,
             du_ref, da_ref, db_ref, dct_ref, sh_ref, sg_ref, g_ref) = refs
        j = pl.program_id(0)

        @pl.when(j == 0)
        def _():
            g_ref[...] = jnp.zeros_like(g_ref)
            da_ref[...] = jnp.zeros_like(da_ref)
            db_ref[...] = jnp.zeros_like(db_ref)
            dct_ref[...] = jnp.zeros_like(dct_ref)
            if with_skip:
                dd_ref[...] = jnp.zeros_like(dd_ref)

        for q in range(N_BLOCKS):
            _scatter_steps(sh_ref, q, _dot(u_ref[:, q * 128:(q + 1) * 128], b_ref[q]))
            _scatter_steps(sg_ref, q, _dot(dy_ref[:, q * 128:(q + 1) * 128], ct_ref[q]))
        a = _tiles(a_ref[0]) + _tiles(a_ref[1])
        time_of = (lambda s: tc - 1 - s) if reverse else (lambda s: s)

        def fwd_step(s, h):
            h = _cmul_add(a, h, _load_step(sh_ref, time_of(s)), conj=False)
            _store_step(sh_ref, time_of(s), h)
            return h

        h0 = _tiles(hb_ref[0])
        _unrolled_loop(tc, SCAN_UNROLL, fwd_step, h0)

        def adj(t, h_prev, carry):
            g, da = carry
            g = _cmul_add(a, g, _load_step(sg_ref, t), conj=True)
            _store_step(sg_ref, t, g)
            da_re = [da[l] + g[l] * h_prev[l] + g[RE_TILES + l] * h_prev[RE_TILES + l] for l in range(RE_TILES)]
            da_im = [da[RE_TILES + l] + g[RE_TILES + l] * h_prev[l] - g[l] * h_prev[RE_TILES + l] for l in range(RE_TILES)]
            return g, da_re + da_im

        def bwd_step(i, carry):
            s = tc - 1 - i
            return adj(time_of(s), _load_step(sh_ref, time_of(s - 1)), carry)

        carry = (_tiles(g_ref[...]), _tiles(da_ref[0]) + _tiles(da_ref[1]))
        carry = _unrolled_loop(tc - 1, ADJOINT_UNROLL, bwd_step, carry)
        g, da = adj(time_of(0), h0, carry)
        g_ref[...] = jnp.concatenate(g, axis=1)
        da_ref[0] = jnp.concatenate(da[:RE_TILES], axis=1)
        da_ref[1] = jnp.concatenate(da[RE_TILES:], axis=1)

        for q in range(N_BLOCKS):
            cols = slice(q * 128, (q + 1) * 128)
            uq, dyq = u_ref[:, cols], dy_ref[:, cols]
            gq = _gather_steps(sg_ref, q, tc)
            duq = _dot(gq, bt_ref[q])
            if with_skip:
                duq = duq + d_ref[:, cols] * dyq
                dd_ref[:, cols] += jnp.sum(dyq * uq, axis=0, keepdims=True)
            du_ref[:, cols] = duq
            db_ref[q] += _dot_t0(uq, gq)
            dct_ref[q] += _dot_t0(dyq, _gather_steps(sh_ref, q, tc))

    def chunk(j):
        return _scan_chunk_of_step(n_chunks - 1 - j, n_chunks, n_ctx_chunks, reverse)

    full2 = lambda j: (0, 0)
    full3 = lambda j: (0, 0, 0)
    row = pl.BlockSpec((tc, width), lambda j: (chunk(j), 0))
    in_specs = [row, row,
                pl.BlockSpec((1, N_BLOCKS, 2 * HALF), lambda j: (chunk(j), 0, 0)),
                pl.BlockSpec((2, N_BLOCKS, HALF), full3),
                pl.BlockSpec((N_BLOCKS, 128, 2 * HALF), full3),
                pl.BlockSpec((N_BLOCKS, 2 * HALF, 128), full3),
                pl.BlockSpec((N_BLOCKS, 128, 2 * HALF), full3)]
    args = [u, dy, hb, abar, b_blk.astype(BF16), jnp.swapaxes(b_blk, 1, 2).astype(BF16),
            jnp.swapaxes(c_blk, 1, 2).astype(BF16)]
    out_specs = [row,
                 pl.BlockSpec((2, N_BLOCKS, HALF), full3),
                 pl.BlockSpec((N_BLOCKS, 128, 2 * HALF), full3),
                 pl.BlockSpec((N_BLOCKS, 128, 2 * HALF), full3)]
    out_shape = [jax.ShapeDtypeStruct((n_rows, width), F32),
                 jax.ShapeDtypeStruct((2, N_BLOCKS, HALF), F32),
                 jax.ShapeDtypeStruct((N_BLOCKS, 128, 2 * HALF), F32),
                 jax.ShapeDtypeStruct((N_BLOCKS, 128, 2 * HALF), F32)]
    if with_skip:
        in_specs.append(pl.BlockSpec((1, width), full2))
        args.append(d_skip.reshape(1, width))
        out_specs.append(pl.BlockSpec((1, width), full2))
        out_shape.append(jax.ShapeDtypeStruct((1, width), F32))
    scratch = [pltpu.VMEM((LANE_TILES, tc * N_BLOCKS, 128), F32), pltpu.VMEM((LANE_TILES, tc * N_BLOCKS, 128), F32),
               pltpu.VMEM((N_BLOCKS, 2 * HALF), F32)]
    if rider is not None:
        in_specs.append(HBM_SPEC)
        args.append(rider[0])
        out_specs.append(HBM_SPEC)
        out_shape.append(_exchange_out_shape(*rider))
        scratch += _exchange_semaphores()
    return pl.pallas_call(
        body, name="s5_scan_bwd_rev" if reverse else "s5_scan_bwd",
        grid=(n_chunks,), in_specs=in_specs, out_specs=out_specs, out_shape=out_shape, scratch_shapes=scratch,
        compiler_params=_cparams(1),
    )(*args)


def _s5_dir_params(d, a_re, a_im, log_dt, b_re, b_im):
    return a_re[d], a_im[d], log_dt[d], b_re[d], b_im[d]


def _s5_forward(u, ssm, n_ctx, riders=(None, None)):
    a_re, a_im, log_dt, b_re, b_im, c_re, c_im, d_skip = ssm
    outs, saved, carried = [], [], [None, None]
    for d in range(2):
        prep = _s5_prep(*_s5_dir_params(d, a_re, a_im, log_dt, b_re, b_im))
        abar, b_blk, c_blk = _s5_blocks(*prep, c_re[d], c_im[d])
        res = _s5_scan_fwd(u, abar, b_blk, c_blk, d_skip if d == 0 else None, n_ctx, reverse=(d == 1), rider=riders[d])
        if riders[d] is not None:
            carried[d] = res[2]
        outs.append(res[0])
        saved.append((res[1], abar, b_blk, c_blk))
    return outs, saved, carried


def _s5_backward(u, dy, ssm, saved, n_ctx, riders=(None, None)):
    a_re, a_im, log_dt, b_re, b_im, c_re, c_im, d_skip = ssm
    dus, grads = [], [[] for _ in range(7)]
    d_d, carried = None, [None, None]
    for d in range(2):
        hb, abar, b_blk, c_blk = saved[d]
        res = _s5_scan_bwd(u, dy, hb, abar, b_blk, c_blk, d_skip if d == 0 else None, n_ctx, reverse=(d == 1),
                           rider=riders[d])
        if riders[d] is not None:
            carried[d], res = res[-1], res[:-1]
        if d == 0:
            du, d_abar, d_b_blk, d_ct_blk, d_d = res
        else:
            du, d_abar, d_b_blk, d_ct_blk = res
        dus.append(du)
        dbb_re, dbb_im, dc_re, dc_im = _s5_unblock(d_b_blk, d_ct_blk)
        _, vjp = jax.vjp(_s5_prep, *_s5_dir_params(d, a_re, a_im, log_dt, b_re, b_im))
        shape = (SSM_GROUPS, SSM_STATE)
        g5 = vjp((d_abar[0].reshape(shape), d_abar[1].reshape(shape), dbb_re, dbb_im))
        for k, g in enumerate(tuple(g5) + (dc_re, dc_im)):
            grads[k].append(g)
    grads = [jnp.stack(g) for g in grads]
    return dus, grads + [d_d.reshape(-1)], carried


INV_SQRT2 = 0.7071067811865476
INV_SQRT_2PI = 0.3989422804014327


def _rows(cols):
    return pl.BlockSpec((ROW_TILE, cols), lambda i: (i, 0))


def _rows_skip_ctx(cols):
    return pl.BlockSpec((ROW_TILE, cols), lambda i: (i + 1, 0))


def _rows_lat(cols):
    return pl.BlockSpec((ROW_TILE, cols), lambda i: (jnp.maximum(i - 1, 0), 0))


def _full(shape):
    nd = len(shape)
    return pl.BlockSpec(shape, lambda i: (0,) * nd)


def _seg(cols):
    return pl.BlockSpec((1, 1, cols), lambda i: (jnp.minimum(i, 1), 0, 0))


def _lat_seg(cols):
    return pl.BlockSpec((1, 1, cols), lambda i: (1, 0, 0))


def _sds(shape, dtype=F32):
    return jax.ShapeDtypeStruct(shape, dtype)


def _sum0(x):
    return jnp.sum(x, axis=0, keepdims=True)


def _sigmoid(x):
    return jax.nn.sigmoid(x)


def _rms_mod(x, g, scale, shift):
    r = lax.rsqrt(jnp.mean(x * x, axis=-1, keepdims=True) + NORM_EPS)
    return (x * r * g) * (1.0 + scale) + shift


def _rms_mod_bwd(x, g, scale, dh):
    r = lax.rsqrt(jnp.mean(x * x, axis=-1, keepdims=True) + NORM_EPS)
    n = x * r
    dyg = dh * (1.0 + scale)
    dn = dyg * g
    dx = r * (dn - n * jnp.mean(dn * n, axis=-1, keepdims=True))
    return dx, _sum0(dyg * n), _sum0(dh * (n * g)), _sum0(dh)


def _head_of_lane(width):
    return (jnp.arange(width)[:, None] // HEAD_DIM == jnp.arange(128)[None, :]).astype(BF16)


def _split_dot(t, w, transposed):
    hi = t.astype(BF16)
    lo = (t - hi.astype(F32)).astype(BF16)
    f = _dot_t1 if transposed else _dot
    return f(hi, w) + f(lo, w)


def _head_sums(t, hl):
    return _split_dot(_split_dot(t, hl, False), hl, True)


def _rope_partner(x):
    n = x.shape[1]
    lane = lax.broadcasted_iota(jnp.int32, x.shape, 1)
    return jnp.where((lane & 16) == 0, pltpu.roll(x, n - 16, 1), pltpu.roll(x, 16, 1))


def _lanes(tab, width):
    return jnp.tile(tab, (1, width // tab.shape[1]))


def _head_norm_rope(x, gain, cos, sin, hl):
    r = lax.rsqrt(_head_sums(x * x, hl) * (1.0 / HEAD_DIM) + NORM_EPS)
    y = x * r * gain
    return y * cos + _rope_partner(y) * sin


def _head_norm_rope_bwd(x, gain, cos, sin, hl, dout):
    dy = dout * cos + _rope_partner(dout * sin)
    r = lax.rsqrt(_head_sums(x * x, hl) * (1.0 / HEAD_DIM) + NORM_EPS)
    n = x * r
    dn = dy * gain
    dx = r * (dn - n * (_head_sums(dn * n, hl) * (1.0 / HEAD_DIM)))
    return dx, _sum0(dy * n)


def _rope_tables(n_ctx, n_lat):
    t = jnp.arange(n_lat)
    pos = jnp.stack([(t // GRID_W).astype(F32), (t % GRID_W).astype(F32)], axis=1)
    n_freq = HEAD_DIM // 4
    freqs = ROPE_THETA ** (-jnp.arange(n_freq, dtype=F32) / n_freq)
    ang = pos[:, :, None] * freqs[None, None, :]
    cos = jnp.repeat(jnp.cos(ang)[:, :, None, :], 2, axis=2).reshape(n_lat, HEAD_DIM)
    sin = jnp.sin(ang)
    sin = jnp.stack([-sin, sin], axis=2).reshape(n_lat, HEAD_DIM)
    cos = jnp.concatenate([jnp.ones((n_ctx, HEAD_DIM), F32), cos], axis=0)
    sin = jnp.concatenate([jnp.zeros((n_ctx, HEAD_DIM), F32), sin], axis=0)
    return jnp.tile(cos, (1, 2)), jnp.tile(sin, (1, 2))


def _token_tile(ctx_ref, x_ref):
    return jnp.where(pl.program_id(0) == 0, ctx_ref[...], x_ref[...])


def _ssm_in(ctx, x, g, scale, shift, w_in):
    n_rows, d = ctx.shape[0] + x.shape[0], x.shape[1]
    e = w_in.shape[1] // 2

    def body(c_ref, x_ref, g_ref, sc_ref, sh_ref, w_ref, u_ref, z_ref):
        h = _rms_mod(_token_tile(c_ref, x_ref), g_ref[...], sc_ref[0], sh_ref[0])
        proj = _dot(h, w_ref[...])
        u_ref[...] = proj[:, :e]
        z_ref[...] = proj[:, e:]

    return pl.pallas_call(
        body, name="ssm_in", grid=(n_rows // ROW_TILE,),
        in_specs=[_full(ctx.shape), _rows_lat(d), _full((1, d)), _seg(d), _seg(d), _full(w_in.shape)],
        out_specs=[_rows(e), _rows(e)], out_shape=[_sds((n_rows, e)), _sds((n_rows, e))],
        compiler_params=_cparams(1),
    )(ctx, x, g, scale, shift, w_in)


def _s5_post_math(y, z, w_glu, b_glu, w_out):
    er = lax.erf(y * INV_SQRT2)
    g = 0.5 * y * (1.0 + er)
    sg = _sigmoid(_dot(g, w_glu) + b_glu)
    g2 = g * sg
    sz = _sigmoid(z)
    silu_z = z * sz
    m = g2 * silu_z
    return er, g, sg, g2, sz, silu_z, m, _dot(m, w_out)


def _ssm_post(ctx, x, y0, y1, z, gate, w_glu, b_glu, w_out):
    n_rows, d = ctx.shape[0] + x.shape[0], x.shape[1]
    e = z.shape[1]

    def body(c_ref, x_ref, y0_ref, y1_ref, z_ref, gt_ref, wg_ref, bg_ref, wo_ref, o_ref):
        out = _s5_post_math(y0_ref[...] + y1_ref[...], z_ref[...], wg_ref[...], bg_ref[...], wo_ref[...])[-1]
        o_ref[...] = _token_tile(c_ref, x_ref) + gt_ref[0] * out

    return pl.pallas_call(
        body, name="ssm_post", grid=(n_rows // ROW_TILE,),
        in_specs=[_full(ctx.shape), _rows_lat(d), _rows(e), _rows(e), _rows(e), _seg(d), _full(w_glu.shape), _full((1, e)),
                  _full(w_out.shape)],
        out_specs=_rows(d), out_shape=_sds((n_rows, d)),
        compiler_params=_cparams(1),
    )(ctx, x, y0, y1, z, gate, w_glu, b_glu, w_out)


def _init_acc(first, *refs):
    @pl.when(first)
    def _():
        for r in refs:
            r[...] = jnp.zeros_like(r)


def _ssm_post_bwd(dxa, y0, y1, z, gate, w_glu, b_glu, w_out):
    n_rows, d = dxa.shape
    e = z.shape[1]

    def body(dx_ref, y0_ref, y1_ref, z_ref, gt_ref, wg_ref, bg_ref, wo_ref,
             dy_ref, dz_ref, dgt_ref, dwo_ref, dwg_ref, dbg_ref):
        i = pl.program_id(0)
        _init_acc(i == 0, dwo_ref, dwg_ref, dbg_ref)
        _init_acc(i <= 1, dgt_ref)
        y, zz = y0_ref[...] + y1_ref[...], z_ref[...]
        er, g, sg, g2, sz, silu_z, m, out = _s5_post_math(y, zz, wg_ref[...], bg_ref[...], wo_ref[...])
        dxa_t = dx_ref[...]
        dgt_ref[0] += _sum0(dxa_t * out)
        dout = gt_ref[0] * dxa_t
        dm = _dot_t1(dout, wo_ref[...])
        dwo_ref[...] += _dot_t0(m, dout)
        dg2 = dm * silu_z
        dz_ref[...] = dm * g2 * (sz * (1.0 + zz * (1.0 - sz)))
        dt = dg2 * g * sg * (1.0 - sg)
        dwg_ref[...] += _dot_t0(g, dt)
        dbg_ref[...] += _sum0(dt)
        dg = dg2 * sg + _dot_t1(dt, wg_ref[...])
        dy_ref[...] = dg * (0.5 * (1.0 + er) + y * jnp.exp(-0.5 * y * y) * INV_SQRT_2PI)

    return pl.pallas_call(
        body, name="ssm_post_bwd", grid=(n_rows // ROW_TILE,),
        in_specs=[_rows(d), _rows(e), _rows(e), _rows(e), _seg(d), _full(w_glu.shape), _full((1, e)), _full(w_out.shape)],
        out_specs=[_rows(e), _rows(e), _seg(d), _full(w_out.shape), _full(w_glu.shape), _full((1, e))],
        out_shape=[_sds((n_rows, e)), _sds((n_rows, e)), _sds((2, 1, d)), _sds(w_out.shape), _sds(w_glu.shape), _sds((1, e))],
        compiler_params=_cparams(1),
    )(dxa, y0, y1, z, gate, w_glu, b_glu, w_out)


def _ssm_in_bwd(du0, du1, dz, ctx, x_lat, dxa_next, g, scale, shift, w_in):
    n_lat, d = x_lat.shape
    n_rows = ctx.shape[0] + n_lat
    e = dz.shape[1]

    def body(du0_ref, du1_ref, dz_ref, c_ref, x_ref, dn_ref, g_ref, sc_ref, sh_ref, w_ref,
             gx_ref, dw_ref, dg_ref, dsc_ref, dsh_ref):
        i = pl.program_id(0)
        _init_acc(i == 0, dw_ref, dg_ref)
        _init_acc(i <= 1, dsc_ref, dsh_ref)
        x = _token_tile(c_ref, x_ref)
        h = _rms_mod(x, g_ref[...], sc_ref[0], sh_ref[0])
        dproj = jnp.concatenate([du0_ref[...] + du1_ref[...], dz_ref[...]], axis=1)
        dh = _dot_t1(dproj, w_ref[...])
        dw_ref[...] += _dot_t0(h, dproj)
        dx, dg, dsc, dsh = _rms_mod_bwd(x, g_ref[...], sc_ref[0], dh)
        dg_ref[...] += dg
        dsc_ref[0] += dsc
        dsh_ref[0] += dsh
        gx_ref[...] = dn_ref[...] + dx

    return pl.pallas_call(
        body, name="ssm_in_bwd", grid=(n_rows // ROW_TILE,),
        in_specs=[_rows(e), _rows(e), _rows(e), _full(ctx.shape), _rows_lat(d), _rows(d), _full((1, d)), _seg(d), _seg(d),
                  _full(w_in.shape)],
        out_specs=[_rows_lat(d), _full((d, 2 * e)), _full((1, d)), _seg(d), _seg(d)],
        out_shape=[_sds((n_lat, d)), _sds((d, 2 * e)), _sds((1, d)), _sds((2, 1, d)), _sds((2, 1, d))],
        compiler_params=_cparams(1),
    )(du0, du1, dz, ctx, x_lat, dxa_next, g, scale, shift, w_in)


Q_WIDTH = N_Q_HEADS * HEAD_DIM
SM_SCALE = 1.0 / math.sqrt(HEAD_DIM)


def _attn_in(xa, g, scale, shift, w_in, q_gain, k_gain, cos, sin):
    n_rows, d = xa.shape
    qk = Q_WIDTH + KV_WIDTH

    def body(x_ref, g_ref, sc_ref, sh_ref, w_ref, qg_ref, kg_ref, cos_ref, sin_ref, hq_ref, hk_ref,
             q_ref, k_ref, v_ref, z_ref, raw_ref):
        h = _rms_mod(x_ref[...], g_ref[...], sc_ref[0], sh_ref[0])
        proj = _dot(h, w_ref[...])
        q_raw, k_raw = proj[:, :Q_WIDTH], proj[:, Q_WIDTH:qk]
        cos, sin = cos_ref[...], sin_ref[...]
        q = _head_norm_rope(q_raw, qg_ref[...], _lanes(cos, Q_WIDTH), _lanes(sin, Q_WIDTH), hq_ref[...])
        k = _head_norm_rope(k_raw, kg_ref[...], _lanes(cos, KV_WIDTH), _lanes(sin, KV_WIDTH), hk_ref[...])
        q_ref[...] = (q * SM_SCALE).astype(BF16)
        k_ref[...] = k.astype(BF16)
        v_ref[...] = proj[:, qk:qk + KV_WIDTH].astype(BF16)
        z_ref[...] = proj[:, qk + KV_WIDTH:]
        raw_ref[...] = proj[:, :qk]

    return pl.pallas_call(
        body, name="attn_in", grid=(n_rows // ROW_TILE,),
        in_specs=[_rows(d), _full((1, d)), _seg(d), _seg(d), _full(w_in.shape), _full((1, Q_WIDTH)), _full((1, KV_WIDTH)),
                  _rows(128), _rows(128), _full((Q_WIDTH, 128)), _full((KV_WIDTH, 128))],
        out_specs=[_rows_lat(Q_WIDTH), _rows(KV_WIDTH), _rows(KV_WIDTH), _rows(Q_WIDTH), _rows(qk)],
        out_shape=[_sds((n_rows - ROW_TILE, Q_WIDTH), BF16), _sds((n_rows, KV_WIDTH), BF16), _sds((n_rows, KV_WIDTH), BF16),
                   _sds((n_rows, Q_WIDTH)), _sds((n_rows, qk))],
        compiler_params=_cparams(1),
    )(xa, g, scale, shift, w_in, q_gain, k_gain, cos, sin, _head_of_lane(Q_WIDTH), _head_of_lane(KV_WIDTH))


GROUP_WIDTH = KV_REP * HEAD_DIM


def _stack_heads(ref):
    return jnp.concatenate([ref[:, h * HEAD_DIM:(h + 1) * HEAD_DIM] for h in range(KV_REP)], axis=0)


def _unstack_heads(a_t, tq):
    return jnp.concatenate([a_t[:, h * tq:(h + 1) * tq].T for h in range(KV_REP)], axis=1)


def _kv_tile(n_keys):
    return 768 if n_keys % 768 == 0 else 256


def _kv_tile_fwd(n_keys):
    return 1408 if n_keys % 1408 == 0 else _kv_tile(n_keys)


def _q_tile(n_lat):
    return 512 if n_lat % 512 == 0 else 256


def _flash_fwd(q, k, v_t):
    n_lat = q.shape[0]
    tq = _q_tile(n_lat)
    rows = KV_REP * tq
    n_kv, tk, n_q = k.shape[1], k.shape[2], n_lat // tq

    v_rows = v_t.shape[2]

    def body(q_ref, k_ref, vt_ref, o_ref, lse_ref):
        q = _stack_heads(q_ref)

        def step(j, carry):
            m_prev, acc = carry
            s_t = _dot_t1(k_ref[0, j], q)
            m_new = jnp.maximum(m_prev, jnp.max(s_t, axis=0, keepdims=True))
            alpha = jnp.exp(m_prev - m_new)
            p_t = jnp.exp(s_t - m_new)
            return m_new, alpha * acc + _dot(vt_ref[0, j], p_t)

        init = (jnp.full((1, rows), -jnp.inf, F32), jnp.zeros((v_rows, rows), F32))
        m, acc = lax.fori_loop(0, n_kv, step, init)
        l = acc[HEAD_DIM:HEAD_DIM + 1]
        o_ref[...] = _unstack_heads(acc[:HEAD_DIM] / l, tq)
        lse_ref[0, 0] = m + jnp.log(l)

    kv_all = lambda a: pl.BlockSpec((1,) + a.shape[1:], lambda g, i: (g, 0, 0, 0))
    return pl.pallas_call(
        body, name="flash_fwd", grid=(N_KV_HEADS, n_q),
        in_specs=[pl.BlockSpec((tq, GROUP_WIDTH), lambda g, i: (i, g)), kv_all(k), kv_all(v_t)],
        out_specs=[pl.BlockSpec((tq, GROUP_WIDTH), lambda g, i: (i, g)),
                   pl.BlockSpec((1, 1, 1, rows), lambda g, i: (g, i, 0, 0))],
        out_shape=[_sds((n_lat, Q_WIDTH)), _sds((N_KV_HEADS, n_q, 1, rows))],
        compiler_params=_cparams(2),
    )(q, k, v_t)


def _flash_bwd(q, k, k_t, v, do, lse_t, delta_t):
    n_lat = q.shape[0]
    tq = _q_tile(n_lat)
    rows = KV_REP * tq
    n_kv, tk, n_q = k.shape[1], k.shape[2], n_lat // tq

    def body(q_ref, k_ref, kt_ref, v_ref, do_ref, lse_ref, dl_ref, dq_ref, dk_ref, dv_ref):
        _init_acc(pl.program_id(1) == 0, dk_ref, dv_ref)
        q, do = _stack_heads(q_ref), _stack_heads(do_ref)
        lse, delta = lse_ref[0, 0], dl_ref[0, 0]

        def step(j, dq_acc):
            p_t = jnp.exp(_dot_t1(k_ref[0, j], q) - lse)
            dv_ref[0, j] += _dot(p_t, do)
            ds_t = p_t * (_dot_t1(v_ref[0, j], do) - delta)
            dk_ref[0, j] += _dot(ds_t, q)
            return dq_acc + _dot(kt_ref[0, j], ds_t)

        dq = lax.fori_loop(0, n_kv, step, jnp.zeros((HEAD_DIM, rows), F32))
        dq_ref[...] = _unstack_heads(dq, tq)

    qspec = pl.BlockSpec((tq, GROUP_WIDTH), lambda g, i: (i, g))
    rowspec = pl.BlockSpec((1, 1, 1, rows), lambda g, i: (g, i, 0, 0))
    kv_all = lambda a: pl.BlockSpec((1,) + a.shape[1:], lambda g, i: (g, 0, 0, 0))
    return pl.pallas_call(
        body, name="flash_bwd", grid=(N_KV_HEADS, n_q),
        in_specs=[qspec, kv_all(k), kv_all(k_t), kv_all(v), qspec, rowspec, rowspec],
        out_specs=[qspec, kv_all(k), kv_all(k)],
        out_shape=[_sds((n_lat, Q_WIDTH)), _sds(k.shape), _sds(k.shape)],
        compiler_params=_cparams(2),
    )(q, k, k_t, v, do, lse_t, delta_t)


def _to_lane_stacked(a, tq):
    n_lat = a.shape[0]
    a = a.reshape(n_lat // tq, tq, N_KV_HEADS, KV_REP).transpose(2, 0, 3, 1)
    return a.reshape(N_KV_HEADS, n_lat // tq, 1, KV_REP * tq)


def _attn_post_loss(o, z, xa, gate, w_out, final_g, target):
    n_lat, d = target.shape
    e = o.shape[1]
    head_of_lane = (jnp.arange(e)[:, None] // HEAD_DIM == jnp.arange(128)[None, :]).astype(BF16)

    def body(o_ref, z_ref, x_ref, gt_ref, w_ref, fg_ref, tg_ref, hl_ref,
             do_ref, dl_ref, dz_ref, dx_ref, loss_ref, dfg_ref, dgt_ref, dw_ref):
        _init_acc(pl.program_id(0) == 0, loss_ref, dfg_ref, dgt_ref, dw_ref)
        oo, zz, gate_t, fg = o_ref[...], z_ref[...], gt_ref[0], fg_ref[...]
        sz = _sigmoid(zz)
        silu_z = zz * sz
        m = oo * silu_z
        out = _dot(m, w_ref[...])
        x2 = x_ref[...] + gate_t * out
        r = lax.rsqrt(jnp.mean(x2 * x2, axis=-1, keepdims=True) + NORM_EPS)
        n = x2 * r
        err = n * fg - tg_ref[...]
        loss_ref[...] += 0.5 * jnp.sum(jnp.mean(err * err, axis=-1, keepdims=True), axis=0, keepdims=True)
        dy = err * (1.0 / d)
        dfg_ref[...] += _sum0(dy * n)
        dn = dy * fg
        dx2 = r * (dn - n * jnp.mean(dn * n, axis=-1, keepdims=True))
        dx_ref[...] = dx2
        dgt_ref[...] += _sum0(dx2 * out)
        dout = gate_t * dx2
        dw_ref[...] += _dot_t0(m, dout)
        dm = _dot_t1(dout, w_ref[...])
        do = dm * silu_z
        do_ref[...] = do.astype(BF16)
        prod = do * oo
        hi = prod.astype(BF16)
        lo = (prod - hi.astype(F32)).astype(BF16)
        dl_ref[...] = _dot(hi, hl_ref[...]) + _dot(lo, hl_ref[...])
        dz_ref[...] = dm * oo * (sz * (1.0 + zz * (1.0 - sz)))

    return pl.pallas_call(
        body, name="attn_post_loss", grid=(n_lat // ROW_TILE,),
        in_specs=[_rows(e), _rows_skip_ctx(e), _rows_skip_ctx(d), _lat_seg(d), _full(w_out.shape),
                  _full((1, d)), _rows(d), _full((e, 128))],
        out_specs=[_rows(e), _rows(128), _rows(e), _rows(d), _full((1, 1)), _full((1, d)), _full((1, d)), _full(w_out.shape)],
        out_shape=[_sds((n_lat, e), BF16), _sds((n_lat, 128)), _sds((n_lat, e)), _sds((n_lat, d)), _sds((1, 1)), _sds((1, d)),
                   _sds((1, d)), _sds(w_out.shape)],
        compiler_params=_cparams(1),
    )(o, z, xa, gate, w_out, final_g, target, head_of_lane)


def _attn_in_bwd(dq, dk, dv, dz, raw, xa, dx2, g, scale, shift, q_gain, k_gain, cos, sin, w_in):
    n_rows, d = xa.shape
    qk = Q_WIDTH + KV_WIDTH
    n_in = w_in.shape[1]

    def body(dq_ref, dk_ref, dv_ref, dz_ref, raw_ref, x_ref, dx2_ref, g_ref, sc_ref, sh_ref, qg_ref, kg_ref, cos_ref, sin_ref,
             w_ref, hq_ref, hk_ref, dxa_ref, dw_ref, dqg_ref, dkg_ref, dg_ref, dsc_ref, dsh_ref):
        i = pl.program_id(0)
        _init_acc(i == 0, dw_ref, dqg_ref, dkg_ref, dg_ref)
        _init_acc(i <= 1, dsc_ref, dsh_ref)
        is_lat = (i > 0).astype(F32)
        x = x_ref[...]
        h = _rms_mod(x, g_ref[...], sc_ref[0], sh_ref[0])
        cos, sin = cos_ref[...], sin_ref[...]
        raw_t = raw_ref[...]
        dq_raw, dqg = _head_norm_rope_bwd(raw_t[:, :Q_WIDTH], qg_ref[...], _lanes(cos, Q_WIDTH), _lanes(sin, Q_WIDTH),
                                          hq_ref[...], dq_ref[...] * (SM_SCALE * is_lat))
        dk_raw, dkg = _head_norm_rope_bwd(raw_t[:, Q_WIDTH:], kg_ref[...], _lanes(cos, KV_WIDTH), _lanes(sin, KV_WIDTH),
                                          hk_ref[...], dk_ref[...])
        dqg_ref[...] += dqg
        dkg_ref[...] += dkg
        dproj = jnp.concatenate([dq_raw, dk_raw, dv_ref[...], dz_ref[...] * is_lat], axis=1)
        dh = _dot_t1(dproj, w_ref[...])
        dw_ref[...] += _dot_t0(h, dproj)
        dx, dg, dsc, dsh = _rms_mod_bwd(x, g_ref[...], sc_ref[0], dh)
        dg_ref[...] += dg
        dsc_ref[0] += dsc
        dsh_ref[0] += dsh
        dxa_ref[...] = dx + dx2_ref[...] * is_lat

    return pl.pallas_call(
        body, name="attn_in_bwd", grid=(n_rows // ROW_TILE,),
        in_specs=[_rows_lat(Q_WIDTH), _rows(KV_WIDTH), _rows(KV_WIDTH), _rows_lat(Q_WIDTH), _rows(qk), _rows(d), _rows_lat(d),
                  _full((1, d)), _seg(d), _seg(d), _full((1, Q_WIDTH)), _full((1, KV_WIDTH)), _rows(128), _rows(128),
                  _full(w_in.shape), _full((Q_WIDTH, 128)), _full((KV_WIDTH, 128))],
        out_specs=[_rows(d), _full((d, n_in)), _full((1, Q_WIDTH)), _full((1, KV_WIDTH)), _full((1, d)), _seg(d), _seg(d)],
        out_shape=[_sds((n_rows, d)), _sds((d, n_in)), _sds((1, Q_WIDTH)), _sds((1, KV_WIDTH)), _sds((1, d)),
                   _sds((2, 1, d)), _sds((2, 1, d))],
        compiler_params=_cparams(1),
    )(dq, dk, dv, dz, raw, xa, dx2, g, scale, shift, q_gain, k_gain, cos, sin, w_in,
      _head_of_lane(Q_WIDTH), _head_of_lane(KV_WIDTH))


def _heads_major(a, n_heads):
    return a.reshape(a.shape[0], n_heads, HEAD_DIM).transpose(1, 0, 2)


def _tokens_major(a):
    return a.transpose(1, 0, 2).reshape(a.shape[1], a.shape[0] * HEAD_DIM)


def _local_step(x, ctx, target, mods, norm_g, ssm, w_ssm_in, w_glu, b_glu, w_ssm_out, w_attn_in, q_norm, k_norm, w_attn_out,
                final_g, attn_exchange=None, post_exchange=None):
    n_ctx, d = ctx.shape
    assert n_ctx == ROW_TILE
    n_lat = x.shape[0]
    (shift0, scale0, gate0), (shift1, scale1, gate1) = mods
    g0, g1, fg = norm_g[0:1], norm_g[1:2], final_g.reshape(1, d)
    b_glu = b_glu.reshape(1, -1)
    q_gain = jnp.tile(q_norm.reshape(1, HEAD_DIM), (1, N_Q_HEADS))
    k_gain = jnp.tile(k_norm.reshape(1, HEAD_DIM), (1, N_KV_HEADS))
    cos, sin = _rope_tables(n_ctx, n_lat)

    u, z0 = _ssm_in(ctx, x, g0, scale0, shift0, w_ssm_in)
    gather = lambda ex: (ex[0], False) if ex else None
    (y0, y1), saved, gathered = _s5_forward(u, ssm, n_ctx, (gather(attn_exchange), gather(post_exchange)))
    if attn_exchange:
        w_attn_in, w_attn_out = attn_exchange[1](gathered[0])
    if post_exchange:
        w_glu, w_ssm_out = post_exchange[1](gathered[1])
    xa1 = _ssm_post(ctx, x, y0, y1, z0, gate0, w_glu, b_glu, w_ssm_out)

    q, k, v, z1, raw = _attn_in(xa1, g1, scale1, shift1, w_attn_in, q_gain, k_gain, cos, sin)
    tq, tk, tk_fwd = _q_tile(n_lat), _kv_tile(n_ctx + n_lat), _kv_tile_fwd(n_ctx + n_lat)
    k_h, v_h = _heads_major(k, N_KV_HEADS), _heads_major(v, N_KV_HEADS)
    k_b, v_b = k_h.reshape(N_KV_HEADS, -1, tk, HEAD_DIM), v_h.reshape(N_KV_HEADS, -1, tk, HEAD_DIM)
    v_t = v_h.reshape(N_KV_HEADS, -1, tk_fwd, HEAD_DIM).transpose(0, 1, 3, 2)
    v_t_ones = jnp.concatenate([v_t, jnp.ones(v_t.shape[:2] + (16, tk_fwd), BF16)], axis=2)
    o, lse_t = _flash_fwd(q, k_h.reshape(N_KV_HEADS, -1, tk_fwd, HEAD_DIM), v_t_ones)
    do, delta, dz1, dx2, loss, d_fg, d_gate1, d_w_attn_out = _attn_post_loss(
        o, z1, xa1, gate1, w_attn_out, fg, target)

    dq, dk_b, dv_b = _flash_bwd(q, k_b, k_b.transpose(0, 1, 3, 2), v_b, do, lse_t, _to_lane_stacked(delta[:, :N_Q_HEADS], tq))
    keys_major = lambda a: _tokens_major(a.reshape(N_KV_HEADS, -1, HEAD_DIM))
    dxa1, d_w_attn_in, d_qg, d_kg, d_g1, d_scale1, d_shift1 = _attn_in_bwd(
        dq, keys_major(dk_b), keys_major(dv_b), dz1, raw, xa1, dx2, g1, scale1, shift1,
        q_gain, k_gain, cos, sin, w_attn_in)
    dy, dz0, d_gate0, d_w_ssm_out, d_w_glu, d_b_glu = _ssm_post_bwd(
        dxa1, y0, y1, z0, gate0, w_glu, b_glu, w_ssm_out)
    scatter = lambda ex, *g: (ex[2](*g), True) if ex else None
    (du0, du1), d_ssm, parts = _s5_backward(
        u, dy, ssm, saved, n_ctx,
        (scatter(attn_exchange, d_w_attn_in, d_w_attn_out), scatter(post_exchange, d_w_glu, d_w_ssm_out)))
    grad_x, d_w_ssm_in, d_g0, d_scale0, d_shift0 = _ssm_in_bwd(du0, du1, dz0, ctx, x, dxa1, g0, scale0, shift0, w_ssm_in)

    d_gate1_seg = jnp.concatenate([jnp.zeros((1, 1, d), F32), d_gate1.reshape(1, 1, d)], axis=0)
    grads = dict(
        norm_g=jnp.concatenate([d_g0, d_g1], axis=0), ssm_w_in=d_w_ssm_in, ssm=d_ssm, ssm_b_glu=d_b_glu.reshape(-1),
        attn_q_norm=d_qg.reshape(N_Q_HEADS, HEAD_DIM).sum(0), attn_k_norm=d_kg.reshape(N_KV_HEADS, HEAD_DIM).sum(0),
        final_norm_g=d_fg.reshape(-1))
    if attn_exchange:
        grads.update(attn_parts=parts[0])
    else:
        grads.update(attn_w_in=d_w_attn_in, attn_w_out=d_w_attn_out)
    if post_exchange:
        grads.update(post_parts=parts[1])
    else:
        grads.update(ssm_w_glu=d_w_glu, ssm_w_out=d_w_ssm_out)
    d_mods = ((d_shift0, d_scale0, d_gate0), (d_shift1, d_scale1, d_gate1_seg))
    return loss[0, 0], grad_x, grads, d_mods


def _my_index():
    return 4 * lax.axis_index("x") + 2 * lax.axis_index("y") + lax.axis_index("c")


def _peer(k):
    mx, my, mc = lax.axis_index("x"), lax.axis_index("y"), lax.axis_index("c")
    px = 1 - mx if k & 4 else mx
    py = 1 - my if k & 2 else my
    pc = 1 - mc if k & 1 else mc
    return (px, py, pc), 4 * px + 2 * py + pc


HBM_SPEC = pl.BlockSpec(memory_space=pl.ANY)


def _exchange(x, name, all_to_all):
    def body(x_ref, out_ref, send_sems, recv_sems, local_sem):
        _exchange_copies(all_to_all, x_ref, out_ref, send_sems, recv_sems, local_sem, start=True)
        _exchange_copies(all_to_all, x_ref, out_ref, send_sems, recv_sems, local_sem, start=False)

    return pl.pallas_call(
        body, name=name, in_specs=[HBM_SPEC], out_specs=HBM_SPEC,
        out_shape=_exchange_out_shape(x, all_to_all), scratch_shapes=_exchange_semaphores(),
    )(x)


def _all_gather_two_level(x, name, also=None):
    def body(x_ref, out_ref, send_sems, recv_sems, local_sem):
        mx, my, mc = lax.axis_index("x"), lax.axis_index("y"), lax.axis_index("c")
        me, sibling = (mx, my, mc), (mx, my, 1 - mc)
        chips = [(1 - mx, my), (mx, 1 - my), (1 - mx, 1 - my)]

        def block(px, py, pc):
            return out_ref.at[4 * px + 2 * py + pc]

        def copy(k, owner, to, src=None):
            return pltpu.make_async_remote_copy(
                src_ref=block(*owner) if src is None else src, dst_ref=block(*owner),
                send_sem=send_sems.at[k], recv_sem=recv_sems.at[k], device_id=to, device_id_type=MESH_IDS)

        mine = pltpu.make_async_copy(x_ref, block(*me), local_sem)
        mine.start()
        first = [copy(0, me, sibling, src=x_ref)] + [copy(1 + j, me, (*chip, mc), src=x_ref) for j, chip in enumerate(chips)]
        for cp in first:
            cp.start()
        passed = [copy(4 + j, (*chip, mc), sibling) for j, chip in enumerate(chips)]
        for j, chip in enumerate(chips):
            copy(1 + j, (*chip, mc), me).wait_recv()
            passed[j].start()
        copy(0, sibling, me).wait_recv()
        for j, chip in enumerate(chips):
            copy(4 + j, (*chip, 1 - mc), me).wait_recv()
        for cp in first + passed:
            cp.wait_send()
        mine.wait()

    if also is None:
        return pl.pallas_call(
            body, name=name, in_specs=[HBM_SPEC], out_specs=HBM_SPEC,
            out_shape=_exchange_out_shape(x, False), scratch_shapes=_exchange_semaphores(),
        )(x)

    def body_with_direct(x_ref, a_ref, out_ref, a_out_ref, send_sems, recv_sems, local_sem, *direct_sems):
        _exchange_copies(False, a_ref, a_out_ref, *direct_sems, start=True)
        body(x_ref, out_ref, send_sems, recv_sems, local_sem)
        _exchange_copies(False, a_ref, a_out_ref, *direct_sems, start=False)

    return pl.pallas_call(
        body_with_direct, name=name, in_specs=[HBM_SPEC, HBM_SPEC], out_specs=[HBM_SPEC, HBM_SPEC],
        out_shape=[_exchange_out_shape(x, False), _exchange_out_shape(also, False)],
        scratch_shapes=_exchange_semaphores() + _exchange_semaphores(),
    )(x, also)


def _exchange_pair(xa, xa_all_to_all, xb, xb_all_to_all, name):
    def body(xa_ref, xb_ref, oa_ref, ob_ref, sa, ra, la, sb, rb, lb):
        for start in (True, False):
            _exchange_copies(xa_all_to_all, xa_ref, oa_ref, sa, ra, la, start=start)
            _exchange_copies(xb_all_to_all, xb_ref, ob_ref, sb, rb, lb, start=start)

    return pl.pallas_call(
        body, name=name, in_specs=[HBM_SPEC, HBM_SPEC], out_specs=[HBM_SPEC, HBM_SPEC],
        out_shape=[_exchange_out_shape(xa, xa_all_to_all), _exchange_out_shape(xb, xb_all_to_all)],
        scratch_shapes=_exchange_semaphores() + _exchange_semaphores(),
    )(xa, xb)


def _exchange_out_shape(x, all_to_all):
    return _sds((N_DEV,) + tuple(x.shape[1:] if all_to_all else x.shape), x.dtype)


def _exchange_semaphores():
    return [pltpu.SemaphoreType.DMA((N_DEV - 1,)), pltpu.SemaphoreType.DMA((N_DEV - 1,)), pltpu.SemaphoreType.DMA]


def _exchange_copies(all_to_all, x_ref, out_ref, send_sems, recv_sems, local_sem, start):
    me = _my_index()
    mine = pltpu.make_async_copy(x_ref.at[me] if all_to_all else x_ref, out_ref.at[me], local_sem)
    if start:
        mine.start()
    for k in range(1, N_DEV):
        peer, peer_idx = _peer(k)
        send = pltpu.make_async_remote_copy(
            src_ref=x_ref.at[peer_idx] if all_to_all else x_ref, dst_ref=out_ref.at[me],
            send_sem=send_sems.at[k - 1], recv_sem=recv_sems.at[k - 1], device_id=peer, device_id_type=MESH_IDS)
        if start:
            send.start()
        else:
            pltpu.make_async_remote_copy(
                src_ref=x_ref.at[me] if all_to_all else x_ref, dst_ref=out_ref.at[peer_idx],
                send_sem=send_sems.at[k - 1], recv_sem=recv_sems.at[k - 1], device_id=peer,
                device_id_type=MESH_IDS).wait_recv()
            send.wait_send()
    if not start:
        mine.wait()


def _ride(rider, first, last, refs):
    @pl.when(first)
    def _():
        _exchange_copies(rider[1], *refs, start=True)

    @pl.when(last)
    def _():
        _exchange_copies(rider[1], *refs, start=False)


MOD_ROWS = 16
CTX_ROW = N_DEV


def _mod_fwd(cond, w_shard, b_cols):
    n_layers, d, cols = w_shard.shape

    def body(c_ref, w_ref, b_ref, o_ref):
        c = c_ref[...]
        s = c * _sigmoid(c)
        for i in range(n_layers):
            o_ref[i] = _dot(s, w_ref[i]) + b_ref[i]

    return pl.pallas_call(
        body, name="mod_fwd", out_shape=_sds((n_layers, MOD_ROWS, cols)),
        compiler_params=pltpu.CompilerParams(vmem_limit_bytes=VMEM_LIMIT),
    )(cond, w_shard, b_cols.reshape(n_layers, 1, cols))


def _mod_bwd(cond, d_lat_cols, d_ctx_cols, w_shard):
    n_layers, d, cols = w_shard.shape

    def body(c_ref, dl_ref, dc_ref, w_ref, dw_ref, dcc_ref):
        c = c_ref[...]
        sg = _sigmoid(c)
        s = c * sg
        d_s = jnp.zeros((MOD_ROWS, d), F32)
        for i in range(n_layers):
            d_ctx = dc_ref[0, i]
            for j in range(1, N_DEV):
                d_ctx = d_ctx + dc_ref[j, i]
            dm = jnp.concatenate([dl_ref[i], d_ctx, jnp.zeros((MOD_ROWS - N_DEV - 1, cols), F32)], axis=0)
            dw_ref[i] = _dot_t0(s, dm)
            d_s = d_s + _dot_t1(dm, w_ref[i])
        d_c = d_s * (sg * (1.0 + c * (1.0 - sg)))
        dcc_ref[...] = d_c[CTX_ROW:CTX_ROW + 1]

    return pl.pallas_call(
        body, name="mod_bwd", out_shape=[_sds((n_layers, d, cols)), _sds((1, d))],
        compiler_params=pltpu.CompilerParams(vmem_limit_bytes=VMEM_LIMIT),
    )(cond, d_lat_cols, d_ctx_cols, w_shard)


ADAM_TILE = 512


def _adamw(w, g_parts, m, v, name):
    n_parts, n_rows, lanes = g_parts.shape
    tile = min(ADAM_TILE, n_rows)
    assert n_rows % tile == 0
    c1 = 1.0 - ADAM_B1 ** ADAM_STEP
    c2 = 1.0 - ADAM_B2 ** ADAM_STEP

    def body(w_ref, g_ref, m_ref, v_ref, go_ref, d_ref, mo_ref, vo_ref):
        g = g_ref[0].astype(F32)
        for p in range(1, n_parts):
            g = g + g_ref[p].astype(F32)
        m_new = ADAM_B1 * m_ref[...] + (1.0 - ADAM_B1) * g
        v_new = ADAM_B2 * v_ref[...] + (1.0 - ADAM_B2) * (g * g)
        go_ref[...] = g
        mo_ref[...] = m_new
        vo_ref[...] = v_new
        d_ref[...] = -ADAM_LR * ((m_new / c1) / (jnp.sqrt(v_new / c2) + ADAM_EPS) + ADAM_WD * w_ref[...])

    row = pl.BlockSpec((tile, lanes), lambda i: (i, 0))
    return pl.pallas_call(
        body, name=name, grid=(n_rows // tile,),
        in_specs=[row, pl.BlockSpec((n_parts, tile, lanes), lambda i: (0, i, 0)), row, row],
        out_specs=[row] * 4, out_shape=[_sds((n_rows, lanes))] * 4,
        compiler_params=_cparams(1),
    )(w, g_parts, m, v)


def _sum_parts(parts):
    n_parts, n_rows, lanes = parts.shape

    def body(p_ref, o_ref):
        acc = p_ref[0]
        for p in range(1, n_parts):
            acc = acc + p_ref[p]
        o_ref[...] = acc

    return pl.pallas_call(body, name="sum_parts", out_shape=_sds((n_rows, lanes)))(parts)


def _pack(arrays, row_multiple):
    parts = []
    for a in arrays:
        flat = a.reshape(-1)
        parts.append(jnp.pad(flat, (0, (-flat.shape[0]) % 1024)))
    flat = jnp.concatenate(parts)
    flat = jnp.pad(flat, (0, (-flat.shape[0]) % (row_multiple * 128)))
    return flat.reshape(-1, 128)


def _unpack(packed, shapes):
    flat = packed.reshape(-1)
    out, pos = [], 0
    for s in shapes:
        n = math.prod(s)
        out.append(flat[pos:pos + n].reshape(s))
        pos += n + (-n) % 1024
    return out


WEIGHT_NAMES = ['c_ctx', 'w_mod', 'b_mod', 'norm_g', 'ssm_w_in', 'ssm_a_re', 'ssm_a_im', 'ssm_log_dt', 'ssm_b_re', 'ssm_b_im',
                'ssm_c_re', 'ssm_c_im', 'ssm_d', 'ssm_w_glu', 'ssm_b_glu', 'ssm_w_out', 'attn_w_in', 'attn_q_norm',
                'attn_k_norm', 'attn_w_out', 'final_norm_g']
FIRST_SHARDED = ['ssm_w_in']
POST_SHARDED = ['ssm_w_glu', 'ssm_w_out']
ATTN_SHARDED = ['attn_w_in', 'attn_w_out']
SHARDED = FIRST_SHARDED + POST_SHARDED + ATTN_SHARDED
COLUMN_SHARDED = ('ssm_w_in', 'attn_w_in')
REPLICATED = ['c_ctx', 'b_mod', 'norm_g', 'ssm_a_re', 'ssm_a_im', 'ssm_log_dt', 'ssm_b_re', 'ssm_b_im', 'ssm_c_re', 'ssm_c_im',
              'ssm_d', 'ssm_b_glu', 'attn_q_norm', 'attn_k_norm', 'final_norm_g']
SSM_NAMES = ['ssm_a_re', 'ssm_a_im', 'ssm_log_dt', 'ssm_b_re', 'ssm_b_im', 'ssm_c_re', 'ssm_c_im', 'ssm_d']


def _full_from_shards(gathered, name, shard_shape):
    rows, cols = shard_shape
    w = gathered.reshape(N_DEV, rows, cols)
    if name in COLUMN_SHARDED:
        return w.transpose(1, 0, 2).reshape(rows, N_DEV * cols)
    return w.reshape(N_DEV * rows, cols)


def _shards_from_full(g, name):
    if name in COLUMN_SHARDED:
        rows, cols = g.shape
        g = g.reshape(rows, N_DEV, cols // N_DEV).transpose(1, 0, 2)
    return g.reshape(N_DEV, -1, 128)


def kernel(x, c, ctx, c_ctx, w_mod, b_mod, norm_g, ssm_w_in, ssm_a_re, ssm_a_im, ssm_log_dt, ssm_b_re, ssm_b_im, ssm_c_re, ssm_c_im, ssm_d, ssm_w_glu, ssm_b_glu, ssm_w_out, attn_w_in, attn_q_norm, attn_k_norm, attn_w_out, final_norm_g, loss_target, m_c_ctx, m_w_mod, m_b_mod, m_norm_g, m_ssm_w_in, m_ssm_a_re, m_ssm_a_im, m_ssm_log_dt, m_ssm_b_re, m_ssm_b_im, m_ssm_c_re, m_ssm_c_im, m_ssm_d, m_ssm_w_glu, m_ssm_b_glu, m_ssm_w_out, m_attn_w_in, m_attn_q_norm, m_attn_k_norm, m_attn_w_out, m_final_norm_g, v_c_ctx, v_w_mod, v_b_mod, v_norm_g, v_ssm_w_in, v_ssm_a_re, v_ssm_a_im, v_ssm_log_dt, v_ssm_b_re, v_ssm_b_im, v_ssm_c_re, v_ssm_c_im, v_ssm_d, v_ssm_w_glu, v_ssm_b_glu, v_ssm_w_out, v_attn_w_in, v_attn_q_norm, v_attn_k_norm, v_attn_w_out, v_final_norm_g):
    env = dict(locals())
    weights = {n: env[n] for n in WEIGHT_NAMES}
    mom_m = {n: env["m_" + n] for n in WEIGHT_NAMES}
    mom_v = {n: env["v_" + n] for n in WEIGHT_NAMES}
    d = D_MODEL
    me = _my_index()
    mod_cols = w_mod.shape[-1]

    shard_shapes = {n: weights[n].shape[1:] for n in SHARDED}
    pack_shards = lambda names: _pack([weights[n] for n in names], 1).astype(BF16)

    def unpack_full(gathered, names):
        full, pos = [], 0
        for n in names:
            rows = math.prod(shard_shapes[n]) // 128
            full.append(_full_from_shards(gathered[:, pos:pos + rows], n, shard_shapes[n]))
            pos += rows
        return full

    def exchange_of(names):
        return (pack_shards(names), lambda gathered: unpack_full(gathered, names),
                lambda *grads: jnp.concatenate([_shards_from_full(t, n) for t, n in zip(grads, names)], axis=1).astype(BF16))

    w_first, c_all = _all_gather_two_level(pack_shards(FIRST_SHARDED), "gather_ssm_w_in_and_c", also=c.reshape(8, d // 8))
    cond = jnp.concatenate([c_all.reshape(N_DEV, d), c_ctx.reshape(1, d), jnp.zeros((MOD_ROWS - N_DEV - 1, d), F32)], axis=0)
    (w_ssm_in,) = unpack_full(w_first, FIRST_SHARDED)

    b_cols = lax.dynamic_slice(b_mod, (0, me * mod_cols), (2, mod_cols))
    mod_shard = _mod_fwd(cond, w_mod, b_cols)
    mod_all = _exchange(mod_shard.reshape(2 * MOD_ROWS, mod_cols), "gather_mod", False)
    mod_full = mod_all.reshape(N_DEV, 2, MOD_ROWS, mod_cols).transpose(1, 2, 0, 3).reshape(2, MOD_ROWS, 3 * d)
    lat_rows = lax.dynamic_slice(mod_full, (0, me, 0), (2, 1, 3 * d))
    mods = []
    for i in range(2):
        seg = jnp.stack([mod_full[i, CTX_ROW:CTX_ROW + 1], lat_rows[i]])
        mods.append((seg[:, :, :d], seg[:, :, d:2 * d], seg[:, :, 2 * d:]))

    ssm = tuple(weights[n][0] for n in SSM_NAMES)
    loss, grad_x, g, d_mods = _local_step(
        x[0], ctx[0], loss_target[0], mods, norm_g, ssm, w_ssm_in, None, ssm_b_glu[0], None,
        None, attn_q_norm[0], attn_k_norm[0], None, final_norm_g, exchange_of(ATTN_SHARDED), exchange_of(POST_SHARDED))

    d_rows = jnp.stack([jnp.concatenate(dm, axis=-1) for dm in d_mods])
    d_rows = jnp.concatenate([d_rows.reshape(4, 3 * d), jnp.zeros((4, 3 * d), F32)], axis=0)
    d_all, first_parts = _exchange_pair(d_rows, False, _shards_from_full(g['ssm_w_in'], 'ssm_w_in').astype(BF16), True,
                                        "gather_dmod_and_scatter_ssm_w_in_grads")
    d_all = d_all[:, :4].reshape(N_DEV, 2, 2, 3 * d)
    d_all = lax.dynamic_slice(d_all, (0, 0, 0, me * mod_cols), (N_DEV, 2, 2, mod_cols))
    d_w_mod, d_c_ctx = _mod_bwd(cond, d_all[:, :, 1].transpose(1, 0, 2), d_all[:, :, 0:1], w_mod)
    d_b_mod = jnp.stack([jnp.concatenate([t[0] + t[1] for t in dm], axis=-1).reshape(3 * d) for dm in d_mods])

    parts_of = {}
    for names, parts in ((FIRST_SHARDED, first_parts), (POST_SHARDED, g['post_parts']), (ATTN_SHARDED, g['attn_parts'])):
        pos = 0
        for n in names:
            rows = math.prod(shard_shapes[n]) // 128
            parts_of[n] = parts[:, pos:pos + rows].reshape((N_DEV,) + shard_shapes[n])
            pos += rows

    def update(n, g_parts):
        as_2d = lambda t: t.reshape(-1, t.shape[-1])
        res = _adamw(as_2d(weights[n]), g_parts, as_2d(mom_m[n]), as_2d(mom_v[n]), "adamw_" + n)
        return [t.reshape(weights[n].shape) for t in res]

    big = {n: update(n, parts_of[n]) for n in SHARDED}
    big['w_mod'] = update('w_mod', d_w_mod.reshape(1, -1, mod_cols))

    small = dict(zip(SSM_NAMES, g['ssm']))
    small.update(c_ctx=d_c_ctx, b_mod=d_b_mod, norm_g=g['norm_g'], ssm_b_glu=g['ssm_b_glu'], attn_q_norm=g['attn_q_norm'],
                 attn_k_norm=g['attn_k_norm'], final_norm_g=g['final_norm_g'])
    pack_small = lambda t, last: _pack([t[n] for n in REPLICATED] + [last], ADAM_TILE)
    no_weight = jnp.zeros((1,), F32)
    g_small = pack_small(small, loss.reshape(1))
    slices = _exchange(g_small.reshape(N_DEV, -1, 128), "scatter_small_grads", True)
    g_small = _exchange(_sum_parts(slices), "gather_small_grads", False).reshape(1, -1, 128)
    rep = _adamw(pack_small(weights, no_weight), g_small, pack_small(mom_m, no_weight), pack_small(mom_v, no_weight),
                 "adamw_replicated")
    rep = [_unpack(t, [weights[n].shape for n in REPLICATED] + [(1,)]) for t in rep]
    loss = rep[0][-1][0]

    results = []
    for kind in range(4):
        by_name = {n: res[kind] for n, res in big.items()}
        by_name.update(zip(REPLICATED, rep[kind]))
        results.extend(by_name[n] for n in WEIGHT_NAMES)
    return (loss, grad_x[None], *results)
```
